```python
import jax, jax.numpy as jnp
from jax import lax
import numpy as np

D_MODEL = 1024
BATCH = 8
SEQ = 4096
DEPTH = 4

CHUNK = 64
N_PREV_CHUNKS = 8
BAND = N_PREV_CHUNKS + 1

D_MIX = D_MODEL
HEAD_DIM = 64
ATTN_WIDTH = D_MIX // 2
N_ATTN_HEADS = ATTN_WIDTH // HEAD_DIM
CONV_WIDTH = D_MIX // 4
CONV_K = 3
POOL_WIDTH = D_MIX - ATTN_WIDTH - CONV_WIDTH
POOL_WINDOWS = (2, 4, 8, 16)
N_POOL_GROUPS = len(POOL_WINDOWS)
POOL_GROUP = POOL_WIDTH // N_POOL_GROUPS
REL_CLIP = 128

D_IN = 3 * ATTN_WIDTH + 3 * CONV_WIDTH + POOL_WIDTH
D_FF = 4 * D_MODEL
EPS = 1e-6
NEG_INF = -1e30

kernel_name = "hybrid_chunked_attn_conv_pool_trunk"


def rms_norm(x, g):
    x32 = x.astype(jnp.float32)
    y = x32 * lax.rsqrt(jnp.mean(x32 * x32, axis=-1, keepdims=True) + EPS)
    return (y * g.astype(jnp.float32)).astype(x.dtype)


def chunked_band_attention(q, k, v, rel_bias):
    b, s, h, d = q.shape
    nc = s // CHUNK
    qc = q.reshape(b, nc, CHUNK, h, d)
    pad = ((0, 0), (N_PREV_CHUNKS, 0), (0, 0), (0, 0), (0, 0))
    kp = jnp.pad(k.reshape(b, nc, CHUNK, h, d), pad)
    vp = jnp.pad(v.reshape(b, nc, CHUNK, h, d), pad)
    band_idx = jnp.arange(nc)[:, None] + jnp.arange(BAND)[None, :]
    kb = kp[:, band_idx].reshape(b, nc, BAND * CHUNK, h, d)
    vb = vp[:, band_idx].reshape(b, nc, BAND * CHUNK, h, d)
    scores = jnp.einsum('bnqhd,bnkhd->bnhqk', qc, kb).astype(jnp.float32) * (d ** -0.5)
    qi = jnp.arange(CHUNK)[:, None]
    kj = jnp.arange(BAND * CHUNK)[None, :]
    dist = qi + N_PREV_CHUNKS * CHUNK - kj
    bias = rel_bias[:, jnp.clip(dist, -REL_CLIP, REL_CLIP) + REL_CLIP].astype(jnp.float32)
    valid = jnp.repeat(band_idx >= N_PREV_CHUNKS, CHUNK, axis=1)
    scores = jnp.where(valid[None, :, None, None, :], scores + bias[None, None], NEG_INF)
    p = jax.nn.softmax(scores, axis=-1).astype(v.dtype)
    out = jnp.einsum('bnhqk,bnkhd->bnqhd', p, vb)
    return out.reshape(b, s, h * d)


def gated_short_conv(gb, gc, hin, conv_w):
    z = gc * hin
    s = z.shape[1]
    zp = jnp.pad(z, ((0, 0), (CONV_K - 1, 0), (0, 0)))
    y = sum(conv_w[i] * zp[:, i:i + s] for i in range(CONV_K))
    return gb * y


def multiscale_pool(u, pool_w, pool_scale):
    b, s, c = u.shape
    u32 = u.astype(jnp.float32)
    cs = jnp.concatenate([jnp.zeros((b, 1, c), jnp.float32), jnp.cumsum(u32, axis=1)], axis=1)
    pos1 = jnp.arange(s) + 1
    outs = []
    for g, w in enumerate(POOL_WINDOWS):
        sl = slice(g * POOL_GROUP, (g + 1) * POOL_GROUP)
        csg = cs[:, :, sl]
        lag = jnp.pad(csg[:, :s + 1 - w], ((0, 0), (w - 1, 0), (0, 0)))
        cnt = jnp.minimum(pos1, w).astype(jnp.float32)[None, :, None]
        m = (csg[:, 1:] - lag) / cnt - u32[:, :, sl]
        outs.append(jnp.einsum('bsc,cd->bsd', m.astype(u.dtype), pool_w[g]))
    return jnp.concatenate(outs, axis=-1) * pool_scale


def _fwd_setup_inputs(seed: int = 0) -> dict:
    key = jax.random.key(seed)
    ks = jax.random.split(key, 13)
    f32 = jnp.float32
    L = DEPTH
    x = jax.random.normal(ks[0], (BATCH, SEQ, D_MODEL), f32)
    norm1_g = 1.0 + 0.05 * jax.random.normal(ks[1], (L, D_MODEL), f32)
    w_in = jax.random.normal(ks[2], (L, D_MODEL, D_IN), f32) * D_MODEL ** -0.5
    q_norm_g = 1.0 + 0.05 * jax.random.normal(ks[3], (L, HEAD_DIM), f32)
    k_norm_g = 1.0 + 0.05 * jax.random.normal(ks[4], (L, HEAD_DIM), f32)
    rel_bias = 0.1 * jax.random.normal(ks[5], (L, N_ATTN_HEADS, 2 * REL_CLIP + 1), f32)
    conv_w = jax.random.normal(ks[6], (L, CONV_K, CONV_WIDTH), f32) * CONV_K ** -0.5
    pool_w = jax.random.normal(ks[7], (L, N_POOL_GROUPS, POOL_GROUP, POOL_GROUP), f32) * POOL_GROUP ** -0.5
    pool_scale = 0.5 + 0.1 * jax.random.normal(ks[8], (L, POOL_WIDTH), f32)
    w_out = jax.random.normal(ks[9], (L, D_MIX, D_MODEL), f32) * D_MIX ** -0.5
    norm2_g = 1.0 + 0.05 * jax.random.normal(ks[10], (L, D_MODEL), f32)
    w_mlp1 = jax.random.normal(ks[11], (L, D_MODEL, D_FF), f32) * D_MODEL ** -0.5
    w_mlp2 = jax.random.normal(ks[12], (L, D_FF, D_MODEL), f32) * D_FF ** -0.5
    return {"x": x, "norm1_g": norm1_g, "w_in": w_in, "q_norm_g": q_norm_g,
            "k_norm_g": k_norm_g, "rel_bias": rel_bias, "conv_w": conv_w,
            "pool_w": pool_w, "pool_scale": pool_scale, "w_out": w_out,
            "norm2_g": norm2_g, "w_mlp1": w_mlp1, "w_mlp2": w_mlp2}


def _fwd_reference(x, norm1_g, w_in, q_norm_g, k_norm_g, rel_bias, conv_w, pool_w,
              pool_scale, w_out, norm2_g, w_mlp1, w_mlp2):
    b, s, _ = x.shape
    a = ATTN_WIDTH
    c = CONV_WIDTH
    for l in range(DEPTH):
        h = rms_norm(x, norm1_g[l])
        p = jnp.einsum('bsd,de->bse', h, w_in[l])
        q = p[..., 0:a].reshape(b, s, N_ATTN_HEADS, HEAD_DIM)
        k = p[..., a:2 * a].reshape(b, s, N_ATTN_HEADS, HEAD_DIM)
        v = p[..., 2 * a:3 * a].reshape(b, s, N_ATTN_HEADS, HEAD_DIM)
        o = 3 * a
        gb = p[..., o:o + c]
        gc = p[..., o + c:o + 2 * c]
        hin = p[..., o + 2 * c:o + 3 * c]
        u = p[..., o + 3 * c:]
        q = rms_norm(q, q_norm_g[l])
        k = rms_norm(k, k_norm_g[l])
        y_attn = chunked_band_attention(q, k, v, rel_bias[l])
        y_conv = gated_short_conv(gb, gc, hin, conv_w[l])
        y_pool = multiscale_pool(u, pool_w[l], pool_scale[l])
        mix = jnp.concatenate([y_attn, y_conv, y_pool], axis=-1)
        x = x + jnp.einsum('bse,ed->bsd', mix, w_out[l])
        h2 = rms_norm(x, norm2_g[l])
        f = jnp.square(jax.nn.relu(jnp.einsum('bsd,df->bsf', h2, w_mlp1[l])))
        x = x + jnp.einsum('bsf,fd->bsd', f, w_mlp2[l])
    return x


import jax as _jax
import jax.numpy as _jnp

TWIN_FORMAT = 'train_step'
FWD_PARAMS = ['x', 'norm1_g', 'w_in', 'q_norm_g', 'k_norm_g', 'rel_bias', 'conv_w', 'pool_w', 'pool_scale', 'w_out', 'norm2_g', 'w_mlp1', 'w_mlp2']
TWIN_WEIGHTS = ['norm1_g', 'w_in', 'q_norm_g', 'k_norm_g', 'rel_bias', 'conv_w', 'pool_w', 'pool_scale', 'w_out', 'norm2_g', 'w_mlp1', 'w_mlp2']
TWIN_DIFF_INPUT = 'x'
TWIN_INPUTS = ['x', 'norm1_g', 'w_in', 'q_norm_g', 'k_norm_g', 'rel_bias', 'conv_w', 'pool_w', 'pool_scale', 'w_out', 'norm2_g', 'w_mlp1', 'w_mlp2', 'loss_target', 'm_norm1_g', 'm_w_in', 'm_q_norm_g', 'm_k_norm_g', 'm_rel_bias', 'm_conv_w', 'm_pool_w', 'm_pool_scale', 'm_w_out', 'm_norm2_g', 'm_w_mlp1', 'm_w_mlp2', 'v_norm1_g', 'v_w_in', 'v_q_norm_g', 'v_k_norm_g', 'v_rel_bias', 'v_conv_w', 'v_pool_w', 'v_pool_scale', 'v_w_out', 'v_norm2_g', 'v_w_mlp1', 'v_w_mlp2']
TWIN_OUTPUTS = ['loss', 'grad_x', 'grad_norm1_g', 'grad_w_in', 'grad_q_norm_g', 'grad_k_norm_g', 'grad_rel_bias', 'grad_conv_w', 'grad_pool_w', 'grad_pool_scale', 'grad_w_out', 'grad_norm2_g', 'grad_w_mlp1', 'grad_w_mlp2', 'delta_norm1_g', 'delta_w_in', 'delta_q_norm_g', 'delta_k_norm_g', 'delta_rel_bias', 'delta_conv_w', 'delta_pool_w', 'delta_pool_scale', 'delta_w_out', 'delta_norm2_g', 'delta_w_mlp1', 'delta_w_mlp2', 'new_m_norm1_g', 'new_m_w_in', 'new_m_q_norm_g', 'new_m_k_norm_g', 'new_m_rel_bias', 'new_m_conv_w', 'new_m_pool_w', 'new_m_pool_scale', 'new_m_w_out', 'new_m_norm2_g', 'new_m_w_mlp1', 'new_m_w_mlp2', 'new_v_norm1_g', 'new_v_w_in', 'new_v_q_norm_g', 'new_v_k_norm_g', 'new_v_rel_bias', 'new_v_conv_w', 'new_v_pool_w', 'new_v_pool_scale', 'new_v_w_out', 'new_v_norm2_g', 'new_v_w_mlp1', 'new_v_w_mlp2']
TWIN_LEAF_KINDS = {'loss': 'loss', 'grad_x': 'grad_x', 'grad_norm1_g': 'grad_w', 'grad_w_in': 'grad_w', 'grad_q_norm_g': 'grad_w', 'grad_k_norm_g': 'grad_w', 'grad_rel_bias': 'grad_w', 'grad_conv_w': 'grad_w', 'grad_pool_w': 'grad_w', 'grad_pool_scale': 'grad_w', 'grad_w_out': 'grad_w', 'grad_norm2_g': 'grad_w', 'grad_w_mlp1': 'grad_w', 'grad_w_mlp2': 'grad_w', 'delta_norm1_g': 'delta_w', 'delta_w_in': 'delta_w', 'delta_q_norm_g': 'delta_w', 'delta_k_norm_g': 'delta_w', 'delta_rel_bias': 'delta_w', 'delta_conv_w': 'delta_w', 'delta_pool_w': 'delta_w', 'delta_pool_scale': 'delta_w', 'delta_w_out': 'delta_w', 'delta_norm2_g': 'delta_w', 'delta_w_mlp1': 'delta_w', 'delta_w_mlp2': 'delta_w', 'new_m_norm1_g': 'new_m', 'new_m_w_in': 'new_m', 'new_m_q_norm_g': 'new_m', 'new_m_k_norm_g': 'new_m', 'new_m_rel_bias': 'new_m', 'new_m_conv_w': 'new_m', 'new_m_pool_w': 'new_m', 'new_m_pool_scale': 'new_m', 'new_m_w_out': 'new_m', 'new_m_norm2_g': 'new_m', 'new_m_w_mlp1': 'new_m', 'new_m_w_mlp2': 'new_m', 'new_v_norm1_g': 'new_v', 'new_v_w_in': 'new_v', 'new_v_q_norm_g': 'new_v', 'new_v_k_norm_g': 'new_v', 'new_v_rel_bias': 'new_v', 'new_v_conv_w': 'new_v', 'new_v_pool_w': 'new_v', 'new_v_pool_scale': 'new_v', 'new_v_w_out': 'new_v', 'new_v_norm2_g': 'new_v', 'new_v_w_mlp1': 'new_v', 'new_v_w_mlp2': 'new_v'}


def _forward(args):
    return _fwd_reference(*[args[k] for k in FWD_PARAMS])


def _output_shape():
    out = _jax.eval_shape(lambda: _forward(_fwd_setup_inputs(0)))
    return out.shape, out.dtype

N_MICROBATCH = 1
ADAM_LR = 0.001
ADAM_B1 = 0.9
ADAM_B2 = 0.999
ADAM_EPS = 1e-08
ADAM_WD = 0.01
ADAM_STEP = 10
PER_EXAMPLE_BATCH_AXIS = {'x': 0, 'loss_target': 0}
SHARED_INPUTS = []
_WEIGHT_DTYPES = {'norm1_g': _jnp.float32, 'w_in': _jnp.float32, 'q_norm_g': _jnp.float32, 'k_norm_g': _jnp.float32, 'rel_bias': _jnp.float32, 'conv_w': _jnp.float32, 'pool_w': _jnp.float32, 'pool_scale': _jnp.float32, 'w_out': _jnp.float32, 'norm2_g': _jnp.float32, 'w_mlp1': _jnp.float32, 'w_mlp2': _jnp.float32}
MOMENT_SCALE = {'norm1_g': 2.882185e+01, 'w_in': 6.894094e+00, 'q_norm_g': 9.972599e-01, 'k_norm_g': 9.978936e-01, 'rel_bias': 1.043415e-01, 'conv_w': 1.754790e+01, 'pool_w': 1.223358e+00, 'pool_scale': 1.023570e+01, 'w_out': 1.075700e+01, 'norm2_g': 9.888200e+01, 'w_mlp1': 7.722020e+00, 'w_mlp2': 2.947624e+01}


def _to_microbatches(a, axis):
    t = _jnp.moveaxis(a, axis, 0)
    t = t.reshape((N_MICROBATCH, t.shape[0] // N_MICROBATCH) + t.shape[1:])
    return _jnp.moveaxis(t, 1, axis + 1)


def setup_inputs(seed: int = 0) -> dict:
    inp = _fwd_setup_inputs(seed)
    key = _jax.random.fold_in(_jax.random.key(seed), 7919)
    shape, _ = _output_shape()
    out = dict(inp)
    out["loss_target"] = _jax.random.normal(_jax.random.fold_in(key, 0), shape, _jnp.float32)
    for i, name in enumerate(TWIN_WEIGHTS):
        w = inp[name].astype(_jnp.float32)
        if MOMENT_SCALE is None:
            s = _jnp.sqrt(_jnp.mean(_jnp.square(w)) + 1e-30)
        else:
            s = MOMENT_SCALE[name]
        km, kv = _jax.random.split(_jax.random.fold_in(key, i + 1))
        out[name] = w
        out["m_" + name] = s * _jax.random.normal(km, w.shape, _jnp.float32)
        out["v_" + name] = (s * s) * _jax.random.uniform(kv, w.shape, _jnp.float32, 0.5, 1.5)
    if N_MICROBATCH > 1:
        for name, axis in PER_EXAMPLE_BATCH_AXIS.items():
            out[name] = _to_microbatches(out[name], axis)
    return {'x': out['x'], 'norm1_g': out['norm1_g'], 'w_in': out['w_in'], 'q_norm_g': out['q_norm_g'], 'k_norm_g': out['k_norm_g'], 'rel_bias': out['rel_bias'], 'conv_w': out['conv_w'], 'pool_w': out['pool_w'], 'pool_scale': out['pool_scale'], 'w_out': out['w_out'], 'norm2_g': out['norm2_g'], 'w_mlp1': out['w_mlp1'], 'w_mlp2': out['w_mlp2'], 'loss_target': out['loss_target'], 'm_norm1_g': out['m_norm1_g'], 'm_w_in': out['m_w_in'], 'm_q_norm_g': out['m_q_norm_g'], 'm_k_norm_g': out['m_k_norm_g'], 'm_rel_bias': out['m_rel_bias'], 'm_conv_w': out['m_conv_w'], 'm_pool_w': out['m_pool_w'], 'm_pool_scale': out['m_pool_scale'], 'm_w_out': out['m_w_out'], 'm_norm2_g': out['m_norm2_g'], 'm_w_mlp1': out['m_w_mlp1'], 'm_w_mlp2': out['m_w_mlp2'], 'v_norm1_g': out['v_norm1_g'], 'v_w_in': out['v_w_in'], 'v_q_norm_g': out['v_q_norm_g'], 'v_k_norm_g': out['v_k_norm_g'], 'v_rel_bias': out['v_rel_bias'], 'v_conv_w': out['v_conv_w'], 'v_pool_w': out['v_pool_w'], 'v_pool_scale': out['v_pool_scale'], 'v_w_out': out['v_w_out'], 'v_norm2_g': out['v_norm2_g'], 'v_w_mlp1': out['v_w_mlp1'], 'v_w_mlp2': out['v_w_mlp2']}


def _loss(weights, diff, rest, loss_target):
    with _jax.named_scope("forward"):
        args = {**rest, TWIN_DIFF_INPUT: diff, **{k: w.astype(_WEIGHT_DTYPES[k]) for k, w in weights.items()}}
        y = _forward(args)
    with _jax.named_scope("loss_head"):
        err = _jnp.square(y.astype(_jnp.float32) - loss_target)
        return 0.5 * _jnp.sum(_jnp.mean(err, axis=-1)) if err.ndim else 0.5 * err


def _adamw(w, g, m, v):
    m = ADAM_B1 * m + (1.0 - ADAM_B1) * g
    v = ADAM_B2 * v + (1.0 - ADAM_B2) * _jnp.square(g)
    m_hat = m / (1.0 - ADAM_B1 ** ADAM_STEP)
    v_hat = v / (1.0 - ADAM_B2 ** ADAM_STEP)
    delta = -ADAM_LR * (m_hat / (_jnp.sqrt(v_hat) + ADAM_EPS) + ADAM_WD * w)
    return delta, m, v


def reference(x, norm1_g, w_in, q_norm_g, k_norm_g, rel_bias, conv_w, pool_w, pool_scale, w_out, norm2_g, w_mlp1, w_mlp2, loss_target, m_norm1_g, m_w_in, m_q_norm_g, m_k_norm_g, m_rel_bias, m_conv_w, m_pool_w, m_pool_scale, m_w_out, m_norm2_g, m_w_mlp1, m_w_mlp2, v_norm1_g, v_w_in, v_q_norm_g, v_k_norm_g, v_rel_bias, v_conv_w, v_pool_w, v_pool_scale, v_w_out, v_norm2_g, v_w_mlp1, v_w_mlp2):
    given = dict(x=x, norm1_g=norm1_g, w_in=w_in, q_norm_g=q_norm_g, k_norm_g=k_norm_g, rel_bias=rel_bias, conv_w=conv_w, pool_w=pool_w, pool_scale=pool_scale, w_out=w_out, norm2_g=norm2_g, w_mlp1=w_mlp1, w_mlp2=w_mlp2, loss_target=loss_target, m_norm1_g=m_norm1_g, m_w_in=m_w_in, m_q_norm_g=m_q_norm_g, m_k_norm_g=m_k_norm_g, m_rel_bias=m_rel_bias, m_conv_w=m_conv_w, m_pool_w=m_pool_w, m_pool_scale=m_pool_scale, m_w_out=m_w_out, m_norm2_g=m_norm2_g, m_w_mlp1=m_w_mlp1, m_w_mlp2=m_w_mlp2, v_norm1_g=v_norm1_g, v_w_in=v_w_in, v_q_norm_g=v_q_norm_g, v_k_norm_g=v_k_norm_g, v_rel_bias=v_rel_bias, v_conv_w=v_conv_w, v_pool_w=v_pool_w, v_pool_scale=v_pool_scale, v_w_out=v_w_out, v_norm2_g=v_norm2_g, v_w_mlp1=v_w_mlp1, v_w_mlp2=v_w_mlp2)
    weights = {n: given[n] for n in TWIN_WEIGHTS}
    shared = {n: given[n] for n in SHARED_INPUTS}
    per_example = {n: given[n] for n in ['x']}
    grad_fn = _jax.value_and_grad(_loss, argnums=(0, 1))

    def one_microbatch(ex, loss_target):
        ex = dict(ex)
        diff = ex.pop(TWIN_DIFF_INPUT)
        return grad_fn(weights, diff, {**shared, **ex}, loss_target)

    if N_MICROBATCH == 1:
        loss, (grad_w, grad_x) = one_microbatch(per_example, given["loss_target"])
    else:
        def body(carry, xs):
            loss_sum, grad_sum = carry
            l_k, (gw_k, gx_k) = one_microbatch(xs[0], xs[1])
            with _jax.named_scope("update"):
                return (loss_sum + l_k, _jax.tree.map(_jnp.add, grad_sum, gw_k)), gx_k

        init = (_jnp.zeros((), _jnp.float32), _jax.tree.map(_jnp.zeros_like, weights))
        (loss, grad_w), grad_x = _jax.lax.scan(body, init, (per_example, given["loss_target"]))
    with _jax.named_scope("update"):
        delta_w, new_m, new_v = {}, {}, {}
        for n in TWIN_WEIGHTS:
            delta_w[n], new_m[n], new_v[n] = _adamw(weights[n], grad_w[n], given["m_" + n], given["v_" + n])
    return (loss, grad_x, *[grad_w[n] for n in TWIN_WEIGHTS], *[delta_w[n] for n in TWIN_WEIGHTS],
            *[new_m[n] for n in TWIN_WEIGHTS], *[new_v[n] for n in TWIN_WEIGHTS])
```

```python
import functools

import numpy as np
import jax
import jax.numpy as jnp
from jax import lax
from jax.experimental import pallas as pl
from jax.experimental.pallas import tpu as pltpu

F32 = jnp.float32
BF16 = jnp.bfloat16
SDS = jax.ShapeDtypeStruct
MESH_ID = pl.DeviceIdType.MESH

D = 1024
L = 4
CHUNK = 64
N_PREV = 8
HD = 64
NH = 8
AW = 512
CW = 256
PWD = 256
DIN = 2560
DFF = 4096
EPS = 1e-6
NEG = -1e30
REL_CLIP = 128
POOL_WINDOWS = (2, 4, 8, 16)
LR, B1, B2, AEPS, WD, STEP = 0.001, 0.9, 0.999, 1e-08, 0.01, 10

NDEV = 8
LANE = 128
BAND = N_PREV * CHUNK
TQ = 256
WIN = TQ + BAND
NVAR = BAND // TQ + 1
PAD = 16
VMEM_LIMIT = 56 * 1024 * 1024
SHARD_ROWS = (DIN // NDEV, D // NDEV, DFF // NDEV, DFF // NDEV)


def _cp(**kw):
    return pltpu.CompilerParams(vmem_limit_bytes=VMEM_LIMIT, **kw)


def _nn(a, b):
    return jnp.dot(a, b, preferred_element_type=F32)


def _nt(a, b):
    return lax.dot_general(a, b, (((1,), (1,)), ((), ())), preferred_element_type=F32)


def _tn(a, b):
    return lax.dot_general(a, b, (((0,), (0,)), ((), ())), preferred_element_type=F32)


def _const(shape):
    n = len(shape)
    return pl.BlockSpec(shape, lambda *_: (0,) * n, pipeline_mode=pl.Buffered(1))


def _lo_mask():
    return lax.broadcasted_iota(jnp.int32, (1, LANE), 1) < HD


def _half_sum(t, lo):
    s_lo = jnp.sum(jnp.where(lo, t, 0.0), axis=-1, keepdims=True)
    s_hi = jnp.sum(jnp.where(lo, 0.0, t), axis=-1, keepdims=True)
    return jnp.where(lo, s_lo, s_hi)


def _head_norm(x, lo):
    r = lax.rsqrt(_half_sum(x * x, lo) * (1.0 / HD) + EPS)
    return x * r, r


def _head_norm_bwd(dy, xn, r, g, lo):
    dxn = dy * g
    mu = _half_sum(dxn * xn, lo) * (1.0 / HD)
    return r * (dxn - xn * mu), dy * xn


def _rms_bwd(dy, x, g):
    r = lax.rsqrt(jnp.mean(x * x, axis=-1, keepdims=True) + EPS)
    xn = x * r
    dxn = dy * g
    mu = jnp.mean(dxn * xn, axis=-1, keepdims=True)
    return r * (dxn - xn * mu), dy * xn


def _in_proj(x, g1, win_t):
    s = x.shape[0]
    t = min(512, s)

    def body(x_ref, g_ref, w_ref, p_ref, h_ref):
        xv = x_ref[...]
        r = lax.rsqrt(jnp.mean(xv * xv, axis=-1, keepdims=True) + EPS)
        h = (xv * r * g_ref[...]).astype(BF16)
        h_ref[...] = h
        p_ref[...] = _nt(h, w_ref[...])

    return pl.pallas_call(
        body, name="in_proj", grid=(s // t,),
        in_specs=[pl.BlockSpec((t, D), lambda i: (i, 0)), _const((1, D)), _const((DIN, D))],
        out_specs=[pl.BlockSpec((t, DIN), lambda i: (i, 0)), pl.BlockSpec((t, D), lambda i: (i, 0))],
        out_shape=[SDS((s, DIN), F32), SDS((s, D), BF16)],
        compiler_params=_cp(),
    )(x, g1, win_t)


def _bias_spec():
    return pl.BlockSpec((None, 2, TQ, WIN), lambda j, i: (jnp.maximum(NVAR - 1 - i, 0), j, 0, 0))


def _attn_fwd(p, qg2, kg2, bias):
    s = p.shape[0]
    nq = s // TQ
    scale = HD ** -0.5

    def body(q_ref, k_ref, v_ref, qg_ref, kg_ref, b_ref, o_ref, lse_ref):
        i = pl.program_id(1)
        ks = pl.multiple_of(jnp.maximum(i * TQ - BAND, 0), TQ)
        lo = _lo_mask()
        qn = (_head_norm(q_ref[...], lo)[0] * qg_ref[...]).astype(BF16)
        kn = (_head_norm(k_ref[pl.ds(ks, WIN), :], lo)[0] * kg_ref[...]).astype(BF16)
        vb = v_ref[pl.ds(ks, WIN), :].astype(BF16)
        outs, lses = [], []
        for half in range(2):
            m_ = lo if half == 0 else jnp.logical_not(lo)
            qa = jnp.where(m_, qn, jnp.zeros_like(qn))
            sc = _nt(qa, kn) * scale + b_ref[half]
            mx = jnp.max(sc, axis=-1, keepdims=True)
            e = jnp.exp(sc - mx)
            den = jnp.sum(e, axis=-1, keepdims=True)
            pm = (e * (1.0 / den)).astype(BF16)
            outs.append(_nn(pm, vb))
            lses.append(mx + jnp.log(den))
        o_ref[...] = jnp.where(lo, outs[0], outs[1]).astype(BF16)
        lse_ref[...] = jnp.where(lo, lses[0], lses[1])

    return pl.pallas_call(
        body, name="attn_fwd", grid=(NH // 2, nq),
        in_specs=[
            pl.BlockSpec((TQ, LANE), lambda j, i: (i, j)),
            pl.BlockSpec((s, LANE), lambda j, i: (0, AW // LANE + j)),
            pl.BlockSpec((s, LANE), lambda j, i: (0, 2 * AW // LANE + j)),
            _const((1, LANE)), _const((1, LANE)),
            _bias_spec(),
        ],
        out_specs=[pl.BlockSpec((TQ, LANE), lambda j, i: (i, j)), pl.BlockSpec((TQ, LANE), lambda j, i: (i, j))],
        out_shape=[SDS((s, D), BF16), SDS((s, AW), F32)],
        compiler_params=_cp(),
    )(p, p, p, qg2, kg2, bias)


_C0 = 3 * AW // LANE


def _cp_in_specs(s):
    blk = lambda f: pl.BlockSpec((s, LANE), f)
    return [
        blk(lambda g: (0, _C0 + jnp.minimum(g, 1))),
        blk(lambda g: (0, _C0 + 2 + jnp.minimum(g, 1))),
        blk(lambda g: (0, _C0 + 4 + jnp.minimum(g, 1))),
        blk(lambda g: (0, _C0 + 6 + jnp.maximum(g - 2, 0))),
        pl.BlockSpec((3, LANE), lambda g: (0, jnp.minimum(g, 1))),
        pl.BlockSpec((None, LANE, LANE), lambda g: (jnp.maximum(g - 2, 0), 0, 0)),
        pl.BlockSpec((1, LANE), lambda g: (0, jnp.maximum(g - 2, 0))),
    ]


def _pool_window_sums(u_ref, buf_a, buf_b, jj, s, rt):
    nrt = s // rt
    for t in range(nrt):
        buf_a[pl.ds(PAD + t * rt, rt), :] = u_ref[pl.ds(t * rt, rt), :]

    def stage(src, dst, sh):
        for t in range(nrt):
            r0 = PAD + t * rt
            dst[pl.ds(r0, rt), :] = src[pl.ds(r0, rt), :] + src[pl.ds(r0 - sh, rt), :]

    stage(buf_a, buf_b, 1)
    stage(buf_b, buf_a, 2)

    @pl.when(jj == 1)
    def _():
        stage(buf_a, buf_b, 4)
        stage(buf_b, buf_a, 8)


def _pool_counts(jj, lo, r0, rt):
    w = jnp.where(lo, jnp.where(jj == 0, 2.0, 8.0), jnp.where(jj == 0, 4.0, 16.0))
    pos1 = (lax.broadcasted_iota(jnp.int32, (rt, LANE), 0) + (r0 + 1)).astype(F32)
    return jnp.minimum(pos1, w)


def _conv_pool_fwd(p, mix, conv_w, wbd, pscale):
    s = p.shape[0]
    rt = min(256, s)
    nrt = s // rt

    def body(gb_ref, gc_ref, hin_ref, u_ref, cw_ref, wbd_ref, ps_ref, mix_in, o_ref, buf_a, buf_b):
        del mix_in
        g = pl.program_id(0)
        zpad = jnp.zeros((PAD, LANE), F32)
        buf_a[pl.ds(0, PAD), :] = zpad
        buf_b[pl.ds(0, PAD), :] = zpad

        @pl.when(g < 2)
        def _conv():
            for t in range(nrt):
                buf_a[pl.ds(PAD + t * rt, rt), :] = gc_ref[pl.ds(t * rt, rt), :] * hin_ref[pl.ds(t * rt, rt), :]
            w0, w1, w2 = cw_ref[0:1, :], cw_ref[1:2, :], cw_ref[2:3, :]
            for t in range(nrt):
                r0 = PAD + t * rt
                y = w0 * buf_a[pl.ds(r0 - 2, rt), :] + w1 * buf_a[pl.ds(r0 - 1, rt), :] + w2 * buf_a[pl.ds(r0, rt), :]
                o_ref[pl.ds(t * rt, rt), :] = (gb_ref[pl.ds(t * rt, rt), :] * y).astype(BF16)

        @pl.when(g >= 2)
        def _pool():
            jj = g - 2
            lo = _lo_mask()
            _pool_window_sums(u_ref, buf_a, buf_b, jj, s, rt)
            wb = wbd_ref[...]
            for t in range(nrt):
                r0 = PAD + t * rt
                wsum = jnp.where(lo, buf_b[pl.ds(r0, rt), :], buf_a[pl.ds(r0, rt), :])
                m = wsum / _pool_counts(jj, lo, t * rt, rt) - u_ref[pl.ds(t * rt, rt), :]
                o_ref[pl.ds(t * rt, rt), :] = (_nn(m.astype(BF16), wb) * ps_ref[...]).astype(BF16)

    return pl.pallas_call(
        body, name="conv_pool_fwd", grid=(4,),
        in_specs=_cp_in_specs(s) + [pl.BlockSpec(memory_space=pl.ANY)],
        out_specs=pl.BlockSpec((s, LANE), lambda g: (0, AW // LANE + g)),
        out_shape=SDS((s, D), BF16),
        scratch_shapes=[pltpu.VMEM((s + 2 * PAD, LANE), F32), pltpu.VMEM((s + 2 * PAD, LANE), F32)],
        input_output_aliases={7: 0},
        compiler_params=_cp(),
    )(p, p, p, p, conv_w, wbd, pscale, mix)


def _mlp_fwd(x, mix, wout, g2, w1_t, w2):
    s = x.shape[0]
    t = min(256, s)

    def body(x_ref, mix_ref, wo_ref, g_ref, w1_ref, w2_ref, xm_ref, a_ref, xo_ref):
        xm = x_ref[...] + _nn(mix_ref[...], wo_ref[...])
        xm_ref[...] = xm
        r = lax.rsqrt(jnp.mean(xm * xm, axis=-1, keepdims=True) + EPS)
        h2 = (xm * r * g_ref[...]).astype(BF16)
        a = _nt(h2, w1_ref[...])
        a_ref[...] = a.astype(BF16)
        f = jnp.square(jnp.maximum(a, 0.0)).astype(BF16)
        xo_ref[...] = xm + _nn(f, w2_ref[...])

    row = lambda c: pl.BlockSpec((t, c), lambda i: (i, 0))
    return pl.pallas_call(
        body, name="mlp_fwd", grid=(s // t,),
        in_specs=[row(D), row(D), _const((D, D)), _const((1, D)), _const((DFF, D)), _const((DFF, D))],
        out_specs=[row(D), row(DFF), row(D)],
        out_shape=[SDS((s, D), F32), SDS((s, DFF), BF16), SDS((s, D), F32)],
        compiler_params=_cp(),
    )(x, mix, wout, g2, w1_t, w2)


def _loss_grad(y, target):
    s = y.shape[0]
    t = min(512, s)

    def body(y_ref, t_ref, dy_ref, acc_ref):
        @pl.when(pl.program_id(0) == 0)
        def _():
            acc_ref[...] = jnp.zeros_like(acc_ref)
        e = y_ref[...] - t_ref[...]
        dy_ref[...] = e * (1.0 / D)
        acc_ref[...] += jnp.sum(e * e)

    row = pl.BlockSpec((t, D), lambda i: (i, 0))
    return pl.pallas_call(
        body, name="loss_grad", grid=(s // t,),
        in_specs=[row, row],
        out_specs=[row, pl.BlockSpec((8, LANE), lambda i: (0, 0))],
        out_shape=[SDS((s, D), F32), SDS((8, LANE), F32)],
        compiler_params=_cp(),
    )(y, target)


def _mlp_bwd(dxo, a, xm, g2, w1_t, w2, wout):
    s = dxo.shape[0]
    t = min(256, s)

    def body(dxo_ref, a_ref, xm_ref, g_ref, w1_ref, w2_ref, wo_ref,
             dxm_ref, dmix_ref, f_ref, da_ref, h2_ref, dxob_ref, dxmb_ref, dg_ref):
        @pl.when(pl.program_id(0) == 0)
        def _():
            dg_ref[...] = jnp.zeros_like(dg_ref)
        dxo = dxo_ref[...]
        dxob = dxo.astype(BF16)
        dxob_ref[...] = dxob
        ra = jnp.maximum(a_ref[...].astype(F32), 0.0)
        f_ref[...] = jnp.square(ra).astype(BF16)
        dab = (_nt(dxob, w2_ref[...]) * (2.0 * ra)).astype(BF16)
        da_ref[...] = dab
        dh2 = _nn(dab, w1_ref[...])
        xm = xm_ref[...]
        g = g_ref[...]
        r = lax.rsqrt(jnp.mean(xm * xm, axis=-1, keepdims=True) + EPS)
        h2_ref[...] = (xm * r * g).astype(BF16)
        dx_n, dgr = _rms_bwd(dh2, xm, g)
        dg_ref[...] += jnp.sum(dgr, axis=0, keepdims=True)
        dxm = dxo + dx_n
        dxm_ref[...] = dxm
        dxmb = dxm.astype(BF16)
        dxmb_ref[...] = dxmb
        dmix_ref[...] = _nt(dxmb, wo_ref[...])

    row = lambda c: pl.BlockSpec((t, c), lambda i: (i, 0))
    return pl.pallas_call(
        body, name="mlp_bwd", grid=(s // t,),
        in_specs=[row(D), row(DFF), row(D), _const((1, D)), _const((DFF, D)), _const((DFF, D)), _const((D, D))],
        out_specs=[row(D), row(D), row(DFF), row(DFF), row(D), row(D), row(D), pl.BlockSpec((1, D), lambda i: (0, 0))],
        out_shape=[SDS((s, D), F32), SDS((s, D), F32), SDS((s, DFF), BF16), SDS((s, DFF), BF16),
                   SDS((s, D), BF16), SDS((s, D), BF16), SDS((s, D), BF16), SDS((1, D), F32)],
        compiler_params=_cp(),
    )(dxo, a, xm, g2, w1_t, w2, wout)


def _attn_bwd(p, lse, dmix, qg2, kg2, bias):
    s = p.shape[0]
    nq = s // TQ
    scale = HD ** -0.5
    rt = min(512, s)

    def body(q_ref, k_ref, v_ref, qg_ref, kg_ref, b_ref, lse_ref, do_ref,
             dq_ref, dk_ref, dv_ref, db_ref, dqg_ref, dkg_ref, dk_acc, dv_acc):
        i = pl.program_id(1)
        ks = pl.multiple_of(jnp.maximum(i * TQ - BAND, 0), TQ)
        lo = _lo_mask()

        @pl.when(i == 0)
        def _():
            dk_acc[...] = jnp.zeros_like(dk_acc)
            dv_acc[...] = jnp.zeros_like(dv_acc)
            dqg_ref[...] = jnp.zeros_like(dqg_ref)
            dkg_ref[...] = jnp.zeros_like(dkg_ref)

        qg, kg = qg_ref[...], kg_ref[...]
        xq, rq = _head_norm(q_ref[...], lo)
        qn = (xq * qg).astype(BF16)
        kn = (_head_norm(k_ref[pl.ds(ks, WIN), :], lo)[0] * kg).astype(BF16)
        vb = v_ref[pl.ds(ks, WIN), :].astype(BF16)
        dob = do_ref[...].astype(BF16)
        lse = lse_ref[...]
        dqn = jnp.zeros((TQ, LANE), F32)
        dkn = jnp.zeros((WIN, LANE), F32)
        dvw = jnp.zeros((WIN, LANE), F32)
        for half in range(2):
            m_ = lo if half == 0 else jnp.logical_not(lo)
            qa = jnp.where(m_, qn, jnp.zeros_like(qn))
            doa = jnp.where(m_, dob, jnp.zeros_like(dob))
            lse_h = lse[:, half * HD:half * HD + 1]
            pm = jnp.exp(_nt(qa, kn) * scale + b_ref[half] - lse_h)
            dp = _nt(doa, vb)
            delta = jnp.sum(pm * dp, axis=-1, keepdims=True)
            ds = pm * (dp - delta)

            @pl.when(i < NVAR)
            def _():
                db_ref[half] = ds

            @pl.when(i >= NVAR)
            def _():
                db_ref[half] += ds

            dsb = ds.astype(BF16)
            dqn = jnp.where(m_, _nn(dsb, kn), dqn)
            dkn = dkn + _tn(dsb, qa)
            dvw = dvw + _tn(pm.astype(BF16), doa)
        dq, dqg_rows = _head_norm_bwd(dqn * scale, xq, rq, qg, lo)
        dq_ref[...] = dq.astype(BF16)
        dqg_ref[...] += jnp.sum(dqg_rows, axis=0, keepdims=True)
        dk_acc[pl.ds(ks, WIN), :] += dkn * scale
        dv_acc[pl.ds(ks, WIN), :] += dvw

        @pl.when(i == nq - 1)
        def _():
            dkg = jnp.zeros((1, LANE), F32)
            for t in range(s // rt):
                rows = pl.ds(t * rt, rt)
                xk, rk = _head_norm(k_ref[rows, :], lo)
                dk, dkg_rows = _head_norm_bwd(dk_acc[rows, :], xk, rk, kg, lo)
                dk_ref[rows, :] = dk.astype(BF16)
                dv_ref[rows, :] = dv_acc[rows, :].astype(BF16)
                dkg = dkg + jnp.sum(dkg_rows, axis=0, keepdims=True)
            dkg_ref[...] = dkg

    tile = pl.BlockSpec((TQ, LANE), lambda j, i: (i, j))
    gain = pl.BlockSpec((None, 1, LANE), lambda j, i: (j, 0, 0))
    return pl.pallas_call(
        body, name="attn_bwd", grid=(NH // 2, nq),
        in_specs=[
            tile,
            pl.BlockSpec((s, LANE), lambda j, i: (0, AW // LANE + j)),
            pl.BlockSpec((s, LANE), lambda j, i: (0, 2 * AW // LANE + j)),
            _const((1, LANE)), _const((1, LANE)),
            _bias_spec(), tile, tile,
        ],
        out_specs=[tile, pl.BlockSpec((s, LANE), lambda j, i: (0, j)), pl.BlockSpec((s, LANE), lambda j, i: (0, j)),
                   _bias_spec(), gain, gain],
        out_shape=[SDS((s, AW), BF16), SDS((s, AW), BF16), SDS((s, AW), BF16),
                   SDS((NVAR, NH, TQ, WIN), F32), SDS((NH // 2, 1, LANE), F32), SDS((NH // 2, 1, LANE), F32)],
        scratch_shapes=[pltpu.VMEM((s, LANE), F32), pltpu.VMEM((s, LANE), F32)],
        compiler_params=_cp(),
    )(p, p, p, qg2, kg2, bias, lse, dmix)


def _conv_pool_bwd(p, dmix, conv_w, wbd, pscale):
    s = p.shape[0]
    rt = min(256, s)
    nrt = s // rt

    def body(gb_ref, gc_ref, hin_ref, u_ref, cw_ref, wbd_ref, ps_ref, dy_ref,
             dgb_ref, dgc_ref, dhin_ref, du_ref, dcw_ref, dwbd_ref, dps_ref, buf_a, buf_b, buf_c, buf_d):
        g = pl.program_id(0)
        zpad = jnp.zeros((PAD, LANE), F32)
        for buf in (buf_a, buf_b, buf_c):
            buf[pl.ds(0, PAD), :] = zpad
            buf[pl.ds(PAD + s, PAD), :] = zpad

        @pl.when(g < 2)
        def _conv():
            for t in range(nrt):
                rows = pl.ds(t * rt, rt)
                buf_a[pl.ds(PAD + t * rt, rt), :] = gc_ref[rows, :] * hin_ref[rows, :]
                buf_b[pl.ds(PAD + t * rt, rt), :] = dy_ref[rows, :] * gb_ref[rows, :]
            w0, w1, w2 = cw_ref[0:1, :], cw_ref[1:2, :], cw_ref[2:3, :]
            d0 = jnp.zeros((1, LANE), F32)
            d1 = jnp.zeros((1, LANE), F32)
            d2 = jnp.zeros((1, LANE), F32)
            for t in range(nrt):
                rows = pl.ds(t * rt, rt)
                r0 = PAD + t * rt
                z2, z1, z0 = buf_a[pl.ds(r0 - 2, rt), :], buf_a[pl.ds(r0 - 1, rt), :], buf_a[pl.ds(r0, rt), :]
                y = w0 * z2 + w1 * z1 + w2 * z0
                dgb_ref[rows, :] = (dy_ref[rows, :] * y).astype(BF16)
                e0 = buf_b[pl.ds(r0, rt), :]
                d0 = d0 + jnp.sum(e0 * z2, axis=0, keepdims=True)
                d1 = d1 + jnp.sum(e0 * z1, axis=0, keepdims=True)
                d2 = d2 + jnp.sum(e0 * z0, axis=0, keepdims=True)
                dz = w2 * e0 + w1 * buf_b[pl.ds(r0 + 1, rt), :] + w0 * buf_b[pl.ds(r0 + 2, rt), :]
                dgc_ref[rows, :] = (dz * hin_ref[rows, :]).astype(BF16)
                dhin_ref[rows, :] = (dz * gc_ref[rows, :]).astype(BF16)
            dcw_ref[0:1, :] = d0
            dcw_ref[1:2, :] = d1
            dcw_ref[2:3, :] = d2

        @pl.when(g >= 2)
        def _pool():
            jj = g - 2
            lo = _lo_mask()
            _pool_window_sums(u_ref, buf_a, buf_b, jj, s, rt)
            wb = wbd_ref[...]
            ps = ps_ref[...]
            dps = jnp.zeros((1, LANE), F32)
            dwb = jnp.zeros((LANE, LANE), F32)
            for t in range(nrt):
                rows = pl.ds(t * rt, rt)
                r0 = PAD + t * rt
                cnt = _pool_counts(jj, lo, t * rt, rt)
                wsum = jnp.where(lo, buf_b[pl.ds(r0, rt), :], buf_a[pl.ds(r0, rt), :])
                mb = (wsum / cnt - u_ref[rows, :]).astype(BF16)
                dy = dy_ref[rows, :]
                dps = dps + jnp.sum(dy * _nn(mb, wb), axis=0, keepdims=True)
                dmp = (dy * ps).astype(BF16)
                dwb = dwb + _tn(mb, dmp)
                dm = _nt(dmp, wb)
                buf_d[rows, :] = dm
                buf_c[pl.ds(r0, rt), :] = dm / cnt
            dps_ref[...] = dps
            dwbd_ref[...] = dwb

            def stage(src, dst, sh):
                for t in range(nrt):
                    r0 = PAD + t * rt
                    dst[pl.ds(r0, rt), :] = src[pl.ds(r0, rt), :] + src[pl.ds(r0 + sh, rt), :]

            def finish(first, second):
                for t in range(nrt):
                    rows = pl.ds(t * rt, rt)
                    r0 = PAD + t * rt
                    fw = jnp.where(lo, first[pl.ds(r0, rt), :], second[pl.ds(r0, rt), :])
                    du_ref[rows, :] = (fw - buf_d[rows, :]).astype(BF16)

            stage(buf_c, buf_a, 1)
            stage(buf_a, buf_b, 2)

            @pl.when(jj == 0)
            def _():
                finish(buf_a, buf_b)

            @pl.when(jj == 1)
            def _():
                stage(buf_b, buf_c, 4)
                stage(buf_c, buf_a, 8)
                finish(buf_c, buf_a)

    cblk = pl.BlockSpec((s, LANE), lambda g: (0, jnp.minimum(g, 1)))
    pblk = pl.BlockSpec((s, LANE), lambda g: (0, jnp.maximum(g - 2, 0)))
    padded = pltpu.VMEM((s + 2 * PAD, LANE), F32)
    return pl.pallas_call(
        body, name="conv_pool_bwd", grid=(4,),
        in_specs=_cp_in_specs(s) + [pl.BlockSpec((s, LANE), lambda g: (0, AW // LANE + g))],
        out_specs=[cblk, cblk, cblk, pblk,
                   pl.BlockSpec((3, LANE), lambda g: (0, jnp.minimum(g, 1))),
                   pl.BlockSpec((None, LANE, LANE), lambda g: (jnp.maximum(g - 2, 0), 0, 0)),
                   pl.BlockSpec((1, LANE), lambda g: (0, jnp.maximum(g - 2, 0)))],
        out_shape=[SDS((s, CW), BF16), SDS((s, CW), BF16), SDS((s, CW), BF16), SDS((s, PWD), BF16),
                   SDS((3, CW), F32), SDS((2, LANE, LANE), F32), SDS((1, PWD), F32)],
        scratch_shapes=[padded, padded, padded, pltpu.VMEM((s, LANE), F32)],
        compiler_params=_cp(),
    )(p, p, p, p, conv_w, wbd, pscale, dmix)


def _in_proj_bwd(parts, x, dxm, g1, win_t):
    s = x.shape[0]
    t = min(256, s)
    widths = [a.shape[1] for a in parts]
    offs = [int(o) for o in np.cumsum([0] + widths[:-1])]
    n = len(parts)

    def body(*refs):
        part_refs = refs[:n]
        x_ref, dxm_ref, g_ref, w_ref, dx_ref, dp_ref, dg_ref = refs[n:]

        @pl.when(pl.program_id(0) == 0)
        def _():
            dg_ref[...] = jnp.zeros_like(dg_ref)
        for r, o, w in zip(part_refs, offs, widths):
            dp_ref[:, o:o + w] = r[...]
        dh = _nn(dp_ref[...], w_ref[...])
        dx_n, dgr = _rms_bwd(dh, x_ref[...], g_ref[...])
        dg_ref[...] += jnp.sum(dgr, axis=0, keepdims=True)
        dx_ref[...] = dxm_ref[...] + dx_n

    row = lambda c: pl.BlockSpec((t, c), lambda i: (i, 0))
    return pl.pallas_call(
        body, name="in_proj_bwd", grid=(s // t,),
        in_specs=[row(w) for w in widths] + [row(D), row(D), _const((1, D)), _const((DIN, D))],
        out_specs=[row(D), row(DIN), pl.BlockSpec((1, D), lambda i: (0, 0))],
        out_shape=[SDS((s, D), F32), SDS((s, DIN), BF16), SDS((1, D), F32)],
        compiler_params=_cp(),
    )(*parts, x, dxm, g1, win_t)


def _wgrad(a, b, stacked, layer, tag):
    s, m = a.shape
    mb = 512
    t = min(1024, s)
    nt = s // t

    def body(*refs):
        a_ref, b_ref = refs[:2]
        o_ref, acc = refs[-2:]

        @pl.when(pl.program_id(1) == 0)
        def _():
            acc[...] = jnp.zeros_like(acc)
        acc[...] += _tn(a_ref[...], b_ref[...])

        @pl.when(pl.program_id(1) == nt - 1)
        def _():
            o_ref[...] = acc[...].astype(BF16)

    in_specs = [pl.BlockSpec((t, mb), lambda mi, ti: (ti, mi)), pl.BlockSpec((t, D), lambda mi, ti: (ti, 0))]
    args = [a, b]
    aliases = {}
    if stacked is not None:
        in_specs.append(pl.BlockSpec(memory_space=pl.ANY))
        args.append(stacked)
        aliases = {2: 0}
    return pl.pallas_call(
        body, name=f"wgrad_{tag}_l{layer}", grid=(m // mb, nt),
        in_specs=in_specs,
        out_specs=pl.BlockSpec((None, mb, D), lambda mi, ti: (layer, mi, 0)),
        out_shape=SDS((L, m, D), BF16),
        scratch_shapes=[pltpu.VMEM((mb, D), F32)],
        input_output_aliases=aliases,
        compiler_params=_cp(),
    )(*args)


_MASKS = [(mx, my, mc) for mx in (0, 1) for my in (0, 1) for mc in (0, 1)][1:]


def _position():
    return lax.axis_index("x"), lax.axis_index("y"), lax.axis_index("c")


def _peer(pos, mask):
    return tuple(1 - a if f else a for a, f in zip(pos, mask))


def _index(pos):
    return 4 * pos[0] + 2 * pos[1] + pos[2]


def _gather_weights(shards):
    nm = len(shards)

    def body(*refs):
        src = refs[:nm]
        dst = refs[nm:2 * nm]
        send_sems, recv_sems, local_sems = refs[2 * nm:]
        me = _position()

        def rows(k, pos):
            r = SHARD_ROWS[k]
            return dst[k].at[:, pl.ds(_index(pos) * r, r), :]

        local = [pltpu.make_async_copy(src[k], rows(k, me), local_sems.at[k]) for k in range(nm)]
        for cp in local:
            cp.start()
        sends = []
        for pi, mask in enumerate(_MASKS):
            for k in range(nm):
                cp = pltpu.make_async_remote_copy(
                    src_ref=src[k], dst_ref=rows(k, me), send_sem=send_sems.at[pi, k], recv_sem=recv_sems.at[pi, k],
                    device_id=_peer(me, mask), device_id_type=MESH_ID)
                cp.start()
                sends.append(cp)
        for pi, mask in enumerate(_MASKS):
            for k in range(nm):
                pltpu.make_async_remote_copy(
                    src_ref=src[k], dst_ref=rows(k, _peer(me, mask)), send_sem=send_sems.at[pi, k],
                    recv_sem=recv_sems.at[pi, k], device_id=_peer(me, mask), device_id_type=MESH_ID).wait_recv()
        for cp in sends:
            cp.wait_send()
        for cp in local:
            cp.wait()

    hbm = pl.BlockSpec(memory_space=pl.ANY)
    return pl.pallas_call(
        body, name="gather_weights",
        in_specs=[hbm] * nm, out_specs=[hbm] * nm,
        out_shape=[SDS((L, NDEV * a.shape[1], D), a.dtype) for a in shards],
        scratch_shapes=[pltpu.SemaphoreType.DMA((len(_MASKS), nm)), pltpu.SemaphoreType.DMA((len(_MASKS), nm)),
                        pltpu.SemaphoreType.DMA((nm,))],
    )(*shards)


def _scatter_grads(grads):
    nm = len(grads)

    def body(*refs):
        src = refs[:nm]
        dst = refs[nm:2 * nm]
        send_sems, recv_sems, local_sems = refs[2 * nm:]
        me = _position()

        def rows(k, pos):
            r = SHARD_ROWS[k]
            return src[k].at[:, pl.ds(_index(pos) * r, r), :]

        local = [pltpu.make_async_copy(rows(k, me), dst[k].at[_index(me)], local_sems.at[k]) for k in range(nm)]
        for cp in local:
            cp.start()
        sends = []
        for pi, mask in enumerate(_MASKS):
            peer = _peer(me, mask)
            for k in range(nm):
                cp = pltpu.make_async_remote_copy(
                    src_ref=rows(k, peer), dst_ref=dst[k].at[_index(me)], send_sem=send_sems.at[pi, k],
                    recv_sem=recv_sems.at[pi, k], device_id=peer, device_id_type=MESH_ID)
                cp.start()
                sends.append(cp)
        for pi, mask in enumerate(_MASKS):
            peer = _peer(me, mask)
            for k in range(nm):
                pltpu.make_async_remote_copy(
                    src_ref=rows(k, me), dst_ref=dst[k].at[_index(peer)], send_sem=send_sems.at[pi, k],
                    recv_sem=recv_sems.at[pi, k], device_id=peer, device_id_type=MESH_ID).wait_recv()
        for cp in sends:
            cp.wait_send()
        for cp in local:
            cp.wait()

    hbm = pl.BlockSpec(memory_space=pl.ANY)
    return pl.pallas_call(
        body, name="scatter_grads",
        in_specs=[hbm] * nm, out_specs=[hbm] * nm,
        out_shape=[SDS((NDEV, L, a.shape[1] // NDEV, D), a.dtype) for a in grads],
        scratch_shapes=[pltpu.SemaphoreType.DMA((len(_MASKS), nm)), pltpu.SemaphoreType.DMA((len(_MASKS), nm)),
                        pltpu.SemaphoreType.DMA((nm,))],
    )(*grads)


def _sum_slots(slots):
    _, _, r, _ = slots.shape
    rt = 64

    def body(in_ref, o_ref):
        acc = in_ref[0].astype(F32)
        for d in range(1, NDEV):
            acc = acc + in_ref[d].astype(F32)
        o_ref[...] = acc

    return pl.pallas_call(
        body, name=f"sum_slots_r{r}", grid=(L, r // rt),
        in_specs=[pl.BlockSpec((NDEV, None, rt, D), lambda l, i: (0, l, i, 0))],
        out_specs=pl.BlockSpec((None, rt, D), lambda l, i: (l, i, 0)),
        out_shape=SDS((L, r, D), F32),
        compiler_params=_cp(),
    )(slots)


def _exchange_small(v, reduce):
    rows = v.shape[0]

    def body(v_ref, o_ref, *scratch):
        if reduce:
            slots, send_sems, recv_sems = scratch
        else:
            slots = o_ref
            send_sems, recv_sems = scratch
        me = _position()
        slots[_index(me)] = v_ref[...]
        sends = []
        for pi, mask in enumerate(_MASKS):
            cp = pltpu.make_async_remote_copy(
                src_ref=v_ref, dst_ref=slots.at[_index(me)], send_sem=send_sems.at[pi], recv_sem=recv_sems.at[pi],
                device_id=_peer(me, mask), device_id_type=MESH_ID)
            cp.start()
            sends.append(cp)
        for pi, mask in enumerate(_MASKS):
            peer = _peer(me, mask)
            pltpu.make_async_remote_copy(
                src_ref=v_ref, dst_ref=slots.at[_index(peer)], send_sem=send_sems.at[pi], recv_sem=recv_sems.at[pi],
                device_id=peer, device_id_type=MESH_ID).wait_recv()
        for cp in sends:
            cp.wait_send()
        if reduce:
            acc = slots[0]
            for d in range(1, NDEV):
                acc = acc + slots[d]
            o_ref[...] = acc

    vm = pl.BlockSpec(memory_space=pltpu.VMEM)
    sems = [pltpu.SemaphoreType.DMA((len(_MASKS),)), pltpu.SemaphoreType.DMA((len(_MASKS),))]
    return pl.pallas_call(
        body, name="reduce_small" if reduce else "gather_small",
        in_specs=[vm], out_specs=vm,
        out_shape=SDS((rows, LANE) if reduce else (NDEV, rows, LANE), F32),
        scratch_shapes=([pltpu.VMEM((NDEV, rows, LANE), F32)] if reduce else []) + sems,
        compiler_params=_cp(),
    )(v)


def _adamw(w, g, m, v):
    rows, cols = w.shape
    t = rows
    for cand in (512, 256, 128, 64, 32, 16, 8):
        if rows % cand == 0:
            t = cand
            break

    def body(w_ref, g_ref, m_ref, v_ref, d_ref, nm_ref, nv_ref):
        gv = g_ref[...]
        mn = B1 * m_ref[...] + (1.0 - B1) * gv
        vn = B2 * v_ref[...] + (1.0 - B2) * jnp.square(gv)
        nm_ref[...] = mn
        nv_ref[...] = vn
        m_hat = mn / (1.0 - B1 ** STEP)
        v_hat = vn / (1.0 - B2 ** STEP)
        d_ref[...] = -LR * (m_hat / (jnp.sqrt(v_hat) + AEPS) + WD * w_ref[...])

    blk = pl.BlockSpec((t, cols), lambda i: (i, 0))
    return pl.pallas_call(
        body, name=f"adamw_{rows}x{cols}", grid=(rows // t,),
        in_specs=[blk] * 4, out_specs=[blk] * 3,
        out_shape=[SDS((rows, cols), F32)] * 3,
        compiler_params=_cp(),
    )(w, g, m, v)


def _build_bias(rel_bias_l):
    n = WIN + TQ - 1
    variants = []
    for var in range(NVAR):
        base = BAND - var * TQ
        dist = base + (TQ - 1) - np.arange(n)
        n_far = int(np.sum(dist >= REL_CLIP))
        n_near = int(np.sum(dist <= -REL_CLIP))
        n_mid = n - n_far - n_near
        hi = int(dist[n_far]) + REL_CLIP
        mid = lax.rev(rel_bias_l[:, hi - n_mid + 1:hi + 1], (1,))
        e = jnp.concatenate([
            jnp.broadcast_to(rel_bias_l[:, -1:], (NH, n_far)), mid,
            jnp.broadcast_to(rel_bias_l[:, :1], (NH, n_near))], axis=1)
        flat = jnp.pad(jnp.tile(e, (1, TQ)), ((0, 0), (0, TQ)))
        toep = lax.rev(flat.reshape(NH, TQ, n + 1)[:, :, :WIN], (1,))
        qc = np.arange(TQ)[:, None] // CHUNK
        kc = np.arange(WIN)[None, :] // CHUNK
        rel = base // CHUNK + qc - kc
        valid = (rel >= 0) & (rel <= N_PREV)
        variants.append(jnp.where(jnp.asarray(valid)[None], toep, NEG))
    return jnp.stack(variants)


def _pool_blockdiag(pool_w_l):
    z = jnp.zeros((HD, HD), F32)
    blocks = [jnp.block([[pool_w_l[2 * j], z], [z, pool_w_l[2 * j + 1]]]) for j in range(2)]
    return jnp.stack(blocks)


def _pack(arrays, rows):
    flat = jnp.concatenate([a.reshape(-1).astype(F32) for a in arrays])
    return jnp.pad(flat, (0, rows * LANE - flat.shape[0])).reshape(rows, LANE)


def _unpack(packed, shapes):
    flat = packed.reshape(-1)
    out, o = [], 0
    for shp in shapes:
        n = int(np.prod(shp))
        out.append(flat[o:o + n].reshape(shp))
        o += n
    return out


def _rows_for(shapes):
    n = sum(int(np.prod(s)) for s in shapes)
    return -(-n // (8 * LANE)) * 8


def _local_grads(x, target, weights):
    g1s, win_ts, qgs, kgs, rbs, cws, pws, pss, wouts, g2s, w1_ts, w2s = weights
    saved = []
    h = x
    for l in range(L):
        qg2 = jnp.tile(qgs[l].reshape(1, HD), (1, 2))
        kg2 = jnp.tile(kgs[l].reshape(1, HD), (1, 2))
        bias, bias_vjp = jax.vjp(_build_bias, rbs[l])
        wbd, wbd_vjp = jax.vjp(_pool_blockdiag, pws[l])
        wbd16 = wbd.astype(BF16)
        ps = pss[l].reshape(1, PWD)
        p, h_b = _in_proj(h, g1s[l].reshape(1, D), win_ts[l])
        mix, lse = _attn_fwd(p, qg2, kg2, bias)
        mix = _conv_pool_fwd(p, mix, cws[l], wbd16, ps)
        xm, a, xo = _mlp_fwd(h, mix, wouts[l], g2s[l].reshape(1, D), w1_ts[l], w2s[l])
        saved.append((h, h_b, p, mix, lse, xm, a, qg2, kg2, bias, bias_vjp, wbd16, wbd_vjp, ps))
        h = xo
    dx, sq = _loss_grad(h, target)
    stacks = [None] * 4
    small = [None] * L
    for l in reversed(range(L)):
        x_in, h_b, p, mix, lse, xm, a, qg2, kg2, bias, bias_vjp, wbd16, wbd_vjp, ps = saved[l]
        dxm, dmix, f_b, da_b, h2_b, dxo_b, dxm_b, dg2 = _mlp_bwd(
            dx, a, xm, g2s[l].reshape(1, D), w1_ts[l], w2s[l], wouts[l])
        dq, dk, dv, dbias, dqg, dkg = _attn_bwd(p, lse, dmix, qg2, kg2, bias)
        dgb, dgc, dhin, du, dcw, dwbd, dps = _conv_pool_bwd(p, dmix, cws[l], wbd16, ps)
        dx, dp_b, dg1 = _in_proj_bwd([dq, dk, dv, dgb, dgc, dhin, du], x_in, dxm, g1s[l].reshape(1, D), win_ts[l])
        stacks[0] = _wgrad(dp_b, h_b, stacks[0], l, "w_in")
        stacks[1] = _wgrad(mix, dxm_b, stacks[1], l, "w_out")
        stacks[2] = _wgrad(da_b, h2_b, stacks[2], l, "w_mlp1")
        stacks[3] = _wgrad(f_b, dxo_b, stacks[3], l, "w_mlp2")
        small[l] = dict(
            g1=dg1.reshape(D), g2=dg2.reshape(D),
            qg=dqg.reshape(NH, HD).sum(0), kg=dkg.reshape(NH, HD).sum(0),
            rb=bias_vjp(dbias)[0], cw=dcw, pw=wbd_vjp(dwbd)[0], ps=dps.reshape(PWD))
    return sq, dx, stacks, small


def kernel(x, norm1_g, w_in, q_norm_g, k_norm_g, rel_bias, conv_w, pool_w, pool_scale, w_out, norm2_g, w_mlp1, w_mlp2, loss_target, m_norm1_g, m_w_in, m_q_norm_g, m_k_norm_g, m_rel_bias, m_conv_w, m_pool_w, m_pool_scale, m_w_out, m_norm2_g, m_w_mlp1, m_w_mlp2, v_norm1_g, v_w_in, v_q_norm_g, v_k_norm_g, v_rel_bias, v_conv_w, v_pool_w, v_pool_scale, v_w_out, v_norm2_g, v_w_mlp1, v_w_mlp2):
    me = _index(_position())
    cshard = CW // NDEV

    shards = [jnp.swapaxes(w_in, 1, 2).astype(BF16), w_out.astype(BF16),
              jnp.swapaxes(w_mlp1, 1, 2).astype(BF16), w_mlp2.astype(BF16)]
    win_t, wout, w1_t, w2 = _gather_weights(shards)
    cw_all = _exchange_small(_pack([conv_w], 8), reduce=False)
    cw_full = jnp.concatenate(
        [cw_all[d].reshape(-1)[:L * 3 * cshard].reshape(L, 3, cshard) for d in range(NDEV)], axis=2)

    weights = ([norm1_g[l] for l in range(L)], [win_t[l] for l in range(L)],
               [q_norm_g[l] for l in range(L)], [k_norm_g[l] for l in range(L)],
               [rel_bias[l] for l in range(L)], [cw_full[l] for l in range(L)],
               [pool_w[l] for l in range(L)], [pool_scale[l] for l in range(L)],
               [wout[l] for l in range(L)], [norm2_g[l] for l in range(L)],
               [w1_t[l] for l in range(L)], [w2[l] for l in range(L)])
    sq, grad_x, stacks, small = _local_grads(x[0], loss_target[0], weights)

    slots = _scatter_grads(stacks)
    g_win_t, g_wout, g_w1_t, g_w2 = [_sum_slots(sl) for sl in slots]
    g_w_in = jnp.swapaxes(g_win_t, 1, 2)
    g_w_mlp1 = jnp.swapaxes(g_w1_t, 1, 2)

    names = ("g1", "qg", "kg", "rb", "cw", "pw", "ps", "g2")
    gshapes = [(L, D), (L, HD), (L, HD), (L, NH, 2 * REL_CLIP + 1), (L, 3, CW), (L, 4, HD, HD), (L, PWD), (L, D)]
    garrs = [jnp.stack([small[l][n] for l in range(L)]) for n in names]
    rows = _rows_for(gshapes + [(1,)])
    total = _exchange_small(_pack(garrs + [sq[0, :1]], rows), reduce=True)
    g_g1, g_qg, g_kg, g_rb, g_cw_full, g_pw, g_ps, g_g2, sq_sum = _unpack(total, gshapes + [(1,)])
    loss = (0.5 / D) * sq_sum[0]
    g_cw = lax.dynamic_slice_in_dim(g_cw_full, me * cshard, cshard, axis=2)

    def big(w, g, m, v):
        shp = w.shape
        r = lambda a: a.reshape(-1, shp[-1])
        return [o.reshape(shp) for o in _adamw(r(w), r(g), r(m), r(v))]

    up_in = big(w_in, g_w_in, m_w_in, v_w_in)
    up_out = big(w_out, g_wout, m_w_out, v_w_out)
    up_1 = big(w_mlp1, g_w_mlp1, m_w_mlp1, v_w_mlp1)
    up_2 = big(w_mlp2, g_w2, m_w_mlp2, v_w_mlp2)

    sw = [norm1_g, q_norm_g, k_norm_g, rel_bias, conv_w, pool_w, pool_scale, norm2_g]
    sg = [g_g1, g_qg, g_kg, g_rb, g_cw, g_pw, g_ps, g_g2]
    sm = [m_norm1_g, m_q_norm_g, m_k_norm_g, m_rel_bias, m_conv_w, m_pool_w, m_pool_scale, m_norm2_g]
    sv = [v_norm1_g, v_q_norm_g, v_k_norm_g, v_rel_bias, v_conv_w, v_pool_w, v_pool_scale, v_norm2_g]
    sshapes = [a.shape for a in sw]
    srows = _rows_for(sshapes)
    ups = _adamw(_pack(sw, srows), _pack(sg, srows), _pack(sm, srows), _pack(sv, srows))
    s_delta, s_m, s_v = [_unpack(u, sshapes) for u in ups]

    def order(small_list, in_, out_, m1, m2):
        g1_, qg_, kg_, rb_, cw_, pw_, ps_, g2_ = small_list
        return [g1_, in_, qg_, kg_, rb_, cw_, pw_, ps_, out_, g2_, m1, m2]

    grads = order(sg, g_w_in, g_wout, g_w_mlp1, g_w2)
    deltas = order(s_delta, up_in[0], up_out[0], up_1[0], up_2[0])
    new_m = order(s_m, up_in[1], up_out[1], up_1[1], up_2[1])
    new_v = order(s_v, up_in[2], up_out[2], up_1[2], up_2[2])
    return (loss, grad_x[None], *grads, *deltas, *new_m, *new_v)
```

```python
import numpy as np
import jax
import jax.numpy as jnp
from jax import lax
from jax.experimental import pallas as pl
from jax.experimental.pallas import tpu as pltpu

F32 = jnp.float32
BF16 = jnp.bfloat16
SDS = jax.ShapeDtypeStruct
MESH_ID = pl.DeviceIdType.MESH

D = 1024
L = 4
CHUNK = 64
N_PREV = 8
HD = 64
NH = 8
AW = 512
CW = 256
PWD = 256
DIN = 2560
DFF = 4096
EPS = 1e-6
NEG = -1e30
REL_CLIP = 128
POOL_WINDOWS = (2, 4, 8, 16)
LR, B1, B2, AEPS, WD, STEP = 0.001, 0.9, 0.999, 1e-08, 0.01, 10

NDEV = 8
LANE = 128
BAND = N_PREV * CHUNK
TQ = 256
WIN = TQ + BAND
NVAR = BAND // TQ + 1
NTOE = -(-(WIN + TQ - 1) // LANE) * LANE
NG = (NVAR - 1) * TQ + NTOE
PAD = 16
VMEM_LIMIT = 56 * 1024 * 1024
SHARD_ROWS = (DIN // NDEV, D // NDEV, DFF // NDEV, DFF // NDEV)


def _cp(**kw):
    return pltpu.CompilerParams(vmem_limit_bytes=VMEM_LIMIT, **kw)


def _nn(a, b):
    return jnp.dot(a, b, preferred_element_type=F32)


def _nt(a, b):
    return lax.dot_general(a, b, (((1,), (1,)), ((), ())), preferred_element_type=F32)


def _tn(a, b):
    return lax.dot_general(a, b, (((0,), (0,)), ((), ())), preferred_element_type=F32)


def _const(shape):
    n = len(shape)
    return pl.BlockSpec(shape, lambda *_: (0,) * n, pipeline_mode=pl.Buffered(1))


def _layer(shape, l):
    n = len(shape)
    return pl.BlockSpec((None,) + tuple(shape), lambda *_: (l,) + (0,) * n, pipeline_mode=pl.Buffered(1))


def _lo_mask():
    return lax.broadcasted_iota(jnp.int32, (1, LANE), 1) < HD


def _half_sum(t, lo):
    s_lo = jnp.sum(jnp.where(lo, t, 0.0), axis=-1, keepdims=True)
    s_hi = jnp.sum(jnp.where(lo, 0.0, t), axis=-1, keepdims=True)
    return jnp.where(lo, s_lo, s_hi)


def _head_norm(x, lo):
    r = lax.rsqrt(_half_sum(x * x, lo) * (1.0 / HD) + EPS)
    return x * r, r


def _head_norm_bwd(dy, xn, r, g, lo):
    dxn = dy * g
    mu = _half_sum(dxn * xn, lo) * (1.0 / HD)
    return r * (dxn - xn * mu), dy * xn


def _rms_bwd(dy, x, g):
    r = lax.rsqrt(jnp.mean(x * x, axis=-1, keepdims=True) + EPS)
    xn = x * r
    dxn = dy * g
    mu = jnp.mean(dxn * xn, axis=-1, keepdims=True)
    return r * (dxn - xn * mu), dy * xn


def _in_proj(x, g1, win_t, l):
    s = x.shape[0]
    t = min(512, s)

    def body(x_ref, g_ref, w_ref, p_ref, h_ref):
        xv = x_ref[...]
        r = lax.rsqrt(jnp.mean(xv * xv, axis=-1, keepdims=True) + EPS)
        h = (xv * r * g_ref[...]).astype(BF16)
        h_ref[...] = h
        p_ref[...] = _nt(h, w_ref[...])

    return pl.pallas_call(
        body, name="in_proj", grid=(s // t,),
        in_specs=[pl.BlockSpec((t, D), lambda i: (i, 0)), _layer((1, D), l), _layer((DIN, D), l)],
        out_specs=[pl.BlockSpec((t, DIN), lambda i: (i, 0)), pl.BlockSpec((t, D), lambda i: (i, 0))],
        out_shape=[SDS((s, DIN), F32), SDS((s, D), BF16)],
        compiler_params=_cp(),
    )(x, g1, win_t)


def _bias_spec():
    return pl.BlockSpec((None, 2, TQ, WIN), lambda j, i: (jnp.maximum(NVAR - 1 - i, 0), j, 0, 0))


def _bias_layer_spec(l):
    return pl.BlockSpec((None, None, 2, TQ, WIN), lambda j, i: (l, jnp.maximum(NVAR - 1 - i, 0), j, 0, 0))


def _attn_fwd(p, qg2, kg2, bias, l):
    s = p.shape[0]
    nq = s // TQ
    scale = HD ** -0.5

    def body(q_ref, k_ref, v_ref, qg_ref, kg_ref, b_ref, o_ref, lse_ref):
        i = pl.program_id(1)
        ks = pl.multiple_of(jnp.maximum(i * TQ - BAND, 0), TQ)
        lo = _lo_mask()
        qn = (_head_norm(q_ref[...], lo)[0] * qg_ref[...]).astype(BF16)
        kn = (_head_norm(k_ref[pl.ds(ks, WIN), :], lo)[0] * kg_ref[...]).astype(BF16)
        vb = v_ref[pl.ds(ks, WIN), :].astype(BF16)
        outs, lses = [], []
        for half in range(2):
            m_ = lo if half == 0 else jnp.logical_not(lo)
            qa = jnp.where(m_, qn, jnp.zeros_like(qn))
            sc = _nt(qa, kn) * scale + b_ref[half]
            mx = jnp.max(sc, axis=-1, keepdims=True)
            e = jnp.exp(sc - mx)
            den = jnp.sum(e, axis=-1, keepdims=True)
            pm = (e * (1.0 / den)).astype(BF16)
            outs.append(_nn(pm, vb))
            lses.append(mx + jnp.log(den))
        o_ref[...] = jnp.where(lo, outs[0], outs[1]).astype(BF16)
        lse_ref[...] = jnp.where(lo, lses[0], lses[1])

    return pl.pallas_call(
        body, name="attn_fwd", grid=(NH // 2, nq),
        in_specs=[
            pl.BlockSpec((TQ, LANE), lambda j, i: (i, j)),
            pl.BlockSpec((s, LANE), lambda j, i: (0, AW // LANE + j)),
            pl.BlockSpec((s, LANE), lambda j, i: (0, 2 * AW // LANE + j)),
            _layer((1, LANE), l), _layer((1, LANE), l),
            _bias_layer_spec(l),
        ],
        out_specs=[pl.BlockSpec((TQ, LANE), lambda j, i: (i, j)), pl.BlockSpec((TQ, LANE), lambda j, i: (i, j))],
        out_shape=[SDS((s, D), BF16), SDS((s, AW), F32)],
        compiler_params=_cp(),
    )(p, p, p, qg2, kg2, bias)


_C0 = 3 * AW // LANE


def _cp_in_specs(s, l):
    blk = lambda f: pl.BlockSpec((s, LANE), f)
    return [
        blk(lambda g: (0, _C0 + jnp.minimum(g, 1))),
        blk(lambda g: (0, _C0 + 2 + jnp.minimum(g, 1))),
        blk(lambda g: (0, _C0 + 4 + jnp.minimum(g, 1))),
        blk(lambda g: (0, _C0 + 6 + jnp.maximum(g - 2, 0))),
        pl.BlockSpec((None, 3, LANE), lambda g: (l, 0, jnp.minimum(g, 1))),
        pl.BlockSpec((None, None, LANE, LANE), lambda g: (l, jnp.maximum(g - 2, 0), 0, 0)),
        pl.BlockSpec((None, 1, LANE), lambda g: (l, 0, jnp.maximum(g - 2, 0))),
    ]


def _pool_window_sums(u_ref, buf_a, buf_b, jj, s, rt):
    nrt = s // rt
    for t in range(nrt):
        buf_a[pl.ds(PAD + t * rt, rt), :] = u_ref[pl.ds(t * rt, rt), :]

    def stage(src, dst, sh):
        for t in range(nrt):
            r0 = PAD + t * rt
            dst[pl.ds(r0, rt), :] = src[pl.ds(r0, rt), :] + src[pl.ds(r0 - sh, rt), :]

    stage(buf_a, buf_b, 1)
    stage(buf_b, buf_a, 2)

    @pl.when(jj == 1)
    def _():
        stage(buf_a, buf_b, 4)
        stage(buf_b, buf_a, 8)


def _pool_counts(jj, lo, r0, rt):
    w = jnp.where(lo, jnp.where(jj == 0, 2.0, 8.0), jnp.where(jj == 0, 4.0, 16.0))
    pos1 = (lax.broadcasted_iota(jnp.int32, (rt, LANE), 0) + (r0 + 1)).astype(F32)
    return jnp.minimum(pos1, w)


def _conv_pool_fwd(p, mix, conv_w, wbd, pscale, l):
    s = p.shape[0]
    rt = min(256, s)
    nrt = s // rt

    def body(gb_ref, gc_ref, hin_ref, u_ref, cw_ref, wbd_ref, ps_ref, mix_in, o_ref, buf_a, buf_b):
        del mix_in
        g = pl.program_id(0)
        zpad = jnp.zeros((PAD, LANE), F32)
        buf_a[pl.ds(0, PAD), :] = zpad
        buf_b[pl.ds(0, PAD), :] = zpad

        @pl.when(g < 2)
        def _conv():
            for t in range(nrt):
                buf_a[pl.ds(PAD + t * rt, rt), :] = gc_ref[pl.ds(t * rt, rt), :] * hin_ref[pl.ds(t * rt, rt), :]
            w0, w1, w2 = cw_ref[0:1, :], cw_ref[1:2, :], cw_ref[2:3, :]
            for t in range(nrt):
                r0 = PAD + t * rt
                y = w0 * buf_a[pl.ds(r0 - 2, rt), :] + w1 * buf_a[pl.ds(r0 - 1, rt), :] + w2 * buf_a[pl.ds(r0, rt), :]
                o_ref[pl.ds(t * rt, rt), :] = (gb_ref[pl.ds(t * rt, rt), :] * y).astype(BF16)

        @pl.when(g >= 2)
        def _pool():
            jj = g - 2
            lo = _lo_mask()
            _pool_window_sums(u_ref, buf_a, buf_b, jj, s, rt)
            wb = wbd_ref[...]
            for t in range(nrt):
                r0 = PAD + t * rt
                wsum = jnp.where(lo, buf_b[pl.ds(r0, rt), :], buf_a[pl.ds(r0, rt), :])
                m = wsum / _pool_counts(jj, lo, t * rt, rt) - u_ref[pl.ds(t * rt, rt), :]
                o_ref[pl.ds(t * rt, rt), :] = (_nn(m.astype(BF16), wb) * ps_ref[...]).astype(BF16)

    return pl.pallas_call(
        body, name="conv_pool_fwd", grid=(4,),
        in_specs=_cp_in_specs(s, l) + [pl.BlockSpec(memory_space=pl.ANY)],
        out_specs=pl.BlockSpec((s, LANE), lambda g: (0, AW // LANE + g)),
        out_shape=SDS((s, D), BF16),
        scratch_shapes=[pltpu.VMEM((s + 2 * PAD, LANE), F32), pltpu.VMEM((s + 2 * PAD, LANE), F32)],
        input_output_aliases={7: 0},
        compiler_params=_cp(),
    )(p, p, p, p, conv_w, wbd, pscale, mix)


def _mlp_fwd(x, mix, wout, g2, w1_t, w2, l):
    s = x.shape[0]
    t = min(256, s)

    def body(x_ref, mix_ref, wo_ref, g_ref, w1_ref, w2_ref, xm_ref, a_ref, xo_ref):
        xm = x_ref[...] + _nn(mix_ref[...], wo_ref[...])
        xm_ref[...] = xm
        r = lax.rsqrt(jnp.mean(xm * xm, axis=-1, keepdims=True) + EPS)
        h2 = (xm * r * g_ref[...]).astype(BF16)
        a = _nt(h2, w1_ref[...])
        a_ref[...] = a.astype(BF16)
        f = jnp.square(jnp.maximum(a, 0.0)).astype(BF16)
        xo_ref[...] = xm + _nn(f, w2_ref[...])

    row = lambda c: pl.BlockSpec((t, c), lambda i: (i, 0))
    return pl.pallas_call(
        body, name="mlp_fwd", grid=(s // t,),
        in_specs=[row(D), row(D), _layer((D, D), l), _layer((1, D), l), _layer((DFF, D), l), _layer((DFF, D), l)],
        out_specs=[row(D), row(DFF), row(D)],
        out_shape=[SDS((s, D), F32), SDS((s, DFF), BF16), SDS((s, D), F32)],
        compiler_params=_cp(),
    )(x, mix, wout, g2, w1_t, w2)


def _loss_grad(y, target):
    s = y.shape[0]
    t = min(512, s)

    def body(y_ref, t_ref, dy_ref, acc_ref):
        @pl.when(pl.program_id(0) == 0)
        def _():
            acc_ref[...] = jnp.zeros_like(acc_ref)
        e = y_ref[...] - t_ref[...]
        dy_ref[...] = e * (1.0 / D)
        acc_ref[...] += jnp.sum(e * e)

    row = pl.BlockSpec((t, D), lambda i: (i, 0))
    return pl.pallas_call(
        body, name="loss_grad", grid=(s // t,),
        in_specs=[row, row],
        out_specs=[row, pl.BlockSpec((8, LANE), lambda i: (0, 0))],
        out_shape=[SDS((s, D), F32), SDS((8, LANE), F32)],
        compiler_params=_cp(),
    )(y, target)


def _mlp_bwd(dxo, a, xm, g2, w1_t, w2, wout, l):
    s = dxo.shape[0]
    t = min(256, s)

    def body(dxo_ref, a_ref, xm_ref, g_ref, w1_ref, w2_ref, wo_ref,
             dxm_ref, dmix_ref, f_ref, da_ref, h2_ref, dxob_ref, dxmb_ref, dg_ref):
        @pl.when(pl.program_id(0) == 0)
        def _():
            dg_ref[...] = jnp.zeros_like(dg_ref)
        dxo = dxo_ref[...]
        dxob = dxo.astype(BF16)
        dxob_ref[...] = dxob
        ra = jnp.maximum(a_ref[...].astype(F32), 0.0)
        f_ref[...] = jnp.square(ra).astype(BF16)
        dab = (_nt(dxob, w2_ref[...]) * (2.0 * ra)).astype(BF16)
        da_ref[...] = dab
        dh2 = _nn(dab, w1_ref[...])
        xm = xm_ref[...]
        g = g_ref[...]
        r = lax.rsqrt(jnp.mean(xm * xm, axis=-1, keepdims=True) + EPS)
        h2_ref[...] = (xm * r * g).astype(BF16)
        dx_n, dgr = _rms_bwd(dh2, xm, g)
        dg_ref[...] += jnp.sum(dgr, axis=0, keepdims=True)
        dxm = dxo + dx_n
        dxm_ref[...] = dxm
        dxmb = dxm.astype(BF16)
        dxmb_ref[...] = dxmb
        dmix_ref[...] = _nt(dxmb, wo_ref[...])

    row = lambda c: pl.BlockSpec((t, c), lambda i: (i, 0))
    return pl.pallas_call(
        body, name="mlp_bwd", grid=(s // t,),
        in_specs=[row(D), row(DFF), row(D), _layer((1, D), l), _layer((DFF, D), l), _layer((DFF, D), l), _layer((D, D), l)],
        out_specs=[row(D), row(D), row(DFF), row(DFF), row(D), row(D), row(D), pl.BlockSpec((1, D), lambda i: (0, 0))],
        out_shape=[SDS((s, D), F32), SDS((s, D), F32), SDS((s, DFF), BF16), SDS((s, DFF), BF16),
                   SDS((s, D), BF16), SDS((s, D), BF16), SDS((s, D), BF16), SDS((1, D), F32)],
        compiler_params=_cp(),
    )(dxo, a, xm, g2, w1_t, w2, wout)


def _attn_bwd(p, lse, dmix, qg2, kg2, bias, l):
    s = p.shape[0]
    nq = s // TQ
    scale = HD ** -0.5
    rt = min(512, s)

    def body(q_ref, k_ref, v_ref, qg_ref, kg_ref, b_ref, lse_ref, do_ref,
             dq_ref, dk_ref, dv_ref, db_ref, dqg_ref, dkg_ref, dk_acc, dv_acc):
        i = pl.program_id(1)
        ks = pl.multiple_of(jnp.maximum(i * TQ - BAND, 0), TQ)
        lo = _lo_mask()

        @pl.when(i == 0)
        def _():
            dk_acc[...] = jnp.zeros_like(dk_acc)
            dv_acc[...] = jnp.zeros_like(dv_acc)
            dqg_ref[...] = jnp.zeros_like(dqg_ref)
            dkg_ref[...] = jnp.zeros_like(dkg_ref)

        qg, kg = qg_ref[...], kg_ref[...]
        xq, rq = _head_norm(q_ref[...], lo)
        qn = (xq * qg).astype(BF16)
        kn = (_head_norm(k_ref[pl.ds(ks, WIN), :], lo)[0] * kg).astype(BF16)
        vb = v_ref[pl.ds(ks, WIN), :].astype(BF16)
        dob = do_ref[...].astype(BF16)
        lse = lse_ref[...]
        dqn = jnp.zeros((TQ, LANE), F32)
        dkn = jnp.zeros((WIN, LANE), F32)
        dvw = jnp.zeros((WIN, LANE), F32)
        for half in range(2):
            m_ = lo if half == 0 else jnp.logical_not(lo)
            qa = jnp.where(m_, qn, jnp.zeros_like(qn))
            doa = jnp.where(m_, dob, jnp.zeros_like(dob))
            lse_h = lse[:, half * HD:half * HD + 1]
            pm = jnp.exp(_nt(qa, kn) * scale + b_ref[half] - lse_h)
            dp = _nt(doa, vb)
            delta = jnp.sum(pm * dp, axis=-1, keepdims=True)
            ds = pm * (dp - delta)

            @pl.when(i < NVAR)
            def _():
                db_ref[half] = ds

            @pl.when(i >= NVAR)
            def _():
                db_ref[half] += ds

            dsb = ds.astype(BF16)
            dqn = jnp.where(m_, _nn(dsb, kn), dqn)
            dkn = dkn + _tn(dsb, qa)
            dvw = dvw + _tn(pm.astype(BF16), doa)
        dq, dqg_rows = _head_norm_bwd(dqn * scale, xq, rq, qg, lo)
        dq_ref[...] = dq.astype(BF16)
        dqg_ref[...] += jnp.sum(dqg_rows, axis=0, keepdims=True)
        dk_acc[pl.ds(ks, WIN), :] += dkn * scale
        dv_acc[pl.ds(ks, WIN), :] += dvw

        @pl.when(i == nq - 1)
        def _():
            dkg = jnp.zeros((1, LANE), F32)
            for t in range(s // rt):
                rows = pl.ds(t * rt, rt)
                xk, rk = _head_norm(k_ref[rows, :], lo)
                dk, dkg_rows = _head_norm_bwd(dk_acc[rows, :], xk, rk, kg, lo)
                dk_ref[rows, :] = dk.astype(BF16)
                dv_ref[rows, :] = dv_acc[rows, :].astype(BF16)
                dkg = dkg + jnp.sum(dkg_rows, axis=0, keepdims=True)
            dkg_ref[...] = dkg

    tile = pl.BlockSpec((TQ, LANE), lambda j, i: (i, j))
    gain = pl.BlockSpec((None, 1, LANE), lambda j, i: (j, 0, 0))
    return pl.pallas_call(
        body, name="attn_bwd", grid=(NH // 2, nq),
        in_specs=[
            tile,
            pl.BlockSpec((s, LANE), lambda j, i: (0, AW // LANE + j)),
            pl.BlockSpec((s, LANE), lambda j, i: (0, 2 * AW // LANE + j)),
            _layer((1, LANE), l), _layer((1, LANE), l),
            _bias_layer_spec(l), tile, tile,
        ],
        out_specs=[tile, pl.BlockSpec((s, LANE), lambda j, i: (0, j)), pl.BlockSpec((s, LANE), lambda j, i: (0, j)),
                   _bias_spec(), gain, gain],
        out_shape=[SDS((s, AW), BF16), SDS((s, AW), BF16), SDS((s, AW), BF16),
                   SDS((NVAR, NH, TQ, WIN), F32), SDS((NH // 2, 1, LANE), F32), SDS((NH // 2, 1, LANE), F32)],
        scratch_shapes=[pltpu.VMEM((s, LANE), F32), pltpu.VMEM((s, LANE), F32)],
        compiler_params=_cp(),
    )(p, p, p, qg2, kg2, bias, lse, dmix)


def _conv_pool_bwd(p, dmix, conv_w, wbd, pscale, l):
    s = p.shape[0]
    rt = min(256, s)
    nrt = s // rt

    def body(gb_ref, gc_ref, hin_ref, u_ref, cw_ref, wbd_ref, ps_ref, dy_ref,
             dgb_ref, dgc_ref, dhin_ref, du_ref, dcw_ref, dwbd_ref, dps_ref, buf_a, buf_b, buf_c, buf_d):
        g = pl.program_id(0)
        zpad = jnp.zeros((PAD, LANE), F32)
        for buf in (buf_a, buf_b, buf_c):
            buf[pl.ds(0, PAD), :] = zpad
            buf[pl.ds(PAD + s, PAD), :] = zpad

        @pl.when(g < 2)
        def _conv():
            for t in range(nrt):
                rows = pl.ds(t * rt, rt)
                buf_a[pl.ds(PAD + t * rt, rt), :] = gc_ref[rows, :] * hin_ref[rows, :]
                buf_b[pl.ds(PAD + t * rt, rt), :] = dy_ref[rows, :] * gb_ref[rows, :]
            w0, w1, w2 = cw_ref[0:1, :], cw_ref[1:2, :], cw_ref[2:3, :]
            d0 = jnp.zeros((1, LANE), F32)
            d1 = jnp.zeros((1, LANE), F32)
            d2 = jnp.zeros((1, LANE), F32)
            for t in range(nrt):
                rows = pl.ds(t * rt, rt)
                r0 = PAD + t * rt
                z2, z1, z0 = buf_a[pl.ds(r0 - 2, rt), :], buf_a[pl.ds(r0 - 1, rt), :], buf_a[pl.ds(r0, rt), :]
                y = w0 * z2 + w1 * z1 + w2 * z0
                dgb_ref[rows, :] = (dy_ref[rows, :] * y).astype(BF16)
                e0 = buf_b[pl.ds(r0, rt), :]
                d0 = d0 + jnp.sum(e0 * z2, axis=0, keepdims=True)
                d1 = d1 + jnp.sum(e0 * z1, axis=0, keepdims=True)
                d2 = d2 + jnp.sum(e0 * z0, axis=0, keepdims=True)
                dz = w2 * e0 + w1 * buf_b[pl.ds(r0 + 1, rt), :] + w0 * buf_b[pl.ds(r0 + 2, rt), :]
                dgc_ref[rows, :] = (dz * hin_ref[rows, :]).astype(BF16)
                dhin_ref[rows, :] = (dz * gc_ref[rows, :]).astype(BF16)
            dcw_ref[0:1, :] = d0
            dcw_ref[1:2, :] = d1
            dcw_ref[2:3, :] = d2

        @pl.when(g >= 2)
        def _pool():
            jj = g - 2
            lo = _lo_mask()
            _pool_window_sums(u_ref, buf_a, buf_b, jj, s, rt)
            wb = wbd_ref[...]
            ps = ps_ref[...]
            dps = jnp.zeros((1, LANE), F32)
            dwb = jnp.zeros((LANE, LANE), F32)
            for t in range(nrt):
                rows = pl.ds(t * rt, rt)
                r0 = PAD + t * rt
                cnt = _pool_counts(jj, lo, t * rt, rt)
                wsum = jnp.where(lo, buf_b[pl.ds(r0, rt), :], buf_a[pl.ds(r0, rt), :])
                mb = (wsum / cnt - u_ref[rows, :]).astype(BF16)
                dy = dy_ref[rows, :]
                dps = dps + jnp.sum(dy * _nn(mb, wb), axis=0, keepdims=True)
                dmp = (dy * ps).astype(BF16)
                dwb = dwb + _tn(mb, dmp)
                dm = _nt(dmp, wb)
                buf_d[rows, :] = dm
                buf_c[pl.ds(r0, rt), :] = dm / cnt
            dps_ref[...] = dps
            dwbd_ref[...] = dwb

            def stage(src, dst, sh):
                for t in range(nrt):
                    r0 = PAD + t * rt
                    dst[pl.ds(r0, rt), :] = src[pl.ds(r0, rt), :] + src[pl.ds(r0 + sh, rt), :]

            def finish(first, second):
                for t in range(nrt):
                    rows = pl.ds(t * rt, rt)
                    r0 = PAD + t * rt
                    fw = jnp.where(lo, first[pl.ds(r0, rt), :], second[pl.ds(r0, rt), :])
                    du_ref[rows, :] = (fw - buf_d[rows, :]).astype(BF16)

            stage(buf_c, buf_a, 1)
            stage(buf_a, buf_b, 2)

            @pl.when(jj == 0)
            def _():
                finish(buf_a, buf_b)

            @pl.when(jj == 1)
            def _():
                stage(buf_b, buf_c, 4)
                stage(buf_c, buf_a, 8)
                finish(buf_c, buf_a)

    cblk = pl.BlockSpec((s, LANE), lambda g: (0, jnp.minimum(g, 1)))
    pblk = pl.BlockSpec((s, LANE), lambda g: (0, jnp.maximum(g - 2, 0)))
    padded = pltpu.VMEM((s + 2 * PAD, LANE), F32)
    return pl.pallas_call(
        body, name="conv_pool_bwd", grid=(4,),
        in_specs=_cp_in_specs(s, l) + [pl.BlockSpec((s, LANE), lambda g: (0, AW // LANE + g))],
        out_specs=[cblk, cblk, cblk, pblk,
                   pl.BlockSpec((3, LANE), lambda g: (0, jnp.minimum(g, 1))),
                   pl.BlockSpec((None, LANE, LANE), lambda g: (jnp.maximum(g - 2, 0), 0, 0)),
                   pl.BlockSpec((1, LANE), lambda g: (0, jnp.maximum(g - 2, 0)))],
        out_shape=[SDS((s, CW), BF16), SDS((s, CW), BF16), SDS((s, CW), BF16), SDS((s, PWD), BF16),
                   SDS((3, CW), F32), SDS((2, LANE, LANE), F32), SDS((1, PWD), F32)],
        scratch_shapes=[padded, padded, padded, pltpu.VMEM((s, LANE), F32)],
        compiler_params=_cp(),
    )(p, p, p, p, conv_w, wbd, pscale, dmix)


def _in_proj_bwd(parts, x, dxm, g1, win_t, l):
    s = x.shape[0]
    t = min(256, s)
    widths = [a.shape[1] for a in parts]
    offs = [int(o) for o in np.cumsum([0] + widths[:-1])]
    n = len(parts)

    def body(*refs):
        part_refs = refs[:n]
        x_ref, dxm_ref, g_ref, w_ref, dx_ref, dp_ref, dg_ref = refs[n:]

        @pl.when(pl.program_id(0) == 0)
        def _():
            dg_ref[...] = jnp.zeros_like(dg_ref)
        for r, o, w in zip(part_refs, offs, widths):
            dp_ref[:, o:o + w] = r[...]
        dh = _nn(dp_ref[...], w_ref[...])
        dx_n, dgr = _rms_bwd(dh, x_ref[...], g_ref[...])
        dg_ref[...] += jnp.sum(dgr, axis=0, keepdims=True)
        dx_ref[...] = dxm_ref[...] + dx_n

    row = lambda c: pl.BlockSpec((t, c), lambda i: (i, 0))
    return pl.pallas_call(
        body, name="in_proj_bwd", grid=(s // t,),
        in_specs=[row(w) for w in widths] + [row(D), row(D), _layer((1, D), l), _layer((DIN, D), l)],
        out_specs=[row(D), row(DIN), pl.BlockSpec((1, D), lambda i: (0, 0))],
        out_shape=[SDS((s, D), F32), SDS((s, DIN), BF16), SDS((1, D), F32)],
        compiler_params=_cp(),
    )(*parts, x, dxm, g1, win_t)


def _wgrad(a, b, stacked, layer, tag):
    s, m = a.shape
    mb = 512
    t = min(1024, s)
    nt = s // t

    def body(*refs):
        a_ref, b_ref = refs[:2]
        o_ref, acc = refs[-2:]

        @pl.when(pl.program_id(1) == 0)
        def _():
            acc[...] = jnp.zeros_like(acc)
        acc[...] += _tn(a_ref[...], b_ref[...])

        @pl.when(pl.program_id(1) == nt - 1)
        def _():
            o_ref[...] = acc[...].astype(BF16)

    in_specs = [pl.BlockSpec((t, mb), lambda mi, ti: (ti, mi)), pl.BlockSpec((t, D), lambda mi, ti: (ti, 0))]
    args = [a, b]
    aliases = {}
    if stacked is not None:
        in_specs.append(pl.BlockSpec(memory_space=pl.ANY))
        args.append(stacked)
        aliases = {2: 0}
    return pl.pallas_call(
        body, name=f"wgrad_{tag}_l{layer}", grid=(m // mb, nt),
        in_specs=in_specs,
        out_specs=pl.BlockSpec((None, mb, D), lambda mi, ti: (layer, mi, 0)),
        out_shape=SDS((L, m, D), BF16),
        scratch_shapes=[pltpu.VMEM((mb, D), F32)],
        input_output_aliases=aliases,
        compiler_params=_cp(),
    )(*args)


def _bias_tables(gvec):
    def body(g_ref, o_ref):
        qc = lax.broadcasted_iota(jnp.int32, (TQ, WIN), 0) // CHUNK
        kc = lax.broadcasted_iota(jnp.int32, (TQ, WIN), 1) // CHUNK
        for var in range(NVAR):
            vec = jnp.broadcast_to(g_ref[:, var * TQ:var * TQ + NTOE], (TQ, NTOE))
            toe = pltpu.roll(vec, NTOE - TQ + 1, 1, stride=1, stride_axis=0)[:, :WIN]
            rel = (BAND - var * TQ) // CHUNK + qc - kc
            o_ref[var] = jnp.where((rel >= 0) & (rel <= N_PREV), toe, NEG)

    return pl.pallas_call(
        body, name="bias_tables", grid=(L, NH),
        in_specs=[pl.BlockSpec((None, None, 1, NG), lambda l, h: (l, h, 0, 0))],
        out_specs=pl.BlockSpec((None, NVAR, None, TQ, WIN), lambda l, h: (l, 0, h, 0, 0)),
        out_shape=SDS((L, NVAR, NH, TQ, WIN), F32),
        compiler_params=_cp(),
    )(gvec)


def _bias_tables_grad(dbias, l):
    nb = NTOE // LANE
    wb = WIN // LANE

    def body(d_ref, o_ref):
        ii = lax.broadcasted_iota(jnp.int32, (LANE, LANE), 0)
        jj = lax.broadcasted_iota(jnp.int32, (LANE, LANE), 1)
        flip = jnp.where(ii + jj == LANE - 1, 1.0, 0.0).astype(BF16)
        o_ref[...] = jnp.zeros_like(o_ref)
        for var in range(NVAR):
            blocks = []
            for b in range(nb):
                src = nb - 1 - b
                if src >= wb:
                    blocks.append(jnp.zeros((TQ, LANE), F32))
                    continue
                xv = d_ref[var, :, src * LANE:(src + 1) * LANE]
                hi = xv.astype(BF16)
                lo = (xv - hi.astype(F32)).astype(BF16)
                blocks.append(_nn(hi, flip) + _nn(lo, flip))
            rev = jnp.concatenate(blocks, axis=1)
            skew = pltpu.roll(rev, NTOE - TQ + 1, 1, stride=1, stride_axis=0)
            off = NG - NTOE - var * TQ
            o_ref[:, off:off + NTOE] += jnp.sum(skew, axis=0, keepdims=True)

    return pl.pallas_call(
        body, name=f"bias_tables_grad_l{l}", grid=(NH,),
        in_specs=[pl.BlockSpec((NVAR, None, TQ, WIN), lambda h: (0, h, 0, 0))],
        out_specs=pl.BlockSpec((None, 1, NG), lambda h: (h, 0, 0)),
        out_shape=SDS((NH, 1, NG), F32),
        compiler_params=_cp(),
    )(dbias)


_MASKS = [(mx, my, mc) for mx in (0, 1) for my in (0, 1) for mc in (0, 1)][1:]


def _position():
    return lax.axis_index("x"), lax.axis_index("y"), lax.axis_index("c")


def _peer(pos, mask):
    return tuple(1 - a if f else a for a, f in zip(pos, mask))


def _index(pos):
    return 4 * pos[0] + 2 * pos[1] + pos[2]


def _gather_weights(shards):
    nm = len(shards)

    def body(*refs):
        src = refs[:nm]
        dst = refs[nm:2 * nm]
        send_sems, recv_sems, local_sems = refs[2 * nm:]
        me = _position()

        def rows(k, pos):
            r = SHARD_ROWS[k]
            return dst[k].at[:, pl.ds(_index(pos) * r, r), :]

        local = [pltpu.make_async_copy(src[k], rows(k, me), local_sems.at[k]) for k in range(nm)]
        for cp in local:
            cp.start()
        sends = []
        for pi, mask in enumerate(_MASKS):
            for k in range(nm):
                cp = pltpu.make_async_remote_copy(
                    src_ref=src[k], dst_ref=rows(k, me), send_sem=send_sems.at[pi, k], recv_sem=recv_sems.at[pi, k],
                    device_id=_peer(me, mask), device_id_type=MESH_ID)
                cp.start()
                sends.append(cp)
        for pi, mask in enumerate(_MASKS):
            for k in range(nm):
                pltpu.make_async_remote_copy(
                    src_ref=src[k], dst_ref=rows(k, _peer(me, mask)), send_sem=send_sems.at[pi, k],
                    recv_sem=recv_sems.at[pi, k], device_id=_peer(me, mask), device_id_type=MESH_ID).wait_recv()
        for cp in sends:
            cp.wait_send()
        for cp in local:
            cp.wait()

    hbm = pl.BlockSpec(memory_space=pl.ANY)
    return pl.pallas_call(
        body, name="gather_weights",
        in_specs=[hbm] * nm, out_specs=[hbm] * nm,
        out_shape=[SDS((L, NDEV * a.shape[1], D), a.dtype) for a in shards],
        scratch_shapes=[pltpu.SemaphoreType.DMA((len(_MASKS), nm)), pltpu.SemaphoreType.DMA((len(_MASKS), nm)),
                        pltpu.SemaphoreType.DMA((nm,))],
    )(*shards)


def _scatter_grads(grads):
    nm = len(grads)

    def body(*refs):
        src = refs[:nm]
        dst = refs[nm:2 * nm]
        send_sems, recv_sems, local_sems = refs[2 * nm:]
        me = _position()

        def rows(k, pos):
            r = SHARD_ROWS[k]
            return src[k].at[:, pl.ds(_index(pos) * r, r), :]

        local = [pltpu.make_async_copy(rows(k, me), dst[k].at[_index(me)], local_sems.at[k]) for k in range(nm)]
        for cp in local:
            cp.start()
        sends = []
        for pi, mask in enumerate(_MASKS):
            peer = _peer(me, mask)
            for k in range(nm):
                cp = pltpu.make_async_remote_copy(
                    src_ref=rows(k, peer), dst_ref=dst[k].at[_index(me)], send_sem=send_sems.at[pi, k],
                    recv_sem=recv_sems.at[pi, k], device_id=peer, device_id_type=MESH_ID)
                cp.start()
                sends.append(cp)
        for pi, mask in enumerate(_MASKS):
            peer = _peer(me, mask)
            for k in range(nm):
                pltpu.make_async_remote_copy(
                    src_ref=rows(k, me), dst_ref=dst[k].at[_index(peer)], send_sem=send_sems.at[pi, k],
                    recv_sem=recv_sems.at[pi, k], device_id=peer, device_id_type=MESH_ID).wait_recv()
        for cp in sends:
            cp.wait_send()
        for cp in local:
            cp.wait()

    hbm = pl.BlockSpec(memory_space=pl.ANY)
    return pl.pallas_call(
        body, name="scatter_grads",
        in_specs=[hbm] * nm, out_specs=[hbm] * nm,
        out_shape=[SDS((NDEV, L, a.shape[1] // NDEV, D), a.dtype) for a in grads],
        scratch_shapes=[pltpu.SemaphoreType.DMA((len(_MASKS), nm)), pltpu.SemaphoreType.DMA((len(_MASKS), nm)),
                        pltpu.SemaphoreType.DMA((nm,))],
    )(*grads)


def _sum_slots(slots):
    _, _, r, _ = slots.shape
    rt = 64

    def body(in_ref, o_ref):
        acc = in_ref[0].astype(F32)
        for d in range(1, NDEV):
            acc = acc + in_ref[d].astype(F32)
        o_ref[...] = acc

    return pl.pallas_call(
        body, name=f"sum_slots_r{r}", grid=(L, r // rt),
        in_specs=[pl.BlockSpec((NDEV, None, rt, D), lambda l, i: (0, l, i, 0))],
        out_specs=pl.BlockSpec((None, rt, D), lambda l, i: (l, i, 0)),
        out_shape=SDS((L, r, D), F32),
        compiler_params=_cp(),
    )(slots)


def _exchange_small(v, reduce):
    rows = v.shape[0]

    def body(v_ref, o_ref, *scratch):
        if reduce:
            slots, send_sems, recv_sems = scratch
        else:
            slots = o_ref
            send_sems, recv_sems = scratch
        me = _position()
        slots[_index(me)] = v_ref[...]
        sends = []
        for pi, mask in enumerate(_MASKS):
            cp = pltpu.make_async_remote_copy(
                src_ref=v_ref, dst_ref=slots.at[_index(me)], send_sem=send_sems.at[pi], recv_sem=recv_sems.at[pi],
                device_id=_peer(me, mask), device_id_type=MESH_ID)
            cp.start()
            sends.append(cp)
        for pi, mask in enumerate(_MASKS):
            peer = _peer(me, mask)
            pltpu.make_async_remote_copy(
                src_ref=v_ref, dst_ref=slots.at[_index(peer)], send_sem=send_sems.at[pi], recv_sem=recv_sems.at[pi],
                device_id=peer, device_id_type=MESH_ID).wait_recv()
        for cp in sends:
            cp.wait_send()
        if reduce:
            acc = slots[0]
            for d in range(1, NDEV):
                acc = acc + slots[d]
            o_ref[...] = acc

    vm = pl.BlockSpec(memory_space=pltpu.VMEM)
    sems = [pltpu.SemaphoreType.DMA((len(_MASKS),)), pltpu.SemaphoreType.DMA((len(_MASKS),))]
    return pl.pallas_call(
        body, name="reduce_small" if reduce else "gather_small",
        in_specs=[vm], out_specs=vm,
        out_shape=SDS((rows, LANE) if reduce else (NDEV, rows, LANE), F32),
        scratch_shapes=([pltpu.VMEM((NDEV, rows, LANE), F32)] if reduce else []) + sems,
        compiler_params=_cp(),
    )(v)


def _adamw(w, g, m, v):
    rows, cols = w.shape
    t = rows
    for cand in (512, 256, 128, 64, 32, 16, 8):
        if rows % cand == 0:
            t = cand
            break

    def body(w_ref, g_ref, m_ref, v_ref, d_ref, nm_ref, nv_ref):
        gv = g_ref[...]
        mn = B1 * m_ref[...] + (1.0 - B1) * gv
        vn = B2 * v_ref[...] + (1.0 - B2) * jnp.square(gv)
        nm_ref[...] = mn
        nv_ref[...] = vn
        m_hat = mn / (1.0 - B1 ** STEP)
        v_hat = vn / (1.0 - B2 ** STEP)
        d_ref[...] = -LR * (m_hat / (jnp.sqrt(v_hat) + AEPS) + WD * w_ref[...])

    blk = pl.BlockSpec((t, cols), lambda i: (i, 0))
    return pl.pallas_call(
        body, name=f"adamw_{rows}x{cols}", grid=(rows // t,),
        in_specs=[blk] * 4, out_specs=[blk] * 3,
        out_shape=[SDS((rows, cols), F32)] * 3,
        compiler_params=_cp(),
    )(w, g, m, v)


_DIST0 = BAND + TQ - 1
_N_FAR = _DIST0 - REL_CLIP + 1
_N_NEAR = NG - _N_FAR - (2 * REL_CLIP - 1)


def _bias_vector(rel_bias):
    far = jnp.broadcast_to(rel_bias[..., -1:], (L, NH, _N_FAR))
    near = jnp.broadcast_to(rel_bias[..., :1], (L, NH, _N_NEAR))
    return jnp.concatenate([far, lax.rev(rel_bias[..., 1:-1], (2,)), near], axis=2)[:, :, None, :]


def _bias_vector_grad(dgr):
    first = jnp.sum(dgr[..., :_N_NEAR], axis=-1, keepdims=True)
    last = jnp.sum(dgr[..., NG - _N_FAR:], axis=-1, keepdims=True)
    return jnp.concatenate([first, dgr[..., _N_NEAR:NG - _N_FAR], last], axis=-1)


def _pool_blockdiag(pool_w):
    eye = jnp.eye(2, dtype=F32)
    pw = pool_w.reshape(L, 2, 2, HD, HD)
    return jnp.einsum("ljaik,ab->ljaibk", pw, eye).reshape(L, 2, LANE, LANE)


def _pool_blockdiag_grad(dwbd):
    d = dwbd.reshape(L, 2, 2, HD, 2, HD)
    return jnp.stack([d[:, :, 0, :, 0, :], d[:, :, 1, :, 1, :]], axis=2).reshape(L, 4, HD, HD)


def _pack(arrays, rows):
    flat = jnp.concatenate([a.reshape(-1).astype(F32) for a in arrays])
    return jnp.pad(flat, (0, rows * LANE - flat.shape[0])).reshape(rows, LANE)


def _unpack(packed, shapes):
    flat = packed.reshape(-1)
    out, o = [], 0
    for shp in shapes:
        n = int(np.prod(shp))
        out.append(flat[o:o + n].reshape(shp))
        o += n
    return out


def _rows_for(shapes):
    n = sum(int(np.prod(s)) for s in shapes)
    return -(-n // (8 * LANE)) * 8


def _local_grads(x, target, weights):
    g1, win_t, qg, kg, rb, cw, pw, ps, wout, g2, w1_t, w2 = weights
    g1 = g1.reshape(L, 1, D)
    g2 = g2.reshape(L, 1, D)
    qg2 = jnp.tile(qg, (1, 2)).reshape(L, 1, LANE)
    kg2 = jnp.tile(kg, (1, 2)).reshape(L, 1, LANE)
    ps3 = ps.reshape(L, 1, PWD)
    bias = _bias_tables(_bias_vector(rb))
    wbd = _pool_blockdiag(pw).astype(BF16)
    saved = []
    h = x
    for l in range(L):
        p, h_b = _in_proj(h, g1, win_t, l)
        mix, lse = _attn_fwd(p, qg2, kg2, bias, l)
        mix = _conv_pool_fwd(p, mix, cw, wbd, ps3, l)
        xm, a, xo = _mlp_fwd(h, mix, wout, g2, w1_t, w2, l)
        saved.append((h, h_b, p, mix, lse, xm, a))
        h = xo
    dx, sq = _loss_grad(h, target)
    stacks = [None] * 4
    per_layer = [None] * L
    for l in reversed(range(L)):
        x_in, h_b, p, mix, lse, xm, a = saved[l]
        dxm, dmix, f_b, da_b, h2_b, dxo_b, dxm_b, dg2 = _mlp_bwd(dx, a, xm, g2, w1_t, w2, wout, l)
        dq, dk, dv, dbias, dqg, dkg = _attn_bwd(p, lse, dmix, qg2, kg2, bias, l)
        dgb, dgc, dhin, du, dcw, dwbd, dps = _conv_pool_bwd(p, dmix, cw, wbd, ps3, l)
        dx, dp_b, dg1 = _in_proj_bwd([dq, dk, dv, dgb, dgc, dhin, du], x_in, dxm, g1, win_t, l)
        stacks[0] = _wgrad(dp_b, h_b, stacks[0], l, "w_in")
        stacks[1] = _wgrad(mix, dxm_b, stacks[1], l, "w_out")
        stacks[2] = _wgrad(da_b, h2_b, stacks[2], l, "w_mlp1")
        stacks[3] = _wgrad(f_b, dxo_b, stacks[3], l, "w_mlp2")
        per_layer[l] = (dg1, dg2, dqg, dkg, _bias_tables_grad(dbias, l), dcw, dwbd, dps)
    st = [jnp.stack([per_layer[l][k] for l in range(L)]) for k in range(8)]
    small = dict(
        g1=st[0].reshape(L, D), g2=st[1].reshape(L, D),
        qg=st[2].reshape(L, NH, HD).sum(1), kg=st[3].reshape(L, NH, HD).sum(1),
        rb=_bias_vector_grad(st[4].reshape(L, NH, NG)), cw=st[5], pw=_pool_blockdiag_grad(st[6]),
        ps=st[7].reshape(L, PWD))
    return sq, dx, stacks, small


def kernel(x, norm1_g, w_in, q_norm_g, k_norm_g, rel_bias, conv_w, pool_w, pool_scale, w_out, norm2_g, w_mlp1, w_mlp2, loss_target, m_norm1_g, m_w_in, m_q_norm_g, m_k_norm_g, m_rel_bias, m_conv_w, m_pool_w, m_pool_scale, m_w_out, m_norm2_g, m_w_mlp1, m_w_mlp2, v_norm1_g, v_w_in, v_q_norm_g, v_k_norm_g, v_rel_bias, v_conv_w, v_pool_w, v_pool_scale, v_w_out, v_norm2_g, v_w_mlp1, v_w_mlp2):
    me = _index(_position())
    cshard = CW // NDEV

    shards = [jnp.swapaxes(w_in, 1, 2).astype(BF16), w_out.astype(BF16),
              jnp.swapaxes(w_mlp1, 1, 2).astype(BF16), w_mlp2.astype(BF16)]
    win_t, wout, w1_t, w2 = _gather_weights(shards)
    cw_all = _exchange_small(_pack([conv_w], 8), reduce=False)
    cw_full = jnp.concatenate(
        [cw_all[d].reshape(-1)[:L * 3 * cshard].reshape(L, 3, cshard) for d in range(NDEV)], axis=2)

    weights = (norm1_g, win_t, q_norm_g, k_norm_g, rel_bias, cw_full, pool_w, pool_scale, wout, norm2_g, w1_t, w2)
    sq, grad_x, stacks, small = _local_grads(x[0], loss_target[0], weights)

    slots = _scatter_grads(stacks)
    g_win_t, g_wout, g_w1_t, g_w2 = [_sum_slots(sl) for sl in slots]
    g_w_in = jnp.swapaxes(g_win_t, 1, 2)
    g_w_mlp1 = jnp.swapaxes(g_w1_t, 1, 2)

    names = ("g1", "qg", "kg", "rb", "cw", "pw", "ps", "g2")
    gshapes = [(L, D), (L, HD), (L, HD), (L, NH, 2 * REL_CLIP + 1), (L, 3, CW), (L, 4, HD, HD), (L, PWD), (L, D)]
    garrs = [small[n] for n in names]
    rows = _rows_for(gshapes + [(1,)])
    total = _exchange_small(_pack(garrs + [sq[0, :1]], rows), reduce=True)
    g_g1, g_qg, g_kg, g_rb, g_cw_full, g_pw, g_ps, g_g2, sq_sum = _unpack(total, gshapes + [(1,)])
    loss = (0.5 / D) * sq_sum[0]
    g_cw = lax.dynamic_slice_in_dim(g_cw_full, me * cshard, cshard, axis=2)

    def big(w, g, m, v):
        shp = w.shape
        r = lambda a: a.reshape(-1, shp[-1])
        return [o.reshape(shp) for o in _adamw(r(w), r(g), r(m), r(v))]

    up_in = big(w_in, g_w_in, m_w_in, v_w_in)
    up_out = big(w_out, g_wout, m_w_out, v_w_out)
    up_1 = big(w_mlp1, g_w_mlp1, m_w_mlp1, v_w_mlp1)
    up_2 = big(w_mlp2, g_w2, m_w_mlp2, v_w_mlp2)

    sw = [norm1_g, q_norm_g, k_norm_g, rel_bias, conv_w, pool_w, pool_scale, norm2_g]
    sg = [g_g1, g_qg, g_kg, g_rb, g_cw, g_pw, g_ps, g_g2]
    sm = [m_norm1_g, m_q_norm_g, m_k_norm_g, m_rel_bias, m_conv_w, m_pool_w, m_pool_scale, m_norm2_g]
    sv = [v_norm1_g, v_q_norm_g, v_k_norm_g, v_rel_bias, v_conv_w, v_pool_w, v_pool_scale, v_norm2_g]
    sshapes = [a.shape for a in sw]
    srows = _rows_for(sshapes)
    ups = _adamw(_pack(sw, srows), _pack(sg, srows), _pack(sm, srows), _pack(sv, srows))
    s_delta, s_m, s_v = [_unpack(u, sshapes) for u in ups]

    def order(small_list, in_, out_, m1, m2):
        g1_, qg_, kg_, rb_, cw_, pw_, ps_, g2_ = small_list
        return [g1_, in_, qg_, kg_, rb_, cw_, pw_, ps_, out_, g2_, m1, m2]

    grads = order(sg, g_w_in, g_wout, g_w_mlp1, g_w2)
    deltas = order(s_delta, up_in[0], up_out[0], up_1[0], up_2[0])
    new_m = order(s_m, up_in[1], up_out[1], up_1[1], up_2[1])
    new_v = order(s_v, up_in[2], up_out[2], up_1[2], up_2[2])
    return (loss, grad_x[None], *grads, *deltas, *new_m, *new_v)
```

```python
import numpy as np
import jax
import jax.numpy as jnp
from jax import lax
from jax.experimental import pallas as pl
from jax.experimental.pallas import tpu as pltpu

F32 = jnp.float32
BF16 = jnp.bfloat16
SDS = jax.ShapeDtypeStruct
MESH_ID = pl.DeviceIdType.MESH

D = 1024
L = 4
CHUNK = 64
N_PREV = 8
HD = 64
NH = 8
AW = 512
CW = 256
PWD = 256
DIN = 2560
DFF = 4096
EPS = 1e-6
NEG = -1e30
REL_CLIP = 128
POOL_WINDOWS = (2, 4, 8, 16)
LR, B1, B2, AEPS, WD, STEP = 0.001, 0.9, 0.999, 1e-08, 0.01, 10

NDEV = 8
LANE = 128
BAND = N_PREV * CHUNK
TQ = 256
WIN = TQ + BAND
NVAR = BAND // TQ + 1
NTOE = -(-(WIN + TQ - 1) // LANE) * LANE
NG = (NVAR - 1) * TQ + NTOE
PAD = 16
VMEM_LIMIT = 56 * 1024 * 1024
SHARD_ROWS = (DIN // NDEV, D // NDEV, DFF // NDEV, DFF // NDEV)


def _cp(**kw):
    return pltpu.CompilerParams(vmem_limit_bytes=VMEM_LIMIT, **kw)


def _nn(a, b):
    return jnp.dot(a, b, preferred_element_type=F32)


def _nt(a, b):
    return lax.dot_general(a, b, (((1,), (1,)), ((), ())), preferred_element_type=F32)


def _tn(a, b):
    return lax.dot_general(a, b, (((0,), (0,)), ((), ())), preferred_element_type=F32)


def _const(shape):
    n = len(shape)
    return pl.BlockSpec(shape, lambda *_: (0,) * n, pipeline_mode=pl.Buffered(1))


def _layer(shape, l):
    n = len(shape)
    return pl.BlockSpec((None,) + tuple(shape), lambda *_: (l,) + (0,) * n, pipeline_mode=pl.Buffered(1))


def _lo_mask():
    return lax.broadcasted_iota(jnp.int32, (1, LANE), 1) < HD


def _half_sum(t, lo):
    s_lo = jnp.sum(jnp.where(lo, t, 0.0), axis=-1, keepdims=True)
    s_hi = jnp.sum(jnp.where(lo, 0.0, t), axis=-1, keepdims=True)
    return jnp.where(lo, s_lo, s_hi)


def _head_norm(x, lo):
    r = lax.rsqrt(_half_sum(x * x, lo) * (1.0 / HD) + EPS)
    return x * r, r


def _head_norm_bwd(dy, xn, r, g, lo):
    dxn = dy * g
    mu = _half_sum(dxn * xn, lo) * (1.0 / HD)
    return r * (dxn - xn * mu), dy * xn


def _rms_bwd(dy, x, g):
    r = lax.rsqrt(jnp.mean(x * x, axis=-1, keepdims=True) + EPS)
    xn = x * r
    dxn = dy * g
    mu = jnp.mean(dxn * xn, axis=-1, keepdims=True)
    return r * (dxn - xn * mu), dy * xn


def _in_proj(x, g1, win_t, l, xchg=None):
    s = x.shape[0]
    t = min(512, s)

    def body(x_ref, g_ref, w_ref, p_ref, h_ref):
        xv = x_ref[...]
        r = lax.rsqrt(jnp.mean(xv * xv, axis=-1, keepdims=True) + EPS)
        h = (xv * r * g_ref[...]).astype(BF16)
        h_ref[...] = h
        p_ref[...] = _nt(h, w_ref[...])

    return _hosted_call(
        body, xchg, name="in_proj", grid=(s // t,),
        in_specs=[pl.BlockSpec((t, D), lambda i: (i, 0)), _layer((1, D), l), _layer((DIN, D), l)],
        out_specs=[pl.BlockSpec((t, DIN), lambda i: (i, 0)), pl.BlockSpec((t, D), lambda i: (i, 0))],
        out_shape=[SDS((s, DIN), F32), SDS((s, D), BF16)], args=[x, g1, win_t])


def _bias_spec():
    return pl.BlockSpec((None, 2, TQ, WIN), lambda j, i: (jnp.maximum(NVAR - 1 - i, 0), j, 0, 0))


def _bias_layer_spec(l):
    return pl.BlockSpec((None, None, 2, TQ, WIN), lambda j, i: (l, jnp.maximum(NVAR - 1 - i, 0), j, 0, 0))


def _attn_fwd(p, qg2, kg2, bias, l, xchg=None):
    s = p.shape[0]
    nq = s // TQ
    scale = HD ** -0.5

    def body(q_ref, k_ref, v_ref, qg_ref, kg_ref, b_ref, o_ref, lse_ref):
        i = pl.program_id(1)
        ks = pl.multiple_of(jnp.maximum(i * TQ - BAND, 0), TQ)
        lo = _lo_mask()
        qn = (_head_norm(q_ref[...], lo)[0] * qg_ref[...]).astype(BF16)
        kn = (_head_norm(k_ref[pl.ds(ks, WIN), :], lo)[0] * kg_ref[...]).astype(BF16)
        vb = v_ref[pl.ds(ks, WIN), :].astype(BF16)
        outs, lses = [], []
        for half in range(2):
            m_ = lo if half == 0 else jnp.logical_not(lo)
            qa = jnp.where(m_, qn, jnp.zeros_like(qn))
            sc = _nt(qa, kn) * scale + b_ref[half]
            mx = jnp.max(sc, axis=-1, keepdims=True)
            e = jnp.exp(sc - mx)
            den = jnp.sum(e, axis=-1, keepdims=True)
            pm = (e * (1.0 / den)).astype(BF16)
            outs.append(_nn(pm, vb))
            lses.append(mx + jnp.log(den))
        o_ref[...] = jnp.where(lo, outs[0], outs[1]).astype(BF16)
        lse_ref[...] = jnp.where(lo, lses[0], lses[1])

    return _hosted_call(
        body, xchg, name="attn_fwd", grid=(NH // 2, nq),
        in_specs=[
            pl.BlockSpec((TQ, LANE), lambda j, i: (i, j)),
            pl.BlockSpec((s, LANE), lambda j, i: (0, AW // LANE + j)),
            pl.BlockSpec((s, LANE), lambda j, i: (0, 2 * AW // LANE + j)),
            _layer((1, LANE), l), _layer((1, LANE), l),
            _bias_layer_spec(l),
        ],
        out_specs=[pl.BlockSpec((TQ, LANE), lambda j, i: (i, j)), pl.BlockSpec((TQ, LANE), lambda j, i: (i, j))],
        out_shape=[SDS((s, D), BF16), SDS((s, AW), F32)], args=[p, p, p, qg2, kg2, bias])


_C0 = 3 * AW // LANE


def _cp_in_specs(s, l):
    blk = lambda f: pl.BlockSpec((s, LANE), f)
    return [
        blk(lambda g: (0, _C0 + jnp.minimum(g, 1))),
        blk(lambda g: (0, _C0 + 2 + jnp.minimum(g, 1))),
        blk(lambda g: (0, _C0 + 4 + jnp.minimum(g, 1))),
        blk(lambda g: (0, _C0 + 6 + jnp.maximum(g - 2, 0))),
        pl.BlockSpec((None, 3, LANE), lambda g: (l, 0, jnp.minimum(g, 1))),
        pl.BlockSpec((None, None, LANE, LANE), lambda g: (l, jnp.maximum(g - 2, 0), 0, 0)),
        pl.BlockSpec((None, 1, LANE), lambda g: (l, 0, jnp.maximum(g - 2, 0))),
    ]


def _pool_window_sums(u_ref, buf_a, buf_b, jj, s, rt):
    nrt = s // rt
    for t in range(nrt):
        buf_a[pl.ds(PAD + t * rt, rt), :] = u_ref[pl.ds(t * rt, rt), :]

    def stage(src, dst, sh):
        for t in range(nrt):
            r0 = PAD + t * rt
            dst[pl.ds(r0, rt), :] = src[pl.ds(r0, rt), :] + src[pl.ds(r0 - sh, rt), :]

    stage(buf_a, buf_b, 1)
    stage(buf_b, buf_a, 2)

    @pl.when(jj == 1)
    def _():
        stage(buf_a, buf_b, 4)
        stage(buf_b, buf_a, 8)


def _pool_counts(jj, lo, r0, rt):
    w = jnp.where(lo, jnp.where(jj == 0, 2.0, 8.0), jnp.where(jj == 0, 4.0, 16.0))
    pos1 = (lax.broadcasted_iota(jnp.int32, (rt, LANE), 0) + (r0 + 1)).astype(F32)
    return jnp.minimum(pos1, w)


def _conv_pool_fwd(p, mix, conv_w, wbd, pscale, l):
    s = p.shape[0]
    rt = min(256, s)
    nrt = s // rt

    def body(gb_ref, gc_ref, hin_ref, u_ref, cw_ref, wbd_ref, ps_ref, mix_in, o_ref, buf_a, buf_b):
        del mix_in
        g = pl.program_id(0)
        zpad = jnp.zeros((PAD, LANE), F32)
        buf_a[pl.ds(0, PAD), :] = zpad
        buf_b[pl.ds(0, PAD), :] = zpad

        @pl.when(g < 2)
        def _conv():
            for t in range(nrt):
                buf_a[pl.ds(PAD + t * rt, rt), :] = gc_ref[pl.ds(t * rt, rt), :] * hin_ref[pl.ds(t * rt, rt), :]
            w0, w1, w2 = cw_ref[0:1, :], cw_ref[1:2, :], cw_ref[2:3, :]
            for t in range(nrt):
                r0 = PAD + t * rt
                y = w0 * buf_a[pl.ds(r0 - 2, rt), :] + w1 * buf_a[pl.ds(r0 - 1, rt), :] + w2 * buf_a[pl.ds(r0, rt), :]
                o_ref[pl.ds(t * rt, rt), :] = (gb_ref[pl.ds(t * rt, rt), :] * y).astype(BF16)

        @pl.when(g >= 2)
        def _pool():
            jj = g - 2
            lo = _lo_mask()
            _pool_window_sums(u_ref, buf_a, buf_b, jj, s, rt)
            wb = wbd_ref[...]
            for t in range(nrt):
                r0 = PAD + t * rt
                wsum = jnp.where(lo, buf_b[pl.ds(r0, rt), :], buf_a[pl.ds(r0, rt), :])
                m = wsum / _pool_counts(jj, lo, t * rt, rt) - u_ref[pl.ds(t * rt, rt), :]
                o_ref[pl.ds(t * rt, rt), :] = (_nn(m.astype(BF16), wb) * ps_ref[...]).astype(BF16)

    return pl.pallas_call(
        body, name="conv_pool_fwd", grid=(4,),
        in_specs=_cp_in_specs(s, l) + [pl.BlockSpec(memory_space=pl.ANY)],
        out_specs=pl.BlockSpec((s, LANE), lambda g: (0, AW // LANE + g)),
        out_shape=SDS((s, D), BF16),
        scratch_shapes=[pltpu.VMEM((s + 2 * PAD, LANE), F32), pltpu.VMEM((s + 2 * PAD, LANE), F32)],
        input_output_aliases={7: 0},
        compiler_params=_cp(),
    )(p, p, p, p, conv_w, wbd, pscale, mix)


def _mlp_fwd(x, mix, wout, g2, w1_t, w2, l, xchg=None):
    s = x.shape[0]
    t = min(256, s)

    def body(x_ref, mix_ref, wo_ref, g_ref, w1_ref, w2_ref, xm_ref, a_ref, xo_ref):
        xm = x_ref[...] + _nn(mix_ref[...], wo_ref[...])
        xm_ref[...] = xm
        r = lax.rsqrt(jnp.mean(xm * xm, axis=-1, keepdims=True) + EPS)
        h2 = (xm * r * g_ref[...]).astype(BF16)
        a = _nt(h2, w1_ref[...])
        a_ref[...] = a.astype(BF16)
        f = jnp.square(jnp.maximum(a, 0.0)).astype(BF16)
        xo_ref[...] = xm + _nn(f, w2_ref[...])

    row = lambda c: pl.BlockSpec((t, c), lambda i: (i, 0))
    return _hosted_call(
        body, xchg, name="mlp_fwd", grid=(s // t,),
        in_specs=[row(D), row(D), _layer((D, D), l), _layer((1, D), l), _layer((DFF, D), l), _layer((DFF, D), l)],
        out_specs=[row(D), row(DFF), row(D)],
        out_shape=[SDS((s, D), F32), SDS((s, DFF), BF16), SDS((s, D), F32)], args=[x, mix, wout, g2, w1_t, w2])


def _loss_grad(y, target):
    s = y.shape[0]
    t = min(512, s)

    def body(y_ref, t_ref, dy_ref, acc_ref):
        @pl.when(pl.program_id(0) == 0)
        def _():
            acc_ref[...] = jnp.zeros_like(acc_ref)
        e = y_ref[...] - t_ref[...]
        dy_ref[...] = e * (1.0 / D)
        acc_ref[...] += jnp.sum(e * e)

    row = pl.BlockSpec((t, D), lambda i: (i, 0))
    return pl.pallas_call(
        body, name="loss_grad", grid=(s // t,),
        in_specs=[row, row],
        out_specs=[row, pl.BlockSpec((8, LANE), lambda i: (0, 0))],
        out_shape=[SDS((s, D), F32), SDS((8, LANE), F32)],
        compiler_params=_cp(),
    )(y, target)


def _mlp_bwd(dxo, a, xm, g2, w1_t, w2, wout, l, xchg=None):
    s = dxo.shape[0]
    t = min(256, s)

    def body(dxo_ref, a_ref, xm_ref, g_ref, w1_ref, w2_ref, wo_ref,
             dxm_ref, dmix_ref, f_ref, da_ref, h2_ref, dxob_ref, dxmb_ref, dg_ref):
        @pl.when(pl.program_id(0) == 0)
        def _():
            dg_ref[...] = jnp.zeros_like(dg_ref)
        dxo = dxo_ref[...]
        dxob = dxo.astype(BF16)
        dxob_ref[...] = dxob
        ra = jnp.maximum(a_ref[...].astype(F32), 0.0)
        f_ref[...] = jnp.square(ra).astype(BF16)
        dab = (_nt(dxob, w2_ref[...]) * (2.0 * ra)).astype(BF16)
        da_ref[...] = dab
        dh2 = _nn(dab, w1_ref[...])
        xm = xm_ref[...]
        g = g_ref[...]
        r = lax.rsqrt(jnp.mean(xm * xm, axis=-1, keepdims=True) + EPS)
        h2_ref[...] = (xm * r * g).astype(BF16)
        dx_n, dgr = _rms_bwd(dh2, xm, g)
        dg_ref[...] += jnp.sum(dgr, axis=0, keepdims=True)
        dxm = dxo + dx_n
        dxm_ref[...] = dxm
        dxmb = dxm.astype(BF16)
        dxmb_ref[...] = dxmb
        dmix_ref[...] = _nt(dxmb, wo_ref[...])

    row = lambda c: pl.BlockSpec((t, c), lambda i: (i, 0))
    return _hosted_call(
        body, xchg, name="mlp_bwd", grid=(s // t,),
        in_specs=[row(D), row(DFF), row(D), _layer((1, D), l), _layer((DFF, D), l), _layer((DFF, D), l), _layer((D, D), l)],
        out_specs=[row(D), row(D), row(DFF), row(DFF), row(D), row(D), row(D), pl.BlockSpec((1, D), lambda i: (0, 0))],
        out_shape=[SDS((s, D), F32), SDS((s, D), F32), SDS((s, DFF), BF16), SDS((s, DFF), BF16),
                   SDS((s, D), BF16), SDS((s, D), BF16), SDS((s, D), BF16), SDS((1, D), F32)],
        args=[dxo, a, xm, g2, w1_t, w2, wout])


def _attn_bwd(p, lse, dmix, qg2, kg2, bias, l, xchg=None):
    s = p.shape[0]
    nq = s // TQ
    scale = HD ** -0.5
    rt = min(512, s)

    def body(q_ref, k_ref, v_ref, qg_ref, kg_ref, b_ref, lse_ref, do_ref,
             dq_ref, dk_ref, dv_ref, db_ref, dqg_ref, dkg_ref, dk_acc, dv_acc):
        i = pl.program_id(1)
        ks = pl.multiple_of(jnp.maximum(i * TQ - BAND, 0), TQ)
        lo = _lo_mask()

        @pl.when(i == 0)
        def _():
            dk_acc[...] = jnp.zeros_like(dk_acc)
            dv_acc[...] = jnp.zeros_like(dv_acc)
            dqg_ref[...] = jnp.zeros_like(dqg_ref)
            dkg_ref[...] = jnp.zeros_like(dkg_ref)

        qg, kg = qg_ref[...], kg_ref[...]
        xq, rq = _head_norm(q_ref[...], lo)
        qn = (xq * qg).astype(BF16)
        kn = (_head_norm(k_ref[pl.ds(ks, WIN), :], lo)[0] * kg).astype(BF16)
        vb = v_ref[pl.ds(ks, WIN), :].astype(BF16)
        dob = do_ref[...].astype(BF16)
        lse = lse_ref[...]
        dqn = jnp.zeros((TQ, LANE), F32)
        dkn = jnp.zeros((WIN, LANE), F32)
        dvw = jnp.zeros((WIN, LANE), F32)
        for half in range(2):
            m_ = lo if half == 0 else jnp.logical_not(lo)
            qa = jnp.where(m_, qn, jnp.zeros_like(qn))
            doa = jnp.where(m_, dob, jnp.zeros_like(dob))
            lse_h = lse[:, half * HD:half * HD + 1]
            pm = jnp.exp(_nt(qa, kn) * scale + b_ref[half] - lse_h)
            dp = _nt(doa, vb)
            delta = jnp.sum(pm * dp, axis=-1, keepdims=True)
            ds = pm * (dp - delta)

            @pl.when(i < NVAR)
            def _():
                db_ref[half] = ds

            @pl.when(i >= NVAR)
            def _():
                db_ref[half] += ds

            dsb = ds.astype(BF16)
            dqn = jnp.where(m_, _nn(dsb, kn), dqn)
            dkn = dkn + _tn(dsb, qa)
            dvw = dvw + _tn(pm.astype(BF16), doa)
        dq, dqg_rows = _head_norm_bwd(dqn * scale, xq, rq, qg, lo)
        dq_ref[...] = dq.astype(BF16)
        dqg_ref[...] += jnp.sum(dqg_rows, axis=0, keepdims=True)
        dk_acc[pl.ds(ks, WIN), :] += dkn * scale
        dv_acc[pl.ds(ks, WIN), :] += dvw

        @pl.when(i == nq - 1)
        def _():
            dkg = jnp.zeros((1, LANE), F32)
            for t in range(s // rt):
                rows = pl.ds(t * rt, rt)
                xk, rk = _head_norm(k_ref[rows, :], lo)
                dk, dkg_rows = _head_norm_bwd(dk_acc[rows, :], xk, rk, kg, lo)
                dk_ref[rows, :] = dk.astype(BF16)
                dv_ref[rows, :] = dv_acc[rows, :].astype(BF16)
                dkg = dkg + jnp.sum(dkg_rows, axis=0, keepdims=True)
            dkg_ref[...] = dkg

    tile = pl.BlockSpec((TQ, LANE), lambda j, i: (i, j))
    gain = pl.BlockSpec((None, 1, LANE), lambda j, i: (j, 0, 0))
    return _hosted_call(
        body, xchg, name="attn_bwd", grid=(NH // 2, nq),
        in_specs=[
            tile,
            pl.BlockSpec((s, LANE), lambda j, i: (0, AW // LANE + j)),
            pl.BlockSpec((s, LANE), lambda j, i: (0, 2 * AW // LANE + j)),
            _layer((1, LANE), l), _layer((1, LANE), l),
            _bias_layer_spec(l), tile, tile,
        ],
        out_specs=[tile, pl.BlockSpec((s, LANE), lambda j, i: (0, j)), pl.BlockSpec((s, LANE), lambda j, i: (0, j)),
                   _bias_spec(), gain, gain],
        out_shape=[SDS((s, AW), BF16), SDS((s, AW), BF16), SDS((s, AW), BF16),
                   SDS((NVAR, NH, TQ, WIN), F32), SDS((NH // 2, 1, LANE), F32), SDS((NH // 2, 1, LANE), F32)],
        scratch_shapes=[pltpu.VMEM((s, LANE), F32), pltpu.VMEM((s, LANE), F32)],
        args=[p, p, p, qg2, kg2, bias, lse, dmix])


def _conv_pool_bwd(p, dmix, conv_w, wbd, pscale, l):
    s = p.shape[0]
    rt = min(256, s)
    nrt = s // rt

    def body(gb_ref, gc_ref, hin_ref, u_ref, cw_ref, wbd_ref, ps_ref, dy_ref,
             dgb_ref, dgc_ref, dhin_ref, du_ref, dcw_ref, dwbd_ref, dps_ref, buf_a, buf_b, buf_c, buf_d):
        g = pl.program_id(0)
        zpad = jnp.zeros((PAD, LANE), F32)
        for buf in (buf_a, buf_b, buf_c):
            buf[pl.ds(0, PAD), :] = zpad
            buf[pl.ds(PAD + s, PAD), :] = zpad

        @pl.when(g < 2)
        def _conv():
            for t in range(nrt):
                rows = pl.ds(t * rt, rt)
                buf_a[pl.ds(PAD + t * rt, rt), :] = gc_ref[rows, :] * hin_ref[rows, :]
                buf_b[pl.ds(PAD + t * rt, rt), :] = dy_ref[rows, :] * gb_ref[rows, :]
            w0, w1, w2 = cw_ref[0:1, :], cw_ref[1:2, :], cw_ref[2:3, :]
            d0 = jnp.zeros((1, LANE), F32)
            d1 = jnp.zeros((1, LANE), F32)
            d2 = jnp.zeros((1, LANE), F32)
            for t in range(nrt):
                rows = pl.ds(t * rt, rt)
                r0 = PAD + t * rt
                z2, z1, z0 = buf_a[pl.ds(r0 - 2, rt), :], buf_a[pl.ds(r0 - 1, rt), :], buf_a[pl.ds(r0, rt), :]
                y = w0 * z2 + w1 * z1 + w2 * z0
                dgb_ref[rows, :] = (dy_ref[rows, :] * y).astype(BF16)
                e0 = buf_b[pl.ds(r0, rt), :]
                d0 = d0 + jnp.sum(e0 * z2, axis=0, keepdims=True)
                d1 = d1 + jnp.sum(e0 * z1, axis=0, keepdims=True)
                d2 = d2 + jnp.sum(e0 * z0, axis=0, keepdims=True)
                dz = w2 * e0 + w1 * buf_b[pl.ds(r0 + 1, rt), :] + w0 * buf_b[pl.ds(r0 + 2, rt), :]
                dgc_ref[rows, :] = (dz * hin_ref[rows, :]).astype(BF16)
                dhin_ref[rows, :] = (dz * gc_ref[rows, :]).astype(BF16)
            dcw_ref[0:1, :] = d0
            dcw_ref[1:2, :] = d1
            dcw_ref[2:3, :] = d2

        @pl.when(g >= 2)
        def _pool():
            jj = g - 2
            lo = _lo_mask()
            _pool_window_sums(u_ref, buf_a, buf_b, jj, s, rt)
            wb = wbd_ref[...]
            ps = ps_ref[...]
            dps = jnp.zeros((1, LANE), F32)
            dwb = jnp.zeros((LANE, LANE), F32)
            for t in range(nrt):
                rows = pl.ds(t * rt, rt)
                r0 = PAD + t * rt
                cnt = _pool_counts(jj, lo, t * rt, rt)
                wsum = jnp.where(lo, buf_b[pl.ds(r0, rt), :], buf_a[pl.ds(r0, rt), :])
                mb = (wsum / cnt - u_ref[rows, :]).astype(BF16)
                dy = dy_ref[rows, :]
                dps = dps + jnp.sum(dy * _nn(mb, wb), axis=0, keepdims=True)
                dmp = (dy * ps).astype(BF16)
                dwb = dwb + _tn(mb, dmp)
                dm = _nt(dmp, wb)
                buf_d[rows, :] = dm
                buf_c[pl.ds(r0, rt), :] = dm / cnt
            dps_ref[...] = dps
            dwbd_ref[...] = dwb

            def stage(src, dst, sh):
                for t in range(nrt):
                    r0 = PAD + t * rt
                    dst[pl.ds(r0, rt), :] = src[pl.ds(r0, rt), :] + src[pl.ds(r0 + sh, rt), :]

            def finish(first, second):
                for t in range(nrt):
                    rows = pl.ds(t * rt, rt)
                    r0 = PAD + t * rt
                    fw = jnp.where(lo, first[pl.ds(r0, rt), :], second[pl.ds(r0, rt), :])
                    du_ref[rows, :] = (fw - buf_d[rows, :]).astype(BF16)

            stage(buf_c, buf_a, 1)
            stage(buf_a, buf_b, 2)

            @pl.when(jj == 0)
            def _():
                finish(buf_a, buf_b)

            @pl.when(jj == 1)
            def _():
                stage(buf_b, buf_c, 4)
                stage(buf_c, buf_a, 8)
                finish(buf_c, buf_a)

    cblk = pl.BlockSpec((s, LANE), lambda g: (0, jnp.minimum(g, 1)))
    pblk = pl.BlockSpec((s, LANE), lambda g: (0, jnp.maximum(g - 2, 0)))
    padded = pltpu.VMEM((s + 2 * PAD, LANE), F32)
    return pl.pallas_call(
        body, name="conv_pool_bwd", grid=(4,),
        in_specs=_cp_in_specs(s, l) + [pl.BlockSpec((s, LANE), lambda g: (0, AW // LANE + g))],
        out_specs=[cblk, cblk, cblk, pblk,
                   pl.BlockSpec((3, LANE), lambda g: (0, jnp.minimum(g, 1))),
                   pl.BlockSpec((None, LANE, LANE), lambda g: (jnp.maximum(g - 2, 0), 0, 0)),
                   pl.BlockSpec((1, LANE), lambda g: (0, jnp.maximum(g - 2, 0)))],
        out_shape=[SDS((s, CW), BF16), SDS((s, CW), BF16), SDS((s, CW), BF16), SDS((s, PWD), BF16),
                   SDS((3, CW), F32), SDS((2, LANE, LANE), F32), SDS((1, PWD), F32)],
        scratch_shapes=[padded, padded, padded, pltpu.VMEM((s, LANE), F32)],
        compiler_params=_cp(),
    )(p, p, p, p, conv_w, wbd, pscale, dmix)


def _in_proj_bwd(parts, x, dxm, g1, win_t, l):
    s = x.shape[0]
    t = min(256, s)
    widths = [a.shape[1] for a in parts]
    offs = [int(o) for o in np.cumsum([0] + widths[:-1])]
    n = len(parts)

    def body(*refs):
        part_refs = refs[:n]
        x_ref, dxm_ref, g_ref, w_ref, dx_ref, dp_ref, dg_ref = refs[n:]

        @pl.when(pl.program_id(0) == 0)
        def _():
            dg_ref[...] = jnp.zeros_like(dg_ref)
        for r, o, w in zip(part_refs, offs, widths):
            dp_ref[:, o:o + w] = r[...]
        dh = _nn(dp_ref[...], w_ref[...])
        dx_n, dgr = _rms_bwd(dh, x_ref[...], g_ref[...])
        dg_ref[...] += jnp.sum(dgr, axis=0, keepdims=True)
        dx_ref[...] = dxm_ref[...] + dx_n

    row = lambda c: pl.BlockSpec((t, c), lambda i: (i, 0))
    return pl.pallas_call(
        body, name="in_proj_bwd", grid=(s // t,),
        in_specs=[row(w) for w in widths] + [row(D), row(D), _layer((1, D), l), _layer((DIN, D), l)],
        out_specs=[row(D), row(DIN), pl.BlockSpec((1, D), lambda i: (0, 0))],
        out_shape=[SDS((s, D), F32), SDS((s, DIN), BF16), SDS((1, D), F32)],
        compiler_params=_cp(),
    )(*parts, x, dxm, g1, win_t)


def _wgrad(a, b, stacked, layer, tag):
    s, m = a.shape
    mb = 512
    t = min(1024, s)
    nt = s // t

    def body(*refs):
        a_ref, b_ref = refs[:2]
        o_ref, acc = refs[-2:]

        @pl.when(pl.program_id(1) == 0)
        def _():
            acc[...] = jnp.zeros_like(acc)
        acc[...] += _tn(a_ref[...], b_ref[...])

        @pl.when(pl.program_id(1) == nt - 1)
        def _():
            o_ref[...] = acc[...].astype(BF16)

    in_specs = [pl.BlockSpec((t, mb), lambda mi, ti: (ti, mi)), pl.BlockSpec((t, D), lambda mi, ti: (ti, 0))]
    args = [a, b]
    aliases = {}
    if stacked is not None:
        in_specs.append(pl.BlockSpec(memory_space=pl.ANY))
        args.append(stacked)
        aliases = {2: 0}
    return pl.pallas_call(
        body, name=f"wgrad_{tag}_l{layer}", grid=(m // mb, nt),
        in_specs=in_specs,
        out_specs=pl.BlockSpec((None, mb, D), lambda mi, ti: (layer, mi, 0)),
        out_shape=SDS((L, m, D), BF16),
        scratch_shapes=[pltpu.VMEM((mb, D), F32)],
        input_output_aliases=aliases,
        compiler_params=_cp(),
    )(*args)


def _bias_tables(gvec):
    def body(g_ref, o_ref):
        qc = lax.broadcasted_iota(jnp.int32, (TQ, WIN), 0) // CHUNK
        kc = lax.broadcasted_iota(jnp.int32, (TQ, WIN), 1) // CHUNK
        for var in range(NVAR):
            vec = jnp.broadcast_to(g_ref[:, var * TQ:var * TQ + NTOE], (TQ, NTOE))
            toe = pltpu.roll(vec, NTOE - TQ + 1, 1, stride=1, stride_axis=0)[:, :WIN]
            rel = (BAND - var * TQ) // CHUNK + qc - kc
            o_ref[var] = jnp.where((rel >= 0) & (rel <= N_PREV), toe, NEG)

    return pl.pallas_call(
        body, name="bias_tables", grid=(L, NH),
        in_specs=[pl.BlockSpec((None, None, 1, NG), lambda l, h: (l, h, 0, 0))],
        out_specs=pl.BlockSpec((None, NVAR, None, TQ, WIN), lambda l, h: (l, 0, h, 0, 0)),
        out_shape=SDS((L, NVAR, NH, TQ, WIN), F32),
        compiler_params=_cp(),
    )(gvec)


def _bias_tables_grad(dbias, l):
    nb = NTOE // LANE
    wb = WIN // LANE

    def body(d_ref, o_ref):
        ii = lax.broadcasted_iota(jnp.int32, (LANE, LANE), 0)
        jj = lax.broadcasted_iota(jnp.int32, (LANE, LANE), 1)
        flip = jnp.where(ii + jj == LANE - 1, 1.0, 0.0).astype(BF16)
        o_ref[...] = jnp.zeros_like(o_ref)
        for var in range(NVAR):
            blocks = []
            for b in range(nb):
                src = nb - 1 - b
                if src >= wb:
                    blocks.append(jnp.zeros((TQ, LANE), F32))
                    continue
                xv = d_ref[var, :, src * LANE:(src + 1) * LANE]
                hi = xv.astype(BF16)
                lo = (xv - hi.astype(F32)).astype(BF16)
                blocks.append(_nn(hi, flip) + _nn(lo, flip))
            rev = jnp.concatenate(blocks, axis=1)
            skew = pltpu.roll(rev, NTOE - TQ + 1, 1, stride=1, stride_axis=0)
            off = NG - NTOE - var * TQ
            o_ref[:, off:off + NTOE] += jnp.sum(skew, axis=0, keepdims=True)

    return pl.pallas_call(
        body, name=f"bias_tables_grad_l{l}", grid=(NH,),
        in_specs=[pl.BlockSpec((NVAR, None, TQ, WIN), lambda h: (0, h, 0, 0))],
        out_specs=pl.BlockSpec((None, 1, NG), lambda h: (h, 0, 0)),
        out_shape=SDS((NH, 1, NG), F32),
        compiler_params=_cp(),
    )(dbias)


_MASKS = [(mx, my, mc) for mx in (0, 1) for my in (0, 1) for mc in (0, 1)][1:]


def _position():
    return lax.axis_index("x"), lax.axis_index("y"), lax.axis_index("c")


def _peer(pos, mask):
    return tuple(1 - a if f else a for a, f in zip(pos, mask))


def _index(pos):
    return 4 * pos[0] + 2 * pos[1] + pos[2]


def _exchange_plan(kind, items, src, dst, sems, receives):
    send_sems, recv_sems, local_sems = sems
    me = _position()

    def rows(k, pos):
        return pl.ds(_index(pos) * SHARD_ROWS[k], SHARD_ROWS[k])

    local, sends, recvs = [], [], []
    for n, (k, l) in enumerate(items):
        if kind == "gather":
            src_for = lambda pos, k=k, l=l: src[k].at[l]
            land = lambda pos, k=k, l=l: dst[k].at[l, rows(k, pos), :]
        else:
            src_for = lambda pos, k=k, l=l: src[k].at[l, rows(k, pos), :]
            land = lambda pos, k=k, l=l: dst[k].at[_index(pos), l]
        local.append(pltpu.make_async_copy(src_for(me), land(me), local_sems.at[n]))
        for pi, mask in enumerate(_MASKS):
            peer = _peer(me, mask)
            for lst, where in ((sends, me), (recvs, peer)) if receives else ((sends, me),):
                lst.append(pltpu.make_async_remote_copy(
                    src_ref=src_for(peer), dst_ref=land(where), send_sem=send_sems.at[pi, n],
                    recv_sem=recv_sems.at[pi, n], device_id=peer, device_id_type=MESH_ID))
    return local, sends, recvs


def _hosted_call(body, xchg, *, name, grid, in_specs, out_specs, out_shape, args, scratch_shapes=()):
    if xchg is None:
        outs = pl.pallas_call(
            body, name=name, grid=grid, in_specs=list(in_specs), out_specs=list(out_specs),
            out_shape=list(out_shape), scratch_shapes=list(scratch_shapes), compiler_params=_cp())(*args)
        return outs, {}
    kind, items, src, dst = xchg
    ks = sorted({k for k, _ in items})
    given = [k for k in ks if dst.get(k) is not None]
    n_in, n_out, n_scr, nk, ng = len(args), len(out_shape), len(scratch_shapes), len(ks), len(given)
    hbm = pl.BlockSpec(memory_space=pl.ANY)

    def dst_shape(k):
        r = SHARD_ROWS[k]
        return SDS((L, NDEV * r, D) if kind == "gather" else (NDEV, L, r, D), BF16)

    def wrapped(*refs):
        ins = refs[:n_in]
        xsrc = dict(zip(ks, refs[n_in:n_in + nk]))
        o0 = n_in + nk + ng
        outs = refs[o0:o0 + n_out]
        xdst = dict(zip(ks, refs[o0 + n_out:o0 + n_out + nk]))
        s0 = o0 + n_out + nk
        scratch = refs[s0:s0 + n_scr]
        sems = refs[s0 + n_scr:]
        first, last = True, True
        for d, g in enumerate(grid):
            first = jnp.logical_and(first, pl.program_id(d) == 0)
            last = jnp.logical_and(last, pl.program_id(d) == g - 1)

        @pl.when(first)
        def _():
            local, sends, _ = _exchange_plan(kind, items, xsrc, xdst, sems, False)
            for cp in local + sends:
                cp.start()

        body(*ins, *outs, *scratch)

        @pl.when(last)
        def _():
            local, sends, recvs = _exchange_plan(kind, items, xsrc, xdst, sems, True)
            for cp in recvs:
                cp.wait_recv()
            for cp in sends:
                cp.wait_send()
            for cp in local:
                cp.wait()

    npeer, nit = len(_MASKS), len(items)
    res = pl.pallas_call(
        wrapped, name=name, grid=grid,
        in_specs=list(in_specs) + [hbm] * (nk + ng),
        out_specs=list(out_specs) + [hbm] * nk,
        out_shape=list(out_shape) + [dst_shape(k) for k in ks],
        scratch_shapes=list(scratch_shapes) + [
            pltpu.SemaphoreType.DMA((npeer, nit)), pltpu.SemaphoreType.DMA((npeer, nit)), pltpu.SemaphoreType.DMA((nit,))],
        input_output_aliases={n_in + nk + j: n_out + ks.index(k) for j, k in enumerate(given)},
        compiler_params=_cp(),
    )(*args, *[src[k] for k in ks], *[dst[k] for k in given])
    return list(res[:n_out]), dict(zip(ks, res[n_out:]))


def _gather_first(shards):
    def body(o_ref):
        o_ref[...] = jnp.zeros_like(o_ref)

    _, got = _hosted_call(
        body, ("gather", [(0, 0)], {0: shards[0]}, {}), name="gather_first", grid=(1,),
        in_specs=[], out_specs=[pl.BlockSpec((8, LANE), lambda i: (0, 0))], out_shape=[SDS((8, LANE), F32)], args=[])
    return got[0]


def _sum_slots(slots, xchg=None):
    _, _, r, _ = slots.shape
    rt = 64

    def body(in_ref, o_ref):
        acc = in_ref[0].astype(F32)
        for d in range(1, NDEV):
            acc = acc + in_ref[d].astype(F32)
        o_ref[...] = acc

    (out,), got = _hosted_call(
        body, xchg, name=f"sum_slots_r{r}" + ("_x" if xchg else ""), grid=(L, r // rt),
        in_specs=[pl.BlockSpec((NDEV, None, rt, D), lambda l, i: (0, l, i, 0))],
        out_specs=[pl.BlockSpec((None, rt, D), lambda l, i: (l, i, 0))],
        out_shape=[SDS((L, r, D), F32)], args=[slots])
    return out, got


def _exchange_small(v, reduce):
    rows = v.shape[0]

    def body(v_ref, o_ref, *scratch):
        if reduce:
            slots, send_sems, recv_sems = scratch
        else:
            slots = o_ref
            send_sems, recv_sems = scratch
        me = _position()
        slots[_index(me)] = v_ref[...]
        sends = []
        for pi, mask in enumerate(_MASKS):
            cp = pltpu.make_async_remote_copy(
                src_ref=v_ref, dst_ref=slots.at[_index(me)], send_sem=send_sems.at[pi], recv_sem=recv_sems.at[pi],
                device_id=_peer(me, mask), device_id_type=MESH_ID)
            cp.start()
            sends.append(cp)
        for pi, mask in enumerate(_MASKS):
            peer = _peer(me, mask)
            pltpu.make_async_remote_copy(
                src_ref=v_ref, dst_ref=slots.at[_index(peer)], send_sem=send_sems.at[pi], recv_sem=recv_sems.at[pi],
                device_id=peer, device_id_type=MESH_ID).wait_recv()
        for cp in sends:
            cp.wait_send()
        if reduce:
            acc = slots[0]
            for d in range(1, NDEV):
                acc = acc + slots[d]
            o_ref[...] = acc

    vm = pl.BlockSpec(memory_space=pltpu.VMEM)
    sems = [pltpu.SemaphoreType.DMA((len(_MASKS),)), pltpu.SemaphoreType.DMA((len(_MASKS),))]
    return pl.pallas_call(
        body, name="reduce_small" if reduce else "gather_small",
        in_specs=[vm], out_specs=vm,
        out_shape=SDS((rows, LANE) if reduce else (NDEV, rows, LANE), F32),
        scratch_shapes=([pltpu.VMEM((NDEV, rows, LANE), F32)] if reduce else []) + sems,
        compiler_params=_cp(),
    )(v)


def _adamw(w, g, m, v):
    rows, cols = w.shape
    t = rows
    for cand in (512, 256, 128, 64, 32, 16, 8):
        if rows % cand == 0:
            t = cand
            break

    def body(w_ref, g_ref, m_ref, v_ref, d_ref, nm_ref, nv_ref):
        gv = g_ref[...]
        mn = B1 * m_ref[...] + (1.0 - B1) * gv
        vn = B2 * v_ref[...] + (1.0 - B2) * jnp.square(gv)
        nm_ref[...] = mn
        nv_ref[...] = vn
        m_hat = mn / (1.0 - B1 ** STEP)
        v_hat = vn / (1.0 - B2 ** STEP)
        d_ref[...] = -LR * (m_hat / (jnp.sqrt(v_hat) + AEPS) + WD * w_ref[...])

    blk = pl.BlockSpec((t, cols), lambda i: (i, 0))
    return pl.pallas_call(
        body, name=f"adamw_{rows}x{cols}", grid=(rows // t,),
        in_specs=[blk] * 4, out_specs=[blk] * 3,
        out_shape=[SDS((rows, cols), F32)] * 3,
        compiler_params=_cp(),
    )(w, g, m, v)


_DIST0 = BAND + TQ - 1
_N_FAR = _DIST0 - REL_CLIP + 1
_N_NEAR = NG - _N_FAR - (2 * REL_CLIP - 1)


def _bias_vector(rel_bias):
    far = jnp.broadcast_to(rel_bias[..., -1:], (L, NH, _N_FAR))
    near = jnp.broadcast_to(rel_bias[..., :1], (L, NH, _N_NEAR))
    return jnp.concatenate([far, lax.rev(rel_bias[..., 1:-1], (2,)), near], axis=2)[:, :, None, :]


def _bias_vector_grad(dgr):
    first = jnp.sum(dgr[..., :_N_NEAR], axis=-1, keepdims=True)
    last = jnp.sum(dgr[..., NG - _N_FAR:], axis=-1, keepdims=True)
    return jnp.concatenate([first, dgr[..., _N_NEAR:NG - _N_FAR], last], axis=-1)


def _pool_blockdiag(pool_w):
    eye = jnp.eye(2, dtype=F32)
    pw = pool_w.reshape(L, 2, 2, HD, HD)
    return jnp.einsum("ljaik,ab->ljaibk", pw, eye).reshape(L, 2, LANE, LANE)


def _pool_blockdiag_grad(dwbd):
    d = dwbd.reshape(L, 2, 2, HD, 2, HD)
    return jnp.stack([d[:, :, 0, :, 0, :], d[:, :, 1, :, 1, :]], axis=2).reshape(L, 4, HD, HD)


def _pack(arrays, rows):
    flat = jnp.concatenate([a.reshape(-1).astype(F32) for a in arrays])
    return jnp.pad(flat, (0, rows * LANE - flat.shape[0])).reshape(rows, LANE)


def _unpack(packed, shapes):
    flat = packed.reshape(-1)
    out, o = [], 0
    for shp in shapes:
        n = int(np.prod(shp))
        out.append(flat[o:o + n].reshape(shp))
        o += n
    return out


def _rows_for(shapes):
    n = sum(int(np.prod(s)) for s in shapes)
    return -(-n // (8 * LANE)) * 8


def _grads(x, target, small_w, shards):
    g1, qg, kg, rb, cw, pw, ps, g2 = small_w
    g1 = g1.reshape(L, 1, D)
    g2 = g2.reshape(L, 1, D)
    qg2 = jnp.tile(qg, (1, 2)).reshape(L, 1, LANE)
    kg2 = jnp.tile(kg, (1, 2)).reshape(L, 1, LANE)
    ps3 = ps.reshape(L, 1, PWD)
    bias = _bias_tables(_bias_vector(rb))
    wbd = _pool_blockdiag(pw).astype(BF16)
    src = dict(enumerate(shards))

    def gather(items, full):
        return ("gather", items, {k: src[k] for k, _ in items}, {k: full.get(k) for k, _ in items})

    full = {0: _gather_first(shards)}
    saved = []
    h = x
    for l in range(L):
        (p, h_b), got = _in_proj(h, g1, full[0], l, gather([(1, l)], full) if l else None)
        full.update(got)
        (mix, lse), got = _attn_fwd(p, qg2, kg2, bias, l, gather([(2, l), (3, l)] + ([] if l else [(1, 0)]), full))
        full.update(got)
        mix = _conv_pool_fwd(p, mix, cw, wbd, ps3, l)
        (xm, a, xo), got = _mlp_fwd(h, mix, full[1], g2, full[2], full[3], l,
                                    gather([(0, l + 1)], full) if l + 1 < L else None)
        full.update(got)
        saved.append((h, h_b, p, mix, lse, xm, a))
        h = xo
    dx, sq = _loss_grad(h, target)

    stacks = [None] * 4
    slots = {}

    def scatter(items):
        return ("scatter", items, {k: stacks[k] for k, _ in items}, {k: slots.get(k) for k, _ in items})

    per_layer = [None] * L
    for l in reversed(range(L)):
        x_in, h_b, p, mix, lse, xm, a = saved[l]
        (dxm, dmix, f_b, da_b, h2_b, dxo_b, dxm_b, dg2), got = _mlp_bwd(
            dx, a, xm, g2, full[2], full[3], full[1], l, scatter([(0, l + 1), (1, l + 1)]) if l + 1 < L else None)
        slots.update(got)
        stacks[1] = _wgrad(mix, dxm_b, stacks[1], l, "w_out")
        stacks[2] = _wgrad(da_b, h2_b, stacks[2], l, "w_mlp1")
        stacks[3] = _wgrad(f_b, dxo_b, stacks[3], l, "w_mlp2")
        (dq, dk, dv, dbias, dqg, dkg), got = _attn_bwd(p, lse, dmix, qg2, kg2, bias, l, scatter([(2, l), (3, l)]))
        slots.update(got)
        dgb, dgc, dhin, du, dcw, dwbd, dps = _conv_pool_bwd(p, dmix, cw, wbd, ps3, l)
        dx, dp_b, dg1 = _in_proj_bwd([dq, dk, dv, dgb, dgc, dhin, du], x_in, dxm, g1, full[0], l)
        stacks[0] = _wgrad(dp_b, h_b, stacks[0], l, "w_in")
        per_layer[l] = (dg1, dg2, dqg, dkg, _bias_tables_grad(dbias, l), dcw, dwbd, dps)
    g_w1_t, got = _sum_slots(slots[2], scatter([(0, 0), (1, 0)]))
    slots.update(got)
    sums = [_sum_slots(slots[0])[0], _sum_slots(slots[1])[0], g_w1_t, _sum_slots(slots[3])[0]]

    st = [jnp.stack([per_layer[l][k] for l in range(L)]) for k in range(8)]
    small = dict(
        g1=st[0].reshape(L, D), g2=st[1].reshape(L, D),
        qg=st[2].reshape(L, NH, HD).sum(1), kg=st[3].reshape(L, NH, HD).sum(1),
        rb=_bias_vector_grad(st[4].reshape(L, NH, NG)), cw=st[5], pw=_pool_blockdiag_grad(st[6]),
        ps=st[7].reshape(L, PWD))
    return sq, dx, sums, small


def kernel(x, norm1_g, w_in, q_norm_g, k_norm_g, rel_bias, conv_w, pool_w, pool_scale, w_out, norm2_g, w_mlp1, w_mlp2, loss_target, m_norm1_g, m_w_in, m_q_norm_g, m_k_norm_g, m_rel_bias, m_conv_w, m_pool_w, m_pool_scale, m_w_out, m_norm2_g, m_w_mlp1, m_w_mlp2, v_norm1_g, v_w_in, v_q_norm_g, v_k_norm_g, v_rel_bias, v_conv_w, v_pool_w, v_pool_scale, v_w_out, v_norm2_g, v_w_mlp1, v_w_mlp2):
    me = _index(_position())
    cshard = CW // NDEV

    shards = [jnp.swapaxes(w_in, 1, 2).astype(BF16), w_out.astype(BF16),
              jnp.swapaxes(w_mlp1, 1, 2).astype(BF16), w_mlp2.astype(BF16)]
    cw_all = _exchange_small(_pack([conv_w], 8), reduce=False)
    cw_full = jnp.concatenate(
        [cw_all[d].reshape(-1)[:L * 3 * cshard].reshape(L, 3, cshard) for d in range(NDEV)], axis=2)

    small_w = (norm1_g, q_norm_g, k_norm_g, rel_bias, cw_full, pool_w, pool_scale, norm2_g)
    sq, grad_x, (g_win_t, g_wout, g_w1_t, g_w2), small = _grads(x[0], loss_target[0], small_w, shards)
    g_w_in = jnp.swapaxes(g_win_t, 1, 2)
    g_w_mlp1 = jnp.swapaxes(g_w1_t, 1, 2)

    names = ("g1", "qg", "kg", "rb", "cw", "pw", "ps", "g2")
    gshapes = [(L, D), (L, HD), (L, HD), (L, NH, 2 * REL_CLIP + 1), (L, 3, CW), (L, 4, HD, HD), (L, PWD), (L, D)]
    garrs = [small[n] for n in names]
    rows = _rows_for(gshapes + [(1,)])
    total = _exchange_small(_pack(garrs + [sq[0, :1]], rows), reduce=True)
    g_g1, g_qg, g_kg, g_rb, g_cw_full, g_pw, g_ps, g_g2, sq_sum = _unpack(total, gshapes + [(1,)])
    loss = (0.5 / D) * sq_sum[0]
    g_cw = lax.dynamic_slice_in_dim(g_cw_full, me * cshard, cshard, axis=2)

    def big(w, g, m, v):
        shp = w.shape
        r = lambda a: a.reshape(-1, shp[-1])
        return [o.reshape(shp) for o in _adamw(r(w), r(g), r(m), r(v))]

    up_in = big(w_in, g_w_in, m_w_in, v_w_in)
    up_out = big(w_out, g_wout, m_w_out, v_w_out)
    up_1 = big(w_mlp1, g_w_mlp1, m_w_mlp1, v_w_mlp1)
    up_2 = big(w_mlp2, g_w2, m_w_mlp2, v_w_mlp2)

    sw = [norm1_g, q_norm_g, k_norm_g, rel_bias, conv_w, pool_w, pool_scale, norm2_g]
    sg = [g_g1, g_qg, g_kg, g_rb, g_cw, g_pw, g_ps, g_g2]
    sm = [m_norm1_g, m_q_norm_g, m_k_norm_g, m_rel_bias, m_conv_w, m_pool_w, m_pool_scale, m_norm2_g]
    sv = [v_norm1_g, v_q_norm_g, v_k_norm_g, v_rel_bias, v_conv_w, v_pool_w, v_pool_scale, v_norm2_g]
    sshapes = [a.shape for a in sw]
    srows = _rows_for(sshapes)
    ups = _adamw(_pack(sw, srows), _pack(sg, srows), _pack(sm, srows), _pack(sv, srows))
    s_delta, s_m, s_v = [_unpack(u, sshapes) for u in ups]

    def order(small_list, in_, out_, m1, m2):
        g1_, qg_, kg_, rb_, cw_, pw_, ps_, g2_ = small_list
        return [g1_, in_, qg_, kg_, rb_, cw_, pw_, ps_, out_, g2_, m1, m2]

    grads = order(sg, g_w_in, g_wout, g_w_mlp1, g_w2)
    deltas = order(s_delta, up_in[0], up_out[0], up_1[0], up_2[0])
    new_m = order(s_m, up_in[1], up_out[1], up_1[1], up_2[1])
    new_v = order(s_v, up_in[2], up_out[2], up_1[2], up_2[2])
    return (loss, grad_x[None], *grads, *deltas, *new_m, *new_v)
```

```python
import numpy as np
import jax
import jax.numpy as jnp
from jax import lax
from jax.experimental import pallas as pl
from jax.experimental.pallas import tpu as pltpu

F32 = jnp.float32
BF16 = jnp.bfloat16
SDS = jax.ShapeDtypeStruct
MESH_ID = pl.DeviceIdType.MESH

D = 1024
L = 4
CHUNK = 64
N_PREV = 8
HD = 64
NH = 8
AW = 512
CW = 256
PWD = 256
DIN = 2560
DFF = 4096
EPS = 1e-6
NEG = -1e30
REL_CLIP = 128
POOL_WINDOWS = (2, 4, 8, 16)
LR, B1, B2, AEPS, WD, STEP = 0.001, 0.9, 0.999, 1e-08, 0.01, 10

NDEV = 8
LANE = 128
BAND = N_PREV * CHUNK
TQ = 256
WIN = TQ + BAND
NVAR = BAND // TQ + 1
NTOE = -(-(WIN + TQ - 1) // LANE) * LANE
NG = (NVAR - 1) * TQ + NTOE
PAD = 16
RB_NORM = 64
RB_SOFT = 16
VMEM_LIMIT = 56 * 1024 * 1024
SHARD_ROWS = (DIN // NDEV, D // NDEV, DFF // NDEV, DFF // NDEV)


def _cp(**kw):
    return pltpu.CompilerParams(vmem_limit_bytes=VMEM_LIMIT, **kw)


def _nn(a, b):
    return jnp.dot(a, b, preferred_element_type=F32)


def _nt(a, b):
    return lax.dot_general(a, b, (((1,), (1,)), ((), ())), preferred_element_type=F32)


def _tn(a, b):
    return lax.dot_general(a, b, (((0,), (0,)), ((), ())), preferred_element_type=F32)


def _const(shape):
    n = len(shape)
    return pl.BlockSpec(shape, lambda *_: (0,) * n, pipeline_mode=pl.Buffered(1))


def _layer(shape, l):
    n = len(shape)
    return pl.BlockSpec((None,) + tuple(shape), lambda *_: (l,) + (0,) * n, pipeline_mode=pl.Buffered(1))


def _lo_mask():
    return lax.broadcasted_iota(jnp.int32, (1, LANE), 1) < HD


def _half_sum(t, lo):
    s_lo = jnp.sum(jnp.where(lo, t, 0.0), axis=-1, keepdims=True)
    s_hi = jnp.sum(jnp.where(lo, 0.0, t), axis=-1, keepdims=True)
    return jnp.where(lo, s_lo, s_hi)


def _head_norm(x, lo):
    r = lax.rsqrt(_half_sum(x * x, lo) * (1.0 / HD) + EPS)
    return x * r, r


def _head_norm_bwd(dy, xn, r, g, lo):
    dxn = dy * g
    mu = _half_sum(dxn * xn, lo) * (1.0 / HD)
    return r * (dxn - xn * mu), dy * xn


def _rms_bwd(dy, x, g):
    r = lax.rsqrt(jnp.mean(x * x, axis=-1, keepdims=True) + EPS)
    xn = x * r
    dxn = dy * g
    mu = jnp.mean(dxn * xn, axis=-1, keepdims=True)
    return r * (dxn - xn * mu), dy * xn


def _in_proj(x, g1, win_t, qg2, kg2, l, xchg=None):
    s = x.shape[0]
    t = min(512, s)
    nblk = AW // LANE

    def body(x_ref, g_ref, w_ref, qg_ref, kg_ref, p_ref, h_ref, qkv_ref):
        xv = x_ref[...]
        r = lax.rsqrt(jnp.mean(xv * xv, axis=-1, keepdims=True) + EPS)
        h = (xv * r * g_ref[...]).astype(BF16)
        h_ref[...] = h
        p_ref[...] = _nt(h, w_ref[...])
        lo = _lo_mask()
        gains = (qg_ref[...] * (HD ** -0.5), kg_ref[...])
        for r0 in range(0, t, RB_NORM):
            rows = pl.ds(r0, RB_NORM)
            for c in range(3 * nblk):
                cols = pl.ds(c * LANE, LANE)
                v = p_ref[rows, cols]
                if c < 2 * nblk:
                    v = _head_norm(v, lo)[0] * gains[c // nblk]
                qkv_ref[rows, cols] = v.astype(BF16)

    row = lambda c: pl.BlockSpec((t, c), lambda i: (i, 0))
    return _hosted_call(
        body, xchg, name="in_proj", grid=(s // t,),
        in_specs=[row(D), _layer((1, D), l), _layer((DIN, D), l), _layer((1, LANE), l), _layer((1, LANE), l)],
        out_specs=[row(DIN), row(D), row(3 * AW)],
        out_shape=[SDS((s, DIN), F32), SDS((s, D), BF16), SDS((s, 3 * AW), BF16)], args=[x, g1, win_t, qg2, kg2])


def _bias_spec():
    return pl.BlockSpec((None, 2, TQ, WIN), lambda j, i: (jnp.maximum(NVAR - 1 - i, 0), j, 0, 0))


def _bias_layer_spec(l):
    return pl.BlockSpec((None, None, 2, TQ, WIN), lambda j, i: (l, jnp.maximum(NVAR - 1 - i, 0), j, 0, 0))


def _attn_fwd(qkv, bias, l, xchg=None):
    s = qkv.shape[0]
    nq = s // TQ

    def body(q_ref, k_ref, v_ref, b_ref, o_ref, lse_ref, s_ref, p_ref, o0_ref):
        i = pl.program_id(1)
        ks = pl.multiple_of(jnp.maximum(i * TQ - BAND, 0), TQ)
        lo = _lo_mask()
        q = q_ref[...]
        kwin = k_ref[pl.ds(ks, WIN), :]
        vwin = v_ref[pl.ds(ks, WIN), :]
        for half in range(2):
            m_ = lo if half == 0 else jnp.logical_not(lo)
            s_ref[...] = _nt(jnp.where(m_, q, jnp.zeros_like(q)), kwin)
            for r0 in range(0, TQ, RB_SOFT):
                rows = pl.ds(r0, RB_SOFT)
                sc = s_ref[rows, :] + b_ref[half, rows, :]
                mx = jnp.max(sc, axis=-1, keepdims=True)
                e = jnp.exp(sc - mx)
                den = jnp.sum(e, axis=-1, keepdims=True)
                p_ref[rows, :] = (e * (1.0 / den)).astype(BF16)
                lse = mx + jnp.log(den)
                if half == 0:
                    lse_ref[rows, :] = jnp.broadcast_to(lse, (RB_SOFT, LANE))
                else:
                    lse_ref[rows, :] = jnp.where(lo, lse_ref[rows, :], lse)
            o = _nn(p_ref[...], vwin)
            if half == 0:
                o0_ref[...] = o
            else:
                o_ref[...] = jnp.where(lo, o0_ref[...], o).astype(BF16)

    tile = pl.BlockSpec((TQ, LANE), lambda j, i: (i, j))
    return _hosted_call(
        body, xchg, name="attn_fwd", grid=(NH // 2, nq),
        in_specs=[
            tile,
            pl.BlockSpec((s, LANE), lambda j, i: (0, AW // LANE + j)),
            pl.BlockSpec((s, LANE), lambda j, i: (0, 2 * AW // LANE + j)),
            _bias_layer_spec(l),
        ],
        out_specs=[tile, tile],
        out_shape=[SDS((s, D), BF16), SDS((s, AW), F32)], args=[qkv, qkv, qkv, bias],
        scratch_shapes=[pltpu.VMEM((TQ, WIN), F32), pltpu.VMEM((TQ, WIN), BF16), pltpu.VMEM((TQ, LANE), F32)])


_C0 = 3 * AW // LANE


def _cp_in_specs(s, l):
    blk = lambda f: pl.BlockSpec((s, LANE), f)
    return [
        blk(lambda g: (0, _C0 + jnp.minimum(g, 1))),
        blk(lambda g: (0, _C0 + 2 + jnp.minimum(g, 1))),
        blk(lambda g: (0, _C0 + 4 + jnp.minimum(g, 1))),
        blk(lambda g: (0, _C0 + 6 + jnp.maximum(g - 2, 0))),
        pl.BlockSpec((None, 3, LANE), lambda g: (l, 0, jnp.minimum(g, 1))),
        pl.BlockSpec((None, None, LANE, LANE), lambda g: (l, jnp.maximum(g - 2, 0), 0, 0)),
        pl.BlockSpec((None, 1, LANE), lambda g: (l, 0, jnp.maximum(g - 2, 0))),
    ]


def _pool_window_sums(u_ref, buf_a, buf_b, jj, s, rt):
    nrt = s // rt
    for t in range(nrt):
        buf_a[pl.ds(PAD + t * rt, rt), :] = u_ref[pl.ds(t * rt, rt), :]

    def stage(src, dst, sh):
        for t in range(nrt):
            r0 = PAD + t * rt
            dst[pl.ds(r0, rt), :] = src[pl.ds(r0, rt), :] + src[pl.ds(r0 - sh, rt), :]

    stage(buf_a, buf_b, 1)
    stage(buf_b, buf_a, 2)

    @pl.when(jj == 1)
    def _():
        stage(buf_a, buf_b, 4)
        stage(buf_b, buf_a, 8)


def _pool_counts(jj, lo, r0, rt):
    w = jnp.where(lo, jnp.where(jj == 0, 2.0, 8.0), jnp.where(jj == 0, 4.0, 16.0))
    pos1 = (lax.broadcasted_iota(jnp.int32, (rt, LANE), 0) + (r0 + 1)).astype(F32)
    return jnp.minimum(pos1, w)


def _conv_pool_fwd(p, mix, conv_w, wbd, pscale, l):
    s = p.shape[0]
    rt = min(256, s)
    nrt = s // rt

    def body(gb_ref, gc_ref, hin_ref, u_ref, cw_ref, wbd_ref, ps_ref, mix_in, o_ref, buf_a, buf_b):
        del mix_in
        g = pl.program_id(0)
        zpad = jnp.zeros((PAD, LANE), F32)
        buf_a[pl.ds(0, PAD), :] = zpad
        buf_b[pl.ds(0, PAD), :] = zpad

        @pl.when(g < 2)
        def _conv():
            for t in range(nrt):
                buf_a[pl.ds(PAD + t * rt, rt), :] = gc_ref[pl.ds(t * rt, rt), :] * hin_ref[pl.ds(t * rt, rt), :]
            w0, w1, w2 = cw_ref[0:1, :], cw_ref[1:2, :], cw_ref[2:3, :]
            for t in range(nrt):
                r0 = PAD + t * rt
                y = w0 * buf_a[pl.ds(r0 - 2, rt), :] + w1 * buf_a[pl.ds(r0 - 1, rt), :] + w2 * buf_a[pl.ds(r0, rt), :]
                o_ref[pl.ds(t * rt, rt), :] = (gb_ref[pl.ds(t * rt, rt), :] * y).astype(BF16)

        @pl.when(g >= 2)
        def _pool():
            jj = g - 2
            lo = _lo_mask()
            _pool_window_sums(u_ref, buf_a, buf_b, jj, s, rt)
            wb = wbd_ref[...]
            for t in range(nrt):
                r0 = PAD + t * rt
                wsum = jnp.where(lo, buf_b[pl.ds(r0, rt), :], buf_a[pl.ds(r0, rt), :])
                m = wsum / _pool_counts(jj, lo, t * rt, rt) - u_ref[pl.ds(t * rt, rt), :]
                o_ref[pl.ds(t * rt, rt), :] = (_nn(m.astype(BF16), wb) * ps_ref[...]).astype(BF16)

    return pl.pallas_call(
        body, name="conv_pool_fwd", grid=(4,),
        in_specs=_cp_in_specs(s, l) + [pl.BlockSpec(memory_space=pl.ANY)],
        out_specs=pl.BlockSpec((s, LANE), lambda g: (0, AW // LANE + g)),
        out_shape=SDS((s, D), BF16),
        scratch_shapes=[pltpu.VMEM((s + 2 * PAD, LANE), F32), pltpu.VMEM((s + 2 * PAD, LANE), F32)],
        input_output_aliases={7: 0},
        compiler_params=_cp(),
    )(p, p, p, p, conv_w, wbd, pscale, mix)


def _mlp_fwd(x, mix, wout, g2, w1_t, w2, l, xchg=None):
    s = x.shape[0]
    t = min(256, s)

    def body(x_ref, mix_ref, wo_ref, g_ref, w1_ref, w2_ref, xm_ref, a_ref, xo_ref):
        xm = x_ref[...] + _nn(mix_ref[...], wo_ref[...])
        xm_ref[...] = xm
        r = lax.rsqrt(jnp.mean(xm * xm, axis=-1, keepdims=True) + EPS)
        h2 = (xm * r * g_ref[...]).astype(BF16)
        a = _nt(h2, w1_ref[...])
        a_ref[...] = a.astype(BF16)
        f = jnp.square(jnp.maximum(a, 0.0)).astype(BF16)
        xo_ref[...] = xm + _nn(f, w2_ref[...])

    row = lambda c: pl.BlockSpec((t, c), lambda i: (i, 0))
    return _hosted_call(
        body, xchg, name="mlp_fwd", grid=(s // t,),
        in_specs=[row(D), row(D), _layer((D, D), l), _layer((1, D), l), _layer((DFF, D), l), _layer((DFF, D), l)],
        out_specs=[row(D), row(DFF), row(D)],
        out_shape=[SDS((s, D), F32), SDS((s, DFF), BF16), SDS((s, D), F32)], args=[x, mix, wout, g2, w1_t, w2])


def _loss_grad(y, target):
    s = y.shape[0]
    t = min(512, s)

    def body(y_ref, t_ref, dy_ref, acc_ref):
        @pl.when(pl.program_id(0) == 0)
        def _():
            acc_ref[...] = jnp.zeros_like(acc_ref)
        e = y_ref[...] - t_ref[...]
        dy_ref[...] = e * (1.0 / D)
        acc_ref[...] += jnp.sum(e * e)

    row = pl.BlockSpec((t, D), lambda i: (i, 0))
    return pl.pallas_call(
        body, name="loss_grad", grid=(s // t,),
        in_specs=[row, row],
        out_specs=[row, pl.BlockSpec((8, LANE), lambda i: (0, 0))],
        out_shape=[SDS((s, D), F32), SDS((8, LANE), F32)],
        compiler_params=_cp(),
    )(y, target)


def _mlp_bwd(dxo, a, xm, g2, w1_t, w2, wout, l, xchg=None):
    s = dxo.shape[0]
    t = min(256, s)

    def body(dxo_ref, a_ref, xm_ref, g_ref, w1_ref, w2_ref, wo_ref,
             dxm_ref, dmix_ref, f_ref, da_ref, h2_ref, dxob_ref, dxmb_ref, dg_ref):
        @pl.when(pl.program_id(0) == 0)
        def _():
            dg_ref[...] = jnp.zeros_like(dg_ref)
        dxo = dxo_ref[...]
        dxob = dxo.astype(BF16)
        dxob_ref[...] = dxob
        ra = jnp.maximum(a_ref[...].astype(F32), 0.0)
        f_ref[...] = jnp.square(ra).astype(BF16)
        dab = (_nt(dxob, w2_ref[...]) * (2.0 * ra)).astype(BF16)
        da_ref[...] = dab
        dh2 = _nn(dab, w1_ref[...])
        xm = xm_ref[...]
        g = g_ref[...]
        r = lax.rsqrt(jnp.mean(xm * xm, axis=-1, keepdims=True) + EPS)
        h2_ref[...] = (xm * r * g).astype(BF16)
        dx_n, dgr = _rms_bwd(dh2, xm, g)
        dg_ref[...] += jnp.sum(dgr, axis=0, keepdims=True)
        dxm = dxo + dx_n
        dxm_ref[...] = dxm
        dxmb = dxm.astype(BF16)
        dxmb_ref[...] = dxmb
        dmix_ref[...] = _nt(dxmb, wo_ref[...])

    row = lambda c: pl.BlockSpec((t, c), lambda i: (i, 0))
    return _hosted_call(
        body, xchg, name="mlp_bwd", grid=(s // t,),
        in_specs=[row(D), row(DFF), row(D), _layer((1, D), l), _layer((DFF, D), l), _layer((DFF, D), l), _layer((D, D), l)],
        out_specs=[row(D), row(D), row(DFF), row(DFF), row(D), row(D), row(D), pl.BlockSpec((1, D), lambda i: (0, 0))],
        out_shape=[SDS((s, D), F32), SDS((s, D), F32), SDS((s, DFF), BF16), SDS((s, DFF), BF16),
                   SDS((s, D), BF16), SDS((s, D), BF16), SDS((s, D), BF16), SDS((1, D), F32)],
        args=[dxo, a, xm, g2, w1_t, w2, wout])


def _attn_bwd(qkv, p, lse, dmix, qg2, kg2, bias, l, xchg=None):
    s = p.shape[0]
    nq = s // TQ
    scale = HD ** -0.5
    rt = min(512, s)

    def body(qs_ref, kb_ref, vb_ref, q_ref, k_ref, qg_ref, kg_ref, b_ref, lse_ref, do_ref,
             dq_ref, dk_ref, dv_ref, db_ref, dqg_ref, dkg_ref,
             dk_acc, dv_acc, s_ref, dp_ref, ds_ref, pb_ref, dqn_ref):
        i = pl.program_id(1)
        ks = pl.multiple_of(jnp.maximum(i * TQ - BAND, 0), TQ)
        lo = _lo_mask()

        @pl.when(i == 0)
        def _():
            dk_acc[...] = jnp.zeros_like(dk_acc)
            dv_acc[...] = jnp.zeros_like(dv_acc)
            dqg_ref[...] = jnp.zeros_like(dqg_ref)
            dkg_ref[...] = jnp.zeros_like(dkg_ref)

        @pl.when(i < NVAR)
        def _():
            db_ref[...] = jnp.zeros_like(db_ref)

        qs = qs_ref[...]
        kwin = kb_ref[pl.ds(ks, WIN), :]
        vwin = vb_ref[pl.ds(ks, WIN), :]
        dob = do_ref[...].astype(BF16)
        for half in range(2):
            m_ = lo if half == 0 else jnp.logical_not(lo)
            qa = jnp.where(m_, qs, jnp.zeros_like(qs))
            doa = jnp.where(m_, dob, jnp.zeros_like(dob))
            s_ref[...] = _nt(qa, kwin)
            dp_ref[...] = _nt(doa, vwin)
            for r0 in range(0, TQ, RB_SOFT):
                rows = pl.ds(r0, RB_SOFT)
                lse_h = lse_ref[rows, half * HD:half * HD + 1]
                pm = jnp.exp(s_ref[rows, :] + b_ref[half, rows, :] - lse_h)
                dp = dp_ref[rows, :]
                delta = jnp.sum(pm * dp, axis=-1, keepdims=True)
                ds = pm * (dp - delta)
                db_ref[half, rows, :] += ds
                ds_ref[rows, :] = ds.astype(BF16)
                pb_ref[rows, :] = pm.astype(BF16)
            dsb = ds_ref[...]
            dq_h = _nn(dsb, kwin)
            if half == 0:
                dqn_ref[...] = dq_h
            else:
                dqn_ref[...] = jnp.where(lo, dqn_ref[...], dq_h)
            dk_acc[pl.ds(ks, WIN), :] += _tn(dsb, qa)
            dv_acc[pl.ds(ks, WIN), :] += _tn(pb_ref[...], doa)
        qg, kg = qg_ref[...], kg_ref[...]
        xq, rq = _head_norm(q_ref[...], lo)
        dq, dqg_rows = _head_norm_bwd(dqn_ref[...] * scale, xq, rq, qg, lo)
        dq_ref[...] = dq.astype(BF16)
        dqg_ref[...] += jnp.sum(dqg_rows, axis=0, keepdims=True)

        @pl.when(i == nq - 1)
        def _():
            dkg = jnp.zeros((1, LANE), F32)
            for t in range(s // rt):
                rows = pl.ds(t * rt, rt)
                xk, rk = _head_norm(k_ref[rows, :], lo)
                dk, dkg_rows = _head_norm_bwd(dk_acc[rows, :], xk, rk, kg, lo)
                dk_ref[rows, :] = dk.astype(BF16)
                dv_ref[rows, :] = dv_acc[rows, :].astype(BF16)
                dkg = dkg + jnp.sum(dkg_rows, axis=0, keepdims=True)
            dkg_ref[...] = dkg

    tile = pl.BlockSpec((TQ, LANE), lambda j, i: (i, j))
    kcol = lambda c0: pl.BlockSpec((s, LANE), lambda j, i: (0, c0 + j))
    gain = pl.BlockSpec((None, 1, LANE), lambda j, i: (j, 0, 0))
    return _hosted_call(
        body, xchg, name="attn_bwd", grid=(NH // 2, nq),
        in_specs=[
            tile, kcol(AW // LANE), kcol(2 * AW // LANE), tile, kcol(AW // LANE),
            _layer((1, LANE), l), _layer((1, LANE), l),
            _bias_layer_spec(l), tile, tile,
        ],
        out_specs=[tile, kcol(0), kcol(0), _bias_spec(), gain, gain],
        out_shape=[SDS((s, AW), BF16), SDS((s, AW), BF16), SDS((s, AW), BF16),
                   SDS((NVAR, NH, TQ, WIN), F32), SDS((NH // 2, 1, LANE), F32), SDS((NH // 2, 1, LANE), F32)],
        scratch_shapes=[pltpu.VMEM((s, LANE), F32), pltpu.VMEM((s, LANE), F32),
                        pltpu.VMEM((TQ, WIN), F32), pltpu.VMEM((TQ, WIN), F32),
                        pltpu.VMEM((TQ, WIN), BF16), pltpu.VMEM((TQ, WIN), BF16), pltpu.VMEM((TQ, LANE), F32)],
        args=[qkv, qkv, qkv, p, p, qg2, kg2, bias, lse, dmix])


def _conv_pool_bwd(p, dmix, conv_w, wbd, pscale, l):
    s = p.shape[0]
    rt = min(256, s)
    nrt = s // rt

    def body(gb_ref, gc_ref, hin_ref, u_ref, cw_ref, wbd_ref, ps_ref, dy_ref,
             dgb_ref, dgc_ref, dhin_ref, du_ref, dcw_ref, dwbd_ref, dps_ref, buf_a, buf_b, buf_c, buf_d):
        g = pl.program_id(0)
        zpad = jnp.zeros((PAD, LANE), F32)
        for buf in (buf_a, buf_b, buf_c):
            buf[pl.ds(0, PAD), :] = zpad
            buf[pl.ds(PAD + s, PAD), :] = zpad

        @pl.when(g < 2)
        def _conv():
            for t in range(nrt):
                rows = pl.ds(t * rt, rt)
                buf_a[pl.ds(PAD + t * rt, rt), :] = gc_ref[rows, :] * hin_ref[rows, :]
                buf_b[pl.ds(PAD + t * rt, rt), :] = dy_ref[rows, :] * gb_ref[rows, :]
            w0, w1, w2 = cw_ref[0:1, :], cw_ref[1:2, :], cw_ref[2:3, :]
            d0 = jnp.zeros((1, LANE), F32)
            d1 = jnp.zeros((1, LANE), F32)
            d2 = jnp.zeros((1, LANE), F32)
            for t in range(nrt):
                rows = pl.ds(t * rt, rt)
                r0 = PAD + t * rt
                z2, z1, z0 = buf_a[pl.ds(r0 - 2, rt), :], buf_a[pl.ds(r0 - 1, rt), :], buf_a[pl.ds(r0, rt), :]
                y = w0 * z2 + w1 * z1 + w2 * z0
                dgb_ref[rows, :] = (dy_ref[rows, :] * y).astype(BF16)
                e0 = buf_b[pl.ds(r0, rt), :]
                d0 = d0 + jnp.sum(e0 * z2, axis=0, keepdims=True)
                d1 = d1 + jnp.sum(e0 * z1, axis=0, keepdims=True)
                d2 = d2 + jnp.sum(e0 * z0, axis=0, keepdims=True)
                dz = w2 * e0 + w1 * buf_b[pl.ds(r0 + 1, rt), :] + w0 * buf_b[pl.ds(r0 + 2, rt), :]
                dgc_ref[rows, :] = (dz * hin_ref[rows, :]).astype(BF16)
                dhin_ref[rows, :] = (dz * gc_ref[rows, :]).astype(BF16)
            dcw_ref[0:1, :] = d0
            dcw_ref[1:2, :] = d1
            dcw_ref[2:3, :] = d2

        @pl.when(g >= 2)
        def _pool():
            jj = g - 2
            lo = _lo_mask()
            _pool_window_sums(u_ref, buf_a, buf_b, jj, s, rt)
            wb = wbd_ref[...]
            ps = ps_ref[...]
            dps = jnp.zeros((1, LANE), F32)
            dwb = jnp.zeros((LANE, LANE), F32)
            for t in range(nrt):
                rows = pl.ds(t * rt, rt)
                r0 = PAD + t * rt
                cnt = _pool_counts(jj, lo, t * rt, rt)
                wsum = jnp.where(lo, buf_b[pl.ds(r0, rt), :], buf_a[pl.ds(r0, rt), :])
                mb = (wsum / cnt - u_ref[rows, :]).astype(BF16)
                dy = dy_ref[rows, :]
                dps = dps + jnp.sum(dy * _nn(mb, wb), axis=0, keepdims=True)
                dmp = (dy * ps).astype(BF16)
                dwb = dwb + _tn(mb, dmp)
                dm = _nt(dmp, wb)
                buf_d[rows, :] = dm
                buf_c[pl.ds(r0, rt), :] = dm / cnt
            dps_ref[...] = dps
            dwbd_ref[...] = dwb

            def stage(src, dst, sh):
                for t in range(nrt):
                    r0 = PAD + t * rt
                    dst[pl.ds(r0, rt), :] = src[pl.ds(r0, rt), :] + src[pl.ds(r0 + sh, rt), :]

            def finish(first, second):
                for t in range(nrt):
                    rows = pl.ds(t * rt, rt)
                    r0 = PAD + t * rt
                    fw = jnp.where(lo, first[pl.ds(r0, rt), :], second[pl.ds(r0, rt), :])
                    du_ref[rows, :] = (fw - buf_d[rows, :]).astype(BF16)

            stage(buf_c, buf_a, 1)
            stage(buf_a, buf_b, 2)

            @pl.when(jj == 0)
            def _():
                finish(buf_a, buf_b)

            @pl.when(jj == 1)
            def _():
                stage(buf_b, buf_c, 4)
                stage(buf_c, buf_a, 8)
                finish(buf_c, buf_a)

    cblk = pl.BlockSpec((s, LANE), lambda g: (0, jnp.minimum(g, 1)))
    pblk = pl.BlockSpec((s, LANE), lambda g: (0, jnp.maximum(g - 2, 0)))
    padded = pltpu.VMEM((s + 2 * PAD, LANE), F32)
    return pl.pallas_call(
        body, name="conv_pool_bwd", grid=(4,),
        in_specs=_cp_in_specs(s, l) + [pl.BlockSpec((s, LANE), lambda g: (0, AW // LANE + g))],
        out_specs=[cblk, cblk, cblk, pblk,
                   pl.BlockSpec((3, LANE), lambda g: (0, jnp.minimum(g, 1))),
                   pl.BlockSpec((None, LANE, LANE), lambda g: (jnp.maximum(g - 2, 0), 0, 0)),
                   pl.BlockSpec((1, LANE), lambda g: (0, jnp.maximum(g - 2, 0)))],
        out_shape=[SDS((s, CW), BF16), SDS((s, CW), BF16), SDS((s, CW), BF16), SDS((s, PWD), BF16),
                   SDS((3, CW), F32), SDS((2, LANE, LANE), F32), SDS((1, PWD), F32)],
        scratch_shapes=[padded, padded, padded, pltpu.VMEM((s, LANE), F32)],
        compiler_params=_cp(),
    )(p, p, p, p, conv_w, wbd, pscale, dmix)


def _in_proj_bwd(parts, x, dxm, g1, win_t, l):
    s = x.shape[0]
    t = min(256, s)
    widths = [a.shape[1] for a in parts]
    offs = [int(o) for o in np.cumsum([0] + widths[:-1])]
    n = len(parts)

    def body(*refs):
        part_refs = refs[:n]
        x_ref, dxm_ref, g_ref, w_ref, dx_ref, dp_ref, dg_ref = refs[n:]

        @pl.when(pl.program_id(0) == 0)
        def _():
            dg_ref[...] = jnp.zeros_like(dg_ref)
        for r, o, w in zip(part_refs, offs, widths):
            dp_ref[:, o:o + w] = r[...]
        dh = _nn(dp_ref[...], w_ref[...])
        dx_n, dgr = _rms_bwd(dh, x_ref[...], g_ref[...])
        dg_ref[...] += jnp.sum(dgr, axis=0, keepdims=True)
        dx_ref[...] = dxm_ref[...] + dx_n

    row = lambda c: pl.BlockSpec((t, c), lambda i: (i, 0))
    return pl.pallas_call(
        body, name="in_proj_bwd", grid=(s // t,),
        in_specs=[row(w) for w in widths] + [row(D), row(D), _layer((1, D), l), _layer((DIN, D), l)],
        out_specs=[row(D), row(DIN), pl.BlockSpec((1, D), lambda i: (0, 0))],
        out_shape=[SDS((s, D), F32), SDS((s, DIN), BF16), SDS((1, D), F32)],
        compiler_params=_cp(),
    )(*parts, x, dxm, g1, win_t)


def _wgrad(a, b, stacked, layer, tag):
    s, m = a.shape
    mb = 512
    t = min(1024, s)
    nt = s // t

    def body(*refs):
        a_ref, b_ref = refs[:2]
        o_ref, acc = refs[-2:]

        @pl.when(pl.program_id(1) == 0)
        def _():
            acc[...] = jnp.zeros_like(acc)
        acc[...] += _tn(a_ref[...], b_ref[...])

        @pl.when(pl.program_id(1) == nt - 1)
        def _():
            o_ref[...] = acc[...].astype(BF16)

    in_specs = [pl.BlockSpec((t, mb), lambda mi, ti: (ti, mi)), pl.BlockSpec((t, D), lambda mi, ti: (ti, 0))]
    args = [a, b]
    aliases = {}
    if stacked is not None:
        in_specs.append(pl.BlockSpec(memory_space=pl.ANY))
        args.append(stacked)
        aliases = {2: 0}
    return pl.pallas_call(
        body, name=f"wgrad_{tag}_l{layer}", grid=(m // mb, nt),
        in_specs=in_specs,
        out_specs=pl.BlockSpec((None, mb, D), lambda mi, ti: (layer, mi, 0)),
        out_shape=SDS((L, m, D), BF16),
        scratch_shapes=[pltpu.VMEM((mb, D), F32)],
        input_output_aliases=aliases,
        compiler_params=_cp(),
    )(*args)


def _bias_tables(gvec):
    def body(g_ref, o_ref):
        qc = lax.broadcasted_iota(jnp.int32, (TQ, WIN), 0) // CHUNK
        kc = lax.broadcasted_iota(jnp.int32, (TQ, WIN), 1) // CHUNK
        for var in range(NVAR):
            vec = jnp.broadcast_to(g_ref[:, var * TQ:var * TQ + NTOE], (TQ, NTOE))
            toe = pltpu.roll(vec, NTOE - TQ + 1, 1, stride=1, stride_axis=0)[:, :WIN]
            rel = (BAND - var * TQ) // CHUNK + qc - kc
            o_ref[var] = jnp.where((rel >= 0) & (rel <= N_PREV), toe, NEG)

    return pl.pallas_call(
        body, name="bias_tables", grid=(L, NH),
        in_specs=[pl.BlockSpec((None, None, 1, NG), lambda l, h: (l, h, 0, 0))],
        out_specs=pl.BlockSpec((None, NVAR, None, TQ, WIN), lambda l, h: (l, 0, h, 0, 0)),
        out_shape=SDS((L, NVAR, NH, TQ, WIN), F32),
        compiler_params=_cp(),
    )(gvec)


def _bias_tables_grad(dbias, l):
    nb = NTOE // LANE
    wb = WIN // LANE

    def body(d_ref, o_ref):
        ii = lax.broadcasted_iota(jnp.int32, (LANE, LANE), 0)
        jj = lax.broadcasted_iota(jnp.int32, (LANE, LANE), 1)
        flip = jnp.where(ii + jj == LANE - 1, 1.0, 0.0).astype(BF16)
        o_ref[...] = jnp.zeros_like(o_ref)
        for var in range(NVAR):
            blocks = []
            for b in range(nb):
                src = nb - 1 - b
                if src >= wb:
                    blocks.append(jnp.zeros((TQ, LANE), F32))
                    continue
                xv = d_ref[var, :, src * LANE:(src + 1) * LANE]
                hi = xv.astype(BF16)
                lo = (xv - hi.astype(F32)).astype(BF16)
                blocks.append(_nn(hi, flip) + _nn(lo, flip))
            rev = jnp.concatenate(blocks, axis=1)
            skew = pltpu.roll(rev, NTOE - TQ + 1, 1, stride=1, stride_axis=0)
            off = NG - NTOE - var * TQ
            o_ref[:, off:off + NTOE] += jnp.sum(skew, axis=0, keepdims=True)

    return pl.pallas_call(
        body, name=f"bias_tables_grad_l{l}", grid=(NH,),
        in_specs=[pl.BlockSpec((NVAR, None, TQ, WIN), lambda h: (0, h, 0, 0))],
        out_specs=pl.BlockSpec((None, 1, NG), lambda h: (h, 0, 0)),
        out_shape=SDS((NH, 1, NG), F32),
        compiler_params=_cp(),
    )(dbias)


_MASKS = [(mx, my, mc) for mx in (0, 1) for my in (0, 1) for mc in (0, 1)][1:]


def _position():
    return lax.axis_index("x"), lax.axis_index("y"), lax.axis_index("c")


def _peer(pos, mask):
    return tuple(1 - a if f else a for a, f in zip(pos, mask))


def _index(pos):
    return 4 * pos[0] + 2 * pos[1] + pos[2]


def _exchange_plan(kind, items, src, dst, sems, receives):
    send_sems, recv_sems, local_sems = sems
    me = _position()

    def rows(k, pos):
        return pl.ds(_index(pos) * SHARD_ROWS[k], SHARD_ROWS[k])

    local, sends, recvs = [], [], []
    for n, (k, l) in enumerate(items):
        if kind == "gather":
            src_for = lambda pos, k=k, l=l: src[k].at[l]
            land = lambda pos, k=k, l=l: dst[k].at[l, rows(k, pos), :]
        else:
            src_for = lambda pos, k=k, l=l: src[k].at[l, rows(k, pos), :]
            land = lambda pos, k=k, l=l: dst[k].at[_index(pos), l]
        local.append(pltpu.make_async_copy(src_for(me), land(me), local_sems.at[n]))
        for pi, mask in enumerate(_MASKS):
            peer = _peer(me, mask)
            for lst, where in ((sends, me), (recvs, peer)) if receives else ((sends, me),):
                lst.append(pltpu.make_async_remote_copy(
                    src_ref=src_for(peer), dst_ref=land(where), send_sem=send_sems.at[pi, n],
                    recv_sem=recv_sems.at[pi, n], device_id=peer, device_id_type=MESH_ID))
    return local, sends, recvs


def _hosted_call(body, xchg, *, name, grid, in_specs, out_specs, out_shape, args, scratch_shapes=()):
    if xchg is None:
        outs = pl.pallas_call(
            body, name=name, grid=grid, in_specs=list(in_specs), out_specs=list(out_specs),
            out_shape=list(out_shape), scratch_shapes=list(scratch_shapes), compiler_params=_cp())(*args)
        return outs, {}
    kind, items, src, dst = xchg
    ks = sorted({k for k, _ in items})
    given = [k for k in ks if dst.get(k) is not None]
    n_in, n_out, n_scr, nk, ng = len(args), len(out_shape), len(scratch_shapes), len(ks), len(given)
    hbm = pl.BlockSpec(memory_space=pl.ANY)

    def dst_shape(k):
        r = SHARD_ROWS[k]
        return SDS((L, NDEV * r, D) if kind == "gather" else (NDEV, L, r, D), BF16)

    def wrapped(*refs):
        ins = refs[:n_in]
        xsrc = dict(zip(ks, refs[n_in:n_in + nk]))
        o0 = n_in + nk + ng
        outs = refs[o0:o0 + n_out]
        xdst = dict(zip(ks, refs[o0 + n_out:o0 + n_out + nk]))
        s0 = o0 + n_out + nk
        scratch = refs[s0:s0 + n_scr]
        sems = refs[s0 + n_scr:]
        first, last = True, True
        for d, g in enumerate(grid):
            first = jnp.logical_and(first, pl.program_id(d) == 0)
            last = jnp.logical_and(last, pl.program_id(d) == g - 1)

        @pl.when(first)
        def _():
            local, sends, _ = _exchange_plan(kind, items, xsrc, xdst, sems, False)
            for cp in local + sends:
                cp.start()

        body(*ins, *outs, *scratch)

        @pl.when(last)
        def _():
            local, sends, recvs = _exchange_plan(kind, items, xsrc, xdst, sems, True)
            for cp in recvs:
                cp.wait_recv()
            for cp in sends:
                cp.wait_send()
            for cp in local:
                cp.wait()

    npeer, nit = len(_MASKS), len(items)
    res = pl.pallas_call(
        wrapped, name=name, grid=grid,
        in_specs=list(in_specs) + [hbm] * (nk + ng),
        out_specs=list(out_specs) + [hbm] * nk,
        out_shape=list(out_shape) + [dst_shape(k) for k in ks],
        scratch_shapes=list(scratch_shapes) + [
            pltpu.SemaphoreType.DMA((npeer, nit)), pltpu.SemaphoreType.DMA((npeer, nit)), pltpu.SemaphoreType.DMA((nit,))],
        input_output_aliases={n_in + nk + j: n_out + ks.index(k) for j, k in enumerate(given)},
        compiler_params=_cp(),
    )(*args, *[src[k] for k in ks], *[dst[k] for k in given])
    return list(res[:n_out]), dict(zip(ks, res[n_out:]))


def _gather_first(shards):
    def body(o_ref):
        o_ref[...] = jnp.zeros_like(o_ref)

    _, got = _hosted_call(
        body, ("gather", [(0, 0)], {0: shards[0]}, {}), name="gather_first", grid=(1,),
        in_specs=[], out_specs=[pl.BlockSpec((8, LANE), lambda i: (0, 0))], out_shape=[SDS((8, LANE), F32)], args=[])
    return got[0]


def _sum_slots(slots, xchg=None):
    _, _, r, _ = slots.shape
    rt = 64

    def body(in_ref, o_ref):
        acc = in_ref[0].astype(F32)
        for d in range(1, NDEV):
            acc = acc + in_ref[d].astype(F32)
        o_ref[...] = acc

    (out,), got = _hosted_call(
        body, xchg, name=f"sum_slots_r{r}" + ("_x" if xchg else ""), grid=(L, r // rt),
        in_specs=[pl.BlockSpec((NDEV, None, rt, D), lambda l, i: (0, l, i, 0))],
        out_specs=[pl.BlockSpec((None, rt, D), lambda l, i: (l, i, 0))],
        out_shape=[SDS((L, r, D), F32)], args=[slots])
    return out, got


def _exchange_small(v, reduce):
    rows = v.shape[0]

    def body(v_ref, o_ref, *scratch):
        if reduce:
            slots, send_sems, recv_sems = scratch
        else:
            slots = o_ref
            send_sems, recv_sems = scratch
        me = _position()
        slots[_index(me)] = v_ref[...]
        sends = []
        for pi, mask in enumerate(_MASKS):
            cp = pltpu.make_async_remote_copy(
                src_ref=v_ref, dst_ref=slots.at[_index(me)], send_sem=send_sems.at[pi], recv_sem=recv_sems.at[pi],
                device_id=_peer(me, mask), device_id_type=MESH_ID)
            cp.start()
            sends.append(cp)
        for pi, mask in enumerate(_MASKS):
            peer = _peer(me, mask)
            pltpu.make_async_remote_copy(
                src_ref=v_ref, dst_ref=slots.at[_index(peer)], send_sem=send_sems.at[pi], recv_sem=recv_sems.at[pi],
                device_id=peer, device_id_type=MESH_ID).wait_recv()
        for cp in sends:
            cp.wait_send()
        if reduce:
            acc = slots[0]
            for d in range(1, NDEV):
                acc = acc + slots[d]
            o_ref[...] = acc

    vm = pl.BlockSpec(memory_space=pltpu.VMEM)
    sems = [pltpu.SemaphoreType.DMA((len(_MASKS),)), pltpu.SemaphoreType.DMA((len(_MASKS),))]
    return pl.pallas_call(
        body, name="reduce_small" if reduce else "gather_small",
        in_specs=[vm], out_specs=vm,
        out_shape=SDS((rows, LANE) if reduce else (NDEV, rows, LANE), F32),
        scratch_shapes=([pltpu.VMEM((NDEV, rows, LANE), F32)] if reduce else []) + sems,
        compiler_params=_cp(),
    )(v)


def _adamw(w, g, m, v):
    rows, cols = w.shape
    t = rows
    for cand in (512, 256, 128, 64, 32, 16, 8):
        if rows % cand == 0:
            t = cand
            break

    def body(w_ref, g_ref, m_ref, v_ref, d_ref, nm_ref, nv_ref):
        gv = g_ref[...]
        mn = B1 * m_ref[...] + (1.0 - B1) * gv
        vn = B2 * v_ref[...] + (1.0 - B2) * jnp.square(gv)
        nm_ref[...] = mn
        nv_ref[...] = vn
        m_hat = mn / (1.0 - B1 ** STEP)
        v_hat = vn / (1.0 - B2 ** STEP)
        d_ref[...] = -LR * (m_hat / (jnp.sqrt(v_hat) + AEPS) + WD * w_ref[...])

    blk = pl.BlockSpec((t, cols), lambda i: (i, 0))
    return pl.pallas_call(
        body, name=f"adamw_{rows}x{cols}", grid=(rows // t,),
        in_specs=[blk] * 4, out_specs=[blk] * 3,
        out_shape=[SDS((rows, cols), F32)] * 3,
        compiler_params=_cp(),
    )(w, g, m, v)


_DIST0 = BAND + TQ - 1
_N_FAR = _DIST0 - REL_CLIP + 1
_N_NEAR = NG - _N_FAR - (2 * REL_CLIP - 1)


def _bias_vector(rel_bias):
    far = jnp.broadcast_to(rel_bias[..., -1:], (L, NH, _N_FAR))
    near = jnp.broadcast_to(rel_bias[..., :1], (L, NH, _N_NEAR))
    return jnp.concatenate([far, lax.rev(rel_bias[..., 1:-1], (2,)), near], axis=2)[:, :, None, :]


def _bias_vector_grad(dgr):
    first = jnp.sum(dgr[..., :_N_NEAR], axis=-1, keepdims=True)
    last = jnp.sum(dgr[..., NG - _N_FAR:], axis=-1, keepdims=True)
    return jnp.concatenate([first, dgr[..., _N_NEAR:NG - _N_FAR], last], axis=-1)


def _pool_blockdiag(pool_w):
    eye = jnp.eye(2, dtype=F32)
    pw = pool_w.reshape(L, 2, 2, HD, HD)
    return jnp.einsum("ljaik,ab->ljaibk", pw, eye).reshape(L, 2, LANE, LANE)


def _pool_blockdiag_grad(dwbd):
    d = dwbd.reshape(L, 2, 2, HD, 2, HD)
    return jnp.stack([d[:, :, 0, :, 0, :], d[:, :, 1, :, 1, :]], axis=2).reshape(L, 4, HD, HD)


def _pack(arrays, rows):
    flat = jnp.concatenate([a.reshape(-1).astype(F32) for a in arrays])
    return jnp.pad(flat, (0, rows * LANE - flat.shape[0])).reshape(rows, LANE)


def _unpack(packed, shapes):
    flat = packed.reshape(-1)
    out, o = [], 0
    for shp in shapes:
        n = int(np.prod(shp))
        out.append(flat[o:o + n].reshape(shp))
        o += n
    return out


def _rows_for(shapes):
    n = sum(int(np.prod(s)) for s in shapes)
    return -(-n // (8 * LANE)) * 8


def _grads(x, target, small_w, shards):
    g1, qg, kg, rb, cw, pw, ps, g2 = small_w
    g1 = g1.reshape(L, 1, D)
    g2 = g2.reshape(L, 1, D)
    qg2 = jnp.tile(qg, (1, 2)).reshape(L, 1, LANE)
    kg2 = jnp.tile(kg, (1, 2)).reshape(L, 1, LANE)
    ps3 = ps.reshape(L, 1, PWD)
    bias = _bias_tables(_bias_vector(rb))
    wbd = _pool_blockdiag(pw).astype(BF16)
    src = dict(enumerate(shards))

    def gather(items, full):
        return ("gather", items, {k: src[k] for k, _ in items}, {k: full.get(k) for k, _ in items})

    full = {0: _gather_first(shards)}
    saved = []
    h = x
    for l in range(L):
        (p, h_b, qkv), got = _in_proj(h, g1, full[0], qg2, kg2, l, gather([(1, l)], full) if l else None)
        full.update(got)
        (mix, lse), got = _attn_fwd(qkv, bias, l, gather([(2, l), (3, l)] + ([] if l else [(1, 0)]), full))
        full.update(got)
        mix = _conv_pool_fwd(p, mix, cw, wbd, ps3, l)
        (xm, a, xo), got = _mlp_fwd(h, mix, full[1], g2, full[2], full[3], l,
                                    gather([(0, l + 1)], full) if l + 1 < L else None)
        full.update(got)
        saved.append((h, h_b, p, qkv, mix, lse, xm, a))
        h = xo
    dx, sq = _loss_grad(h, target)

    stacks = [None] * 4
    slots = {}

    def scatter(items):
        return ("scatter", items, {k: stacks[k] for k, _ in items}, {k: slots.get(k) for k, _ in items})

    per_layer = [None] * L
    for l in reversed(range(L)):
        x_in, h_b, p, qkv, mix, lse, xm, a = saved[l]
        (dxm, dmix, f_b, da_b, h2_b, dxo_b, dxm_b, dg2), got = _mlp_bwd(
            dx, a, xm, g2, full[2], full[3], full[1], l, scatter([(0, l + 1), (1, l + 1)]) if l + 1 < L else None)
        slots.update(got)
        stacks[1] = _wgrad(mix, dxm_b, stacks[1], l, "w_out")
        stacks[2] = _wgrad(da_b, h2_b, stacks[2], l, "w_mlp1")
        stacks[3] = _wgrad(f_b, dxo_b, stacks[3], l, "w_mlp2")
        (dq, dk, dv, dbias, dqg, dkg), got = _attn_bwd(qkv, p, lse, dmix, qg2, kg2, bias, l, scatter([(2, l), (3, l)]))
        slots.update(got)
        dgb, dgc, dhin, du, dcw, dwbd, dps = _conv_pool_bwd(p, dmix, cw, wbd, ps3, l)
        dx, dp_b, dg1 = _in_proj_bwd([dq, dk, dv, dgb, dgc, dhin, du], x_in, dxm, g1, full[0], l)
        stacks[0] = _wgrad(dp_b, h_b, stacks[0], l, "w_in")
        per_layer[l] = (dg1, dg2, dqg, dkg, _bias_tables_grad(dbias, l), dcw, dwbd, dps)
    g_w1_t, got = _sum_slots(slots[2], scatter([(0, 0), (1, 0)]))
    slots.update(got)
    sums = [_sum_slots(slots[0])[0], _sum_slots(slots[1])[0], g_w1_t, _sum_slots(slots[3])[0]]

    st = [jnp.stack([per_layer[l][k] for l in range(L)]) for k in range(8)]
    small = dict(
        g1=st[0].reshape(L, D), g2=st[1].reshape(L, D),
        qg=st[2].reshape(L, NH, HD).sum(1), kg=st[3].reshape(L, NH, HD).sum(1),
        rb=_bias_vector_grad(st[4].reshape(L, NH, NG)), cw=st[5], pw=_pool_blockdiag_grad(st[6]),
        ps=st[7].reshape(L, PWD))
    return sq, dx, sums, small


def kernel(x, norm1_g, w_in, q_norm_g, k_norm_g, rel_bias, conv_w, pool_w, pool_scale, w_out, norm2_g, w_mlp1, w_mlp2, loss_target, m_norm1_g, m_w_in, m_q_norm_g, m_k_norm_g, m_rel_bias, m_conv_w, m_pool_w, m_pool_scale, m_w_out, m_norm2_g, m_w_mlp1, m_w_mlp2, v_norm1_g, v_w_in, v_q_norm_g, v_k_norm_g, v_rel_bias, v_conv_w, v_pool_w, v_pool_scale, v_w_out, v_norm2_g, v_w_mlp1, v_w_mlp2):
    me = _index(_position())
    cshard = CW // NDEV

    shards = [jnp.swapaxes(w_in, 1, 2).astype(BF16), w_out.astype(BF16),
              jnp.swapaxes(w_mlp1, 1, 2).astype(BF16), w_mlp2.astype(BF16)]
    cw_all = _exchange_small(_pack([conv_w], 8), reduce=False)
    cw_full = jnp.concatenate(
        [cw_all[d].reshape(-1)[:L * 3 * cshard].reshape(L, 3, cshard) for d in range(NDEV)], axis=2)

    small_w = (norm1_g, q_norm_g, k_norm_g, rel_bias, cw_full, pool_w, pool_scale, norm2_g)
    sq, grad_x, (g_win_t, g_wout, g_w1_t, g_w2), small = _grads(x[0], loss_target[0], small_w, shards)
    g_w_in = jnp.swapaxes(g_win_t, 1, 2)
    g_w_mlp1 = jnp.swapaxes(g_w1_t, 1, 2)

    names = ("g1", "qg", "kg", "rb", "cw", "pw", "ps", "g2")
    gshapes = [(L, D), (L, HD), (L, HD), (L, NH, 2 * REL_CLIP + 1), (L, 3, CW), (L, 4, HD, HD), (L, PWD), (L, D)]
    garrs = [small[n] for n in names]
    rows = _rows_for(gshapes + [(1,)])
    total = _exchange_small(_pack(garrs + [sq[0, :1]], rows), reduce=True)
    g_g1, g_qg, g_kg, g_rb, g_cw_full, g_pw, g_ps, g_g2, sq_sum = _unpack(total, gshapes + [(1,)])
    loss = (0.5 / D) * sq_sum[0]
    g_cw = lax.dynamic_slice_in_dim(g_cw_full, me * cshard, cshard, axis=2)

    def big(w, g, m, v):
        shp = w.shape
        r = lambda a: a.reshape(-1, shp[-1])
        return [o.reshape(shp) for o in _adamw(r(w), r(g), r(m), r(v))]

    up_in = big(w_in, g_w_in, m_w_in, v_w_in)
    up_out = big(w_out, g_wout, m_w_out, v_w_out)
    up_1 = big(w_mlp1, g_w_mlp1, m_w_mlp1, v_w_mlp1)
    up_2 = big(w_mlp2, g_w2, m_w_mlp2, v_w_mlp2)

    sw = [norm1_g, q_norm_g, k_norm_g, rel_bias, conv_w, pool_w, pool_scale, norm2_g]
    sg = [g_g1, g_qg, g_kg, g_rb, g_cw, g_pw, g_ps, g_g2]
    sm = [m_norm1_g, m_q_norm_g, m_k_norm_g, m_rel_bias, m_conv_w, m_pool_w, m_pool_scale, m_norm2_g]
    sv = [v_norm1_g, v_q_norm_g, v_k_norm_g, v_rel_bias, v_conv_w, v_pool_w, v_pool_scale, v_norm2_g]
    sshapes = [a.shape for a in sw]
    srows = _rows_for(sshapes)
    ups = _adamw(_pack(sw, srows), _pack(sg, srows), _pack(sm, srows), _pack(sv, srows))
    s_delta, s_m, s_v = [_unpack(u, sshapes) for u in ups]

    def order(small_list, in_, out_, m1, m2):
        g1_, qg_, kg_, rb_, cw_, pw_, ps_, g2_ = small_list
        return [g1_, in_, qg_, kg_, rb_, cw_, pw_, ps_, out_, g2_, m1, m2]

    grads = order(sg, g_w_in, g_wout, g_w_mlp1, g_w2)
    deltas = order(s_delta, up_in[0], up_out[0], up_1[0], up_2[0])
    new_m = order(s_m, up_in[1], up_out[1], up_1[1], up_2[1])
    new_v = order(s_v, up_in[2], up_out[2], up_1[2], up_2[2])
    return (loss, grad_x[None], *grads, *deltas, *new_m, *new_v)
```

```python
import numpy as np
import jax
import jax.numpy as jnp
from jax import lax
from jax.experimental import pallas as pl
from jax.experimental.pallas import tpu as pltpu

F32 = jnp.float32
BF16 = jnp.bfloat16
SDS = jax.ShapeDtypeStruct
MESH_ID = pl.DeviceIdType.MESH

D = 1024
L = 4
CHUNK = 64
N_PREV = 8
HD = 64
NH = 8
AW = 512
CW = 256
PWD = 256
DIN = 2560
DFF = 4096
EPS = 1e-6
NEG = -1e30
REL_CLIP = 128
POOL_WINDOWS = (2, 4, 8, 16)
LR, B1, B2, AEPS, WD, STEP = 0.001, 0.9, 0.999, 1e-08, 0.01, 10

NDEV = 8
LANE = 128
BAND = N_PREV * CHUNK
TQ = 256
WIN = TQ + BAND
NVAR = BAND // TQ + 1
NTOE = -(-(WIN + TQ - 1) // LANE) * LANE
NG = (NVAR - 1) * TQ + NTOE
PAD = 16
RB_NORM = 64
RB_SOFT = 16
VMEM_LIMIT = 56 * 1024 * 1024
SHARD_ROWS = (DIN // NDEV, D // NDEV, DFF // NDEV, DFF // NDEV)


def _cp(**kw):
    return pltpu.CompilerParams(vmem_limit_bytes=VMEM_LIMIT, **kw)


def _nn(a, b):
    return jnp.dot(a, b, preferred_element_type=F32)


def _nt(a, b):
    return lax.dot_general(a, b, (((1,), (1,)), ((), ())), preferred_element_type=F32)


def _tn(a, b):
    return lax.dot_general(a, b, (((0,), (0,)), ((), ())), preferred_element_type=F32)


def _const(shape):
    n = len(shape)
    return pl.BlockSpec(shape, lambda *_: (0,) * n, pipeline_mode=pl.Buffered(1))


def _layer(shape, l):
    n = len(shape)
    return pl.BlockSpec((None,) + tuple(shape), lambda *_: (l,) + (0,) * n, pipeline_mode=pl.Buffered(1))


def _lo_mask():
    return lax.broadcasted_iota(jnp.int32, (1, LANE), 1) < HD


def _half_sum(t, lo):
    s_lo = jnp.sum(jnp.where(lo, t, 0.0), axis=-1, keepdims=True)
    s_hi = jnp.sum(jnp.where(lo, 0.0, t), axis=-1, keepdims=True)
    return jnp.where(lo, s_lo, s_hi)


def _head_norm(x, lo):
    r = lax.rsqrt(_half_sum(x * x, lo) * (1.0 / HD) + EPS)
    return x * r, r


def _head_norm_bwd(dy, xn, r, g, lo):
    dxn = dy * g
    mu = _half_sum(dxn * xn, lo) * (1.0 / HD)
    return r * (dxn - xn * mu), dy * xn


def _rms_bwd(dy, x, g):
    r = lax.rsqrt(jnp.mean(x * x, axis=-1, keepdims=True) + EPS)
    xn = x * r
    dxn = dy * g
    mu = jnp.mean(dxn * xn, axis=-1, keepdims=True)
    return r * (dxn - xn * mu), dy * xn


def _in_proj(x, g1, win_t, qg2, kg2, l, xchg=None):
    s = x.shape[0]
    t = min(512, s)
    nblk = AW // LANE

    def body(x_ref, g_ref, w_ref, qg_ref, kg_ref, p_ref, h_ref, qkv_ref):
        xv = x_ref[...]
        r = lax.rsqrt(jnp.mean(xv * xv, axis=-1, keepdims=True) + EPS)
        h = (xv * r * g_ref[...]).astype(BF16)
        h_ref[...] = h
        p_ref[...] = _nt(h, w_ref[...])
        lo = _lo_mask()
        gains = (qg_ref[...] * (HD ** -0.5), kg_ref[...])
        for r0 in range(0, t, RB_NORM):
            rows = pl.ds(r0, RB_NORM)
            for c in range(3 * nblk):
                cols = pl.ds(c * LANE, LANE)
                v = p_ref[rows, cols]
                if c < 2 * nblk:
                    v = _head_norm(v, lo)[0] * gains[c // nblk]
                qkv_ref[rows, cols] = v.astype(BF16)

    row = lambda c: pl.BlockSpec((t, c), lambda i: (i, 0))
    return _hosted_call(
        body, xchg, name="in_proj", grid=(s // t,),
        in_specs=[row(D), _layer((1, D), l), _const((DIN, D)), _layer((1, LANE), l), _layer((1, LANE), l)],
        out_specs=[row(DIN), row(D), row(3 * AW)],
        out_shape=[SDS((s, DIN), F32), SDS((s, D), BF16), SDS((s, 3 * AW), BF16)], args=[x, g1, win_t, qg2, kg2])


def _bias_spec():
    return pl.BlockSpec((None, 2, TQ, WIN), lambda j, i: (jnp.maximum(NVAR - 1 - i, 0), j, 0, 0))


def _bias_layer_spec(l):
    return pl.BlockSpec((None, None, 2, TQ, WIN), lambda j, i: (l, jnp.maximum(NVAR - 1 - i, 0), j, 0, 0))


def _attn_fwd(qkv, bias, l, xchg=None):
    s = qkv.shape[0]
    nq = s // TQ

    def body(q_ref, k_ref, v_ref, b_ref, o_ref, lse_ref, s_ref, p_ref, o0_ref):
        i = pl.program_id(1)
        ks = pl.multiple_of(jnp.maximum(i * TQ - BAND, 0), TQ)
        lo = _lo_mask()
        q = q_ref[...]
        kwin = k_ref[pl.ds(ks, WIN), :]
        vwin = v_ref[pl.ds(ks, WIN), :]
        for half in range(2):
            m_ = lo if half == 0 else jnp.logical_not(lo)
            s_ref[...] = _nt(jnp.where(m_, q, jnp.zeros_like(q)), kwin)
            for r0 in range(0, TQ, RB_SOFT):
                rows = pl.ds(r0, RB_SOFT)
                sc = s_ref[rows, :] + b_ref[half, rows, :]
                mx = jnp.max(sc, axis=-1, keepdims=True)
                e = jnp.exp(sc - mx)
                den = jnp.sum(e, axis=-1, keepdims=True)
                p_ref[rows, :] = (e * (1.0 / den)).astype(BF16)
                lse = mx + jnp.log(den)
                if half == 0:
                    lse_ref[rows, :] = jnp.broadcast_to(lse, (RB_SOFT, LANE))
                else:
                    lse_ref[rows, :] = jnp.where(lo, lse_ref[rows, :], lse)
            o = _nn(p_ref[...], vwin)
            if half == 0:
                o0_ref[...] = o
            else:
                o_ref[...] = jnp.where(lo, o0_ref[...], o).astype(BF16)

    tile = pl.BlockSpec((TQ, LANE), lambda j, i: (i, j))
    return _hosted_call(
        body, xchg, name="attn_fwd", grid=(NH // 2, nq),
        in_specs=[
            tile,
            pl.BlockSpec((s, LANE), lambda j, i: (0, AW // LANE + j)),
            pl.BlockSpec((s, LANE), lambda j, i: (0, 2 * AW // LANE + j)),
            _bias_layer_spec(l),
        ],
        out_specs=[tile, tile],
        out_shape=[SDS((s, D), BF16), SDS((s, AW), F32)], args=[qkv, qkv, qkv, bias],
        scratch_shapes=[pltpu.VMEM((TQ, WIN), F32), pltpu.VMEM((TQ, WIN), BF16), pltpu.VMEM((TQ, LANE), F32)])


_C0 = 3 * AW // LANE


def _cp_in_specs(s, l):
    blk = lambda f: pl.BlockSpec((s, LANE), f)
    return [
        blk(lambda g: (0, _C0 + jnp.minimum(g, 1))),
        blk(lambda g: (0, _C0 + 2 + jnp.minimum(g, 1))),
        blk(lambda g: (0, _C0 + 4 + jnp.minimum(g, 1))),
        blk(lambda g: (0, _C0 + 6 + jnp.maximum(g - 2, 0))),
        pl.BlockSpec((None, 3, LANE), lambda g: (l, 0, jnp.minimum(g, 1))),
        pl.BlockSpec((None, None, LANE, LANE), lambda g: (l, jnp.maximum(g - 2, 0), 0, 0)),
        pl.BlockSpec((None, 1, LANE), lambda g: (l, 0, jnp.maximum(g - 2, 0))),
    ]


def _pool_window_sums(u_ref, buf_a, buf_b, jj, s, rt):
    nrt = s // rt
    for t in range(nrt):
        buf_a[pl.ds(PAD + t * rt, rt), :] = u_ref[pl.ds(t * rt, rt), :]

    def stage(src, dst, sh):
        for t in range(nrt):
            r0 = PAD + t * rt
            dst[pl.ds(r0, rt), :] = src[pl.ds(r0, rt), :] + src[pl.ds(r0 - sh, rt), :]

    stage(buf_a, buf_b, 1)
    stage(buf_b, buf_a, 2)

    @pl.when(jj == 1)
    def _():
        stage(buf_a, buf_b, 4)
        stage(buf_b, buf_a, 8)


def _pool_counts(jj, lo, r0, rt):
    w = jnp.where(lo, jnp.where(jj == 0, 2.0, 8.0), jnp.where(jj == 0, 4.0, 16.0))
    pos1 = (lax.broadcasted_iota(jnp.int32, (rt, LANE), 0) + (r0 + 1)).astype(F32)
    return jnp.minimum(pos1, w)


def _conv_pool_fwd(p, mix, conv_w, wbd, pscale, l):
    s = p.shape[0]
    rt = min(256, s)
    nrt = s // rt

    def body(gb_ref, gc_ref, hin_ref, u_ref, cw_ref, wbd_ref, ps_ref, mix_in, o_ref, buf_a, buf_b):
        del mix_in
        g = pl.program_id(0)
        zpad = jnp.zeros((PAD, LANE), F32)
        buf_a[pl.ds(0, PAD), :] = zpad
        buf_b[pl.ds(0, PAD), :] = zpad

        @pl.when(g < 2)
        def _conv():
            for t in range(nrt):
                buf_a[pl.ds(PAD + t * rt, rt), :] = gc_ref[pl.ds(t * rt, rt), :] * hin_ref[pl.ds(t * rt, rt), :]
            w0, w1, w2 = cw_ref[0:1, :], cw_ref[1:2, :], cw_ref[2:3, :]
            for t in range(nrt):
                r0 = PAD + t * rt
                y = w0 * buf_a[pl.ds(r0 - 2, rt), :] + w1 * buf_a[pl.ds(r0 - 1, rt), :] + w2 * buf_a[pl.ds(r0, rt), :]
                o_ref[pl.ds(t * rt, rt), :] = (gb_ref[pl.ds(t * rt, rt), :] * y).astype(BF16)

        @pl.when(g >= 2)
        def _pool():
            jj = g - 2
            lo = _lo_mask()
            _pool_window_sums(u_ref, buf_a, buf_b, jj, s, rt)
            wb = wbd_ref[...]
            for t in range(nrt):
                r0 = PAD + t * rt
                wsum = jnp.where(lo, buf_b[pl.ds(r0, rt), :], buf_a[pl.ds(r0, rt), :])
                m = wsum / _pool_counts(jj, lo, t * rt, rt) - u_ref[pl.ds(t * rt, rt), :]
                o_ref[pl.ds(t * rt, rt), :] = (_nn(m.astype(BF16), wb) * ps_ref[...]).astype(BF16)

    return pl.pallas_call(
        body, name="conv_pool_fwd", grid=(4,),
        in_specs=_cp_in_specs(s, l) + [pl.BlockSpec(memory_space=pl.ANY)],
        out_specs=pl.BlockSpec((s, LANE), lambda g: (0, AW // LANE + g)),
        out_shape=SDS((s, D), BF16),
        scratch_shapes=[pltpu.VMEM((s + 2 * PAD, LANE), F32), pltpu.VMEM((s + 2 * PAD, LANE), F32)],
        input_output_aliases={7: 0},
        compiler_params=_cp(),
    )(p, p, p, p, conv_w, wbd, pscale, mix)


def _mlp_fwd(x, mix, wout, g2, w1_t, w2, l, xchg=None):
    s = x.shape[0]
    t = min(256, s)

    def body(x_ref, mix_ref, wo_ref, g_ref, w1_ref, w2_ref, xm_ref, a_ref, xo_ref):
        xm = x_ref[...] + _nn(mix_ref[...], wo_ref[...])
        xm_ref[...] = xm
        r = lax.rsqrt(jnp.mean(xm * xm, axis=-1, keepdims=True) + EPS)
        h2 = (xm * r * g_ref[...]).astype(BF16)
        a = _nt(h2, w1_ref[...])
        a_ref[...] = a.astype(BF16)
        f = jnp.square(jnp.maximum(a, 0.0)).astype(BF16)
        xo_ref[...] = xm + _nn(f, w2_ref[...])

    row = lambda c: pl.BlockSpec((t, c), lambda i: (i, 0))
    return _hosted_call(
        body, xchg, name="mlp_fwd", grid=(s // t,),
        in_specs=[row(D), row(D), _const((D, D)), _layer((1, D), l), _const((DFF, D)), _const((DFF, D))],
        out_specs=[row(D), row(DFF), row(D)],
        out_shape=[SDS((s, D), F32), SDS((s, DFF), BF16), SDS((s, D), F32)], args=[x, mix, wout, g2, w1_t, w2])


def _loss_grad(y, target):
    s = y.shape[0]
    t = min(512, s)

    def body(y_ref, t_ref, dy_ref, acc_ref):
        @pl.when(pl.program_id(0) == 0)
        def _():
            acc_ref[...] = jnp.zeros_like(acc_ref)
        e = y_ref[...] - t_ref[...]
        dy_ref[...] = e * (1.0 / D)
        acc_ref[...] += jnp.sum(e * e)

    row = pl.BlockSpec((t, D), lambda i: (i, 0))
    return pl.pallas_call(
        body, name="loss_grad", grid=(s // t,),
        in_specs=[row, row],
        out_specs=[row, pl.BlockSpec((8, LANE), lambda i: (0, 0))],
        out_shape=[SDS((s, D), F32), SDS((8, LANE), F32)],
        compiler_params=_cp(),
    )(y, target)


def _mlp_bwd(dxo, a, xm, g2, w1_t, w2, wout, l, xchg=None):
    s = dxo.shape[0]
    t = min(256, s)

    def body(dxo_ref, a_ref, xm_ref, g_ref, w1_ref, w2_ref, wo_ref,
             dxm_ref, dmix_ref, f_ref, da_ref, h2_ref, dxob_ref, dxmb_ref, dg_ref):
        @pl.when(pl.program_id(0) == 0)
        def _():
            dg_ref[...] = jnp.zeros_like(dg_ref)
        dxo = dxo_ref[...]
        dxob = dxo.astype(BF16)
        dxob_ref[...] = dxob
        ra = jnp.maximum(a_ref[...].astype(F32), 0.0)
        f_ref[...] = jnp.square(ra).astype(BF16)
        dab = (_nt(dxob, w2_ref[...]) * (2.0 * ra)).astype(BF16)
        da_ref[...] = dab
        dh2 = _nn(dab, w1_ref[...])
        xm = xm_ref[...]
        g = g_ref[...]
        r = lax.rsqrt(jnp.mean(xm * xm, axis=-1, keepdims=True) + EPS)
        h2_ref[...] = (xm * r * g).astype(BF16)
        dx_n, dgr = _rms_bwd(dh2, xm, g)
        dg_ref[...] += jnp.sum(dgr, axis=0, keepdims=True)
        dxm = dxo + dx_n
        dxm_ref[...] = dxm
        dxmb = dxm.astype(BF16)
        dxmb_ref[...] = dxmb
        dmix_ref[...] = _nt(dxmb, wo_ref[...])

    row = lambda c: pl.BlockSpec((t, c), lambda i: (i, 0))
    return _hosted_call(
        body, xchg, name="mlp_bwd", grid=(s // t,),
        in_specs=[row(D), row(DFF), row(D), _layer((1, D), l), _const((DFF, D)), _const((DFF, D)), _const((D, D))],
        out_specs=[row(D), row(D), row(DFF), row(DFF), row(D), row(D), row(D), pl.BlockSpec((1, D), lambda i: (0, 0))],
        out_shape=[SDS((s, D), F32), SDS((s, D), F32), SDS((s, DFF), BF16), SDS((s, DFF), BF16),
                   SDS((s, D), BF16), SDS((s, D), BF16), SDS((s, D), BF16), SDS((1, D), F32)],
        args=[dxo, a, xm, g2, w1_t, w2, wout])


def _attn_bwd(qkv, p, lse, dmix, qg2, kg2, bias, l, xchg=None):
    s = p.shape[0]
    nq = s // TQ
    scale = HD ** -0.5
    rt = min(512, s)

    def body(qs_ref, kb_ref, vb_ref, q_ref, k_ref, qg_ref, kg_ref, b_ref, lse_ref, do_ref,
             dq_ref, dk_ref, dv_ref, db_ref, dqg_ref, dkg_ref,
             dk_acc, dv_acc, s_ref, dp_ref, ds_ref, pb_ref, dqn_ref):
        i = pl.program_id(1)
        ks = pl.multiple_of(jnp.maximum(i * TQ - BAND, 0), TQ)
        lo = _lo_mask()

        @pl.when(i == 0)
        def _():
            dk_acc[...] = jnp.zeros_like(dk_acc)
            dv_acc[...] = jnp.zeros_like(dv_acc)
            dqg_ref[...] = jnp.zeros_like(dqg_ref)
            dkg_ref[...] = jnp.zeros_like(dkg_ref)

        @pl.when(i < NVAR)
        def _():
            db_ref[...] = jnp.zeros_like(db_ref)

        qs = qs_ref[...]
        kwin = kb_ref[pl.ds(ks, WIN), :]
        vwin = vb_ref[pl.ds(ks, WIN), :]
        dob = do_ref[...].astype(BF16)
        for half in range(2):
            m_ = lo if half == 0 else jnp.logical_not(lo)
            qa = jnp.where(m_, qs, jnp.zeros_like(qs))
            doa = jnp.where(m_, dob, jnp.zeros_like(dob))
            s_ref[...] = _nt(qa, kwin)
            dp_ref[...] = _nt(doa, vwin)
            for r0 in range(0, TQ, RB_SOFT):
                rows = pl.ds(r0, RB_SOFT)
                lse_h = lse_ref[rows, half * HD:half * HD + 1]
                pm = jnp.exp(s_ref[rows, :] + b_ref[half, rows, :] - lse_h)
                dp = dp_ref[rows, :]
                delta = jnp.sum(pm * dp, axis=-1, keepdims=True)
                ds = pm * (dp - delta)
                db_ref[half, rows, :] += ds
                ds_ref[rows, :] = ds.astype(BF16)
                pb_ref[rows, :] = pm.astype(BF16)
            dsb = ds_ref[...]
            dq_h = _nn(dsb, kwin)
            if half == 0:
                dqn_ref[...] = dq_h
            else:
                dqn_ref[...] = jnp.where(lo, dqn_ref[...], dq_h)
            dk_acc[pl.ds(ks, WIN), :] += _tn(dsb, qa)
            dv_acc[pl.ds(ks, WIN), :] += _tn(pb_ref[...], doa)
        qg, kg = qg_ref[...], kg_ref[...]
        xq, rq = _head_norm(q_ref[...], lo)
        dq, dqg_rows = _head_norm_bwd(dqn_ref[...] * scale, xq, rq, qg, lo)
        dq_ref[...] = dq.astype(BF16)
        dqg_ref[...] += jnp.sum(dqg_rows, axis=0, keepdims=True)

        @pl.when(i == nq - 1)
        def _():
            dkg = jnp.zeros((1, LANE), F32)
            for t in range(s // rt):
                rows = pl.ds(t * rt, rt)
                xk, rk = _head_norm(k_ref[rows, :], lo)
                dk, dkg_rows = _head_norm_bwd(dk_acc[rows, :], xk, rk, kg, lo)
                dk_ref[rows, :] = dk.astype(BF16)
                dv_ref[rows, :] = dv_acc[rows, :].astype(BF16)
                dkg = dkg + jnp.sum(dkg_rows, axis=0, keepdims=True)
            dkg_ref[...] = dkg

    tile = pl.BlockSpec((TQ, LANE), lambda j, i: (i, j))
    kcol = lambda c0: pl.BlockSpec((s, LANE), lambda j, i: (0, c0 + j))
    gain = pl.BlockSpec((None, 1, LANE), lambda j, i: (j, 0, 0))
    return _hosted_call(
        body, xchg, name="attn_bwd", grid=(NH // 2, nq),
        in_specs=[
            tile, kcol(AW // LANE), kcol(2 * AW // LANE), tile, kcol(AW // LANE),
            _layer((1, LANE), l), _layer((1, LANE), l),
            _bias_layer_spec(l), tile, tile,
        ],
        out_specs=[tile, kcol(0), kcol(0), _bias_spec(), gain, gain],
        out_shape=[SDS((s, AW), BF16), SDS((s, AW), BF16), SDS((s, AW), BF16),
                   SDS((NVAR, NH, TQ, WIN), F32), SDS((NH // 2, 1, LANE), F32), SDS((NH // 2, 1, LANE), F32)],
        scratch_shapes=[pltpu.VMEM((s, LANE), F32), pltpu.VMEM((s, LANE), F32),
                        pltpu.VMEM((TQ, WIN), F32), pltpu.VMEM((TQ, WIN), F32),
                        pltpu.VMEM((TQ, WIN), BF16), pltpu.VMEM((TQ, WIN), BF16), pltpu.VMEM((TQ, LANE), F32)],
        args=[qkv, qkv, qkv, p, p, qg2, kg2, bias, lse, dmix])


def _conv_pool_bwd(p, dmix, conv_w, wbd, pscale, l):
    s = p.shape[0]
    rt = min(256, s)
    nrt = s // rt

    def body(gb_ref, gc_ref, hin_ref, u_ref, cw_ref, wbd_ref, ps_ref, dy_ref,
             dgb_ref, dgc_ref, dhin_ref, du_ref, dcw_ref, dwbd_ref, dps_ref, buf_a, buf_b, buf_c, buf_d):
        g = pl.program_id(0)
        zpad = jnp.zeros((PAD, LANE), F32)
        for buf in (buf_a, buf_b, buf_c):
            buf[pl.ds(0, PAD), :] = zpad
            buf[pl.ds(PAD + s, PAD), :] = zpad

        @pl.when(g < 2)
        def _conv():
            for t in range(nrt):
                rows = pl.ds(t * rt, rt)
                buf_a[pl.ds(PAD + t * rt, rt), :] = gc_ref[rows, :] * hin_ref[rows, :]
                buf_b[pl.ds(PAD + t * rt, rt), :] = dy_ref[rows, :] * gb_ref[rows, :]
            w0, w1, w2 = cw_ref[0:1, :], cw_ref[1:2, :], cw_ref[2:3, :]
            d0 = jnp.zeros((1, LANE), F32)
            d1 = jnp.zeros((1, LANE), F32)
            d2 = jnp.zeros((1, LANE), F32)
            for t in range(nrt):
                rows = pl.ds(t * rt, rt)
                r0 = PAD + t * rt
                z2, z1, z0 = buf_a[pl.ds(r0 - 2, rt), :], buf_a[pl.ds(r0 - 1, rt), :], buf_a[pl.ds(r0, rt), :]
                y = w0 * z2 + w1 * z1 + w2 * z0
                dgb_ref[rows, :] = (dy_ref[rows, :] * y).astype(BF16)
                e0 = buf_b[pl.ds(r0, rt), :]
                d0 = d0 + jnp.sum(e0 * z2, axis=0, keepdims=True)
                d1 = d1 + jnp.sum(e0 * z1, axis=0, keepdims=True)
                d2 = d2 + jnp.sum(e0 * z0, axis=0, keepdims=True)
                dz = w2 * e0 + w1 * buf_b[pl.ds(r0 + 1, rt), :] + w0 * buf_b[pl.ds(r0 + 2, rt), :]
                dgc_ref[rows, :] = (dz * hin_ref[rows, :]).astype(BF16)
                dhin_ref[rows, :] = (dz * gc_ref[rows, :]).astype(BF16)
            dcw_ref[0:1, :] = d0
            dcw_ref[1:2, :] = d1
            dcw_ref[2:3, :] = d2

        @pl.when(g >= 2)
        def _pool():
            jj = g - 2
            lo = _lo_mask()
            _pool_window_sums(u_ref, buf_a, buf_b, jj, s, rt)
            wb = wbd_ref[...]
            ps = ps_ref[...]
            dps = jnp.zeros((1, LANE), F32)
            dwb = jnp.zeros((LANE, LANE), F32)
            for t in range(nrt):
                rows = pl.ds(t * rt, rt)
                r0 = PAD + t * rt
                cnt = _pool_counts(jj, lo, t * rt, rt)
                wsum = jnp.where(lo, buf_b[pl.ds(r0, rt), :], buf_a[pl.ds(r0, rt), :])
                mb = (wsum / cnt - u_ref[rows, :]).astype(BF16)
                dy = dy_ref[rows, :]
                dps = dps + jnp.sum(dy * _nn(mb, wb), axis=0, keepdims=True)
                dmp = (dy * ps).astype(BF16)
                dwb = dwb + _tn(mb, dmp)
                dm = _nt(dmp, wb)
                buf_d[rows, :] = dm
                buf_c[pl.ds(r0, rt), :] = dm / cnt
            dps_ref[...] = dps
            dwbd_ref[...] = dwb

            def stage(src, dst, sh):
                for t in range(nrt):
                    r0 = PAD + t * rt
                    dst[pl.ds(r0, rt), :] = src[pl.ds(r0, rt), :] + src[pl.ds(r0 + sh, rt), :]

            def finish(first, second):
                for t in range(nrt):
                    rows = pl.ds(t * rt, rt)
                    r0 = PAD + t * rt
                    fw = jnp.where(lo, first[pl.ds(r0, rt), :], second[pl.ds(r0, rt), :])
                    du_ref[rows, :] = (fw - buf_d[rows, :]).astype(BF16)

            stage(buf_c, buf_a, 1)
            stage(buf_a, buf_b, 2)

            @pl.when(jj == 0)
            def _():
                finish(buf_a, buf_b)

            @pl.when(jj == 1)
            def _():
                stage(buf_b, buf_c, 4)
                stage(buf_c, buf_a, 8)
                finish(buf_c, buf_a)

    cblk = pl.BlockSpec((s, LANE), lambda g: (0, jnp.minimum(g, 1)))
    pblk = pl.BlockSpec((s, LANE), lambda g: (0, jnp.maximum(g - 2, 0)))
    padded = pltpu.VMEM((s + 2 * PAD, LANE), F32)
    return pl.pallas_call(
        body, name="conv_pool_bwd", grid=(4,),
        in_specs=_cp_in_specs(s, l) + [pl.BlockSpec((s, LANE), lambda g: (0, AW // LANE + g))],
        out_specs=[cblk, cblk, cblk, pblk,
                   pl.BlockSpec((3, LANE), lambda g: (0, jnp.minimum(g, 1))),
                   pl.BlockSpec((None, LANE, LANE), lambda g: (jnp.maximum(g - 2, 0), 0, 0)),
                   pl.BlockSpec((1, LANE), lambda g: (0, jnp.maximum(g - 2, 0)))],
        out_shape=[SDS((s, CW), BF16), SDS((s, CW), BF16), SDS((s, CW), BF16), SDS((s, PWD), BF16),
                   SDS((3, CW), F32), SDS((2, LANE, LANE), F32), SDS((1, PWD), F32)],
        scratch_shapes=[padded, padded, padded, pltpu.VMEM((s, LANE), F32)],
        compiler_params=_cp(),
    )(p, p, p, p, conv_w, wbd, pscale, dmix)


def _in_proj_bwd(parts, x, dxm, g1, win_t, l, xchg=None):
    s = x.shape[0]
    t = min(256, s)
    widths = [a.shape[1] for a in parts]
    offs = [int(o) for o in np.cumsum([0] + widths[:-1])]
    n = len(parts)

    def body(*refs):
        part_refs = refs[:n]
        x_ref, dxm_ref, g_ref, w_ref, dx_ref, dp_ref, dg_ref = refs[n:]

        @pl.when(pl.program_id(0) == 0)
        def _():
            dg_ref[...] = jnp.zeros_like(dg_ref)
        for r, o, w in zip(part_refs, offs, widths):
            dp_ref[:, o:o + w] = r[...]
        dh = _nn(dp_ref[...], w_ref[...])
        dx_n, dgr = _rms_bwd(dh, x_ref[...], g_ref[...])
        dg_ref[...] += jnp.sum(dgr, axis=0, keepdims=True)
        dx_ref[...] = dxm_ref[...] + dx_n

    row = lambda c: pl.BlockSpec((t, c), lambda i: (i, 0))
    return _hosted_call(
        body, xchg, name="in_proj_bwd", grid=(s // t,),
        in_specs=[row(w) for w in widths] + [row(D), row(D), _layer((1, D), l), _const((DIN, D))],
        out_specs=[row(D), row(DIN), pl.BlockSpec((1, D), lambda i: (0, 0))],
        out_shape=[SDS((s, D), F32), SDS((s, DIN), BF16), SDS((1, D), F32)], args=[*parts, x, dxm, g1, win_t])


def _wgrad(a, b, tag, xchg=None):
    s, m = a.shape
    mb = 512
    t = min(1024, s)
    nt = s // t

    def body(a_ref, b_ref, o_ref, acc):
        @pl.when(pl.program_id(1) == 0)
        def _():
            acc[...] = jnp.zeros_like(acc)
        acc[...] += _tn(a_ref[...], b_ref[...])

        @pl.when(pl.program_id(1) == nt - 1)
        def _():
            o_ref[...] = acc[...].astype(BF16)

    (out,), got = _hosted_call(
        body, xchg, name=f"wgrad_{tag}", grid=(m // mb, nt),
        in_specs=[pl.BlockSpec((t, mb), lambda mi, ti: (ti, mi)), pl.BlockSpec((t, D), lambda mi, ti: (ti, 0))],
        out_specs=[pl.BlockSpec((mb, D), lambda mi, ti: (mi, 0))],
        out_shape=[SDS((m, D), BF16)], scratch_shapes=[pltpu.VMEM((mb, D), F32)], args=[a, b])
    return out, got


def _bias_tables(gvec):
    def body(g_ref, o_ref):
        qc = lax.broadcasted_iota(jnp.int32, (TQ, WIN), 0) // CHUNK
        kc = lax.broadcasted_iota(jnp.int32, (TQ, WIN), 1) // CHUNK
        for var in range(NVAR):
            vec = jnp.broadcast_to(g_ref[:, var * TQ:var * TQ + NTOE], (TQ, NTOE))
            toe = pltpu.roll(vec, NTOE - TQ + 1, 1, stride=1, stride_axis=0)[:, :WIN]
            rel = (BAND - var * TQ) // CHUNK + qc - kc
            o_ref[var] = jnp.where((rel >= 0) & (rel <= N_PREV), toe, NEG)

    return pl.pallas_call(
        body, name="bias_tables", grid=(L, NH),
        in_specs=[pl.BlockSpec((None, None, 1, NG), lambda l, h: (l, h, 0, 0))],
        out_specs=pl.BlockSpec((None, NVAR, None, TQ, WIN), lambda l, h: (l, 0, h, 0, 0)),
        out_shape=SDS((L, NVAR, NH, TQ, WIN), F32),
        compiler_params=_cp(),
    )(gvec)


def _bias_tables_grad(dbias, l):
    nb = NTOE // LANE
    wb = WIN // LANE

    def body(d_ref, o_ref):
        ii = lax.broadcasted_iota(jnp.int32, (LANE, LANE), 0)
        jj = lax.broadcasted_iota(jnp.int32, (LANE, LANE), 1)
        flip = jnp.where(ii + jj == LANE - 1, 1.0, 0.0).astype(BF16)
        o_ref[...] = jnp.zeros_like(o_ref)
        for var in range(NVAR):
            blocks = []
            for b in range(nb):
                src = nb - 1 - b
                if src >= wb:
                    blocks.append(jnp.zeros((TQ, LANE), F32))
                    continue
                xv = d_ref[var, :, src * LANE:(src + 1) * LANE]
                hi = xv.astype(BF16)
                lo = (xv - hi.astype(F32)).astype(BF16)
                blocks.append(_nn(hi, flip) + _nn(lo, flip))
            rev = jnp.concatenate(blocks, axis=1)
            skew = pltpu.roll(rev, NTOE - TQ + 1, 1, stride=1, stride_axis=0)
            off = NG - NTOE - var * TQ
            o_ref[:, off:off + NTOE] += jnp.sum(skew, axis=0, keepdims=True)

    return pl.pallas_call(
        body, name=f"bias_tables_grad_l{l}", grid=(NH,),
        in_specs=[pl.BlockSpec((NVAR, None, TQ, WIN), lambda h: (0, h, 0, 0))],
        out_specs=pl.BlockSpec((None, 1, NG), lambda h: (h, 0, 0)),
        out_shape=SDS((NH, 1, NG), F32),
        compiler_params=_cp(),
    )(dbias)


_SIBLING = (0, 0, 1)
_CHIPS = [(1, 0, 0), (0, 1, 0), (1, 1, 0)]
_MASKS = [_SIBLING] + _CHIPS + [(1, 0, 1), (0, 1, 1), (1, 1, 1)]


def _position():
    return lax.axis_index("x"), lax.axis_index("y"), lax.axis_index("c")


def _peer(pos, mask):
    return tuple(1 - a if f else a for a, f in zip(pos, mask))


def _index(pos):
    return 4 * pos[0] + 2 * pos[1] + pos[2]


def _exchange_phases(items, src, dst, sems):
    send_sems, recv_sems, local_sems = sems
    me = _position()
    sib = _peer(me, _SIBLING)

    def remote(s_ref, d_ref, pi, n, to):
        return pltpu.make_async_remote_copy(
            src_ref=s_ref, dst_ref=d_ref, send_sem=send_sems.at[pi, n], recv_sem=recv_sems.at[pi, n],
            device_id=to, device_id_type=MESH_ID)

    def parts(n):
        it = items[n]
        r = SHARD_ROWS[it[1]]
        block = lambda ref, pos: ref.at[pl.ds(_index(pos) * r, r), :]
        if it[0] == "gather":
            own = src[n].at[it[2]]
            local = pltpu.make_async_copy(own, block(dst[n], me), local_sems.at[n])
            sends = [remote(own, block(dst[n], me), pi, n, _peer(me, m)) for pi, m in enumerate([_SIBLING] + _CHIPS)]
            hops = [(remote(block(dst[n], _peer(me, m)), block(dst[n], _peer(me, m)), 1 + j, n, _peer(me, m)),
                     remote(block(dst[n], _peer(me, m)), block(dst[n], _peer(me, m)), 4 + j, n, sib))
                    for j, m in enumerate(_CHIPS)]
            lands = [remote(own, block(dst[n], sib), 0, n, sib)]
            lands += [remote(own, block(dst[n], _peer(sib, m)), 4 + j, n, sib) for j, m in enumerate(_CHIPS)]
        else:
            local = pltpu.make_async_copy(block(src[n], me), dst[n].at[_index(me)], local_sems.at[n])
            sends = [remote(block(src[n], _peer(me, m)), dst[n].at[_index(me)], pi, n, _peer(me, m))
                     for pi, m in enumerate(_MASKS)]
            hops = []
            lands = [remote(block(src[n], me), dst[n].at[_index(_peer(me, m))], pi, n, _peer(me, m))
                     for pi, m in enumerate(_MASKS)]
        return local, sends, hops, lands

    def start():
        for n in range(len(items)):
            local, sends, _, _ = parts(n)
            local.start()
            for cp in sends:
                cp.start()

    def relay():
        for n in range(len(items)):
            for arrived, onward in parts(n)[2]:
                arrived.wait_recv()
                onward.start()

    def finish():
        for n in range(len(items)):
            local, sends, hops, lands = parts(n)
            for cp in lands:
                cp.wait_recv()
            for cp in sends + [onward for _, onward in hops]:
                cp.wait_send()
            local.wait()

    return start, relay, finish


def _hosted_call(body, xchg, *, name, grid, in_specs, out_specs, out_shape, args, scratch_shapes=(), relay_at=0.7):
    if not xchg:
        outs = pl.pallas_call(
            body, name=name, grid=grid, in_specs=list(in_specs), out_specs=list(out_specs),
            out_shape=list(out_shape), scratch_shapes=list(scratch_shapes), compiler_params=_cp())(*args)
        return outs, []
    items = [it for it, _ in xchg]
    n_in, n_out, n_scr, nit = len(args), len(out_shape), len(scratch_shapes), len(items)
    hbm = pl.BlockSpec(memory_space=pl.ANY)
    steps = int(np.prod(grid))
    relay_step = min(int(relay_at * steps), steps - 1)

    def dst_shape(it):
        r = SHARD_ROWS[it[1]]
        return SDS((NDEV * r, D) if it[0] == "gather" else (NDEV, r, D), BF16)

    def wrapped(*refs):
        ins = refs[:n_in]
        src = refs[n_in:n_in + nit]
        outs = refs[n_in + nit:n_in + nit + n_out]
        dst = refs[n_in + nit + n_out:n_in + 2 * nit + n_out]
        scratch = refs[n_in + 2 * nit + n_out:n_in + 2 * nit + n_out + n_scr]
        start, relay, finish = _exchange_phases(items, src, dst, refs[n_in + 2 * nit + n_out + n_scr:])
        step = 0
        for d, g in enumerate(grid):
            step = step * g + pl.program_id(d)
        pl.when(step == 0)(start)
        body(*ins, *outs, *scratch)
        pl.when(step == relay_step)(relay)
        pl.when(step == steps - 1)(finish)

    npeer = len(_MASKS)
    res = pl.pallas_call(
        wrapped, name=name, grid=grid,
        in_specs=list(in_specs) + [hbm] * nit,
        out_specs=list(out_specs) + [hbm] * nit,
        out_shape=list(out_shape) + [dst_shape(it) for it in items],
        scratch_shapes=list(scratch_shapes) + [
            pltpu.SemaphoreType.DMA((npeer, nit)), pltpu.SemaphoreType.DMA((npeer, nit)), pltpu.SemaphoreType.DMA((nit,))],
        compiler_params=_cp(),
    )(*args, *[a for _, a in xchg])
    return list(res[:n_out]), list(res[n_out:])


def _gather_first(xchg):
    def body(o_ref):
        o_ref[...] = jnp.zeros_like(o_ref)

    return _hosted_call(
        body, xchg, name="gather_first", grid=(1,),
        in_specs=[], out_specs=[pl.BlockSpec((8, LANE), lambda i: (0, 0))], out_shape=[SDS((8, LANE), F32)], args=[])[1]


def _sum_slots(slots, xchg=None):
    _, r, _ = slots[0].shape
    rt = 64

    def body(*refs):
        o_ref = refs[L]
        for l in range(L):
            acc = refs[l][0].astype(F32)
            for d in range(1, NDEV):
                acc = acc + refs[l][d].astype(F32)
            o_ref[l] = acc

    (out,), got = _hosted_call(
        body, xchg, name=f"sum_slots_r{r}" + ("_x" if xchg else ""), grid=(r // rt,),
        in_specs=[pl.BlockSpec((NDEV, rt, D), lambda i: (0, i, 0))] * L,
        out_specs=[pl.BlockSpec((L, rt, D), lambda i: (0, i, 0))],
        out_shape=[SDS((L, r, D), F32)], args=list(slots))
    return out, got


def _exchange_small(v, reduce):
    rows = v.shape[0]

    def body(v_ref, o_ref, *scratch):
        if reduce:
            slots, send_sems, recv_sems = scratch
        else:
            slots = o_ref
            send_sems, recv_sems = scratch
        me = _position()
        slots[_index(me)] = v_ref[...]
        sends = []
        for pi, mask in enumerate(_MASKS):
            cp = pltpu.make_async_remote_copy(
                src_ref=v_ref, dst_ref=slots.at[_index(me)], send_sem=send_sems.at[pi], recv_sem=recv_sems.at[pi],
                device_id=_peer(me, mask), device_id_type=MESH_ID)
            cp.start()
            sends.append(cp)
        for pi, mask in enumerate(_MASKS):
            peer = _peer(me, mask)
            pltpu.make_async_remote_copy(
                src_ref=v_ref, dst_ref=slots.at[_index(peer)], send_sem=send_sems.at[pi], recv_sem=recv_sems.at[pi],
                device_id=peer, device_id_type=MESH_ID).wait_recv()
        for cp in sends:
            cp.wait_send()
        if reduce:
            acc = slots[0]
            for d in range(1, NDEV):
                acc = acc + slots[d]
            o_ref[...] = acc

    vm = pl.BlockSpec(memory_space=pltpu.VMEM)
    sems = [pltpu.SemaphoreType.DMA((len(_MASKS),)), pltpu.SemaphoreType.DMA((len(_MASKS),))]
    return pl.pallas_call(
        body, name="reduce_small" if reduce else "gather_small",
        in_specs=[vm], out_specs=vm,
        out_shape=SDS((rows, LANE) if reduce else (NDEV, rows, LANE), F32),
        scratch_shapes=([pltpu.VMEM((NDEV, rows, LANE), F32)] if reduce else []) + sems,
        compiler_params=_cp(),
    )(v)


def _adamw(w, g, m, v):
    rows, cols = w.shape
    t = rows
    for cand in (512, 256, 128, 64, 32, 16, 8):
        if rows % cand == 0:
            t = cand
            break

    def body(w_ref, g_ref, m_ref, v_ref, d_ref, nm_ref, nv_ref):
        gv = g_ref[...]
        mn = B1 * m_ref[...] + (1.0 - B1) * gv
        vn = B2 * v_ref[...] + (1.0 - B2) * jnp.square(gv)
        nm_ref[...] = mn
        nv_ref[...] = vn
        m_hat = mn / (1.0 - B1 ** STEP)
        v_hat = vn / (1.0 - B2 ** STEP)
        d_ref[...] = -LR * (m_hat / (jnp.sqrt(v_hat) + AEPS) + WD * w_ref[...])

    blk = pl.BlockSpec((t, cols), lambda i: (i, 0))
    return pl.pallas_call(
        body, name=f"adamw_{rows}x{cols}", grid=(rows // t,),
        in_specs=[blk] * 4, out_specs=[blk] * 3,
        out_shape=[SDS((rows, cols), F32)] * 3,
        compiler_params=_cp(),
    )(w, g, m, v)


_DIST0 = BAND + TQ - 1
_N_FAR = _DIST0 - REL_CLIP + 1
_N_NEAR = NG - _N_FAR - (2 * REL_CLIP - 1)


def _bias_vector(rel_bias):
    far = jnp.broadcast_to(rel_bias[..., -1:], (L, NH, _N_FAR))
    near = jnp.broadcast_to(rel_bias[..., :1], (L, NH, _N_NEAR))
    return jnp.concatenate([far, lax.rev(rel_bias[..., 1:-1], (2,)), near], axis=2)[:, :, None, :]


def _bias_vector_grad(dgr):
    first = jnp.sum(dgr[..., :_N_NEAR], axis=-1, keepdims=True)
    last = jnp.sum(dgr[..., NG - _N_FAR:], axis=-1, keepdims=True)
    return jnp.concatenate([first, dgr[..., _N_NEAR:NG - _N_FAR], last], axis=-1)


def _pool_blockdiag(pool_w):
    eye = jnp.eye(2, dtype=F32)
    pw = pool_w.reshape(L, 2, 2, HD, HD)
    return jnp.einsum("ljaik,ab->ljaibk", pw, eye).reshape(L, 2, LANE, LANE)


def _pool_blockdiag_grad(dwbd):
    d = dwbd.reshape(L, 2, 2, HD, 2, HD)
    return jnp.stack([d[:, :, 0, :, 0, :], d[:, :, 1, :, 1, :]], axis=2).reshape(L, 4, HD, HD)


def _pack(arrays, rows):
    flat = jnp.concatenate([a.reshape(-1).astype(F32) for a in arrays])
    return jnp.pad(flat, (0, rows * LANE - flat.shape[0])).reshape(rows, LANE)


def _unpack(packed, shapes):
    flat = packed.reshape(-1)
    out, o = [], 0
    for shp in shapes:
        n = int(np.prod(shp))
        out.append(flat[o:o + n].reshape(shp))
        o += n
    return out


def _rows_for(shapes):
    n = sum(int(np.prod(s)) for s in shapes)
    return -(-n // (8 * LANE)) * 8


def _grads(x, target, small_w, shards):
    g1, qg, kg, rb, cw, pw, ps, g2 = small_w
    g1 = g1.reshape(L, 1, D)
    g2 = g2.reshape(L, 1, D)
    qg2 = jnp.tile(qg, (1, 2)).reshape(L, 1, LANE)
    kg2 = jnp.tile(kg, (1, 2)).reshape(L, 1, LANE)
    ps3 = ps.reshape(L, 1, PWD)
    bias = _bias_tables(_bias_vector(rb))
    wbd = _pool_blockdiag(pw).astype(BF16)

    def gather(*kl):
        return [(("gather", k, l), shards[k]) for k, l in kl if l < L]

    full = {}

    def arrived(got, *kl):
        full.update(zip([x for x in kl if x[1] < L], got))

    arrived(_gather_first(gather((0, 0), (1, 0))), (0, 0), (1, 0))
    saved = []
    h = x
    for l in range(L):
        kl = ((2, 0), (1, 1)) if l == 0 else ((1, l + 1),)
        (p, h_b, qkv), got = _in_proj(h, g1, full[0, l], qg2, kg2, l, gather(*kl))
        arrived(got, *kl)
        kl = ((3, 0), (3, 1)) if l == 0 else ((3, l + 1),)
        (mix, lse), got = _attn_fwd(qkv, bias, l, gather(*kl))
        arrived(got, *kl)
        mix = _conv_pool_fwd(p, mix, cw, wbd, ps3, l)
        kl = ((2, l + 1), (0, l + 1))
        (xm, a, xo), got = _mlp_fwd(h, mix, full[1, l], g2, full[2, l], full[3, l], l, gather(*kl))
        arrived(got, *kl)
        saved.append((h, h_b, p, qkv, mix, lse, xm, a))
        h = xo
    dx, sq = _loss_grad(h, target)

    grads = {}
    slots = {}

    def scatter(*kl):
        return [(("scatter", k), grads[k, l]) for k, l in kl if l < L]

    def left(got, *kl):
        slots.update(zip([x for x in kl if x[1] < L], got))

    per_layer = [None] * L
    for l in reversed(range(L)):
        x_in, h_b, p, qkv, mix, lse, xm, a = saved[l]
        (dxm, dmix, f_b, da_b, h2_b, dxo_b, dxm_b, dg2), got = _mlp_bwd(
            dx, a, xm, g2, full[2, l], full[3, l], full[1, l], l, scatter((3, l + 1)))
        left(got, (3, l + 1))
        grads[1, l], _ = _wgrad(mix, dxm_b, f"w_out_l{l}")
        grads[2, l], got = _wgrad(da_b, h2_b, f"w_mlp1_l{l}", scatter((1, l)))
        left(got, (1, l))
        grads[3, l], _ = _wgrad(f_b, dxo_b, f"w_mlp2_l{l}")
        (dq, dk, dv, dbias, dqg, dkg), got = _attn_bwd(
            qkv, p, lse, dmix, qg2, kg2, bias, l, scatter((2, l), (0, l + 1)))
        left(got, (2, l), (0, l + 1))
        dgb, dgc, dhin, du, dcw, dwbd, dps = _conv_pool_bwd(p, dmix, cw, wbd, ps3, l)
        (dx, dp_b, dg1), got = _in_proj_bwd(
            [dq, dk, dv, dgb, dgc, dhin, du], x_in, dxm, g1, full[0, l], l, scatter((3, 0)) if l == 0 else None)
        left(got, (3, 0))
        grads[0, l], _ = _wgrad(dp_b, h_b, f"w_in_l{l}")
        per_layer[l] = (dg1, dg2, dqg, dkg, _bias_tables_grad(dbias, l), dcw, dwbd, dps)
    g_w1_t, got = _sum_slots([slots[2, l] for l in range(L)], scatter((0, 0)))
    left(got, (0, 0))
    sums = [_sum_slots([slots[k, l] for l in range(L)])[0] for k in (0, 1)] + [
        g_w1_t, _sum_slots([slots[3, l] for l in range(L)])[0]]

    st = [jnp.stack([per_layer[l][k] for l in range(L)]) for k in range(8)]
    small = dict(
        g1=st[0].reshape(L, D), g2=st[1].reshape(L, D),
        qg=st[2].reshape(L, NH, HD).sum(1), kg=st[3].reshape(L, NH, HD).sum(1),
        rb=_bias_vector_grad(st[4].reshape(L, NH, NG)), cw=st[5], pw=_pool_blockdiag_grad(st[6]),
        ps=st[7].reshape(L, PWD))
    return sq, dx, sums, small


def kernel(x, norm1_g, w_in, q_norm_g, k_norm_g, rel_bias, conv_w, pool_w, pool_scale, w_out, norm2_g, w_mlp1, w_mlp2, loss_target, m_norm1_g, m_w_in, m_q_norm_g, m_k_norm_g, m_rel_bias, m_conv_w, m_pool_w, m_pool_scale, m_w_out, m_norm2_g, m_w_mlp1, m_w_mlp2, v_norm1_g, v_w_in, v_q_norm_g, v_k_norm_g, v_rel_bias, v_conv_w, v_pool_w, v_pool_scale, v_w_out, v_norm2_g, v_w_mlp1, v_w_mlp2):
    me = _index(_position())
    cshard = CW // NDEV

    shards = [jnp.swapaxes(w_in, 1, 2).astype(BF16), w_out.astype(BF16),
              jnp.swapaxes(w_mlp1, 1, 2).astype(BF16), w_mlp2.astype(BF16)]
    cw_all = _exchange_small(_pack([conv_w], 8), reduce=False)
    cw_full = jnp.concatenate(
        [cw_all[d].reshape(-1)[:L * 3 * cshard].reshape(L, 3, cshard) for d in range(NDEV)], axis=2)

    small_w = (norm1_g, q_norm_g, k_norm_g, rel_bias, cw_full, pool_w, pool_scale, norm2_g)
    sq, grad_x, (g_win_t, g_wout, g_w1_t, g_w2), small = _grads(x[0], loss_target[0], small_w, shards)
    g_w_in = jnp.swapaxes(g_win_t, 1, 2)
    g_w_mlp1 = jnp.swapaxes(g_w1_t, 1, 2)

    names = ("g1", "qg", "kg", "rb", "cw", "pw", "ps", "g2")
    gshapes = [(L, D), (L, HD), (L, HD), (L, NH, 2 * REL_CLIP + 1), (L, 3, CW), (L, 4, HD, HD), (L, PWD), (L, D)]
    garrs = [small[n] for n in names]
    rows = _rows_for(gshapes + [(1,)])
    total = _exchange_small(_pack(garrs + [sq[0, :1]], rows), reduce=True)
    g_g1, g_qg, g_kg, g_rb, g_cw_full, g_pw, g_ps, g_g2, sq_sum = _unpack(total, gshapes + [(1,)])
    loss = (0.5 / D) * sq_sum[0]
    g_cw = lax.dynamic_slice_in_dim(g_cw_full, me * cshard, cshard, axis=2)

    def big(w, g, m, v):
        shp = w.shape
        r = lambda a: a.reshape(-1, shp[-1])
        return [o.reshape(shp) for o in _adamw(r(w), r(g), r(m), r(v))]

    up_in = big(w_in, g_w_in, m_w_in, v_w_in)
    up_out = big(w_out, g_wout, m_w_out, v_w_out)
    up_1 = big(w_mlp1, g_w_mlp1, m_w_mlp1, v_w_mlp1)
    up_2 = big(w_mlp2, g_w2, m_w_mlp2, v_w_mlp2)

    sw = [norm1_g, q_norm_g, k_norm_g, rel_bias, conv_w, pool_w, pool_scale, norm2_g]
    sg = [g_g1, g_qg, g_kg, g_rb, g_cw, g_pw, g_ps, g_g2]
    sm = [m_norm1_g, m_q_norm_g, m_k_norm_g, m_rel_bias, m_conv_w, m_pool_w, m_pool_scale, m_norm2_g]
    sv = [v_norm1_g, v_q_norm_g, v_k_norm_g, v_rel_bias, v_conv_w, v_pool_w, v_pool_scale, v_norm2_g]
    sshapes = [a.shape for a in sw]
    srows = _rows_for(sshapes)
    ups = _adamw(_pack(sw, srows), _pack(sg, srows), _pack(sm, srows), _pack(sv, srows))
    s_delta, s_m, s_v = [_unpack(u, sshapes) for u in ups]

    def order(small_list, in_, out_, m1, m2):
        g1_, qg_, kg_, rb_, cw_, pw_, ps_, g2_ = small_list
        return [g1_, in_, qg_, kg_, rb_, cw_, pw_, ps_, out_, g2_, m1, m2]

    grads = order(sg, g_w_in, g_wout, g_w_mlp1, g_w2)
    deltas = order(s_delta, up_in[0], up_out[0], up_1[0], up_2[0])
    new_m = order(s_m, up_in[1], up_out[1], up_1[1], up_2[1])
    new_v = order(s_v, up_in[2], up_out[2], up_1[2], up_2[2])
    return (loss, grad_x[None], *grads, *deltas, *new_m, *new_v)
```

```python
import numpy as np
import jax
import jax.numpy as jnp
from jax import lax
from jax.experimental import pallas as pl
from jax.experimental.pallas import tpu as pltpu

F32 = jnp.float32
BF16 = jnp.bfloat16
SDS = jax.ShapeDtypeStruct
MESH_ID = pl.DeviceIdType.MESH

D = 1024
L = 4
CHUNK = 64
N_PREV = 8
HD = 64
NH = 8
AW = 512
CW = 256
PWD = 256
DIN = 2560
DFF = 4096
EPS = 1e-6
NEG = -1e30
REL_CLIP = 128
POOL_WINDOWS = (2, 4, 8, 16)
LR, B1, B2, AEPS, WD, STEP = 0.001, 0.9, 0.999, 1e-08, 0.01, 10

NDEV = 8
LANE = 128
BAND = N_PREV * CHUNK
TQ = 256
WIN = TQ + BAND
NVAR = BAND // TQ + 1
NTOE = -(-(WIN + TQ - 1) // LANE) * LANE
NG = (NVAR - 1) * TQ + NTOE
PAD = 16
RB_NORM = 64
RB_SOFT = 16
VMEM_LIMIT = 56 * 1024 * 1024
SHARD_ROWS = (DIN // NDEV, D // NDEV, DFF // NDEV, DFF // NDEV)


def _cp(**kw):
    return pltpu.CompilerParams(vmem_limit_bytes=VMEM_LIMIT, **kw)


def _nn(a, b):
    return jnp.dot(a, b, preferred_element_type=F32)


def _nt(a, b):
    return lax.dot_general(a, b, (((1,), (1,)), ((), ())), preferred_element_type=F32)


def _tn(a, b):
    return lax.dot_general(a, b, (((0,), (0,)), ((), ())), preferred_element_type=F32)


def _const(shape):
    n = len(shape)
    return pl.BlockSpec(shape, lambda *_: (0,) * n, pipeline_mode=pl.Buffered(1))


def _layer(shape, l):
    n = len(shape)
    return pl.BlockSpec((None,) + tuple(shape), lambda *_: (l,) + (0,) * n, pipeline_mode=pl.Buffered(1))


def _lo_mask():
    return lax.broadcasted_iota(jnp.int32, (1, LANE), 1) < HD


def _half_sum(t, lo):
    s_lo = jnp.sum(jnp.where(lo, t, 0.0), axis=-1, keepdims=True)
    s_hi = jnp.sum(jnp.where(lo, 0.0, t), axis=-1, keepdims=True)
    return jnp.where(lo, s_lo, s_hi)


def _head_norm(x, lo):
    r = lax.rsqrt(_half_sum(x * x, lo) * (1.0 / HD) + EPS)
    return x * r, r


def _head_norm_bwd(dy, xn, r, g, lo):
    dxn = dy * g
    mu = _half_sum(dxn * xn, lo) * (1.0 / HD)
    return r * (dxn - xn * mu), dy * xn


def _rms_bwd(dy, x, g):
    r = lax.rsqrt(jnp.mean(x * x, axis=-1, keepdims=True) + EPS)
    xn = x * r
    dxn = dy * g
    mu = jnp.mean(dxn * xn, axis=-1, keepdims=True)
    return r * (dxn - xn * mu), dy * xn


def _in_proj(x, g1, win_t, qg2, kg2, l, xchg=None):
    s = x.shape[0]
    t = min(512, s)
    nblk = AW // LANE

    def body(x_ref, g_ref, w_ref, qg_ref, kg_ref, p_ref, h_ref, qkv_ref):
        xv = x_ref[...]
        r = lax.rsqrt(jnp.mean(xv * xv, axis=-1, keepdims=True) + EPS)
        h = (xv * r * g_ref[...]).astype(BF16)
        h_ref[...] = h
        p_ref[...] = _nt(h, w_ref[...])
        lo = _lo_mask()
        gains = (qg_ref[...] * (HD ** -0.5), kg_ref[...])
        for r0 in range(0, t, RB_NORM):
            rows = pl.ds(r0, RB_NORM)
            for c in range(3 * nblk):
                cols = pl.ds(c * LANE, LANE)
                v = p_ref[rows, cols]
                if c < 2 * nblk:
                    v = _head_norm(v, lo)[0] * gains[c // nblk]
                qkv_ref[rows, cols] = v.astype(BF16)

    row = lambda c: pl.BlockSpec((t, c), lambda i: (i, 0))
    return _hosted_call(
        body, xchg, name="in_proj", grid=(s // t,),
        in_specs=[row(D), _layer((1, D), l), _const((DIN, D)), _layer((1, LANE), l), _layer((1, LANE), l)],
        out_specs=[row(DIN), row(D), row(3 * AW)],
        out_shape=[SDS((s, DIN), F32), SDS((s, D), BF16), SDS((s, 3 * AW), BF16)], args=[x, g1, win_t, qg2, kg2])


def _bias_spec():
    return pl.BlockSpec((None, 2, TQ, WIN), lambda j, i: (jnp.maximum(NVAR - 1 - i, 0), j, 0, 0))


def _bias_layer_spec(l):
    return pl.BlockSpec((None, None, 2, TQ, WIN), lambda j, i: (l, jnp.maximum(NVAR - 1 - i, 0), j, 0, 0))


def _attn_fwd(qkv, bias, l, xchg=None):
    s = qkv.shape[0]
    nq = s // TQ

    def body(q_ref, k_ref, v_ref, b_ref, o_ref, lse_ref, s_ref, p_ref, o0_ref):
        i = pl.program_id(1)
        ks = pl.multiple_of(jnp.maximum(i * TQ - BAND, 0), TQ)
        lo = _lo_mask()
        q = q_ref[...]
        kwin = k_ref[pl.ds(ks, WIN), :]
        vwin = v_ref[pl.ds(ks, WIN), :]
        for half in range(2):
            m_ = lo if half == 0 else jnp.logical_not(lo)
            s_ref[...] = _nt(jnp.where(m_, q, jnp.zeros_like(q)), kwin)
            for r0 in range(0, TQ, RB_SOFT):
                rows = pl.ds(r0, RB_SOFT)
                sc = s_ref[rows, :] + b_ref[half, rows, :]
                mx = jnp.max(sc, axis=-1, keepdims=True)
                e = jnp.exp(sc - mx)
                den = jnp.sum(e, axis=-1, keepdims=True)
                p_ref[rows, :] = (e * (1.0 / den)).astype(BF16)
                lse = mx + jnp.log(den)
                if half == 0:
                    lse_ref[rows, :] = jnp.broadcast_to(lse, (RB_SOFT, LANE))
                else:
                    lse_ref[rows, :] = jnp.where(lo, lse_ref[rows, :], lse)
            o = _nn(p_ref[...], vwin)
            if half == 0:
                o0_ref[...] = o
            else:
                o_ref[...] = jnp.where(lo, o0_ref[...], o).astype(BF16)

    tile = pl.BlockSpec((TQ, LANE), lambda j, i: (i, j))
    return _hosted_call(
        body, xchg, name="attn_fwd", grid=(NH // 2, nq),
        in_specs=[
            tile,
            pl.BlockSpec((s, LANE), lambda j, i: (0, AW // LANE + j)),
            pl.BlockSpec((s, LANE), lambda j, i: (0, 2 * AW // LANE + j)),
            _bias_layer_spec(l),
        ],
        out_specs=[tile, tile],
        out_shape=[SDS((s, D), BF16), SDS((s, AW), F32)], args=[qkv, qkv, qkv, bias],
        scratch_shapes=[pltpu.VMEM((TQ, WIN), F32), pltpu.VMEM((TQ, WIN), BF16), pltpu.VMEM((TQ, LANE), F32)])


_C0 = 3 * AW // LANE


def _cp_in_specs(s, l):
    blk = lambda f: pl.BlockSpec((s, LANE), f)
    return [
        blk(lambda g: (0, _C0 + jnp.minimum(g, 1))),
        blk(lambda g: (0, _C0 + 2 + jnp.minimum(g, 1))),
        blk(lambda g: (0, _C0 + 4 + jnp.minimum(g, 1))),
        blk(lambda g: (0, _C0 + 6 + jnp.maximum(g - 2, 0))),
        pl.BlockSpec((None, 3, LANE), lambda g: (l, 0, jnp.minimum(g, 1))),
        pl.BlockSpec((None, None, LANE, LANE), lambda g: (l, jnp.maximum(g - 2, 0), 0, 0)),
        pl.BlockSpec((None, 1, LANE), lambda g: (l, 0, jnp.maximum(g - 2, 0))),
    ]


def _pool_window_sums(u_ref, buf_a, buf_b, jj, s, rt):
    nrt = s // rt
    for t in range(nrt):
        buf_a[pl.ds(PAD + t * rt, rt), :] = u_ref[pl.ds(t * rt, rt), :]

    def stage(src, dst, sh):
        for t in range(nrt):
            r0 = PAD + t * rt
            dst[pl.ds(r0, rt), :] = src[pl.ds(r0, rt), :] + src[pl.ds(r0 - sh, rt), :]

    stage(buf_a, buf_b, 1)
    stage(buf_b, buf_a, 2)

    @pl.when(jj == 1)
    def _():
        stage(buf_a, buf_b, 4)
        stage(buf_b, buf_a, 8)


def _pool_counts(jj, lo, r0, rt):
    w = jnp.where(lo, jnp.where(jj == 0, 2.0, 8.0), jnp.where(jj == 0, 4.0, 16.0))
    pos1 = (lax.broadcasted_iota(jnp.int32, (rt, LANE), 0) + (r0 + 1)).astype(F32)
    return jnp.minimum(pos1, w)


def _conv_pool_fwd(p, mix, conv_w, wbd, pscale, l):
    s = p.shape[0]
    rt = min(256, s)
    nrt = s // rt

    def body(gb_ref, gc_ref, hin_ref, u_ref, cw_ref, wbd_ref, ps_ref, mix_in, o_ref, buf_a, buf_b):
        del mix_in
        g = pl.program_id(0)
        zpad = jnp.zeros((PAD, LANE), F32)
        buf_a[pl.ds(0, PAD), :] = zpad
        buf_b[pl.ds(0, PAD), :] = zpad

        @pl.when(g < 2)
        def _conv():
            for t in range(nrt):
                buf_a[pl.ds(PAD + t * rt, rt), :] = gc_ref[pl.ds(t * rt, rt), :] * hin_ref[pl.ds(t * rt, rt), :]
            w0, w1, w2 = cw_ref[0:1, :], cw_ref[1:2, :], cw_ref[2:3, :]
            for t in range(nrt):
                r0 = PAD + t * rt
                y = w0 * buf_a[pl.ds(r0 - 2, rt), :] + w1 * buf_a[pl.ds(r0 - 1, rt), :] + w2 * buf_a[pl.ds(r0, rt), :]
                o_ref[pl.ds(t * rt, rt), :] = (gb_ref[pl.ds(t * rt, rt), :] * y).astype(BF16)

        @pl.when(g >= 2)
        def _pool():
            jj = g - 2
            lo = _lo_mask()
            _pool_window_sums(u_ref, buf_a, buf_b, jj, s, rt)
            wb = wbd_ref[...]
            for t in range(nrt):
                r0 = PAD + t * rt
                wsum = jnp.where(lo, buf_b[pl.ds(r0, rt), :], buf_a[pl.ds(r0, rt), :])
                m = wsum / _pool_counts(jj, lo, t * rt, rt) - u_ref[pl.ds(t * rt, rt), :]
                o_ref[pl.ds(t * rt, rt), :] = (_nn(m.astype(BF16), wb) * ps_ref[...]).astype(BF16)

    return pl.pallas_call(
        body, name="conv_pool_fwd", grid=(4,),
        in_specs=_cp_in_specs(s, l) + [pl.BlockSpec(memory_space=pl.ANY)],
        out_specs=pl.BlockSpec((s, LANE), lambda g: (0, AW // LANE + g)),
        out_shape=SDS((s, D), BF16),
        scratch_shapes=[pltpu.VMEM((s + 2 * PAD, LANE), F32), pltpu.VMEM((s + 2 * PAD, LANE), F32)],
        input_output_aliases={7: 0},
        compiler_params=_cp(),
    )(p, p, p, p, conv_w, wbd, pscale, mix)


def _mlp_fwd(x, mix, wout, g2, w1_t, w2, l, xchg=None):
    s = x.shape[0]
    t = min(256, s)

    def body(x_ref, mix_ref, wo_ref, g_ref, w1_ref, w2_ref, xm_ref, a_ref, xo_ref):
        xm = x_ref[...] + _nn(mix_ref[...], wo_ref[...])
        xm_ref[...] = xm
        r = lax.rsqrt(jnp.mean(xm * xm, axis=-1, keepdims=True) + EPS)
        h2 = (xm * r * g_ref[...]).astype(BF16)
        a = _nt(h2, w1_ref[...])
        a_ref[...] = a.astype(BF16)
        f = jnp.square(jnp.maximum(a, 0.0)).astype(BF16)
        xo_ref[...] = xm + _nn(f, w2_ref[...])

    row = lambda c: pl.BlockSpec((t, c), lambda i: (i, 0))
    return _hosted_call(
        body, xchg, name="mlp_fwd", grid=(s // t,),
        in_specs=[row(D), row(D), _const((D, D)), _layer((1, D), l), _const((DFF, D)), _const((DFF, D))],
        out_specs=[row(D), row(DFF), row(D)],
        out_shape=[SDS((s, D), F32), SDS((s, DFF), BF16), SDS((s, D), F32)], args=[x, mix, wout, g2, w1_t, w2])


def _loss_grad(y, target):
    s = y.shape[0]
    t = min(512, s)

    def body(y_ref, t_ref, dy_ref, acc_ref):
        @pl.when(pl.program_id(0) == 0)
        def _():
            acc_ref[...] = jnp.zeros_like(acc_ref)
        e = y_ref[...] - t_ref[...]
        dy_ref[...] = e * (1.0 / D)
        acc_ref[...] += jnp.sum(e * e)

    row = pl.BlockSpec((t, D), lambda i: (i, 0))
    return pl.pallas_call(
        body, name="loss_grad", grid=(s // t,),
        in_specs=[row, row],
        out_specs=[row, pl.BlockSpec((8, LANE), lambda i: (0, 0))],
        out_shape=[SDS((s, D), F32), SDS((8, LANE), F32)],
        compiler_params=_cp(),
    )(y, target)


def _mlp_bwd(dxo, a, xm, g2, w1_t, w2, wout, l, xchg=None):
    s = dxo.shape[0]
    t = min(256, s)

    def body(dxo_ref, a_ref, xm_ref, g_ref, w1_ref, w2_ref, wo_ref,
             dxm_ref, dmix_ref, f_ref, da_ref, h2_ref, dxob_ref, dxmb_ref, dg_ref):
        @pl.when(pl.program_id(0) == 0)
        def _():
            dg_ref[...] = jnp.zeros_like(dg_ref)
        dxo = dxo_ref[...]
        dxob = dxo.astype(BF16)
        dxob_ref[...] = dxob
        ra = jnp.maximum(a_ref[...].astype(F32), 0.0)
        f_ref[...] = jnp.square(ra).astype(BF16)
        dab = (_nt(dxob, w2_ref[...]) * (2.0 * ra)).astype(BF16)
        da_ref[...] = dab
        dh2 = _nn(dab, w1_ref[...])
        xm = xm_ref[...]
        g = g_ref[...]
        r = lax.rsqrt(jnp.mean(xm * xm, axis=-1, keepdims=True) + EPS)
        h2_ref[...] = (xm * r * g).astype(BF16)
        dx_n, dgr = _rms_bwd(dh2, xm, g)
        dg_ref[...] += jnp.sum(dgr, axis=0, keepdims=True)
        dxm = dxo + dx_n
        dxm_ref[...] = dxm
        dxmb = dxm.astype(BF16)
        dxmb_ref[...] = dxmb
        dmix_ref[...] = _nt(dxmb, wo_ref[...])

    row = lambda c: pl.BlockSpec((t, c), lambda i: (i, 0))
    return _hosted_call(
        body, xchg, name="mlp_bwd", grid=(s // t,),
        in_specs=[row(D), row(DFF), row(D), _layer((1, D), l), _const((DFF, D)), _const((DFF, D)), _const((D, D))],
        out_specs=[row(D), row(D), row(DFF), row(DFF), row(D), row(D), row(D), pl.BlockSpec((1, D), lambda i: (0, 0))],
        out_shape=[SDS((s, D), F32), SDS((s, D), F32), SDS((s, DFF), BF16), SDS((s, DFF), BF16),
                   SDS((s, D), BF16), SDS((s, D), BF16), SDS((s, D), BF16), SDS((1, D), F32)],
        args=[dxo, a, xm, g2, w1_t, w2, wout])


def _attn_bwd(qkv, p, lse, dmix, qg2, kg2, bias, l, xchg=None):
    s = p.shape[0]
    nq = s // TQ
    scale = HD ** -0.5
    rt = min(512, s)

    def body(qs_ref, kb_ref, vb_ref, q_ref, k_ref, qg_ref, kg_ref, b_ref, lse_ref, do_ref,
             dq_ref, dk_ref, dv_ref, db_ref, dqg_ref, dkg_ref,
             dk_acc, dv_acc, s_ref, dp_ref, ds_ref, pb_ref, dqn_ref):
        i = pl.program_id(1)
        ks = pl.multiple_of(jnp.maximum(i * TQ - BAND, 0), TQ)
        lo = _lo_mask()

        @pl.when(i == 0)
        def _():
            dk_acc[...] = jnp.zeros_like(dk_acc)
            dv_acc[...] = jnp.zeros_like(dv_acc)
            dqg_ref[...] = jnp.zeros_like(dqg_ref)
            dkg_ref[...] = jnp.zeros_like(dkg_ref)

        @pl.when(i < NVAR)
        def _():
            db_ref[...] = jnp.zeros_like(db_ref)

        qs = qs_ref[...]
        kwin = kb_ref[pl.ds(ks, WIN), :]
        vwin = vb_ref[pl.ds(ks, WIN), :]
        dob = do_ref[...].astype(BF16)
        for half in range(2):
            m_ = lo if half == 0 else jnp.logical_not(lo)
            qa = jnp.where(m_, qs, jnp.zeros_like(qs))
            doa = jnp.where(m_, dob, jnp.zeros_like(dob))
            s_ref[...] = _nt(qa, kwin)
            dp_ref[...] = _nt(doa, vwin)
            for r0 in range(0, TQ, RB_SOFT):
                rows = pl.ds(r0, RB_SOFT)
                lse_h = lse_ref[rows, half * HD:half * HD + 1]
                pm = jnp.exp(s_ref[rows, :] + b_ref[half, rows, :] - lse_h)
                dp = dp_ref[rows, :]
                delta = jnp.sum(pm * dp, axis=-1, keepdims=True)
                ds = pm * (dp - delta)
                db_ref[half, rows, :] += ds
                ds_ref[rows, :] = ds.astype(BF16)
                pb_ref[rows, :] = pm.astype(BF16)
            dsb = ds_ref[...]
            dq_h = _nn(dsb, kwin)
            if half == 0:
                dqn_ref[...] = dq_h
            else:
                dqn_ref[...] = jnp.where(lo, dqn_ref[...], dq_h)
            dk_acc[pl.ds(ks, WIN), :] += _tn(dsb, qa)
            dv_acc[pl.ds(ks, WIN), :] += _tn(pb_ref[...], doa)
        qg, kg = qg_ref[...], kg_ref[...]
        xq, rq = _head_norm(q_ref[...], lo)
        dq, dqg_rows = _head_norm_bwd(dqn_ref[...] * scale, xq, rq, qg, lo)
        dq_ref[...] = dq.astype(BF16)
        dqg_ref[...] += jnp.sum(dqg_rows, axis=0, keepdims=True)

        @pl.when(i == nq - 1)
        def _():
            dkg = jnp.zeros((1, LANE), F32)
            for t in range(s // rt):
                rows = pl.ds(t * rt, rt)
                xk, rk = _head_norm(k_ref[rows, :], lo)
                dk, dkg_rows = _head_norm_bwd(dk_acc[rows, :], xk, rk, kg, lo)
                dk_ref[rows, :] = dk.astype(BF16)
                dv_ref[rows, :] = dv_acc[rows, :].astype(BF16)
                dkg = dkg + jnp.sum(dkg_rows, axis=0, keepdims=True)
            dkg_ref[...] = dkg

    tile = pl.BlockSpec((TQ, LANE), lambda j, i: (i, j))
    kcol = lambda c0: pl.BlockSpec((s, LANE), lambda j, i: (0, c0 + j))
    gain = pl.BlockSpec((None, 1, LANE), lambda j, i: (j, 0, 0))
    return _hosted_call(
        body, xchg, name="attn_bwd", grid=(NH // 2, nq),
        in_specs=[
            tile, kcol(AW // LANE), kcol(2 * AW // LANE), tile, kcol(AW // LANE),
            _layer((1, LANE), l), _layer((1, LANE), l),
            _bias_layer_spec(l), tile, tile,
        ],
        out_specs=[tile, kcol(0), kcol(0), _bias_spec(), gain, gain],
        out_shape=[SDS((s, AW), BF16), SDS((s, AW), BF16), SDS((s, AW), BF16),
                   SDS((NVAR, NH, TQ, WIN), F32), SDS((NH // 2, 1, LANE), F32), SDS((NH // 2, 1, LANE), F32)],
        scratch_shapes=[pltpu.VMEM((s, LANE), F32), pltpu.VMEM((s, LANE), F32),
                        pltpu.VMEM((TQ, WIN), F32), pltpu.VMEM((TQ, WIN), F32),
                        pltpu.VMEM((TQ, WIN), BF16), pltpu.VMEM((TQ, WIN), BF16), pltpu.VMEM((TQ, LANE), F32)],
        args=[qkv, qkv, qkv, p, p, qg2, kg2, bias, lse, dmix])


def _conv_pool_bwd(p, dmix, conv_w, wbd, pscale, l):
    s = p.shape[0]
    rt = min(256, s)
    nrt = s // rt

    def body(gb_ref, gc_ref, hin_ref, u_ref, cw_ref, wbd_ref, ps_ref, dy_ref,
             dgb_ref, dgc_ref, dhin_ref, du_ref, dcw_ref, dwbd_ref, dps_ref, buf_a, buf_b, buf_c, buf_d):
        g = pl.program_id(0)
        zpad = jnp.zeros((PAD, LANE), F32)
        for buf in (buf_a, buf_b, buf_c):
            buf[pl.ds(0, PAD), :] = zpad
            buf[pl.ds(PAD + s, PAD), :] = zpad

        @pl.when(g < 2)
        def _conv():
            for t in range(nrt):
                rows = pl.ds(t * rt, rt)
                buf_a[pl.ds(PAD + t * rt, rt), :] = gc_ref[rows, :] * hin_ref[rows, :]
                buf_b[pl.ds(PAD + t * rt, rt), :] = dy_ref[rows, :] * gb_ref[rows, :]
            w0, w1, w2 = cw_ref[0:1, :], cw_ref[1:2, :], cw_ref[2:3, :]
            d0 = jnp.zeros((1, LANE), F32)
            d1 = jnp.zeros((1, LANE), F32)
            d2 = jnp.zeros((1, LANE), F32)
            for t in range(nrt):
                rows = pl.ds(t * rt, rt)
                r0 = PAD + t * rt
                z2, z1, z0 = buf_a[pl.ds(r0 - 2, rt), :], buf_a[pl.ds(r0 - 1, rt), :], buf_a[pl.ds(r0, rt), :]
                y = w0 * z2 + w1 * z1 + w2 * z0
                dgb_ref[rows, :] = (dy_ref[rows, :] * y).astype(BF16)
                e0 = buf_b[pl.ds(r0, rt), :]
                d0 = d0 + jnp.sum(e0 * z2, axis=0, keepdims=True)
                d1 = d1 + jnp.sum(e0 * z1, axis=0, keepdims=True)
                d2 = d2 + jnp.sum(e0 * z0, axis=0, keepdims=True)
                dz = w2 * e0 + w1 * buf_b[pl.ds(r0 + 1, rt), :] + w0 * buf_b[pl.ds(r0 + 2, rt), :]
                dgc_ref[rows, :] = (dz * hin_ref[rows, :]).astype(BF16)
                dhin_ref[rows, :] = (dz * gc_ref[rows, :]).astype(BF16)
            dcw_ref[0:1, :] = d0
            dcw_ref[1:2, :] = d1
            dcw_ref[2:3, :] = d2

        @pl.when(g >= 2)
        def _pool():
            jj = g - 2
            lo = _lo_mask()
            _pool_window_sums(u_ref, buf_a, buf_b, jj, s, rt)
            wb = wbd_ref[...]
            ps = ps_ref[...]
            dps = jnp.zeros((1, LANE), F32)
            dwb = jnp.zeros((LANE, LANE), F32)
            for t in range(nrt):
                rows = pl.ds(t * rt, rt)
                r0 = PAD + t * rt
                cnt = _pool_counts(jj, lo, t * rt, rt)
                wsum = jnp.where(lo, buf_b[pl.ds(r0, rt), :], buf_a[pl.ds(r0, rt), :])
                mb = (wsum / cnt - u_ref[rows, :]).astype(BF16)
                dy = dy_ref[rows, :]
                dps = dps + jnp.sum(dy * _nn(mb, wb), axis=0, keepdims=True)
                dmp = (dy * ps).astype(BF16)
                dwb = dwb + _tn(mb, dmp)
                dm = _nt(dmp, wb)
                buf_d[rows, :] = dm
                buf_c[pl.ds(r0, rt), :] = dm / cnt
            dps_ref[...] = dps
            dwbd_ref[...] = dwb

            def stage(src, dst, sh):
                for t in range(nrt):
                    r0 = PAD + t * rt
                    dst[pl.ds(r0, rt), :] = src[pl.ds(r0, rt), :] + src[pl.ds(r0 + sh, rt), :]

            def finish(first, second):
                for t in range(nrt):
                    rows = pl.ds(t * rt, rt)
                    r0 = PAD + t * rt
                    fw = jnp.where(lo, first[pl.ds(r0, rt), :], second[pl.ds(r0, rt), :])
                    du_ref[rows, :] = (fw - buf_d[rows, :]).astype(BF16)

            stage(buf_c, buf_a, 1)
            stage(buf_a, buf_b, 2)

            @pl.when(jj == 0)
            def _():
                finish(buf_a, buf_b)

            @pl.when(jj == 1)
            def _():
                stage(buf_b, buf_c, 4)
                stage(buf_c, buf_a, 8)
                finish(buf_c, buf_a)

    cblk = pl.BlockSpec((s, LANE), lambda g: (0, jnp.minimum(g, 1)))
    pblk = pl.BlockSpec((s, LANE), lambda g: (0, jnp.maximum(g - 2, 0)))
    padded = pltpu.VMEM((s + 2 * PAD, LANE), F32)
    return pl.pallas_call(
        body, name="conv_pool_bwd", grid=(4,),
        in_specs=_cp_in_specs(s, l) + [pl.BlockSpec((s, LANE), lambda g: (0, AW // LANE + g))],
        out_specs=[cblk, cblk, cblk, pblk,
                   pl.BlockSpec((3, LANE), lambda g: (0, jnp.minimum(g, 1))),
                   pl.BlockSpec((None, LANE, LANE), lambda g: (jnp.maximum(g - 2, 0), 0, 0)),
                   pl.BlockSpec((1, LANE), lambda g: (0, jnp.maximum(g - 2, 0)))],
        out_shape=[SDS((s, CW), BF16), SDS((s, CW), BF16), SDS((s, CW), BF16), SDS((s, PWD), BF16),
                   SDS((3, CW), F32), SDS((2, LANE, LANE), F32), SDS((1, PWD), F32)],
        scratch_shapes=[padded, padded, padded, pltpu.VMEM((s, LANE), F32)],
        compiler_params=_cp(),
    )(p, p, p, p, conv_w, wbd, pscale, dmix)


def _in_proj_bwd(parts, x, dxm, g1, win_t, l, xchg=None):
    s = x.shape[0]
    t = min(256, s)
    widths = [a.shape[1] for a in parts]
    offs = [int(o) for o in np.cumsum([0] + widths[:-1])]
    n = len(parts)

    def body(*refs):
        part_refs = refs[:n]
        x_ref, dxm_ref, g_ref, w_ref, dx_ref, dp_ref, dg_ref = refs[n:]

        @pl.when(pl.program_id(0) == 0)
        def _():
            dg_ref[...] = jnp.zeros_like(dg_ref)
        for r, o, w in zip(part_refs, offs, widths):
            dp_ref[:, o:o + w] = r[...]
        dh = _nn(dp_ref[...], w_ref[...])
        dx_n, dgr = _rms_bwd(dh, x_ref[...], g_ref[...])
        dg_ref[...] += jnp.sum(dgr, axis=0, keepdims=True)
        dx_ref[...] = dxm_ref[...] + dx_n

    row = lambda c: pl.BlockSpec((t, c), lambda i: (i, 0))
    return _hosted_call(
        body, xchg, name="in_proj_bwd", grid=(s // t,),
        in_specs=[row(w) for w in widths] + [row(D), row(D), _layer((1, D), l), _const((DIN, D))],
        out_specs=[row(D), row(DIN), pl.BlockSpec((1, D), lambda i: (0, 0))],
        out_shape=[SDS((s, D), F32), SDS((s, DIN), BF16), SDS((1, D), F32)], args=[*parts, x, dxm, g1, win_t])


def _wgrad(a, b, tag, xchg=None):
    s, m = a.shape
    mb = 512

    def body(a_ref, b_ref, o_ref):
        o_ref[...] = _tn(a_ref[...], b_ref[...]).astype(BF16)

    (out,), got = _hosted_call(
        body, xchg, name=f"wgrad_{tag}", grid=(m // mb,),
        in_specs=[pl.BlockSpec((s, mb), lambda mi: (0, mi)), _const((s, D))],
        out_specs=[pl.BlockSpec((mb, D), lambda mi: (mi, 0))],
        out_shape=[SDS((m, D), BF16)], args=[a, b])
    return out, got


def _bias_tables(gvec):
    def body(g_ref, o_ref):
        qc = lax.broadcasted_iota(jnp.int32, (TQ, WIN), 0) // CHUNK
        kc = lax.broadcasted_iota(jnp.int32, (TQ, WIN), 1) // CHUNK
        for var in range(NVAR):
            vec = jnp.broadcast_to(g_ref[:, var * TQ:var * TQ + NTOE], (TQ, NTOE))
            toe = pltpu.roll(vec, NTOE - TQ + 1, 1, stride=1, stride_axis=0)[:, :WIN]
            rel = (BAND - var * TQ) // CHUNK + qc - kc
            o_ref[var] = jnp.where((rel >= 0) & (rel <= N_PREV), toe, NEG)

    return pl.pallas_call(
        body, name="bias_tables", grid=(L, NH),
        in_specs=[pl.BlockSpec((None, None, 1, NG), lambda l, h: (l, h, 0, 0))],
        out_specs=pl.BlockSpec((None, NVAR, None, TQ, WIN), lambda l, h: (l, 0, h, 0, 0)),
        out_shape=SDS((L, NVAR, NH, TQ, WIN), F32),
        compiler_params=_cp(),
    )(gvec)


def _bias_tables_grad(dbias, l):
    nb = NTOE // LANE
    wb = WIN // LANE

    def body(d_ref, o_ref):
        ii = lax.broadcasted_iota(jnp.int32, (LANE, LANE), 0)
        jj = lax.broadcasted_iota(jnp.int32, (LANE, LANE), 1)
        flip = jnp.where(ii + jj == LANE - 1, 1.0, 0.0).astype(BF16)
        o_ref[...] = jnp.zeros_like(o_ref)
        for var in range(NVAR):
            blocks = []
            for b in range(nb):
                src = nb - 1 - b
                if src >= wb:
                    blocks.append(jnp.zeros((TQ, LANE), F32))
                    continue
                xv = d_ref[var, :, src * LANE:(src + 1) * LANE]
                hi = xv.astype(BF16)
                lo = (xv - hi.astype(F32)).astype(BF16)
                blocks.append(_nn(hi, flip) + _nn(lo, flip))
            rev = jnp.concatenate(blocks, axis=1)
            skew = pltpu.roll(rev, NTOE - TQ + 1, 1, stride=1, stride_axis=0)
            off = NG - NTOE - var * TQ
            o_ref[:, off:off + NTOE] += jnp.sum(skew, axis=0, keepdims=True)

    return pl.pallas_call(
        body, name=f"bias_tables_grad_l{l}", grid=(NH,),
        in_specs=[pl.BlockSpec((NVAR, None, TQ, WIN), lambda h: (0, h, 0, 0))],
        out_specs=pl.BlockSpec((None, 1, NG), lambda h: (h, 0, 0)),
        out_shape=SDS((NH, 1, NG), F32),
        compiler_params=_cp(),
    )(dbias)


_SIBLING = (0, 0, 1)
_CHIPS = [(1, 0, 0), (0, 1, 0), (1, 1, 0)]
_MASKS = [_SIBLING] + _CHIPS + [(1, 0, 1), (0, 1, 1), (1, 1, 1)]


def _position():
    return lax.axis_index("x"), lax.axis_index("y"), lax.axis_index("c")


def _peer(pos, mask):
    return tuple(1 - a if f else a for a, f in zip(pos, mask))


def _index(pos):
    return 4 * pos[0] + 2 * pos[1] + pos[2]


def _exchange_phases(items, src, dst, sems):
    send_sems, recv_sems, local_sems = sems
    me = _position()
    sib = _peer(me, _SIBLING)

    def remote(s_ref, d_ref, pi, n, to):
        return pltpu.make_async_remote_copy(
            src_ref=s_ref, dst_ref=d_ref, send_sem=send_sems.at[pi, n], recv_sem=recv_sems.at[pi, n],
            device_id=to, device_id_type=MESH_ID)

    def parts(n):
        it = items[n]
        r = SHARD_ROWS[it[1]]
        block = lambda ref, pos: ref.at[pl.ds(_index(pos) * r, r), :]
        if it[0] == "gather":
            own = src[n].at[it[2]]
            local = pltpu.make_async_copy(own, block(dst[n], me), local_sems.at[n])
            sends = [remote(own, block(dst[n], me), pi, n, _peer(me, m)) for pi, m in enumerate([_SIBLING] + _CHIPS)]
            hops = [(remote(block(dst[n], _peer(me, m)), block(dst[n], _peer(me, m)), 1 + j, n, _peer(me, m)),
                     remote(block(dst[n], _peer(me, m)), block(dst[n], _peer(me, m)), 4 + j, n, sib))
                    for j, m in enumerate(_CHIPS)]
            lands = [remote(own, block(dst[n], sib), 0, n, sib)]
            lands += [remote(own, block(dst[n], _peer(sib, m)), 4 + j, n, sib) for j, m in enumerate(_CHIPS)]
        else:
            local = pltpu.make_async_copy(block(src[n], me), dst[n].at[_index(me)], local_sems.at[n])
            sends = [remote(block(src[n], _peer(me, m)), dst[n].at[_index(me)], pi, n, _peer(me, m))
                     for pi, m in enumerate(_MASKS)]
            hops = []
            lands = [remote(block(src[n], me), dst[n].at[_index(_peer(me, m))], pi, n, _peer(me, m))
                     for pi, m in enumerate(_MASKS)]
        return local, sends, hops, lands

    def start():
        for n in range(len(items)):
            local, sends, _, _ = parts(n)
            local.start()
            for cp in sends:
                cp.start()

    def relay():
        for n in range(len(items)):
            for arrived, onward in parts(n)[2]:
                arrived.wait_recv()
                onward.start()

    def finish():
        for n in range(len(items)):
            local, sends, hops, lands = parts(n)
            for cp in lands:
                cp.wait_recv()
            for cp in sends + [onward for _, onward in hops]:
                cp.wait_send()
            local.wait()

    return start, relay, finish


def _hosted_call(body, xchg, *, name, grid, in_specs, out_specs, out_shape, args, scratch_shapes=(), relay_at=0.7):
    if not xchg:
        outs = pl.pallas_call(
            body, name=name, grid=grid, in_specs=list(in_specs), out_specs=list(out_specs),
            out_shape=list(out_shape), scratch_shapes=list(scratch_shapes), compiler_params=_cp())(*args)
        return outs, []
    items = [it for it, _ in xchg]
    n_in, n_out, n_scr, nit = len(args), len(out_shape), len(scratch_shapes), len(items)
    hbm = pl.BlockSpec(memory_space=pl.ANY)
    steps = int(np.prod(grid))
    relay_step = min(int(relay_at * steps), steps - 1)

    def dst_shape(it):
        r = SHARD_ROWS[it[1]]
        return SDS((NDEV * r, D) if it[0] == "gather" else (NDEV, r, D), BF16)

    def wrapped(*refs):
        ins = refs[:n_in]
        src = refs[n_in:n_in + nit]
        outs = refs[n_in + nit:n_in + nit + n_out]
        dst = refs[n_in + nit + n_out:n_in + 2 * nit + n_out]
        scratch = refs[n_in + 2 * nit + n_out:n_in + 2 * nit + n_out + n_scr]
        start, relay, finish = _exchange_phases(items, src, dst, refs[n_in + 2 * nit + n_out + n_scr:])
        step = 0
        for d, g in enumerate(grid):
            step = step * g + pl.program_id(d)
        pl.when(step == 0)(start)
        body(*ins, *outs, *scratch)
        pl.when(step == relay_step)(relay)
        pl.when(step == steps - 1)(finish)

    npeer = len(_MASKS)
    res = pl.pallas_call(
        wrapped, name=name, grid=grid,
        in_specs=list(in_specs) + [hbm] * nit,
        out_specs=list(out_specs) + [hbm] * nit,
        out_shape=list(out_shape) + [dst_shape(it) for it in items],
        scratch_shapes=list(scratch_shapes) + [
            pltpu.SemaphoreType.DMA((npeer, nit)), pltpu.SemaphoreType.DMA((npeer, nit)), pltpu.SemaphoreType.DMA((nit,))],
        compiler_params=_cp(),
    )(*args, *[a for _, a in xchg])
    return list(res[:n_out]), list(res[n_out:])


def _gather_first(xchg):
    def body(o_ref):
        o_ref[...] = jnp.zeros_like(o_ref)

    return _hosted_call(
        body, xchg, name="gather_first", grid=(1,),
        in_specs=[], out_specs=[pl.BlockSpec((8, LANE), lambda i: (0, 0))], out_shape=[SDS((8, LANE), F32)], args=[])[1]


def _sum_slots(slots, xchg=None):
    _, r, _ = slots[0].shape
    rt = 64

    def body(*refs):
        o_ref = refs[L]
        for l in range(L):
            acc = refs[l][0].astype(F32)
            for d in range(1, NDEV):
                acc = acc + refs[l][d].astype(F32)
            o_ref[l] = acc

    (out,), got = _hosted_call(
        body, xchg, name=f"sum_slots_r{r}" + ("_x" if xchg else ""), grid=(r // rt,),
        in_specs=[pl.BlockSpec((NDEV, rt, D), lambda i: (0, i, 0))] * L,
        out_specs=[pl.BlockSpec((L, rt, D), lambda i: (0, i, 0))],
        out_shape=[SDS((L, r, D), F32)], args=list(slots))
    return out, got


def _exchange_small(v, reduce):
    rows = v.shape[0]

    def body(v_ref, o_ref, *scratch):
        if reduce:
            slots, send_sems, recv_sems = scratch
        else:
            slots = o_ref
            send_sems, recv_sems = scratch
        me = _position()
        slots[_index(me)] = v_ref[...]
        sends = []
        for pi, mask in enumerate(_MASKS):
            cp = pltpu.make_async_remote_copy(
                src_ref=v_ref, dst_ref=slots.at[_index(me)], send_sem=send_sems.at[pi], recv_sem=recv_sems.at[pi],
                device_id=_peer(me, mask), device_id_type=MESH_ID)
            cp.start()
            sends.append(cp)
        for pi, mask in enumerate(_MASKS):
            peer = _peer(me, mask)
            pltpu.make_async_remote_copy(
                src_ref=v_ref, dst_ref=slots.at[_index(peer)], send_sem=send_sems.at[pi], recv_sem=recv_sems.at[pi],
                device_id=peer, device_id_type=MESH_ID).wait_recv()
        for cp in sends:
            cp.wait_send()
        if reduce:
            acc = slots[0]
            for d in range(1, NDEV):
                acc = acc + slots[d]
            o_ref[...] = acc

    vm = pl.BlockSpec(memory_space=pltpu.VMEM)
    sems = [pltpu.SemaphoreType.DMA((len(_MASKS),)), pltpu.SemaphoreType.DMA((len(_MASKS),))]
    return pl.pallas_call(
        body, name="reduce_small" if reduce else "gather_small",
        in_specs=[vm], out_specs=vm,
        out_shape=SDS((rows, LANE) if reduce else (NDEV, rows, LANE), F32),
        scratch_shapes=([pltpu.VMEM((NDEV, rows, LANE), F32)] if reduce else []) + sems,
        compiler_params=_cp(),
    )(v)


def _adamw_update(w_ref, g_ref, m_ref, v_ref, d_ref, nm_ref, nv_ref):
    gv = g_ref[...]
    mn = B1 * m_ref[...] + (1.0 - B1) * gv
    vn = B2 * v_ref[...] + (1.0 - B2) * jnp.square(gv)
    nm_ref[...] = mn
    nv_ref[...] = vn
    m_hat = mn / (1.0 - B1 ** STEP)
    v_hat = vn / (1.0 - B2 ** STEP)
    d_ref[...] = -LR * (m_hat / (jnp.sqrt(v_hat) + AEPS) + WD * w_ref[...])


def _adamw_small(ws, gs, ms, vs):
    n = len(ws)

    def body(*refs):
        for i in range(n):
            _adamw_update(*[refs[j * n + i] for j in range(7)])

    vm = pl.BlockSpec(memory_space=pltpu.VMEM)
    res = pl.pallas_call(
        body, name="adamw_small", in_specs=[vm] * (4 * n), out_specs=[vm] * (3 * n),
        out_shape=[SDS(w.shape, F32) for _ in range(3) for w in ws],
        compiler_params=_cp(),
    )(*ws, *gs, *ms, *vs)
    return res[:n], res[n:2 * n], res[2 * n:]


def _adamw(w, g, m, v):
    rows, cols = w.shape
    t = rows
    for cand in (512, 256, 128, 64, 32, 16, 8):
        if rows % cand == 0:
            t = cand
            break

    def body(*refs):
        _adamw_update(*refs)

    blk = pl.BlockSpec((t, cols), lambda i: (i, 0))
    return pl.pallas_call(
        body, name=f"adamw_{rows}x{cols}", grid=(rows // t,),
        in_specs=[blk] * 4, out_specs=[blk] * 3,
        out_shape=[SDS((rows, cols), F32)] * 3,
        compiler_params=_cp(),
    )(w, g, m, v)


_DIST0 = BAND + TQ - 1
_N_FAR = _DIST0 - REL_CLIP + 1
_N_NEAR = NG - _N_FAR - (2 * REL_CLIP - 1)


def _bias_vector(rel_bias):
    far = jnp.broadcast_to(rel_bias[..., -1:], (L, NH, _N_FAR))
    near = jnp.broadcast_to(rel_bias[..., :1], (L, NH, _N_NEAR))
    return jnp.concatenate([far, lax.rev(rel_bias[..., 1:-1], (2,)), near], axis=2)[:, :, None, :]


def _bias_vector_grad(dgr):
    first = jnp.sum(dgr[..., :_N_NEAR], axis=-1, keepdims=True)
    last = jnp.sum(dgr[..., NG - _N_FAR:], axis=-1, keepdims=True)
    return jnp.concatenate([first, dgr[..., _N_NEAR:NG - _N_FAR], last], axis=-1)


def _pool_blockdiag(pool_w):
    eye = jnp.eye(2, dtype=F32)
    pw = pool_w.reshape(L, 2, 2, HD, HD)
    return jnp.einsum("ljaik,ab->ljaibk", pw, eye).reshape(L, 2, LANE, LANE)


def _pool_blockdiag_grad(dwbd):
    d = dwbd.reshape(L, 2, 2, HD, 2, HD)
    return jnp.stack([d[:, :, 0, :, 0, :], d[:, :, 1, :, 1, :]], axis=2).reshape(L, 4, HD, HD)


def _pack(arrays, rows):
    flat = jnp.concatenate([a.reshape(-1).astype(F32) for a in arrays])
    return jnp.pad(flat, (0, rows * LANE - flat.shape[0])).reshape(rows, LANE)


def _unpack(packed, shapes):
    flat = packed.reshape(-1)
    out, o = [], 0
    for shp in shapes:
        n = int(np.prod(shp))
        out.append(flat[o:o + n].reshape(shp))
        o += n
    return out


def _rows_for(shapes):
    n = sum(int(np.prod(s)) for s in shapes)
    return -(-n // (8 * LANE)) * 8


def _grads(x, target, small_w, shards):
    g1, qg, kg, rb, cw, pw, ps, g2 = small_w
    g1 = g1.reshape(L, 1, D)
    g2 = g2.reshape(L, 1, D)
    qg2 = jnp.tile(qg, (1, 2)).reshape(L, 1, LANE)
    kg2 = jnp.tile(kg, (1, 2)).reshape(L, 1, LANE)
    ps3 = ps.reshape(L, 1, PWD)
    bias = _bias_tables(_bias_vector(rb))
    wbd = _pool_blockdiag(pw).astype(BF16)

    def gather(*kl):
        return [(("gather", k, l), shards[k]) for k, l in kl if l < L]

    full = {}

    def arrived(got, *kl):
        full.update(zip([x for x in kl if x[1] < L], got))

    arrived(_gather_first(gather((0, 0), (1, 0))), (0, 0), (1, 0))
    saved = []
    h = x
    for l in range(L):
        kl = ((2, 0), (1, 1)) if l == 0 else ((1, l + 1),)
        (p, h_b, qkv), got = _in_proj(h, g1, full[0, l], qg2, kg2, l, gather(*kl))
        arrived(got, *kl)
        kl = ((3, 0), (3, 1)) if l == 0 else ((3, l + 1),)
        (mix, lse), got = _attn_fwd(qkv, bias, l, gather(*kl))
        arrived(got, *kl)
        mix = _conv_pool_fwd(p, mix, cw, wbd, ps3, l)
        kl = ((2, l + 1), (0, l + 1))
        (xm, a, xo), got = _mlp_fwd(h, mix, full[1, l], g2, full[2, l], full[3, l], l, gather(*kl))
        arrived(got, *kl)
        saved.append((h, h_b, p, qkv, mix, lse, xm, a))
        h = xo
    dx, sq = _loss_grad(h, target)

    grads = {}
    slots = {}

    def scatter(*kl):
        return [(("scatter", k), grads[k, l]) for k, l in kl if l < L]

    def left(got, *kl):
        slots.update(zip([x for x in kl if x[1] < L], got))

    per_layer = [None] * L
    for l in reversed(range(L)):
        x_in, h_b, p, qkv, mix, lse, xm, a = saved[l]
        (dxm, dmix, f_b, da_b, h2_b, dxo_b, dxm_b, dg2), got = _mlp_bwd(
            dx, a, xm, g2, full[2, l], full[3, l], full[1, l], l, scatter((3, l + 1)))
        left(got, (3, l + 1))
        grads[1, l], _ = _wgrad(mix, dxm_b, f"w_out_l{l}")
        grads[2, l], got = _wgrad(da_b, h2_b, f"w_mlp1_l{l}", scatter((1, l)))
        left(got, (1, l))
        grads[3, l], _ = _wgrad(f_b, dxo_b, f"w_mlp2_l{l}")
        (dq, dk, dv, dbias, dqg, dkg), got = _attn_bwd(
            qkv, p, lse, dmix, qg2, kg2, bias, l, scatter((2, l), (0, l + 1)))
        left(got, (2, l), (0, l + 1))
        dgb, dgc, dhin, du, dcw, dwbd, dps = _conv_pool_bwd(p, dmix, cw, wbd, ps3, l)
        (dx, dp_b, dg1), got = _in_proj_bwd(
            [dq, dk, dv, dgb, dgc, dhin, du], x_in, dxm, g1, full[0, l], l, scatter((3, 0)) if l == 0 else None)
        left(got, (3, 0))
        grads[0, l], _ = _wgrad(dp_b, h_b, f"w_in_l{l}")
        per_layer[l] = (dg1, dg2, dqg, dkg, _bias_tables_grad(dbias, l), dcw, dwbd, dps)
    g_w1_t, got = _sum_slots([slots[2, l] for l in range(L)], scatter((0, 0)))
    left(got, (0, 0))
    sums = [_sum_slots([slots[k, l] for l in range(L)])[0] for k in (0, 1)] + [
        g_w1_t, _sum_slots([slots[3, l] for l in range(L)])[0]]

    st = [jnp.stack([per_layer[l][k] for l in range(L)]) for k in range(8)]
    small = dict(
        g1=st[0].reshape(L, D), g2=st[1].reshape(L, D),
        qg=st[2].reshape(L, NH, HD).sum(1), kg=st[3].reshape(L, NH, HD).sum(1),
        rb=_bias_vector_grad(st[4].reshape(L, NH, NG)), cw=st[5], pw=_pool_blockdiag_grad(st[6]),
        ps=st[7].reshape(L, PWD))
    return sq, dx, sums, small


def kernel(x, norm1_g, w_in, q_norm_g, k_norm_g, rel_bias, conv_w, pool_w, pool_scale, w_out, norm2_g, w_mlp1, w_mlp2, loss_target, m_norm1_g, m_w_in, m_q_norm_g, m_k_norm_g, m_rel_bias, m_conv_w, m_pool_w, m_pool_scale, m_w_out, m_norm2_g, m_w_mlp1, m_w_mlp2, v_norm1_g, v_w_in, v_q_norm_g, v_k_norm_g, v_rel_bias, v_conv_w, v_pool_w, v_pool_scale, v_w_out, v_norm2_g, v_w_mlp1, v_w_mlp2):
    me = _index(_position())
    cshard = CW // NDEV

    shards = [jnp.swapaxes(w_in, 1, 2).astype(BF16), w_out.astype(BF16),
              jnp.swapaxes(w_mlp1, 1, 2).astype(BF16), w_mlp2.astype(BF16)]
    cw_all = _exchange_small(_pack([conv_w], 8), reduce=False)
    cw_full = jnp.concatenate(
        [cw_all[d].reshape(-1)[:L * 3 * cshard].reshape(L, 3, cshard) for d in range(NDEV)], axis=2)

    small_w = (norm1_g, q_norm_g, k_norm_g, rel_bias, cw_full, pool_w, pool_scale, norm2_g)
    sq, grad_x, (g_win_t, g_wout, g_w1_t, g_w2), small = _grads(x[0], loss_target[0], small_w, shards)
    g_w_in = jnp.swapaxes(g_win_t, 1, 2)
    g_w_mlp1 = jnp.swapaxes(g_w1_t, 1, 2)

    names = ("g1", "qg", "kg", "rb", "cw", "pw", "ps", "g2")
    gshapes = [(L, D), (L, HD), (L, HD), (L, NH, 2 * REL_CLIP + 1), (L, 3, CW), (L, 4, HD, HD), (L, PWD), (L, D)]
    garrs = [small[n] for n in names]
    rows = _rows_for(gshapes + [(1,)])
    total = _exchange_small(_pack(garrs + [sq[0, :1]], rows), reduce=True)
    g_g1, g_qg, g_kg, g_rb, g_cw_full, g_pw, g_ps, g_g2, sq_sum = _unpack(total, gshapes + [(1,)])
    loss = (0.5 / D) * sq_sum[0]
    g_cw = lax.dynamic_slice_in_dim(g_cw_full, me * cshard, cshard, axis=2)

    def big(w, g, m, v):
        shp = w.shape
        r = lambda a: a.reshape(-1, shp[-1])
        return [o.reshape(shp) for o in _adamw(r(w), r(g), r(m), r(v))]

    up_in = big(w_in, g_w_in, m_w_in, v_w_in)
    up_out = big(w_out, g_wout, m_w_out, v_w_out)
    up_1 = big(w_mlp1, g_w_mlp1, m_w_mlp1, v_w_mlp1)
    up_2 = big(w_mlp2, g_w2, m_w_mlp2, v_w_mlp2)

    sw = [norm1_g, q_norm_g, k_norm_g, rel_bias, conv_w, pool_w, pool_scale, norm2_g]
    sg = [g_g1, g_qg, g_kg, g_rb, g_cw, g_pw, g_ps, g_g2]
    sm = [m_norm1_g, m_q_norm_g, m_k_norm_g, m_rel_bias, m_conv_w, m_pool_w, m_pool_scale, m_norm2_g]
    sv = [v_norm1_g, v_q_norm_g, v_k_norm_g, v_rel_bias, v_conv_w, v_pool_w, v_pool_scale, v_norm2_g]
    s_delta, s_m, s_v = _adamw_small(sw, sg, sm, sv)

    def order(small_list, in_, out_, m1, m2):
        g1_, qg_, kg_, rb_, cw_, pw_, ps_, g2_ = small_list
        return [g1_, in_, qg_, kg_, rb_, cw_, pw_, ps_, out_, g2_, m1, m2]

    grads = order(sg, g_w_in, g_wout, g_w_mlp1, g_w2)
    deltas = order(s_delta, up_in[0], up_out[0], up_1[0], up_2[0])
    new_m = order(s_m, up_in[1], up_out[1], up_1[1], up_2[1])
    new_v = order(s_v, up_in[2], up_out[2], up_1[2], up_2[2])
    return (loss, grad_x[None], *grads, *deltas, *new_m, *new_v)
```

```python
import numpy as np
import jax
import jax.numpy as jnp
from jax import lax
from jax.experimental import pallas as pl
from jax.experimental.pallas import tpu as pltpu

F32 = jnp.float32
BF16 = jnp.bfloat16
SDS = jax.ShapeDtypeStruct
MESH_ID = pl.DeviceIdType.MESH

D = 1024
L = 4
CHUNK = 64
N_PREV = 8
HD = 64
NH = 8
AW = 512
CW = 256
PWD = 256
DIN = 2560
DFF = 4096
EPS = 1e-6
NEG = -1e30
REL_CLIP = 128
POOL_WINDOWS = (2, 4, 8, 16)
LR, B1, B2, AEPS, WD, STEP = 0.001, 0.9, 0.999, 1e-08, 0.01, 10

NDEV = 8
LANE = 128
BAND = N_PREV * CHUNK
TQ = 256
WIN = TQ + BAND
NVAR = BAND // TQ + 1
NTOE = -(-(WIN + TQ - 1) // LANE) * LANE
NG = (NVAR - 1) * TQ + NTOE
PAD = 16
RB_NORM = 64
RB_SOFT = 16
VMEM_LIMIT = 56 * 1024 * 1024
SHARD_ROWS = (DIN // NDEV, D // NDEV, DFF // NDEV, DFF // NDEV)


def _cp(**kw):
    return pltpu.CompilerParams(vmem_limit_bytes=VMEM_LIMIT, **kw)


def _nn(a, b):
    return jnp.dot(a, b, preferred_element_type=F32)


def _nt(a, b):
    return lax.dot_general(a, b, (((1,), (1,)), ((), ())), preferred_element_type=F32)


def _tn(a, b):
    return lax.dot_general(a, b, (((0,), (0,)), ((), ())), preferred_element_type=F32)


def _const(shape):
    n = len(shape)
    return pl.BlockSpec(shape, lambda *_: (0,) * n, pipeline_mode=pl.Buffered(1))


def _layer(shape, l):
    n = len(shape)
    return pl.BlockSpec((None,) + tuple(shape), lambda *_: (l,) + (0,) * n, pipeline_mode=pl.Buffered(1))


def _lo_mask():
    return lax.broadcasted_iota(jnp.int32, (1, LANE), 1) < HD


def _half_sum(t, lo):
    s_lo = jnp.sum(jnp.where(lo, t, 0.0), axis=-1, keepdims=True)
    s_hi = jnp.sum(jnp.where(lo, 0.0, t), axis=-1, keepdims=True)
    return jnp.where(lo, s_lo, s_hi)


def _head_norm(x, lo):
    r = lax.rsqrt(_half_sum(x * x, lo) * (1.0 / HD) + EPS)
    return x * r, r


def _head_norm_bwd(dy, xn, r, g, lo):
    dxn = dy * g
    mu = _half_sum(dxn * xn, lo) * (1.0 / HD)
    return r * (dxn - xn * mu), dy * xn


def _rms_bwd(dy, x, g):
    r = lax.rsqrt(jnp.mean(x * x, axis=-1, keepdims=True) + EPS)
    xn = x * r
    dxn = dy * g
    mu = jnp.mean(dxn * xn, axis=-1, keepdims=True)
    return r * (dxn - xn * mu), dy * xn


def _in_proj(x, g1, win_t, qg2, kg2, l, xchg=None):
    s = x.shape[0]
    t = min(512, s)
    nblk = AW // LANE

    def body(x_ref, g_ref, w_ref, qg_ref, kg_ref, p_ref, h_ref, qkv_ref):
        xv = x_ref[...]
        r = lax.rsqrt(jnp.mean(xv * xv, axis=-1, keepdims=True) + EPS)
        h = (xv * r * g_ref[...]).astype(BF16)
        h_ref[...] = h
        p_ref[...] = _nt(h, w_ref[...])
        lo = _lo_mask()
        gains = (qg_ref[...] * (HD ** -0.5), kg_ref[...])
        for r0 in range(0, t, RB_NORM):
            rows = pl.ds(r0, RB_NORM)
            for c in range(3 * nblk):
                cols = pl.ds(c * LANE, LANE)
                v = p_ref[rows, cols]
                if c < 2 * nblk:
                    v = _head_norm(v, lo)[0] * gains[c // nblk]
                qkv_ref[rows, cols] = v.astype(BF16)

    row = lambda c: pl.BlockSpec((t, c), lambda i: (i, 0))
    return _hosted_call(
        body, xchg, name="in_proj", grid=(s // t,),
        in_specs=[row(D), _layer((1, D), l), _const((DIN, D)), _layer((1, LANE), l), _layer((1, LANE), l)],
        out_specs=[row(DIN), row(D), row(3 * AW)],
        out_shape=[SDS((s, DIN), F32), SDS((s, D), BF16), SDS((s, 3 * AW), BF16)], args=[x, g1, win_t, qg2, kg2])


def _bias_spec():
    return pl.BlockSpec((None, 2, TQ, WIN), lambda j, i: (jnp.maximum(NVAR - 1 - i, 0), j, 0, 0))


def _bias_layer_spec(l):
    return pl.BlockSpec((None, None, 2, TQ, WIN), lambda j, i: (l, jnp.maximum(NVAR - 1 - i, 0), j, 0, 0))


def _attn_fwd(qkv, bias, l, xchg=None):
    s = qkv.shape[0]
    nq = s // TQ

    def body(q_ref, k_ref, v_ref, b_ref, o_ref, lse_ref, o32_ref, s_ref, p_ref, m_ref, den_ref, o0_ref):
        i = pl.program_id(1)
        ks = pl.multiple_of(jnp.maximum(i * TQ - BAND, 0), TQ)
        lo = _lo_mask()
        q = q_ref[...]
        kwin = k_ref[pl.ds(ks, WIN), :]
        vwin = v_ref[pl.ds(ks, WIN), :]
        for half in range(2):
            m_ = lo if half == 0 else jnp.logical_not(lo)
            s_ref[half] = _nt(jnp.where(m_, q, jnp.zeros_like(q)), kwin)
            for r0 in range(0, TQ, RB_SOFT):
                rows = pl.ds(r0, RB_SOFT)
                mx = jnp.max(s_ref[half, rows, :] + b_ref[half, rows, :], axis=-1, keepdims=True)
                m_ref[rows, :] = jnp.broadcast_to(mx, (RB_SOFT, LANE))
            for r0 in range(0, TQ, RB_SOFT):
                rows = pl.ds(r0, RB_SOFT)
                mx = m_ref[rows, 0:1]
                e = jnp.exp(s_ref[half, rows, :] + b_ref[half, rows, :] - mx)
                p_ref[half, rows, :] = e.astype(BF16)
                den = jnp.sum(e, axis=-1, keepdims=True)
                den_ref[rows, :] = jnp.broadcast_to(den, (RB_SOFT, LANE))
                lse = mx + jnp.log(den)
                if half == 0:
                    lse_ref[rows, :] = jnp.broadcast_to(lse, (RB_SOFT, LANE))
                else:
                    lse_ref[rows, :] = jnp.where(lo, lse_ref[rows, :], lse)
            o = _nn(p_ref[half], vwin) * (1.0 / den_ref[...])
            if half == 0:
                o0_ref[...] = o
            else:
                o = jnp.where(lo, o0_ref[...], o)
                o32_ref[...] = o
                o_ref[...] = o.astype(BF16)

    tile = pl.BlockSpec((TQ, LANE), lambda j, i: (i, j))
    stat = pltpu.VMEM((TQ, LANE), F32)
    return _hosted_call(
        body, xchg, name="attn_fwd", grid=(NH // 2, nq),
        in_specs=[
            tile,
            pl.BlockSpec((s, LANE), lambda j, i: (0, AW // LANE + j)),
            pl.BlockSpec((s, LANE), lambda j, i: (0, 2 * AW // LANE + j)),
            _bias_layer_spec(l),
        ],
        out_specs=[tile, tile, tile],
        out_shape=[SDS((s, D), BF16), SDS((s, AW), F32), SDS((s, AW), F32)], args=[qkv, qkv, qkv, bias],
        scratch_shapes=[pltpu.VMEM((2, TQ, WIN), F32), pltpu.VMEM((2, TQ, WIN), BF16), stat, stat, stat])


_C0 = 3 * AW // LANE


def _cp_in_specs(s, l):
    blk = lambda f: pl.BlockSpec((s, LANE), f)
    return [
        blk(lambda g: (0, _C0 + jnp.minimum(g, 1))),
        blk(lambda g: (0, _C0 + 2 + jnp.minimum(g, 1))),
        blk(lambda g: (0, _C0 + 4 + jnp.minimum(g, 1))),
        blk(lambda g: (0, _C0 + 6 + jnp.maximum(g - 2, 0))),
        pl.BlockSpec((None, 3, LANE), lambda g: (l, 0, jnp.minimum(g, 1))),
        pl.BlockSpec((None, None, LANE, LANE), lambda g: (l, jnp.maximum(g - 2, 0), 0, 0)),
        pl.BlockSpec((None, 1, LANE), lambda g: (l, 0, jnp.maximum(g - 2, 0))),
    ]


def _pool_window_sums(u_ref, buf_a, buf_b, jj, s, rt):
    nrt = s // rt
    for t in range(nrt):
        buf_a[pl.ds(PAD + t * rt, rt), :] = u_ref[pl.ds(t * rt, rt), :]

    def stage(src, dst, sh):
        for t in range(nrt):
            r0 = PAD + t * rt
            dst[pl.ds(r0, rt), :] = src[pl.ds(r0, rt), :] + src[pl.ds(r0 - sh, rt), :]

    stage(buf_a, buf_b, 1)
    stage(buf_b, buf_a, 2)

    @pl.when(jj == 1)
    def _():
        stage(buf_a, buf_b, 4)
        stage(buf_b, buf_a, 8)


def _pool_counts(jj, lo, r0, rt):
    w = jnp.where(lo, jnp.where(jj == 0, 2.0, 8.0), jnp.where(jj == 0, 4.0, 16.0))
    pos1 = (lax.broadcasted_iota(jnp.int32, (rt, LANE), 0) + (r0 + 1)).astype(F32)
    return jnp.minimum(pos1, w)


def _conv_pool_fwd(p, mix, conv_w, wbd, pscale, l):
    s = p.shape[0]
    rt = min(256, s)
    nrt = s // rt

    def body(gb_ref, gc_ref, hin_ref, u_ref, cw_ref, wbd_ref, ps_ref, mix_in, o_ref, buf_a, buf_b):
        del mix_in
        g = pl.program_id(0)
        zpad = jnp.zeros((PAD, LANE), F32)
        buf_a[pl.ds(0, PAD), :] = zpad
        buf_b[pl.ds(0, PAD), :] = zpad

        @pl.when(g < 2)
        def _conv():
            for t in range(nrt):
                buf_a[pl.ds(PAD + t * rt, rt), :] = gc_ref[pl.ds(t * rt, rt), :] * hin_ref[pl.ds(t * rt, rt), :]
            w0, w1, w2 = cw_ref[0:1, :], cw_ref[1:2, :], cw_ref[2:3, :]
            for t in range(nrt):
                r0 = PAD + t * rt
                y = w0 * buf_a[pl.ds(r0 - 2, rt), :] + w1 * buf_a[pl.ds(r0 - 1, rt), :] + w2 * buf_a[pl.ds(r0, rt), :]
                o_ref[pl.ds(t * rt, rt), :] = (gb_ref[pl.ds(t * rt, rt), :] * y).astype(BF16)

        @pl.when(g >= 2)
        def _pool():
            jj = g - 2
            lo = _lo_mask()
            _pool_window_sums(u_ref, buf_a, buf_b, jj, s, rt)
            wb = wbd_ref[...]
            for t in range(nrt):
                r0 = PAD + t * rt
                wsum = jnp.where(lo, buf_b[pl.ds(r0, rt), :], buf_a[pl.ds(r0, rt), :])
                m = wsum / _pool_counts(jj, lo, t * rt, rt) - u_ref[pl.ds(t * rt, rt), :]
                o_ref[pl.ds(t * rt, rt), :] = (_nn(m.astype(BF16), wb) * ps_ref[...]).astype(BF16)

    return pl.pallas_call(
        body, name="conv_pool_fwd", grid=(4,),
        in_specs=_cp_in_specs(s, l) + [pl.BlockSpec(memory_space=pl.ANY)],
        out_specs=pl.BlockSpec((s, LANE), lambda g: (0, AW // LANE + g)),
        out_shape=SDS((s, D), BF16),
        scratch_shapes=[pltpu.VMEM((s + 2 * PAD, LANE), F32), pltpu.VMEM((s + 2 * PAD, LANE), F32)],
        input_output_aliases={7: 0},
        compiler_params=_cp(),
    )(p, p, p, p, conv_w, wbd, pscale, mix)


def _mlp_fwd(x, mix, wout, g2, w1_t, w2, l, xchg=None):
    s = x.shape[0]
    t = min(256, s)

    def body(x_ref, mix_ref, wo_ref, g_ref, w1_ref, w2_ref, xm_ref, a_ref, xo_ref):
        xm = x_ref[...] + _nn(mix_ref[...], wo_ref[...])
        xm_ref[...] = xm
        r = lax.rsqrt(jnp.mean(xm * xm, axis=-1, keepdims=True) + EPS)
        h2 = (xm * r * g_ref[...]).astype(BF16)
        a = _nt(h2, w1_ref[...])
        a_ref[...] = a.astype(BF16)
        f = jnp.square(jnp.maximum(a, 0.0)).astype(BF16)
        xo_ref[...] = xm + _nn(f, w2_ref[...])

    row = lambda c: pl.BlockSpec((t, c), lambda i: (i, 0))
    return _hosted_call(
        body, xchg, name="mlp_fwd", grid=(s // t,),
        in_specs=[row(D), row(D), _const((D, D)), _layer((1, D), l), _const((DFF, D)), _const((DFF, D))],
        out_specs=[row(D), row(DFF), row(D)],
        out_shape=[SDS((s, D), F32), SDS((s, DFF), BF16), SDS((s, D), F32)], args=[x, mix, wout, g2, w1_t, w2])


def _loss_grad(y, target):
    s = y.shape[0]
    t = min(512, s)

    def body(y_ref, t_ref, dy_ref, acc_ref):
        @pl.when(pl.program_id(0) == 0)
        def _():
            acc_ref[...] = jnp.zeros_like(acc_ref)
        e = y_ref[...] - t_ref[...]
        dy_ref[...] = e * (1.0 / D)
        acc_ref[...] += jnp.sum(e * e)

    row = pl.BlockSpec((t, D), lambda i: (i, 0))
    return pl.pallas_call(
        body, name="loss_grad", grid=(s // t,),
        in_specs=[row, row],
        out_specs=[row, pl.BlockSpec((8, LANE), lambda i: (0, 0))],
        out_shape=[SDS((s, D), F32), SDS((8, LANE), F32)],
        compiler_params=_cp(),
    )(y, target)


def _mlp_bwd(dxo, a, xm, g2, w1_t, w2, wout, l, xchg=None):
    s = dxo.shape[0]
    t = min(256, s)

    def body(dxo_ref, a_ref, xm_ref, g_ref, w1_ref, w2_ref, wo_ref,
             dxm_ref, dmix_ref, f_ref, da_ref, h2_ref, dxob_ref, dxmb_ref, dg_ref):
        @pl.when(pl.program_id(0) == 0)
        def _():
            dg_ref[...] = jnp.zeros_like(dg_ref)
        dxo = dxo_ref[...]
        dxob = dxo.astype(BF16)
        dxob_ref[...] = dxob
        ra = jnp.maximum(a_ref[...].astype(F32), 0.0)
        f_ref[...] = jnp.square(ra).astype(BF16)
        dab = (_nt(dxob, w2_ref[...]) * (2.0 * ra)).astype(BF16)
        da_ref[...] = dab
        dh2 = _nn(dab, w1_ref[...])
        xm = xm_ref[...]
        g = g_ref[...]
        r = lax.rsqrt(jnp.mean(xm * xm, axis=-1, keepdims=True) + EPS)
        h2_ref[...] = (xm * r * g).astype(BF16)
        dx_n, dgr = _rms_bwd(dh2, xm, g)
        dg_ref[...] += jnp.sum(dgr, axis=0, keepdims=True)
        dxm = dxo + dx_n
        dxm_ref[...] = dxm
        dxmb = dxm.astype(BF16)
        dxmb_ref[...] = dxmb
        dmix_ref[...] = _nt(dxmb, wo_ref[...])

    row = lambda c: pl.BlockSpec((t, c), lambda i: (i, 0))
    return _hosted_call(
        body, xchg, name="mlp_bwd", grid=(s // t,),
        in_specs=[row(D), row(DFF), row(D), _layer((1, D), l), _const((DFF, D)), _const((DFF, D)), _const((D, D))],
        out_specs=[row(D), row(D), row(DFF), row(DFF), row(D), row(D), row(D), pl.BlockSpec((1, D), lambda i: (0, 0))],
        out_shape=[SDS((s, D), F32), SDS((s, D), F32), SDS((s, DFF), BF16), SDS((s, DFF), BF16),
                   SDS((s, D), BF16), SDS((s, D), BF16), SDS((s, D), BF16), SDS((1, D), F32)],
        args=[dxo, a, xm, g2, w1_t, w2, wout])


def _attn_bwd(qkv, p, lse, o32, dmix, qg2, kg2, bias, l, xchg=None):
    s = p.shape[0]
    nq = s // TQ
    scale = HD ** -0.5
    rt = min(512, s)

    def body(qs_ref, kb_ref, vb_ref, q_ref, k_ref, qg_ref, kg_ref, b_ref, lse_ref, o_ref, do_ref,
             dq_ref, dk_ref, dv_ref, db_ref, dqg_ref, dkg_ref,
             dk_acc, dv_acc, s_ref, dp_ref, ds_ref, pb_ref, dqn_ref, dl_ref):
        i = pl.program_id(1)
        ks = pl.multiple_of(jnp.maximum(i * TQ - BAND, 0), TQ)
        lo = _lo_mask()

        @pl.when(i == 0)
        def _():
            dk_acc[...] = jnp.zeros_like(dk_acc)
            dv_acc[...] = jnp.zeros_like(dv_acc)
            dqg_ref[...] = jnp.zeros_like(dqg_ref)
            dkg_ref[...] = jnp.zeros_like(dkg_ref)

        @pl.when(i < NVAR)
        def _():
            db_ref[...] = jnp.zeros_like(db_ref)

        qs = qs_ref[...]
        kwin = kb_ref[pl.ds(ks, WIN), :]
        vwin = vb_ref[pl.ds(ks, WIN), :]
        do = do_ref[...]
        dob = do.astype(BF16)
        dl_ref[...] = _half_sum(do * o_ref[...], lo)
        for half in range(2):
            m_ = lo if half == 0 else jnp.logical_not(lo)
            qa = jnp.where(m_, qs, jnp.zeros_like(qs))
            doa = jnp.where(m_, dob, jnp.zeros_like(dob))
            s_ref[half] = _nt(qa, kwin)
            dp_ref[half] = _nt(doa, vwin)
            for r0 in range(0, TQ, RB_SOFT):
                rows = pl.ds(r0, RB_SOFT)
                lse_h = lse_ref[rows, half * HD:half * HD + 1]
                pm = jnp.exp(s_ref[half, rows, :] + b_ref[half, rows, :] - lse_h)
                ds = pm * (dp_ref[half, rows, :] - dl_ref[rows, half * HD:half * HD + 1])
                db_ref[half, rows, :] += ds
                ds_ref[half, rows, :] = ds.astype(BF16)
                pb_ref[half, rows, :] = pm.astype(BF16)
            dsb = ds_ref[half]
            dq_h = _nn(dsb, kwin)
            if half == 0:
                dqn_ref[...] = dq_h
            else:
                dqn_ref[...] = jnp.where(lo, dqn_ref[...], dq_h)
            dk_acc[pl.ds(ks, WIN), :] += _tn(dsb, qa)
            dv_acc[pl.ds(ks, WIN), :] += _tn(pb_ref[half], doa)
        qg, kg = qg_ref[...], kg_ref[...]
        xq, rq = _head_norm(q_ref[...], lo)
        dq, dqg_rows = _head_norm_bwd(dqn_ref[...] * scale, xq, rq, qg, lo)
        dq_ref[...] = dq.astype(BF16)
        dqg_ref[...] += jnp.sum(dqg_rows, axis=0, keepdims=True)

        @pl.when(i == nq - 1)
        def _():
            dkg = jnp.zeros((1, LANE), F32)
            for t in range(s // rt):
                rows = pl.ds(t * rt, rt)
                xk, rk = _head_norm(k_ref[rows, :], lo)
                dk, dkg_rows = _head_norm_bwd(dk_acc[rows, :], xk, rk, kg, lo)
                dk_ref[rows, :] = dk.astype(BF16)
                dv_ref[rows, :] = dv_acc[rows, :].astype(BF16)
                dkg = dkg + jnp.sum(dkg_rows, axis=0, keepdims=True)
            dkg_ref[...] = dkg

    tile = pl.BlockSpec((TQ, LANE), lambda j, i: (i, j))
    kcol = lambda c0: pl.BlockSpec((s, LANE), lambda j, i: (0, c0 + j))
    gain = pl.BlockSpec((None, 1, LANE), lambda j, i: (j, 0, 0))
    return _hosted_call(
        body, xchg, name="attn_bwd", grid=(NH // 2, nq),
        in_specs=[
            tile, kcol(AW // LANE), kcol(2 * AW // LANE), tile, kcol(AW // LANE),
            _layer((1, LANE), l), _layer((1, LANE), l),
            _bias_layer_spec(l), tile, tile, tile,
        ],
        out_specs=[tile, kcol(0), kcol(0), _bias_spec(), gain, gain],
        out_shape=[SDS((s, AW), BF16), SDS((s, AW), BF16), SDS((s, AW), BF16),
                   SDS((NVAR, NH, TQ, WIN), F32), SDS((NH // 2, 1, LANE), F32), SDS((NH // 2, 1, LANE), F32)],
        scratch_shapes=[pltpu.VMEM((s, LANE), F32), pltpu.VMEM((s, LANE), F32),
                        pltpu.VMEM((2, TQ, WIN), F32), pltpu.VMEM((2, TQ, WIN), F32),
                        pltpu.VMEM((2, TQ, WIN), BF16), pltpu.VMEM((2, TQ, WIN), BF16),
                        pltpu.VMEM((TQ, LANE), F32), pltpu.VMEM((TQ, LANE), F32)],
        args=[qkv, qkv, qkv, p, p, qg2, kg2, bias, lse, o32, dmix])


def _conv_pool_bwd(p, dmix, conv_w, wbd, pscale, l):
    s = p.shape[0]
    rt = min(256, s)
    nrt = s // rt

    def body(gb_ref, gc_ref, hin_ref, u_ref, cw_ref, wbd_ref, ps_ref, dy_ref,
             dgb_ref, dgc_ref, dhin_ref, du_ref, dcw_ref, dwbd_ref, dps_ref, buf_a, buf_b, buf_c, buf_d):
        g = pl.program_id(0)
        zpad = jnp.zeros((PAD, LANE), F32)
        for buf in (buf_a, buf_b, buf_c):
            buf[pl.ds(0, PAD), :] = zpad
            buf[pl.ds(PAD + s, PAD), :] = zpad

        @pl.when(g < 2)
        def _conv():
            for t in range(nrt):
                rows = pl.ds(t * rt, rt)
                buf_a[pl.ds(PAD + t * rt, rt), :] = gc_ref[rows, :] * hin_ref[rows, :]
                buf_b[pl.ds(PAD + t * rt, rt), :] = dy_ref[rows, :] * gb_ref[rows, :]
            w0, w1, w2 = cw_ref[0:1, :], cw_ref[1:2, :], cw_ref[2:3, :]
            d0 = jnp.zeros((1, LANE), F32)
            d1 = jnp.zeros((1, LANE), F32)
            d2 = jnp.zeros((1, LANE), F32)
            for t in range(nrt):
                rows = pl.ds(t * rt, rt)
                r0 = PAD + t * rt
                z2, z1, z0 = buf_a[pl.ds(r0 - 2, rt), :], buf_a[pl.ds(r0 - 1, rt), :], buf_a[pl.ds(r0, rt), :]
                y = w0 * z2 + w1 * z1 + w2 * z0
                dgb_ref[rows, :] = (dy_ref[rows, :] * y).astype(BF16)
                e0 = buf_b[pl.ds(r0, rt), :]
                d0 = d0 + jnp.sum(e0 * z2, axis=0, keepdims=True)
                d1 = d1 + jnp.sum(e0 * z1, axis=0, keepdims=True)
                d2 = d2 + jnp.sum(e0 * z0, axis=0, keepdims=True)
                dz = w2 * e0 + w1 * buf_b[pl.ds(r0 + 1, rt), :] + w0 * buf_b[pl.ds(r0 + 2, rt), :]
                dgc_ref[rows, :] = (dz * hin_ref[rows, :]).astype(BF16)
                dhin_ref[rows, :] = (dz * gc_ref[rows, :]).astype(BF16)
            dcw_ref[0:1, :] = d0
            dcw_ref[1:2, :] = d1
            dcw_ref[2:3, :] = d2

        @pl.when(g >= 2)
        def _pool():
            jj = g - 2
            lo = _lo_mask()
            _pool_window_sums(u_ref, buf_a, buf_b, jj, s, rt)
            wb = wbd_ref[...]
            ps = ps_ref[...]
            dps = jnp.zeros((1, LANE), F32)
            dwb = jnp.zeros((LANE, LANE), F32)
            for t in range(nrt):
                rows = pl.ds(t * rt, rt)
                r0 = PAD + t * rt
                cnt = _pool_counts(jj, lo, t * rt, rt)
                wsum = jnp.where(lo, buf_b[pl.ds(r0, rt), :], buf_a[pl.ds(r0, rt), :])
                mb = (wsum / cnt - u_ref[rows, :]).astype(BF16)
                dy = dy_ref[rows, :]
                dps = dps + jnp.sum(dy * _nn(mb, wb), axis=0, keepdims=True)
                dmp = (dy * ps).astype(BF16)
                dwb = dwb + _tn(mb, dmp)
                dm = _nt(dmp, wb)
                buf_d[rows, :] = dm
                buf_c[pl.ds(r0, rt), :] = dm / cnt
            dps_ref[...] = dps
            dwbd_ref[...] = dwb

            def stage(src, dst, sh):
                for t in range(nrt):
                    r0 = PAD + t * rt
                    dst[pl.ds(r0, rt), :] = src[pl.ds(r0, rt), :] + src[pl.ds(r0 + sh, rt), :]

            def finish(first, second):
                for t in range(nrt):
                    rows = pl.ds(t * rt, rt)
                    r0 = PAD + t * rt
                    fw = jnp.where(lo, first[pl.ds(r0, rt), :], second[pl.ds(r0, rt), :])
                    du_ref[rows, :] = (fw - buf_d[rows, :]).astype(BF16)

            stage(buf_c, buf_a, 1)
            stage(buf_a, buf_b, 2)

            @pl.when(jj == 0)
            def _():
                finish(buf_a, buf_b)

            @pl.when(jj == 1)
            def _():
                stage(buf_b, buf_c, 4)
                stage(buf_c, buf_a, 8)
                finish(buf_c, buf_a)

    cblk = pl.BlockSpec((s, LANE), lambda g: (0, jnp.minimum(g, 1)))
    pblk = pl.BlockSpec((s, LANE), lambda g: (0, jnp.maximum(g - 2, 0)))
    padded = pltpu.VMEM((s + 2 * PAD, LANE), F32)
    return pl.pallas_call(
        body, name="conv_pool_bwd", grid=(4,),
        in_specs=_cp_in_specs(s, l) + [pl.BlockSpec((s, LANE), lambda g: (0, AW // LANE + g))],
        out_specs=[cblk, cblk, cblk, pblk,
                   pl.BlockSpec((3, LANE), lambda g: (0, jnp.minimum(g, 1))),
                   pl.BlockSpec((None, LANE, LANE), lambda g: (jnp.maximum(g - 2, 0), 0, 0)),
                   pl.BlockSpec((1, LANE), lambda g: (0, jnp.maximum(g - 2, 0)))],
        out_shape=[SDS((s, CW), BF16), SDS((s, CW), BF16), SDS((s, CW), BF16), SDS((s, PWD), BF16),
                   SDS((3, CW), F32), SDS((2, LANE, LANE), F32), SDS((1, PWD), F32)],
        scratch_shapes=[padded, padded, padded, pltpu.VMEM((s, LANE), F32)],
        compiler_params=_cp(),
    )(p, p, p, p, conv_w, wbd, pscale, dmix)


def _in_proj_bwd(parts, x, dxm, g1, win_t, l, xchg=None):
    s = x.shape[0]
    t = min(256, s)
    widths = [a.shape[1] for a in parts]
    offs = [int(o) for o in np.cumsum([0] + widths[:-1])]
    n = len(parts)

    def body(*refs):
        part_refs = refs[:n]
        x_ref, dxm_ref, g_ref, w_ref, dx_ref, dp_ref, dg_ref = refs[n:]

        @pl.when(pl.program_id(0) == 0)
        def _():
            dg_ref[...] = jnp.zeros_like(dg_ref)
        for r, o, w in zip(part_refs, offs, widths):
            dp_ref[:, o:o + w] = r[...]
        dh = _nn(dp_ref[...], w_ref[...])
        dx_n, dgr = _rms_bwd(dh, x_ref[...], g_ref[...])
        dg_ref[...] += jnp.sum(dgr, axis=0, keepdims=True)
        dx_ref[...] = dxm_ref[...] + dx_n

    row = lambda c: pl.BlockSpec((t, c), lambda i: (i, 0))
    return _hosted_call(
        body, xchg, name="in_proj_bwd", grid=(s // t,),
        in_specs=[row(w) for w in widths] + [row(D), row(D), _layer((1, D), l), _const((DIN, D))],
        out_specs=[row(D), row(DIN), pl.BlockSpec((1, D), lambda i: (0, 0))],
        out_shape=[SDS((s, D), F32), SDS((s, DIN), BF16), SDS((1, D), F32)], args=[*parts, x, dxm, g1, win_t])


def _wgrad(a, b, tag, xchg=None):
    s, m = a.shape
    mb = 512

    def body(a_ref, b_ref, o_ref):
        o_ref[...] = _tn(a_ref[...], b_ref[...]).astype(BF16)

    (out,), got = _hosted_call(
        body, xchg, name=f"wgrad_{tag}", grid=(m // mb,),
        in_specs=[pl.BlockSpec((s, mb), lambda mi: (0, mi)), _const((s, D))],
        out_specs=[pl.BlockSpec((mb, D), lambda mi: (mi, 0))],
        out_shape=[SDS((m, D), BF16)], args=[a, b])
    return out, got


def _bias_tables(gvec):
    def body(g_ref, o_ref):
        qc = lax.broadcasted_iota(jnp.int32, (TQ, WIN), 0) // CHUNK
        kc = lax.broadcasted_iota(jnp.int32, (TQ, WIN), 1) // CHUNK
        for var in range(NVAR):
            vec = jnp.broadcast_to(g_ref[:, var * TQ:var * TQ + NTOE], (TQ, NTOE))
            toe = pltpu.roll(vec, NTOE - TQ + 1, 1, stride=1, stride_axis=0)[:, :WIN]
            rel = (BAND - var * TQ) // CHUNK + qc - kc
            o_ref[var] = jnp.where((rel >= 0) & (rel <= N_PREV), toe, NEG)

    return pl.pallas_call(
        body, name="bias_tables", grid=(L, NH),
        in_specs=[pl.BlockSpec((None, None, 1, NG), lambda l, h: (l, h, 0, 0))],
        out_specs=pl.BlockSpec((None, NVAR, None, TQ, WIN), lambda l, h: (l, 0, h, 0, 0)),
        out_shape=SDS((L, NVAR, NH, TQ, WIN), F32),
        compiler_params=_cp(),
    )(gvec)


def _bias_tables_grad(dbias, l):
    nb = NTOE // LANE
    wb = WIN // LANE

    def body(d_ref, o_ref):
        ii = lax.broadcasted_iota(jnp.int32, (LANE, LANE), 0)
        jj = lax.broadcasted_iota(jnp.int32, (LANE, LANE), 1)
        flip = jnp.where(ii + jj == LANE - 1, 1.0, 0.0).astype(BF16)
        o_ref[...] = jnp.zeros_like(o_ref)
        for var in range(NVAR):
            blocks = []
            for b in range(nb):
                src = nb - 1 - b
                if src >= wb:
                    blocks.append(jnp.zeros((TQ, LANE), F32))
                    continue
                xv = d_ref[var, :, src * LANE:(src + 1) * LANE]
                hi = xv.astype(BF16)
                lo = (xv - hi.astype(F32)).astype(BF16)
                blocks.append(_nn(hi, flip) + _nn(lo, flip))
            rev = jnp.concatenate(blocks, axis=1)
            skew = pltpu.roll(rev, NTOE - TQ + 1, 1, stride=1, stride_axis=0)
            off = NG - NTOE - var * TQ
            o_ref[:, off:off + NTOE] += jnp.sum(skew, axis=0, keepdims=True)

    return pl.pallas_call(
        body, name=f"bias_tables_grad_l{l}", grid=(NH,),
        in_specs=[pl.BlockSpec((NVAR, None, TQ, WIN), lambda h: (0, h, 0, 0))],
        out_specs=pl.BlockSpec((None, 1, NG), lambda h: (h, 0, 0)),
        out_shape=SDS((NH, 1, NG), F32),
        compiler_params=_cp(),
    )(dbias)


_SIBLING = (0, 0, 1)
_CHIPS = [(1, 0, 0), (0, 1, 0), (1, 1, 0)]
_MASKS = [_SIBLING] + _CHIPS + [(1, 0, 1), (0, 1, 1), (1, 1, 1)]


def _position():
    return lax.axis_index("x"), lax.axis_index("y"), lax.axis_index("c")


def _peer(pos, mask):
    return tuple(1 - a if f else a for a, f in zip(pos, mask))


def _index(pos):
    return 4 * pos[0] + 2 * pos[1] + pos[2]


def _exchange_phases(items, src, dst, sems):
    send_sems, recv_sems, local_sems = sems
    me = _position()
    sib = _peer(me, _SIBLING)

    def remote(s_ref, d_ref, pi, n, to):
        return pltpu.make_async_remote_copy(
            src_ref=s_ref, dst_ref=d_ref, send_sem=send_sems.at[pi, n], recv_sem=recv_sems.at[pi, n],
            device_id=to, device_id_type=MESH_ID)

    def parts(n):
        it = items[n]
        r = SHARD_ROWS[it[1]]
        block = lambda ref, pos: ref.at[pl.ds(_index(pos) * r, r), :]
        if it[0] == "gather":
            own = src[n].at[it[2]]
            local = pltpu.make_async_copy(own, block(dst[n], me), local_sems.at[n])
            sends = [remote(own, block(dst[n], me), pi, n, _peer(me, m)) for pi, m in enumerate([_SIBLING] + _CHIPS)]
            hops = [(remote(block(dst[n], _peer(me, m)), block(dst[n], _peer(me, m)), 1 + j, n, _peer(me, m)),
                     remote(block(dst[n], _peer(me, m)), block(dst[n], _peer(me, m)), 4 + j, n, sib))
                    for j, m in enumerate(_CHIPS)]
            lands = [remote(own, block(dst[n], sib), 0, n, sib)]
            lands += [remote(own, block(dst[n], _peer(sib, m)), 4 + j, n, sib) for j, m in enumerate(_CHIPS)]
        else:
            local = pltpu.make_async_copy(block(src[n], me), dst[n].at[_index(me)], local_sems.at[n])
            sends = [remote(block(src[n], _peer(me, m)), dst[n].at[_index(me)], pi, n, _peer(me, m))
                     for pi, m in enumerate(_MASKS)]
            hops = []
            lands = [remote(block(src[n], me), dst[n].at[_index(_peer(me, m))], pi, n, _peer(me, m))
                     for pi, m in enumerate(_MASKS)]
        return local, sends, hops, lands

    def start():
        for n in range(len(items)):
            local, sends, _, _ = parts(n)
            local.start()
            for cp in sends:
                cp.start()

    def relay():
        for n in range(len(items)):
            for arrived, onward in parts(n)[2]:
                arrived.wait_recv()
                onward.start()

    def finish():
        for n in range(len(items)):
            local, sends, hops, lands = parts(n)
            for cp in lands:
                cp.wait_recv()
            for cp in sends + [onward for _, onward in hops]:
                cp.wait_send()
            local.wait()

    return start, relay, finish


def _hosted_call(body, xchg, *, name, grid, in_specs, out_specs, out_shape, args, scratch_shapes=(), relay_at=0.7):
    if not xchg:
        outs = pl.pallas_call(
            body, name=name, grid=grid, in_specs=list(in_specs), out_specs=list(out_specs),
            out_shape=list(out_shape), scratch_shapes=list(scratch_shapes), compiler_params=_cp())(*args)
        return outs, []
    items = [it for it, _ in xchg]
    n_in, n_out, n_scr, nit = len(args), len(out_shape), len(scratch_shapes), len(items)
    hbm = pl.BlockSpec(memory_space=pl.ANY)
    steps = int(np.prod(grid))
    relay_step = min(int(relay_at * steps), steps - 1)

    def dst_shape(it):
        r = SHARD_ROWS[it[1]]
        return SDS((NDEV * r, D) if it[0] == "gather" else (NDEV, r, D), BF16)

    def wrapped(*refs):
        ins = refs[:n_in]
        src = refs[n_in:n_in + nit]
        outs = refs[n_in + nit:n_in + nit + n_out]
        dst = refs[n_in + nit + n_out:n_in + 2 * nit + n_out]
        scratch = refs[n_in + 2 * nit + n_out:n_in + 2 * nit + n_out + n_scr]
        start, relay, finish = _exchange_phases(items, src, dst, refs[n_in + 2 * nit + n_out + n_scr:])
        step = 0
        for d, g in enumerate(grid):
            step = step * g + pl.program_id(d)
        pl.when(step == 0)(start)
        body(*ins, *outs, *scratch)
        pl.when(step == relay_step)(relay)
        pl.when(step == steps - 1)(finish)

    npeer = len(_MASKS)
    res = pl.pallas_call(
        wrapped, name=name, grid=grid,
        in_specs=list(in_specs) + [hbm] * nit,
        out_specs=list(out_specs) + [hbm] * nit,
        out_shape=list(out_shape) + [dst_shape(it) for it in items],
        scratch_shapes=list(scratch_shapes) + [
            pltpu.SemaphoreType.DMA((npeer, nit)), pltpu.SemaphoreType.DMA((npeer, nit)), pltpu.SemaphoreType.DMA((nit,))],
        compiler_params=_cp(),
    )(*args, *[a for _, a in xchg])
    return list(res[:n_out]), list(res[n_out:])


def _gather_first(xchg):
    def body(o_ref):
        o_ref[...] = jnp.zeros_like(o_ref)

    return _hosted_call(
        body, xchg, name="gather_first", grid=(1,),
        in_specs=[], out_specs=[pl.BlockSpec((8, LANE), lambda i: (0, 0))], out_shape=[SDS((8, LANE), F32)], args=[])[1]


def _sum_slots(slots, xchg=None):
    _, r, _ = slots[0].shape
    rt = 64

    def body(*refs):
        o_ref = refs[L]
        for l in range(L):
            acc = refs[l][0].astype(F32)
            for d in range(1, NDEV):
                acc = acc + refs[l][d].astype(F32)
            o_ref[l] = acc

    (out,), got = _hosted_call(
        body, xchg, name=f"sum_slots_r{r}" + ("_x" if xchg else ""), grid=(r // rt,),
        in_specs=[pl.BlockSpec((NDEV, rt, D), lambda i: (0, i, 0))] * L,
        out_specs=[pl.BlockSpec((L, rt, D), lambda i: (0, i, 0))],
        out_shape=[SDS((L, r, D), F32)], args=list(slots))
    return out, got


def _exchange_small(v, reduce):
    rows = v.shape[0]

    def body(v_ref, o_ref, *scratch):
        if reduce:
            slots, send_sems, recv_sems = scratch
        else:
            slots = o_ref
            send_sems, recv_sems = scratch
        me = _position()
        slots[_index(me)] = v_ref[...]
        sends = []
        for pi, mask in enumerate(_MASKS):
            cp = pltpu.make_async_remote_copy(
                src_ref=v_ref, dst_ref=slots.at[_index(me)], send_sem=send_sems.at[pi], recv_sem=recv_sems.at[pi],
                device_id=_peer(me, mask), device_id_type=MESH_ID)
            cp.start()
            sends.append(cp)
        for pi, mask in enumerate(_MASKS):
            peer = _peer(me, mask)
            pltpu.make_async_remote_copy(
                src_ref=v_ref, dst_ref=slots.at[_index(peer)], send_sem=send_sems.at[pi], recv_sem=recv_sems.at[pi],
                device_id=peer, device_id_type=MESH_ID).wait_recv()
        for cp in sends:
            cp.wait_send()
        if reduce:
            acc = slots[0]
            for d in range(1, NDEV):
                acc = acc + slots[d]
            o_ref[...] = acc

    vm = pl.BlockSpec(memory_space=pltpu.VMEM)
    sems = [pltpu.SemaphoreType.DMA((len(_MASKS),)), pltpu.SemaphoreType.DMA((len(_MASKS),))]
    return pl.pallas_call(
        body, name="reduce_small" if reduce else "gather_small",
        in_specs=[vm], out_specs=vm,
        out_shape=SDS((rows, LANE) if reduce else (NDEV, rows, LANE), F32),
        scratch_shapes=([pltpu.VMEM((NDEV, rows, LANE), F32)] if reduce else []) + sems,
        compiler_params=_cp(),
    )(v)


def _adamw_update(w_ref, g_ref, m_ref, v_ref, d_ref, nm_ref, nv_ref):
    gv = g_ref[...]
    mn = B1 * m_ref[...] + (1.0 - B1) * gv
    vn = B2 * v_ref[...] + (1.0 - B2) * jnp.square(gv)
    nm_ref[...] = mn
    nv_ref[...] = vn
    m_hat = mn / (1.0 - B1 ** STEP)
    v_hat = vn / (1.0 - B2 ** STEP)
    d_ref[...] = -LR * (m_hat / (jnp.sqrt(v_hat) + AEPS) + WD * w_ref[...])


def _adamw_small(ws, gs, ms, vs):
    n = len(ws)

    def body(*refs):
        for i in range(n):
            _adamw_update(*[refs[j * n + i] for j in range(7)])

    vm = pl.BlockSpec(memory_space=pltpu.VMEM)
    res = pl.pallas_call(
        body, name="adamw_small", in_specs=[vm] * (4 * n), out_specs=[vm] * (3 * n),
        out_shape=[SDS(w.shape, F32) for _ in range(3) for w in ws],
        compiler_params=_cp(),
    )(*ws, *gs, *ms, *vs)
    return res[:n], res[n:2 * n], res[2 * n:]


def _adamw(w, g, m, v):
    rows, cols = w.shape
    t = rows
    for cand in (512, 256, 128, 64, 32, 16, 8):
        if rows % cand == 0:
            t = cand
            break

    def body(*refs):
        _adamw_update(*refs)

    blk = pl.BlockSpec((t, cols), lambda i: (i, 0))
    return pl.pallas_call(
        body, name=f"adamw_{rows}x{cols}", grid=(rows // t,),
        in_specs=[blk] * 4, out_specs=[blk] * 3,
        out_shape=[SDS((rows, cols), F32)] * 3,
        compiler_params=_cp(),
    )(w, g, m, v)


_DIST0 = BAND + TQ - 1
_N_FAR = _DIST0 - REL_CLIP + 1
_N_NEAR = NG - _N_FAR - (2 * REL_CLIP - 1)


def _bias_vector(rel_bias):
    far = jnp.broadcast_to(rel_bias[..., -1:], (L, NH, _N_FAR))
    near = jnp.broadcast_to(rel_bias[..., :1], (L, NH, _N_NEAR))
    return jnp.concatenate([far, lax.rev(rel_bias[..., 1:-1], (2,)), near], axis=2)[:, :, None, :]


def _bias_vector_grad(dgr):
    first = jnp.sum(dgr[..., :_N_NEAR], axis=-1, keepdims=True)
    last = jnp.sum(dgr[..., NG - _N_FAR:], axis=-1, keepdims=True)
    return jnp.concatenate([first, dgr[..., _N_NEAR:NG - _N_FAR], last], axis=-1)


def _pool_blockdiag(pool_w):
    eye = jnp.eye(2, dtype=F32)
    pw = pool_w.reshape(L, 2, 2, HD, HD)
    return jnp.einsum("ljaik,ab->ljaibk", pw, eye).reshape(L, 2, LANE, LANE)


def _pool_blockdiag_grad(dwbd):
    d = dwbd.reshape(L, 2, 2, HD, 2, HD)
    return jnp.stack([d[:, :, 0, :, 0, :], d[:, :, 1, :, 1, :]], axis=2).reshape(L, 4, HD, HD)


def _pack(arrays, rows):
    flat = jnp.concatenate([a.reshape(-1).astype(F32) for a in arrays])
    return jnp.pad(flat, (0, rows * LANE - flat.shape[0])).reshape(rows, LANE)


def _unpack(packed, shapes):
    flat = packed.reshape(-1)
    out, o = [], 0
    for shp in shapes:
        n = int(np.prod(shp))
        out.append(flat[o:o + n].reshape(shp))
        o += n
    return out


def _rows_for(shapes):
    n = sum(int(np.prod(s)) for s in shapes)
    return -(-n // (8 * LANE)) * 8


def _grads(x, target, small_w, shards):
    g1, qg, kg, rb, cw, pw, ps, g2 = small_w
    g1 = g1.reshape(L, 1, D)
    g2 = g2.reshape(L, 1, D)
    qg2 = jnp.tile(qg, (1, 2)).reshape(L, 1, LANE)
    kg2 = jnp.tile(kg, (1, 2)).reshape(L, 1, LANE)
    ps3 = ps.reshape(L, 1, PWD)
    bias = _bias_tables(_bias_vector(rb))
    wbd = _pool_blockdiag(pw).astype(BF16)

    def gather(*kl):
        return [(("gather", k, l), shards[k]) for k, l in kl if l < L]

    full = {}

    def arrived(got, *kl):
        full.update(zip([x for x in kl if x[1] < L], got))

    arrived(_gather_first(gather((0, 0), (1, 0))), (0, 0), (1, 0))
    saved = []
    h = x
    for l in range(L):
        kl = ((2, 0), (1, 1)) if l == 0 else ((1, l + 1),)
        (p, h_b, qkv), got = _in_proj(h, g1, full[0, l], qg2, kg2, l, gather(*kl))
        arrived(got, *kl)
        kl = ((3, 0), (3, 1)) if l == 0 else ((3, l + 1),)
        (mix, lse, o32), got = _attn_fwd(qkv, bias, l, gather(*kl))
        arrived(got, *kl)
        mix = _conv_pool_fwd(p, mix, cw, wbd, ps3, l)
        kl = ((2, l + 1), (0, l + 1))
        (xm, a, xo), got = _mlp_fwd(h, mix, full[1, l], g2, full[2, l], full[3, l], l, gather(*kl))
        arrived(got, *kl)
        saved.append((h, h_b, p, qkv, mix, lse, o32, xm, a))
        h = xo
    dx, sq = _loss_grad(h, target)

    grads = {}
    slots = {}

    def scatter(*kl):
        return [(("scatter", k), grads[k, l]) for k, l in kl if l < L]

    def left(got, *kl):
        slots.update(zip([x for x in kl if x[1] < L], got))

    per_layer = [None] * L
    for l in reversed(range(L)):
        x_in, h_b, p, qkv, mix, lse, o32, xm, a = saved[l]
        (dxm, dmix, f_b, da_b, h2_b, dxo_b, dxm_b, dg2), got = _mlp_bwd(
            dx, a, xm, g2, full[2, l], full[3, l], full[1, l], l, scatter((3, l + 1)))
        left(got, (3, l + 1))
        grads[1, l], _ = _wgrad(mix, dxm_b, f"w_out_l{l}")
        grads[2, l], got = _wgrad(da_b, h2_b, f"w_mlp1_l{l}", scatter((1, l)))
        left(got, (1, l))
        grads[3, l], got = _wgrad(f_b, dxo_b, f"w_mlp2_l{l}", scatter((0, l + 1)))
        left(got, (0, l + 1))
        (dq, dk, dv, dbias, dqg, dkg), got = _attn_bwd(
            qkv, p, lse, o32, dmix, qg2, kg2, bias, l, scatter((2, l)))
        left(got, (2, l))
        dgb, dgc, dhin, du, dcw, dwbd, dps = _conv_pool_bwd(p, dmix, cw, wbd, ps3, l)
        (dx, dp_b, dg1), _ = _in_proj_bwd([dq, dk, dv, dgb, dgc, dhin, du], x_in, dxm, g1, full[0, l], l)
        grads[0, l], got = _wgrad(dp_b, h_b, f"w_in_l{l}", scatter((3, 0)) if l == 0 else None)
        left(got, (3, 0))
        per_layer[l] = (dg1, dg2, dqg, dkg, _bias_tables_grad(dbias, l), dcw, dwbd, dps)
    g_w1_t, got = _sum_slots([slots[2, l] for l in range(L)], scatter((0, 0)))
    left(got, (0, 0))
    sums = [_sum_slots([slots[k, l] for l in range(L)])[0] for k in (0, 1)] + [
        g_w1_t, _sum_slots([slots[3, l] for l in range(L)])[0]]

    st = [jnp.stack([per_layer[l][k] for l in range(L)]) for k in range(8)]
    small = dict(
        g1=st[0].reshape(L, D), g2=st[1].reshape(L, D),
        qg=st[2].reshape(L, NH, HD).sum(1), kg=st[3].reshape(L, NH, HD).sum(1),
        rb=_bias_vector_grad(st[4].reshape(L, NH, NG)), cw=st[5], pw=_pool_blockdiag_grad(st[6]),
        ps=st[7].reshape(L, PWD))
    return sq, dx, sums, small


def kernel(x, norm1_g, w_in, q_norm_g, k_norm_g, rel_bias, conv_w, pool_w, pool_scale, w_out, norm2_g, w_mlp1, w_mlp2, loss_target, m_norm1_g, m_w_in, m_q_norm_g, m_k_norm_g, m_rel_bias, m_conv_w, m_pool_w, m_pool_scale, m_w_out, m_norm2_g, m_w_mlp1, m_w_mlp2, v_norm1_g, v_w_in, v_q_norm_g, v_k_norm_g, v_rel_bias, v_conv_w, v_pool_w, v_pool_scale, v_w_out, v_norm2_g, v_w_mlp1, v_w_mlp2):
    me = _index(_position())
    cshard = CW // NDEV

    shards = [jnp.swapaxes(w_in, 1, 2).astype(BF16), w_out.astype(BF16),
              jnp.swapaxes(w_mlp1, 1, 2).astype(BF16), w_mlp2.astype(BF16)]
    cw_all = _exchange_small(_pack([conv_w], 8), reduce=False)
    cw_full = jnp.concatenate(
        [cw_all[d].reshape(-1)[:L * 3 * cshard].reshape(L, 3, cshard) for d in range(NDEV)], axis=2)

    small_w = (norm1_g, q_norm_g, k_norm_g, rel_bias, cw_full, pool_w, pool_scale, norm2_g)
    sq, grad_x, (g_win_t, g_wout, g_w1_t, g_w2), small = _grads(x[0], loss_target[0], small_w, shards)
    g_w_in = jnp.swapaxes(g_win_t, 1, 2)
    g_w_mlp1 = jnp.swapaxes(g_w1_t, 1, 2)

    names = ("g1", "qg", "kg", "rb", "cw", "pw", "ps", "g2")
    gshapes = [(L, D), (L, HD), (L, HD), (L, NH, 2 * REL_CLIP + 1), (L, 3, CW), (L, 4, HD, HD), (L, PWD), (L, D)]
    garrs = [small[n] for n in names]
    rows = _rows_for(gshapes + [(1,)])
    total = _exchange_small(_pack(garrs + [sq[0, :1]], rows), reduce=True)
    g_g1, g_qg, g_kg, g_rb, g_cw_full, g_pw, g_ps, g_g2, sq_sum = _unpack(total, gshapes + [(1,)])
    loss = (0.5 / D) * sq_sum[0]
    g_cw = lax.dynamic_slice_in_dim(g_cw_full, me * cshard, cshard, axis=2)

    def big(w, g, m, v):
        shp = w.shape
        r = lambda a: a.reshape(-1, shp[-1])
        return [o.reshape(shp) for o in _adamw(r(w), r(g), r(m), r(v))]

    up_in = big(w_in, g_w_in, m_w_in, v_w_in)
    up_out = big(w_out, g_wout, m_w_out, v_w_out)
    up_1 = big(w_mlp1, g_w_mlp1, m_w_mlp1, v_w_mlp1)
    up_2 = big(w_mlp2, g_w2, m_w_mlp2, v_w_mlp2)

    sw = [norm1_g, q_norm_g, k_norm_g, rel_bias, conv_w, pool_w, pool_scale, norm2_g]
    sg = [g_g1, g_qg, g_kg, g_rb, g_cw, g_pw, g_ps, g_g2]
    sm = [m_norm1_g, m_q_norm_g, m_k_norm_g, m_rel_bias, m_conv_w, m_pool_w, m_pool_scale, m_norm2_g]
    sv = [v_norm1_g, v_q_norm_g, v_k_norm_g, v_rel_bias, v_conv_w, v_pool_w, v_pool_scale, v_norm2_g]
    s_delta, s_m, s_v = _adamw_small(sw, sg, sm, sv)

    def order(small_list, in_, out_, m1, m2):
        g1_, qg_, kg_, rb_, cw_, pw_, ps_, g2_ = small_list
        return [g1_, in_, qg_, kg_, rb_, cw_, pw_, ps_, out_, g2_, m1, m2]

    grads = order(sg, g_w_in, g_wout, g_w_mlp1, g_w2)
    deltas = order(s_delta, up_in[0], up_out[0], up_1[0], up_2[0])
    new_m = order(s_m, up_in[1], up_out[1], up_1[1], up_2[1])
    new_v = order(s_v, up_in[2], up_out[2], up_1[2], up_2[2])
    return (loss, grad_x[None], *grads, *deltas, *new_m, *new_v)
```

```python
import numpy as np
import jax
import jax.numpy as jnp
from jax import lax
from jax.experimental import pallas as pl
from jax.experimental.pallas import tpu as pltpu

F32 = jnp.float32
BF16 = jnp.bfloat16
SDS = jax.ShapeDtypeStruct
MESH_ID = pl.DeviceIdType.MESH

D = 1024
L = 4
CHUNK = 64
N_PREV = 8
HD = 64
NH = 8
AW = 512
CW = 256
PWD = 256
DIN = 2560
DFF = 4096
EPS = 1e-6
NEG = -1e30
REL_CLIP = 128
POOL_WINDOWS = (2, 4, 8, 16)
LR, B1, B2, AEPS, WD, STEP = 0.001, 0.9, 0.999, 1e-08, 0.01, 10

NDEV = 8
LANE = 128
BAND = N_PREV * CHUNK
TQ = 256
WIN = TQ + BAND
NVAR = BAND // TQ + 1
NTOE = -(-(WIN + TQ - 1) // LANE) * LANE
NG = (NVAR - 1) * TQ + NTOE
PAD = 16
RB_NORM = 64
RB_SOFT = 16
VMEM_LIMIT = 56 * 1024 * 1024
SHARD_ROWS = (DIN // NDEV, D // NDEV, DFF // NDEV, DFF // NDEV)


def _cp(**kw):
    return pltpu.CompilerParams(vmem_limit_bytes=VMEM_LIMIT, **kw)


def _nn(a, b):
    return jnp.dot(a, b, preferred_element_type=F32)


def _nt(a, b):
    return lax.dot_general(a, b, (((1,), (1,)), ((), ())), preferred_element_type=F32)


def _tn(a, b):
    return lax.dot_general(a, b, (((0,), (0,)), ((), ())), preferred_element_type=F32)


def _const(shape):
    n = len(shape)
    return pl.BlockSpec(shape, lambda *_: (0,) * n, pipeline_mode=pl.Buffered(1))


def _layer(shape, l):
    n = len(shape)
    return pl.BlockSpec((None,) + tuple(shape), lambda *_: (l,) + (0,) * n, pipeline_mode=pl.Buffered(1))


def _lo_mask():
    return lax.broadcasted_iota(jnp.int32, (1, LANE), 1) < HD


def _half_sum(t, lo):
    s_lo = jnp.sum(jnp.where(lo, t, 0.0), axis=-1, keepdims=True)
    s_hi = jnp.sum(jnp.where(lo, 0.0, t), axis=-1, keepdims=True)
    return jnp.where(lo, s_lo, s_hi)


def _head_norm(x, lo):
    r = lax.rsqrt(_half_sum(x * x, lo) * (1.0 / HD) + EPS)
    return x * r, r


def _head_norm_bwd(dy, xn, r, g, lo):
    dxn = dy * g
    mu = _half_sum(dxn * xn, lo) * (1.0 / HD)
    return r * (dxn - xn * mu), dy * xn


def _rms_bwd(dy, x, g):
    r = lax.rsqrt(jnp.mean(x * x, axis=-1, keepdims=True) + EPS)
    xn = x * r
    dxn = dy * g
    mu = jnp.mean(dxn * xn, axis=-1, keepdims=True)
    return r * (dxn - xn * mu), dy * xn


def _in_proj(x, g1, win_t, qg2, kg2, l, xchg=None):
    s = x.shape[0]
    t = min(512, s)
    nblk = AW // LANE

    def body(x_ref, g_ref, w_ref, qg_ref, kg_ref, p_ref, h_ref, qkv_ref):
        xv = x_ref[...]
        r = lax.rsqrt(jnp.mean(xv * xv, axis=-1, keepdims=True) + EPS)
        h = (xv * r * g_ref[...]).astype(BF16)
        h_ref[...] = h
        p_ref[...] = _nt(h, w_ref[...])
        lo = _lo_mask()
        gains = (qg_ref[...] * (HD ** -0.5), kg_ref[...])
        for r0 in range(0, t, RB_NORM):
            rows = pl.ds(r0, RB_NORM)
            for c in range(3 * nblk):
                cols = pl.ds(c * LANE, LANE)
                v = p_ref[rows, cols]
                if c < 2 * nblk:
                    v = _head_norm(v, lo)[0] * gains[c // nblk]
                qkv_ref[rows, cols] = v.astype(BF16)

    row = lambda c: pl.BlockSpec((t, c), lambda i: (i, 0))
    return _hosted_call(
        body, xchg, name="in_proj", grid=(s // t,),
        in_specs=[row(D), _layer((1, D), l), _const((DIN, D)), _layer((1, LANE), l), _layer((1, LANE), l)],
        out_specs=[row(DIN), row(D), row(3 * AW)],
        out_shape=[SDS((s, DIN), F32), SDS((s, D), BF16), SDS((s, 3 * AW), BF16)], args=[x, g1, win_t, qg2, kg2])


def _bias_spec():
    return pl.BlockSpec((None, 2, TQ, WIN), lambda j, i: (jnp.maximum(NVAR - 1 - i, 0), j, 0, 0))


def _bias_layer_spec(l):
    return pl.BlockSpec((None, None, 2, TQ, WIN), lambda j, i: (l, jnp.maximum(NVAR - 1 - i, 0), j, 0, 0))


def _attn_fwd(qkv, bias, l, xchg=None):
    s = qkv.shape[0]
    nq = s // TQ

    def body(q_ref, k_ref, v_ref, b_ref, o_ref, lse_ref, o32_ref, s_ref, p_ref, m_ref, den_ref, o0_ref):
        i = pl.program_id(1)
        ks = pl.multiple_of(jnp.maximum(i * TQ - BAND, 0), TQ)
        lo = _lo_mask()
        q = q_ref[...]
        kwin = k_ref[pl.ds(ks, WIN), :]
        vwin = v_ref[pl.ds(ks, WIN), :]
        for half in range(2):
            m_ = lo if half == 0 else jnp.logical_not(lo)
            s_ref[half] = _nt(jnp.where(m_, q, jnp.zeros_like(q)), kwin)
            for r0 in range(0, TQ, RB_SOFT):
                rows = pl.ds(r0, RB_SOFT)
                mx = jnp.max(s_ref[half, rows, :] + b_ref[half, rows, :], axis=-1, keepdims=True)
                m_ref[rows, :] = jnp.broadcast_to(mx, (RB_SOFT, LANE))
            for r0 in range(0, TQ, RB_SOFT):
                rows = pl.ds(r0, RB_SOFT)
                mx = m_ref[rows, 0:1]
                e = jnp.exp(s_ref[half, rows, :] + b_ref[half, rows, :] - mx)
                p_ref[half, rows, :] = e.astype(BF16)
                den = jnp.sum(e, axis=-1, keepdims=True)
                den_ref[rows, :] = jnp.broadcast_to(den, (RB_SOFT, LANE))
                lse = mx + jnp.log(den)
                if half == 0:
                    lse_ref[rows, :] = jnp.broadcast_to(lse, (RB_SOFT, LANE))
                else:
                    lse_ref[rows, :] = jnp.where(lo, lse_ref[rows, :], lse)
            o = _nn(p_ref[half], vwin) * (1.0 / den_ref[...])
            if half == 0:
                o0_ref[...] = o
            else:
                o = jnp.where(lo, o0_ref[...], o)
                o32_ref[...] = o
                o_ref[...] = o.astype(BF16)

    tile = pl.BlockSpec((TQ, LANE), lambda j, i: (i, j))
    stat = pltpu.VMEM((TQ, LANE), F32)
    return _hosted_call(
        body, xchg, name="attn_fwd", grid=(NH // 2, nq),
        in_specs=[
            tile,
            pl.BlockSpec((s, LANE), lambda j, i: (0, AW // LANE + j)),
            pl.BlockSpec((s, LANE), lambda j, i: (0, 2 * AW // LANE + j)),
            _bias_layer_spec(l),
        ],
        out_specs=[tile, tile, tile],
        out_shape=[SDS((s, D), BF16), SDS((s, AW), F32), SDS((s, AW), F32)], args=[qkv, qkv, qkv, bias],
        scratch_shapes=[pltpu.VMEM((2, TQ, WIN), F32), pltpu.VMEM((2, TQ, WIN), BF16), stat, stat, stat])


_C0 = 3 * AW // LANE


def _cp_in_specs(s, l):
    blk = lambda f: pl.BlockSpec((s, LANE), f)
    return [
        blk(lambda g: (0, _C0 + jnp.minimum(g, 1))),
        blk(lambda g: (0, _C0 + 2 + jnp.minimum(g, 1))),
        blk(lambda g: (0, _C0 + 4 + jnp.minimum(g, 1))),
        blk(lambda g: (0, _C0 + 6 + jnp.maximum(g - 2, 0))),
        pl.BlockSpec((None, 3, LANE), lambda g: (l, 0, jnp.minimum(g, 1))),
        pl.BlockSpec((None, None, LANE, LANE), lambda g: (l, jnp.maximum(g - 2, 0), 0, 0)),
        pl.BlockSpec((None, 1, LANE), lambda g: (l, 0, jnp.maximum(g - 2, 0))),
    ]


def _pool_window_sums(u_ref, buf_a, buf_b, jj, s, rt):
    nrt = s // rt
    for t in range(nrt):
        buf_a[pl.ds(PAD + t * rt, rt), :] = u_ref[pl.ds(t * rt, rt), :]

    def stage(src, dst, sh):
        for t in range(nrt):
            r0 = PAD + t * rt
            dst[pl.ds(r0, rt), :] = src[pl.ds(r0, rt), :] + src[pl.ds(r0 - sh, rt), :]

    stage(buf_a, buf_b, 1)
    stage(buf_b, buf_a, 2)

    @pl.when(jj == 1)
    def _():
        stage(buf_a, buf_b, 4)
        stage(buf_b, buf_a, 8)


def _pool_counts(jj, lo, r0, rt):
    w = jnp.where(lo, jnp.where(jj == 0, 2.0, 8.0), jnp.where(jj == 0, 4.0, 16.0))
    pos1 = (lax.broadcasted_iota(jnp.int32, (rt, LANE), 0) + (r0 + 1)).astype(F32)
    return jnp.minimum(pos1, w)


def _conv_pool_fwd(p, mix, conv_w, wbd, pscale, l):
    s = p.shape[0]
    rt = min(256, s)
    nrt = s // rt

    def body(gb_ref, gc_ref, hin_ref, u_ref, cw_ref, wbd_ref, ps_ref, mix_in, o_ref, buf_a, buf_b):
        del mix_in
        g = pl.program_id(0)
        zpad = jnp.zeros((PAD, LANE), F32)
        buf_a[pl.ds(0, PAD), :] = zpad
        buf_b[pl.ds(0, PAD), :] = zpad

        @pl.when(g < 2)
        def _conv():
            for t in range(nrt):
                buf_a[pl.ds(PAD + t * rt, rt), :] = gc_ref[pl.ds(t * rt, rt), :] * hin_ref[pl.ds(t * rt, rt), :]
            w0, w1, w2 = cw_ref[0:1, :], cw_ref[1:2, :], cw_ref[2:3, :]
            for t in range(nrt):
                r0 = PAD + t * rt
                y = w0 * buf_a[pl.ds(r0 - 2, rt), :] + w1 * buf_a[pl.ds(r0 - 1, rt), :] + w2 * buf_a[pl.ds(r0, rt), :]
                o_ref[pl.ds(t * rt, rt), :] = (gb_ref[pl.ds(t * rt, rt), :] * y).astype(BF16)

        @pl.when(g >= 2)
        def _pool():
            jj = g - 2
            lo = _lo_mask()
            _pool_window_sums(u_ref, buf_a, buf_b, jj, s, rt)
            wb = wbd_ref[...]
            for t in range(nrt):
                r0 = PAD + t * rt
                wsum = jnp.where(lo, buf_b[pl.ds(r0, rt), :], buf_a[pl.ds(r0, rt), :])
                m = wsum / _pool_counts(jj, lo, t * rt, rt) - u_ref[pl.ds(t * rt, rt), :]
                o_ref[pl.ds(t * rt, rt), :] = (_nn(m.astype(BF16), wb) * ps_ref[...]).astype(BF16)

    return pl.pallas_call(
        body, name="conv_pool_fwd", grid=(4,),
        in_specs=_cp_in_specs(s, l) + [pl.BlockSpec(memory_space=pl.ANY)],
        out_specs=pl.BlockSpec((s, LANE), lambda g: (0, AW // LANE + g)),
        out_shape=SDS((s, D), BF16),
        scratch_shapes=[pltpu.VMEM((s + 2 * PAD, LANE), F32), pltpu.VMEM((s + 2 * PAD, LANE), F32)],
        input_output_aliases={7: 0},
        compiler_params=_cp(),
    )(p, p, p, p, conv_w, wbd, pscale, mix)


def _mlp_fwd(x, mix, wout, g2, w1_t, w2, l, xchg=None):
    s = x.shape[0]
    t = min(256, s)

    def body(x_ref, mix_ref, wo_ref, g_ref, w1_ref, w2_ref, xm_ref, a_ref, xo_ref):
        xm = x_ref[...] + _nn(mix_ref[...], wo_ref[...])
        xm_ref[...] = xm
        r = lax.rsqrt(jnp.mean(xm * xm, axis=-1, keepdims=True) + EPS)
        h2 = (xm * r * g_ref[...]).astype(BF16)
        a = _nt(h2, w1_ref[...])
        a_ref[...] = a.astype(BF16)
        f = jnp.square(jnp.maximum(a, 0.0)).astype(BF16)
        xo_ref[...] = xm + _nn(f, w2_ref[...])

    row = lambda c: pl.BlockSpec((t, c), lambda i: (i, 0))
    return _hosted_call(
        body, xchg, name="mlp_fwd", grid=(s // t,),
        in_specs=[row(D), row(D), _const((D, D)), _layer((1, D), l), _const((DFF, D)), _const((DFF, D))],
        out_specs=[row(D), row(DFF), row(D)],
        out_shape=[SDS((s, D), F32), SDS((s, DFF), BF16), SDS((s, D), F32)], args=[x, mix, wout, g2, w1_t, w2])


def _loss_grad(y, target):
    s = y.shape[0]
    t = min(512, s)

    def body(y_ref, t_ref, dy_ref, acc_ref):
        @pl.when(pl.program_id(0) == 0)
        def _():
            acc_ref[...] = jnp.zeros_like(acc_ref)
        e = y_ref[...] - t_ref[...]
        dy_ref[...] = e * (1.0 / D)
        acc_ref[...] += jnp.sum(e * e)

    row = pl.BlockSpec((t, D), lambda i: (i, 0))
    return pl.pallas_call(
        body, name="loss_grad", grid=(s // t,),
        in_specs=[row, row],
        out_specs=[row, pl.BlockSpec((8, LANE), lambda i: (0, 0))],
        out_shape=[SDS((s, D), F32), SDS((8, LANE), F32)],
        compiler_params=_cp(),
    )(y, target)


def _mlp_bwd(dxo, a, xm, g2, w1_t, w2, wout, l, xchg=None):
    s = dxo.shape[0]
    t = min(256, s)

    def body(dxo_ref, a_ref, xm_ref, g_ref, w1_ref, w2_ref, wo_ref,
             dxm_ref, dmix_ref, f_ref, da_ref, h2_ref, dxob_ref, dxmb_ref, dg_ref):
        @pl.when(pl.program_id(0) == 0)
        def _():
            dg_ref[...] = jnp.zeros_like(dg_ref)
        dxo = dxo_ref[...]
        dxob = dxo.astype(BF16)
        dxob_ref[...] = dxob
        ra = jnp.maximum(a_ref[...].astype(F32), 0.0)
        f_ref[...] = jnp.square(ra).astype(BF16)
        dab = (_nt(dxob, w2_ref[...]) * (2.0 * ra)).astype(BF16)
        da_ref[...] = dab
        dh2 = _nn(dab, w1_ref[...])
        xm = xm_ref[...]
        g = g_ref[...]
        r = lax.rsqrt(jnp.mean(xm * xm, axis=-1, keepdims=True) + EPS)
        h2_ref[...] = (xm * r * g).astype(BF16)
        dx_n, dgr = _rms_bwd(dh2, xm, g)
        dg_ref[...] += jnp.sum(dgr, axis=0, keepdims=True)
        dxm = dxo + dx_n
        dxm_ref[...] = dxm
        dxmb = dxm.astype(BF16)
        dxmb_ref[...] = dxmb
        dmix_ref[...] = _nt(dxmb, wo_ref[...])

    row = lambda c: pl.BlockSpec((t, c), lambda i: (i, 0))
    return _hosted_call(
        body, xchg, name="mlp_bwd", grid=(s // t,),
        in_specs=[row(D), row(DFF), row(D), _layer((1, D), l), _const((DFF, D)), _const((DFF, D)), _const((D, D))],
        out_specs=[row(D), row(D), row(DFF), row(DFF), row(D), row(D), row(D), pl.BlockSpec((1, D), lambda i: (0, 0))],
        out_shape=[SDS((s, D), F32), SDS((s, D), F32), SDS((s, DFF), BF16), SDS((s, DFF), BF16),
                   SDS((s, D), BF16), SDS((s, D), BF16), SDS((s, D), BF16), SDS((1, D), F32)],
        args=[dxo, a, xm, g2, w1_t, w2, wout])


def _attn_bwd(qkv, p, lse, o32, dmix, qg2, kg2, bias, l, xchg=None):
    s = p.shape[0]
    nq = s // TQ
    scale = HD ** -0.5
    rt = min(512, s)

    def body(qs_ref, kb_ref, vb_ref, q_ref, k_ref, qg_ref, kg_ref, b_ref, lse_ref, o_ref, do_ref,
             dq_ref, dk_ref, dv_ref, db_ref, dqg_ref, dkg_ref,
             dk_acc, dv_acc, s_ref, dp_ref, ds_ref, pb_ref, dqn_ref, dl_ref):
        i = pl.program_id(1)
        ks = pl.multiple_of(jnp.maximum(i * TQ - BAND, 0), TQ)
        lo = _lo_mask()

        @pl.when(i == 0)
        def _():
            dk_acc[...] = jnp.zeros_like(dk_acc)
            dv_acc[...] = jnp.zeros_like(dv_acc)
            dqg_ref[...] = jnp.zeros_like(dqg_ref)
            dkg_ref[...] = jnp.zeros_like(dkg_ref)

        @pl.when(i < NVAR)
        def _():
            db_ref[...] = jnp.zeros_like(db_ref)

        qs = qs_ref[...]
        kwin = kb_ref[pl.ds(ks, WIN), :]
        vwin = vb_ref[pl.ds(ks, WIN), :]
        do = do_ref[...]
        dob = do.astype(BF16)
        dl_ref[...] = _half_sum(do * o_ref[...], lo)
        for half in range(2):
            m_ = lo if half == 0 else jnp.logical_not(lo)
            qa = jnp.where(m_, qs, jnp.zeros_like(qs))
            doa = jnp.where(m_, dob, jnp.zeros_like(dob))
            s_ref[half] = _nt(qa, kwin)
            dp_ref[half] = _nt(doa, vwin)
            for r0 in range(0, TQ, RB_SOFT):
                rows = pl.ds(r0, RB_SOFT)
                lse_h = lse_ref[rows, half * HD:half * HD + 1]
                pm = jnp.exp(s_ref[half, rows, :] + b_ref[half, rows, :] - lse_h)
                ds = pm * (dp_ref[half, rows, :] - dl_ref[rows, half * HD:half * HD + 1])
                db_ref[half, rows, :] += ds
                ds_ref[half, rows, :] = ds.astype(BF16)
                pb_ref[half, rows, :] = pm.astype(BF16)
            dsb = ds_ref[half]
            dq_h = _nn(dsb, kwin)
            if half == 0:
                dqn_ref[...] = dq_h
            else:
                dqn_ref[...] = jnp.where(lo, dqn_ref[...], dq_h)
            dk_acc[pl.ds(ks, WIN), :] += _tn(dsb, qa)
            dv_acc[pl.ds(ks, WIN), :] += _tn(pb_ref[half], doa)
        qg, kg = qg_ref[...], kg_ref[...]
        xq, rq = _head_norm(q_ref[...], lo)
        dq, dqg_rows = _head_norm_bwd(dqn_ref[...] * scale, xq, rq, qg, lo)
        dq_ref[...] = dq.astype(BF16)
        dqg_ref[...] += jnp.sum(dqg_rows, axis=0, keepdims=True)

        @pl.when(i == nq - 1)
        def _():
            dkg = jnp.zeros((1, LANE), F32)
            for t in range(s // rt):
                rows = pl.ds(t * rt, rt)
                xk, rk = _head_norm(k_ref[rows, :], lo)
                dk, dkg_rows = _head_norm_bwd(dk_acc[rows, :], xk, rk, kg, lo)
                dk_ref[rows, :] = dk.astype(BF16)
                dv_ref[rows, :] = dv_acc[rows, :].astype(BF16)
                dkg = dkg + jnp.sum(dkg_rows, axis=0, keepdims=True)
            dkg_ref[...] = dkg

    tile = pl.BlockSpec((TQ, LANE), lambda j, i: (i, j))
    kcol = lambda c0: pl.BlockSpec((s, LANE), lambda j, i: (0, c0 + j))
    gain = pl.BlockSpec((None, 1, LANE), lambda j, i: (j, 0, 0))
    return _hosted_call(
        body, xchg, name="attn_bwd", grid=(NH // 2, nq),
        in_specs=[
            tile, kcol(AW // LANE), kcol(2 * AW // LANE), tile, kcol(AW // LANE),
            _layer((1, LANE), l), _layer((1, LANE), l),
            _bias_layer_spec(l), tile, tile, tile,
        ],
        out_specs=[tile, kcol(0), kcol(0), _bias_spec(), gain, gain],
        out_shape=[SDS((s, AW), BF16), SDS((s, AW), BF16), SDS((s, AW), BF16),
                   SDS((NVAR, NH, TQ, WIN), F32), SDS((NH // 2, 1, LANE), F32), SDS((NH // 2, 1, LANE), F32)],
        scratch_shapes=[pltpu.VMEM((s, LANE), F32), pltpu.VMEM((s, LANE), F32),
                        pltpu.VMEM((2, TQ, WIN), F32), pltpu.VMEM((2, TQ, WIN), F32),
                        pltpu.VMEM((2, TQ, WIN), BF16), pltpu.VMEM((2, TQ, WIN), BF16),
                        pltpu.VMEM((TQ, LANE), F32), pltpu.VMEM((TQ, LANE), F32)],
        args=[qkv, qkv, qkv, p, p, qg2, kg2, bias, lse, o32, dmix])


def _conv_pool_bwd(p, dmix, conv_w, wbd, pscale, l):
    s = p.shape[0]
    rt = min(256, s)
    nrt = s // rt

    def body(gb_ref, gc_ref, hin_ref, u_ref, cw_ref, wbd_ref, ps_ref, dy_ref,
             dgb_ref, dgc_ref, dhin_ref, du_ref, dcw_ref, dwbd_ref, dps_ref, buf_a, buf_b, buf_c, buf_d):
        g = pl.program_id(0)
        zpad = jnp.zeros((PAD, LANE), F32)
        for buf in (buf_a, buf_b, buf_c):
            buf[pl.ds(0, PAD), :] = zpad
            buf[pl.ds(PAD + s, PAD), :] = zpad

        @pl.when(g < 2)
        def _conv():
            for t in range(nrt):
                rows = pl.ds(t * rt, rt)
                buf_a[pl.ds(PAD + t * rt, rt), :] = gc_ref[rows, :] * hin_ref[rows, :]
                buf_b[pl.ds(PAD + t * rt, rt), :] = dy_ref[rows, :] * gb_ref[rows, :]
            w0, w1, w2 = cw_ref[0:1, :], cw_ref[1:2, :], cw_ref[2:3, :]
            d0 = jnp.zeros((1, LANE), F32)
            d1 = jnp.zeros((1, LANE), F32)
            d2 = jnp.zeros((1, LANE), F32)
            for t in range(nrt):
                rows = pl.ds(t * rt, rt)
                r0 = PAD + t * rt
                z2, z1, z0 = buf_a[pl.ds(r0 - 2, rt), :], buf_a[pl.ds(r0 - 1, rt), :], buf_a[pl.ds(r0, rt), :]
                y = w0 * z2 + w1 * z1 + w2 * z0
                dgb_ref[rows, :] = (dy_ref[rows, :] * y).astype(BF16)
                e0 = buf_b[pl.ds(r0, rt), :]
                d0 = d0 + jnp.sum(e0 * z2, axis=0, keepdims=True)
                d1 = d1 + jnp.sum(e0 * z1, axis=0, keepdims=True)
                d2 = d2 + jnp.sum(e0 * z0, axis=0, keepdims=True)
                dz = w2 * e0 + w1 * buf_b[pl.ds(r0 + 1, rt), :] + w0 * buf_b[pl.ds(r0 + 2, rt), :]
                dgc_ref[rows, :] = (dz * hin_ref[rows, :]).astype(BF16)
                dhin_ref[rows, :] = (dz * gc_ref[rows, :]).astype(BF16)
            dcw_ref[0:1, :] = d0
            dcw_ref[1:2, :] = d1
            dcw_ref[2:3, :] = d2

        @pl.when(g >= 2)
        def _pool():
            jj = g - 2
            lo = _lo_mask()
            _pool_window_sums(u_ref, buf_a, buf_b, jj, s, rt)
            wb = wbd_ref[...]
            ps = ps_ref[...]
            dps = jnp.zeros((1, LANE), F32)
            dwb = jnp.zeros((LANE, LANE), F32)
            for t in range(nrt):
                rows = pl.ds(t * rt, rt)
                r0 = PAD + t * rt
                cnt = _pool_counts(jj, lo, t * rt, rt)
                wsum = jnp.where(lo, buf_b[pl.ds(r0, rt), :], buf_a[pl.ds(r0, rt), :])
                mb = (wsum / cnt - u_ref[rows, :]).astype(BF16)
                dy = dy_ref[rows, :]
                dps = dps + jnp.sum(dy * _nn(mb, wb), axis=0, keepdims=True)
                dmp = (dy * ps).astype(BF16)
                dwb = dwb + _tn(mb, dmp)
                dm = _nt(dmp, wb)
                buf_d[rows, :] = dm
                buf_c[pl.ds(r0, rt), :] = dm / cnt
            dps_ref[...] = dps
            dwbd_ref[...] = dwb

            def stage(src, dst, sh):
                for t in range(nrt):
                    r0 = PAD + t * rt
                    dst[pl.ds(r0, rt), :] = src[pl.ds(r0, rt), :] + src[pl.ds(r0 + sh, rt), :]

            def finish(first, second):
                for t in range(nrt):
                    rows = pl.ds(t * rt, rt)
                    r0 = PAD + t * rt
                    fw = jnp.where(lo, first[pl.ds(r0, rt), :], second[pl.ds(r0, rt), :])
                    du_ref[rows, :] = (fw - buf_d[rows, :]).astype(BF16)

            stage(buf_c, buf_a, 1)
            stage(buf_a, buf_b, 2)

            @pl.when(jj == 0)
            def _():
                finish(buf_a, buf_b)

            @pl.when(jj == 1)
            def _():
                stage(buf_b, buf_c, 4)
                stage(buf_c, buf_a, 8)
                finish(buf_c, buf_a)

    cblk = pl.BlockSpec((s, LANE), lambda g: (0, jnp.minimum(g, 1)))
    pblk = pl.BlockSpec((s, LANE), lambda g: (0, jnp.maximum(g - 2, 0)))
    padded = pltpu.VMEM((s + 2 * PAD, LANE), F32)
    return pl.pallas_call(
        body, name="conv_pool_bwd", grid=(4,),
        in_specs=_cp_in_specs(s, l) + [pl.BlockSpec((s, LANE), lambda g: (0, AW // LANE + g))],
        out_specs=[cblk, cblk, cblk, pblk,
                   pl.BlockSpec((3, LANE), lambda g: (0, jnp.minimum(g, 1))),
                   pl.BlockSpec((None, LANE, LANE), lambda g: (jnp.maximum(g - 2, 0), 0, 0)),
                   pl.BlockSpec((1, LANE), lambda g: (0, jnp.maximum(g - 2, 0)))],
        out_shape=[SDS((s, CW), BF16), SDS((s, CW), BF16), SDS((s, CW), BF16), SDS((s, PWD), BF16),
                   SDS((3, CW), F32), SDS((2, LANE, LANE), F32), SDS((1, PWD), F32)],
        scratch_shapes=[padded, padded, padded, pltpu.VMEM((s, LANE), F32)],
        compiler_params=_cp(),
    )(p, p, p, p, conv_w, wbd, pscale, dmix)


def _in_proj_bwd(parts, x, dxm, g1, win_t, l, xchg=None):
    s = x.shape[0]
    t = min(256, s)
    widths = [a.shape[1] for a in parts]
    offs = [int(o) for o in np.cumsum([0] + widths[:-1])]
    n = len(parts)

    def body(*refs):
        part_refs = refs[:n]
        x_ref, dxm_ref, g_ref, w_ref, dx_ref, dp_ref, dg_ref = refs[n:]

        @pl.when(pl.program_id(0) == 0)
        def _():
            dg_ref[...] = jnp.zeros_like(dg_ref)
        for r, o, w in zip(part_refs, offs, widths):
            dp_ref[:, o:o + w] = r[...]
        dh = _nn(dp_ref[...], w_ref[...])
        dx_n, dgr = _rms_bwd(dh, x_ref[...], g_ref[...])
        dg_ref[...] += jnp.sum(dgr, axis=0, keepdims=True)
        dx_ref[...] = dxm_ref[...] + dx_n

    row = lambda c: pl.BlockSpec((t, c), lambda i: (i, 0))
    return _hosted_call(
        body, xchg, name="in_proj_bwd", grid=(s // t,),
        in_specs=[row(w) for w in widths] + [row(D), row(D), _layer((1, D), l), _const((DIN, D))],
        out_specs=[row(D), row(DIN), pl.BlockSpec((1, D), lambda i: (0, 0))],
        out_shape=[SDS((s, D), F32), SDS((s, DIN), BF16), SDS((1, D), F32)], args=[*parts, x, dxm, g1, win_t])


def _wgrad(a, b, tag, xchg=None):
    s, m = a.shape
    mb = 512

    def body(a_ref, b_ref, o_ref):
        o_ref[...] = _tn(a_ref[...], b_ref[...]).astype(BF16)

    (out,), got = _hosted_call(
        body, xchg, name=f"wgrad_{tag}", grid=(m // mb,),
        in_specs=[pl.BlockSpec((s, mb), lambda mi: (0, mi)), _const((s, D))],
        out_specs=[pl.BlockSpec((mb, D), lambda mi: (mi, 0))],
        out_shape=[SDS((m, D), BF16)], args=[a, b])
    return out, got


def _bias_tables(gvec):
    def body(g_ref, o_ref):
        qc = lax.broadcasted_iota(jnp.int32, (TQ, WIN), 0) // CHUNK
        kc = lax.broadcasted_iota(jnp.int32, (TQ, WIN), 1) // CHUNK
        for var in range(NVAR):
            vec = jnp.broadcast_to(g_ref[:, var * TQ:var * TQ + NTOE], (TQ, NTOE))
            toe = pltpu.roll(vec, NTOE - TQ + 1, 1, stride=1, stride_axis=0)[:, :WIN]
            rel = (BAND - var * TQ) // CHUNK + qc - kc
            o_ref[var] = jnp.where((rel >= 0) & (rel <= N_PREV), toe, NEG)

    return pl.pallas_call(
        body, name="bias_tables", grid=(L, NH),
        in_specs=[pl.BlockSpec((None, None, 1, NG), lambda l, h: (l, h, 0, 0))],
        out_specs=pl.BlockSpec((None, NVAR, None, TQ, WIN), lambda l, h: (l, 0, h, 0, 0)),
        out_shape=SDS((L, NVAR, NH, TQ, WIN), F32),
        compiler_params=_cp(),
    )(gvec)


def _bias_tables_grad(dbias, l):
    nb = NTOE // LANE
    wb = WIN // LANE

    def body(d_ref, o_ref):
        ii = lax.broadcasted_iota(jnp.int32, (LANE, LANE), 0)
        jj = lax.broadcasted_iota(jnp.int32, (LANE, LANE), 1)
        flip = jnp.where(ii + jj == LANE - 1, 1.0, 0.0).astype(BF16)
        o_ref[...] = jnp.zeros_like(o_ref)
        for var in range(NVAR):
            blocks = []
            for b in range(nb):
                src = nb - 1 - b
                if src >= wb:
                    blocks.append(jnp.zeros((TQ, LANE), F32))
                    continue
                xv = d_ref[var, :, src * LANE:(src + 1) * LANE]
                hi = xv.astype(BF16)
                lo = (xv - hi.astype(F32)).astype(BF16)
                blocks.append(_nn(hi, flip) + _nn(lo, flip))
            rev = jnp.concatenate(blocks, axis=1)
            skew = pltpu.roll(rev, NTOE - TQ + 1, 1, stride=1, stride_axis=0)
            off = NG - NTOE - var * TQ
            o_ref[:, off:off + NTOE] += jnp.sum(skew, axis=0, keepdims=True)

    return pl.pallas_call(
        body, name=f"bias_tables_grad_l{l}", grid=(NH,),
        in_specs=[pl.BlockSpec((NVAR, None, TQ, WIN), lambda h: (0, h, 0, 0))],
        out_specs=pl.BlockSpec((None, 1, NG), lambda h: (h, 0, 0)),
        out_shape=SDS((NH, 1, NG), F32),
        compiler_params=_cp(),
    )(dbias)


_SIBLING = (0, 0, 1)
_CHIPS = [(1, 0, 0), (0, 1, 0), (1, 1, 0)]
_MASKS = [_SIBLING] + _CHIPS + [(1, 0, 1), (0, 1, 1), (1, 1, 1)]


def _position():
    return lax.axis_index("x"), lax.axis_index("y"), lax.axis_index("c")


def _peer(pos, mask):
    return tuple(1 - a if f else a for a, f in zip(pos, mask))


def _index(pos):
    return 4 * pos[0] + 2 * pos[1] + pos[2]


def _exchange_phases(items, src, dst, sems):
    send_sems, recv_sems, local_sems = sems
    me = _position()
    sib = _peer(me, _SIBLING)

    def remote(s_ref, d_ref, pi, n, to):
        return pltpu.make_async_remote_copy(
            src_ref=s_ref, dst_ref=d_ref, send_sem=send_sems.at[pi, n], recv_sem=recv_sems.at[pi, n],
            device_id=to, device_id_type=MESH_ID)

    def parts(n):
        it = items[n]
        r = SHARD_ROWS[it[1]]
        block = lambda ref, pos: ref.at[pl.ds(_index(pos) * r, r), :]
        if it[0] == "gather":
            own = src[n].at[it[2]]
            local = pltpu.make_async_copy(own, block(dst[n], me), local_sems.at[n])
            sends = [remote(own, block(dst[n], me), pi, n, _peer(me, m)) for pi, m in enumerate([_SIBLING] + _CHIPS)]
            hops = [(remote(block(dst[n], _peer(me, m)), block(dst[n], _peer(me, m)), 1 + j, n, _peer(me, m)),
                     remote(block(dst[n], _peer(me, m)), block(dst[n], _peer(me, m)), 4 + j, n, sib))
                    for j, m in enumerate(_CHIPS)]
            lands = [remote(own, block(dst[n], sib), 0, n, sib)]
            lands += [remote(own, block(dst[n], _peer(sib, m)), 4 + j, n, sib) for j, m in enumerate(_CHIPS)]
        else:
            local = pltpu.make_async_copy(block(src[n], me), dst[n].at[_index(me)], local_sems.at[n])
            sends = [remote(block(src[n], _peer(me, m)), dst[n].at[_index(me)], pi, n, _peer(me, m))
                     for pi, m in enumerate(_MASKS)]
            hops = []
            lands = [remote(block(src[n], me), dst[n].at[_index(_peer(me, m))], pi, n, _peer(me, m))
                     for pi, m in enumerate(_MASKS)]
        return local, sends, hops, lands

    def start():
        for n in range(len(items)):
            local, sends, _, _ = parts(n)
            local.start()
            for cp in sends:
                cp.start()

    def relay():
        for n in range(len(items)):
            for arrived, onward in parts(n)[2]:
                arrived.wait_recv()
                onward.start()

    def finish():
        for n in range(len(items)):
            local, sends, hops, lands = parts(n)
            for cp in lands:
                cp.wait_recv()
            for cp in sends + [onward for _, onward in hops]:
                cp.wait_send()
            local.wait()

    return start, relay, finish


def _hosted_call(body, xchg, *, name, grid, in_specs, out_specs, out_shape, args, scratch_shapes=(), relay_at=0.8):
    if not xchg:
        outs = pl.pallas_call(
            body, name=name, grid=grid, in_specs=list(in_specs), out_specs=list(out_specs),
            out_shape=list(out_shape), scratch_shapes=list(scratch_shapes), compiler_params=_cp())(*args)
        return outs, []
    items = [it for it, _ in xchg]
    n_in, n_out, n_scr, nit = len(args), len(out_shape), len(scratch_shapes), len(items)
    hbm = pl.BlockSpec(memory_space=pl.ANY)
    steps = int(np.prod(grid))
    relay_step = min(int(relay_at * steps), steps - 1)

    def dst_shape(it):
        r = SHARD_ROWS[it[1]]
        return SDS((NDEV * r, D) if it[0] == "gather" else (NDEV, r, D), BF16)

    def wrapped(*refs):
        ins = refs[:n_in]
        src = refs[n_in:n_in + nit]
        outs = refs[n_in + nit:n_in + nit + n_out]
        dst = refs[n_in + nit + n_out:n_in + 2 * nit + n_out]
        scratch = refs[n_in + 2 * nit + n_out:n_in + 2 * nit + n_out + n_scr]
        start, relay, finish = _exchange_phases(items, src, dst, refs[n_in + 2 * nit + n_out + n_scr:])
        step = 0
        for d, g in enumerate(grid):
            step = step * g + pl.program_id(d)
        pl.when(step == 0)(start)
        body(*ins, *outs, *scratch)
        pl.when(step == relay_step)(relay)
        pl.when(step == steps - 1)(finish)

    npeer = len(_MASKS)
    res = pl.pallas_call(
        wrapped, name=name, grid=grid,
        in_specs=list(in_specs) + [hbm] * nit,
        out_specs=list(out_specs) + [hbm] * nit,
        out_shape=list(out_shape) + [dst_shape(it) for it in items],
        scratch_shapes=list(scratch_shapes) + [
            pltpu.SemaphoreType.DMA((npeer, nit)), pltpu.SemaphoreType.DMA((npeer, nit)), pltpu.SemaphoreType.DMA((nit,))],
        compiler_params=_cp(),
    )(*args, *[a for _, a in xchg])
    return list(res[:n_out]), list(res[n_out:])


def _gather_first(xchg):
    def body(o_ref):
        o_ref[...] = jnp.zeros_like(o_ref)

    return _hosted_call(
        body, xchg, name="gather_first", grid=(1,),
        in_specs=[], out_specs=[pl.BlockSpec((8, LANE), lambda i: (0, 0))], out_shape=[SDS((8, LANE), F32)], args=[])[1]


def _sum_slots(slots, xchg=None):
    _, r, _ = slots[0].shape
    rt = 64

    def body(*refs):
        o_ref = refs[L]
        for l in range(L):
            acc = refs[l][0].astype(F32)
            for d in range(1, NDEV):
                acc = acc + refs[l][d].astype(F32)
            o_ref[l] = acc

    (out,), got = _hosted_call(
        body, xchg, name=f"sum_slots_r{r}" + ("_x" if xchg else ""), grid=(r // rt,),
        in_specs=[pl.BlockSpec((NDEV, rt, D), lambda i: (0, i, 0))] * L,
        out_specs=[pl.BlockSpec((L, rt, D), lambda i: (0, i, 0))],
        out_shape=[SDS((L, r, D), F32)], args=list(slots))
    return out, got


def _exchange_small(v, reduce):
    rows = v.shape[0]

    def body(v_ref, o_ref, *scratch):
        if reduce:
            slots, send_sems, recv_sems = scratch
        else:
            slots = o_ref
            send_sems, recv_sems = scratch
        me = _position()
        slots[_index(me)] = v_ref[...]
        sends = []
        for pi, mask in enumerate(_MASKS):
            cp = pltpu.make_async_remote_copy(
                src_ref=v_ref, dst_ref=slots.at[_index(me)], send_sem=send_sems.at[pi], recv_sem=recv_sems.at[pi],
                device_id=_peer(me, mask), device_id_type=MESH_ID)
            cp.start()
            sends.append(cp)
        for pi, mask in enumerate(_MASKS):
            peer = _peer(me, mask)
            pltpu.make_async_remote_copy(
                src_ref=v_ref, dst_ref=slots.at[_index(peer)], send_sem=send_sems.at[pi], recv_sem=recv_sems.at[pi],
                device_id=peer, device_id_type=MESH_ID).wait_recv()
        for cp in sends:
            cp.wait_send()
        if reduce:
            acc = slots[0]
            for d in range(1, NDEV):
                acc = acc + slots[d]
            o_ref[...] = acc

    vm = pl.BlockSpec(memory_space=pltpu.VMEM)
    sems = [pltpu.SemaphoreType.DMA((len(_MASKS),)), pltpu.SemaphoreType.DMA((len(_MASKS),))]
    return pl.pallas_call(
        body, name="reduce_small" if reduce else "gather_small",
        in_specs=[vm], out_specs=vm,
        out_shape=SDS((rows, LANE) if reduce else (NDEV, rows, LANE), F32),
        scratch_shapes=([pltpu.VMEM((NDEV, rows, LANE), F32)] if reduce else []) + sems,
        compiler_params=_cp(),
    )(v)


def _adamw_update(w_ref, g_ref, m_ref, v_ref, d_ref, nm_ref, nv_ref):
    gv = g_ref[...]
    mn = B1 * m_ref[...] + (1.0 - B1) * gv
    vn = B2 * v_ref[...] + (1.0 - B2) * jnp.square(gv)
    nm_ref[...] = mn
    nv_ref[...] = vn
    m_hat = mn / (1.0 - B1 ** STEP)
    v_hat = vn / (1.0 - B2 ** STEP)
    d_ref[...] = -LR * (m_hat / (jnp.sqrt(v_hat) + AEPS) + WD * w_ref[...])


def _adamw_small(ws, gs, ms, vs):
    n = len(ws)

    def body(*refs):
        for i in range(n):
            _adamw_update(*[refs[j * n + i] for j in range(7)])

    vm = pl.BlockSpec(memory_space=pltpu.VMEM)
    res = pl.pallas_call(
        body, name="adamw_small", in_specs=[vm] * (4 * n), out_specs=[vm] * (3 * n),
        out_shape=[SDS(w.shape, F32) for _ in range(3) for w in ws],
        compiler_params=_cp(),
    )(*ws, *gs, *ms, *vs)
    return res[:n], res[n:2 * n], res[2 * n:]


def _adamw(w, g, m, v):
    rows, cols = w.shape
    t = rows
    for cand in (512, 256, 128, 64, 32, 16, 8):
        if rows % cand == 0:
            t = cand
            break

    def body(*refs):
        _adamw_update(*refs)

    blk = pl.BlockSpec((t, cols), lambda i: (i, 0))
    return pl.pallas_call(
        body, name=f"adamw_{rows}x{cols}", grid=(rows // t,),
        in_specs=[blk] * 4, out_specs=[blk] * 3,
        out_shape=[SDS((rows, cols), F32)] * 3,
        compiler_params=_cp(),
    )(w, g, m, v)


_DIST0 = BAND + TQ - 1
_N_FAR = _DIST0 - REL_CLIP + 1
_N_NEAR = NG - _N_FAR - (2 * REL_CLIP - 1)


def _bias_vector(rel_bias):
    far = jnp.broadcast_to(rel_bias[..., -1:], (L, NH, _N_FAR))
    near = jnp.broadcast_to(rel_bias[..., :1], (L, NH, _N_NEAR))
    return jnp.concatenate([far, lax.rev(rel_bias[..., 1:-1], (2,)), near], axis=2)[:, :, None, :]


def _bias_vector_grad(dgr):
    first = jnp.sum(dgr[..., :_N_NEAR], axis=-1, keepdims=True)
    last = jnp.sum(dgr[..., NG - _N_FAR:], axis=-1, keepdims=True)
    return jnp.concatenate([first, dgr[..., _N_NEAR:NG - _N_FAR], last], axis=-1)


def _pool_blockdiag(pool_w):
    eye = jnp.eye(2, dtype=F32)
    pw = pool_w.reshape(L, 2, 2, HD, HD)
    return jnp.einsum("ljaik,ab->ljaibk", pw, eye).reshape(L, 2, LANE, LANE)


def _pool_blockdiag_grad(dwbd):
    d = dwbd.reshape(L, 2, 2, HD, 2, HD)
    return jnp.stack([d[:, :, 0, :, 0, :], d[:, :, 1, :, 1, :]], axis=2).reshape(L, 4, HD, HD)


def _pack(arrays, rows):
    flat = jnp.concatenate([a.reshape(-1).astype(F32) for a in arrays])
    return jnp.pad(flat, (0, rows * LANE - flat.shape[0])).reshape(rows, LANE)


def _unpack(packed, shapes):
    flat = packed.reshape(-1)
    out, o = [], 0
    for shp in shapes:
        n = int(np.prod(shp))
        out.append(flat[o:o + n].reshape(shp))
        o += n
    return out


def _rows_for(shapes):
    n = sum(int(np.prod(s)) for s in shapes)
    return -(-n // (8 * LANE)) * 8


def _grads(x, target, small_w, shards):
    g1, qg, kg, rb, cw, pw, ps, g2 = small_w
    g1 = g1.reshape(L, 1, D)
    g2 = g2.reshape(L, 1, D)
    qg2 = jnp.tile(qg, (1, 2)).reshape(L, 1, LANE)
    kg2 = jnp.tile(kg, (1, 2)).reshape(L, 1, LANE)
    ps3 = ps.reshape(L, 1, PWD)
    bias = _bias_tables(_bias_vector(rb))
    wbd = _pool_blockdiag(pw).astype(BF16)

    def gather(*kl):
        return [(("gather", k, l), shards[k]) for k, l in kl if l < L]

    full = {}

    def arrived(got, *kl):
        full.update(zip([x for x in kl if x[1] < L], got))

    arrived(_gather_first(gather((0, 0))), (0, 0))
    saved = []
    h = x
    for l in range(L):
        kl = ((1, 0), (2, 0)) if l == 0 else ((1, l),)
        (p, h_b, qkv), got = _in_proj(h, g1, full[0, l], qg2, kg2, l, gather(*kl))
        arrived(got, *kl)
        kl = ((3, l),)
        (mix, lse, o32), got = _attn_fwd(qkv, bias, l, gather(*kl))
        arrived(got, *kl)
        mix = _conv_pool_fwd(p, mix, cw, wbd, ps3, l)
        kl = ((0, l + 1), (2, l + 1))
        (xm, a, xo), got = _mlp_fwd(h, mix, full[1, l], g2, full[2, l], full[3, l], l, gather(*kl))
        arrived(got, *kl)
        saved.append((h, h_b, p, qkv, mix, lse, o32, xm, a))
        h = xo
    dx, sq = _loss_grad(h, target)

    grads = {}
    slots = {}

    def scatter(*kl):
        return [(("scatter", k), grads[k, l]) for k, l in kl if l < L]

    def left(got, *kl):
        slots.update(zip([x for x in kl if x[1] < L], got))

    per_layer = [None] * L
    for l in reversed(range(L)):
        x_in, h_b, p, qkv, mix, lse, o32, xm, a = saved[l]
        (dxm, dmix, f_b, da_b, h2_b, dxo_b, dxm_b, dg2), got = _mlp_bwd(
            dx, a, xm, g2, full[2, l], full[3, l], full[1, l], l, scatter((3, l + 1)))
        left(got, (3, l + 1))
        grads[1, l], _ = _wgrad(mix, dxm_b, f"w_out_l{l}")
        grads[2, l], got = _wgrad(da_b, h2_b, f"w_mlp1_l{l}", scatter((1, l)))
        left(got, (1, l))
        grads[3, l], got = _wgrad(f_b, dxo_b, f"w_mlp2_l{l}", scatter((0, 1)) if l == 0 else None)
        left(got, (0, 1))
        (dq, dk, dv, dbias, dqg, dkg), got = _attn_bwd(
            qkv, p, lse, o32, dmix, qg2, kg2, bias, l, scatter((2, l)))
        left(got, (2, l))
        dgb, dgc, dhin, du, dcw, dwbd, dps = _conv_pool_bwd(p, dmix, cw, wbd, ps3, l)
        (dx, dp_b, dg1), got = _in_proj_bwd(
            [dq, dk, dv, dgb, dgc, dhin, du], x_in, dxm, g1, full[0, l], l, scatter((0, l + 1)) if l else None)
        left(got, (0, l + 1))
        grads[0, l], got = _wgrad(dp_b, h_b, f"w_in_l{l}", scatter((3, 0)) if l == 0 else None)
        left(got, (3, 0))
        per_layer[l] = (dg1, dg2, dqg, dkg, _bias_tables_grad(dbias, l), dcw, dwbd, dps)
    g_w1_t, got = _sum_slots([slots[2, l] for l in range(L)], scatter((0, 0)))
    left(got, (0, 0))
    sums = [_sum_slots([slots[k, l] for l in range(L)])[0] for k in (0, 1)] + [
        g_w1_t, _sum_slots([slots[3, l] for l in range(L)])[0]]

    st = [jnp.stack([per_layer[l][k] for l in range(L)]) for k in range(8)]
    small = dict(
        g1=st[0].reshape(L, D), g2=st[1].reshape(L, D),
        qg=st[2].reshape(L, NH, HD).sum(1), kg=st[3].reshape(L, NH, HD).sum(1),
        rb=_bias_vector_grad(st[4].reshape(L, NH, NG)), cw=st[5], pw=_pool_blockdiag_grad(st[6]),
        ps=st[7].reshape(L, PWD))
    return sq, dx, sums, small


def kernel(x, norm1_g, w_in, q_norm_g, k_norm_g, rel_bias, conv_w, pool_w, pool_scale, w_out, norm2_g, w_mlp1, w_mlp2, loss_target, m_norm1_g, m_w_in, m_q_norm_g, m_k_norm_g, m_rel_bias, m_conv_w, m_pool_w, m_pool_scale, m_w_out, m_norm2_g, m_w_mlp1, m_w_mlp2, v_norm1_g, v_w_in, v_q_norm_g, v_k_norm_g, v_rel_bias, v_conv_w, v_pool_w, v_pool_scale, v_w_out, v_norm2_g, v_w_mlp1, v_w_mlp2):
    me = _index(_position())
    cshard = CW // NDEV

    shards = [jnp.swapaxes(w_in, 1, 2).astype(BF16), w_out.astype(BF16),
              jnp.swapaxes(w_mlp1, 1, 2).astype(BF16), w_mlp2.astype(BF16)]
    cw_all = _exchange_small(_pack([conv_w], 8), reduce=False)
    cw_full = jnp.concatenate(
        [cw_all[d].reshape(-1)[:L * 3 * cshard].reshape(L, 3, cshard) for d in range(NDEV)], axis=2)

    small_w = (norm1_g, q_norm_g, k_norm_g, rel_bias, cw_full, pool_w, pool_scale, norm2_g)
    sq, grad_x, (g_win_t, g_wout, g_w1_t, g_w2), small = _grads(x[0], loss_target[0], small_w, shards)
    g_w_in = jnp.swapaxes(g_win_t, 1, 2)
    g_w_mlp1 = jnp.swapaxes(g_w1_t, 1, 2)

    names = ("g1", "qg", "kg", "rb", "cw", "pw", "ps", "g2")
    gshapes = [(L, D), (L, HD), (L, HD), (L, NH, 2 * REL_CLIP + 1), (L, 3, CW), (L, 4, HD, HD), (L, PWD), (L, D)]
    garrs = [small[n] for n in names]
    rows = _rows_for(gshapes + [(1,)])
    total = _exchange_small(_pack(garrs + [sq[0, :1]], rows), reduce=True)
    g_g1, g_qg, g_kg, g_rb, g_cw_full, g_pw, g_ps, g_g2, sq_sum = _unpack(total, gshapes + [(1,)])
    loss = (0.5 / D) * sq_sum[0]
    g_cw = lax.dynamic_slice_in_dim(g_cw_full, me * cshard, cshard, axis=2)

    def big(w, g, m, v):
        shp = w.shape
        r = lambda a: a.reshape(-1, shp[-1])
        return [o.reshape(shp) for o in _adamw(r(w), r(g), r(m), r(v))]

    up_in = big(w_in, g_w_in, m_w_in, v_w_in)
    up_out = big(w_out, g_wout, m_w_out, v_w_out)
    up_1 = big(w_mlp1, g_w_mlp1, m_w_mlp1, v_w_mlp1)
    up_2 = big(w_mlp2, g_w2, m_w_mlp2, v_w_mlp2)

    sw = [norm1_g, q_norm_g, k_norm_g, rel_bias, conv_w, pool_w, pool_scale, norm2_g]
    sg = [g_g1, g_qg, g_kg, g_rb, g_cw, g_pw, g_ps, g_g2]
    sm = [m_norm1_g, m_q_norm_g, m_k_norm_g, m_rel_bias, m_conv_w, m_pool_w, m_pool_scale, m_norm2_g]
    sv = [v_norm1_g, v_q_norm_g, v_k_norm_g, v_rel_bias, v_conv_w, v_pool_w, v_pool_scale, v_norm2_g]
    s_delta, s_m, s_v = _adamw_small(sw, sg, sm, sv)

    def order(small_list, in_, out_, m1, m2):
        g1_, qg_, kg_, rb_, cw_, pw_, ps_, g2_ = small_list
        return [g1_, in_, qg_, kg_, rb_, cw_, pw_, ps_, out_, g2_, m1, m2]

    grads = order(sg, g_w_in, g_wout, g_w_mlp1, g_w2)
    deltas = order(s_delta, up_in[0], up_out[0], up_1[0], up_2[0])
    new_m = order(s_m, up_in[1], up_out[1], up_1[1], up_2[1])
    new_v = order(s_v, up_in[2], up_out[2], up_1[2], up_2[2])
    return (loss, grad_x[None], *grads, *deltas, *new_m, *new_v)
```

```python
import numpy as np
import jax
import jax.numpy as jnp
from jax import lax
from jax.experimental import pallas as pl
from jax.experimental.pallas import tpu as pltpu

F32 = jnp.float32
BF16 = jnp.bfloat16
SDS = jax.ShapeDtypeStruct
MESH_ID = pl.DeviceIdType.MESH

D = 1024
L = 4
CHUNK = 64
N_PREV = 8
HD = 64
NH = 8
AW = 512
CW = 256
PWD = 256
DIN = 2560
DFF = 4096
EPS = 1e-6
NEG = -1e30
REL_CLIP = 128
POOL_WINDOWS = (2, 4, 8, 16)
LR, B1, B2, AEPS, WD, STEP = 0.001, 0.9, 0.999, 1e-08, 0.01, 10

NDEV = 8
LANE = 128
BAND = N_PREV * CHUNK
TQ = 256
WIN = TQ + BAND
NVAR = BAND // TQ + 1
NTOE = -(-(WIN + TQ - 1) // LANE) * LANE
NG = (NVAR - 1) * TQ + NTOE
PAD = 16
RB_NORM = 64
RB_SOFT = 16
VMEM_LIMIT = 56 * 1024 * 1024
SHARD_ROWS = (DIN // NDEV, D // NDEV, DFF // NDEV, DFF // NDEV)


def _cp(**kw):
    return pltpu.CompilerParams(vmem_limit_bytes=VMEM_LIMIT, **kw)


def _nn(a, b):
    return jnp.dot(a, b, preferred_element_type=F32)


def _nt(a, b):
    return lax.dot_general(a, b, (((1,), (1,)), ((), ())), preferred_element_type=F32)


def _tn(a, b):
    return lax.dot_general(a, b, (((0,), (0,)), ((), ())), preferred_element_type=F32)


def _const(shape):
    n = len(shape)
    return pl.BlockSpec(shape, lambda *_: (0,) * n, pipeline_mode=pl.Buffered(1))


def _layer(shape, l):
    n = len(shape)
    return pl.BlockSpec((None,) + tuple(shape), lambda *_: (l,) + (0,) * n, pipeline_mode=pl.Buffered(1))


def _lo_mask():
    return lax.broadcasted_iota(jnp.int32, (1, LANE), 1) < HD


def _half_sum(t, lo):
    s_lo = jnp.sum(jnp.where(lo, t, 0.0), axis=-1, keepdims=True)
    s_hi = jnp.sum(jnp.where(lo, 0.0, t), axis=-1, keepdims=True)
    return jnp.where(lo, s_lo, s_hi)


def _head_norm(x, lo):
    r = lax.rsqrt(_half_sum(x * x, lo) * (1.0 / HD) + EPS)
    return x * r, r


def _head_norm_bwd(dy, xn, r, g, lo):
    dxn = dy * g
    mu = _half_sum(dxn * xn, lo) * (1.0 / HD)
    return r * (dxn - xn * mu), dy * xn


def _rms_bwd(dy, x, g):
    r = lax.rsqrt(jnp.mean(x * x, axis=-1, keepdims=True) + EPS)
    xn = x * r
    dxn = dy * g
    mu = jnp.mean(dxn * xn, axis=-1, keepdims=True)
    return r * (dxn - xn * mu), dy * xn


def _in_proj(x, g1, win_t, qg2, kg2, l, xchg=None):
    s = x.shape[0]
    t = min(512, s)
    nblk = AW // LANE

    def body(x_ref, g_ref, w_ref, qg_ref, kg_ref, p_ref, h_ref, qkv_ref):
        xv = x_ref[...]
        r = lax.rsqrt(jnp.mean(xv * xv, axis=-1, keepdims=True) + EPS)
        h = (xv * r * g_ref[...]).astype(BF16)
        h_ref[...] = h
        p_ref[...] = _nt(h, w_ref[...])
        lo = _lo_mask()
        gains = (qg_ref[...] * (HD ** -0.5), kg_ref[...])
        for r0 in range(0, t, RB_NORM):
            rows = pl.ds(r0, RB_NORM)
            for c in range(3 * nblk):
                cols = pl.ds(c * LANE, LANE)
                v = p_ref[rows, cols]
                if c < 2 * nblk:
                    v = _head_norm(v, lo)[0] * gains[c // nblk]
                qkv_ref[rows, cols] = v.astype(BF16)

    row = lambda c: pl.BlockSpec((t, c), lambda i: (i, 0))
    return _hosted_call(
        body, xchg, name="in_proj", grid=(s // t,),
        in_specs=[row(D), _layer((1, D), l), _const((DIN, D)), _layer((1, LANE), l), _layer((1, LANE), l)],
        out_specs=[row(DIN), row(D), row(3 * AW)],
        out_shape=[SDS((s, DIN), F32), SDS((s, D), BF16), SDS((s, 3 * AW), BF16)], args=[x, g1, win_t, qg2, kg2])


def _bias_spec():
    return pl.BlockSpec((None, 2, TQ, WIN), lambda j, i: (jnp.maximum(NVAR - 1 - i, 0), j, 0, 0))


def _bias_layer_spec(l):
    return pl.BlockSpec((None, None, 2, TQ, WIN), lambda j, i: (l, jnp.maximum(NVAR - 1 - i, 0), j, 0, 0))


def _attn_fwd(qkv, bias, l, xchg=None):
    s = qkv.shape[0]
    nq = s // TQ

    def body(q_ref, k_ref, v_ref, b_ref, o_ref, lse_ref, o32_ref, s_ref, p_ref, m_ref, den_ref, o0_ref):
        i = pl.program_id(1)
        ks = pl.multiple_of(jnp.maximum(i * TQ - BAND, 0), TQ)
        lo = _lo_mask()
        q = q_ref[...]
        kwin = k_ref[pl.ds(ks, WIN), :]
        vwin = v_ref[pl.ds(ks, WIN), :]
        for half in range(2):
            m_ = lo if half == 0 else jnp.logical_not(lo)
            s_ref[half] = _nt(jnp.where(m_, q, jnp.zeros_like(q)), kwin)
            for r0 in range(0, TQ, RB_SOFT):
                rows = pl.ds(r0, RB_SOFT)
                mx = jnp.max(s_ref[half, rows, :] + b_ref[half, rows, :], axis=-1, keepdims=True)
                m_ref[rows, :] = jnp.broadcast_to(mx, (RB_SOFT, LANE))
            for r0 in range(0, TQ, RB_SOFT):
                rows = pl.ds(r0, RB_SOFT)
                mx = m_ref[rows, 0:1]
                e = jnp.exp(s_ref[half, rows, :] + b_ref[half, rows, :] - mx)
                p_ref[half, rows, :] = e.astype(BF16)
                den = jnp.sum(e, axis=-1, keepdims=True)
                den_ref[rows, :] = jnp.broadcast_to(den, (RB_SOFT, LANE))
                lse = mx + jnp.log(den)
                if half == 0:
                    lse_ref[rows, :] = jnp.broadcast_to(lse, (RB_SOFT, LANE))
                else:
                    lse_ref[rows, :] = jnp.where(lo, lse_ref[rows, :], lse)
            o = _nn(p_ref[half], vwin) * (1.0 / den_ref[...])
            if half == 0:
                o0_ref[...] = o
            else:
                o = jnp.where(lo, o0_ref[...], o)
                o32_ref[...] = o
                o_ref[...] = o.astype(BF16)

    tile = pl.BlockSpec((TQ, LANE), lambda j, i: (i, j))
    stat = pltpu.VMEM((TQ, LANE), F32)
    return _hosted_call(
        body, xchg, name="attn_fwd", grid=(NH // 2, nq),
        in_specs=[
            tile,
            pl.BlockSpec((s, LANE), lambda j, i: (0, AW // LANE + j)),
            pl.BlockSpec((s, LANE), lambda j, i: (0, 2 * AW // LANE + j)),
            _bias_layer_spec(l),
        ],
        out_specs=[tile, tile, tile],
        out_shape=[SDS((s, D), BF16), SDS((s, AW), F32), SDS((s, AW), F32)], args=[qkv, qkv, qkv, bias],
        scratch_shapes=[pltpu.VMEM((2, TQ, WIN), F32), pltpu.VMEM((2, TQ, WIN), BF16), stat, stat, stat])


_C0 = 3 * AW // LANE


def _cp_in_specs(s, l):
    blk = lambda f: pl.BlockSpec((s, LANE), f)
    return [
        blk(lambda g: (0, _C0 + jnp.minimum(g, 1))),
        blk(lambda g: (0, _C0 + 2 + jnp.minimum(g, 1))),
        blk(lambda g: (0, _C0 + 4 + jnp.minimum(g, 1))),
        blk(lambda g: (0, _C0 + 6 + jnp.maximum(g - 2, 0))),
        pl.BlockSpec((None, 3, LANE), lambda g: (l, 0, jnp.minimum(g, 1))),
        pl.BlockSpec((None, None, LANE, LANE), lambda g: (l, jnp.maximum(g - 2, 0), 0, 0)),
        pl.BlockSpec((None, 1, LANE), lambda g: (l, 0, jnp.maximum(g - 2, 0))),
    ]


def _pool_window_sums(u_ref, buf_a, buf_b, jj, s, rt):
    nrt = s // rt
    for t in range(nrt):
        buf_a[pl.ds(PAD + t * rt, rt), :] = u_ref[pl.ds(t * rt, rt), :]

    def stage(src, dst, sh):
        for t in range(nrt):
            r0 = PAD + t * rt
            dst[pl.ds(r0, rt), :] = src[pl.ds(r0, rt), :] + src[pl.ds(r0 - sh, rt), :]

    stage(buf_a, buf_b, 1)
    stage(buf_b, buf_a, 2)

    @pl.when(jj == 1)
    def _():
        stage(buf_a, buf_b, 4)
        stage(buf_b, buf_a, 8)


def _pool_counts(jj, lo, r0, rt):
    w = jnp.where(lo, jnp.where(jj == 0, 2.0, 8.0), jnp.where(jj == 0, 4.0, 16.0))
    pos1 = (lax.broadcasted_iota(jnp.int32, (rt, LANE), 0) + (r0 + 1)).astype(F32)
    return jnp.minimum(pos1, w)


def _conv_pool_fwd(p, mix, conv_w, wbd, pscale, l):
    s = p.shape[0]
    rt = min(256, s)
    nrt = s // rt

    def body(gb_ref, gc_ref, hin_ref, u_ref, cw_ref, wbd_ref, ps_ref, mix_in, o_ref, buf_a, buf_b):
        del mix_in
        g = pl.program_id(0)
        zpad = jnp.zeros((PAD, LANE), F32)
        buf_a[pl.ds(0, PAD), :] = zpad
        buf_b[pl.ds(0, PAD), :] = zpad

        @pl.when(g < 2)
        def _conv():
            for t in range(nrt):
                buf_a[pl.ds(PAD + t * rt, rt), :] = gc_ref[pl.ds(t * rt, rt), :] * hin_ref[pl.ds(t * rt, rt), :]
            w0, w1, w2 = cw_ref[0:1, :], cw_ref[1:2, :], cw_ref[2:3, :]
            for t in range(nrt):
                r0 = PAD + t * rt
                y = w0 * buf_a[pl.ds(r0 - 2, rt), :] + w1 * buf_a[pl.ds(r0 - 1, rt), :] + w2 * buf_a[pl.ds(r0, rt), :]
                o_ref[pl.ds(t * rt, rt), :] = (gb_ref[pl.ds(t * rt, rt), :] * y).astype(BF16)

        @pl.when(g >= 2)
        def _pool():
            jj = g - 2
            lo = _lo_mask()
            _pool_window_sums(u_ref, buf_a, buf_b, jj, s, rt)
            wb = wbd_ref[...]
            for t in range(nrt):
                r0 = PAD + t * rt
                wsum = jnp.where(lo, buf_b[pl.ds(r0, rt), :], buf_a[pl.ds(r0, rt), :])
                m = wsum / _pool_counts(jj, lo, t * rt, rt) - u_ref[pl.ds(t * rt, rt), :]
                o_ref[pl.ds(t * rt, rt), :] = (_nn(m.astype(BF16), wb) * ps_ref[...]).astype(BF16)

    return pl.pallas_call(
        body, name="conv_pool_fwd", grid=(4,),
        in_specs=_cp_in_specs(s, l) + [pl.BlockSpec(memory_space=pl.ANY)],
        out_specs=pl.BlockSpec((s, LANE), lambda g: (0, AW // LANE + g)),
        out_shape=SDS((s, D), BF16),
        scratch_shapes=[pltpu.VMEM((s + 2 * PAD, LANE), F32), pltpu.VMEM((s + 2 * PAD, LANE), F32)],
        input_output_aliases={7: 0},
        compiler_params=_cp(),
    )(p, p, p, p, conv_w, wbd, pscale, mix)


def _mlp_fwd(x, mix, wout, g2, w1_t, w2, l, xchg=None):
    s = x.shape[0]
    t = min(256, s)

    def body(x_ref, mix_ref, wo_ref, g_ref, w1_ref, w2_ref, xm_ref, a_ref, xo_ref):
        xm = x_ref[...] + _nn(mix_ref[...], wo_ref[...])
        xm_ref[...] = xm
        r = lax.rsqrt(jnp.mean(xm * xm, axis=-1, keepdims=True) + EPS)
        h2 = (xm * r * g_ref[...]).astype(BF16)
        a = _nt(h2, w1_ref[...])
        a_ref[...] = a.astype(BF16)
        f = jnp.square(jnp.maximum(a, 0.0)).astype(BF16)
        xo_ref[...] = xm + _nn(f, w2_ref[...])

    row = lambda c: pl.BlockSpec((t, c), lambda i: (i, 0))
    return _hosted_call(
        body, xchg, name="mlp_fwd", grid=(s // t,),
        in_specs=[row(D), row(D), _const((D, D)), _layer((1, D), l), _const((DFF, D)), _const((DFF, D))],
        out_specs=[row(D), row(DFF), row(D)],
        out_shape=[SDS((s, D), F32), SDS((s, DFF), BF16), SDS((s, D), F32)], args=[x, mix, wout, g2, w1_t, w2])


def _loss_grad(y, target):
    s = y.shape[0]
    t = min(512, s)

    def body(y_ref, t_ref, dy_ref, acc_ref):
        @pl.when(pl.program_id(0) == 0)
        def _():
            acc_ref[...] = jnp.zeros_like(acc_ref)
        e = y_ref[...] - t_ref[...]
        dy_ref[...] = e * (1.0 / D)
        acc_ref[...] += jnp.sum(e * e)

    row = pl.BlockSpec((t, D), lambda i: (i, 0))
    return pl.pallas_call(
        body, name="loss_grad", grid=(s // t,),
        in_specs=[row, row],
        out_specs=[row, pl.BlockSpec((8, LANE), lambda i: (0, 0))],
        out_shape=[SDS((s, D), F32), SDS((8, LANE), F32)],
        compiler_params=_cp(),
    )(y, target)


def _mlp_bwd(dxo, a, xm, g2, w1_t, w2, wout, l, xchg=None):
    s = dxo.shape[0]
    t = min(256, s)

    def body(dxo_ref, a_ref, xm_ref, g_ref, w1_ref, w2_ref, wo_ref,
             dxm_ref, dmix_ref, f_ref, da_ref, h2_ref, dxob_ref, dxmb_ref, dg_ref):
        @pl.when(pl.program_id(0) == 0)
        def _():
            dg_ref[...] = jnp.zeros_like(dg_ref)
        dxo = dxo_ref[...]
        dxob = dxo.astype(BF16)
        dxob_ref[...] = dxob
        ra = jnp.maximum(a_ref[...].astype(F32), 0.0)
        f_ref[...] = jnp.square(ra).astype(BF16)
        dab = (_nt(dxob, w2_ref[...]) * (2.0 * ra)).astype(BF16)
        da_ref[...] = dab
        dh2 = _nn(dab, w1_ref[...])
        xm = xm_ref[...]
        g = g_ref[...]
        r = lax.rsqrt(jnp.mean(xm * xm, axis=-1, keepdims=True) + EPS)
        h2_ref[...] = (xm * r * g).astype(BF16)
        dx_n, dgr = _rms_bwd(dh2, xm, g)
        dg_ref[...] += jnp.sum(dgr, axis=0, keepdims=True)
        dxm = dxo + dx_n
        dxm_ref[...] = dxm
        dxmb = dxm.astype(BF16)
        dxmb_ref[...] = dxmb
        dmix_ref[...] = _nt(dxmb, wo_ref[...])

    row = lambda c: pl.BlockSpec((t, c), lambda i: (i, 0))
    return _hosted_call(
        body, xchg, name="mlp_bwd", grid=(s // t,),
        in_specs=[row(D), row(DFF), row(D), _layer((1, D), l), _const((DFF, D)), _const((DFF, D)), _const((D, D))],
        out_specs=[row(D), row(D), row(DFF), row(DFF), row(D), row(D), row(D), pl.BlockSpec((1, D), lambda i: (0, 0))],
        out_shape=[SDS((s, D), F32), SDS((s, D), F32), SDS((s, DFF), BF16), SDS((s, DFF), BF16),
                   SDS((s, D), BF16), SDS((s, D), BF16), SDS((s, D), BF16), SDS((1, D), F32)],
        args=[dxo, a, xm, g2, w1_t, w2, wout])


def _attn_bwd(qkv, p, lse, o32, dmix, qg2, kg2, bias, l, xchg=None):
    s = p.shape[0]
    nq = s // TQ
    scale = HD ** -0.5
    rt = min(512, s)

    def body(qs_ref, kb_ref, vb_ref, q_ref, k_ref, qg_ref, kg_ref, b_ref, lse_ref, o_ref, do_ref,
             dq_ref, dk_ref, dv_ref, db_ref, dqg_ref, dkg_ref,
             dk_acc, dv_acc, s_ref, dp_ref, ds_ref, pb_ref, dqn_ref, dl_ref):
        i = pl.program_id(1)
        ks = pl.multiple_of(jnp.maximum(i * TQ - BAND, 0), TQ)
        lo = _lo_mask()

        @pl.when(i == 0)
        def _():
            dk_acc[...] = jnp.zeros_like(dk_acc)
            dv_acc[...] = jnp.zeros_like(dv_acc)
            dqg_ref[...] = jnp.zeros_like(dqg_ref)
            dkg_ref[...] = jnp.zeros_like(dkg_ref)

        @pl.when(i < NVAR)
        def _():
            db_ref[...] = jnp.zeros_like(db_ref)

        qs = qs_ref[...]
        kwin = kb_ref[pl.ds(ks, WIN), :]
        vwin = vb_ref[pl.ds(ks, WIN), :]
        do = do_ref[...]
        dob = do.astype(BF16)
        dl_ref[...] = _half_sum(do * o_ref[...], lo)
        for half in range(2):
            m_ = lo if half == 0 else jnp.logical_not(lo)
            qa = jnp.where(m_, qs, jnp.zeros_like(qs))
            doa = jnp.where(m_, dob, jnp.zeros_like(dob))
            s_ref[half] = _nt(qa, kwin)
            dp_ref[half] = _nt(doa, vwin)
            for r0 in range(0, TQ, RB_SOFT):
                rows = pl.ds(r0, RB_SOFT)
                lse_h = lse_ref[rows, half * HD:half * HD + 1]
                pm = jnp.exp(s_ref[half, rows, :] + b_ref[half, rows, :] - lse_h)
                ds = pm * (dp_ref[half, rows, :] - dl_ref[rows, half * HD:half * HD + 1])
                db_ref[half, rows, :] += ds
                ds_ref[half, rows, :] = ds.astype(BF16)
                pb_ref[half, rows, :] = pm.astype(BF16)
            dsb = ds_ref[half]
            dq_h = _nn(dsb, kwin)
            if half == 0:
                dqn_ref[...] = dq_h
            else:
                dqn_ref[...] = jnp.where(lo, dqn_ref[...], dq_h)
            dk_acc[pl.ds(ks, WIN), :] += _tn(dsb, qa)
            dv_acc[pl.ds(ks, WIN), :] += _tn(pb_ref[half], doa)
        qg, kg = qg_ref[...], kg_ref[...]
        xq, rq = _head_norm(q_ref[...], lo)
        dq, dqg_rows = _head_norm_bwd(dqn_ref[...] * scale, xq, rq, qg, lo)
        dq_ref[...] = dq.astype(BF16)
        dqg_ref[...] += jnp.sum(dqg_rows, axis=0, keepdims=True)

        @pl.when(i == nq - 1)
        def _():
            dkg = jnp.zeros((1, LANE), F32)
            for t in range(s // rt):
                rows = pl.ds(t * rt, rt)
                xk, rk = _head_norm(k_ref[rows, :], lo)
                dk, dkg_rows = _head_norm_bwd(dk_acc[rows, :], xk, rk, kg, lo)
                dk_ref[rows, :] = dk.astype(BF16)
                dv_ref[rows, :] = dv_acc[rows, :].astype(BF16)
                dkg = dkg + jnp.sum(dkg_rows, axis=0, keepdims=True)
            dkg_ref[...] = dkg

    tile = pl.BlockSpec((TQ, LANE), lambda j, i: (i, j))
    kcol = lambda c0: pl.BlockSpec((s, LANE), lambda j, i: (0, c0 + j))
    gain = pl.BlockSpec((None, 1, LANE), lambda j, i: (j, 0, 0))
    return _hosted_call(
        body, xchg, name="attn_bwd", grid=(NH // 2, nq),
        in_specs=[
            tile, kcol(AW // LANE), kcol(2 * AW // LANE), tile, kcol(AW // LANE),
            _layer((1, LANE), l), _layer((1, LANE), l),
            _bias_layer_spec(l), tile, tile, tile,
        ],
        out_specs=[tile, kcol(0), kcol(0), _bias_spec(), gain, gain],
        out_shape=[SDS((s, AW), BF16), SDS((s, AW), BF16), SDS((s, AW), BF16),
                   SDS((NVAR, NH, TQ, WIN), F32), SDS((NH // 2, 1, LANE), F32), SDS((NH // 2, 1, LANE), F32)],
        scratch_shapes=[pltpu.VMEM((s, LANE), F32), pltpu.VMEM((s, LANE), F32),
                        pltpu.VMEM((2, TQ, WIN), F32), pltpu.VMEM((2, TQ, WIN), F32),
                        pltpu.VMEM((2, TQ, WIN), BF16), pltpu.VMEM((2, TQ, WIN), BF16),
                        pltpu.VMEM((TQ, LANE), F32), pltpu.VMEM((TQ, LANE), F32)],
        args=[qkv, qkv, qkv, p, p, qg2, kg2, bias, lse, o32, dmix])


def _conv_pool_bwd(p, dmix, conv_w, wbd, pscale, l):
    s = p.shape[0]
    rt = min(256, s)
    nrt = s // rt

    def body(gb_ref, gc_ref, hin_ref, u_ref, cw_ref, wbd_ref, ps_ref, dy_ref,
             dgb_ref, dgc_ref, dhin_ref, du_ref, dcw_ref, dwbd_ref, dps_ref, buf_a, buf_b, buf_c, buf_d):
        g = pl.program_id(0)
        zpad = jnp.zeros((PAD, LANE), F32)
        for buf in (buf_a, buf_b, buf_c):
            buf[pl.ds(0, PAD), :] = zpad
            buf[pl.ds(PAD + s, PAD), :] = zpad

        @pl.when(g < 2)
        def _conv():
            for t in range(nrt):
                rows = pl.ds(t * rt, rt)
                buf_a[pl.ds(PAD + t * rt, rt), :] = gc_ref[rows, :] * hin_ref[rows, :]
                buf_b[pl.ds(PAD + t * rt, rt), :] = dy_ref[rows, :] * gb_ref[rows, :]
            w0, w1, w2 = cw_ref[0:1, :], cw_ref[1:2, :], cw_ref[2:3, :]
            d0 = jnp.zeros((1, LANE), F32)
            d1 = jnp.zeros((1, LANE), F32)
            d2 = jnp.zeros((1, LANE), F32)
            for t in range(nrt):
                rows = pl.ds(t * rt, rt)
                r0 = PAD + t * rt
                z2, z1, z0 = buf_a[pl.ds(r0 - 2, rt), :], buf_a[pl.ds(r0 - 1, rt), :], buf_a[pl.ds(r0, rt), :]
                y = w0 * z2 + w1 * z1 + w2 * z0
                dgb_ref[rows, :] = (dy_ref[rows, :] * y).astype(BF16)
                e0 = buf_b[pl.ds(r0, rt), :]
                d0 = d0 + jnp.sum(e0 * z2, axis=0, keepdims=True)
                d1 = d1 + jnp.sum(e0 * z1, axis=0, keepdims=True)
                d2 = d2 + jnp.sum(e0 * z0, axis=0, keepdims=True)
                dz = w2 * e0 + w1 * buf_b[pl.ds(r0 + 1, rt), :] + w0 * buf_b[pl.ds(r0 + 2, rt), :]
                dgc_ref[rows, :] = (dz * hin_ref[rows, :]).astype(BF16)
                dhin_ref[rows, :] = (dz * gc_ref[rows, :]).astype(BF16)
            dcw_ref[0:1, :] = d0
            dcw_ref[1:2, :] = d1
            dcw_ref[2:3, :] = d2

        @pl.when(g >= 2)
        def _pool():
            jj = g - 2
            lo = _lo_mask()
            _pool_window_sums(u_ref, buf_a, buf_b, jj, s, rt)
            wb = wbd_ref[...]
            ps = ps_ref[...]
            dps = jnp.zeros((1, LANE), F32)
            dwb = jnp.zeros((LANE, LANE), F32)
            for t in range(nrt):
                rows = pl.ds(t * rt, rt)
                r0 = PAD + t * rt
                cnt = _pool_counts(jj, lo, t * rt, rt)
                wsum = jnp.where(lo, buf_b[pl.ds(r0, rt), :], buf_a[pl.ds(r0, rt), :])
                mb = (wsum / cnt - u_ref[rows, :]).astype(BF16)
                dy = dy_ref[rows, :]
                dps = dps + jnp.sum(dy * _nn(mb, wb), axis=0, keepdims=True)
                dmp = (dy * ps).astype(BF16)
                dwb = dwb + _tn(mb, dmp)
                dm = _nt(dmp, wb)
                buf_d[rows, :] = dm
                buf_c[pl.ds(r0, rt), :] = dm / cnt
            dps_ref[...] = dps
            dwbd_ref[...] = dwb

            def stage(src, dst, sh):
                for t in range(nrt):
                    r0 = PAD + t * rt
                    dst[pl.ds(r0, rt), :] = src[pl.ds(r0, rt), :] + src[pl.ds(r0 + sh, rt), :]

            def finish(first, second):
                for t in range(nrt):
                    rows = pl.ds(t * rt, rt)
                    r0 = PAD + t * rt
                    fw = jnp.where(lo, first[pl.ds(r0, rt), :], second[pl.ds(r0, rt), :])
                    du_ref[rows, :] = (fw - buf_d[rows, :]).astype(BF16)

            stage(buf_c, buf_a, 1)
            stage(buf_a, buf_b, 2)

            @pl.when(jj == 0)
            def _():
                finish(buf_a, buf_b)

            @pl.when(jj == 1)
            def _():
                stage(buf_b, buf_c, 4)
                stage(buf_c, buf_a, 8)
                finish(buf_c, buf_a)

    cblk = pl.BlockSpec((s, LANE), lambda g: (0, jnp.minimum(g, 1)))
    pblk = pl.BlockSpec((s, LANE), lambda g: (0, jnp.maximum(g - 2, 0)))
    padded = pltpu.VMEM((s + 2 * PAD, LANE), F32)
    return pl.pallas_call(
        body, name="conv_pool_bwd", grid=(4,),
        in_specs=_cp_in_specs(s, l) + [pl.BlockSpec((s, LANE), lambda g: (0, AW // LANE + g))],
        out_specs=[cblk, cblk, cblk, pblk,
                   pl.BlockSpec((3, LANE), lambda g: (0, jnp.minimum(g, 1))),
                   pl.BlockSpec((None, LANE, LANE), lambda g: (jnp.maximum(g - 2, 0), 0, 0)),
                   pl.BlockSpec((1, LANE), lambda g: (0, jnp.maximum(g - 2, 0)))],
        out_shape=[SDS((s, CW), BF16), SDS((s, CW), BF16), SDS((s, CW), BF16), SDS((s, PWD), BF16),
                   SDS((3, CW), F32), SDS((2, LANE, LANE), F32), SDS((1, PWD), F32)],
        scratch_shapes=[padded, padded, padded, pltpu.VMEM((s, LANE), F32)],
        compiler_params=_cp(),
    )(p, p, p, p, conv_w, wbd, pscale, dmix)


def _in_proj_bwd(parts, x, dxm, g1, win_t, l, xchg=None):
    s = x.shape[0]
    t = min(256, s)
    widths = [a.shape[1] for a in parts]
    offs = [int(o) for o in np.cumsum([0] + widths[:-1])]
    n = len(parts)

    def body(*refs):
        part_refs = refs[:n]
        x_ref, dxm_ref, g_ref, w_ref, dx_ref, dp_ref, dg_ref = refs[n:]

        @pl.when(pl.program_id(0) == 0)
        def _():
            dg_ref[...] = jnp.zeros_like(dg_ref)
        for r, o, w in zip(part_refs, offs, widths):
            dp_ref[:, o:o + w] = r[...]
        dh = _nn(dp_ref[...], w_ref[...])
        dx_n, dgr = _rms_bwd(dh, x_ref[...], g_ref[...])
        dg_ref[...] += jnp.sum(dgr, axis=0, keepdims=True)
        dx_ref[...] = dxm_ref[...] + dx_n

    row = lambda c: pl.BlockSpec((t, c), lambda i: (i, 0))
    return _hosted_call(
        body, xchg, name="in_proj_bwd", grid=(s // t,),
        in_specs=[row(w) for w in widths] + [row(D), row(D), _layer((1, D), l), _const((DIN, D))],
        out_specs=[row(D), row(DIN), pl.BlockSpec((1, D), lambda i: (0, 0))],
        out_shape=[SDS((s, D), F32), SDS((s, DIN), BF16), SDS((1, D), F32)], args=[*parts, x, dxm, g1, win_t])


def _wgrad(a, b, tag, xchg=None):
    s, m = a.shape
    mb = 512

    def body(a_ref, b_ref, o_ref):
        o_ref[...] = _tn(a_ref[...], b_ref[...]).astype(BF16)

    (out,), got = _hosted_call(
        body, xchg, name=f"wgrad_{tag}", grid=(m // mb,),
        in_specs=[pl.BlockSpec((s, mb), lambda mi: (0, mi)), _const((s, D))],
        out_specs=[pl.BlockSpec((mb, D), lambda mi: (mi, 0))],
        out_shape=[SDS((m, D), BF16)], args=[a, b])
    return out, got


def _bias_tables(gvec, xchg=None):
    def body(g_ref, o_ref):
        qc = lax.broadcasted_iota(jnp.int32, (TQ, WIN), 0) // CHUNK
        kc = lax.broadcasted_iota(jnp.int32, (TQ, WIN), 1) // CHUNK
        for var in range(NVAR):
            vec = jnp.broadcast_to(g_ref[:, var * TQ:var * TQ + NTOE], (TQ, NTOE))
            toe = pltpu.roll(vec, NTOE - TQ + 1, 1, stride=1, stride_axis=0)[:, :WIN]
            rel = (BAND - var * TQ) // CHUNK + qc - kc
            o_ref[var] = jnp.where((rel >= 0) & (rel <= N_PREV), toe, NEG)

    (out,), got = _hosted_call(
        body, xchg, name="bias_tables", grid=(L, NH),
        in_specs=[pl.BlockSpec((None, None, 1, NG), lambda l, h: (l, h, 0, 0))],
        out_specs=[pl.BlockSpec((None, NVAR, None, TQ, WIN), lambda l, h: (l, 0, h, 0, 0))],
        out_shape=[SDS((L, NVAR, NH, TQ, WIN), F32)], args=[gvec])
    return out, got


def _bias_tables_grad(dbias, l):
    nb = NTOE // LANE
    wb = WIN // LANE

    def body(d_ref, o_ref):
        ii = lax.broadcasted_iota(jnp.int32, (LANE, LANE), 0)
        jj = lax.broadcasted_iota(jnp.int32, (LANE, LANE), 1)
        flip = jnp.where(ii + jj == LANE - 1, 1.0, 0.0).astype(BF16)
        o_ref[...] = jnp.zeros_like(o_ref)
        for var in range(NVAR):
            blocks = []
            for b in range(nb):
                src = nb - 1 - b
                if src >= wb:
                    blocks.append(jnp.zeros((TQ, LANE), F32))
                    continue
                xv = d_ref[var, :, src * LANE:(src + 1) * LANE]
                hi = xv.astype(BF16)
                lo = (xv - hi.astype(F32)).astype(BF16)
                blocks.append(_nn(hi, flip) + _nn(lo, flip))
            rev = jnp.concatenate(blocks, axis=1)
            skew = pltpu.roll(rev, NTOE - TQ + 1, 1, stride=1, stride_axis=0)
            off = NG - NTOE - var * TQ
            o_ref[:, off:off + NTOE] += jnp.sum(skew, axis=0, keepdims=True)

    return pl.pallas_call(
        body, name=f"bias_tables_grad_l{l}", grid=(NH,),
        in_specs=[pl.BlockSpec((NVAR, None, TQ, WIN), lambda h: (0, h, 0, 0))],
        out_specs=pl.BlockSpec((None, 1, NG), lambda h: (h, 0, 0)),
        out_shape=SDS((NH, 1, NG), F32),
        compiler_params=_cp(),
    )(dbias)


_SIBLING = (0, 0, 1)
_CHIPS = [(1, 0, 0), (0, 1, 0), (1, 1, 0)]
_MASKS = [_SIBLING] + _CHIPS + [(1, 0, 1), (0, 1, 1), (1, 1, 1)]


def _position():
    return lax.axis_index("x"), lax.axis_index("y"), lax.axis_index("c")


def _peer(pos, mask):
    return tuple(1 - a if f else a for a, f in zip(pos, mask))


def _index(pos):
    return 4 * pos[0] + 2 * pos[1] + pos[2]


def _exchange_phases(items, src, dst, sems):
    send_sems, recv_sems, local_sems = sems
    me = _position()
    sib = _peer(me, _SIBLING)

    def remote(s_ref, d_ref, pi, n, to):
        return pltpu.make_async_remote_copy(
            src_ref=s_ref, dst_ref=d_ref, send_sem=send_sems.at[pi, n], recv_sem=recv_sems.at[pi, n],
            device_id=to, device_id_type=MESH_ID)

    def parts(n):
        it = items[n]
        r = SHARD_ROWS[it[1]]
        block = lambda ref, pos: ref.at[pl.ds(_index(pos) * r, r), :]
        if it[0] == "gather":
            own = src[n].at[it[2]]
            local = pltpu.make_async_copy(own, block(dst[n], me), local_sems.at[n])
            sends = [remote(own, block(dst[n], me), pi, n, _peer(me, m)) for pi, m in enumerate([_SIBLING] + _CHIPS)]
            hops = [(remote(block(dst[n], _peer(me, m)), block(dst[n], _peer(me, m)), 1 + j, n, _peer(me, m)),
                     remote(block(dst[n], _peer(me, m)), block(dst[n], _peer(me, m)), 4 + j, n, sib))
                    for j, m in enumerate(_CHIPS)]
            lands = [remote(own, block(dst[n], sib), 0, n, sib)]
            lands += [remote(own, block(dst[n], _peer(sib, m)), 4 + j, n, sib) for j, m in enumerate(_CHIPS)]
        else:
            local = pltpu.make_async_copy(block(src[n], me), dst[n].at[_index(me)], local_sems.at[n])
            sends = [remote(block(src[n], _peer(me, m)), dst[n].at[_index(me)], pi, n, _peer(me, m))
                     for pi, m in enumerate(_MASKS)]
            hops = []
            lands = [remote(block(src[n], me), dst[n].at[_index(_peer(me, m))], pi, n, _peer(me, m))
                     for pi, m in enumerate(_MASKS)]
        return local, sends, hops, lands

    def start():
        for n in range(len(items)):
            local, sends, _, _ = parts(n)
            local.start()
            for cp in sends:
                cp.start()

    def relay():
        for n in range(len(items)):
            for arrived, onward in parts(n)[2]:
                arrived.wait_recv()
                onward.start()

    def finish():
        for n in range(len(items)):
            local, sends, hops, lands = parts(n)
            for cp in lands:
                cp.wait_recv()
            for cp in sends + [onward for _, onward in hops]:
                cp.wait_send()
            local.wait()

    return start, relay, finish


def _hosted_call(body, xchg, *, name, grid, in_specs, out_specs, out_shape, args, scratch_shapes=(), relay_at=0.8):
    if not xchg:
        outs = pl.pallas_call(
            body, name=name, grid=grid, in_specs=list(in_specs), out_specs=list(out_specs),
            out_shape=list(out_shape), scratch_shapes=list(scratch_shapes), compiler_params=_cp())(*args)
        return outs, []
    items = [it for it, _ in xchg]
    n_in, n_out, n_scr, nit = len(args), len(out_shape), len(scratch_shapes), len(items)
    hbm = pl.BlockSpec(memory_space=pl.ANY)
    steps = int(np.prod(grid))
    relay_step = min(int(relay_at * steps), steps - 1)

    def dst_shape(it):
        r = SHARD_ROWS[it[1]]
        return SDS((NDEV * r, D) if it[0] == "gather" else (NDEV, r, D), BF16)

    def wrapped(*refs):
        ins = refs[:n_in]
        src = refs[n_in:n_in + nit]
        outs = refs[n_in + nit:n_in + nit + n_out]
        dst = refs[n_in + nit + n_out:n_in + 2 * nit + n_out]
        scratch = refs[n_in + 2 * nit + n_out:n_in + 2 * nit + n_out + n_scr]
        start, relay, finish = _exchange_phases(items, src, dst, refs[n_in + 2 * nit + n_out + n_scr:])
        step = 0
        for d, g in enumerate(grid):
            step = step * g + pl.program_id(d)
        pl.when(step == 0)(start)
        body(*ins, *outs, *scratch)
        pl.when(step == relay_step)(relay)
        pl.when(step == steps - 1)(finish)

    npeer = len(_MASKS)
    res = pl.pallas_call(
        wrapped, name=name, grid=grid,
        in_specs=list(in_specs) + [hbm] * nit,
        out_specs=list(out_specs) + [hbm] * nit,
        out_shape=list(out_shape) + [dst_shape(it) for it in items],
        scratch_shapes=list(scratch_shapes) + [
            pltpu.SemaphoreType.DMA((npeer, nit)), pltpu.SemaphoreType.DMA((npeer, nit)), pltpu.SemaphoreType.DMA((nit,))],
        compiler_params=_cp(),
    )(*args, *[a for _, a in xchg])
    return list(res[:n_out]), list(res[n_out:])


def _sum_slots(slots, xchg=None):
    _, r, _ = slots[0].shape
    rt = 64

    def body(*refs):
        o_ref = refs[L]
        for l in range(L):
            acc = refs[l][0].astype(F32)
            for d in range(1, NDEV):
                acc = acc + refs[l][d].astype(F32)
            o_ref[l] = acc

    (out,), got = _hosted_call(
        body, xchg, name=f"sum_slots_r{r}" + ("_x" if xchg else ""), grid=(r // rt,),
        in_specs=[pl.BlockSpec((NDEV, rt, D), lambda i: (0, i, 0))] * L,
        out_specs=[pl.BlockSpec((L, rt, D), lambda i: (0, i, 0))],
        out_shape=[SDS((L, r, D), F32)], args=list(slots))
    return out, got


def _exchange_small(v, reduce):
    rows = v.shape[0]

    def body(v_ref, o_ref, *scratch):
        if reduce:
            slots, send_sems, recv_sems = scratch
        else:
            slots = o_ref
            send_sems, recv_sems = scratch
        me = _position()
        slots[_index(me)] = v_ref[...]
        sends = []
        for pi, mask in enumerate(_MASKS):
            cp = pltpu.make_async_remote_copy(
                src_ref=v_ref, dst_ref=slots.at[_index(me)], send_sem=send_sems.at[pi], recv_sem=recv_sems.at[pi],
                device_id=_peer(me, mask), device_id_type=MESH_ID)
            cp.start()
            sends.append(cp)
        for pi, mask in enumerate(_MASKS):
            peer = _peer(me, mask)
            pltpu.make_async_remote_copy(
                src_ref=v_ref, dst_ref=slots.at[_index(peer)], send_sem=send_sems.at[pi], recv_sem=recv_sems.at[pi],
                device_id=peer, device_id_type=MESH_ID).wait_recv()
        for cp in sends:
            cp.wait_send()
        if reduce:
            acc = slots[0]
            for d in range(1, NDEV):
                acc = acc + slots[d]
            o_ref[...] = acc

    vm = pl.BlockSpec(memory_space=pltpu.VMEM)
    sems = [pltpu.SemaphoreType.DMA((len(_MASKS),)), pltpu.SemaphoreType.DMA((len(_MASKS),))]
    return pl.pallas_call(
        body, name="reduce_small" if reduce else "gather_small",
        in_specs=[vm], out_specs=vm,
        out_shape=SDS((rows, LANE) if reduce else (NDEV, rows, LANE), F32),
        scratch_shapes=([pltpu.VMEM((NDEV, rows, LANE), F32)] if reduce else []) + sems,
        compiler_params=_cp(),
    )(v)


def _adamw_update(w_ref, g_ref, m_ref, v_ref, d_ref, nm_ref, nv_ref):
    gv = g_ref[...]
    mn = B1 * m_ref[...] + (1.0 - B1) * gv
    vn = B2 * v_ref[...] + (1.0 - B2) * jnp.square(gv)
    nm_ref[...] = mn
    nv_ref[...] = vn
    m_hat = mn / (1.0 - B1 ** STEP)
    v_hat = vn / (1.0 - B2 ** STEP)
    d_ref[...] = -LR * (m_hat / (jnp.sqrt(v_hat) + AEPS) + WD * w_ref[...])


def _adamw_small(ws, gs, ms, vs):
    n = len(ws)

    def body(*refs):
        for i in range(n):
            _adamw_update(*[refs[j * n + i] for j in range(7)])

    vm = pl.BlockSpec(memory_space=pltpu.VMEM)
    res = pl.pallas_call(
        body, name="adamw_small", in_specs=[vm] * (4 * n), out_specs=[vm] * (3 * n),
        out_shape=[SDS(w.shape, F32) for _ in range(3) for w in ws],
        compiler_params=_cp(),
    )(*ws, *gs, *ms, *vs)
    return res[:n], res[n:2 * n], res[2 * n:]


def _adamw(w, g, m, v):
    rows, cols = w.shape
    t = rows
    for cand in (512, 256, 128, 64, 32, 16, 8):
        if rows % cand == 0:
            t = cand
            break

    def body(*refs):
        _adamw_update(*refs)

    blk = pl.BlockSpec((t, cols), lambda i: (i, 0))
    return pl.pallas_call(
        body, name=f"adamw_{rows}x{cols}", grid=(rows // t,),
        in_specs=[blk] * 4, out_specs=[blk] * 3,
        out_shape=[SDS((rows, cols), F32)] * 3,
        compiler_params=_cp(),
    )(w, g, m, v)


_DIST0 = BAND + TQ - 1
_N_FAR = _DIST0 - REL_CLIP + 1
_N_NEAR = NG - _N_FAR - (2 * REL_CLIP - 1)


def _bias_vector(rel_bias):
    far = jnp.broadcast_to(rel_bias[..., -1:], (L, NH, _N_FAR))
    near = jnp.broadcast_to(rel_bias[..., :1], (L, NH, _N_NEAR))
    return jnp.concatenate([far, lax.rev(rel_bias[..., 1:-1], (2,)), near], axis=2)[:, :, None, :]


def _bias_vector_grad(dgr):
    first = jnp.sum(dgr[..., :_N_NEAR], axis=-1, keepdims=True)
    last = jnp.sum(dgr[..., NG - _N_FAR:], axis=-1, keepdims=True)
    return jnp.concatenate([first, dgr[..., _N_NEAR:NG - _N_FAR], last], axis=-1)


def _pool_blockdiag(pool_w):
    eye = jnp.eye(2, dtype=F32)
    pw = pool_w.reshape(L, 2, 2, HD, HD)
    return jnp.einsum("ljaik,ab->ljaibk", pw, eye).reshape(L, 2, LANE, LANE)


def _pool_blockdiag_grad(dwbd):
    d = dwbd.reshape(L, 2, 2, HD, 2, HD)
    return jnp.stack([d[:, :, 0, :, 0, :], d[:, :, 1, :, 1, :]], axis=2).reshape(L, 4, HD, HD)


def _pack(arrays, rows):
    flat = jnp.concatenate([a.reshape(-1).astype(F32) for a in arrays])
    return jnp.pad(flat, (0, rows * LANE - flat.shape[0])).reshape(rows, LANE)


def _unpack(packed, shapes):
    flat = packed.reshape(-1)
    out, o = [], 0
    for shp in shapes:
        n = int(np.prod(shp))
        out.append(flat[o:o + n].reshape(shp))
        o += n
    return out


def _rows_for(shapes):
    n = sum(int(np.prod(s)) for s in shapes)
    return -(-n // (8 * LANE)) * 8


def _grads(x, target, small_w, shards):
    g1, qg, kg, rb, cw, pw, ps, g2 = small_w
    g1 = g1.reshape(L, 1, D)
    g2 = g2.reshape(L, 1, D)
    qg2 = jnp.tile(qg, (1, 2)).reshape(L, 1, LANE)
    kg2 = jnp.tile(kg, (1, 2)).reshape(L, 1, LANE)
    ps3 = ps.reshape(L, 1, PWD)
    wbd = _pool_blockdiag(pw).astype(BF16)

    def gather(*kl):
        return [(("gather", k, l), shards[k]) for k, l in kl if l < L]

    full = {}

    def arrived(got, *kl):
        full.update(zip([x for x in kl if x[1] < L], got))

    bias, got = _bias_tables(_bias_vector(rb), gather((0, 0)))
    arrived(got, (0, 0))
    saved = []
    h = x
    for l in range(L):
        kl = ((1, 0), (2, 0)) if l == 0 else ((1, l),)
        (p, h_b, qkv), got = _in_proj(h, g1, full[0, l], qg2, kg2, l, gather(*kl))
        arrived(got, *kl)
        kl = ((3, l),)
        (mix, lse, o32), got = _attn_fwd(qkv, bias, l, gather(*kl))
        arrived(got, *kl)
        mix = _conv_pool_fwd(p, mix, cw, wbd, ps3, l)
        kl = ((0, l + 1), (2, l + 1))
        (xm, a, xo), got = _mlp_fwd(h, mix, full[1, l], g2, full[2, l], full[3, l], l, gather(*kl))
        arrived(got, *kl)
        saved.append((h, h_b, p, qkv, mix, lse, o32, xm, a))
        h = xo
    dx, sq = _loss_grad(h, target)

    grads = {}
    slots = {}

    def scatter(*kl):
        return [(("scatter", k), grads[k, l]) for k, l in kl if l < L]

    def left(got, *kl):
        slots.update(zip([x for x in kl if x[1] < L], got))

    per_layer = [None] * L
    for l in reversed(range(L)):
        x_in, h_b, p, qkv, mix, lse, o32, xm, a = saved[l]
        (dxm, dmix, f_b, da_b, h2_b, dxo_b, dxm_b, dg2), got = _mlp_bwd(
            dx, a, xm, g2, full[2, l], full[3, l], full[1, l], l, scatter((3, l + 1)))
        left(got, (3, l + 1))
        grads[1, l], _ = _wgrad(mix, dxm_b, f"w_out_l{l}")
        grads[2, l], got = _wgrad(da_b, h2_b, f"w_mlp1_l{l}", scatter((1, l)))
        left(got, (1, l))
        kl = ((2, 0),) if l == 0 else ()
        grads[3, l], got = _wgrad(f_b, dxo_b, f"w_mlp2_l{l}", scatter(*kl))
        left(got, *kl)
        kl = ((3, 0), (0, 1)) if l == 0 else ((2, l),)
        (dq, dk, dv, dbias, dqg, dkg), got = _attn_bwd(qkv, p, lse, o32, dmix, qg2, kg2, bias, l, scatter(*kl))
        left(got, *kl)
        dgb, dgc, dhin, du, dcw, dwbd, dps = _conv_pool_bwd(p, dmix, cw, wbd, ps3, l)
        (dx, dp_b, dg1), got = _in_proj_bwd(
            [dq, dk, dv, dgb, dgc, dhin, du], x_in, dxm, g1, full[0, l], l, scatter((0, l + 1)) if l else None)
        left(got, (0, l + 1))
        grads[0, l], _ = _wgrad(dp_b, h_b, f"w_in_l{l}")
        per_layer[l] = (dg1, dg2, dqg, dkg, _bias_tables_grad(dbias, l), dcw, dwbd, dps)
    g_w1_t, got = _sum_slots([slots[2, l] for l in range(L)], scatter((0, 0)))
    left(got, (0, 0))
    sums = [_sum_slots([slots[k, l] for l in range(L)])[0] for k in (0, 1)] + [
        g_w1_t, _sum_slots([slots[3, l] for l in range(L)])[0]]

    st = [jnp.stack([per_layer[l][k] for l in range(L)]) for k in range(8)]
    small = dict(
        g1=st[0].reshape(L, D), g2=st[1].reshape(L, D),
        qg=st[2].reshape(L, NH, HD).sum(1), kg=st[3].reshape(L, NH, HD).sum(1),
        rb=_bias_vector_grad(st[4].reshape(L, NH, NG)), cw=st[5], pw=_pool_blockdiag_grad(st[6]),
        ps=st[7].reshape(L, PWD))
    return sq, dx, sums, small


def kernel(x, norm1_g, w_in, q_norm_g, k_norm_g, rel_bias, conv_w, pool_w, pool_scale, w_out, norm2_g, w_mlp1, w_mlp2, loss_target, m_norm1_g, m_w_in, m_q_norm_g, m_k_norm_g, m_rel_bias, m_conv_w, m_pool_w, m_pool_scale, m_w_out, m_norm2_g, m_w_mlp1, m_w_mlp2, v_norm1_g, v_w_in, v_q_norm_g, v_k_norm_g, v_rel_bias, v_conv_w, v_pool_w, v_pool_scale, v_w_out, v_norm2_g, v_w_mlp1, v_w_mlp2):
    me = _index(_position())
    cshard = CW // NDEV

    shards = [jnp.swapaxes(w_in, 1, 2).astype(BF16), w_out.astype(BF16),
              jnp.swapaxes(w_mlp1, 1, 2).astype(BF16), w_mlp2.astype(BF16)]
    cw_all = _exchange_small(_pack([conv_w], 8), reduce=False)
    cw_full = jnp.concatenate(
        [cw_all[d].reshape(-1)[:L * 3 * cshard].reshape(L, 3, cshard) for d in range(NDEV)], axis=2)

    small_w = (norm1_g, q_norm_g, k_norm_g, rel_bias, cw_full, pool_w, pool_scale, norm2_g)
    sq, grad_x, (g_win_t, g_wout, g_w1_t, g_w2), small = _grads(x[0], loss_target[0], small_w, shards)
    g_w_in = jnp.swapaxes(g_win_t, 1, 2)
    g_w_mlp1 = jnp.swapaxes(g_w1_t, 1, 2)

    names = ("g1", "qg", "kg", "rb", "cw", "pw", "ps", "g2")
    gshapes = [(L, D), (L, HD), (L, HD), (L, NH, 2 * REL_CLIP + 1), (L, 3, CW), (L, 4, HD, HD), (L, PWD), (L, D)]
    garrs = [small[n] for n in names]
    rows = _rows_for(gshapes + [(1,)])
    total = _exchange_small(_pack(garrs + [sq[0, :1]], rows), reduce=True)
    g_g1, g_qg, g_kg, g_rb, g_cw_full, g_pw, g_ps, g_g2, sq_sum = _unpack(total, gshapes + [(1,)])
    loss = (0.5 / D) * sq_sum[0]
    g_cw = lax.dynamic_slice_in_dim(g_cw_full, me * cshard, cshard, axis=2)

    def big(w, g, m, v):
        shp = w.shape
        r = lambda a: a.reshape(-1, shp[-1])
        return [o.reshape(shp) for o in _adamw(r(w), r(g), r(m), r(v))]

    up_in = big(w_in, g_w_in, m_w_in, v_w_in)
    up_out = big(w_out, g_wout, m_w_out, v_w_out)
    up_1 = big(w_mlp1, g_w_mlp1, m_w_mlp1, v_w_mlp1)
    up_2 = big(w_mlp2, g_w2, m_w_mlp2, v_w_mlp2)

    sw = [norm1_g, q_norm_g, k_norm_g, rel_bias, conv_w, pool_w, pool_scale, norm2_g]
    sg = [g_g1, g_qg, g_kg, g_rb, g_cw, g_pw, g_ps, g_g2]
    sm = [m_norm1_g, m_q_norm_g, m_k_norm_g, m_rel_bias, m_conv_w, m_pool_w, m_pool_scale, m_norm2_g]
    sv = [v_norm1_g, v_q_norm_g, v_k_norm_g, v_rel_bias, v_conv_w, v_pool_w, v_pool_scale, v_norm2_g]
    s_delta, s_m, s_v = _adamw_small(sw, sg, sm, sv)

    def order(small_list, in_, out_, m1, m2):
        g1_, qg_, kg_, rb_, cw_, pw_, ps_, g2_ = small_list
        return [g1_, in_, qg_, kg_, rb_, cw_, pw_, ps_, out_, g2_, m1, m2]

    grads = order(sg, g_w_in, g_wout, g_w_mlp1, g_w2)
    deltas = order(s_delta, up_in[0], up_out[0], up_1[0], up_2[0])
    new_m = order(s_m, up_in[1], up_out[1], up_1[1], up_2[1])
    new_v = order(s_v, up_in[2], up_out[2], up_1[2], up_2[2])
    return (loss, grad_x[None], *grads, *deltas, *new_m, *new_v)
```

```python
import numpy as np
import jax
import jax.numpy as jnp
from jax import lax
from jax.experimental import pallas as pl
from jax.experimental.pallas import tpu as pltpu

F32 = jnp.float32
BF16 = jnp.bfloat16
SDS = jax.ShapeDtypeStruct
MESH_ID = pl.DeviceIdType.MESH

D = 1024
L = 4
CHUNK = 64
N_PREV = 8
HD = 64
NH = 8
AW = 512
CW = 256
PWD = 256
DIN = 2560
DFF = 4096
EPS = 1e-6
NEG = -1e30
REL_CLIP = 128
POOL_WINDOWS = (2, 4, 8, 16)
LR, B1, B2, AEPS, WD, STEP = 0.001, 0.9, 0.999, 1e-08, 0.01, 10

NDEV = 8
LANE = 128
BAND = N_PREV * CHUNK
TQ = 256
WIN = TQ + BAND
NVAR = BAND // TQ + 1
NTOE = -(-(WIN + TQ - 1) // LANE) * LANE
NG = (NVAR - 1) * TQ + NTOE
PAD = 16
RB_NORM = 64
RB_SOFT = 16
VMEM_LIMIT = 56 * 1024 * 1024
SHARD_ROWS = (DIN // NDEV, D // NDEV, DFF // NDEV, DFF // NDEV)


def _cp(**kw):
    return pltpu.CompilerParams(vmem_limit_bytes=VMEM_LIMIT, **kw)


def _nn(a, b):
    return jnp.dot(a, b, preferred_element_type=F32)


def _nt(a, b):
    return lax.dot_general(a, b, (((1,), (1,)), ((), ())), preferred_element_type=F32)


def _tn(a, b):
    return lax.dot_general(a, b, (((0,), (0,)), ((), ())), preferred_element_type=F32)


def _const(shape):
    n = len(shape)
    return pl.BlockSpec(shape, lambda *_: (0,) * n, pipeline_mode=pl.Buffered(1))


def _layer(shape, l):
    n = len(shape)
    return pl.BlockSpec((None,) + tuple(shape), lambda *_: (l,) + (0,) * n, pipeline_mode=pl.Buffered(1))


def _lo_mask():
    return lax.broadcasted_iota(jnp.int32, (1, LANE), 1) < HD


def _half_sum(t, lo):
    s_lo = jnp.sum(jnp.where(lo, t, 0.0), axis=-1, keepdims=True)
    s_hi = jnp.sum(jnp.where(lo, 0.0, t), axis=-1, keepdims=True)
    return jnp.where(lo, s_lo, s_hi)


def _head_norm(x, lo):
    r = lax.rsqrt(_half_sum(x * x, lo) * (1.0 / HD) + EPS)
    return x * r, r


def _head_norm_bwd(dy, xn, r, g, lo):
    dxn = dy * g
    mu = _half_sum(dxn * xn, lo) * (1.0 / HD)
    return r * (dxn - xn * mu), dy * xn


def _rms_bwd(dy, x, g):
    r = lax.rsqrt(jnp.mean(x * x, axis=-1, keepdims=True) + EPS)
    xn = x * r
    dxn = dy * g
    mu = jnp.mean(dxn * xn, axis=-1, keepdims=True)
    return r * (dxn - xn * mu), dy * xn


def _in_proj(x, g1, win_t, qg2, kg2, l, xchg=None):
    s = x.shape[0]
    t = min(512, s)
    nblk = AW // LANE

    def body(x_ref, g_ref, w_ref, qg_ref, kg_ref, p_ref, h_ref, qkv_ref):
        xv = x_ref[...]
        r = lax.rsqrt(jnp.mean(xv * xv, axis=-1, keepdims=True) + EPS)
        h = (xv * r * g_ref[...]).astype(BF16)
        h_ref[...] = h
        p_ref[...] = _nt(h, w_ref[...])
        lo = _lo_mask()
        gains = (qg_ref[...] * (HD ** -0.5), kg_ref[...])
        for r0 in range(0, t, RB_NORM):
            rows = pl.ds(r0, RB_NORM)
            for c in range(3 * nblk):
                cols = pl.ds(c * LANE, LANE)
                v = p_ref[rows, cols]
                if c < 2 * nblk:
                    v = _head_norm(v, lo)[0] * gains[c // nblk]
                qkv_ref[rows, cols] = v.astype(BF16)

    row = lambda c: pl.BlockSpec((t, c), lambda i: (i, 0))
    return _hosted_call(
        body, xchg, name="in_proj", grid=(s // t,),
        in_specs=[row(D), _layer((1, D), l), _const((DIN, D)), _layer((1, LANE), l), _layer((1, LANE), l)],
        out_specs=[row(DIN), row(D), row(3 * AW)],
        out_shape=[SDS((s, DIN), F32), SDS((s, D), BF16), SDS((s, 3 * AW), BF16)], args=[x, g1, win_t, qg2, kg2])


def _bias_spec():
    return pl.BlockSpec((None, 2, TQ, WIN), lambda j, i: (jnp.maximum(NVAR - 1 - i, 0), j, 0, 0))


def _bias_layer_spec(l):
    return pl.BlockSpec((None, None, 2, TQ, WIN), lambda j, i: (l, jnp.maximum(NVAR - 1 - i, 0), j, 0, 0))


def _attn_fwd(qkv, bias, l, xchg=None):
    s = qkv.shape[0]
    nq = s // TQ

    def body(q_ref, k_ref, v_ref, b_ref, o_ref, lse_ref, o32_ref, s_ref, p_ref, m_ref, den_ref, o0_ref):
        i = pl.program_id(1)
        ks = pl.multiple_of(jnp.maximum(i * TQ - BAND, 0), TQ)
        lo = _lo_mask()
        q = q_ref[...]
        kwin = k_ref[pl.ds(ks, WIN), :]
        vwin = v_ref[pl.ds(ks, WIN), :]
        for half in range(2):
            m_ = lo if half == 0 else jnp.logical_not(lo)
            s_ref[half] = _nt(jnp.where(m_, q, jnp.zeros_like(q)), kwin)
            for r0 in range(0, TQ, RB_SOFT):
                rows = pl.ds(r0, RB_SOFT)
                mx = jnp.max(s_ref[half, rows, :] + b_ref[half, rows, :], axis=-1, keepdims=True)
                m_ref[rows, :] = jnp.broadcast_to(mx, (RB_SOFT, LANE))
            for r0 in range(0, TQ, RB_SOFT):
                rows = pl.ds(r0, RB_SOFT)
                mx = m_ref[rows, 0:1]
                e = jnp.exp(s_ref[half, rows, :] + b_ref[half, rows, :] - mx)
                p_ref[half, rows, :] = e.astype(BF16)
                den = jnp.sum(e, axis=-1, keepdims=True)
                den_ref[rows, :] = jnp.broadcast_to(den, (RB_SOFT, LANE))
                lse = mx + jnp.log(den)
                if half == 0:
                    lse_ref[rows, :] = jnp.broadcast_to(lse, (RB_SOFT, LANE))
                else:
                    lse_ref[rows, :] = jnp.where(lo, lse_ref[rows, :], lse)
            o = _nn(p_ref[half], vwin) * (1.0 / den_ref[...])
            if half == 0:
                o0_ref[...] = o
            else:
                o = jnp.where(lo, o0_ref[...], o)
                o32_ref[...] = o
                o_ref[...] = o.astype(BF16)

    tile = pl.BlockSpec((TQ, LANE), lambda j, i: (i, j))
    stat = pltpu.VMEM((TQ, LANE), F32)
    return _hosted_call(
        body, xchg, name="attn_fwd", grid=(NH // 2, nq),
        in_specs=[
            tile,
            pl.BlockSpec((s, LANE), lambda j, i: (0, AW // LANE + j)),
            pl.BlockSpec((s, LANE), lambda j, i: (0, 2 * AW // LANE + j)),
            _bias_layer_spec(l),
        ],
        out_specs=[tile, tile, tile],
        out_shape=[SDS((s, D), BF16), SDS((s, AW), F32), SDS((s, AW), F32)], args=[qkv, qkv, qkv, bias],
        scratch_shapes=[pltpu.VMEM((2, TQ, WIN), F32), pltpu.VMEM((2, TQ, WIN), BF16), stat, stat, stat])


_C0 = 3 * AW // LANE


def _cp_in_specs(s, l):
    blk = lambda f: pl.BlockSpec((s, LANE), f)
    return [
        blk(lambda g: (0, _C0 + jnp.minimum(g, 1))),
        blk(lambda g: (0, _C0 + 2 + jnp.minimum(g, 1))),
        blk(lambda g: (0, _C0 + 4 + jnp.minimum(g, 1))),
        blk(lambda g: (0, _C0 + 6 + jnp.maximum(g - 2, 0))),
        pl.BlockSpec((None, 3, LANE), lambda g: (l, 0, jnp.minimum(g, 1))),
        pl.BlockSpec((None, None, LANE, LANE), lambda g: (l, jnp.maximum(g - 2, 0), 0, 0)),
        pl.BlockSpec((None, 1, LANE), lambda g: (l, 0, jnp.maximum(g - 2, 0))),
    ]


def _pool_window_sums(u_ref, buf_a, buf_b, jj, s, rt):
    nrt = s // rt
    for t in range(nrt):
        buf_a[pl.ds(PAD + t * rt, rt), :] = u_ref[pl.ds(t * rt, rt), :]

    def stage(src, dst, sh):
        for t in range(nrt):
            r0 = PAD + t * rt
            dst[pl.ds(r0, rt), :] = src[pl.ds(r0, rt), :] + src[pl.ds(r0 - sh, rt), :]

    stage(buf_a, buf_b, 1)
    stage(buf_b, buf_a, 2)

    @pl.when(jj == 1)
    def _():
        stage(buf_a, buf_b, 4)
        stage(buf_b, buf_a, 8)


def _pool_counts(jj, lo, r0, rt):
    w = jnp.where(lo, jnp.where(jj == 0, 2.0, 8.0), jnp.where(jj == 0, 4.0, 16.0))
    pos1 = (lax.broadcasted_iota(jnp.int32, (rt, LANE), 0) + (r0 + 1)).astype(F32)
    return jnp.minimum(pos1, w)


def _conv_pool_fwd(p, mix, conv_w, wbd, pscale, l):
    s = p.shape[0]
    rt = min(256, s)
    nrt = s // rt

    def body(gb_ref, gc_ref, hin_ref, u_ref, cw_ref, wbd_ref, ps_ref, mix_in, o_ref, buf_a, buf_b):
        del mix_in
        g = pl.program_id(0)
        zpad = jnp.zeros((PAD, LANE), F32)
        buf_a[pl.ds(0, PAD), :] = zpad
        buf_b[pl.ds(0, PAD), :] = zpad

        @pl.when(g < 2)
        def _conv():
            for t in range(nrt):
                buf_a[pl.ds(PAD + t * rt, rt), :] = gc_ref[pl.ds(t * rt, rt), :] * hin_ref[pl.ds(t * rt, rt), :]
            w0, w1, w2 = cw_ref[0:1, :], cw_ref[1:2, :], cw_ref[2:3, :]
            for t in range(nrt):
                r0 = PAD + t * rt
                y = w0 * buf_a[pl.ds(r0 - 2, rt), :] + w1 * buf_a[pl.ds(r0 - 1, rt), :] + w2 * buf_a[pl.ds(r0, rt), :]
                o_ref[pl.ds(t * rt, rt), :] = (gb_ref[pl.ds(t * rt, rt), :] * y).astype(BF16)

        @pl.when(g >= 2)
        def _pool():
            jj = g - 2
            lo = _lo_mask()
            _pool_window_sums(u_ref, buf_a, buf_b, jj, s, rt)
            wb = wbd_ref[...]
            for t in range(nrt):
                r0 = PAD + t * rt
                wsum = jnp.where(lo, buf_b[pl.ds(r0, rt), :], buf_a[pl.ds(r0, rt), :])
                m = wsum / _pool_counts(jj, lo, t * rt, rt) - u_ref[pl.ds(t * rt, rt), :]
                o_ref[pl.ds(t * rt, rt), :] = (_nn(m.astype(BF16), wb) * ps_ref[...]).astype(BF16)

    return pl.pallas_call(
        body, name="conv_pool_fwd", grid=(4,),
        in_specs=_cp_in_specs(s, l) + [pl.BlockSpec(memory_space=pl.ANY)],
        out_specs=pl.BlockSpec((s, LANE), lambda g: (0, AW // LANE + g)),
        out_shape=SDS((s, D), BF16),
        scratch_shapes=[pltpu.VMEM((s + 2 * PAD, LANE), F32), pltpu.VMEM((s + 2 * PAD, LANE), F32)],
        input_output_aliases={7: 0},
        compiler_params=_cp(),
    )(p, p, p, p, conv_w, wbd, pscale, mix)


def _mlp_fwd(x, mix, wout, g2, w1_t, w2, l, xchg=None):
    s = x.shape[0]
    t = min(256, s)

    def body(x_ref, mix_ref, wo_ref, g_ref, w1_ref, w2_ref, xm_ref, a_ref, xo_ref):
        xm = x_ref[...] + _nn(mix_ref[...], wo_ref[...])
        xm_ref[...] = xm
        r = lax.rsqrt(jnp.mean(xm * xm, axis=-1, keepdims=True) + EPS)
        h2 = (xm * r * g_ref[...]).astype(BF16)
        a = _nt(h2, w1_ref[...])
        a_ref[...] = a.astype(BF16)
        f = jnp.square(jnp.maximum(a, 0.0)).astype(BF16)
        xo_ref[...] = xm + _nn(f, w2_ref[...])

    row = lambda c: pl.BlockSpec((t, c), lambda i: (i, 0))
    return _hosted_call(
        body, xchg, name="mlp_fwd", grid=(s // t,),
        in_specs=[row(D), row(D), _const((D, D)), _layer((1, D), l), _const((DFF, D)), _const((DFF, D))],
        out_specs=[row(D), row(DFF), row(D)],
        out_shape=[SDS((s, D), F32), SDS((s, DFF), BF16), SDS((s, D), F32)], args=[x, mix, wout, g2, w1_t, w2])


def _loss_grad(y, target):
    s = y.shape[0]
    t = min(512, s)

    def body(y_ref, t_ref, dy_ref, acc_ref):
        @pl.when(pl.program_id(0) == 0)
        def _():
            acc_ref[...] = jnp.zeros_like(acc_ref)
        e = y_ref[...] - t_ref[...]
        dy_ref[...] = e * (1.0 / D)
        acc_ref[...] += jnp.sum(e * e)

    row = pl.BlockSpec((t, D), lambda i: (i, 0))
    return pl.pallas_call(
        body, name="loss_grad", grid=(s // t,),
        in_specs=[row, row],
        out_specs=[row, pl.BlockSpec((8, LANE), lambda i: (0, 0))],
        out_shape=[SDS((s, D), F32), SDS((8, LANE), F32)],
        compiler_params=_cp(),
    )(y, target)


def _mlp_bwd(dxo, a, xm, g2, w1_t, w2, wout, l, xchg=None):
    s = dxo.shape[0]
    t = min(256, s)

    def body(dxo_ref, a_ref, xm_ref, g_ref, w1_ref, w2_ref, wo_ref,
             dxm_ref, dmix_ref, f_ref, da_ref, h2_ref, dxob_ref, dxmb_ref, dg_ref):
        @pl.when(pl.program_id(0) == 0)
        def _():
            dg_ref[...] = jnp.zeros_like(dg_ref)
        dxo = dxo_ref[...]
        dxob = dxo.astype(BF16)
        dxob_ref[...] = dxob
        ra = jnp.maximum(a_ref[...].astype(F32), 0.0)
        f_ref[...] = jnp.square(ra).astype(BF16)
        dab = (_nt(dxob, w2_ref[...]) * (2.0 * ra)).astype(BF16)
        da_ref[...] = dab
        dh2 = _nn(dab, w1_ref[...])
        xm = xm_ref[...]
        g = g_ref[...]
        r = lax.rsqrt(jnp.mean(xm * xm, axis=-1, keepdims=True) + EPS)
        h2_ref[...] = (xm * r * g).astype(BF16)
        dx_n, dgr = _rms_bwd(dh2, xm, g)
        dg_ref[...] += jnp.sum(dgr, axis=0, keepdims=True)
        dxm = dxo + dx_n
        dxm_ref[...] = dxm
        dxmb = dxm.astype(BF16)
        dxmb_ref[...] = dxmb
        dmix_ref[...] = _nt(dxmb, wo_ref[...])

    row = lambda c: pl.BlockSpec((t, c), lambda i: (i, 0))
    return _hosted_call(
        body, xchg, name="mlp_bwd", grid=(s // t,),
        in_specs=[row(D), row(DFF), row(D), _layer((1, D), l), _const((DFF, D)), _const((DFF, D)), _const((D, D))],
        out_specs=[row(D), row(D), row(DFF), row(DFF), row(D), row(D), row(D), pl.BlockSpec((1, D), lambda i: (0, 0))],
        out_shape=[SDS((s, D), F32), SDS((s, D), F32), SDS((s, DFF), BF16), SDS((s, DFF), BF16),
                   SDS((s, D), BF16), SDS((s, D), BF16), SDS((s, D), BF16), SDS((1, D), F32)],
        args=[dxo, a, xm, g2, w1_t, w2, wout])


def _attn_bwd(qkv, p, lse, o32, dmix, qg2, kg2, bias, l, xchg=None):
    s = p.shape[0]
    nq = s // TQ
    scale = HD ** -0.5

    def body(qs_ref, kb_ref, vb_ref, q_ref, k_ref, qg_ref, kg_ref, b_ref, lse_ref, o_ref, do_ref,
             dq_ref, dk_ref, dv_ref, db_ref, dqg_ref, dkg_ref,
             dk_acc, dv_acc, s_ref, dp_ref, ds_ref, pb_ref, dqn_ref, dl_ref):
        i = pl.program_id(1)
        kt = jnp.maximum(i - BAND // TQ, 0)
        ks = pl.multiple_of(kt * TQ, TQ)
        lo = _lo_mask()

        @pl.when(i == 0)
        def _():
            dk_acc[...] = jnp.zeros_like(dk_acc)
            dv_acc[...] = jnp.zeros_like(dv_acc)
            dqg_ref[...] = jnp.zeros_like(dqg_ref)
            dkg_ref[...] = jnp.zeros_like(dkg_ref)

        @pl.when(i < NVAR)
        def _():
            db_ref[...] = jnp.zeros_like(db_ref)

        qs = qs_ref[...]
        kwin = kb_ref[pl.ds(ks, WIN), :]
        vwin = vb_ref[pl.ds(ks, WIN), :]
        do = do_ref[...]
        dob = do.astype(BF16)
        dl_ref[...] = _half_sum(do * o_ref[...], lo)
        for half in range(2):
            m_ = lo if half == 0 else jnp.logical_not(lo)
            qa = jnp.where(m_, qs, jnp.zeros_like(qs))
            doa = jnp.where(m_, dob, jnp.zeros_like(dob))
            s_ref[half] = _nt(qa, kwin)
            dp_ref[half] = _nt(doa, vwin)
            for r0 in range(0, TQ, RB_SOFT):
                rows = pl.ds(r0, RB_SOFT)
                lse_h = lse_ref[rows, half * HD:half * HD + 1]
                pm = jnp.exp(s_ref[half, rows, :] + b_ref[half, rows, :] - lse_h)
                ds = pm * (dp_ref[half, rows, :] - dl_ref[rows, half * HD:half * HD + 1])
                db_ref[half, rows, :] += ds
                ds_ref[half, rows, :] = ds.astype(BF16)
                pb_ref[half, rows, :] = pm.astype(BF16)
            dsb = ds_ref[half]
            dq_h = _nn(dsb, kwin)
            if half == 0:
                dqn_ref[...] = dq_h
            else:
                dqn_ref[...] = jnp.where(lo, dqn_ref[...], dq_h)
            dk_t = _tn(qa, dsb)
            dv_t = _tn(doa, pb_ref[half])
            for t in range(WIN // TQ):
                dk_acc[kt + t] += dk_t[:, t * TQ:(t + 1) * TQ]
                dv_acc[kt + t] += dv_t[:, t * TQ:(t + 1) * TQ]
        qg, kg = qg_ref[...], kg_ref[...]
        xq, rq = _head_norm(q_ref[...], lo)
        dq, dqg_rows = _head_norm_bwd(dqn_ref[...] * scale, xq, rq, qg, lo)
        dq_ref[...] = dq.astype(BF16)
        dqg_ref[...] += jnp.sum(dqg_rows, axis=0, keepdims=True)

        @pl.when(i == nq - 1)
        def _():
            dkg = jnp.zeros((1, LANE), F32)
            for t in range(nq):
                rows = pl.ds(t * TQ, TQ)
                xk, rk = _head_norm(k_ref[rows, :], lo)
                dk, dkg_rows = _head_norm_bwd(dk_acc[t].T, xk, rk, kg, lo)
                dk_ref[rows, :] = dk.astype(BF16)
                dv_ref[rows, :] = dv_acc[t].T.astype(BF16)
                dkg = dkg + jnp.sum(dkg_rows, axis=0, keepdims=True)
            dkg_ref[...] = dkg

    tile = pl.BlockSpec((TQ, LANE), lambda j, i: (i, j))
    kcol = lambda c0: pl.BlockSpec((s, LANE), lambda j, i: (0, c0 + j))
    gain = pl.BlockSpec((None, 1, LANE), lambda j, i: (j, 0, 0))
    return _hosted_call(
        body, xchg, name="attn_bwd", grid=(NH // 2, nq),
        in_specs=[
            tile, kcol(AW // LANE), kcol(2 * AW // LANE), tile, kcol(AW // LANE),
            _layer((1, LANE), l), _layer((1, LANE), l),
            _bias_layer_spec(l), tile, tile, tile,
        ],
        out_specs=[tile, kcol(0), kcol(0), _bias_spec(), gain, gain],
        out_shape=[SDS((s, AW), BF16), SDS((s, AW), BF16), SDS((s, AW), BF16),
                   SDS((NVAR, NH, TQ, WIN), F32), SDS((NH // 2, 1, LANE), F32), SDS((NH // 2, 1, LANE), F32)],
        scratch_shapes=[pltpu.VMEM((nq, LANE, TQ), F32), pltpu.VMEM((nq, LANE, TQ), F32),
                        pltpu.VMEM((2, TQ, WIN), F32), pltpu.VMEM((2, TQ, WIN), F32),
                        pltpu.VMEM((2, TQ, WIN), BF16), pltpu.VMEM((2, TQ, WIN), BF16),
                        pltpu.VMEM((TQ, LANE), F32), pltpu.VMEM((TQ, LANE), F32)],
        args=[qkv, qkv, qkv, p, p, qg2, kg2, bias, lse, o32, dmix])


def _conv_pool_bwd(p, dmix, conv_w, wbd, pscale, l):
    s = p.shape[0]
    rt = min(256, s)
    nrt = s // rt

    def body(gb_ref, gc_ref, hin_ref, u_ref, cw_ref, wbd_ref, ps_ref, dy_ref,
             dgb_ref, dgc_ref, dhin_ref, du_ref, dcw_ref, dwbd_ref, dps_ref, buf_a, buf_b, buf_c, buf_d):
        g = pl.program_id(0)
        zpad = jnp.zeros((PAD, LANE), F32)
        for buf in (buf_a, buf_b, buf_c):
            buf[pl.ds(0, PAD), :] = zpad
            buf[pl.ds(PAD + s, PAD), :] = zpad

        @pl.when(g < 2)
        def _conv():
            for t in range(nrt):
                rows = pl.ds(t * rt, rt)
                buf_a[pl.ds(PAD + t * rt, rt), :] = gc_ref[rows, :] * hin_ref[rows, :]
                buf_b[pl.ds(PAD + t * rt, rt), :] = dy_ref[rows, :] * gb_ref[rows, :]
            w0, w1, w2 = cw_ref[0:1, :], cw_ref[1:2, :], cw_ref[2:3, :]
            d0 = jnp.zeros((1, LANE), F32)
            d1 = jnp.zeros((1, LANE), F32)
            d2 = jnp.zeros((1, LANE), F32)
            for t in range(nrt):
                rows = pl.ds(t * rt, rt)
                r0 = PAD + t * rt
                z2, z1, z0 = buf_a[pl.ds(r0 - 2, rt), :], buf_a[pl.ds(r0 - 1, rt), :], buf_a[pl.ds(r0, rt), :]
                y = w0 * z2 + w1 * z1 + w2 * z0
                dgb_ref[rows, :] = (dy_ref[rows, :] * y).astype(BF16)
                e0 = buf_b[pl.ds(r0, rt), :]
                d0 = d0 + jnp.sum(e0 * z2, axis=0, keepdims=True)
                d1 = d1 + jnp.sum(e0 * z1, axis=0, keepdims=True)
                d2 = d2 + jnp.sum(e0 * z0, axis=0, keepdims=True)
                dz = w2 * e0 + w1 * buf_b[pl.ds(r0 + 1, rt), :] + w0 * buf_b[pl.ds(r0 + 2, rt), :]
                dgc_ref[rows, :] = (dz * hin_ref[rows, :]).astype(BF16)
                dhin_ref[rows, :] = (dz * gc_ref[rows, :]).astype(BF16)
            dcw_ref[0:1, :] = d0
            dcw_ref[1:2, :] = d1
            dcw_ref[2:3, :] = d2

        @pl.when(g >= 2)
        def _pool():
            jj = g - 2
            lo = _lo_mask()
            _pool_window_sums(u_ref, buf_a, buf_b, jj, s, rt)
            wb = wbd_ref[...]
            ps = ps_ref[...]
            dps = jnp.zeros((1, LANE), F32)
            dwb = jnp.zeros((LANE, LANE), F32)
            for t in range(nrt):
                rows = pl.ds(t * rt, rt)
                r0 = PAD + t * rt
                cnt = _pool_counts(jj, lo, t * rt, rt)
                wsum = jnp.where(lo, buf_b[pl.ds(r0, rt), :], buf_a[pl.ds(r0, rt), :])
                mb = (wsum / cnt - u_ref[rows, :]).astype(BF16)
                dy = dy_ref[rows, :]
                dps = dps + jnp.sum(dy * _nn(mb, wb), axis=0, keepdims=True)
                dmp = (dy * ps).astype(BF16)
                dwb = dwb + _tn(mb, dmp)
                dm = _nt(dmp, wb)
                buf_d[rows, :] = dm
                buf_c[pl.ds(r0, rt), :] = dm / cnt
            dps_ref[...] = dps
            dwbd_ref[...] = dwb

            def stage(src, dst, sh):
                for t in range(nrt):
                    r0 = PAD + t * rt
                    dst[pl.ds(r0, rt), :] = src[pl.ds(r0, rt), :] + src[pl.ds(r0 + sh, rt), :]

            def finish(first, second):
                for t in range(nrt):
                    rows = pl.ds(t * rt, rt)
                    r0 = PAD + t * rt
                    fw = jnp.where(lo, first[pl.ds(r0, rt), :], second[pl.ds(r0, rt), :])
                    du_ref[rows, :] = (fw - buf_d[rows, :]).astype(BF16)

            stage(buf_c, buf_a, 1)
            stage(buf_a, buf_b, 2)

            @pl.when(jj == 0)
            def _():
                finish(buf_a, buf_b)

            @pl.when(jj == 1)
            def _():
                stage(buf_b, buf_c, 4)
                stage(buf_c, buf_a, 8)
                finish(buf_c, buf_a)

    cblk = pl.BlockSpec((s, LANE), lambda g: (0, jnp.minimum(g, 1)))
    pblk = pl.BlockSpec((s, LANE), lambda g: (0, jnp.maximum(g - 2, 0)))
    padded = pltpu.VMEM((s + 2 * PAD, LANE), F32)
    return pl.pallas_call(
        body, name="conv_pool_bwd", grid=(4,),
        in_specs=_cp_in_specs(s, l) + [pl.BlockSpec((s, LANE), lambda g: (0, AW // LANE + g))],
        out_specs=[cblk, cblk, cblk, pblk,
                   pl.BlockSpec((3, LANE), lambda g: (0, jnp.minimum(g, 1))),
                   pl.BlockSpec((None, LANE, LANE), lambda g: (jnp.maximum(g - 2, 0), 0, 0)),
                   pl.BlockSpec((1, LANE), lambda g: (0, jnp.maximum(g - 2, 0)))],
        out_shape=[SDS((s, CW), BF16), SDS((s, CW), BF16), SDS((s, CW), BF16), SDS((s, PWD), BF16),
                   SDS((3, CW), F32), SDS((2, LANE, LANE), F32), SDS((1, PWD), F32)],
        scratch_shapes=[padded, padded, padded, pltpu.VMEM((s, LANE), F32)],
        compiler_params=_cp(),
    )(p, p, p, p, conv_w, wbd, pscale, dmix)


def _in_proj_bwd(parts, x, dxm, g1, win_t, l, xchg=None):
    s = x.shape[0]
    t = min(256, s)
    widths = [a.shape[1] for a in parts]
    offs = [int(o) for o in np.cumsum([0] + widths[:-1])]
    n = len(parts)

    def body(*refs):
        part_refs = refs[:n]
        x_ref, dxm_ref, g_ref, w_ref, dx_ref, dp_ref, dg_ref = refs[n:]

        @pl.when(pl.program_id(0) == 0)
        def _():
            dg_ref[...] = jnp.zeros_like(dg_ref)
        for r, o, w in zip(part_refs, offs, widths):
            dp_ref[:, o:o + w] = r[...]
        dh = _nn(dp_ref[...], w_ref[...])
        dx_n, dgr = _rms_bwd(dh, x_ref[...], g_ref[...])
        dg_ref[...] += jnp.sum(dgr, axis=0, keepdims=True)
        dx_ref[...] = dxm_ref[...] + dx_n

    row = lambda c: pl.BlockSpec((t, c), lambda i: (i, 0))
    return _hosted_call(
        body, xchg, name="in_proj_bwd", grid=(s // t,),
        in_specs=[row(w) for w in widths] + [row(D), row(D), _layer((1, D), l), _const((DIN, D))],
        out_specs=[row(D), row(DIN), pl.BlockSpec((1, D), lambda i: (0, 0))],
        out_shape=[SDS((s, D), F32), SDS((s, DIN), BF16), SDS((1, D), F32)], args=[*parts, x, dxm, g1, win_t])


def _wgrad(a, b, tag, xchg=None):
    s, m = a.shape
    mb = 512

    def body(a_ref, b_ref, o_ref):
        o_ref[...] = _tn(a_ref[...], b_ref[...]).astype(BF16)

    (out,), got = _hosted_call(
        body, xchg, name=f"wgrad_{tag}", grid=(m // mb,),
        in_specs=[pl.BlockSpec((s, mb), lambda mi: (0, mi)), _const((s, D))],
        out_specs=[pl.BlockSpec((mb, D), lambda mi: (mi, 0))],
        out_shape=[SDS((m, D), BF16)], args=[a, b])
    return out, got


def _bias_tables(gvec, xchg=None):
    def body(g_ref, o_ref):
        qc = lax.broadcasted_iota(jnp.int32, (TQ, WIN), 0) // CHUNK
        kc = lax.broadcasted_iota(jnp.int32, (TQ, WIN), 1) // CHUNK
        for var in range(NVAR):
            vec = jnp.broadcast_to(g_ref[:, var * TQ:var * TQ + NTOE], (TQ, NTOE))
            toe = pltpu.roll(vec, NTOE - TQ + 1, 1, stride=1, stride_axis=0)[:, :WIN]
            rel = (BAND - var * TQ) // CHUNK + qc - kc
            o_ref[var] = jnp.where((rel >= 0) & (rel <= N_PREV), toe, NEG)

    (out,), got = _hosted_call(
        body, xchg, name="bias_tables", grid=(L, NH),
        in_specs=[pl.BlockSpec((None, None, 1, NG), lambda l, h: (l, h, 0, 0))],
        out_specs=[pl.BlockSpec((None, NVAR, None, TQ, WIN), lambda l, h: (l, 0, h, 0, 0))],
        out_shape=[SDS((L, NVAR, NH, TQ, WIN), F32)], args=[gvec])
    return out, got


def _bias_tables_grad(dbias, l):
    nb = NTOE // LANE
    wb = WIN // LANE

    def body(d_ref, o_ref):
        ii = lax.broadcasted_iota(jnp.int32, (LANE, LANE), 0)
        jj = lax.broadcasted_iota(jnp.int32, (LANE, LANE), 1)
        flip = jnp.where(ii + jj == LANE - 1, 1.0, 0.0).astype(BF16)
        o_ref[...] = jnp.zeros_like(o_ref)
        for var in range(NVAR):
            blocks = []
            for b in range(nb):
                src = nb - 1 - b
                if src >= wb:
                    blocks.append(jnp.zeros((TQ, LANE), F32))
                    continue
                xv = d_ref[var, :, src * LANE:(src + 1) * LANE]
                hi = xv.astype(BF16)
                lo = (xv - hi.astype(F32)).astype(BF16)
                blocks.append(_nn(hi, flip) + _nn(lo, flip))
            rev = jnp.concatenate(blocks, axis=1)
            skew = pltpu.roll(rev, NTOE - TQ + 1, 1, stride=1, stride_axis=0)
            off = NG - NTOE - var * TQ
            o_ref[:, off:off + NTOE] += jnp.sum(skew, axis=0, keepdims=True)

    return pl.pallas_call(
        body, name=f"bias_tables_grad_l{l}", grid=(NH,),
        in_specs=[pl.BlockSpec((NVAR, None, TQ, WIN), lambda h: (0, h, 0, 0))],
        out_specs=pl.BlockSpec((None, 1, NG), lambda h: (h, 0, 0)),
        out_shape=SDS((NH, 1, NG), F32),
        compiler_params=_cp(),
    )(dbias)


_SIBLING = (0, 0, 1)
_CHIPS = [(1, 0, 0), (0, 1, 0), (1, 1, 0)]
_MASKS = [_SIBLING] + _CHIPS + [(1, 0, 1), (0, 1, 1), (1, 1, 1)]


def _position():
    return lax.axis_index("x"), lax.axis_index("y"), lax.axis_index("c")


def _peer(pos, mask):
    return tuple(1 - a if f else a for a, f in zip(pos, mask))


def _index(pos):
    return 4 * pos[0] + 2 * pos[1] + pos[2]


def _exchange_phases(items, src, dst, sems):
    send_sems, recv_sems, local_sems = sems
    me = _position()
    sib = _peer(me, _SIBLING)

    def remote(s_ref, d_ref, pi, n, to):
        return pltpu.make_async_remote_copy(
            src_ref=s_ref, dst_ref=d_ref, send_sem=send_sems.at[pi, n], recv_sem=recv_sems.at[pi, n],
            device_id=to, device_id_type=MESH_ID)

    def parts(n):
        it = items[n]
        r = SHARD_ROWS[it[1]]
        block = lambda ref, pos: ref.at[pl.ds(_index(pos) * r, r), :]
        if it[0] == "gather":
            own = src[n].at[it[2]]
            local = pltpu.make_async_copy(own, block(dst[n], me), local_sems.at[n])
            sends = [remote(own, block(dst[n], me), pi, n, _peer(me, m)) for pi, m in enumerate([_SIBLING] + _CHIPS)]
            hops = [(remote(block(dst[n], _peer(me, m)), block(dst[n], _peer(me, m)), 1 + j, n, _peer(me, m)),
                     remote(block(dst[n], _peer(me, m)), block(dst[n], _peer(me, m)), 4 + j, n, sib))
                    for j, m in enumerate(_CHIPS)]
            lands = [remote(own, block(dst[n], sib), 0, n, sib)]
            lands += [remote(own, block(dst[n], _peer(sib, m)), 4 + j, n, sib) for j, m in enumerate(_CHIPS)]
        else:
            local = pltpu.make_async_copy(block(src[n], me), dst[n].at[_index(me)], local_sems.at[n])
            sends = [remote(block(src[n], _peer(me, m)), dst[n].at[_index(me)], pi, n, _peer(me, m))
                     for pi, m in enumerate(_MASKS)]
            hops = []
            lands = [remote(block(src[n], me), dst[n].at[_index(_peer(me, m))], pi, n, _peer(me, m))
                     for pi, m in enumerate(_MASKS)]
        return local, sends, hops, lands

    def start():
        for n in range(len(items)):
            local, sends, _, _ = parts(n)
            local.start()
            for cp in sends:
                cp.start()

    def relay():
        for n in range(len(items)):
            for arrived, onward in parts(n)[2]:
                arrived.wait_recv()
                onward.start()

    def finish():
        for n in range(len(items)):
            local, sends, hops, lands = parts(n)
            for cp in lands:
                cp.wait_recv()
            for cp in sends + [onward for _, onward in hops]:
                cp.wait_send()
            local.wait()

    return start, relay, finish


def _hosted_call(body, xchg, *, name, grid, in_specs, out_specs, out_shape, args, scratch_shapes=(), relay_at=0.8):
    if not xchg:
        outs = pl.pallas_call(
            body, name=name, grid=grid, in_specs=list(in_specs), out_specs=list(out_specs),
            out_shape=list(out_shape), scratch_shapes=list(scratch_shapes), compiler_params=_cp())(*args)
        return outs, []
    items = [it for it, _ in xchg]
    n_in, n_out, n_scr, nit = len(args), len(out_shape), len(scratch_shapes), len(items)
    hbm = pl.BlockSpec(memory_space=pl.ANY)
    steps = int(np.prod(grid))
    relay_step = min(int(relay_at * steps), steps - 1)

    def dst_shape(it):
        r = SHARD_ROWS[it[1]]
        return SDS((NDEV * r, D) if it[0] == "gather" else (NDEV, r, D), BF16)

    def wrapped(*refs):
        ins = refs[:n_in]
        src = refs[n_in:n_in + nit]
        outs = refs[n_in + nit:n_in + nit + n_out]
        dst = refs[n_in + nit + n_out:n_in + 2 * nit + n_out]
        scratch = refs[n_in + 2 * nit + n_out:n_in + 2 * nit + n_out + n_scr]
        start, relay, finish = _exchange_phases(items, src, dst, refs[n_in + 2 * nit + n_out + n_scr:])
        step = 0
        for d, g in enumerate(grid):
            step = step * g + pl.program_id(d)
        pl.when(step == 0)(start)
        body(*ins, *outs, *scratch)
        pl.when(step == relay_step)(relay)
        pl.when(step == steps - 1)(finish)

    npeer = len(_MASKS)
    res = pl.pallas_call(
        wrapped, name=name, grid=grid,
        in_specs=list(in_specs) + [hbm] * nit,
        out_specs=list(out_specs) + [hbm] * nit,
        out_shape=list(out_shape) + [dst_shape(it) for it in items],
        scratch_shapes=list(scratch_shapes) + [
            pltpu.SemaphoreType.DMA((npeer, nit)), pltpu.SemaphoreType.DMA((npeer, nit)), pltpu.SemaphoreType.DMA((nit,))],
        compiler_params=_cp(),
    )(*args, *[a for _, a in xchg])
    return list(res[:n_out]), list(res[n_out:])


def _sum_slots(slots, xchg=None):
    _, r, _ = slots[0].shape
    rt = 64

    def body(*refs):
        o_ref = refs[L]
        for l in range(L):
            acc = refs[l][0].astype(F32)
            for d in range(1, NDEV):
                acc = acc + refs[l][d].astype(F32)
            o_ref[l] = acc

    (out,), got = _hosted_call(
        body, xchg, name=f"sum_slots_r{r}" + ("_x" if xchg else ""), grid=(r // rt,),
        in_specs=[pl.BlockSpec((NDEV, rt, D), lambda i: (0, i, 0))] * L,
        out_specs=[pl.BlockSpec((L, rt, D), lambda i: (0, i, 0))],
        out_shape=[SDS((L, r, D), F32)], args=list(slots))
    return out, got


def _exchange_small(v, reduce):
    rows = v.shape[0]

    def body(v_ref, o_ref, *scratch):
        if reduce:
            slots, send_sems, recv_sems = scratch
        else:
            slots = o_ref
            send_sems, recv_sems = scratch
        me = _position()
        slots[_index(me)] = v_ref[...]
        sends = []
        for pi, mask in enumerate(_MASKS):
            cp = pltpu.make_async_remote_copy(
                src_ref=v_ref, dst_ref=slots.at[_index(me)], send_sem=send_sems.at[pi], recv_sem=recv_sems.at[pi],
                device_id=_peer(me, mask), device_id_type=MESH_ID)
            cp.start()
            sends.append(cp)
        for pi, mask in enumerate(_MASKS):
            peer = _peer(me, mask)
            pltpu.make_async_remote_copy(
                src_ref=v_ref, dst_ref=slots.at[_index(peer)], send_sem=send_sems.at[pi], recv_sem=recv_sems.at[pi],
                device_id=peer, device_id_type=MESH_ID).wait_recv()
        for cp in sends:
            cp.wait_send()
        if reduce:
            acc = slots[0]
            for d in range(1, NDEV):
                acc = acc + slots[d]
            o_ref[...] = acc

    vm = pl.BlockSpec(memory_space=pltpu.VMEM)
    sems = [pltpu.SemaphoreType.DMA((len(_MASKS),)), pltpu.SemaphoreType.DMA((len(_MASKS),))]
    return pl.pallas_call(
        body, name="reduce_small" if reduce else "gather_small",
        in_specs=[vm], out_specs=vm,
        out_shape=SDS((rows, LANE) if reduce else (NDEV, rows, LANE), F32),
        scratch_shapes=([pltpu.VMEM((NDEV, rows, LANE), F32)] if reduce else []) + sems,
        compiler_params=_cp(),
    )(v)


def _adamw_update(w_ref, g_ref, m_ref, v_ref, d_ref, nm_ref, nv_ref):
    gv = g_ref[...]
    mn = B1 * m_ref[...] + (1.0 - B1) * gv
    vn = B2 * v_ref[...] + (1.0 - B2) * jnp.square(gv)
    nm_ref[...] = mn
    nv_ref[...] = vn
    m_hat = mn / (1.0 - B1 ** STEP)
    v_hat = vn / (1.0 - B2 ** STEP)
    d_ref[...] = -LR * (m_hat / (jnp.sqrt(v_hat) + AEPS) + WD * w_ref[...])


def _adamw_small(ws, gs, ms, vs):
    n = len(ws)

    def body(*refs):
        for i in range(n):
            _adamw_update(*[refs[j * n + i] for j in range(7)])

    vm = pl.BlockSpec(memory_space=pltpu.VMEM)
    res = pl.pallas_call(
        body, name="adamw_small", in_specs=[vm] * (4 * n), out_specs=[vm] * (3 * n),
        out_shape=[SDS(w.shape, F32) for _ in range(3) for w in ws],
        compiler_params=_cp(),
    )(*ws, *gs, *ms, *vs)
    return res[:n], res[n:2 * n], res[2 * n:]


def _adamw(w, g, m, v):
    rows, cols = w.shape
    t = rows
    for cand in (512, 256, 128, 64, 32, 16, 8):
        if rows % cand == 0:
            t = cand
            break

    def body(*refs):
        _adamw_update(*refs)

    blk = pl.BlockSpec((t, cols), lambda i: (i, 0))
    return pl.pallas_call(
        body, name=f"adamw_{rows}x{cols}", grid=(rows // t,),
        in_specs=[blk] * 4, out_specs=[blk] * 3,
        out_shape=[SDS((rows, cols), F32)] * 3,
        compiler_params=_cp(),
    )(w, g, m, v)


_DIST0 = BAND + TQ - 1
_N_FAR = _DIST0 - REL_CLIP + 1
_N_NEAR = NG - _N_FAR - (2 * REL_CLIP - 1)


def _bias_vector(rel_bias):
    far = jnp.broadcast_to(rel_bias[..., -1:], (L, NH, _N_FAR))
    near = jnp.broadcast_to(rel_bias[..., :1], (L, NH, _N_NEAR))
    return jnp.concatenate([far, lax.rev(rel_bias[..., 1:-1], (2,)), near], axis=2)[:, :, None, :]


def _bias_vector_grad(dgr):
    first = jnp.sum(dgr[..., :_N_NEAR], axis=-1, keepdims=True)
    last = jnp.sum(dgr[..., NG - _N_FAR:], axis=-1, keepdims=True)
    return jnp.concatenate([first, dgr[..., _N_NEAR:NG - _N_FAR], last], axis=-1)


def _pool_blockdiag(pool_w):
    eye = jnp.eye(2, dtype=F32)
    pw = pool_w.reshape(L, 2, 2, HD, HD)
    return jnp.einsum("ljaik,ab->ljaibk", pw, eye).reshape(L, 2, LANE, LANE)


def _pool_blockdiag_grad(dwbd):
    d = dwbd.reshape(L, 2, 2, HD, 2, HD)
    return jnp.stack([d[:, :, 0, :, 0, :], d[:, :, 1, :, 1, :]], axis=2).reshape(L, 4, HD, HD)


def _pack(arrays, rows):
    flat = jnp.concatenate([a.reshape(-1).astype(F32) for a in arrays])
    return jnp.pad(flat, (0, rows * LANE - flat.shape[0])).reshape(rows, LANE)


def _unpack(packed, shapes):
    flat = packed.reshape(-1)
    out, o = [], 0
    for shp in shapes:
        n = int(np.prod(shp))
        out.append(flat[o:o + n].reshape(shp))
        o += n
    return out


def _rows_for(shapes):
    n = sum(int(np.prod(s)) for s in shapes)
    return -(-n // (8 * LANE)) * 8


def _grads(x, target, small_w, shards):
    g1, qg, kg, rb, cw, pw, ps, g2 = small_w
    g1 = g1.reshape(L, 1, D)
    g2 = g2.reshape(L, 1, D)
    qg2 = jnp.tile(qg, (1, 2)).reshape(L, 1, LANE)
    kg2 = jnp.tile(kg, (1, 2)).reshape(L, 1, LANE)
    ps3 = ps.reshape(L, 1, PWD)
    wbd = _pool_blockdiag(pw).astype(BF16)

    def gather(*kl):
        return [(("gather", k, l), shards[k]) for k, l in kl if l < L]

    full = {}

    def arrived(got, *kl):
        full.update(zip([x for x in kl if x[1] < L], got))

    bias, got = _bias_tables(_bias_vector(rb), gather((0, 0)))
    arrived(got, (0, 0))
    saved = []
    h = x
    for l in range(L):
        kl = ((1, 0), (2, 0)) if l == 0 else ((1, l),)
        (p, h_b, qkv), got = _in_proj(h, g1, full[0, l], qg2, kg2, l, gather(*kl))
        arrived(got, *kl)
        kl = ((3, l),)
        (mix, lse, o32), got = _attn_fwd(qkv, bias, l, gather(*kl))
        arrived(got, *kl)
        mix = _conv_pool_fwd(p, mix, cw, wbd, ps3, l)
        kl = ((0, l + 1), (2, l + 1))
        (xm, a, xo), got = _mlp_fwd(h, mix, full[1, l], g2, full[2, l], full[3, l], l, gather(*kl))
        arrived(got, *kl)
        saved.append((h, h_b, p, qkv, mix, lse, o32, xm, a))
        h = xo
    dx, sq = _loss_grad(h, target)

    grads = {}
    slots = {}

    def scatter(*kl):
        return [(("scatter", k), grads[k, l]) for k, l in kl if l < L]

    def left(got, *kl):
        slots.update(zip([x for x in kl if x[1] < L], got))

    per_layer = [None] * L
    for l in reversed(range(L)):
        x_in, h_b, p, qkv, mix, lse, o32, xm, a = saved[l]
        (dxm, dmix, f_b, da_b, h2_b, dxo_b, dxm_b, dg2), got = _mlp_bwd(
            dx, a, xm, g2, full[2, l], full[3, l], full[1, l], l, scatter((3, l + 1)))
        left(got, (3, l + 1))
        grads[1, l], _ = _wgrad(mix, dxm_b, f"w_out_l{l}")
        grads[2, l], got = _wgrad(da_b, h2_b, f"w_mlp1_l{l}", scatter((1, l)))
        left(got, (1, l))
        kl = ((2, 0),) if l == 0 else ()
        grads[3, l], got = _wgrad(f_b, dxo_b, f"w_mlp2_l{l}", scatter(*kl))
        left(got, *kl)
        kl = ((3, 0), (0, 1)) if l == 0 else ((2, l),)
        (dq, dk, dv, dbias, dqg, dkg), got = _attn_bwd(qkv, p, lse, o32, dmix, qg2, kg2, bias, l, scatter(*kl))
        left(got, *kl)
        dgb, dgc, dhin, du, dcw, dwbd, dps = _conv_pool_bwd(p, dmix, cw, wbd, ps3, l)
        (dx, dp_b, dg1), got = _in_proj_bwd(
            [dq, dk, dv, dgb, dgc, dhin, du], x_in, dxm, g1, full[0, l], l, scatter((0, l + 1)) if l else None)
        left(got, (0, l + 1))
        grads[0, l], _ = _wgrad(dp_b, h_b, f"w_in_l{l}")
        per_layer[l] = (dg1, dg2, dqg, dkg, _bias_tables_grad(dbias, l), dcw, dwbd, dps)
    g_w1_t, got = _sum_slots([slots[2, l] for l in range(L)], scatter((0, 0)))
    left(got, (0, 0))
    sums = [_sum_slots([slots[k, l] for l in range(L)])[0] for k in (0, 1)] + [
        g_w1_t, _sum_slots([slots[3, l] for l in range(L)])[0]]

    st = [jnp.stack([per_layer[l][k] for l in range(L)]) for k in range(8)]
    small = dict(
        g1=st[0].reshape(L, D), g2=st[1].reshape(L, D),
        qg=st[2].reshape(L, NH, HD).sum(1), kg=st[3].reshape(L, NH, HD).sum(1),
        rb=_bias_vector_grad(st[4].reshape(L, NH, NG)), cw=st[5], pw=_pool_blockdiag_grad(st[6]),
        ps=st[7].reshape(L, PWD))
    return sq, dx, sums, small


def kernel(x, norm1_g, w_in, q_norm_g, k_norm_g, rel_bias, conv_w, pool_w, pool_scale, w_out, norm2_g, w_mlp1, w_mlp2, loss_target, m_norm1_g, m_w_in, m_q_norm_g, m_k_norm_g, m_rel_bias, m_conv_w, m_pool_w, m_pool_scale, m_w_out, m_norm2_g, m_w_mlp1, m_w_mlp2, v_norm1_g, v_w_in, v_q_norm_g, v_k_norm_g, v_rel_bias, v_conv_w, v_pool_w, v_pool_scale, v_w_out, v_norm2_g, v_w_mlp1, v_w_mlp2):
    me = _index(_position())
    cshard = CW // NDEV

    shards = [jnp.swapaxes(w_in, 1, 2).astype(BF16), w_out.astype(BF16),
              jnp.swapaxes(w_mlp1, 1, 2).astype(BF16), w_mlp2.astype(BF16)]
    cw_all = _exchange_small(_pack([conv_w], 8), reduce=False)
    cw_full = jnp.concatenate(
        [cw_all[d].reshape(-1)[:L * 3 * cshard].reshape(L, 3, cshard) for d in range(NDEV)], axis=2)

    small_w = (norm1_g, q_norm_g, k_norm_g, rel_bias, cw_full, pool_w, pool_scale, norm2_g)
    sq, grad_x, (g_win_t, g_wout, g_w1_t, g_w2), small = _grads(x[0], loss_target[0], small_w, shards)
    g_w_in = jnp.swapaxes(g_win_t, 1, 2)
    g_w_mlp1 = jnp.swapaxes(g_w1_t, 1, 2)

    names = ("g1", "qg", "kg", "rb", "cw", "pw", "ps", "g2")
    gshapes = [(L, D), (L, HD), (L, HD), (L, NH, 2 * REL_CLIP + 1), (L, 3, CW), (L, 4, HD, HD), (L, PWD), (L, D)]
    garrs = [small[n] for n in names]
    rows = _rows_for(gshapes + [(1,)])
    total = _exchange_small(_pack(garrs + [sq[0, :1]], rows), reduce=True)
    g_g1, g_qg, g_kg, g_rb, g_cw_full, g_pw, g_ps, g_g2, sq_sum = _unpack(total, gshapes + [(1,)])
    loss = (0.5 / D) * sq_sum[0]
    g_cw = lax.dynamic_slice_in_dim(g_cw_full, me * cshard, cshard, axis=2)

    def big(w, g, m, v):
        shp = w.shape
        r = lambda a: a.reshape(-1, shp[-1])
        return [o.reshape(shp) for o in _adamw(r(w), r(g), r(m), r(v))]

    up_in = big(w_in, g_w_in, m_w_in, v_w_in)
    up_out = big(w_out, g_wout, m_w_out, v_w_out)
    up_1 = big(w_mlp1, g_w_mlp1, m_w_mlp1, v_w_mlp1)
    up_2 = big(w_mlp2, g_w2, m_w_mlp2, v_w_mlp2)

    sw = [norm1_g, q_norm_g, k_norm_g, rel_bias, conv_w, pool_w, pool_scale, norm2_g]
    sg = [g_g1, g_qg, g_kg, g_rb, g_cw, g_pw, g_ps, g_g2]
    sm = [m_norm1_g, m_q_norm_g, m_k_norm_g, m_rel_bias, m_conv_w, m_pool_w, m_pool_scale, m_norm2_g]
    sv = [v_norm1_g, v_q_norm_g, v_k_norm_g, v_rel_bias, v_conv_w, v_pool_w, v_pool_scale, v_norm2_g]
    s_delta, s_m, s_v = _adamw_small(sw, sg, sm, sv)

    def order(small_list, in_, out_, m1, m2):
        g1_, qg_, kg_, rb_, cw_, pw_, ps_, g2_ = small_list
        return [g1_, in_, qg_, kg_, rb_, cw_, pw_, ps_, out_, g2_, m1, m2]

    grads = order(sg, g_w_in, g_wout, g_w_mlp1, g_w2)
    deltas = order(s_delta, up_in[0], up_out[0], up_1[0], up_2[0])
    new_m = order(s_m, up_in[1], up_out[1], up_1[1], up_2[1])
    new_v = order(s_v, up_in[2], up_out[2], up_1[2], up_2[2])
    return (loss, grad_x[None], *grads, *deltas, *new_m, *new_v)
```

```python
import numpy as np
import jax
import jax.numpy as jnp
from jax import lax
from jax.experimental import pallas as pl
from jax.experimental.pallas import tpu as pltpu

F32 = jnp.float32
BF16 = jnp.bfloat16
SDS = jax.ShapeDtypeStruct
MESH_ID = pl.DeviceIdType.MESH

D = 1024
L = 4
CHUNK = 64
N_PREV = 8
HD = 64
NH = 8
AW = 512
CW = 256
PWD = 256
DIN = 2560
DFF = 4096
EPS = 1e-6
NEG = -1e30
REL_CLIP = 128
POOL_WINDOWS = (2, 4, 8, 16)
LR, B1, B2, AEPS, WD, STEP = 0.001, 0.9, 0.999, 1e-08, 0.01, 10

NDEV = 8
LANE = 128
BAND = N_PREV * CHUNK
TQ = 256
WIN = TQ + BAND
NVAR = BAND // TQ + 1
NTOE = -(-(WIN + TQ - 1) // LANE) * LANE
NG = (NVAR - 1) * TQ + NTOE
PAD = 16
RB_NORM = 64
RB_SOFT = 16
VMEM_LIMIT = 56 * 1024 * 1024
SHARD_ROWS = (DIN // NDEV, D // NDEV, DFF // NDEV, DFF // NDEV)


def _cp(**kw):
    return pltpu.CompilerParams(vmem_limit_bytes=VMEM_LIMIT, **kw)


def _nn(a, b):
    return jnp.dot(a, b, preferred_element_type=F32)


def _nt(a, b):
    return lax.dot_general(a, b, (((1,), (1,)), ((), ())), preferred_element_type=F32)


def _tn(a, b):
    return lax.dot_general(a, b, (((0,), (0,)), ((), ())), preferred_element_type=F32)


def _const(shape):
    n = len(shape)
    return pl.BlockSpec(shape, lambda *_: (0,) * n, pipeline_mode=pl.Buffered(1))


def _layer(shape, l):
    n = len(shape)
    return pl.BlockSpec((None,) + tuple(shape), lambda *_: (l,) + (0,) * n, pipeline_mode=pl.Buffered(1))


def _lo_mask():
    return lax.broadcasted_iota(jnp.int32, (1, LANE), 1) < HD


def _half_sum(t, lo):
    s_lo = jnp.sum(jnp.where(lo, t, 0.0), axis=-1, keepdims=True)
    s_hi = jnp.sum(jnp.where(lo, 0.0, t), axis=-1, keepdims=True)
    return jnp.where(lo, s_lo, s_hi)


def _head_norm(x, lo):
    r = lax.rsqrt(_half_sum(x * x, lo) * (1.0 / HD) + EPS)
    return x * r, r


def _head_norm_bwd(dy, xn, r, g, lo):
    dxn = dy * g
    mu = _half_sum(dxn * xn, lo) * (1.0 / HD)
    return r * (dxn - xn * mu), dy * xn


def _rms_bwd(dy, x, g):
    r = lax.rsqrt(jnp.mean(x * x, axis=-1, keepdims=True) + EPS)
    xn = x * r
    dxn = dy * g
    mu = jnp.mean(dxn * xn, axis=-1, keepdims=True)
    return r * (dxn - xn * mu), dy * xn


def _in_proj(x, g1, win_t, qg2, kg2, l, xchg=None):
    s = x.shape[0]
    t = min(512, s)
    nblk = AW // LANE

    def body(x_ref, g_ref, w_ref, qg_ref, kg_ref, p_ref, h_ref, qkv_ref):
        xv = x_ref[...]
        r = lax.rsqrt(jnp.mean(xv * xv, axis=-1, keepdims=True) + EPS)
        h = (xv * r * g_ref[...]).astype(BF16)
        h_ref[...] = h
        p_ref[...] = _nt(h, w_ref[...])
        lo = _lo_mask()
        gains = (qg_ref[...] * (HD ** -0.5), kg_ref[...])
        for r0 in range(0, t, RB_NORM):
            rows = pl.ds(r0, RB_NORM)
            for c in range(3 * nblk):
                cols = pl.ds(c * LANE, LANE)
                v = p_ref[rows, cols]
                if c < 2 * nblk:
                    v = _head_norm(v, lo)[0] * gains[c // nblk]
                qkv_ref[rows, cols] = v.astype(BF16)

    row = lambda c: pl.BlockSpec((t, c), lambda i: (i, 0))
    return _hosted_call(
        body, xchg, name="in_proj", grid=(s // t,),
        in_specs=[row(D), _layer((1, D), l), _const((DIN, D)), _layer((1, LANE), l), _layer((1, LANE), l)],
        out_specs=[row(DIN), row(D), row(3 * AW)],
        out_shape=[SDS((s, DIN), F32), SDS((s, D), BF16), SDS((s, 3 * AW), BF16)], args=[x, g1, win_t, qg2, kg2])


def _bias_spec():
    return pl.BlockSpec((None, 2, TQ, WIN), lambda j, i: (jnp.maximum(NVAR - 1 - i, 0), j, 0, 0))


def _bias_layer_spec(l):
    return pl.BlockSpec((None, None, 2, TQ, WIN), lambda j, i: (l, jnp.maximum(NVAR - 1 - i, 0), j, 0, 0))


def _attn_fwd(qkv, bias, l, xchg=None, relay_at=0.8):
    s = qkv.shape[0]
    nq = s // TQ

    def body(q_ref, k_ref, v_ref, b_ref, o_ref, lse_ref, o32_ref, s_ref, p_ref, m_ref, den_ref, o0_ref):
        i = pl.program_id(1)
        ks = pl.multiple_of(jnp.maximum(i * TQ - BAND, 0), TQ)
        lo = _lo_mask()
        q = q_ref[...]
        kwin = k_ref[pl.ds(ks, WIN), :]
        vwin = v_ref[pl.ds(ks, WIN), :]
        for half in range(2):
            m_ = lo if half == 0 else jnp.logical_not(lo)
            s_ref[half] = _nt(jnp.where(m_, q, jnp.zeros_like(q)), kwin)
            for r0 in range(0, TQ, RB_SOFT):
                rows = pl.ds(r0, RB_SOFT)
                mx = jnp.max(s_ref[half, rows, :] + b_ref[half, rows, :], axis=-1, keepdims=True)
                m_ref[rows, :] = jnp.broadcast_to(mx, (RB_SOFT, LANE))
            for r0 in range(0, TQ, RB_SOFT):
                rows = pl.ds(r0, RB_SOFT)
                mx = m_ref[rows, 0:1]
                e = jnp.exp(s_ref[half, rows, :] + b_ref[half, rows, :] - mx)
                p_ref[half, rows, :] = e.astype(BF16)
                den = jnp.sum(e, axis=-1, keepdims=True)
                den_ref[rows, :] = jnp.broadcast_to(den, (RB_SOFT, LANE))
                lse = mx + jnp.log(den)
                if half == 0:
                    lse_ref[rows, :] = jnp.broadcast_to(lse, (RB_SOFT, LANE))
                else:
                    lse_ref[rows, :] = jnp.where(lo, lse_ref[rows, :], lse)
            o = _nn(p_ref[half], vwin) * (1.0 / den_ref[...])
            if half == 0:
                o0_ref[...] = o
            else:
                o = jnp.where(lo, o0_ref[...], o)
                o32_ref[...] = o
                o_ref[...] = o.astype(BF16)

    tile = pl.BlockSpec((TQ, LANE), lambda j, i: (i, j))
    stat = pltpu.VMEM((TQ, LANE), F32)
    return _hosted_call(
        body, xchg, name="attn_fwd", grid=(NH // 2, nq),
        in_specs=[
            tile,
            pl.BlockSpec((s, LANE), lambda j, i: (0, AW // LANE + j)),
            pl.BlockSpec((s, LANE), lambda j, i: (0, 2 * AW // LANE + j)),
            _bias_layer_spec(l),
        ],
        out_specs=[tile, tile, tile],
        out_shape=[SDS((s, D), BF16), SDS((s, AW), F32), SDS((s, AW), F32)], args=[qkv, qkv, qkv, bias],
        scratch_shapes=[pltpu.VMEM((2, TQ, WIN), F32), pltpu.VMEM((2, TQ, WIN), BF16), stat, stat, stat],
        relay_at=relay_at)


_C0 = 3 * AW // LANE


def _cp_in_specs(s, l):
    blk = lambda f: pl.BlockSpec((s, LANE), f)
    return [
        blk(lambda g: (0, _C0 + jnp.minimum(g, 1))),
        blk(lambda g: (0, _C0 + 2 + jnp.minimum(g, 1))),
        blk(lambda g: (0, _C0 + 4 + jnp.minimum(g, 1))),
        blk(lambda g: (0, _C0 + 6 + jnp.maximum(g - 2, 0))),
        pl.BlockSpec((None, 3, LANE), lambda g: (l, 0, jnp.minimum(g, 1))),
        pl.BlockSpec((None, None, LANE, LANE), lambda g: (l, jnp.maximum(g - 2, 0), 0, 0)),
        pl.BlockSpec((None, 1, LANE), lambda g: (l, 0, jnp.maximum(g - 2, 0))),
    ]


def _pool_window_sums(u_ref, buf_a, buf_b, jj, s, rt):
    nrt = s // rt
    for t in range(nrt):
        buf_a[pl.ds(PAD + t * rt, rt), :] = u_ref[pl.ds(t * rt, rt), :]

    def stage(src, dst, sh):
        for t in range(nrt):
            r0 = PAD + t * rt
            dst[pl.ds(r0, rt), :] = src[pl.ds(r0, rt), :] + src[pl.ds(r0 - sh, rt), :]

    stage(buf_a, buf_b, 1)
    stage(buf_b, buf_a, 2)

    @pl.when(jj == 1)
    def _():
        stage(buf_a, buf_b, 4)
        stage(buf_b, buf_a, 8)


def _pool_counts(jj, lo, r0, rt):
    w = jnp.where(lo, jnp.where(jj == 0, 2.0, 8.0), jnp.where(jj == 0, 4.0, 16.0))
    pos1 = (lax.broadcasted_iota(jnp.int32, (rt, LANE), 0) + (r0 + 1)).astype(F32)
    return jnp.minimum(pos1, w)


def _conv_pool_fwd(p, mix, conv_w, wbd, pscale, l):
    s = p.shape[0]
    rt = min(256, s)
    nrt = s // rt

    def body(gb_ref, gc_ref, hin_ref, u_ref, cw_ref, wbd_ref, ps_ref, mix_in, o_ref, buf_a, buf_b):
        del mix_in
        g = pl.program_id(0)
        zpad = jnp.zeros((PAD, LANE), F32)
        buf_a[pl.ds(0, PAD), :] = zpad
        buf_b[pl.ds(0, PAD), :] = zpad

        @pl.when(g < 2)
        def _conv():
            for t in range(nrt):
                buf_a[pl.ds(PAD + t * rt, rt), :] = gc_ref[pl.ds(t * rt, rt), :] * hin_ref[pl.ds(t * rt, rt), :]
            w0, w1, w2 = cw_ref[0:1, :], cw_ref[1:2, :], cw_ref[2:3, :]
            for t in range(nrt):
                r0 = PAD + t * rt
                y = w0 * buf_a[pl.ds(r0 - 2, rt), :] + w1 * buf_a[pl.ds(r0 - 1, rt), :] + w2 * buf_a[pl.ds(r0, rt), :]
                o_ref[pl.ds(t * rt, rt), :] = (gb_ref[pl.ds(t * rt, rt), :] * y).astype(BF16)

        @pl.when(g >= 2)
        def _pool():
            jj = g - 2
            lo = _lo_mask()
            _pool_window_sums(u_ref, buf_a, buf_b, jj, s, rt)
            wb = wbd_ref[...]
            for t in range(nrt):
                r0 = PAD + t * rt
                wsum = jnp.where(lo, buf_b[pl.ds(r0, rt), :], buf_a[pl.ds(r0, rt), :])
                m = wsum / _pool_counts(jj, lo, t * rt, rt) - u_ref[pl.ds(t * rt, rt), :]
                o_ref[pl.ds(t * rt, rt), :] = (_nn(m.astype(BF16), wb) * ps_ref[...]).astype(BF16)

    return pl.pallas_call(
        body, name="conv_pool_fwd", grid=(4,),
        in_specs=_cp_in_specs(s, l) + [pl.BlockSpec(memory_space=pl.ANY)],
        out_specs=pl.BlockSpec((s, LANE), lambda g: (0, AW // LANE + g)),
        out_shape=SDS((s, D), BF16),
        scratch_shapes=[pltpu.VMEM((s + 2 * PAD, LANE), F32), pltpu.VMEM((s + 2 * PAD, LANE), F32)],
        input_output_aliases={7: 0},
        compiler_params=_cp(),
    )(p, p, p, p, conv_w, wbd, pscale, mix)


def _mlp_fwd(x, mix, wout, g2, w1_t, w2, l, xchg=None):
    s = x.shape[0]
    t = min(256, s)

    def body(x_ref, mix_ref, wo_ref, g_ref, w1_ref, w2_ref, xm_ref, a_ref, xo_ref):
        xm = x_ref[...] + _nn(mix_ref[...], wo_ref[...])
        xm_ref[...] = xm
        r = lax.rsqrt(jnp.mean(xm * xm, axis=-1, keepdims=True) + EPS)
        h2 = (xm * r * g_ref[...]).astype(BF16)
        a = _nt(h2, w1_ref[...])
        a_ref[...] = a.astype(BF16)
        f = jnp.square(jnp.maximum(a, 0.0)).astype(BF16)
        xo_ref[...] = xm + _nn(f, w2_ref[...])

    row = lambda c: pl.BlockSpec((t, c), lambda i: (i, 0))
    return _hosted_call(
        body, xchg, name="mlp_fwd", grid=(s // t,),
        in_specs=[row(D), row(D), _const((D, D)), _layer((1, D), l), _const((DFF, D)), _const((DFF, D))],
        out_specs=[row(D), row(DFF), row(D)],
        out_shape=[SDS((s, D), F32), SDS((s, DFF), BF16), SDS((s, D), F32)], args=[x, mix, wout, g2, w1_t, w2])


def _loss_grad(y, target):
    s = y.shape[0]
    t = min(512, s)

    def body(y_ref, t_ref, dy_ref, acc_ref):
        @pl.when(pl.program_id(0) == 0)
        def _():
            acc_ref[...] = jnp.zeros_like(acc_ref)
        e = y_ref[...] - t_ref[...]
        dy_ref[...] = e * (1.0 / D)
        acc_ref[...] += jnp.sum(e * e)

    row = pl.BlockSpec((t, D), lambda i: (i, 0))
    return pl.pallas_call(
        body, name="loss_grad", grid=(s // t,),
        in_specs=[row, row],
        out_specs=[row, pl.BlockSpec((8, LANE), lambda i: (0, 0))],
        out_shape=[SDS((s, D), F32), SDS((8, LANE), F32)],
        compiler_params=_cp(),
    )(y, target)


def _mlp_bwd(dxo, a, xm, g2, w1_t, w2, wout, l, xchg=None):
    s = dxo.shape[0]
    t = min(256, s)

    def body(dxo_ref, a_ref, xm_ref, g_ref, w1_ref, w2_ref, wo_ref,
             dxm_ref, dmix_ref, f_ref, da_ref, h2_ref, dxob_ref, dxmb_ref, dg_ref):
        @pl.when(pl.program_id(0) == 0)
        def _():
            dg_ref[...] = jnp.zeros_like(dg_ref)
        dxo = dxo_ref[...]
        dxob = dxo.astype(BF16)
        dxob_ref[...] = dxob
        ra = jnp.maximum(a_ref[...].astype(F32), 0.0)
        f_ref[...] = jnp.square(ra).astype(BF16)
        dab = (_nt(dxob, w2_ref[...]) * (2.0 * ra)).astype(BF16)
        da_ref[...] = dab
        dh2 = _nn(dab, w1_ref[...])
        xm = xm_ref[...]
        g = g_ref[...]
        r = lax.rsqrt(jnp.mean(xm * xm, axis=-1, keepdims=True) + EPS)
        h2_ref[...] = (xm * r * g).astype(BF16)
        dx_n, dgr = _rms_bwd(dh2, xm, g)
        dg_ref[...] += jnp.sum(dgr, axis=0, keepdims=True)
        dxm = dxo + dx_n
        dxm_ref[...] = dxm
        dxmb = dxm.astype(BF16)
        dxmb_ref[...] = dxmb
        dmix_ref[...] = _nt(dxmb, wo_ref[...])

    row = lambda c: pl.BlockSpec((t, c), lambda i: (i, 0))
    return _hosted_call(
        body, xchg, name="mlp_bwd", grid=(s // t,),
        in_specs=[row(D), row(DFF), row(D), _layer((1, D), l), _const((DFF, D)), _const((DFF, D)), _const((D, D))],
        out_specs=[row(D), row(D), row(DFF), row(DFF), row(D), row(D), row(D), pl.BlockSpec((1, D), lambda i: (0, 0))],
        out_shape=[SDS((s, D), F32), SDS((s, D), F32), SDS((s, DFF), BF16), SDS((s, DFF), BF16),
                   SDS((s, D), BF16), SDS((s, D), BF16), SDS((s, D), BF16), SDS((1, D), F32)],
        args=[dxo, a, xm, g2, w1_t, w2, wout])


def _attn_bwd(qkv, p, lse, o32, dmix, qg2, kg2, bias, l, xchg=None):
    s = p.shape[0]
    nq = s // TQ
    scale = HD ** -0.5

    def body(qs_ref, kb_ref, vb_ref, q_ref, k_ref, qg_ref, kg_ref, b_ref, lse_ref, o_ref, do_ref,
             dq_ref, dk_ref, dv_ref, db_ref, dqg_ref, dkg_ref,
             dk_acc, dv_acc, s_ref, dp_ref, ds_ref, pb_ref, dqn_ref, dl_ref):
        i = pl.program_id(1)
        kt = jnp.maximum(i - BAND // TQ, 0)
        ks = pl.multiple_of(kt * TQ, TQ)
        lo = _lo_mask()

        @pl.when(i == 0)
        def _():
            dk_acc[...] = jnp.zeros_like(dk_acc)
            dv_acc[...] = jnp.zeros_like(dv_acc)
            dqg_ref[...] = jnp.zeros_like(dqg_ref)
            dkg_ref[...] = jnp.zeros_like(dkg_ref)

        @pl.when(i < NVAR)
        def _():
            db_ref[...] = jnp.zeros_like(db_ref)

        qs = qs_ref[...]
        kwin = kb_ref[pl.ds(ks, WIN), :]
        vwin = vb_ref[pl.ds(ks, WIN), :]
        do = do_ref[...]
        dob = do.astype(BF16)
        dl_ref[...] = _half_sum(do * o_ref[...], lo)
        for half in range(2):
            m_ = lo if half == 0 else jnp.logical_not(lo)
            qa = jnp.where(m_, qs, jnp.zeros_like(qs))
            doa = jnp.where(m_, dob, jnp.zeros_like(dob))
            s_ref[half] = _nt(qa, kwin)
            dp_ref[half] = _nt(doa, vwin)
            for r0 in range(0, TQ, RB_SOFT):
                rows = pl.ds(r0, RB_SOFT)
                lse_h = lse_ref[rows, half * HD:half * HD + 1]
                pm = jnp.exp(s_ref[half, rows, :] + b_ref[half, rows, :] - lse_h)
                ds = pm * (dp_ref[half, rows, :] - dl_ref[rows, half * HD:half * HD + 1])
                db_ref[half, rows, :] += ds
                ds_ref[half, rows, :] = ds.astype(BF16)
                pb_ref[half, rows, :] = pm.astype(BF16)
            dsb = ds_ref[half]
            dq_h = _nn(dsb, kwin)
            if half == 0:
                dqn_ref[...] = dq_h
            else:
                dqn_ref[...] = jnp.where(lo, dqn_ref[...], dq_h)
            dk_t = _tn(qa, dsb)
            dv_t = _tn(doa, pb_ref[half])
            for t in range(WIN // TQ):
                dk_acc[kt + t] += dk_t[:, t * TQ:(t + 1) * TQ]
                dv_acc[kt + t] += dv_t[:, t * TQ:(t + 1) * TQ]
        qg, kg = qg_ref[...], kg_ref[...]
        xq, rq = _head_norm(q_ref[...], lo)
        dq, dqg_rows = _head_norm_bwd(dqn_ref[...] * scale, xq, rq, qg, lo)
        dq_ref[...] = dq.astype(BF16)
        dqg_ref[...] += jnp.sum(dqg_rows, axis=0, keepdims=True)

        @pl.when(i == nq - 1)
        def _():
            dkg = jnp.zeros((1, LANE), F32)
            for t in range(nq):
                rows = pl.ds(t * TQ, TQ)
                xk, rk = _head_norm(k_ref[rows, :], lo)
                dk, dkg_rows = _head_norm_bwd(dk_acc[t].T, xk, rk, kg, lo)
                dk_ref[rows, :] = dk.astype(BF16)
                dv_ref[rows, :] = dv_acc[t].T.astype(BF16)
                dkg = dkg + jnp.sum(dkg_rows, axis=0, keepdims=True)
            dkg_ref[...] = dkg

    tile = pl.BlockSpec((TQ, LANE), lambda j, i: (i, j))
    kcol = lambda c0: pl.BlockSpec((s, LANE), lambda j, i: (0, c0 + j))
    gain = pl.BlockSpec((None, 1, LANE), lambda j, i: (j, 0, 0))
    return _hosted_call(
        body, xchg, name="attn_bwd", grid=(NH // 2, nq),
        in_specs=[
            tile, kcol(AW // LANE), kcol(2 * AW // LANE), tile, kcol(AW // LANE),
            _layer((1, LANE), l), _layer((1, LANE), l),
            _bias_layer_spec(l), tile, tile, tile,
        ],
        out_specs=[tile, kcol(0), kcol(0), _bias_spec(), gain, gain],
        out_shape=[SDS((s, AW), BF16), SDS((s, AW), BF16), SDS((s, AW), BF16),
                   SDS((NVAR, NH, TQ, WIN), F32), SDS((NH // 2, 1, LANE), F32), SDS((NH // 2, 1, LANE), F32)],
        scratch_shapes=[pltpu.VMEM((nq, LANE, TQ), F32), pltpu.VMEM((nq, LANE, TQ), F32),
                        pltpu.VMEM((2, TQ, WIN), F32), pltpu.VMEM((2, TQ, WIN), F32),
                        pltpu.VMEM((2, TQ, WIN), BF16), pltpu.VMEM((2, TQ, WIN), BF16),
                        pltpu.VMEM((TQ, LANE), F32), pltpu.VMEM((TQ, LANE), F32)],
        args=[qkv, qkv, qkv, p, p, qg2, kg2, bias, lse, o32, dmix])


def _conv_pool_bwd(p, dmix, conv_w, wbd, pscale, l):
    s = p.shape[0]
    rt = min(256, s)
    nrt = s // rt

    def body(gb_ref, gc_ref, hin_ref, u_ref, cw_ref, wbd_ref, ps_ref, dy_ref,
             dgb_ref, dgc_ref, dhin_ref, du_ref, dcw_ref, dwbd_ref, dps_ref, buf_a, buf_b, buf_c, buf_d):
        g = pl.program_id(0)
        zpad = jnp.zeros((PAD, LANE), F32)
        for buf in (buf_a, buf_b, buf_c):
            buf[pl.ds(0, PAD), :] = zpad
            buf[pl.ds(PAD + s, PAD), :] = zpad

        @pl.when(g < 2)
        def _conv():
            for t in range(nrt):
                rows = pl.ds(t * rt, rt)
                buf_a[pl.ds(PAD + t * rt, rt), :] = gc_ref[rows, :] * hin_ref[rows, :]
                buf_b[pl.ds(PAD + t * rt, rt), :] = dy_ref[rows, :] * gb_ref[rows, :]
            w0, w1, w2 = cw_ref[0:1, :], cw_ref[1:2, :], cw_ref[2:3, :]
            d0 = jnp.zeros((1, LANE), F32)
            d1 = jnp.zeros((1, LANE), F32)
            d2 = jnp.zeros((1, LANE), F32)
            for t in range(nrt):
                rows = pl.ds(t * rt, rt)
                r0 = PAD + t * rt
                z2, z1, z0 = buf_a[pl.ds(r0 - 2, rt), :], buf_a[pl.ds(r0 - 1, rt), :], buf_a[pl.ds(r0, rt), :]
                y = w0 * z2 + w1 * z1 + w2 * z0
                dgb_ref[rows, :] = (dy_ref[rows, :] * y).astype(BF16)
                e0 = buf_b[pl.ds(r0, rt), :]
                d0 = d0 + jnp.sum(e0 * z2, axis=0, keepdims=True)
                d1 = d1 + jnp.sum(e0 * z1, axis=0, keepdims=True)
                d2 = d2 + jnp.sum(e0 * z0, axis=0, keepdims=True)
                dz = w2 * e0 + w1 * buf_b[pl.ds(r0 + 1, rt), :] + w0 * buf_b[pl.ds(r0 + 2, rt), :]
                dgc_ref[rows, :] = (dz * hin_ref[rows, :]).astype(BF16)
                dhin_ref[rows, :] = (dz * gc_ref[rows, :]).astype(BF16)
            dcw_ref[0:1, :] = d0
            dcw_ref[1:2, :] = d1
            dcw_ref[2:3, :] = d2

        @pl.when(g >= 2)
        def _pool():
            jj = g - 2
            lo = _lo_mask()
            _pool_window_sums(u_ref, buf_a, buf_b, jj, s, rt)
            wb = wbd_ref[...]
            ps = ps_ref[...]
            dps = jnp.zeros((1, LANE), F32)
            dwb = jnp.zeros((LANE, LANE), F32)
            for t in range(nrt):
                rows = pl.ds(t * rt, rt)
                r0 = PAD + t * rt
                cnt = _pool_counts(jj, lo, t * rt, rt)
                wsum = jnp.where(lo, buf_b[pl.ds(r0, rt), :], buf_a[pl.ds(r0, rt), :])
                mb = (wsum / cnt - u_ref[rows, :]).astype(BF16)
                dy = dy_ref[rows, :]
                dps = dps + jnp.sum(dy * _nn(mb, wb), axis=0, keepdims=True)
                dmp = (dy * ps).astype(BF16)
                dwb = dwb + _tn(mb, dmp)
                dm = _nt(dmp, wb)
                buf_d[rows, :] = dm
                buf_c[pl.ds(r0, rt), :] = dm / cnt
            dps_ref[...] = dps
            dwbd_ref[...] = dwb

            def stage(src, dst, sh):
                for t in range(nrt):
                    r0 = PAD + t * rt
                    dst[pl.ds(r0, rt), :] = src[pl.ds(r0, rt), :] + src[pl.ds(r0 + sh, rt), :]

            def finish(first, second):
                for t in range(nrt):
                    rows = pl.ds(t * rt, rt)
                    r0 = PAD + t * rt
                    fw = jnp.where(lo, first[pl.ds(r0, rt), :], second[pl.ds(r0, rt), :])
                    du_ref[rows, :] = (fw - buf_d[rows, :]).astype(BF16)

            stage(buf_c, buf_a, 1)
            stage(buf_a, buf_b, 2)

            @pl.when(jj == 0)
            def _():
                finish(buf_a, buf_b)

            @pl.when(jj == 1)
            def _():
                stage(buf_b, buf_c, 4)
                stage(buf_c, buf_a, 8)
                finish(buf_c, buf_a)

    cblk = pl.BlockSpec((s, LANE), lambda g: (0, jnp.minimum(g, 1)))
    pblk = pl.BlockSpec((s, LANE), lambda g: (0, jnp.maximum(g - 2, 0)))
    padded = pltpu.VMEM((s + 2 * PAD, LANE), F32)
    return pl.pallas_call(
        body, name="conv_pool_bwd", grid=(4,),
        in_specs=_cp_in_specs(s, l) + [pl.BlockSpec((s, LANE), lambda g: (0, AW // LANE + g))],
        out_specs=[cblk, cblk, cblk, pblk,
                   pl.BlockSpec((3, LANE), lambda g: (0, jnp.minimum(g, 1))),
                   pl.BlockSpec((None, LANE, LANE), lambda g: (jnp.maximum(g - 2, 0), 0, 0)),
                   pl.BlockSpec((1, LANE), lambda g: (0, jnp.maximum(g - 2, 0)))],
        out_shape=[SDS((s, CW), BF16), SDS((s, CW), BF16), SDS((s, CW), BF16), SDS((s, PWD), BF16),
                   SDS((3, CW), F32), SDS((2, LANE, LANE), F32), SDS((1, PWD), F32)],
        scratch_shapes=[padded, padded, padded, pltpu.VMEM((s, LANE), F32)],
        compiler_params=_cp(),
    )(p, p, p, p, conv_w, wbd, pscale, dmix)


def _in_proj_bwd(parts, x, dxm, g1, win_t, l, xchg=None):
    s = x.shape[0]
    t = min(256, s)
    widths = [a.shape[1] for a in parts]
    offs = [int(o) for o in np.cumsum([0] + widths[:-1])]
    n = len(parts)

    def body(*refs):
        part_refs = refs[:n]
        x_ref, dxm_ref, g_ref, w_ref, dx_ref, dp_ref, dg_ref = refs[n:]

        @pl.when(pl.program_id(0) == 0)
        def _():
            dg_ref[...] = jnp.zeros_like(dg_ref)
        for r, o, w in zip(part_refs, offs, widths):
            dp_ref[:, o:o + w] = r[...]
        dh = _nn(dp_ref[...], w_ref[...])
        dx_n, dgr = _rms_bwd(dh, x_ref[...], g_ref[...])
        dg_ref[...] += jnp.sum(dgr, axis=0, keepdims=True)
        dx_ref[...] = dxm_ref[...] + dx_n

    row = lambda c: pl.BlockSpec((t, c), lambda i: (i, 0))
    return _hosted_call(
        body, xchg, name="in_proj_bwd", grid=(s // t,),
        in_specs=[row(w) for w in widths] + [row(D), row(D), _layer((1, D), l), _const((DIN, D))],
        out_specs=[row(D), row(DIN), pl.BlockSpec((1, D), lambda i: (0, 0))],
        out_shape=[SDS((s, D), F32), SDS((s, DIN), BF16), SDS((1, D), F32)], args=[*parts, x, dxm, g1, win_t])


def _wgrad(a, b, tag, xchg=None):
    s, m = a.shape
    mb = 512

    def body(a_ref, b_ref, o_ref):
        o_ref[...] = _tn(a_ref[...], b_ref[...]).astype(BF16)

    (out,), got = _hosted_call(
        body, xchg, name=f"wgrad_{tag}", grid=(m // mb,),
        in_specs=[pl.BlockSpec((s, mb), lambda mi: (0, mi)), _const((s, D))],
        out_specs=[pl.BlockSpec((mb, D), lambda mi: (mi, 0))],
        out_shape=[SDS((m, D), BF16)], args=[a, b])
    return out, got


def _bias_tables(gvec, xchg=None):
    def body(g_ref, o_ref):
        qc = lax.broadcasted_iota(jnp.int32, (TQ, WIN), 0) // CHUNK
        kc = lax.broadcasted_iota(jnp.int32, (TQ, WIN), 1) // CHUNK
        for var in range(NVAR):
            vec = jnp.broadcast_to(g_ref[:, var * TQ:var * TQ + NTOE], (TQ, NTOE))
            toe = pltpu.roll(vec, NTOE - TQ + 1, 1, stride=1, stride_axis=0)[:, :WIN]
            rel = (BAND - var * TQ) // CHUNK + qc - kc
            o_ref[var] = jnp.where((rel >= 0) & (rel <= N_PREV), toe, NEG)

    (out,), got = _hosted_call(
        body, xchg, name="bias_tables", grid=(L, NH),
        in_specs=[pl.BlockSpec((None, None, 1, NG), lambda l, h: (l, h, 0, 0))],
        out_specs=[pl.BlockSpec((None, NVAR, None, TQ, WIN), lambda l, h: (l, 0, h, 0, 0))],
        out_shape=[SDS((L, NVAR, NH, TQ, WIN), F32)], args=[gvec])
    return out, got


def _bias_tables_grad(dbias, l):
    nb = NTOE // LANE
    wb = WIN // LANE

    def body(d_ref, o_ref):
        ii = lax.broadcasted_iota(jnp.int32, (LANE, LANE), 0)
        jj = lax.broadcasted_iota(jnp.int32, (LANE, LANE), 1)
        flip = jnp.where(ii + jj == LANE - 1, 1.0, 0.0).astype(BF16)
        o_ref[...] = jnp.zeros_like(o_ref)
        for var in range(NVAR):
            blocks = []
            for b in range(nb):
                src = nb - 1 - b
                if src >= wb:
                    blocks.append(jnp.zeros((TQ, LANE), F32))
                    continue
                xv = d_ref[var, :, src * LANE:(src + 1) * LANE]
                hi = xv.astype(BF16)
                lo = (xv - hi.astype(F32)).astype(BF16)
                blocks.append(_nn(hi, flip) + _nn(lo, flip))
            rev = jnp.concatenate(blocks, axis=1)
            skew = pltpu.roll(rev, NTOE - TQ + 1, 1, stride=1, stride_axis=0)
            off = NG - NTOE - var * TQ
            o_ref[:, off:off + NTOE] += jnp.sum(skew, axis=0, keepdims=True)

    return pl.pallas_call(
        body, name=f"bias_tables_grad_l{l}", grid=(NH,),
        in_specs=[pl.BlockSpec((NVAR, None, TQ, WIN), lambda h: (0, h, 0, 0))],
        out_specs=pl.BlockSpec((None, 1, NG), lambda h: (h, 0, 0)),
        out_shape=SDS((NH, 1, NG), F32),
        compiler_params=_cp(),
    )(dbias)


_SIBLING = (0, 0, 1)
_CHIPS = [(1, 0, 0), (0, 1, 0), (1, 1, 0)]
_MASKS = [_SIBLING] + _CHIPS + [(1, 0, 1), (0, 1, 1), (1, 1, 1)]


def _position():
    return lax.axis_index("x"), lax.axis_index("y"), lax.axis_index("c")


def _peer(pos, mask):
    return tuple(1 - a if f else a for a, f in zip(pos, mask))


def _index(pos):
    return 4 * pos[0] + 2 * pos[1] + pos[2]


def _exchange_phases(items, src, dst, sems):
    send_sems, recv_sems, local_sems = sems
    me = _position()
    sib = _peer(me, _SIBLING)

    def remote(s_ref, d_ref, pi, n, to):
        return pltpu.make_async_remote_copy(
            src_ref=s_ref, dst_ref=d_ref, send_sem=send_sems.at[pi, n], recv_sem=recv_sems.at[pi, n],
            device_id=to, device_id_type=MESH_ID)

    def parts(n):
        it = items[n]
        r = SHARD_ROWS[it[1]]
        block = lambda ref, pos: ref.at[pl.ds(_index(pos) * r, r), :]
        if it[0] == "gather":
            own = src[n].at[it[2]]
            local = pltpu.make_async_copy(own, block(dst[n], me), local_sems.at[n])
            sends = [remote(own, block(dst[n], me), pi, n, _peer(me, m)) for pi, m in enumerate([_SIBLING] + _CHIPS)]
            hops = [(remote(block(dst[n], _peer(me, m)), block(dst[n], _peer(me, m)), 1 + j, n, _peer(me, m)),
                     remote(block(dst[n], _peer(me, m)), block(dst[n], _peer(me, m)), 4 + j, n, sib))
                    for j, m in enumerate(_CHIPS)]
            lands = [remote(own, block(dst[n], sib), 0, n, sib)]
            lands += [remote(own, block(dst[n], _peer(sib, m)), 4 + j, n, sib) for j, m in enumerate(_CHIPS)]
        else:
            local = pltpu.make_async_copy(block(src[n], me), dst[n].at[_index(me)], local_sems.at[n])
            sends = [remote(block(src[n], _peer(me, m)), dst[n].at[_index(me)], pi, n, _peer(me, m))
                     for pi, m in enumerate(_MASKS)]
            hops = []
            lands = [remote(block(src[n], me), dst[n].at[_index(_peer(me, m))], pi, n, _peer(me, m))
                     for pi, m in enumerate(_MASKS)]
        return local, sends, hops, lands

    def start():
        for n in range(len(items)):
            local, sends, _, _ = parts(n)
            local.start()
            for cp in sends:
                cp.start()

    def relay():
        for n in range(len(items)):
            for arrived, onward in parts(n)[2]:
                arrived.wait_recv()
                onward.start()

    def finish():
        for n in range(len(items)):
            local, sends, hops, lands = parts(n)
            for cp in lands:
                cp.wait_recv()
            for cp in sends + [onward for _, onward in hops]:
                cp.wait_send()
            local.wait()

    return start, relay, finish


def _hosted_call(body, xchg, *, name, grid, in_specs, out_specs, out_shape, args, scratch_shapes=(), relay_at=0.8):
    if not xchg:
        outs = pl.pallas_call(
            body, name=name, grid=grid, in_specs=list(in_specs), out_specs=list(out_specs),
            out_shape=list(out_shape), scratch_shapes=list(scratch_shapes), compiler_params=_cp())(*args)
        return outs, []
    items = [it for it, _ in xchg]
    n_in, n_out, n_scr, nit = len(args), len(out_shape), len(scratch_shapes), len(items)
    hbm = pl.BlockSpec(memory_space=pl.ANY)
    steps = int(np.prod(grid))
    relay_step = min(int(relay_at * steps), steps - 1)

    def dst_shape(it):
        r = SHARD_ROWS[it[1]]
        return SDS((NDEV * r, D) if it[0] == "gather" else (NDEV, r, D), BF16)

    def wrapped(*refs):
        ins = refs[:n_in]
        src = refs[n_in:n_in + nit]
        outs = refs[n_in + nit:n_in + nit + n_out]
        dst = refs[n_in + nit + n_out:n_in + 2 * nit + n_out]
        scratch = refs[n_in + 2 * nit + n_out:n_in + 2 * nit + n_out + n_scr]
        start, relay, finish = _exchange_phases(items, src, dst, refs[n_in + 2 * nit + n_out + n_scr:])
        step = 0
        for d, g in enumerate(grid):
            step = step * g + pl.program_id(d)
        pl.when(step == 0)(start)
        body(*ins, *outs, *scratch)
        pl.when(step == relay_step)(relay)
        pl.when(step == steps - 1)(finish)

    npeer = len(_MASKS)
    res = pl.pallas_call(
        wrapped, name=name, grid=grid,
        in_specs=list(in_specs) + [hbm] * nit,
        out_specs=list(out_specs) + [hbm] * nit,
        out_shape=list(out_shape) + [dst_shape(it) for it in items],
        scratch_shapes=list(scratch_shapes) + [
            pltpu.SemaphoreType.DMA((npeer, nit)), pltpu.SemaphoreType.DMA((npeer, nit)), pltpu.SemaphoreType.DMA((nit,))],
        compiler_params=_cp(),
    )(*args, *[a for _, a in xchg])
    return list(res[:n_out]), list(res[n_out:])


def _sum_slots(slots, xchg=None):
    _, r, _ = slots[0].shape
    rt = 64

    def body(*refs):
        o_ref = refs[L]
        for l in range(L):
            acc = refs[l][0].astype(F32)
            for d in range(1, NDEV):
                acc = acc + refs[l][d].astype(F32)
            o_ref[l] = acc

    (out,), got = _hosted_call(
        body, xchg, name=f"sum_slots_r{r}" + ("_x" if xchg else ""), grid=(r // rt,),
        in_specs=[pl.BlockSpec((NDEV, rt, D), lambda i: (0, i, 0))] * L,
        out_specs=[pl.BlockSpec((L, rt, D), lambda i: (0, i, 0))],
        out_shape=[SDS((L, r, D), F32)], args=list(slots))
    return out, got


def _exchange_small(v, reduce):
    rows = v.shape[0]

    def body(v_ref, o_ref, *scratch):
        if reduce:
            slots, send_sems, recv_sems = scratch
        else:
            slots = o_ref
            send_sems, recv_sems = scratch
        me = _position()
        slots[_index(me)] = v_ref[...]
        sends = []
        for pi, mask in enumerate(_MASKS):
            cp = pltpu.make_async_remote_copy(
                src_ref=v_ref, dst_ref=slots.at[_index(me)], send_sem=send_sems.at[pi], recv_sem=recv_sems.at[pi],
                device_id=_peer(me, mask), device_id_type=MESH_ID)
            cp.start()
            sends.append(cp)
        for pi, mask in enumerate(_MASKS):
            peer = _peer(me, mask)
            pltpu.make_async_remote_copy(
                src_ref=v_ref, dst_ref=slots.at[_index(peer)], send_sem=send_sems.at[pi], recv_sem=recv_sems.at[pi],
                device_id=peer, device_id_type=MESH_ID).wait_recv()
        for cp in sends:
            cp.wait_send()
        if reduce:
            acc = slots[0]
            for d in range(1, NDEV):
                acc = acc + slots[d]
            o_ref[...] = acc

    vm = pl.BlockSpec(memory_space=pltpu.VMEM)
    sems = [pltpu.SemaphoreType.DMA((len(_MASKS),)), pltpu.SemaphoreType.DMA((len(_MASKS),))]
    return pl.pallas_call(
        body, name="reduce_small" if reduce else "gather_small",
        in_specs=[vm], out_specs=vm,
        out_shape=SDS((rows, LANE) if reduce else (NDEV, rows, LANE), F32),
        scratch_shapes=([pltpu.VMEM((NDEV, rows, LANE), F32)] if reduce else []) + sems,
        compiler_params=_cp(),
    )(v)


def _adamw_update(w_ref, g_ref, m_ref, v_ref, d_ref, nm_ref, nv_ref):
    gv = g_ref[...]
    mn = B1 * m_ref[...] + (1.0 - B1) * gv
    vn = B2 * v_ref[...] + (1.0 - B2) * jnp.square(gv)
    nm_ref[...] = mn
    nv_ref[...] = vn
    m_hat = mn / (1.0 - B1 ** STEP)
    v_hat = vn / (1.0 - B2 ** STEP)
    d_ref[...] = -LR * (m_hat / (jnp.sqrt(v_hat) + AEPS) + WD * w_ref[...])


def _adamw_small(ws, gs, ms, vs):
    n = len(ws)

    def body(*refs):
        for i in range(n):
            _adamw_update(*[refs[j * n + i] for j in range(7)])

    vm = pl.BlockSpec(memory_space=pltpu.VMEM)
    res = pl.pallas_call(
        body, name="adamw_small", in_specs=[vm] * (4 * n), out_specs=[vm] * (3 * n),
        out_shape=[SDS(w.shape, F32) for _ in range(3) for w in ws],
        compiler_params=_cp(),
    )(*ws, *gs, *ms, *vs)
    return res[:n], res[n:2 * n], res[2 * n:]


def _adamw(w, g, m, v):
    rows, cols = w.shape
    t = rows
    for cand in (512, 256, 128, 64, 32, 16, 8):
        if rows % cand == 0:
            t = cand
            break

    def body(*refs):
        _adamw_update(*refs)

    blk = pl.BlockSpec((t, cols), lambda i: (i, 0))
    return pl.pallas_call(
        body, name=f"adamw_{rows}x{cols}", grid=(rows // t,),
        in_specs=[blk] * 4, out_specs=[blk] * 3,
        out_shape=[SDS((rows, cols), F32)] * 3,
        compiler_params=_cp(),
    )(w, g, m, v)


_DIST0 = BAND + TQ - 1
_N_FAR = _DIST0 - REL_CLIP + 1
_N_NEAR = NG - _N_FAR - (2 * REL_CLIP - 1)


def _bias_vector(rel_bias):
    far = jnp.broadcast_to(rel_bias[..., -1:], (L, NH, _N_FAR))
    near = jnp.broadcast_to(rel_bias[..., :1], (L, NH, _N_NEAR))
    return jnp.concatenate([far, lax.rev(rel_bias[..., 1:-1], (2,)), near], axis=2)[:, :, None, :]


def _bias_vector_grad(dgr):
    first = jnp.sum(dgr[..., :_N_NEAR], axis=-1, keepdims=True)
    last = jnp.sum(dgr[..., NG - _N_FAR:], axis=-1, keepdims=True)
    return jnp.concatenate([first, dgr[..., _N_NEAR:NG - _N_FAR], last], axis=-1)


def _pool_blockdiag(pool_w):
    eye = jnp.eye(2, dtype=F32)
    pw = pool_w.reshape(L, 2, 2, HD, HD)
    return jnp.einsum("ljaik,ab->ljaibk", pw, eye).reshape(L, 2, LANE, LANE)


def _pool_blockdiag_grad(dwbd):
    d = dwbd.reshape(L, 2, 2, HD, 2, HD)
    return jnp.stack([d[:, :, 0, :, 0, :], d[:, :, 1, :, 1, :]], axis=2).reshape(L, 4, HD, HD)


def _pack(arrays, rows):
    flat = jnp.concatenate([a.reshape(-1).astype(F32) for a in arrays])
    return jnp.pad(flat, (0, rows * LANE - flat.shape[0])).reshape(rows, LANE)


def _unpack(packed, shapes):
    flat = packed.reshape(-1)
    out, o = [], 0
    for shp in shapes:
        n = int(np.prod(shp))
        out.append(flat[o:o + n].reshape(shp))
        o += n
    return out


def _rows_for(shapes):
    n = sum(int(np.prod(s)) for s in shapes)
    return -(-n // (8 * LANE)) * 8


def _grads(x, target, small_w, shards):
    g1, qg, kg, rb, cw, pw, ps, g2 = small_w
    g1 = g1.reshape(L, 1, D)
    g2 = g2.reshape(L, 1, D)
    qg2 = jnp.tile(qg, (1, 2)).reshape(L, 1, LANE)
    kg2 = jnp.tile(kg, (1, 2)).reshape(L, 1, LANE)
    ps3 = ps.reshape(L, 1, PWD)
    wbd = _pool_blockdiag(pw).astype(BF16)

    def gather(*kl):
        return [(("gather", k, l), shards[k]) for k, l in kl if l < L]

    full = {}

    def arrived(got, *kl):
        full.update(zip([x for x in kl if x[1] < L], got))

    bias, got = _bias_tables(_bias_vector(rb), gather((0, 0)))
    arrived(got, (0, 0))
    saved = []
    h = x
    for l in range(L):
        kl = ((1, l),)
        (p, h_b, qkv), got = _in_proj(h, g1, full[0, l], qg2, kg2, l, gather(*kl))
        arrived(got, *kl)
        kl = ((2, 0), (3, 0)) if l == 0 else ((3, l),)
        (mix, lse, o32), got = _attn_fwd(qkv, bias, l, gather(*kl), relay_at=0.95 if l == 0 else 0.8)
        arrived(got, *kl)
        mix = _conv_pool_fwd(p, mix, cw, wbd, ps3, l)
        kl = ((0, l + 1), (2, l + 1))
        (xm, a, xo), got = _mlp_fwd(h, mix, full[1, l], g2, full[2, l], full[3, l], l, gather(*kl))
        arrived(got, *kl)
        saved.append((h, h_b, p, qkv, mix, lse, o32, xm, a))
        h = xo
    dx, sq = _loss_grad(h, target)

    grads = {}
    slots = {}

    def scatter(*kl):
        return [(("scatter", k), grads[k, l]) for k, l in kl if l < L]

    def left(got, *kl):
        slots.update(zip([x for x in kl if x[1] < L], got))

    per_layer = [None] * L
    for l in reversed(range(L)):
        x_in, h_b, p, qkv, mix, lse, o32, xm, a = saved[l]
        (dxm, dmix, f_b, da_b, h2_b, dxo_b, dxm_b, dg2), got = _mlp_bwd(
            dx, a, xm, g2, full[2, l], full[3, l], full[1, l], l, scatter((3, l + 1)))
        left(got, (3, l + 1))
        grads[1, l], _ = _wgrad(mix, dxm_b, f"w_out_l{l}")
        grads[2, l], got = _wgrad(da_b, h2_b, f"w_mlp1_l{l}", scatter((1, l)))
        left(got, (1, l))
        kl = ((2, 0),) if l == 0 else ()
        grads[3, l], got = _wgrad(f_b, dxo_b, f"w_mlp2_l{l}", scatter(*kl))
        left(got, *kl)
        kl = ((3, 0), (0, 1)) if l == 0 else ((2, l),)
        (dq, dk, dv, dbias, dqg, dkg), got = _attn_bwd(qkv, p, lse, o32, dmix, qg2, kg2, bias, l, scatter(*kl))
        left(got, *kl)
        dgb, dgc, dhin, du, dcw, dwbd, dps = _conv_pool_bwd(p, dmix, cw, wbd, ps3, l)
        (dx, dp_b, dg1), got = _in_proj_bwd(
            [dq, dk, dv, dgb, dgc, dhin, du], x_in, dxm, g1, full[0, l], l, scatter((0, l + 1)) if l else None)
        left(got, (0, l + 1))
        grads[0, l], _ = _wgrad(dp_b, h_b, f"w_in_l{l}")
        per_layer[l] = (dg1, dg2, dqg, dkg, _bias_tables_grad(dbias, l), dcw, dwbd, dps)
    g_w1_t, got = _sum_slots([slots[2, l] for l in range(L)], scatter((0, 0)))
    left(got, (0, 0))
    sums = [_sum_slots([slots[k, l] for l in range(L)])[0] for k in (0, 1)] + [
        g_w1_t, _sum_slots([slots[3, l] for l in range(L)])[0]]

    st = [jnp.stack([per_layer[l][k] for l in range(L)]) for k in range(8)]
    small = dict(
        g1=st[0].reshape(L, D), g2=st[1].reshape(L, D),
        qg=st[2].reshape(L, NH, HD).sum(1), kg=st[3].reshape(L, NH, HD).sum(1),
        rb=_bias_vector_grad(st[4].reshape(L, NH, NG)), cw=st[5], pw=_pool_blockdiag_grad(st[6]),
        ps=st[7].reshape(L, PWD))
    return sq, dx, sums, small


def kernel(x, norm1_g, w_in, q_norm_g, k_norm_g, rel_bias, conv_w, pool_w, pool_scale, w_out, norm2_g, w_mlp1, w_mlp2, loss_target, m_norm1_g, m_w_in, m_q_norm_g, m_k_norm_g, m_rel_bias, m_conv_w, m_pool_w, m_pool_scale, m_w_out, m_norm2_g, m_w_mlp1, m_w_mlp2, v_norm1_g, v_w_in, v_q_norm_g, v_k_norm_g, v_rel_bias, v_conv_w, v_pool_w, v_pool_scale, v_w_out, v_norm2_g, v_w_mlp1, v_w_mlp2):
    me = _index(_position())
    cshard = CW // NDEV

    shards = [jnp.swapaxes(w_in, 1, 2).astype(BF16), w_out.astype(BF16),
              jnp.swapaxes(w_mlp1, 1, 2).astype(BF16), w_mlp2.astype(BF16)]
    cw_all = _exchange_small(_pack([conv_w], 8), reduce=False)
    cw_full = jnp.concatenate(
        [cw_all[d].reshape(-1)[:L * 3 * cshard].reshape(L, 3, cshard) for d in range(NDEV)], axis=2)

    small_w = (norm1_g, q_norm_g, k_norm_g, rel_bias, cw_full, pool_w, pool_scale, norm2_g)
    sq, grad_x, (g_win_t, g_wout, g_w1_t, g_w2), small = _grads(x[0], loss_target[0], small_w, shards)
    g_w_in = jnp.swapaxes(g_win_t, 1, 2)
    g_w_mlp1 = jnp.swapaxes(g_w1_t, 1, 2)

    names = ("g1", "qg", "kg", "rb", "cw", "pw", "ps", "g2")
    gshapes = [(L, D), (L, HD), (L, HD), (L, NH, 2 * REL_CLIP + 1), (L, 3, CW), (L, 4, HD, HD), (L, PWD), (L, D)]
    garrs = [small[n] for n in names]
    rows = _rows_for(gshapes + [(1,)])
    total = _exchange_small(_pack(garrs + [sq[0, :1]], rows), reduce=True)
    g_g1, g_qg, g_kg, g_rb, g_cw_full, g_pw, g_ps, g_g2, sq_sum = _unpack(total, gshapes + [(1,)])
    loss = (0.5 / D) * sq_sum[0]
    g_cw = lax.dynamic_slice_in_dim(g_cw_full, me * cshard, cshard, axis=2)

    def big(w, g, m, v):
        shp = w.shape
        r = lambda a: a.reshape(-1, shp[-1])
        return [o.reshape(shp) for o in _adamw(r(w), r(g), r(m), r(v))]

    up_in = big(w_in, g_w_in, m_w_in, v_w_in)
    up_out = big(w_out, g_wout, m_w_out, v_w_out)
    up_1 = big(w_mlp1, g_w_mlp1, m_w_mlp1, v_w_mlp1)
    up_2 = big(w_mlp2, g_w2, m_w_mlp2, v_w_mlp2)

    sw = [norm1_g, q_norm_g, k_norm_g, rel_bias, conv_w, pool_w, pool_scale, norm2_g]
    sg = [g_g1, g_qg, g_kg, g_rb, g_cw, g_pw, g_ps, g_g2]
    sm = [m_norm1_g, m_q_norm_g, m_k_norm_g, m_rel_bias, m_conv_w, m_pool_w, m_pool_scale, m_norm2_g]
    sv = [v_norm1_g, v_q_norm_g, v_k_norm_g, v_rel_bias, v_conv_w, v_pool_w, v_pool_scale, v_norm2_g]
    s_delta, s_m, s_v = _adamw_small(sw, sg, sm, sv)

    def order(small_list, in_, out_, m1, m2):
        g1_, qg_, kg_, rb_, cw_, pw_, ps_, g2_ = small_list
        return [g1_, in_, qg_, kg_, rb_, cw_, pw_, ps_, out_, g2_, m1, m2]

    grads = order(sg, g_w_in, g_wout, g_w_mlp1, g_w2)
    deltas = order(s_delta, up_in[0], up_out[0], up_1[0], up_2[0])
    new_m = order(s_m, up_in[1], up_out[1], up_1[1], up_2[1])
    new_v = order(s_v, up_in[2], up_out[2], up_1[2], up_2[2])
    return (loss, grad_x[None], *grads, *deltas, *new_m, *new_v)
```

```python
import numpy as np
import jax
import jax.numpy as jnp
from jax import lax
from jax.experimental import pallas as pl
from jax.experimental.pallas import tpu as pltpu

F32 = jnp.float32
BF16 = jnp.bfloat16
SDS = jax.ShapeDtypeStruct
MESH_ID = pl.DeviceIdType.MESH

D = 1024
L = 4
CHUNK = 64
N_PREV = 8
HD = 64
NH = 8
AW = 512
CW = 256
PWD = 256
DIN = 2560
DFF = 4096
EPS = 1e-6
NEG = -1e30
REL_CLIP = 128
POOL_WINDOWS = (2, 4, 8, 16)
LR, B1, B2, AEPS, WD, STEP = 0.001, 0.9, 0.999, 1e-08, 0.01, 10

NDEV = 8
LANE = 128
BAND = N_PREV * CHUNK
TQ = 256
WIN = TQ + BAND
NVAR = BAND // TQ + 1
NTOE = -(-(WIN + TQ - 1) // LANE) * LANE
NG = (NVAR - 1) * TQ + NTOE
PAD = 16
RB_NORM = 64
RB_SOFT = 16
VMEM_LIMIT = 56 * 1024 * 1024
SHARD_ROWS = (DIN // NDEV, D // NDEV, DFF // NDEV, DFF // NDEV)


def _cp(**kw):
    return pltpu.CompilerParams(vmem_limit_bytes=VMEM_LIMIT, **kw)


def _nn(a, b):
    return jnp.dot(a, b, preferred_element_type=F32)


def _nt(a, b):
    return lax.dot_general(a, b, (((1,), (1,)), ((), ())), preferred_element_type=F32)


def _tn(a, b):
    return lax.dot_general(a, b, (((0,), (0,)), ((), ())), preferred_element_type=F32)


def _const(shape):
    n = len(shape)
    return pl.BlockSpec(shape, lambda *_: (0,) * n, pipeline_mode=pl.Buffered(1))


def _layer(shape, l):
    n = len(shape)
    return pl.BlockSpec((None,) + tuple(shape), lambda *_: (l,) + (0,) * n, pipeline_mode=pl.Buffered(1))


def _lo_mask():
    return lax.broadcasted_iota(jnp.int32, (1, LANE), 1) < HD


def _half_sum(t, lo):
    s_lo = jnp.sum(jnp.where(lo, t, 0.0), axis=-1, keepdims=True)
    s_hi = jnp.sum(jnp.where(lo, 0.0, t), axis=-1, keepdims=True)
    return jnp.where(lo, s_lo, s_hi)


def _head_norm(x, lo):
    r = lax.rsqrt(_half_sum(x * x, lo) * (1.0 / HD) + EPS)
    return x * r, r


def _head_norm_bwd(dy, xn, r, g, lo):
    dxn = dy * g
    mu = _half_sum(dxn * xn, lo) * (1.0 / HD)
    return r * (dxn - xn * mu), dy * xn


def _rms_bwd(dy, x, g):
    r = lax.rsqrt(jnp.mean(x * x, axis=-1, keepdims=True) + EPS)
    xn = x * r
    dxn = dy * g
    mu = jnp.mean(dxn * xn, axis=-1, keepdims=True)
    return r * (dxn - xn * mu), dy * xn


def _in_proj(x, g1, win_t, qg2, kg2, l, xchg=None):
    s = x.shape[0]
    t = min(512, s)
    nblk = AW // LANE

    def body(x_ref, g_ref, w_ref, qg_ref, kg_ref, p_ref, h_ref, qkv_ref):
        xv = x_ref[...]
        r = lax.rsqrt(jnp.mean(xv * xv, axis=-1, keepdims=True) + EPS)
        h = (xv * r * g_ref[...]).astype(BF16)
        h_ref[...] = h
        p_ref[...] = _nt(h, w_ref[...])
        lo = _lo_mask()
        gains = (qg_ref[...] * (HD ** -0.5), kg_ref[...])
        for r0 in range(0, t, RB_NORM):
            rows = pl.ds(r0, RB_NORM)
            for c in range(3 * nblk):
                cols = pl.ds(c * LANE, LANE)
                v = p_ref[rows, cols]
                if c < 2 * nblk:
                    v = _head_norm(v, lo)[0] * gains[c // nblk]
                qkv_ref[rows, cols] = v.astype(BF16)

    row = lambda c: pl.BlockSpec((t, c), lambda i: (i, 0))
    return _hosted_call(
        body, xchg, name="in_proj", grid=(s // t,),
        in_specs=[row(D), _layer((1, D), l), _const((DIN, D)), _layer((1, LANE), l), _layer((1, LANE), l)],
        out_specs=[row(DIN), row(D), row(3 * AW)],
        out_shape=[SDS((s, DIN), F32), SDS((s, D), BF16), SDS((s, 3 * AW), BF16)], args=[x, g1, win_t, qg2, kg2])


def _bias_spec():
    return pl.BlockSpec((None, 2, TQ, WIN), lambda j, i: (jnp.maximum(NVAR - 1 - i, 0), j, 0, 0))


def _bias_layer_spec(l):
    return pl.BlockSpec((None, None, 2, TQ, WIN), lambda j, i: (l, jnp.maximum(NVAR - 1 - i, 0), j, 0, 0))


def _attn_fwd(qkv, bias, l, xchg=None, relay_at=0.8):
    s = qkv.shape[0]
    nq = s // TQ

    def body(q_ref, k_ref, v_ref, b_ref, o_ref, lse_ref, o32_ref, s_ref, p_ref, m_ref, den_ref, o0_ref):
        i = pl.program_id(1)
        ks = pl.multiple_of(jnp.maximum(i * TQ - BAND, 0), TQ)
        lo = _lo_mask()
        q = q_ref[...]
        kwin = k_ref[pl.ds(ks, WIN), :]
        vwin = v_ref[pl.ds(ks, WIN), :]
        for half in range(2):
            m_ = lo if half == 0 else jnp.logical_not(lo)
            s_ref[half] = _nt(jnp.where(m_, q, jnp.zeros_like(q)), kwin)
            for r0 in range(0, TQ, RB_SOFT):
                rows = pl.ds(r0, RB_SOFT)
                mx = jnp.max(s_ref[half, rows, :] + b_ref[half, rows, :], axis=-1, keepdims=True)
                m_ref[rows, :] = jnp.broadcast_to(mx, (RB_SOFT, LANE))
            for r0 in range(0, TQ, RB_SOFT):
                rows = pl.ds(r0, RB_SOFT)
                mx = m_ref[rows, 0:1]
                e = jnp.exp(s_ref[half, rows, :] + b_ref[half, rows, :] - mx)
                p_ref[half, rows, :] = e.astype(BF16)
                den = jnp.sum(e, axis=-1, keepdims=True)
                den_ref[rows, :] = jnp.broadcast_to(den, (RB_SOFT, LANE))
                lse = mx + jnp.log(den)
                if half == 0:
                    lse_ref[rows, :] = jnp.broadcast_to(lse, (RB_SOFT, LANE))
                else:
                    lse_ref[rows, :] = jnp.where(lo, lse_ref[rows, :], lse)
            o = _nn(p_ref[half], vwin) * (1.0 / den_ref[...])
            if half == 0:
                o0_ref[...] = o
            else:
                o = jnp.where(lo, o0_ref[...], o)
                o32_ref[...] = o
                o_ref[...] = o.astype(BF16)

    tile = pl.BlockSpec((TQ, LANE), lambda j, i: (i, j))
    stat = pltpu.VMEM((TQ, LANE), F32)
    return _hosted_call(
        body, xchg, name="attn_fwd", grid=(NH // 2, nq),
        in_specs=[
            tile,
            pl.BlockSpec((s, LANE), lambda j, i: (0, AW // LANE + j)),
            pl.BlockSpec((s, LANE), lambda j, i: (0, 2 * AW // LANE + j)),
            _bias_layer_spec(l),
        ],
        out_specs=[tile, tile, tile],
        out_shape=[SDS((s, D), BF16), SDS((s, AW), F32), SDS((s, AW), F32)], args=[qkv, qkv, qkv, bias],
        scratch_shapes=[pltpu.VMEM((2, TQ, WIN), F32), pltpu.VMEM((2, TQ, WIN), BF16), stat, stat, stat],
        relay_at=relay_at)


_C0 = 3 * AW // LANE


def _cp_in_specs(s, l):
    blk = lambda f: pl.BlockSpec((s, LANE), f)
    return [
        blk(lambda g: (0, _C0 + jnp.minimum(g, 1))),
        blk(lambda g: (0, _C0 + 2 + jnp.minimum(g, 1))),
        blk(lambda g: (0, _C0 + 4 + jnp.minimum(g, 1))),
        blk(lambda g: (0, _C0 + 6 + jnp.maximum(g - 2, 0))),
        pl.BlockSpec((None, 3, LANE), lambda g: (l, 0, jnp.minimum(g, 1))),
        pl.BlockSpec((None, None, LANE, LANE), lambda g: (l, jnp.maximum(g - 2, 0), 0, 0)),
        pl.BlockSpec((None, 1, LANE), lambda g: (l, 0, jnp.maximum(g - 2, 0))),
    ]


def _pool_window_sums(u_ref, buf_a, buf_b, jj, s, rt):
    nrt = s // rt
    for t in range(nrt):
        buf_a[pl.ds(PAD + t * rt, rt), :] = u_ref[pl.ds(t * rt, rt), :]

    def stage(src, dst, sh):
        for t in range(nrt):
            r0 = PAD + t * rt
            dst[pl.ds(r0, rt), :] = src[pl.ds(r0, rt), :] + src[pl.ds(r0 - sh, rt), :]

    stage(buf_a, buf_b, 1)
    stage(buf_b, buf_a, 2)

    @pl.when(jj == 1)
    def _():
        stage(buf_a, buf_b, 4)
        stage(buf_b, buf_a, 8)


def _pool_counts(jj, lo, r0, rt):
    w = jnp.where(lo, jnp.where(jj == 0, 2.0, 8.0), jnp.where(jj == 0, 4.0, 16.0))
    pos1 = (lax.broadcasted_iota(jnp.int32, (rt, LANE), 0) + (r0 + 1)).astype(F32)
    return jnp.minimum(pos1, w)


def _conv_pool_fwd(p, mix, conv_w, wbd, pscale, l):
    s = p.shape[0]
    rt = min(256, s)
    nrt = s // rt

    def body(gb_ref, gc_ref, hin_ref, u_ref, cw_ref, wbd_ref, ps_ref, mix_in, o_ref, buf_a, buf_b):
        del mix_in
        g = pl.program_id(0)
        zpad = jnp.zeros((PAD, LANE), F32)
        buf_a[pl.ds(0, PAD), :] = zpad
        buf_b[pl.ds(0, PAD), :] = zpad

        @pl.when(g < 2)
        def _conv():
            for t in range(nrt):
                buf_a[pl.ds(PAD + t * rt, rt), :] = gc_ref[pl.ds(t * rt, rt), :] * hin_ref[pl.ds(t * rt, rt), :]
            w0, w1, w2 = cw_ref[0:1, :], cw_ref[1:2, :], cw_ref[2:3, :]
            for t in range(nrt):
                r0 = PAD + t * rt
                y = w0 * buf_a[pl.ds(r0 - 2, rt), :] + w1 * buf_a[pl.ds(r0 - 1, rt), :] + w2 * buf_a[pl.ds(r0, rt), :]
                o_ref[pl.ds(t * rt, rt), :] = (gb_ref[pl.ds(t * rt, rt), :] * y).astype(BF16)

        @pl.when(g >= 2)
        def _pool():
            jj = g - 2
            lo = _lo_mask()
            _pool_window_sums(u_ref, buf_a, buf_b, jj, s, rt)
            wb = wbd_ref[...]
            for t in range(nrt):
                r0 = PAD + t * rt
                wsum = jnp.where(lo, buf_b[pl.ds(r0, rt), :], buf_a[pl.ds(r0, rt), :])
                m = wsum / _pool_counts(jj, lo, t * rt, rt) - u_ref[pl.ds(t * rt, rt), :]
                o_ref[pl.ds(t * rt, rt), :] = (_nn(m.astype(BF16), wb) * ps_ref[...]).astype(BF16)

    return pl.pallas_call(
        body, name="conv_pool_fwd", grid=(4,),
        in_specs=_cp_in_specs(s, l) + [pl.BlockSpec(memory_space=pl.ANY)],
        out_specs=pl.BlockSpec((s, LANE), lambda g: (0, AW // LANE + g)),
        out_shape=SDS((s, D), BF16),
        scratch_shapes=[pltpu.VMEM((s + 2 * PAD, LANE), F32), pltpu.VMEM((s + 2 * PAD, LANE), F32)],
        input_output_aliases={7: 0},
        compiler_params=_cp(),
    )(p, p, p, p, conv_w, wbd, pscale, mix)


def _mlp_fwd(x, mix, wout, g2, w1_t, w2, l, xchg=None):
    s = x.shape[0]
    t = min(256, s)

    def body(x_ref, mix_ref, wo_ref, g_ref, w1_ref, w2_ref, xm_ref, a_ref, xo_ref):
        xm = x_ref[...] + _nn(mix_ref[...], wo_ref[...])
        xm_ref[...] = xm
        r = lax.rsqrt(jnp.mean(xm * xm, axis=-1, keepdims=True) + EPS)
        h2 = (xm * r * g_ref[...]).astype(BF16)
        a = _nt(h2, w1_ref[...])
        a_ref[...] = a.astype(BF16)
        f = jnp.square(jnp.maximum(a, 0.0)).astype(BF16)
        xo_ref[...] = xm + _nn(f, w2_ref[...])

    row = lambda c: pl.BlockSpec((t, c), lambda i: (i, 0))
    return _hosted_call(
        body, xchg, name="mlp_fwd", grid=(s // t,),
        in_specs=[row(D), row(D), _const((D, D)), _layer((1, D), l), _const((DFF, D)), _const((DFF, D))],
        out_specs=[row(D), row(DFF), row(D)],
        out_shape=[SDS((s, D), F32), SDS((s, DFF), BF16), SDS((s, D), F32)], args=[x, mix, wout, g2, w1_t, w2])


def _loss_grad(y, target):
    s = y.shape[0]
    t = min(512, s)

    def body(y_ref, t_ref, dy_ref, acc_ref):
        @pl.when(pl.program_id(0) == 0)
        def _():
            acc_ref[...] = jnp.zeros_like(acc_ref)
        e = y_ref[...] - t_ref[...]
        dy_ref[...] = e * (1.0 / D)
        acc_ref[...] += jnp.sum(e * e)

    row = pl.BlockSpec((t, D), lambda i: (i, 0))
    return pl.pallas_call(
        body, name="loss_grad", grid=(s // t,),
        in_specs=[row, row],
        out_specs=[row, pl.BlockSpec((8, LANE), lambda i: (0, 0))],
        out_shape=[SDS((s, D), F32), SDS((8, LANE), F32)],
        compiler_params=_cp(),
    )(y, target)


def _mlp_bwd(dxo, a, xm, g2, w1_t, w2, wout, l, xchg=None):
    s = dxo.shape[0]
    t = min(256, s)

    def body(dxo_ref, a_ref, xm_ref, g_ref, w1_ref, w2_ref, wo_ref,
             dxm_ref, dmix_ref, f_ref, da_ref, h2_ref, dxob_ref, dxmb_ref, dg_ref):
        @pl.when(pl.program_id(0) == 0)
        def _():
            dg_ref[...] = jnp.zeros_like(dg_ref)
        dxo = dxo_ref[...]
        dxob = dxo.astype(BF16)
        dxob_ref[...] = dxob
        ra = jnp.maximum(a_ref[...].astype(F32), 0.0)
        f_ref[...] = jnp.square(ra).astype(BF16)
        dab = (_nt(dxob, w2_ref[...]) * (2.0 * ra)).astype(BF16)
        da_ref[...] = dab
        dh2 = _nn(dab, w1_ref[...])
        xm = xm_ref[...]
        g = g_ref[...]
        r = lax.rsqrt(jnp.mean(xm * xm, axis=-1, keepdims=True) + EPS)
        h2_ref[...] = (xm * r * g).astype(BF16)
        dx_n, dgr = _rms_bwd(dh2, xm, g)
        dg_ref[...] += jnp.sum(dgr, axis=0, keepdims=True)
        dxm = dxo + dx_n
        dxm_ref[...] = dxm
        dxmb = dxm.astype(BF16)
        dxmb_ref[...] = dxmb
        dmix_ref[...] = _nt(dxmb, wo_ref[...])

    row = lambda c: pl.BlockSpec((t, c), lambda i: (i, 0))
    return _hosted_call(
        body, xchg, name="mlp_bwd", grid=(s // t,),
        in_specs=[row(D), row(DFF), row(D), _layer((1, D), l), _const((DFF, D)), _const((DFF, D)), _const((D, D))],
        out_specs=[row(D), row(D), row(DFF), row(DFF), row(D), row(D), row(D), pl.BlockSpec((1, D), lambda i: (0, 0))],
        out_shape=[SDS((s, D), F32), SDS((s, D), F32), SDS((s, DFF), BF16), SDS((s, DFF), BF16),
                   SDS((s, D), BF16), SDS((s, D), BF16), SDS((s, D), BF16), SDS((1, D), F32)],
        args=[dxo, a, xm, g2, w1_t, w2, wout])


def _attn_bwd(qkv, p, lse, o32, dmix, qg2, kg2, bias, l, xchg=None):
    s = p.shape[0]
    nq = s // TQ
    scale = HD ** -0.5

    def body(qs_ref, kb_ref, vb_ref, q_ref, k_ref, qg_ref, kg_ref, b_ref, lse_ref, o_ref, do_ref,
             dq_ref, dk_ref, dv_ref, db_ref, dqg_ref, dkg_ref,
             dk_acc, dv_acc, s_ref, dp_ref, ds_ref, pb_ref, dqn_ref, dl_ref):
        i = pl.program_id(1)
        kt = jnp.maximum(i - BAND // TQ, 0)
        ks = pl.multiple_of(kt * TQ, TQ)
        lo = _lo_mask()

        @pl.when(i == 0)
        def _():
            dk_acc[...] = jnp.zeros_like(dk_acc)
            dv_acc[...] = jnp.zeros_like(dv_acc)
            dqg_ref[...] = jnp.zeros_like(dqg_ref)
            dkg_ref[...] = jnp.zeros_like(dkg_ref)

        @pl.when(i < NVAR)
        def _():
            db_ref[...] = jnp.zeros_like(db_ref)

        qs = qs_ref[...]
        kwin = kb_ref[pl.ds(ks, WIN), :]
        vwin = vb_ref[pl.ds(ks, WIN), :]
        do = do_ref[...]
        dob = do.astype(BF16)
        dl_ref[...] = _half_sum(do * o_ref[...], lo)
        for half in range(2):
            m_ = lo if half == 0 else jnp.logical_not(lo)
            qa = jnp.where(m_, qs, jnp.zeros_like(qs))
            doa = jnp.where(m_, dob, jnp.zeros_like(dob))
            s_ref[half] = _nt(qa, kwin)
            dp_ref[half] = _nt(doa, vwin)
            for r0 in range(0, TQ, RB_SOFT):
                rows = pl.ds(r0, RB_SOFT)
                lse_h = lse_ref[rows, half * HD:half * HD + 1]
                pm = jnp.exp(s_ref[half, rows, :] + b_ref[half, rows, :] - lse_h)
                ds = pm * (dp_ref[half, rows, :] - dl_ref[rows, half * HD:half * HD + 1])
                db_ref[half, rows, :] += ds
                ds_ref[half, rows, :] = ds.astype(BF16)
                pb_ref[half, rows, :] = pm.astype(BF16)
            dsb = ds_ref[half]
            dq_h = _nn(dsb, kwin)
            if half == 0:
                dqn_ref[...] = dq_h
            else:
                dqn_ref[...] = jnp.where(lo, dqn_ref[...], dq_h)
            dk_t = _tn(qa, dsb)
            dv_t = _tn(doa, pb_ref[half])
            for t in range(WIN // TQ):
                dk_acc[kt + t] += dk_t[:, t * TQ:(t + 1) * TQ]
                dv_acc[kt + t] += dv_t[:, t * TQ:(t + 1) * TQ]
        qg, kg = qg_ref[...], kg_ref[...]
        xq, rq = _head_norm(q_ref[...], lo)
        dq, dqg_rows = _head_norm_bwd(dqn_ref[...] * scale, xq, rq, qg, lo)
        dq_ref[...] = dq.astype(BF16)
        dqg_ref[...] += jnp.sum(dqg_rows, axis=0, keepdims=True)

        @pl.when(i == nq - 1)
        def _():
            dkg = jnp.zeros((1, LANE), F32)
            for t in range(nq):
                rows = pl.ds(t * TQ, TQ)
                xk, rk = _head_norm(k_ref[rows, :], lo)
                dk, dkg_rows = _head_norm_bwd(dk_acc[t].T, xk, rk, kg, lo)
                dk_ref[rows, :] = dk.astype(BF16)
                dv_ref[rows, :] = dv_acc[t].T.astype(BF16)
                dkg = dkg + jnp.sum(dkg_rows, axis=0, keepdims=True)
            dkg_ref[...] = dkg

    tile = pl.BlockSpec((TQ, LANE), lambda j, i: (i, j))
    kcol = lambda c0: pl.BlockSpec((s, LANE), lambda j, i: (0, c0 + j))
    gain = pl.BlockSpec((None, 1, LANE), lambda j, i: (j, 0, 0))
    return _hosted_call(
        body, xchg, name="attn_bwd", grid=(NH // 2, nq),
        in_specs=[
            tile, kcol(AW // LANE), kcol(2 * AW // LANE), tile, kcol(AW // LANE),
            _layer((1, LANE), l), _layer((1, LANE), l),
            _bias_layer_spec(l), tile, tile, tile,
        ],
        out_specs=[tile, kcol(0), kcol(0), _bias_spec(), gain, gain],
        out_shape=[SDS((s, AW), BF16), SDS((s, AW), BF16), SDS((s, AW), BF16),
                   SDS((NVAR, NH, TQ, WIN), F32), SDS((NH // 2, 1, LANE), F32), SDS((NH // 2, 1, LANE), F32)],
        scratch_shapes=[pltpu.VMEM((nq, LANE, TQ), F32), pltpu.VMEM((nq, LANE, TQ), F32),
                        pltpu.VMEM((2, TQ, WIN), F32), pltpu.VMEM((2, TQ, WIN), F32),
                        pltpu.VMEM((2, TQ, WIN), BF16), pltpu.VMEM((2, TQ, WIN), BF16),
                        pltpu.VMEM((TQ, LANE), F32), pltpu.VMEM((TQ, LANE), F32)],
        args=[qkv, qkv, qkv, p, p, qg2, kg2, bias, lse, o32, dmix])


def _conv_pool_bwd(p, dmix, conv_w, wbd, pscale, l):
    s = p.shape[0]
    rt = min(256, s)
    nrt = s // rt

    def body(gb_ref, gc_ref, hin_ref, u_ref, cw_ref, wbd_ref, ps_ref, dy_ref,
             dgb_ref, dgc_ref, dhin_ref, du_ref, dcw_ref, dwbd_ref, dps_ref, buf_a, buf_b, buf_c, buf_d):
        g = pl.program_id(0)
        zpad = jnp.zeros((PAD, LANE), F32)
        for buf in (buf_a, buf_b, buf_c):
            buf[pl.ds(0, PAD), :] = zpad
            buf[pl.ds(PAD + s, PAD), :] = zpad

        @pl.when(g < 2)
        def _conv():
            for t in range(nrt):
                rows = pl.ds(t * rt, rt)
                buf_a[pl.ds(PAD + t * rt, rt), :] = gc_ref[rows, :] * hin_ref[rows, :]
                buf_b[pl.ds(PAD + t * rt, rt), :] = dy_ref[rows, :] * gb_ref[rows, :]
            w0, w1, w2 = cw_ref[0:1, :], cw_ref[1:2, :], cw_ref[2:3, :]
            d0 = jnp.zeros((1, LANE), F32)
            d1 = jnp.zeros((1, LANE), F32)
            d2 = jnp.zeros((1, LANE), F32)
            for t in range(nrt):
                rows = pl.ds(t * rt, rt)
                r0 = PAD + t * rt
                z2, z1, z0 = buf_a[pl.ds(r0 - 2, rt), :], buf_a[pl.ds(r0 - 1, rt), :], buf_a[pl.ds(r0, rt), :]
                y = w0 * z2 + w1 * z1 + w2 * z0
                dgb_ref[rows, :] = (dy_ref[rows, :] * y).astype(BF16)
                e0 = buf_b[pl.ds(r0, rt), :]
                d0 = d0 + jnp.sum(e0 * z2, axis=0, keepdims=True)
                d1 = d1 + jnp.sum(e0 * z1, axis=0, keepdims=True)
                d2 = d2 + jnp.sum(e0 * z0, axis=0, keepdims=True)
                dz = w2 * e0 + w1 * buf_b[pl.ds(r0 + 1, rt), :] + w0 * buf_b[pl.ds(r0 + 2, rt), :]
                dgc_ref[rows, :] = (dz * hin_ref[rows, :]).astype(BF16)
                dhin_ref[rows, :] = (dz * gc_ref[rows, :]).astype(BF16)
            dcw_ref[0:1, :] = d0
            dcw_ref[1:2, :] = d1
            dcw_ref[2:3, :] = d2

        @pl.when(g >= 2)
        def _pool():
            jj = g - 2
            lo = _lo_mask()
            _pool_window_sums(u_ref, buf_a, buf_b, jj, s, rt)
            wb = wbd_ref[...]
            ps = ps_ref[...]
            dps = jnp.zeros((1, LANE), F32)
            dwb = jnp.zeros((LANE, LANE), F32)
            for t in range(nrt):
                rows = pl.ds(t * rt, rt)
                r0 = PAD + t * rt
                cnt = _pool_counts(jj, lo, t * rt, rt)
                wsum = jnp.where(lo, buf_b[pl.ds(r0, rt), :], buf_a[pl.ds(r0, rt), :])
                mb = (wsum / cnt - u_ref[rows, :]).astype(BF16)
                dy = dy_ref[rows, :]
                dps = dps + jnp.sum(dy * _nn(mb, wb), axis=0, keepdims=True)
                dmp = (dy * ps).astype(BF16)
                dwb = dwb + _tn(mb, dmp)
                dm = _nt(dmp, wb)
                buf_d[rows, :] = dm
                buf_c[pl.ds(r0, rt), :] = dm / cnt
            dps_ref[...] = dps
            dwbd_ref[...] = dwb

            def stage(src, dst, sh):
                for t in range(nrt):
                    r0 = PAD + t * rt
                    dst[pl.ds(r0, rt), :] = src[pl.ds(r0, rt), :] + src[pl.ds(r0 + sh, rt), :]

            def finish(first, second):
                for t in range(nrt):
                    rows = pl.ds(t * rt, rt)
                    r0 = PAD + t * rt
                    fw = jnp.where(lo, first[pl.ds(r0, rt), :], second[pl.ds(r0, rt), :])
                    du_ref[rows, :] = (fw - buf_d[rows, :]).astype(BF16)

            stage(buf_c, buf_a, 1)
            stage(buf_a, buf_b, 2)

            @pl.when(jj == 0)
            def _():
                finish(buf_a, buf_b)

            @pl.when(jj == 1)
            def _():
                stage(buf_b, buf_c, 4)
                stage(buf_c, buf_a, 8)
                finish(buf_c, buf_a)

    cblk = pl.BlockSpec((s, LANE), lambda g: (0, jnp.minimum(g, 1)))
    pblk = pl.BlockSpec((s, LANE), lambda g: (0, jnp.maximum(g - 2, 0)))
    padded = pltpu.VMEM((s + 2 * PAD, LANE), F32)
    return pl.pallas_call(
        body, name="conv_pool_bwd", grid=(4,),
        in_specs=_cp_in_specs(s, l) + [pl.BlockSpec((s, LANE), lambda g: (0, AW // LANE + g))],
        out_specs=[cblk, cblk, cblk, pblk,
                   pl.BlockSpec((3, LANE), lambda g: (0, jnp.minimum(g, 1))),
                   pl.BlockSpec((None, LANE, LANE), lambda g: (jnp.maximum(g - 2, 0), 0, 0)),
                   pl.BlockSpec((1, LANE), lambda g: (0, jnp.maximum(g - 2, 0)))],
        out_shape=[SDS((s, CW), BF16), SDS((s, CW), BF16), SDS((s, CW), BF16), SDS((s, PWD), BF16),
                   SDS((3, CW), F32), SDS((2, LANE, LANE), F32), SDS((1, PWD), F32)],
        scratch_shapes=[padded, padded, padded, pltpu.VMEM((s, LANE), F32)],
        compiler_params=_cp(),
    )(p, p, p, p, conv_w, wbd, pscale, dmix)


def _in_proj_bwd(parts, x, dxm, g1, win_t, l, xchg=None):
    s = x.shape[0]
    t = min(256, s)
    widths = [a.shape[1] for a in parts]
    offs = [int(o) for o in np.cumsum([0] + widths[:-1])]
    n = len(parts)

    def body(*refs):
        part_refs = refs[:n]
        x_ref, dxm_ref, g_ref, w_ref, dx_ref, dp_ref, dg_ref = refs[n:]

        @pl.when(pl.program_id(0) == 0)
        def _():
            dg_ref[...] = jnp.zeros_like(dg_ref)
        for r, o, w in zip(part_refs, offs, widths):
            dp_ref[:, o:o + w] = r[...]
        dh = _nn(dp_ref[...], w_ref[...])
        dx_n, dgr = _rms_bwd(dh, x_ref[...], g_ref[...])
        dg_ref[...] += jnp.sum(dgr, axis=0, keepdims=True)
        dx_ref[...] = dxm_ref[...] + dx_n

    row = lambda c: pl.BlockSpec((t, c), lambda i: (i, 0))
    return _hosted_call(
        body, xchg, name="in_proj_bwd", grid=(s // t,),
        in_specs=[row(w) for w in widths] + [row(D), row(D), _layer((1, D), l), _const((DIN, D))],
        out_specs=[row(D), row(DIN), pl.BlockSpec((1, D), lambda i: (0, 0))],
        out_shape=[SDS((s, D), F32), SDS((s, DIN), BF16), SDS((1, D), F32)], args=[*parts, x, dxm, g1, win_t])


def _wgrad(a, b, tag, xchg=None):
    s, m = a.shape
    mb = 512

    def body(a_ref, b_ref, o_ref):
        o_ref[...] = _tn(a_ref[...], b_ref[...]).astype(BF16)

    (out,), got = _hosted_call(
        body, xchg, name=f"wgrad_{tag}", grid=(m // mb,),
        in_specs=[pl.BlockSpec((s, mb), lambda mi: (0, mi)), _const((s, D))],
        out_specs=[pl.BlockSpec((mb, D), lambda mi: (mi, 0))],
        out_shape=[SDS((m, D), BF16)], args=[a, b])
    return out, got


def _bias_tables(gvec, xchg=None):
    def body(g_ref, o_ref):
        qc = lax.broadcasted_iota(jnp.int32, (TQ, WIN), 0) // CHUNK
        kc = lax.broadcasted_iota(jnp.int32, (TQ, WIN), 1) // CHUNK
        for var in range(NVAR):
            vec = jnp.broadcast_to(g_ref[:, var * TQ:var * TQ + NTOE], (TQ, NTOE))
            toe = pltpu.roll(vec, NTOE - TQ + 1, 1, stride=1, stride_axis=0)[:, :WIN]
            rel = (BAND - var * TQ) // CHUNK + qc - kc
            o_ref[var] = jnp.where((rel >= 0) & (rel <= N_PREV), toe, NEG)

    (out,), got = _hosted_call(
        body, xchg, name="bias_tables", grid=(L, NH),
        in_specs=[pl.BlockSpec((None, None, 1, NG), lambda l, h: (l, h, 0, 0))],
        out_specs=[pl.BlockSpec((None, NVAR, None, TQ, WIN), lambda l, h: (l, 0, h, 0, 0))],
        out_shape=[SDS((L, NVAR, NH, TQ, WIN), F32)], args=[gvec])
    return out, got


def _bias_tables_grad(dbias, l):
    nb = NTOE // LANE
    wb = WIN // LANE

    def body(d_ref, o_ref):
        ii = lax.broadcasted_iota(jnp.int32, (LANE, LANE), 0)
        jj = lax.broadcasted_iota(jnp.int32, (LANE, LANE), 1)
        flip = jnp.where(ii + jj == LANE - 1, 1.0, 0.0).astype(BF16)
        o_ref[...] = jnp.zeros_like(o_ref)
        for var in range(NVAR):
            blocks = []
            for b in range(nb):
                src = nb - 1 - b
                if src >= wb:
                    blocks.append(jnp.zeros((TQ, LANE), F32))
                    continue
                xv = d_ref[var, :, src * LANE:(src + 1) * LANE]
                hi = xv.astype(BF16)
                lo = (xv - hi.astype(F32)).astype(BF16)
                blocks.append(_nn(hi, flip) + _nn(lo, flip))
            rev = jnp.concatenate(blocks, axis=1)
            skew = pltpu.roll(rev, NTOE - TQ + 1, 1, stride=1, stride_axis=0)
            off = NG - NTOE - var * TQ
            o_ref[:, off:off + NTOE] += jnp.sum(skew, axis=0, keepdims=True)

    return pl.pallas_call(
        body, name=f"bias_tables_grad_l{l}", grid=(NH,),
        in_specs=[pl.BlockSpec((NVAR, None, TQ, WIN), lambda h: (0, h, 0, 0))],
        out_specs=pl.BlockSpec((None, 1, NG), lambda h: (h, 0, 0)),
        out_shape=SDS((NH, 1, NG), F32),
        compiler_params=_cp(),
    )(dbias)


_SIBLING = (0, 0, 1)
_CHIPS = [(1, 0, 0), (0, 1, 0), (1, 1, 0)]
_MASKS = [_SIBLING] + _CHIPS + [(1, 0, 1), (0, 1, 1), (1, 1, 1)]


def _position():
    return lax.axis_index("x"), lax.axis_index("y"), lax.axis_index("c")


def _peer(pos, mask):
    return tuple(1 - a if f else a for a, f in zip(pos, mask))


def _index(pos):
    return 4 * pos[0] + 2 * pos[1] + pos[2]


def _exchange_phases(items, src, dst, sems):
    send_sems, recv_sems, local_sems = sems
    me = _position()
    sib = _peer(me, _SIBLING)

    def remote(s_ref, d_ref, pi, n, to):
        return pltpu.make_async_remote_copy(
            src_ref=s_ref, dst_ref=d_ref, send_sem=send_sems.at[pi, n], recv_sem=recv_sems.at[pi, n],
            device_id=to, device_id_type=MESH_ID)

    def parts(n):
        it = items[n]
        r = SHARD_ROWS[it[1]]
        block = lambda ref, pos: ref.at[pl.ds(_index(pos) * r, r), :]
        if it[0] == "gather":
            own = src[n].at[it[2]]
            local = pltpu.make_async_copy(own, block(dst[n], me), local_sems.at[n])
            sends = [remote(own, block(dst[n], me), pi, n, _peer(me, m)) for pi, m in enumerate([_SIBLING] + _CHIPS)]
            hops = [(remote(block(dst[n], _peer(me, m)), block(dst[n], _peer(me, m)), 1 + j, n, _peer(me, m)),
                     remote(block(dst[n], _peer(me, m)), block(dst[n], _peer(me, m)), 4 + j, n, sib))
                    for j, m in enumerate(_CHIPS)]
            lands = [remote(own, block(dst[n], sib), 0, n, sib)]
            lands += [remote(own, block(dst[n], _peer(sib, m)), 4 + j, n, sib) for j, m in enumerate(_CHIPS)]
        else:
            local = pltpu.make_async_copy(block(src[n], me), dst[n].at[_index(me)], local_sems.at[n])
            sends = [remote(block(src[n], _peer(me, m)), dst[n].at[_index(me)], pi, n, _peer(me, m))
                     for pi, m in enumerate(_MASKS)]
            hops = []
            lands = [remote(block(src[n], me), dst[n].at[_index(_peer(me, m))], pi, n, _peer(me, m))
                     for pi, m in enumerate(_MASKS)]
        return local, sends, hops, lands

    def start():
        for n in range(len(items)):
            local, sends, _, _ = parts(n)
            local.start()
            for cp in sends:
                cp.start()

    def relay():
        for n in range(len(items)):
            for arrived, onward in parts(n)[2]:
                arrived.wait_recv()
                onward.start()

    def finish():
        for n in range(len(items)):
            local, sends, hops, lands = parts(n)
            for cp in lands:
                cp.wait_recv()
            for cp in sends + [onward for _, onward in hops]:
                cp.wait_send()
            local.wait()

    return start, relay, finish


def _hosted_call(body, xchg, *, name, grid, in_specs, out_specs, out_shape, args, scratch_shapes=(), relay_at=0.8):
    if not xchg:
        outs = pl.pallas_call(
            body, name=name, grid=grid, in_specs=list(in_specs), out_specs=list(out_specs),
            out_shape=list(out_shape), scratch_shapes=list(scratch_shapes), compiler_params=_cp())(*args)
        return outs, []
    items = [it for it, _ in xchg]
    n_in, n_out, n_scr, nit = len(args), len(out_shape), len(scratch_shapes), len(items)
    hbm = pl.BlockSpec(memory_space=pl.ANY)
    steps = int(np.prod(grid))
    relay_step = min(int(relay_at * steps), steps - 1)

    def dst_shape(it):
        r = SHARD_ROWS[it[1]]
        return SDS((NDEV * r, D) if it[0] == "gather" else (NDEV, r, D), BF16)

    def wrapped(*refs):
        ins = refs[:n_in]
        src = refs[n_in:n_in + nit]
        outs = refs[n_in + nit:n_in + nit + n_out]
        dst = refs[n_in + nit + n_out:n_in + 2 * nit + n_out]
        scratch = refs[n_in + 2 * nit + n_out:n_in + 2 * nit + n_out + n_scr]
        start, relay, finish = _exchange_phases(items, src, dst, refs[n_in + 2 * nit + n_out + n_scr:])
        step = 0
        for d, g in enumerate(grid):
            step = step * g + pl.program_id(d)
        pl.when(step == 0)(start)
        body(*ins, *outs, *scratch)
        pl.when(step == relay_step)(relay)
        pl.when(step == steps - 1)(finish)

    npeer = len(_MASKS)
    res = pl.pallas_call(
        wrapped, name=name, grid=grid,
        in_specs=list(in_specs) + [hbm] * nit,
        out_specs=list(out_specs) + [hbm] * nit,
        out_shape=list(out_shape) + [dst_shape(it) for it in items],
        scratch_shapes=list(scratch_shapes) + [
            pltpu.SemaphoreType.DMA((npeer, nit)), pltpu.SemaphoreType.DMA((npeer, nit)), pltpu.SemaphoreType.DMA((nit,))],
        compiler_params=_cp(),
    )(*args, *[a for _, a in xchg])
    return list(res[:n_out]), list(res[n_out:])


def _sum_slots(slots, xchg=None):
    _, r, _ = slots[0].shape
    n = len(slots) // L
    rt = 64

    def body(*refs):
        for k in range(n):
            for l in range(L):
                src = refs[k * L + l]
                acc = src[0].astype(F32)
                for d in range(1, NDEV):
                    acc = acc + src[d].astype(F32)
                refs[n * L + k][l] = acc

    return _hosted_call(
        body, xchg, name=f"sum_slots_r{r}" + ("_x" if xchg else ""), grid=(r // rt,),
        in_specs=[pl.BlockSpec((NDEV, rt, D), lambda i: (0, i, 0))] * (n * L),
        out_specs=[pl.BlockSpec((L, rt, D), lambda i: (0, i, 0))] * n,
        out_shape=[SDS((L, r, D), F32)] * n, args=list(slots))


def _exchange_small(v, reduce):
    rows = v.shape[0]

    def body(v_ref, o_ref, *scratch):
        if reduce:
            slots, send_sems, recv_sems = scratch
        else:
            slots = o_ref
            send_sems, recv_sems = scratch
        me = _position()
        sib = _peer(me, _SIBLING)

        def remote(s_ref, pos, pi, to):
            return pltpu.make_async_remote_copy(
                src_ref=s_ref, dst_ref=slots.at[_index(pos)], send_sem=send_sems.at[pi], recv_sem=recv_sems.at[pi],
                device_id=to, device_id_type=MESH_ID)

        slots[_index(me)] = v_ref[...]
        sends = [remote(v_ref, me, pi, _peer(me, m)) for pi, m in enumerate([_SIBLING] + _CHIPS)]
        for cp in sends:
            cp.start()
        for j, m in enumerate(_CHIPS):
            peer = _peer(me, m)
            remote(v_ref, peer, 1 + j, peer).wait_recv()
            onward = remote(slots.at[_index(peer)], peer, 4 + j, sib)
            onward.start()
            sends.append(onward)
        remote(v_ref, sib, 0, sib).wait_recv()
        for j, m in enumerate(_CHIPS):
            remote(v_ref, _peer(sib, m), 4 + j, sib).wait_recv()
        for cp in sends:
            cp.wait_send()
        if reduce:
            acc = slots[0]
            for d in range(1, NDEV):
                acc = acc + slots[d]
            o_ref[...] = acc

    vm = pl.BlockSpec(memory_space=pltpu.VMEM)
    sems = [pltpu.SemaphoreType.DMA((len(_MASKS),)), pltpu.SemaphoreType.DMA((len(_MASKS),))]
    return pl.pallas_call(
        body, name="reduce_small" if reduce else "gather_small",
        in_specs=[vm], out_specs=vm,
        out_shape=SDS((rows, LANE) if reduce else (NDEV, rows, LANE), F32),
        scratch_shapes=([pltpu.VMEM((NDEV, rows, LANE), F32)] if reduce else []) + sems,
        compiler_params=_cp(),
    )(v)


def _adamw_update(w_ref, g_ref, m_ref, v_ref, d_ref, nm_ref, nv_ref):
    gv = g_ref[...]
    mn = B1 * m_ref[...] + (1.0 - B1) * gv
    vn = B2 * v_ref[...] + (1.0 - B2) * jnp.square(gv)
    nm_ref[...] = mn
    nv_ref[...] = vn
    m_hat = mn / (1.0 - B1 ** STEP)
    v_hat = vn / (1.0 - B2 ** STEP)
    d_ref[...] = -LR * (m_hat / (jnp.sqrt(v_hat) + AEPS) + WD * w_ref[...])


def _adamw_small(ws, gs, ms, vs):
    n = len(ws)

    def body(*refs):
        for i in range(n):
            _adamw_update(*[refs[j * n + i] for j in range(7)])

    vm = pl.BlockSpec(memory_space=pltpu.VMEM)
    res = pl.pallas_call(
        body, name="adamw_small", in_specs=[vm] * (4 * n), out_specs=[vm] * (3 * n),
        out_shape=[SDS(w.shape, F32) for _ in range(3) for w in ws],
        compiler_params=_cp(),
    )(*ws, *gs, *ms, *vs)
    return res[:n], res[n:2 * n], res[2 * n:]


def _adamw(w, g, m, v):
    rows, cols = w.shape
    t = rows
    for cand in (512, 256, 128, 64, 32, 16, 8):
        if rows % cand == 0:
            t = cand
            break

    def body(*refs):
        _adamw_update(*refs)

    blk = pl.BlockSpec((t, cols), lambda i: (i, 0))
    return pl.pallas_call(
        body, name=f"adamw_{rows}x{cols}", grid=(rows // t,),
        in_specs=[blk] * 4, out_specs=[blk] * 3,
        out_shape=[SDS((rows, cols), F32)] * 3,
        compiler_params=_cp(),
    )(w, g, m, v)


_DIST0 = BAND + TQ - 1
_N_FAR = _DIST0 - REL_CLIP + 1
_N_NEAR = NG - _N_FAR - (2 * REL_CLIP - 1)


def _bias_vector(rel_bias):
    far = jnp.broadcast_to(rel_bias[..., -1:], (L, NH, _N_FAR))
    near = jnp.broadcast_to(rel_bias[..., :1], (L, NH, _N_NEAR))
    return jnp.concatenate([far, lax.rev(rel_bias[..., 1:-1], (2,)), near], axis=2)[:, :, None, :]


def _bias_vector_grad(dgr):
    first = jnp.sum(dgr[..., :_N_NEAR], axis=-1, keepdims=True)
    last = jnp.sum(dgr[..., NG - _N_FAR:], axis=-1, keepdims=True)
    return jnp.concatenate([first, dgr[..., _N_NEAR:NG - _N_FAR], last], axis=-1)


def _pool_blockdiag(pool_w):
    eye = jnp.eye(2, dtype=F32)
    pw = pool_w.reshape(L, 2, 2, HD, HD)
    return jnp.einsum("ljaik,ab->ljaibk", pw, eye).reshape(L, 2, LANE, LANE)


def _pool_blockdiag_grad(dwbd):
    d = dwbd.reshape(L, 2, 2, HD, 2, HD)
    return jnp.stack([d[:, :, 0, :, 0, :], d[:, :, 1, :, 1, :]], axis=2).reshape(L, 4, HD, HD)


def _pack(arrays, rows):
    flat = jnp.concatenate([a.reshape(-1).astype(F32) for a in arrays])
    return jnp.pad(flat, (0, rows * LANE - flat.shape[0])).reshape(rows, LANE)


def _unpack(packed, shapes):
    flat = packed.reshape(-1)
    out, o = [], 0
    for shp in shapes:
        n = int(np.prod(shp))
        out.append(flat[o:o + n].reshape(shp))
        o += n
    return out


def _rows_for(shapes):
    n = sum(int(np.prod(s)) for s in shapes)
    return -(-n // (8 * LANE)) * 8


def _grads(x, target, small_w, shards):
    g1, qg, kg, rb, cw, pw, ps, g2 = small_w
    g1 = g1.reshape(L, 1, D)
    g2 = g2.reshape(L, 1, D)
    qg2 = jnp.tile(qg, (1, 2)).reshape(L, 1, LANE)
    kg2 = jnp.tile(kg, (1, 2)).reshape(L, 1, LANE)
    ps3 = ps.reshape(L, 1, PWD)
    wbd = _pool_blockdiag(pw).astype(BF16)

    def gather(*kl):
        return [(("gather", k, l), shards[k]) for k, l in kl if l < L]

    full = {}

    def arrived(got, *kl):
        full.update(zip([x for x in kl if x[1] < L], got))

    bias, got = _bias_tables(_bias_vector(rb), gather((0, 0)))
    arrived(got, (0, 0))
    saved = []
    h = x
    for l in range(L):
        kl = ((1, l),)
        (p, h_b, qkv), got = _in_proj(h, g1, full[0, l], qg2, kg2, l, gather(*kl))
        arrived(got, *kl)
        kl = ((2, 0), (3, 0)) if l == 0 else ((3, l),)
        (mix, lse, o32), got = _attn_fwd(qkv, bias, l, gather(*kl), relay_at=0.95 if l == 0 else 0.8)
        arrived(got, *kl)
        mix = _conv_pool_fwd(p, mix, cw, wbd, ps3, l)
        kl = ((0, l + 1), (2, l + 1))
        (xm, a, xo), got = _mlp_fwd(h, mix, full[1, l], g2, full[2, l], full[3, l], l, gather(*kl))
        arrived(got, *kl)
        saved.append((h, h_b, p, qkv, mix, lse, o32, xm, a))
        h = xo
    dx, sq = _loss_grad(h, target)

    grads = {}
    slots = {}

    def scatter(*kl):
        return [(("scatter", k), grads[k, l]) for k, l in kl if l < L]

    def left(got, *kl):
        slots.update(zip([x for x in kl if x[1] < L], got))

    per_layer = [None] * L
    for l in reversed(range(L)):
        x_in, h_b, p, qkv, mix, lse, o32, xm, a = saved[l]
        (dxm, dmix, f_b, da_b, h2_b, dxo_b, dxm_b, dg2), got = _mlp_bwd(
            dx, a, xm, g2, full[2, l], full[3, l], full[1, l], l, scatter((3, l + 1)))
        left(got, (3, l + 1))
        grads[1, l], _ = _wgrad(mix, dxm_b, f"w_out_l{l}")
        grads[2, l], got = _wgrad(da_b, h2_b, f"w_mlp1_l{l}", scatter((1, l)))
        left(got, (1, l))
        kl = ((2, 0),) if l == 0 else ()
        grads[3, l], got = _wgrad(f_b, dxo_b, f"w_mlp2_l{l}", scatter(*kl))
        left(got, *kl)
        kl = ((3, 0), (0, 1)) if l == 0 else ((2, l),)
        (dq, dk, dv, dbias, dqg, dkg), got = _attn_bwd(qkv, p, lse, o32, dmix, qg2, kg2, bias, l, scatter(*kl))
        left(got, *kl)
        dgb, dgc, dhin, du, dcw, dwbd, dps = _conv_pool_bwd(p, dmix, cw, wbd, ps3, l)
        (dx, dp_b, dg1), got = _in_proj_bwd(
            [dq, dk, dv, dgb, dgc, dhin, du], x_in, dxm, g1, full[0, l], l, scatter((0, l + 1)) if l else None)
        left(got, (0, l + 1))
        grads[0, l], _ = _wgrad(dp_b, h_b, f"w_in_l{l}")
        per_layer[l] = (dg1, dg2, dqg, dkg, _bias_tables_grad(dbias, l), dcw, dwbd, dps)
    (g_w1_t, g_w2), got = _sum_slots([slots[k, l] for k in (2, 3) for l in range(L)], scatter((0, 0)))
    left(got, (0, 0))
    sums = [_sum_slots([slots[k, l] for l in range(L)])[0][0] for k in (0, 1)] + [g_w1_t, g_w2]

    st = [jnp.stack([per_layer[l][k] for l in range(L)]) for k in range(8)]
    small = dict(
        g1=st[0].reshape(L, D), g2=st[1].reshape(L, D),
        qg=st[2].reshape(L, NH, HD).sum(1), kg=st[3].reshape(L, NH, HD).sum(1),
        rb=_bias_vector_grad(st[4].reshape(L, NH, NG)), cw=st[5], pw=_pool_blockdiag_grad(st[6]),
        ps=st[7].reshape(L, PWD))
    return sq, dx, sums, small


def kernel(x, norm1_g, w_in, q_norm_g, k_norm_g, rel_bias, conv_w, pool_w, pool_scale, w_out, norm2_g, w_mlp1, w_mlp2, loss_target, m_norm1_g, m_w_in, m_q_norm_g, m_k_norm_g, m_rel_bias, m_conv_w, m_pool_w, m_pool_scale, m_w_out, m_norm2_g, m_w_mlp1, m_w_mlp2, v_norm1_g, v_w_in, v_q_norm_g, v_k_norm_g, v_rel_bias, v_conv_w, v_pool_w, v_pool_scale, v_w_out, v_norm2_g, v_w_mlp1, v_w_mlp2):
    me = _index(_position())
    cshard = CW // NDEV

    shards = [jnp.swapaxes(w_in, 1, 2).astype(BF16), w_out.astype(BF16),
              jnp.swapaxes(w_mlp1, 1, 2).astype(BF16), w_mlp2.astype(BF16)]
    cw_all = _exchange_small(_pack([conv_w], 8), reduce=False)
    cw_full = jnp.concatenate(
        [cw_all[d].reshape(-1)[:L * 3 * cshard].reshape(L, 3, cshard) for d in range(NDEV)], axis=2)

    small_w = (norm1_g, q_norm_g, k_norm_g, rel_bias, cw_full, pool_w, pool_scale, norm2_g)
    sq, grad_x, (g_win_t, g_wout, g_w1_t, g_w2), small = _grads(x[0], loss_target[0], small_w, shards)
    g_w_in = jnp.swapaxes(g_win_t, 1, 2)
    g_w_mlp1 = jnp.swapaxes(g_w1_t, 1, 2)

    names = ("g1", "qg", "kg", "rb", "cw", "pw", "ps", "g2")
    gshapes = [(L, D), (L, HD), (L, HD), (L, NH, 2 * REL_CLIP + 1), (L, 3, CW), (L, 4, HD, HD), (L, PWD), (L, D)]
    garrs = [small[n] for n in names]
    rows = _rows_for(gshapes + [(1,)])
    total = _exchange_small(_pack(garrs + [sq[0, :1]], rows), reduce=True)
    g_g1, g_qg, g_kg, g_rb, g_cw_full, g_pw, g_ps, g_g2, sq_sum = _unpack(total, gshapes + [(1,)])
    loss = (0.5 / D) * sq_sum[0]
    g_cw = lax.dynamic_slice_in_dim(g_cw_full, me * cshard, cshard, axis=2)

    def big(w, g, m, v):
        shp = w.shape
        r = lambda a: a.reshape(-1, shp[-1])
        return [o.reshape(shp) for o in _adamw(r(w), r(g), r(m), r(v))]

    up_in = big(w_in, g_w_in, m_w_in, v_w_in)
    up_out = big(w_out, g_wout, m_w_out, v_w_out)
    up_1 = big(w_mlp1, g_w_mlp1, m_w_mlp1, v_w_mlp1)
    up_2 = big(w_mlp2, g_w2, m_w_mlp2, v_w_mlp2)

    sw = [norm1_g, q_norm_g, k_norm_g, rel_bias, conv_w, pool_w, pool_scale, norm2_g]
    sg = [g_g1, g_qg, g_kg, g_rb, g_cw, g_pw, g_ps, g_g2]
    sm = [m_norm1_g, m_q_norm_g, m_k_norm_g, m_rel_bias, m_conv_w, m_pool_w, m_pool_scale, m_norm2_g]
    sv = [v_norm1_g, v_q_norm_g, v_k_norm_g, v_rel_bias, v_conv_w, v_pool_w, v_pool_scale, v_norm2_g]
    s_delta, s_m, s_v = _adamw_small(sw, sg, sm, sv)

    def order(small_list, in_, out_, m1, m2):
        g1_, qg_, kg_, rb_, cw_, pw_, ps_, g2_ = small_list
        return [g1_, in_, qg_, kg_, rb_, cw_, pw_, ps_, out_, g2_, m1, m2]

    grads = order(sg, g_w_in, g_wout, g_w_mlp1, g_w2)
    deltas = order(s_delta, up_in[0], up_out[0], up_1[0], up_2[0])
    new_m = order(s_m, up_in[1], up_out[1], up_1[1], up_2[1])
    new_v = order(s_v, up_in[2], up_out[2], up_1[2], up_2[2])
    return (loss, grad_x[None], *grads, *deltas, *new_m, *new_v)
```

```python
import numpy as np
import jax
import jax.numpy as jnp
from jax import lax
from jax.experimental import pallas as pl
from jax.experimental.pallas import tpu as pltpu

F32 = jnp.float32
BF16 = jnp.bfloat16
SDS = jax.ShapeDtypeStruct
MESH_ID = pl.DeviceIdType.MESH

D = 1024
L = 4
CHUNK = 64
N_PREV = 8
HD = 64
NH = 8
AW = 512
CW = 256
PWD = 256
DIN = 2560
DFF = 4096
EPS = 1e-6
NEG = -1e30
REL_CLIP = 128
POOL_WINDOWS = (2, 4, 8, 16)
LR, B1, B2, AEPS, WD, STEP = 0.001, 0.9, 0.999, 1e-08, 0.01, 10

NDEV = 8
LANE = 128
BAND = N_PREV * CHUNK
TQ = 256
WIN = TQ + BAND
NVAR = BAND // TQ + 1
NTOE = -(-(WIN + TQ - 1) // LANE) * LANE
NG = (NVAR - 1) * TQ + NTOE
PAD = 16
RB_NORM = 64
RB_SOFT = 16
VMEM_LIMIT = 56 * 1024 * 1024
SHARD_ROWS = (DIN // NDEV, D // NDEV, DFF // NDEV, DFF // NDEV)

assert 2 * HD == LANE and NH * HD == AW and POOL_WINDOWS == (2, 4, 8, 16)
assert TQ % CHUNK == 0 and BAND % TQ == 0 and max(POOL_WINDOWS) <= PAD and all(r % 16 == 0 for r in SHARD_ROWS)


def _cp(**kw):
    return pltpu.CompilerParams(vmem_limit_bytes=VMEM_LIMIT, **kw)


def _nn(a, b):
    return jnp.dot(a, b, preferred_element_type=F32)


def _nt(a, b):
    return lax.dot_general(a, b, (((1,), (1,)), ((), ())), preferred_element_type=F32)


def _tn(a, b):
    return lax.dot_general(a, b, (((0,), (0,)), ((), ())), preferred_element_type=F32)


def _const(shape):
    n = len(shape)
    return pl.BlockSpec(shape, lambda *_: (0,) * n, pipeline_mode=pl.Buffered(1))


def _layer(shape, l):
    n = len(shape)
    return pl.BlockSpec((None,) + tuple(shape), lambda *_: (l,) + (0,) * n, pipeline_mode=pl.Buffered(1))


def _lo_mask():
    return lax.broadcasted_iota(jnp.int32, (1, LANE), 1) < HD


def _half_sum(t, lo):
    s_lo = jnp.sum(jnp.where(lo, t, 0.0), axis=-1, keepdims=True)
    s_hi = jnp.sum(jnp.where(lo, 0.0, t), axis=-1, keepdims=True)
    return jnp.where(lo, s_lo, s_hi)


def _head_norm(x, lo):
    r = lax.rsqrt(_half_sum(x * x, lo) * (1.0 / HD) + EPS)
    return x * r, r


def _head_norm_bwd(dy, xn, r, g, lo):
    dxn = dy * g
    mu = _half_sum(dxn * xn, lo) * (1.0 / HD)
    return r * (dxn - xn * mu), dy * xn


def _rms_bwd(dy, x, g):
    r = lax.rsqrt(jnp.mean(x * x, axis=-1, keepdims=True) + EPS)
    xn = x * r
    dxn = dy * g
    mu = jnp.mean(dxn * xn, axis=-1, keepdims=True)
    return r * (dxn - xn * mu), dy * xn


def _in_proj(x, g1, win_t, qg2, kg2, l, xchg=None):
    s = x.shape[0]
    t = min(512, s)
    nblk = AW // LANE

    def body(x_ref, g_ref, w_ref, qg_ref, kg_ref, p_ref, h_ref, qkv_ref):
        xv = x_ref[...]
        r = lax.rsqrt(jnp.mean(xv * xv, axis=-1, keepdims=True) + EPS)
        h = (xv * r * g_ref[...]).astype(BF16)
        h_ref[...] = h
        p_ref[...] = _nt(h, w_ref[...])
        lo = _lo_mask()
        gains = (qg_ref[...] * (HD ** -0.5), kg_ref[...])
        for r0 in range(0, t, RB_NORM):
            rows = pl.ds(r0, RB_NORM)
            for c in range(3 * nblk):
                cols = pl.ds(c * LANE, LANE)
                v = p_ref[rows, cols]
                if c < 2 * nblk:
                    v = _head_norm(v, lo)[0] * gains[c // nblk]
                qkv_ref[rows, cols] = v.astype(BF16)

    row = lambda c: pl.BlockSpec((t, c), lambda i: (i, 0))
    return _hosted_call(
        body, xchg, name="in_proj", grid=(s // t,),
        in_specs=[row(D), _layer((1, D), l), _const((DIN, D)), _layer((1, LANE), l), _layer((1, LANE), l)],
        out_specs=[row(DIN), row(D), row(3 * AW)],
        out_shape=[SDS((s, DIN), F32), SDS((s, D), BF16), SDS((s, 3 * AW), BF16)], args=[x, g1, win_t, qg2, kg2])


def _bias_spec():
    return pl.BlockSpec((None, 2, TQ, WIN), lambda j, i: (jnp.maximum(NVAR - 1 - i, 0), j, 0, 0))


def _bias_layer_spec(l):
    return pl.BlockSpec((None, None, 2, TQ, WIN), lambda j, i: (l, jnp.maximum(NVAR - 1 - i, 0), j, 0, 0))


def _attn_fwd(qkv, bias, l, xchg=None, relay_at=0.8):
    s = qkv.shape[0]
    nq = s // TQ

    def body(q_ref, k_ref, v_ref, b_ref, o_ref, lse_ref, o32_ref, s_ref, p_ref, m_ref, den_ref, o0_ref):
        i = pl.program_id(1)
        ks = pl.multiple_of(jnp.maximum(i * TQ - BAND, 0), TQ)
        lo = _lo_mask()
        q = q_ref[...]
        kwin = k_ref[pl.ds(ks, WIN), :]
        vwin = v_ref[pl.ds(ks, WIN), :]
        for half in range(2):
            m_ = lo if half == 0 else jnp.logical_not(lo)
            s_ref[half] = _nt(jnp.where(m_, q, jnp.zeros_like(q)), kwin)
            for r0 in range(0, TQ, RB_SOFT):
                rows = pl.ds(r0, RB_SOFT)
                mx = jnp.max(s_ref[half, rows, :] + b_ref[half, rows, :], axis=-1, keepdims=True)
                m_ref[rows, :] = jnp.broadcast_to(mx, (RB_SOFT, LANE))
            for r0 in range(0, TQ, RB_SOFT):
                rows = pl.ds(r0, RB_SOFT)
                mx = m_ref[rows, 0:1]
                e = jnp.exp(s_ref[half, rows, :] + b_ref[half, rows, :] - mx)
                p_ref[half, rows, :] = e.astype(BF16)
                den = jnp.sum(e, axis=-1, keepdims=True)
                den_ref[rows, :] = jnp.broadcast_to(den, (RB_SOFT, LANE))
                lse = mx + jnp.log(den)
                if half == 0:
                    lse_ref[rows, :] = jnp.broadcast_to(lse, (RB_SOFT, LANE))
                else:
                    lse_ref[rows, :] = jnp.where(lo, lse_ref[rows, :], lse)
            o = _nn(p_ref[half], vwin) * (1.0 / den_ref[...])
            if half == 0:
                o0_ref[...] = o
            else:
                o = jnp.where(lo, o0_ref[...], o)
                o32_ref[...] = o
                o_ref[...] = o.astype(BF16)

    tile = pl.BlockSpec((TQ, LANE), lambda j, i: (i, j))
    stat = pltpu.VMEM((TQ, LANE), F32)
    return _hosted_call(
        body, xchg, name="attn_fwd", grid=(NH // 2, nq),
        in_specs=[
            tile,
            pl.BlockSpec((s, LANE), lambda j, i: (0, AW // LANE + j)),
            pl.BlockSpec((s, LANE), lambda j, i: (0, 2 * AW // LANE + j)),
            _bias_layer_spec(l),
        ],
        out_specs=[tile, tile, tile],
        out_shape=[SDS((s, D), BF16), SDS((s, AW), F32), SDS((s, AW), F32)], args=[qkv, qkv, qkv, bias],
        scratch_shapes=[pltpu.VMEM((2, TQ, WIN), F32), pltpu.VMEM((2, TQ, WIN), BF16), stat, stat, stat],
        relay_at=relay_at)


_C0 = 3 * AW // LANE


def _cp_in_specs(s, l):
    blk = lambda f: pl.BlockSpec((s, LANE), f)
    return [
        blk(lambda g: (0, _C0 + jnp.minimum(g, 1))),
        blk(lambda g: (0, _C0 + 2 + jnp.minimum(g, 1))),
        blk(lambda g: (0, _C0 + 4 + jnp.minimum(g, 1))),
        blk(lambda g: (0, _C0 + 6 + jnp.maximum(g - 2, 0))),
        pl.BlockSpec((None, 3, LANE), lambda g: (l, 0, jnp.minimum(g, 1))),
        pl.BlockSpec((None, None, LANE, LANE), lambda g: (l, jnp.maximum(g - 2, 0), 0, 0)),
        pl.BlockSpec((None, 1, LANE), lambda g: (l, 0, jnp.maximum(g - 2, 0))),
    ]


def _pool_window_sums(u_ref, buf_a, buf_b, jj, s, rt):
    nrt = s // rt
    for t in range(nrt):
        buf_a[pl.ds(PAD + t * rt, rt), :] = u_ref[pl.ds(t * rt, rt), :]

    def stage(src, dst, sh):
        for t in range(nrt):
            r0 = PAD + t * rt
            dst[pl.ds(r0, rt), :] = src[pl.ds(r0, rt), :] + src[pl.ds(r0 - sh, rt), :]

    stage(buf_a, buf_b, 1)
    stage(buf_b, buf_a, 2)

    @pl.when(jj == 1)
    def _():
        stage(buf_a, buf_b, 4)
        stage(buf_b, buf_a, 8)


def _pool_counts(jj, lo, r0, rt):
    w0, w1, w2, w3 = [float(w) for w in POOL_WINDOWS]
    w = jnp.where(lo, jnp.where(jj == 0, w0, w2), jnp.where(jj == 0, w1, w3))
    pos1 = (lax.broadcasted_iota(jnp.int32, (rt, LANE), 0) + (r0 + 1)).astype(F32)
    return jnp.minimum(pos1, w)


def _conv_pool_fwd(p, mix, conv_w, wbd, pscale, l):
    s = p.shape[0]
    rt = min(256, s)
    nrt = s // rt

    def body(gb_ref, gc_ref, hin_ref, u_ref, cw_ref, wbd_ref, ps_ref, mix_in, o_ref, buf_a, buf_b):
        del mix_in
        g = pl.program_id(0)
        zpad = jnp.zeros((PAD, LANE), F32)
        buf_a[pl.ds(0, PAD), :] = zpad
        buf_b[pl.ds(0, PAD), :] = zpad

        @pl.when(g < 2)
        def _conv():
            for t in range(nrt):
                buf_a[pl.ds(PAD + t * rt, rt), :] = gc_ref[pl.ds(t * rt, rt), :] * hin_ref[pl.ds(t * rt, rt), :]
            w0, w1, w2 = cw_ref[0:1, :], cw_ref[1:2, :], cw_ref[2:3, :]
            for t in range(nrt):
                r0 = PAD + t * rt
                y = w0 * buf_a[pl.ds(r0 - 2, rt), :] + w1 * buf_a[pl.ds(r0 - 1, rt), :] + w2 * buf_a[pl.ds(r0, rt), :]
                o_ref[pl.ds(t * rt, rt), :] = (gb_ref[pl.ds(t * rt, rt), :] * y).astype(BF16)

        @pl.when(g >= 2)
        def _pool():
            jj = g - 2
            lo = _lo_mask()
            _pool_window_sums(u_ref, buf_a, buf_b, jj, s, rt)
            wb = wbd_ref[...]
            for t in range(nrt):
                r0 = PAD + t * rt
                wsum = jnp.where(lo, buf_b[pl.ds(r0, rt), :], buf_a[pl.ds(r0, rt), :])
                m = wsum / _pool_counts(jj, lo, t * rt, rt) - u_ref[pl.ds(t * rt, rt), :]
                o_ref[pl.ds(t * rt, rt), :] = (_nn(m.astype(BF16), wb) * ps_ref[...]).astype(BF16)

    return pl.pallas_call(
        body, name="conv_pool_fwd", grid=(4,),
        in_specs=_cp_in_specs(s, l) + [pl.BlockSpec(memory_space=pl.ANY)],
        out_specs=pl.BlockSpec((s, LANE), lambda g: (0, AW // LANE + g)),
        out_shape=SDS((s, D), BF16),
        scratch_shapes=[pltpu.VMEM((s + 2 * PAD, LANE), F32), pltpu.VMEM((s + 2 * PAD, LANE), F32)],
        input_output_aliases={7: 0},
        compiler_params=_cp(),
    )(p, p, p, p, conv_w, wbd, pscale, mix)


def _mlp_fwd(x, mix, wout, g2, w1_t, w2, l, xchg=None, target=None):
    s = x.shape[0]
    t = min(256, s)

    def body(*refs):
        x_ref, mix_ref, wo_ref, g_ref, w1_ref, w2_ref = refs[:6]
        xm_ref, a_ref, xo_ref = refs[-4:-1] if target is not None else refs[-3:]
        xm = x_ref[...] + _nn(mix_ref[...], wo_ref[...])
        xm_ref[...] = xm
        r = lax.rsqrt(jnp.mean(xm * xm, axis=-1, keepdims=True) + EPS)
        h2 = (xm * r * g_ref[...]).astype(BF16)
        a = _nt(h2, w1_ref[...])
        a_ref[...] = a.astype(BF16)
        f = jnp.square(jnp.maximum(a, 0.0)).astype(BF16)
        xo = xm + _nn(f, w2_ref[...])
        if target is None:
            xo_ref[...] = xo
        else:
            acc_ref = refs[-1]

            @pl.when(pl.program_id(0) == 0)
            def _():
                acc_ref[...] = jnp.zeros_like(acc_ref)
            e = xo - refs[6][...]
            xo_ref[...] = e * (1.0 / D)
            acc_ref[...] += jnp.sum(e * e)

    row = lambda c: pl.BlockSpec((t, c), lambda i: (i, 0))
    last = target is not None
    return _hosted_call(
        body, xchg, name="mlp_fwd_loss" if last else "mlp_fwd", grid=(s // t,),
        in_specs=[row(D), row(D), _const((D, D)), _layer((1, D), l), _const((DFF, D)), _const((DFF, D))] + [row(D)] * last,
        out_specs=[row(D), row(DFF), row(D)] + [pl.BlockSpec((8, LANE), lambda i: (0, 0))] * last,
        out_shape=[SDS((s, D), F32), SDS((s, DFF), BF16), SDS((s, D), F32)] + [SDS((8, LANE), F32)] * last,
        args=[x, mix, wout, g2, w1_t, w2] + [target] * last)


def _mlp_bwd(dxo, a, xm, g2, w1_t, w2, wout, l, xchg=None):
    s = dxo.shape[0]
    t = min(256, s)

    def body(dxo_ref, a_ref, xm_ref, g_ref, w1_ref, w2_ref, wo_ref,
             dxm_ref, dmix_ref, f_ref, da_ref, h2_ref, dxob_ref, dxmb_ref, dg_ref):
        @pl.when(pl.program_id(0) == 0)
        def _():
            dg_ref[...] = jnp.zeros_like(dg_ref)
        dxo = dxo_ref[...]
        dxob = dxo.astype(BF16)
        dxob_ref[...] = dxob
        ra = jnp.maximum(a_ref[...].astype(F32), 0.0)
        f_ref[...] = jnp.square(ra).astype(BF16)
        dab = (_nt(dxob, w2_ref[...]) * (2.0 * ra)).astype(BF16)
        da_ref[...] = dab
        dh2 = _nn(dab, w1_ref[...])
        xm = xm_ref[...]
        g = g_ref[...]
        r = lax.rsqrt(jnp.mean(xm * xm, axis=-1, keepdims=True) + EPS)
        h2_ref[...] = (xm * r * g).astype(BF16)
        dx_n, dgr = _rms_bwd(dh2, xm, g)
        dg_ref[...] += jnp.sum(dgr, axis=0, keepdims=True)
        dxm = dxo + dx_n
        dxm_ref[...] = dxm
        dxmb = dxm.astype(BF16)
        dxmb_ref[...] = dxmb
        dmix_ref[...] = _nt(dxmb, wo_ref[...])

    row = lambda c: pl.BlockSpec((t, c), lambda i: (i, 0))
    return _hosted_call(
        body, xchg, name="mlp_bwd", grid=(s // t,),
        in_specs=[row(D), row(DFF), row(D), _layer((1, D), l), _const((DFF, D)), _const((DFF, D)), _const((D, D))],
        out_specs=[row(D), row(D), row(DFF), row(DFF), row(D), row(D), row(D), pl.BlockSpec((1, D), lambda i: (0, 0))],
        out_shape=[SDS((s, D), F32), SDS((s, D), F32), SDS((s, DFF), BF16), SDS((s, DFF), BF16),
                   SDS((s, D), BF16), SDS((s, D), BF16), SDS((s, D), BF16), SDS((1, D), F32)],
        args=[dxo, a, xm, g2, w1_t, w2, wout])


def _attn_bwd(qkv, p, lse, o32, dmix, qg2, kg2, bias, l, xchg=None):
    s = p.shape[0]
    nq = s // TQ
    scale = HD ** -0.5

    def body(qs_ref, kb_ref, vb_ref, q_ref, k_ref, qg_ref, kg_ref, b_ref, lse_ref, o_ref, do_ref,
             dq_ref, dk_ref, dv_ref, db_ref, dqg_ref, dkg_ref,
             dk_acc, dv_acc, s_ref, dp_ref, ds_ref, pb_ref, dqn_ref, dl_ref):
        i = pl.program_id(1)
        kt = jnp.maximum(i - BAND // TQ, 0)
        ks = pl.multiple_of(kt * TQ, TQ)
        lo = _lo_mask()

        @pl.when(i == 0)
        def _():
            dk_acc[...] = jnp.zeros_like(dk_acc)
            dv_acc[...] = jnp.zeros_like(dv_acc)
            dqg_ref[...] = jnp.zeros_like(dqg_ref)
            dkg_ref[...] = jnp.zeros_like(dkg_ref)

        @pl.when(i < NVAR)
        def _():
            db_ref[...] = jnp.zeros_like(db_ref)

        qs = qs_ref[...]
        kwin = kb_ref[pl.ds(ks, WIN), :]
        vwin = vb_ref[pl.ds(ks, WIN), :]
        do = do_ref[...]
        dob = do.astype(BF16)
        dl_ref[...] = _half_sum(do * o_ref[...], lo)
        for half in range(2):
            m_ = lo if half == 0 else jnp.logical_not(lo)
            qa = jnp.where(m_, qs, jnp.zeros_like(qs))
            doa = jnp.where(m_, dob, jnp.zeros_like(dob))
            s_ref[half] = _nt(qa, kwin)
            dp_ref[half] = _nt(doa, vwin)
            for r0 in range(0, TQ, RB_SOFT):
                rows = pl.ds(r0, RB_SOFT)
                lse_h = lse_ref[rows, half * HD:half * HD + 1]
                pm = jnp.exp(s_ref[half, rows, :] + b_ref[half, rows, :] - lse_h)
                ds = pm * (dp_ref[half, rows, :] - dl_ref[rows, half * HD:half * HD + 1])
                db_ref[half, rows, :] += ds
                ds_ref[half, rows, :] = ds.astype(BF16)
                pb_ref[half, rows, :] = pm.astype(BF16)
            dsb = ds_ref[half]
            dq_h = _nn(dsb, kwin)
            if half == 0:
                dqn_ref[...] = dq_h
            else:
                dqn_ref[...] = jnp.where(lo, dqn_ref[...], dq_h)
            dk_t = _tn(qa, dsb)
            dv_t = _tn(doa, pb_ref[half])
            for t in range(WIN // TQ):
                dk_acc[kt + t] += dk_t[:, t * TQ:(t + 1) * TQ]
                dv_acc[kt + t] += dv_t[:, t * TQ:(t + 1) * TQ]
        qg, kg = qg_ref[...], kg_ref[...]
        xq, rq = _head_norm(q_ref[...], lo)
        dq, dqg_rows = _head_norm_bwd(dqn_ref[...] * scale, xq, rq, qg, lo)
        dq_ref[...] = dq.astype(BF16)
        dqg_ref[...] += jnp.sum(dqg_rows, axis=0, keepdims=True)

        @pl.when(i == nq - 1)
        def _():
            dkg = jnp.zeros((1, LANE), F32)
            for t in range(nq):
                rows = pl.ds(t * TQ, TQ)
                xk, rk = _head_norm(k_ref[rows, :], lo)
                dk, dkg_rows = _head_norm_bwd(dk_acc[t].T, xk, rk, kg, lo)
                dk_ref[rows, :] = dk.astype(BF16)
                dv_ref[rows, :] = dv_acc[t].T.astype(BF16)
                dkg = dkg + jnp.sum(dkg_rows, axis=0, keepdims=True)
            dkg_ref[...] = dkg

    tile = pl.BlockSpec((TQ, LANE), lambda j, i: (i, j))
    kcol = lambda c0: pl.BlockSpec((s, LANE), lambda j, i: (0, c0 + j))
    gain = pl.BlockSpec((None, 1, LANE), lambda j, i: (j, 0, 0))
    return _hosted_call(
        body, xchg, name="attn_bwd", grid=(NH // 2, nq),
        in_specs=[
            tile, kcol(AW // LANE), kcol(2 * AW // LANE), tile, kcol(AW // LANE),
            _layer((1, LANE), l), _layer((1, LANE), l),
            _bias_layer_spec(l), tile, tile, tile,
        ],
        out_specs=[tile, kcol(0), kcol(0), _bias_spec(), gain, gain],
        out_shape=[SDS((s, AW), BF16), SDS((s, AW), BF16), SDS((s, AW), BF16),
                   SDS((NVAR, NH, TQ, WIN), F32), SDS((NH // 2, 1, LANE), F32), SDS((NH // 2, 1, LANE), F32)],
        scratch_shapes=[pltpu.VMEM((nq, LANE, TQ), F32), pltpu.VMEM((nq, LANE, TQ), F32),
                        pltpu.VMEM((2, TQ, WIN), F32), pltpu.VMEM((2, TQ, WIN), F32),
                        pltpu.VMEM((2, TQ, WIN), BF16), pltpu.VMEM((2, TQ, WIN), BF16),
                        pltpu.VMEM((TQ, LANE), F32), pltpu.VMEM((TQ, LANE), F32)],
        args=[qkv, qkv, qkv, p, p, qg2, kg2, bias, lse, o32, dmix])


def _conv_pool_bwd(p, dmix, conv_w, wbd, pscale, l):
    s = p.shape[0]
    rt = min(256, s)
    nrt = s // rt

    def body(gb_ref, gc_ref, hin_ref, u_ref, cw_ref, wbd_ref, ps_ref, dy_ref,
             dgb_ref, dgc_ref, dhin_ref, du_ref, dcw_ref, dwbd_ref, dps_ref, buf_a, buf_b, buf_c, buf_d):
        g = pl.program_id(0)
        zpad = jnp.zeros((PAD, LANE), F32)
        for buf in (buf_a, buf_b, buf_c):
            buf[pl.ds(0, PAD), :] = zpad
            buf[pl.ds(PAD + s, PAD), :] = zpad

        @pl.when(g < 2)
        def _conv():
            for t in range(nrt):
                rows = pl.ds(t * rt, rt)
                buf_a[pl.ds(PAD + t * rt, rt), :] = gc_ref[rows, :] * hin_ref[rows, :]
                buf_b[pl.ds(PAD + t * rt, rt), :] = dy_ref[rows, :] * gb_ref[rows, :]
            w0, w1, w2 = cw_ref[0:1, :], cw_ref[1:2, :], cw_ref[2:3, :]
            d0 = jnp.zeros((1, LANE), F32)
            d1 = jnp.zeros((1, LANE), F32)
            d2 = jnp.zeros((1, LANE), F32)
            for t in range(nrt):
                rows = pl.ds(t * rt, rt)
                r0 = PAD + t * rt
                z2, z1, z0 = buf_a[pl.ds(r0 - 2, rt), :], buf_a[pl.ds(r0 - 1, rt), :], buf_a[pl.ds(r0, rt), :]
                y = w0 * z2 + w1 * z1 + w2 * z0
                dgb_ref[rows, :] = (dy_ref[rows, :] * y).astype(BF16)
                e0 = buf_b[pl.ds(r0, rt), :]
                d0 = d0 + jnp.sum(e0 * z2, axis=0, keepdims=True)
                d1 = d1 + jnp.sum(e0 * z1, axis=0, keepdims=True)
                d2 = d2 + jnp.sum(e0 * z0, axis=0, keepdims=True)
                dz = w2 * e0 + w1 * buf_b[pl.ds(r0 + 1, rt), :] + w0 * buf_b[pl.ds(r0 + 2, rt), :]
                dgc_ref[rows, :] = (dz * hin_ref[rows, :]).astype(BF16)
                dhin_ref[rows, :] = (dz * gc_ref[rows, :]).astype(BF16)
            dcw_ref[0:1, :] = d0
            dcw_ref[1:2, :] = d1
            dcw_ref[2:3, :] = d2

        @pl.when(g >= 2)
        def _pool():
            jj = g - 2
            lo = _lo_mask()
            _pool_window_sums(u_ref, buf_a, buf_b, jj, s, rt)
            wb = wbd_ref[...]
            ps = ps_ref[...]
            dps = jnp.zeros((1, LANE), F32)
            dwb = jnp.zeros((LANE, LANE), F32)
            for t in range(nrt):
                rows = pl.ds(t * rt, rt)
                r0 = PAD + t * rt
                cnt = _pool_counts(jj, lo, t * rt, rt)
                wsum = jnp.where(lo, buf_b[pl.ds(r0, rt), :], buf_a[pl.ds(r0, rt), :])
                mb = (wsum / cnt - u_ref[rows, :]).astype(BF16)
                dy = dy_ref[rows, :]
                dps = dps + jnp.sum(dy * _nn(mb, wb), axis=0, keepdims=True)
                dmp = (dy * ps).astype(BF16)
                dwb = dwb + _tn(mb, dmp)
                dm = _nt(dmp, wb)
                buf_d[rows, :] = dm
                buf_c[pl.ds(r0, rt), :] = dm / cnt
            dps_ref[...] = dps
            dwbd_ref[...] = dwb

            def stage(src, dst, sh):
                for t in range(nrt):
                    r0 = PAD + t * rt
                    dst[pl.ds(r0, rt), :] = src[pl.ds(r0, rt), :] + src[pl.ds(r0 + sh, rt), :]

            def finish(first, second):
                for t in range(nrt):
                    rows = pl.ds(t * rt, rt)
                    r0 = PAD + t * rt
                    fw = jnp.where(lo, first[pl.ds(r0, rt), :], second[pl.ds(r0, rt), :])
                    du_ref[rows, :] = (fw - buf_d[rows, :]).astype(BF16)

            stage(buf_c, buf_a, 1)
            stage(buf_a, buf_b, 2)

            @pl.when(jj == 0)
            def _():
                finish(buf_a, buf_b)

            @pl.when(jj == 1)
            def _():
                stage(buf_b, buf_c, 4)
                stage(buf_c, buf_a, 8)
                finish(buf_c, buf_a)

    cblk = pl.BlockSpec((s, LANE), lambda g: (0, jnp.minimum(g, 1)))
    pblk = pl.BlockSpec((s, LANE), lambda g: (0, jnp.maximum(g - 2, 0)))
    padded = pltpu.VMEM((s + 2 * PAD, LANE), F32)
    return pl.pallas_call(
        body, name="conv_pool_bwd", grid=(4,),
        in_specs=_cp_in_specs(s, l) + [pl.BlockSpec((s, LANE), lambda g: (0, AW // LANE + g))],
        out_specs=[cblk, cblk, cblk, pblk,
                   pl.BlockSpec((3, LANE), lambda g: (0, jnp.minimum(g, 1))),
                   pl.BlockSpec((None, LANE, LANE), lambda g: (jnp.maximum(g - 2, 0), 0, 0)),
                   pl.BlockSpec((1, LANE), lambda g: (0, jnp.maximum(g - 2, 0)))],
        out_shape=[SDS((s, CW), BF16), SDS((s, CW), BF16), SDS((s, CW), BF16), SDS((s, PWD), BF16),
                   SDS((3, CW), F32), SDS((2, LANE, LANE), F32), SDS((1, PWD), F32)],
        scratch_shapes=[padded, padded, padded, pltpu.VMEM((s, LANE), F32)],
        compiler_params=_cp(),
    )(p, p, p, p, conv_w, wbd, pscale, dmix)


def _in_proj_bwd(parts, x, dxm, g1, win_t, l, xchg=None):
    s = x.shape[0]
    t = min(256, s)
    widths = [a.shape[1] for a in parts]
    offs = [int(o) for o in np.cumsum([0] + widths[:-1])]
    n = len(parts)

    def body(*refs):
        part_refs = refs[:n]
        x_ref, dxm_ref, g_ref, w_ref, dx_ref, dp_ref, dg_ref = refs[n:]

        @pl.when(pl.program_id(0) == 0)
        def _():
            dg_ref[...] = jnp.zeros_like(dg_ref)
        for r, o, w in zip(part_refs, offs, widths):
            dp_ref[:, o:o + w] = r[...]
        dh = _nn(dp_ref[...], w_ref[...])
        dx_n, dgr = _rms_bwd(dh, x_ref[...], g_ref[...])
        dg_ref[...] += jnp.sum(dgr, axis=0, keepdims=True)
        dx_ref[...] = dxm_ref[...] + dx_n

    row = lambda c: pl.BlockSpec((t, c), lambda i: (i, 0))
    return _hosted_call(
        body, xchg, name="in_proj_bwd", grid=(s // t,),
        in_specs=[row(w) for w in widths] + [row(D), row(D), _layer((1, D), l), _const((DIN, D))],
        out_specs=[row(D), row(DIN), pl.BlockSpec((1, D), lambda i: (0, 0))],
        out_shape=[SDS((s, D), F32), SDS((s, DIN), BF16), SDS((1, D), F32)], args=[*parts, x, dxm, g1, win_t])


def _wgrad(a, b, tag, xchg=None):
    s, m = a.shape
    mb = 512

    def body(a_ref, b_ref, o_ref):
        o_ref[...] = _tn(a_ref[...], b_ref[...]).astype(BF16)

    (out,), got = _hosted_call(
        body, xchg, name=f"wgrad_{tag}", grid=(m // mb,),
        in_specs=[pl.BlockSpec((s, mb), lambda mi: (0, mi)), _const((s, D))],
        out_specs=[pl.BlockSpec((mb, D), lambda mi: (mi, 0))],
        out_shape=[SDS((m, D), BF16)], args=[a, b])
    return out, got


def _bias_tables(gvec, xchg=None):
    def body(g_ref, o_ref):
        qc = lax.broadcasted_iota(jnp.int32, (TQ, WIN), 0) // CHUNK
        kc = lax.broadcasted_iota(jnp.int32, (TQ, WIN), 1) // CHUNK
        for var in range(NVAR):
            vec = jnp.broadcast_to(g_ref[:, var * TQ:var * TQ + NTOE], (TQ, NTOE))
            toe = pltpu.roll(vec, NTOE - TQ + 1, 1, stride=1, stride_axis=0)[:, :WIN]
            rel = (BAND - var * TQ) // CHUNK + qc - kc
            o_ref[var] = jnp.where((rel >= 0) & (rel <= N_PREV), toe, NEG)

    (out,), got = _hosted_call(
        body, xchg, name="bias_tables", grid=(L, NH),
        in_specs=[pl.BlockSpec((None, None, 1, NG), lambda l, h: (l, h, 0, 0))],
        out_specs=[pl.BlockSpec((None, NVAR, None, TQ, WIN), lambda l, h: (l, 0, h, 0, 0))],
        out_shape=[SDS((L, NVAR, NH, TQ, WIN), F32)], args=[gvec])
    return out, got


def _bias_tables_grad(dbias, l):
    nb = NTOE // LANE
    wb = WIN // LANE

    def body(d_ref, o_ref):
        ii = lax.broadcasted_iota(jnp.int32, (LANE, LANE), 0)
        jj = lax.broadcasted_iota(jnp.int32, (LANE, LANE), 1)
        flip = jnp.where(ii + jj == LANE - 1, 1.0, 0.0).astype(BF16)
        o_ref[...] = jnp.zeros_like(o_ref)
        for var in range(NVAR):
            blocks = []
            for b in range(nb):
                src = nb - 1 - b
                if src >= wb:
                    blocks.append(jnp.zeros((TQ, LANE), F32))
                    continue
                xv = d_ref[var, :, src * LANE:(src + 1) * LANE]
                hi = xv.astype(BF16)
                lo = (xv - hi.astype(F32)).astype(BF16)
                blocks.append(_nn(hi, flip) + _nn(lo, flip))
            rev = jnp.concatenate(blocks, axis=1)
            skew = pltpu.roll(rev, NTOE - TQ + 1, 1, stride=1, stride_axis=0)
            off = NG - NTOE - var * TQ
            o_ref[:, off:off + NTOE] += jnp.sum(skew, axis=0, keepdims=True)

    return pl.pallas_call(
        body, name=f"bias_tables_grad_l{l}", grid=(NH,),
        in_specs=[pl.BlockSpec((NVAR, None, TQ, WIN), lambda h: (0, h, 0, 0))],
        out_specs=pl.BlockSpec((None, 1, NG), lambda h: (h, 0, 0)),
        out_shape=SDS((NH, 1, NG), F32),
        compiler_params=_cp(),
    )(dbias)


_SIBLING = (0, 0, 1)
_CHIPS = [(1, 0, 0), (0, 1, 0), (1, 1, 0)]
_MASKS = [_SIBLING] + _CHIPS + [(1, 0, 1), (0, 1, 1), (1, 1, 1)]


def _position():
    return lax.axis_index("x"), lax.axis_index("y"), lax.axis_index("c")


def _peer(pos, mask):
    return tuple(1 - a if f else a for a, f in zip(pos, mask))


def _index(pos):
    return 4 * pos[0] + 2 * pos[1] + pos[2]


def _exchange_phases(items, src, dst, sems):
    send_sems, recv_sems, local_sems = sems
    me = _position()
    sib = _peer(me, _SIBLING)

    def remote(s_ref, d_ref, pi, n, to):
        return pltpu.make_async_remote_copy(
            src_ref=s_ref, dst_ref=d_ref, send_sem=send_sems.at[pi, n], recv_sem=recv_sems.at[pi, n],
            device_id=to, device_id_type=MESH_ID)

    def parts(n):
        it = items[n]
        r = SHARD_ROWS[it[1]]
        block = lambda ref, pos: ref.at[pl.ds(_index(pos) * r, r), :]
        if it[0] == "gather":
            own = src[n].at[it[2]]
            local = pltpu.make_async_copy(own, block(dst[n], me), local_sems.at[n])
            sends = [remote(own, block(dst[n], me), pi, n, _peer(me, m)) for pi, m in enumerate([_SIBLING] + _CHIPS)]
            hops = [(remote(block(dst[n], _peer(me, m)), block(dst[n], _peer(me, m)), 1 + j, n, _peer(me, m)),
                     remote(block(dst[n], _peer(me, m)), block(dst[n], _peer(me, m)), 4 + j, n, sib))
                    for j, m in enumerate(_CHIPS)]
            lands = [remote(own, block(dst[n], sib), 0, n, sib)]
            lands += [remote(own, block(dst[n], _peer(sib, m)), 4 + j, n, sib) for j, m in enumerate(_CHIPS)]
        else:
            local = pltpu.make_async_copy(block(src[n], me), dst[n].at[_index(me)], local_sems.at[n])
            sends = [remote(block(src[n], _peer(me, m)), dst[n].at[_index(me)], pi, n, _peer(me, m))
                     for pi, m in enumerate(_MASKS)]
            hops = []
            lands = [remote(block(src[n], me), dst[n].at[_index(_peer(me, m))], pi, n, _peer(me, m))
                     for pi, m in enumerate(_MASKS)]
        return local, sends, hops, lands

    def start():
        for n in range(len(items)):
            local, sends, _, _ = parts(n)
            local.start()
            for cp in sends:
                cp.start()

    def relay():
        for n in range(len(items)):
            for arrived, onward in parts(n)[2]:
                arrived.wait_recv()
                onward.start()

    def finish():
        for n in range(len(items)):
            local, sends, hops, lands = parts(n)
            for cp in lands:
                cp.wait_recv()
            for cp in sends + [onward for _, onward in hops]:
                cp.wait_send()
            local.wait()

    return start, relay, finish


def _hosted_call(body, xchg, *, name, grid, in_specs, out_specs, out_shape, args, scratch_shapes=(), relay_at=0.8):
    if not xchg:
        outs = pl.pallas_call(
            body, name=name, grid=grid, in_specs=list(in_specs), out_specs=list(out_specs),
            out_shape=list(out_shape), scratch_shapes=list(scratch_shapes), compiler_params=_cp())(*args)
        return outs, []
    items = [it for it, _ in xchg]
    n_in, n_out, n_scr, nit = len(args), len(out_shape), len(scratch_shapes), len(items)
    hbm = pl.BlockSpec(memory_space=pl.ANY)
    steps = int(np.prod(grid))
    relay_step = min(int(relay_at * steps), steps - 1)

    def dst_shape(it):
        r = SHARD_ROWS[it[1]]
        return SDS((NDEV * r, D) if it[0] == "gather" else (NDEV, r, D), BF16)

    def wrapped(*refs):
        ins = refs[:n_in]
        src = refs[n_in:n_in + nit]
        outs = refs[n_in + nit:n_in + nit + n_out]
        dst = refs[n_in + nit + n_out:n_in + 2 * nit + n_out]
        scratch = refs[n_in + 2 * nit + n_out:n_in + 2 * nit + n_out + n_scr]
        start, relay, finish = _exchange_phases(items, src, dst, refs[n_in + 2 * nit + n_out + n_scr:])
        step = 0
        for d, g in enumerate(grid):
            step = step * g + pl.program_id(d)
        pl.when(step == 0)(start)
        body(*ins, *outs, *scratch)
        pl.when(step == relay_step)(relay)
        pl.when(step == steps - 1)(finish)

    npeer = len(_MASKS)
    res = pl.pallas_call(
        wrapped, name=name, grid=grid,
        in_specs=list(in_specs) + [hbm] * nit,
        out_specs=list(out_specs) + [hbm] * nit,
        out_shape=list(out_shape) + [dst_shape(it) for it in items],
        scratch_shapes=list(scratch_shapes) + [
            pltpu.SemaphoreType.DMA((npeer, nit)), pltpu.SemaphoreType.DMA((npeer, nit)), pltpu.SemaphoreType.DMA((nit,))],
        compiler_params=_cp(),
    )(*args, *[a for _, a in xchg])
    return list(res[:n_out]), list(res[n_out:])


def _sum_slots(slots, xchg=None):
    _, r, _ = slots[0].shape
    n = len(slots) // L
    rt = 64

    def body(*refs):
        for k in range(n):
            for l in range(L):
                src = refs[k * L + l]
                acc = src[0].astype(F32)
                for d in range(1, NDEV):
                    acc = acc + src[d].astype(F32)
                refs[n * L + k][l] = acc

    return _hosted_call(
        body, xchg, name=f"sum_slots_r{r}" + ("_x" if xchg else ""), grid=(r // rt,),
        in_specs=[pl.BlockSpec((NDEV, rt, D), lambda i: (0, i, 0))] * (n * L),
        out_specs=[pl.BlockSpec((L, rt, D), lambda i: (0, i, 0))] * n,
        out_shape=[SDS((L, r, D), F32)] * n, args=list(slots))


def _exchange_small(v, reduce):
    rows = v.shape[0]

    def body(v_ref, o_ref, *scratch):
        if reduce:
            slots, send_sems, recv_sems = scratch
        else:
            slots = o_ref
            send_sems, recv_sems = scratch
        me = _position()
        sib = _peer(me, _SIBLING)

        def remote(s_ref, pos, pi, to):
            return pltpu.make_async_remote_copy(
                src_ref=s_ref, dst_ref=slots.at[_index(pos)], send_sem=send_sems.at[pi], recv_sem=recv_sems.at[pi],
                device_id=to, device_id_type=MESH_ID)

        slots[_index(me)] = v_ref[...]
        sends = [remote(v_ref, me, pi, _peer(me, m)) for pi, m in enumerate([_SIBLING] + _CHIPS)]
        for cp in sends:
            cp.start()
        for j, m in enumerate(_CHIPS):
            peer = _peer(me, m)
            remote(v_ref, peer, 1 + j, peer).wait_recv()
            onward = remote(slots.at[_index(peer)], peer, 4 + j, sib)
            onward.start()
            sends.append(onward)
        remote(v_ref, sib, 0, sib).wait_recv()
        for j, m in enumerate(_CHIPS):
            remote(v_ref, _peer(sib, m), 4 + j, sib).wait_recv()
        for cp in sends:
            cp.wait_send()
        if reduce:
            acc = slots[0]
            for d in range(1, NDEV):
                acc = acc + slots[d]
            o_ref[...] = acc

    vm = pl.BlockSpec(memory_space=pltpu.VMEM)
    sems = [pltpu.SemaphoreType.DMA((len(_MASKS),)), pltpu.SemaphoreType.DMA((len(_MASKS),))]
    return pl.pallas_call(
        body, name="reduce_small" if reduce else "gather_small",
        in_specs=[vm], out_specs=vm,
        out_shape=SDS((rows, LANE) if reduce else (NDEV, rows, LANE), F32),
        scratch_shapes=([pltpu.VMEM((NDEV, rows, LANE), F32)] if reduce else []) + sems,
        compiler_params=_cp(),
    )(v)


def _adamw_update(w_ref, g_ref, m_ref, v_ref, d_ref, nm_ref, nv_ref):
    gv = g_ref[...]
    mn = B1 * m_ref[...] + (1.0 - B1) * gv
    vn = B2 * v_ref[...] + (1.0 - B2) * jnp.square(gv)
    nm_ref[...] = mn
    nv_ref[...] = vn
    m_hat = mn / (1.0 - B1 ** STEP)
    v_hat = vn / (1.0 - B2 ** STEP)
    d_ref[...] = -LR * (m_hat / (jnp.sqrt(v_hat) + AEPS) + WD * w_ref[...])


def _adamw_small(ws, gs, ms, vs):
    n = len(ws)

    def body(*refs):
        for i in range(n):
            _adamw_update(*[refs[j * n + i] for j in range(7)])

    vm = pl.BlockSpec(memory_space=pltpu.VMEM)
    res = pl.pallas_call(
        body, name="adamw_small", in_specs=[vm] * (4 * n), out_specs=[vm] * (3 * n),
        out_shape=[SDS(w.shape, F32) for _ in range(3) for w in ws],
        compiler_params=_cp(),
    )(*ws, *gs, *ms, *vs)
    return res[:n], res[n:2 * n], res[2 * n:]


def _adamw(w, g, m, v):
    rows, cols = w.shape
    t = rows
    for cand in (512, 256, 128, 64, 32, 16, 8):
        if rows % cand == 0:
            t = cand
            break

    def body(*refs):
        _adamw_update(*refs)

    blk = pl.BlockSpec((t, cols), lambda i: (i, 0))
    return pl.pallas_call(
        body, name=f"adamw_{rows}x{cols}", grid=(rows // t,),
        in_specs=[blk] * 4, out_specs=[blk] * 3,
        out_shape=[SDS((rows, cols), F32)] * 3,
        compiler_params=_cp(),
    )(w, g, m, v)


_DIST0 = BAND + TQ - 1
_N_FAR = _DIST0 - REL_CLIP + 1
_N_NEAR = NG - _N_FAR - (2 * REL_CLIP - 1)


def _bias_vector(rel_bias):
    far = jnp.broadcast_to(rel_bias[..., -1:], (L, NH, _N_FAR))
    near = jnp.broadcast_to(rel_bias[..., :1], (L, NH, _N_NEAR))
    return jnp.concatenate([far, lax.rev(rel_bias[..., 1:-1], (2,)), near], axis=2)[:, :, None, :]


def _bias_vector_grad(dgr):
    first = jnp.sum(dgr[..., :_N_NEAR], axis=-1, keepdims=True)
    last = jnp.sum(dgr[..., NG - _N_FAR:], axis=-1, keepdims=True)
    return jnp.concatenate([first, dgr[..., _N_NEAR:NG - _N_FAR], last], axis=-1)


def _pool_blockdiag(pool_w):
    eye = jnp.eye(2, dtype=F32)
    pw = pool_w.reshape(L, 2, 2, HD, HD)
    return jnp.einsum("ljaik,ab->ljaibk", pw, eye).reshape(L, 2, LANE, LANE)


def _pool_blockdiag_grad(dwbd):
    d = dwbd.reshape(L, 2, 2, HD, 2, HD)
    return jnp.stack([d[:, :, 0, :, 0, :], d[:, :, 1, :, 1, :]], axis=2).reshape(L, 4, HD, HD)


def _pack(arrays, rows):
    flat = jnp.concatenate([a.reshape(-1).astype(F32) for a in arrays])
    return jnp.pad(flat, (0, rows * LANE - flat.shape[0])).reshape(rows, LANE)


def _unpack(packed, shapes):
    flat = packed.reshape(-1)
    out, o = [], 0
    for shp in shapes:
        n = int(np.prod(shp))
        out.append(flat[o:o + n].reshape(shp))
        o += n
    return out


def _rows_for(shapes):
    n = sum(int(np.prod(s)) for s in shapes)
    return -(-n // (8 * LANE)) * 8


def _grads(x, target, small_w, shards):
    g1, qg, kg, rb, cw, pw, ps, g2 = small_w
    g1 = g1.reshape(L, 1, D)
    g2 = g2.reshape(L, 1, D)
    qg2 = jnp.tile(qg, (1, 2)).reshape(L, 1, LANE)
    kg2 = jnp.tile(kg, (1, 2)).reshape(L, 1, LANE)
    ps3 = ps.reshape(L, 1, PWD)
    wbd = _pool_blockdiag(pw).astype(BF16)

    def gather(*kl):
        return [(("gather", k, l), shards[k]) for k, l in kl if l < L]

    full = {}

    def arrived(got, *kl):
        full.update(zip([x for x in kl if x[1] < L], got))

    bias, got = _bias_tables(_bias_vector(rb), gather((0, 0)))
    arrived(got, (0, 0))
    saved = []
    h = x
    for l in range(L):
        kl = ((1, l),)
        (p, h_b, qkv), got = _in_proj(h, g1, full[0, l], qg2, kg2, l, gather(*kl))
        arrived(got, *kl)
        kl = ((2, 0), (3, 0)) if l == 0 else ((3, l),)
        (mix, lse, o32), got = _attn_fwd(qkv, bias, l, gather(*kl), relay_at=0.95 if l == 0 else 0.8)
        arrived(got, *kl)
        mix = _conv_pool_fwd(p, mix, cw, wbd, ps3, l)
        kl = ((0, l + 1), (2, l + 1))
        (xm, a, *out), got = _mlp_fwd(h, mix, full[1, l], g2, full[2, l], full[3, l], l, gather(*kl),
                                      target if l == L - 1 else None)
        arrived(got, *kl)
        saved.append((h, h_b, p, qkv, mix, lse, o32, xm, a))
        h = out[0]
    dx, sq = out

    grads = {}
    slots = {}

    def scatter(*kl):
        return [(("scatter", k), grads[k, l]) for k, l in kl if l < L]

    def left(got, *kl):
        slots.update(zip([x for x in kl if x[1] < L], got))

    per_layer = [None] * L
    for l in reversed(range(L)):
        x_in, h_b, p, qkv, mix, lse, o32, xm, a = saved[l]
        (dxm, dmix, f_b, da_b, h2_b, dxo_b, dxm_b, dg2), got = _mlp_bwd(
            dx, a, xm, g2, full[2, l], full[3, l], full[1, l], l, scatter((3, l + 1)))
        left(got, (3, l + 1))
        grads[1, l], _ = _wgrad(mix, dxm_b, f"w_out_l{l}")
        grads[2, l], got = _wgrad(da_b, h2_b, f"w_mlp1_l{l}", scatter((1, l)))
        left(got, (1, l))
        kl = ((2, 0),) if l == 0 else ()
        grads[3, l], got = _wgrad(f_b, dxo_b, f"w_mlp2_l{l}", scatter(*kl))
        left(got, *kl)
        kl = ((3, 0), (0, 1)) if l == 0 else ((2, l),)
        (dq, dk, dv, dbias, dqg, dkg), got = _attn_bwd(qkv, p, lse, o32, dmix, qg2, kg2, bias, l, scatter(*kl))
        left(got, *kl)
        dgb, dgc, dhin, du, dcw, dwbd, dps = _conv_pool_bwd(p, dmix, cw, wbd, ps3, l)
        (dx, dp_b, dg1), got = _in_proj_bwd(
            [dq, dk, dv, dgb, dgc, dhin, du], x_in, dxm, g1, full[0, l], l, scatter((0, l + 1)) if l else None)
        left(got, (0, l + 1))
        grads[0, l], _ = _wgrad(dp_b, h_b, f"w_in_l{l}")
        per_layer[l] = (dg1, dg2, dqg, dkg, _bias_tables_grad(dbias, l), dcw, dwbd, dps)
    (g_w1_t, g_w2), got = _sum_slots([slots[k, l] for k in (2, 3) for l in range(L)], scatter((0, 0)))
    left(got, (0, 0))
    sums = [_sum_slots([slots[k, l] for l in range(L)])[0][0] for k in (0, 1)] + [g_w1_t, g_w2]

    st = [jnp.stack([per_layer[l][k] for l in range(L)]) for k in range(8)]
    small = dict(
        g1=st[0].reshape(L, D), g2=st[1].reshape(L, D),
        qg=st[2].reshape(L, NH, HD).sum(1), kg=st[3].reshape(L, NH, HD).sum(1),
        rb=_bias_vector_grad(st[4].reshape(L, NH, NG)), cw=st[5], pw=_pool_blockdiag_grad(st[6]),
        ps=st[7].reshape(L, PWD))
    return sq, dx, sums, small


def kernel(x, norm1_g, w_in, q_norm_g, k_norm_g, rel_bias, conv_w, pool_w, pool_scale, w_out, norm2_g, w_mlp1, w_mlp2, loss_target, m_norm1_g, m_w_in, m_q_norm_g, m_k_norm_g, m_rel_bias, m_conv_w, m_pool_w, m_pool_scale, m_w_out, m_norm2_g, m_w_mlp1, m_w_mlp2, v_norm1_g, v_w_in, v_q_norm_g, v_k_norm_g, v_rel_bias, v_conv_w, v_pool_w, v_pool_scale, v_w_out, v_norm2_g, v_w_mlp1, v_w_mlp2):
    me = _index(_position())
    cshard = CW // NDEV

    shards = [jnp.swapaxes(w_in, 1, 2).astype(BF16), w_out.astype(BF16),
              jnp.swapaxes(w_mlp1, 1, 2).astype(BF16), w_mlp2.astype(BF16)]
    cw_all = _exchange_small(_pack([conv_w], 8), reduce=False)
    cw_full = jnp.concatenate(
        [cw_all[d].reshape(-1)[:L * 3 * cshard].reshape(L, 3, cshard) for d in range(NDEV)], axis=2)

    small_w = (norm1_g, q_norm_g, k_norm_g, rel_bias, cw_full, pool_w, pool_scale, norm2_g)
    sq, grad_x, (g_win_t, g_wout, g_w1_t, g_w2), small = _grads(x[0], loss_target[0], small_w, shards)
    g_w_in = jnp.swapaxes(g_win_t, 1, 2)
    g_w_mlp1 = jnp.swapaxes(g_w1_t, 1, 2)

    names = ("g1", "qg", "kg", "rb", "cw", "pw", "ps", "g2")
    gshapes = [(L, D), (L, HD), (L, HD), (L, NH, 2 * REL_CLIP + 1), (L, 3, CW), (L, 4, HD, HD), (L, PWD), (L, D)]
    garrs = [small[n] for n in names]
    rows = _rows_for(gshapes + [(1,)])
    total = _exchange_small(_pack(garrs + [sq[0, :1]], rows), reduce=True)
    g_g1, g_qg, g_kg, g_rb, g_cw_full, g_pw, g_ps, g_g2, sq_sum = _unpack(total, gshapes + [(1,)])
    loss = (0.5 / D) * sq_sum[0]
    g_cw = lax.dynamic_slice_in_dim(g_cw_full, me * cshard, cshard, axis=2)

    def big(w, g, m, v):
        shp = w.shape
        r = lambda a: a.reshape(-1, shp[-1])
        return [o.reshape(shp) for o in _adamw(r(w), r(g), r(m), r(v))]

    up_in = big(w_in, g_w_in, m_w_in, v_w_in)
    up_out = big(w_out, g_wout, m_w_out, v_w_out)
    up_1 = big(w_mlp1, g_w_mlp1, m_w_mlp1, v_w_mlp1)
    up_2 = big(w_mlp2, g_w2, m_w_mlp2, v_w_mlp2)

    sw = [norm1_g, q_norm_g, k_norm_g, rel_bias, conv_w, pool_w, pool_scale, norm2_g]
    sg = [g_g1, g_qg, g_kg, g_rb, g_cw, g_pw, g_ps, g_g2]
    sm = [m_norm1_g, m_q_norm_g, m_k_norm_g, m_rel_bias, m_conv_w, m_pool_w, m_pool_scale, m_norm2_g]
    sv = [v_norm1_g, v_q_norm_g, v_k_norm_g, v_rel_bias, v_conv_w, v_pool_w, v_pool_scale, v_norm2_g]
    s_delta, s_m, s_v = _adamw_small(sw, sg, sm, sv)

    def order(small_list, in_, out_, m1, m2):
        g1_, qg_, kg_, rb_, cw_, pw_, ps_, g2_ = small_list
        return [g1_, in_, qg_, kg_, rb_, cw_, pw_, ps_, out_, g2_, m1, m2]

    grads = order(sg, g_w_in, g_wout, g_w_mlp1, g_w2)
    deltas = order(s_delta, up_in[0], up_out[0], up_1[0], up_2[0])
    new_m = order(s_m, up_in[1], up_out[1], up_1[1], up_2[1])
    new_v = order(s_v, up_in[2], up_out[2], up_1[2], up_2[2])
    return (loss, grad_x[None], *grads, *deltas, *new_m, *new_v)
```

```python
import numpy as np
import jax
import jax.numpy as jnp
from jax import lax
from jax.experimental import pallas as pl
from jax.experimental.pallas import tpu as pltpu

F32 = jnp.float32
BF16 = jnp.bfloat16
SDS = jax.ShapeDtypeStruct
MESH_ID = pl.DeviceIdType.MESH

D = 1024
L = 4
CHUNK = 64
N_PREV = 8
HD = 64
NH = 8
AW = 512
CW = 256
PWD = 256
DIN = 2560
DFF = 4096
EPS = 1e-6
NEG = -1e30
REL_CLIP = 128
POOL_WINDOWS = (2, 4, 8, 16)
LR, B1, B2, AEPS, WD, STEP = 0.001, 0.9, 0.999, 1e-08, 0.01, 10

NDEV = 8
LANE = 128
BAND = N_PREV * CHUNK
TQ = 256
WIN = TQ + BAND
NVAR = BAND // TQ + 1
NTOE = -(-(WIN + TQ - 1) // LANE) * LANE
NG = (NVAR - 1) * TQ + NTOE
PAD = 16
RB_NORM = 64
RB_SOFT = 16
VMEM_LIMIT = 56 * 1024 * 1024
SHARD_ROWS = (DIN // NDEV, D // NDEV, DFF // NDEV, DFF // NDEV)

assert 2 * HD == LANE and NH * HD == AW and POOL_WINDOWS == (2, 4, 8, 16)
assert TQ % CHUNK == 0 and BAND % TQ == 0 and max(POOL_WINDOWS) <= PAD and all(r % 16 == 0 for r in SHARD_ROWS)


def _cp(**kw):
    return pltpu.CompilerParams(vmem_limit_bytes=VMEM_LIMIT, **kw)


def _nn(a, b):
    return jnp.dot(a, b, preferred_element_type=F32)


def _nt(a, b):
    return lax.dot_general(a, b, (((1,), (1,)), ((), ())), preferred_element_type=F32)


def _tn(a, b):
    return lax.dot_general(a, b, (((0,), (0,)), ((), ())), preferred_element_type=F32)


def _const(shape):
    n = len(shape)
    return pl.BlockSpec(shape, lambda *_: (0,) * n, pipeline_mode=pl.Buffered(1))


def _layer(shape, l):
    n = len(shape)
    return pl.BlockSpec((None,) + tuple(shape), lambda *_: (l,) + (0,) * n, pipeline_mode=pl.Buffered(1))


def _lo_mask():
    return lax.broadcasted_iota(jnp.int32, (1, LANE), 1) < HD


def _half_sum(t, lo):
    s_lo = jnp.sum(jnp.where(lo, t, 0.0), axis=-1, keepdims=True)
    s_hi = jnp.sum(jnp.where(lo, 0.0, t), axis=-1, keepdims=True)
    return jnp.where(lo, s_lo, s_hi)


def _head_norm(x, lo):
    r = lax.rsqrt(_half_sum(x * x, lo) * (1.0 / HD) + EPS)
    return x * r, r


def _head_norm_bwd(dy, xn, r, g, lo):
    dxn = dy * g
    mu = _half_sum(dxn * xn, lo) * (1.0 / HD)
    return r * (dxn - xn * mu), dy * xn


def _rms_bwd(dy, x, g):
    r = lax.rsqrt(jnp.mean(x * x, axis=-1, keepdims=True) + EPS)
    xn = x * r
    dxn = dy * g
    mu = jnp.mean(dxn * xn, axis=-1, keepdims=True)
    return r * (dxn - xn * mu), dy * xn


def _in_proj(x, g1, win_t, qg2, kg2, l, xchg=None):
    s = x.shape[0]
    t = min(512, s)
    nblk = AW // LANE

    def body(x_ref, g_ref, w_ref, qg_ref, kg_ref, p_ref, h_ref, qkv_ref):
        xv = x_ref[...]
        r = lax.rsqrt(jnp.mean(xv * xv, axis=-1, keepdims=True) + EPS)
        h = (xv * r * g_ref[...]).astype(BF16)
        h_ref[...] = h
        p_ref[...] = _nt(h, w_ref[...])
        lo = _lo_mask()
        gains = (qg_ref[...] * (HD ** -0.5), kg_ref[...])
        for r0 in range(0, t, RB_NORM):
            rows = pl.ds(r0, RB_NORM)
            for c in range(3 * nblk):
                cols = pl.ds(c * LANE, LANE)
                v = p_ref[rows, cols]
                if c < 2 * nblk:
                    v = _head_norm(v, lo)[0] * gains[c // nblk]
                qkv_ref[rows, cols] = v.astype(BF16)

    row = lambda c: pl.BlockSpec((t, c), lambda i: (i, 0))
    return _hosted_call(
        body, xchg, name="in_proj", grid=(s // t,),
        in_specs=[row(D), _layer((1, D), l), _const((DIN, D)), _layer((1, LANE), l), _layer((1, LANE), l)],
        out_specs=[row(DIN), row(D), row(3 * AW)],
        out_shape=[SDS((s, DIN), F32), SDS((s, D), BF16), SDS((s, 3 * AW), BF16)], args=[x, g1, win_t, qg2, kg2])


def _bias_spec():
    return pl.BlockSpec((None, 2, TQ, WIN), lambda j, i: (jnp.maximum(NVAR - 1 - i, 0), j, 0, 0))


def _bias_layer_spec(l):
    return pl.BlockSpec((None, None, 2, TQ, WIN), lambda j, i: (l, jnp.maximum(NVAR - 1 - i, 0), j, 0, 0))


def _attn_fwd(qkv, bias, l, xchg=None, relay_at=0.8):
    s = qkv.shape[0]
    nq = s // TQ

    def body(q_ref, k_ref, v_ref, b_ref, o_ref, lse_ref, o32_ref, s_ref, p_ref, m_ref, den_ref, o0_ref):
        i = pl.program_id(1)
        ks = pl.multiple_of(jnp.maximum(i * TQ - BAND, 0), TQ)
        lo = _lo_mask()
        q = q_ref[...]
        kwin = k_ref[pl.ds(ks, WIN), :]
        vwin = v_ref[pl.ds(ks, WIN), :]
        for half in range(2):
            m_ = lo if half == 0 else jnp.logical_not(lo)
            s_ref[half] = _nt(jnp.where(m_, q, jnp.zeros_like(q)), kwin)
            for r0 in range(0, TQ, RB_SOFT):
                rows = pl.ds(r0, RB_SOFT)
                mx = jnp.max(s_ref[half, rows, :] + b_ref[half, rows, :], axis=-1, keepdims=True)
                m_ref[rows, :] = jnp.broadcast_to(mx, (RB_SOFT, LANE))
            for r0 in range(0, TQ, RB_SOFT):
                rows = pl.ds(r0, RB_SOFT)
                mx = m_ref[rows, 0:1]
                e = jnp.exp(s_ref[half, rows, :] + b_ref[half, rows, :] - mx)
                p_ref[half, rows, :] = e.astype(BF16)
                den = jnp.sum(e, axis=-1, keepdims=True)
                den_ref[rows, :] = jnp.broadcast_to(den, (RB_SOFT, LANE))
                lse = mx + jnp.log(den)
                if half == 0:
                    lse_ref[rows, :] = jnp.broadcast_to(lse, (RB_SOFT, LANE))
                else:
                    lse_ref[rows, :] = jnp.where(lo, lse_ref[rows, :], lse)
            o = _nn(p_ref[half], vwin) * (1.0 / den_ref[...])
            if half == 0:
                o0_ref[...] = o
            else:
                o = jnp.where(lo, o0_ref[...], o)
                o32_ref[...] = o
                o_ref[...] = o.astype(BF16)

    tile = pl.BlockSpec((TQ, LANE), lambda j, i: (i, j))
    stat = pltpu.VMEM((TQ, LANE), F32)
    return _hosted_call(
        body, xchg, name="attn_fwd", grid=(NH // 2, nq),
        in_specs=[
            tile,
            pl.BlockSpec((s, LANE), lambda j, i: (0, AW // LANE + j)),
            pl.BlockSpec((s, LANE), lambda j, i: (0, 2 * AW // LANE + j)),
            _bias_layer_spec(l),
        ],
        out_specs=[tile, tile, tile],
        out_shape=[SDS((s, D), BF16), SDS((s, AW), F32), SDS((s, AW), F32)], args=[qkv, qkv, qkv, bias],
        scratch_shapes=[pltpu.VMEM((2, TQ, WIN), F32), pltpu.VMEM((2, TQ, WIN), BF16), stat, stat, stat],
        relay_at=relay_at)


_C0 = 3 * AW // LANE


def _cp_in_specs(s, l):
    blk = lambda f: pl.BlockSpec((s, LANE), f)
    return [
        blk(lambda g: (0, _C0 + jnp.minimum(g, 1))),
        blk(lambda g: (0, _C0 + 2 + jnp.minimum(g, 1))),
        blk(lambda g: (0, _C0 + 4 + jnp.minimum(g, 1))),
        blk(lambda g: (0, _C0 + 6 + jnp.maximum(g - 2, 0))),
        pl.BlockSpec((None, 3, LANE), lambda g: (l, 0, jnp.minimum(g, 1))),
        pl.BlockSpec((None, None, LANE, LANE), lambda g: (l, jnp.maximum(g - 2, 0), 0, 0)),
        pl.BlockSpec((None, 1, LANE), lambda g: (l, 0, jnp.maximum(g - 2, 0))),
    ]


def _pool_window_sums(u_ref, buf_a, buf_b, jj, s, rt):
    nrt = s // rt
    for t in range(nrt):
        buf_a[pl.ds(PAD + t * rt, rt), :] = u_ref[pl.ds(t * rt, rt), :]

    def stage(src, dst, sh):
        for t in range(nrt):
            r0 = PAD + t * rt
            dst[pl.ds(r0, rt), :] = src[pl.ds(r0, rt), :] + src[pl.ds(r0 - sh, rt), :]

    stage(buf_a, buf_b, 1)
    stage(buf_b, buf_a, 2)

    @pl.when(jj == 1)
    def _():
        stage(buf_a, buf_b, 4)
        stage(buf_b, buf_a, 8)


def _pool_counts(jj, lo, r0, rt):
    w0, w1, w2, w3 = [float(w) for w in POOL_WINDOWS]
    w = jnp.where(lo, jnp.where(jj == 0, w0, w2), jnp.where(jj == 0, w1, w3))
    pos1 = (lax.broadcasted_iota(jnp.int32, (rt, LANE), 0) + (r0 + 1)).astype(F32)
    return jnp.minimum(pos1, w)


def _conv_pool_fwd(p, mix, conv_w, wbd, pscale, l):
    s = p.shape[0]
    rt = min(256, s)
    nrt = s // rt

    def body(gb_ref, gc_ref, hin_ref, u_ref, cw_ref, wbd_ref, ps_ref, mix_in, o_ref, buf_a, buf_b):
        del mix_in
        g = pl.program_id(0)
        zpad = jnp.zeros((PAD, LANE), F32)
        buf_a[pl.ds(0, PAD), :] = zpad
        buf_b[pl.ds(0, PAD), :] = zpad

        @pl.when(g < 2)
        def _conv():
            for t in range(nrt):
                buf_a[pl.ds(PAD + t * rt, rt), :] = gc_ref[pl.ds(t * rt, rt), :] * hin_ref[pl.ds(t * rt, rt), :]
            w0, w1, w2 = cw_ref[0:1, :], cw_ref[1:2, :], cw_ref[2:3, :]
            for t in range(nrt):
                r0 = PAD + t * rt
                y = w0 * buf_a[pl.ds(r0 - 2, rt), :] + w1 * buf_a[pl.ds(r0 - 1, rt), :] + w2 * buf_a[pl.ds(r0, rt), :]
                o_ref[pl.ds(t * rt, rt), :] = (gb_ref[pl.ds(t * rt, rt), :] * y).astype(BF16)

        @pl.when(g >= 2)
        def _pool():
            jj = g - 2
            lo = _lo_mask()
            _pool_window_sums(u_ref, buf_a, buf_b, jj, s, rt)
            wb = wbd_ref[...]
            for t in range(nrt):
                r0 = PAD + t * rt
                wsum = jnp.where(lo, buf_b[pl.ds(r0, rt), :], buf_a[pl.ds(r0, rt), :])
                m = wsum / _pool_counts(jj, lo, t * rt, rt) - u_ref[pl.ds(t * rt, rt), :]
                o_ref[pl.ds(t * rt, rt), :] = (_nn(m.astype(BF16), wb) * ps_ref[...]).astype(BF16)

    return pl.pallas_call(
        body, name="conv_pool_fwd", grid=(4,),
        in_specs=_cp_in_specs(s, l) + [pl.BlockSpec(memory_space=pl.ANY)],
        out_specs=pl.BlockSpec((s, LANE), lambda g: (0, AW // LANE + g)),
        out_shape=SDS((s, D), BF16),
        scratch_shapes=[pltpu.VMEM((s + 2 * PAD, LANE), F32), pltpu.VMEM((s + 2 * PAD, LANE), F32)],
        input_output_aliases={7: 0},
        compiler_params=_cp(),
    )(p, p, p, p, conv_w, wbd, pscale, mix)


def _mlp_fwd(x, mix, wout, g2, w1_t, w2, l, xchg=None, target=None):
    s = x.shape[0]
    t = min(256, s)

    def body(*refs):
        x_ref, mix_ref, wo_ref, g_ref, w1_ref, w2_ref = refs[:6]
        xm_ref, a_ref, xo_ref = refs[-4:-1] if target is not None else refs[-3:]
        xm = x_ref[...] + _nn(mix_ref[...], wo_ref[...])
        xm_ref[...] = xm
        r = lax.rsqrt(jnp.mean(xm * xm, axis=-1, keepdims=True) + EPS)
        h2 = (xm * r * g_ref[...]).astype(BF16)
        a = _nt(h2, w1_ref[...])
        a_ref[...] = a.astype(BF16)
        f = jnp.square(jnp.maximum(a, 0.0)).astype(BF16)
        xo = xm + _nn(f, w2_ref[...])
        if target is None:
            xo_ref[...] = xo
        else:
            acc_ref = refs[-1]

            @pl.when(pl.program_id(0) == 0)
            def _():
                acc_ref[...] = jnp.zeros_like(acc_ref)
            e = xo - refs[6][...]
            xo_ref[...] = e * (1.0 / D)
            acc_ref[...] += jnp.sum(e * e)

    row = lambda c: pl.BlockSpec((t, c), lambda i: (i, 0))
    last = target is not None
    return _hosted_call(
        body, xchg, name="mlp_fwd_loss" if last else "mlp_fwd", grid=(s // t,),
        in_specs=[row(D), row(D), _const((D, D)), _layer((1, D), l), _const((DFF, D)), _const((DFF, D))] + [row(D)] * last,
        out_specs=[row(D), row(DFF), row(D)] + [pl.BlockSpec((8, LANE), lambda i: (0, 0))] * last,
        out_shape=[SDS((s, D), F32), SDS((s, DFF), BF16), SDS((s, D), F32)] + [SDS((8, LANE), F32)] * last,
        args=[x, mix, wout, g2, w1_t, w2] + [target] * last)


def _mlp_bwd(dxo, a, xm, g2, w1_t, w2, wout, l, xchg=None):
    s = dxo.shape[0]
    t = min(256, s)

    def body(dxo_ref, a_ref, xm_ref, g_ref, w1_ref, w2_ref, wo_ref,
             dxm_ref, dmix_ref, f_ref, da_ref, h2_ref, dxob_ref, dxmb_ref, dg_ref):
        @pl.when(pl.program_id(0) == 0)
        def _():
            dg_ref[...] = jnp.zeros_like(dg_ref)
        dxo = dxo_ref[...]
        dxob = dxo.astype(BF16)
        dxob_ref[...] = dxob
        ra = jnp.maximum(a_ref[...].astype(F32), 0.0)
        f_ref[...] = jnp.square(ra).astype(BF16)
        dab = (_nt(dxob, w2_ref[...]) * (2.0 * ra)).astype(BF16)
        da_ref[...] = dab
        dh2 = _nn(dab, w1_ref[...])
        xm = xm_ref[...]
        g = g_ref[...]
        r = lax.rsqrt(jnp.mean(xm * xm, axis=-1, keepdims=True) + EPS)
        h2_ref[...] = (xm * r * g).astype(BF16)
        dx_n, dgr = _rms_bwd(dh2, xm, g)
        dg_ref[...] += jnp.sum(dgr, axis=0, keepdims=True)
        dxm = dxo + dx_n
        dxm_ref[...] = dxm
        dxmb = dxm.astype(BF16)
        dxmb_ref[...] = dxmb
        dmix_ref[...] = _nt(dxmb, wo_ref[...])

    row = lambda c: pl.BlockSpec((t, c), lambda i: (i, 0))
    return _hosted_call(
        body, xchg, name="mlp_bwd", grid=(s // t,),
        in_specs=[row(D), row(DFF), row(D), _layer((1, D), l), _const((DFF, D)), _const((DFF, D)), _const((D, D))],
        out_specs=[row(D), row(D), row(DFF), row(DFF), row(D), row(D), row(D), pl.BlockSpec((1, D), lambda i: (0, 0))],
        out_shape=[SDS((s, D), F32), SDS((s, D), F32), SDS((s, DFF), BF16), SDS((s, DFF), BF16),
                   SDS((s, D), BF16), SDS((s, D), BF16), SDS((s, D), BF16), SDS((1, D), F32)],
        args=[dxo, a, xm, g2, w1_t, w2, wout])


def _attn_bwd(qkv, p, lse, o32, dmix, qg2, kg2, bias, l, xchg=None):
    s = p.shape[0]
    nq = s // TQ
    scale = HD ** -0.5

    def body(qs_ref, kb_ref, vb_ref, q_ref, k_ref, qg_ref, kg_ref, b_ref, lse_ref, o_ref, do_ref,
             dq_ref, dk_ref, dv_ref, db_ref, dqg_ref, dkg_ref,
             dk_acc, dv_acc, s_ref, dp_ref, ds_ref, pb_ref, dqn_ref, dl_ref):
        i = pl.program_id(1)
        kt = jnp.maximum(i - BAND // TQ, 0)
        ks = pl.multiple_of(kt * TQ, TQ)
        lo = _lo_mask()

        @pl.when(i == 0)
        def _():
            dk_acc[...] = jnp.zeros_like(dk_acc)
            dv_acc[...] = jnp.zeros_like(dv_acc)
            dqg_ref[...] = jnp.zeros_like(dqg_ref)
            dkg_ref[...] = jnp.zeros_like(dkg_ref)

        @pl.when(i < NVAR)
        def _():
            db_ref[...] = jnp.zeros_like(db_ref)

        qs = qs_ref[...]
        kwin = kb_ref[pl.ds(ks, WIN), :]
        vwin = vb_ref[pl.ds(ks, WIN), :]
        do = do_ref[...]
        dob = do.astype(BF16)
        dl_ref[...] = _half_sum(do * o_ref[...], lo)
        for half in range(2):
            m_ = lo if half == 0 else jnp.logical_not(lo)
            qa = jnp.where(m_, qs, jnp.zeros_like(qs))
            doa = jnp.where(m_, dob, jnp.zeros_like(dob))
            s_ref[half] = _nt(qa, kwin)
            dp_ref[half] = _nt(doa, vwin)
            for r0 in range(0, TQ, RB_SOFT):
                rows = pl.ds(r0, RB_SOFT)
                lse_h = lse_ref[rows, half * HD:half * HD + 1]
                pm = jnp.exp(s_ref[half, rows, :] + b_ref[half, rows, :] - lse_h)
                ds = pm * (dp_ref[half, rows, :] - dl_ref[rows, half * HD:half * HD + 1])
                db_ref[half, rows, :] += ds
                ds_ref[half, rows, :] = ds.astype(BF16)
                pb_ref[half, rows, :] = pm.astype(BF16)
            dsb = ds_ref[half]
            dq_h = _nn(dsb, kwin)
            if half == 0:
                dqn_ref[...] = dq_h
            else:
                dqn_ref[...] = jnp.where(lo, dqn_ref[...], dq_h)
            dk_t = _tn(qa, dsb)
            dv_t = _tn(doa, pb_ref[half])
            for t in range(WIN // TQ):
                dk_acc[kt + t] += dk_t[:, t * TQ:(t + 1) * TQ]
                dv_acc[kt + t] += dv_t[:, t * TQ:(t + 1) * TQ]
        qg, kg = qg_ref[...], kg_ref[...]
        xq, rq = _head_norm(q_ref[...], lo)
        dq, dqg_rows = _head_norm_bwd(dqn_ref[...] * scale, xq, rq, qg, lo)
        dq_ref[...] = dq.astype(BF16)
        dqg_ref[...] += jnp.sum(dqg_rows, axis=0, keepdims=True)

        @pl.when(i == nq - 1)
        def _():
            dkg = jnp.zeros((1, LANE), F32)
            for t in range(nq):
                rows = pl.ds(t * TQ, TQ)
                xk, rk = _head_norm(k_ref[rows, :], lo)
                dk, dkg_rows = _head_norm_bwd(dk_acc[t].T, xk, rk, kg, lo)
                dk_ref[rows, :] = dk.astype(BF16)
                dv_ref[rows, :] = dv_acc[t].T.astype(BF16)
                dkg = dkg + jnp.sum(dkg_rows, axis=0, keepdims=True)
            dkg_ref[...] = dkg

    tile = pl.BlockSpec((TQ, LANE), lambda j, i: (i, j))
    kcol = lambda c0: pl.BlockSpec((s, LANE), lambda j, i: (0, c0 + j))
    gain = pl.BlockSpec((None, 1, LANE), lambda j, i: (j, 0, 0))
    return _hosted_call(
        body, xchg, name="attn_bwd", grid=(NH // 2, nq),
        in_specs=[
            tile, kcol(AW // LANE), kcol(2 * AW // LANE), tile, kcol(AW // LANE),
            _layer((1, LANE), l), _layer((1, LANE), l),
            _bias_layer_spec(l), tile, tile, tile,
        ],
        out_specs=[tile, kcol(0), kcol(0), _bias_spec(), gain, gain],
        out_shape=[SDS((s, AW), BF16), SDS((s, AW), BF16), SDS((s, AW), BF16),
                   SDS((NVAR, NH, TQ, WIN), F32), SDS((NH // 2, 1, LANE), F32), SDS((NH // 2, 1, LANE), F32)],
        scratch_shapes=[pltpu.VMEM((nq, LANE, TQ), F32), pltpu.VMEM((nq, LANE, TQ), F32),
                        pltpu.VMEM((2, TQ, WIN), F32), pltpu.VMEM((2, TQ, WIN), F32),
                        pltpu.VMEM((2, TQ, WIN), BF16), pltpu.VMEM((2, TQ, WIN), BF16),
                        pltpu.VMEM((TQ, LANE), F32), pltpu.VMEM((TQ, LANE), F32)],
        args=[qkv, qkv, qkv, p, p, qg2, kg2, bias, lse, o32, dmix])


def _conv_pool_bwd(p, dmix, conv_w, wbd, pscale, l):
    s = p.shape[0]
    rt = min(256, s)
    nrt = s // rt

    def body(gb_ref, gc_ref, hin_ref, u_ref, cw_ref, wbd_ref, ps_ref, dy_ref,
             dgb_ref, dgc_ref, dhin_ref, du_ref, dcw_ref, dwbd_ref, dps_ref, buf_a, buf_b, buf_c, buf_d):
        g = pl.program_id(0)
        zpad = jnp.zeros((PAD, LANE), F32)
        for buf in (buf_a, buf_b, buf_c):
            buf[pl.ds(0, PAD), :] = zpad
            buf[pl.ds(PAD + s, PAD), :] = zpad

        @pl.when(g < 2)
        def _conv():
            for t in range(nrt):
                rows = pl.ds(t * rt, rt)
                buf_a[pl.ds(PAD + t * rt, rt), :] = gc_ref[rows, :] * hin_ref[rows, :]
                buf_b[pl.ds(PAD + t * rt, rt), :] = dy_ref[rows, :] * gb_ref[rows, :]
            w0, w1, w2 = cw_ref[0:1, :], cw_ref[1:2, :], cw_ref[2:3, :]
            d0 = jnp.zeros((1, LANE), F32)
            d1 = jnp.zeros((1, LANE), F32)
            d2 = jnp.zeros((1, LANE), F32)
            for t in range(nrt):
                rows = pl.ds(t * rt, rt)
                r0 = PAD + t * rt
                z2, z1, z0 = buf_a[pl.ds(r0 - 2, rt), :], buf_a[pl.ds(r0 - 1, rt), :], buf_a[pl.ds(r0, rt), :]
                y = w0 * z2 + w1 * z1 + w2 * z0
                dgb_ref[rows, :] = (dy_ref[rows, :] * y).astype(BF16)
                e0 = buf_b[pl.ds(r0, rt), :]
                d0 = d0 + jnp.sum(e0 * z2, axis=0, keepdims=True)
                d1 = d1 + jnp.sum(e0 * z1, axis=0, keepdims=True)
                d2 = d2 + jnp.sum(e0 * z0, axis=0, keepdims=True)
                dz = w2 * e0 + w1 * buf_b[pl.ds(r0 + 1, rt), :] + w0 * buf_b[pl.ds(r0 + 2, rt), :]
                dgc_ref[rows, :] = (dz * hin_ref[rows, :]).astype(BF16)
                dhin_ref[rows, :] = (dz * gc_ref[rows, :]).astype(BF16)
            dcw_ref[0:1, :] = d0
            dcw_ref[1:2, :] = d1
            dcw_ref[2:3, :] = d2

        @pl.when(g >= 2)
        def _pool():
            jj = g - 2
            lo = _lo_mask()
            _pool_window_sums(u_ref, buf_a, buf_b, jj, s, rt)
            wb = wbd_ref[...]
            ps = ps_ref[...]
            dps = jnp.zeros((1, LANE), F32)
            dwb = jnp.zeros((LANE, LANE), F32)
            for t in range(nrt):
                rows = pl.ds(t * rt, rt)
                r0 = PAD + t * rt
                cnt = _pool_counts(jj, lo, t * rt, rt)
                wsum = jnp.where(lo, buf_b[pl.ds(r0, rt), :], buf_a[pl.ds(r0, rt), :])
                mb = (wsum / cnt - u_ref[rows, :]).astype(BF16)
                dy = dy_ref[rows, :]
                dps = dps + jnp.sum(dy * _nn(mb, wb), axis=0, keepdims=True)
                dmp = (dy * ps).astype(BF16)
                dwb = dwb + _tn(mb, dmp)
                dm = _nt(dmp, wb)
                buf_d[rows, :] = dm
                buf_c[pl.ds(r0, rt), :] = dm / cnt
            dps_ref[...] = dps
            dwbd_ref[...] = dwb

            def stage(src, dst, sh):
                for t in range(nrt):
                    r0 = PAD + t * rt
                    dst[pl.ds(r0, rt), :] = src[pl.ds(r0, rt), :] + src[pl.ds(r0 + sh, rt), :]

            def finish(first, second):
                for t in range(nrt):
                    rows = pl.ds(t * rt, rt)
                    r0 = PAD + t * rt
                    fw = jnp.where(lo, first[pl.ds(r0, rt), :], second[pl.ds(r0, rt), :])
                    du_ref[rows, :] = (fw - buf_d[rows, :]).astype(BF16)

            stage(buf_c, buf_a, 1)
            stage(buf_a, buf_b, 2)

            @pl.when(jj == 0)
            def _():
                finish(buf_a, buf_b)

            @pl.when(jj == 1)
            def _():
                stage(buf_b, buf_c, 4)
                stage(buf_c, buf_a, 8)
                finish(buf_c, buf_a)

    cblk = pl.BlockSpec((s, LANE), lambda g: (0, jnp.minimum(g, 1)))
    pblk = pl.BlockSpec((s, LANE), lambda g: (0, jnp.maximum(g - 2, 0)))
    padded = pltpu.VMEM((s + 2 * PAD, LANE), F32)
    return pl.pallas_call(
        body, name="conv_pool_bwd", grid=(4,),
        in_specs=_cp_in_specs(s, l) + [pl.BlockSpec((s, LANE), lambda g: (0, AW // LANE + g))],
        out_specs=[cblk, cblk, cblk, pblk,
                   pl.BlockSpec((3, LANE), lambda g: (0, jnp.minimum(g, 1))),
                   pl.BlockSpec((None, LANE, LANE), lambda g: (jnp.maximum(g - 2, 0), 0, 0)),
                   pl.BlockSpec((1, LANE), lambda g: (0, jnp.maximum(g - 2, 0)))],
        out_shape=[SDS((s, CW), BF16), SDS((s, CW), BF16), SDS((s, CW), BF16), SDS((s, PWD), BF16),
                   SDS((3, CW), F32), SDS((2, LANE, LANE), F32), SDS((1, PWD), F32)],
        scratch_shapes=[padded, padded, padded, pltpu.VMEM((s, LANE), F32)],
        compiler_params=_cp(),
    )(p, p, p, p, conv_w, wbd, pscale, dmix)


def _in_proj_bwd(parts, x, dxm, g1, win_t, l, xchg=None):
    s = x.shape[0]
    t = min(256, s)
    widths = [a.shape[1] for a in parts]
    offs = [int(o) for o in np.cumsum([0] + widths[:-1])]
    n = len(parts)

    def body(*refs):
        part_refs = refs[:n]
        x_ref, dxm_ref, g_ref, w_ref, dx_ref, dp_ref, dg_ref = refs[n:]

        @pl.when(pl.program_id(0) == 0)
        def _():
            dg_ref[...] = jnp.zeros_like(dg_ref)
        for r, o, w in zip(part_refs, offs, widths):
            dp_ref[:, o:o + w] = r[...]
        dh = _nn(dp_ref[...], w_ref[...])
        dx_n, dgr = _rms_bwd(dh, x_ref[...], g_ref[...])
        dg_ref[...] += jnp.sum(dgr, axis=0, keepdims=True)
        dx_ref[...] = dxm_ref[...] + dx_n

    row = lambda c: pl.BlockSpec((t, c), lambda i: (i, 0))
    return _hosted_call(
        body, xchg, name="in_proj_bwd", grid=(s // t,),
        in_specs=[row(w) for w in widths] + [row(D), row(D), _layer((1, D), l), _const((DIN, D))],
        out_specs=[row(D), row(DIN), pl.BlockSpec((1, D), lambda i: (0, 0))],
        out_shape=[SDS((s, D), F32), SDS((s, DIN), BF16), SDS((1, D), F32)], args=[*parts, x, dxm, g1, win_t])


def _wgrad(a, b, tag, xchg=None):
    s, m = a.shape
    mb = 512

    def body(a_ref, b_ref, o_ref):
        o_ref[...] = _tn(a_ref[...], b_ref[...]).astype(BF16)

    (out,), got = _hosted_call(
        body, xchg, name=f"wgrad_{tag}", grid=(m // mb,),
        in_specs=[pl.BlockSpec((s, mb), lambda mi: (0, mi)), _const((s, D))],
        out_specs=[pl.BlockSpec((mb, D), lambda mi: (mi, 0))],
        out_shape=[SDS((m, D), BF16)], args=[a, b])
    return out, got


def _bias_tables(gvec, xchg=None):
    def body(g_ref, o_ref):
        qc = lax.broadcasted_iota(jnp.int32, (TQ, WIN), 0) // CHUNK
        kc = lax.broadcasted_iota(jnp.int32, (TQ, WIN), 1) // CHUNK
        for var in range(NVAR):
            vec = jnp.broadcast_to(g_ref[:, var * TQ:var * TQ + NTOE], (TQ, NTOE))
            toe = pltpu.roll(vec, NTOE - TQ + 1, 1, stride=1, stride_axis=0)[:, :WIN]
            rel = (BAND - var * TQ) // CHUNK + qc - kc
            o_ref[var] = jnp.where((rel >= 0) & (rel <= N_PREV), toe, NEG)

    (out,), got = _hosted_call(
        body, xchg, name="bias_tables", grid=(L, NH),
        in_specs=[pl.BlockSpec((None, None, 1, NG), lambda l, h: (l, h, 0, 0))],
        out_specs=[pl.BlockSpec((None, NVAR, None, TQ, WIN), lambda l, h: (l, 0, h, 0, 0))],
        out_shape=[SDS((L, NVAR, NH, TQ, WIN), F32)], args=[gvec])
    return out, got


def _bias_tables_grad(dbias, l):
    nb = NTOE // LANE
    wb = WIN // LANE

    def body(d_ref, o_ref):
        ii = lax.broadcasted_iota(jnp.int32, (LANE, LANE), 0)
        jj = lax.broadcasted_iota(jnp.int32, (LANE, LANE), 1)
        flip = jnp.where(ii + jj == LANE - 1, 1.0, 0.0).astype(BF16)
        o_ref[...] = jnp.zeros_like(o_ref)
        for var in range(NVAR):
            blocks = []
            for b in range(nb):
                src = nb - 1 - b
                if src >= wb:
                    blocks.append(jnp.zeros((TQ, LANE), F32))
                    continue
                xv = d_ref[var, :, src * LANE:(src + 1) * LANE]
                hi = xv.astype(BF16)
                lo = (xv - hi.astype(F32)).astype(BF16)
                blocks.append(_nn(hi, flip) + _nn(lo, flip))
            rev = jnp.concatenate(blocks, axis=1)
            skew = pltpu.roll(rev, NTOE - TQ + 1, 1, stride=1, stride_axis=0)
            off = NG - NTOE - var * TQ
            o_ref[:, off:off + NTOE] += jnp.sum(skew, axis=0, keepdims=True)

    return pl.pallas_call(
        body, name=f"bias_tables_grad_l{l}", grid=(NH,),
        in_specs=[pl.BlockSpec((NVAR, None, TQ, WIN), lambda h: (0, h, 0, 0))],
        out_specs=pl.BlockSpec((None, 1, NG), lambda h: (h, 0, 0)),
        out_shape=SDS((NH, 1, NG), F32),
        compiler_params=_cp(),
    )(dbias)


_SIBLING = (0, 0, 1)
_CHIPS = [(1, 0, 0), (0, 1, 0), (1, 1, 0)]
_MASKS = [_SIBLING] + _CHIPS + [(1, 0, 1), (0, 1, 1), (1, 1, 1)]


def _position():
    return lax.axis_index("x"), lax.axis_index("y"), lax.axis_index("c")


def _peer(pos, mask):
    return tuple(1 - a if f else a for a, f in zip(pos, mask))


def _index(pos):
    return 4 * pos[0] + 2 * pos[1] + pos[2]


def _exchange_phases(items, src, dst, sems):
    send_sems, recv_sems, local_sems = sems
    me = _position()
    sib = _peer(me, _SIBLING)

    def remote(s_ref, d_ref, pi, n, to):
        return pltpu.make_async_remote_copy(
            src_ref=s_ref, dst_ref=d_ref, send_sem=send_sems.at[pi, n], recv_sem=recv_sems.at[pi, n],
            device_id=to, device_id_type=MESH_ID)

    def parts(n):
        it = items[n]
        r = SHARD_ROWS[it[1]]
        block = lambda ref, pos: ref.at[pl.ds(_index(pos) * r, r), :]
        if it[0] == "gather":
            own = src[n].at[it[2]]
            local = pltpu.make_async_copy(own, block(dst[n], me), local_sems.at[n])
            sends = [remote(own, block(dst[n], me), pi, n, _peer(me, m)) for pi, m in enumerate([_SIBLING] + _CHIPS)]
            hops = [(remote(block(dst[n], _peer(me, m)), block(dst[n], _peer(me, m)), 1 + j, n, _peer(me, m)),
                     remote(block(dst[n], _peer(me, m)), block(dst[n], _peer(me, m)), 4 + j, n, sib))
                    for j, m in enumerate(_CHIPS)]
            lands = [remote(own, block(dst[n], sib), 0, n, sib)]
            lands += [remote(own, block(dst[n], _peer(sib, m)), 4 + j, n, sib) for j, m in enumerate(_CHIPS)]
        else:
            local = pltpu.make_async_copy(block(src[n], me), dst[n].at[_index(me)], local_sems.at[n])
            sends = [remote(block(src[n], _peer(me, m)), dst[n].at[_index(me)], pi, n, _peer(me, m))
                     for pi, m in enumerate(_MASKS)]
            hops = []
            lands = [remote(block(src[n], me), dst[n].at[_index(_peer(me, m))], pi, n, _peer(me, m))
                     for pi, m in enumerate(_MASKS)]
        return local, sends, hops, lands

    def start():
        for n in range(len(items)):
            local, sends, _, _ = parts(n)
            local.start()
            for cp in sends:
                cp.start()

    def relay():
        for n in range(len(items)):
            for arrived, onward in parts(n)[2]:
                arrived.wait_recv()
                onward.start()

    def finish():
        for n in range(len(items)):
            local, sends, hops, lands = parts(n)
            for cp in lands:
                cp.wait_recv()
            for cp in sends + [onward for _, onward in hops]:
                cp.wait_send()
            local.wait()

    return start, relay, finish


def _hosted_call(body, xchg, *, name, grid, in_specs, out_specs, out_shape, args, scratch_shapes=(), relay_at=0.8):
    if not xchg:
        outs = pl.pallas_call(
            body, name=name, grid=grid, in_specs=list(in_specs), out_specs=list(out_specs),
            out_shape=list(out_shape), scratch_shapes=list(scratch_shapes), compiler_params=_cp())(*args)
        return outs, []
    items = [it for it, _ in xchg]
    n_in, n_out, n_scr, nit = len(args), len(out_shape), len(scratch_shapes), len(items)
    hbm = pl.BlockSpec(memory_space=pl.ANY)
    steps = int(np.prod(grid))
    relay_step = min(int(relay_at * steps), steps - 1)

    def dst_shape(it):
        r = SHARD_ROWS[it[1]]
        return SDS((NDEV * r, D) if it[0] == "gather" else (NDEV, r, D), BF16)

    def wrapped(*refs):
        ins = refs[:n_in]
        src = refs[n_in:n_in + nit]
        outs = refs[n_in + nit:n_in + nit + n_out]
        dst = refs[n_in + nit + n_out:n_in + 2 * nit + n_out]
        scratch = refs[n_in + 2 * nit + n_out:n_in + 2 * nit + n_out + n_scr]
        start, relay, finish = _exchange_phases(items, src, dst, refs[n_in + 2 * nit + n_out + n_scr:])
        step = 0
        for d, g in enumerate(grid):
            step = step * g + pl.program_id(d)
        pl.when(step == 0)(start)
        body(*ins, *outs, *scratch)
        pl.when(step == relay_step)(relay)
        pl.when(step == steps - 1)(finish)

    npeer = len(_MASKS)
    res = pl.pallas_call(
        wrapped, name=name, grid=grid,
        in_specs=list(in_specs) + [hbm] * nit,
        out_specs=list(out_specs) + [hbm] * nit,
        out_shape=list(out_shape) + [dst_shape(it) for it in items],
        scratch_shapes=list(scratch_shapes) + [
            pltpu.SemaphoreType.DMA((npeer, nit)), pltpu.SemaphoreType.DMA((npeer, nit)), pltpu.SemaphoreType.DMA((nit,))],
        compiler_params=_cp(),
    )(*args, *[a for _, a in xchg])
    return list(res[:n_out]), list(res[n_out:])


def _sum_slots(slots, xchg=None):
    _, r, _ = slots[0].shape
    n = len(slots) // L
    rt = 64

    def body(*refs):
        for k in range(n):
            for l in range(L):
                src = refs[k * L + l]
                acc = src[0].astype(F32)
                for d in range(1, NDEV):
                    acc = acc + src[d].astype(F32)
                refs[n * L + k][l] = acc

    return _hosted_call(
        body, xchg, name=f"sum_slots_r{r}" + ("_x" if xchg else ""), grid=(r // rt,),
        in_specs=[pl.BlockSpec((NDEV, rt, D), lambda i: (0, i, 0))] * (n * L),
        out_specs=[pl.BlockSpec((L, rt, D), lambda i: (0, i, 0))] * n,
        out_shape=[SDS((L, r, D), F32)] * n, args=list(slots))


def _exchange_small(v, reduce):
    rows = v.shape[0]

    def body(v_ref, o_ref, *scratch):
        if reduce:
            slots, send_sems, recv_sems = scratch
        else:
            slots = o_ref
            send_sems, recv_sems = scratch
        me = _position()
        sib = _peer(me, _SIBLING)

        def remote(s_ref, pos, pi, to):
            return pltpu.make_async_remote_copy(
                src_ref=s_ref, dst_ref=slots.at[_index(pos)], send_sem=send_sems.at[pi], recv_sem=recv_sems.at[pi],
                device_id=to, device_id_type=MESH_ID)

        slots[_index(me)] = v_ref[...]
        sends = [remote(v_ref, me, pi, _peer(me, m)) for pi, m in enumerate([_SIBLING] + _CHIPS)]
        for cp in sends:
            cp.start()
        for j, m in enumerate(_CHIPS):
            peer = _peer(me, m)
            remote(v_ref, peer, 1 + j, peer).wait_recv()
            onward = remote(slots.at[_index(peer)], peer, 4 + j, sib)
            onward.start()
            sends.append(onward)
        remote(v_ref, sib, 0, sib).wait_recv()
        for j, m in enumerate(_CHIPS):
            remote(v_ref, _peer(sib, m), 4 + j, sib).wait_recv()
        for cp in sends:
            cp.wait_send()
        if reduce:
            acc = slots[0]
            for d in range(1, NDEV):
                acc = acc + slots[d]
            o_ref[...] = acc

    vm = pl.BlockSpec(memory_space=pltpu.VMEM)
    sems = [pltpu.SemaphoreType.DMA((len(_MASKS),)), pltpu.SemaphoreType.DMA((len(_MASKS),))]
    return pl.pallas_call(
        body, name="reduce_small" if reduce else "gather_small",
        in_specs=[vm], out_specs=vm,
        out_shape=SDS((rows, LANE) if reduce else (NDEV, rows, LANE), F32),
        scratch_shapes=([pltpu.VMEM((NDEV, rows, LANE), F32)] if reduce else []) + sems,
        compiler_params=_cp(),
    )(v)


def _adamw_update(w_ref, g_ref, m_ref, v_ref, d_ref, nm_ref, nv_ref):
    gv = g_ref[...]
    mn = B1 * m_ref[...] + (1.0 - B1) * gv
    vn = B2 * v_ref[...] + (1.0 - B2) * jnp.square(gv)
    nm_ref[...] = mn
    nv_ref[...] = vn
    m_hat = mn / (1.0 - B1 ** STEP)
    v_hat = vn / (1.0 - B2 ** STEP)
    d_ref[...] = -LR * (m_hat / (jnp.sqrt(v_hat) + AEPS) + WD * w_ref[...])


def _adamw_small(ws, gs, ms, vs):
    n = len(ws)

    def body(*refs):
        for i in range(n):
            _adamw_update(*[refs[j * n + i] for j in range(7)])

    vm = pl.BlockSpec(memory_space=pltpu.VMEM)
    res = pl.pallas_call(
        body, name="adamw_small", in_specs=[vm] * (4 * n), out_specs=[vm] * (3 * n),
        out_shape=[SDS(w.shape, F32) for _ in range(3) for w in ws],
        compiler_params=_cp(),
    )(*ws, *gs, *ms, *vs)
    return res[:n], res[n:2 * n], res[2 * n:]


def _adamw(w, g, m, v):
    rows, cols = w.shape
    t = rows
    for cand in (512, 256, 128, 64, 32, 16, 8):
        if rows % cand == 0:
            t = cand
            break

    def body(*refs):
        _adamw_update(*refs)

    blk = pl.BlockSpec((t, cols), lambda i: (i, 0))
    return pl.pallas_call(
        body, name=f"adamw_{rows}x{cols}", grid=(rows // t,),
        in_specs=[blk] * 4, out_specs=[blk] * 3,
        out_shape=[SDS((rows, cols), F32)] * 3,
        compiler_params=_cp(),
    )(w, g, m, v)


_DIST0 = BAND + TQ - 1
_N_FAR = _DIST0 - REL_CLIP + 1
_N_NEAR = NG - _N_FAR - (2 * REL_CLIP - 1)


def _bias_vector(rel_bias):
    far = jnp.broadcast_to(rel_bias[..., -1:], (L, NH, _N_FAR))
    near = jnp.broadcast_to(rel_bias[..., :1], (L, NH, _N_NEAR))
    return jnp.concatenate([far, lax.rev(rel_bias[..., 1:-1], (2,)), near], axis=2)[:, :, None, :]


def _bias_vector_grad(dgr):
    first = jnp.sum(dgr[..., :_N_NEAR], axis=-1, keepdims=True)
    last = jnp.sum(dgr[..., NG - _N_FAR:], axis=-1, keepdims=True)
    return jnp.concatenate([first, dgr[..., _N_NEAR:NG - _N_FAR], last], axis=-1)


def _pool_blockdiag(pool_w):
    eye = jnp.eye(2, dtype=F32)
    pw = pool_w.reshape(L, 2, 2, HD, HD)
    return jnp.einsum("ljaik,ab->ljaibk", pw, eye).reshape(L, 2, LANE, LANE)


def _pool_blockdiag_grad(dwbd):
    d = dwbd.reshape(L, 2, 2, HD, 2, HD)
    return jnp.stack([d[:, :, 0, :, 0, :], d[:, :, 1, :, 1, :]], axis=2).reshape(L, 4, HD, HD)


def _pack(arrays, rows):
    flat = jnp.concatenate([a.reshape(-1).astype(F32) for a in arrays])
    return jnp.pad(flat, (0, rows * LANE - flat.shape[0])).reshape(rows, LANE)


def _unpack(packed, shapes):
    flat = packed.reshape(-1)
    out, o = [], 0
    for shp in shapes:
        n = int(np.prod(shp))
        out.append(flat[o:o + n].reshape(shp))
        o += n
    return out


def _rows_for(shapes):
    n = sum(int(np.prod(s)) for s in shapes)
    return -(-n // (8 * LANE)) * 8


def _grads(x, target, small_w, shards):
    g1, qg, kg, rb, cw, pw, ps, g2 = small_w
    g1 = g1.reshape(L, 1, D)
    g2 = g2.reshape(L, 1, D)
    qg2 = jnp.tile(qg, (1, 2)).reshape(L, 1, LANE)
    kg2 = jnp.tile(kg, (1, 2)).reshape(L, 1, LANE)
    ps3 = ps.reshape(L, 1, PWD)
    wbd = _pool_blockdiag(pw).astype(BF16)

    def gather(*kl):
        return [(("gather", k, l), shards[k]) for k, l in kl if l < L]

    full = {}

    def arrived(got, *kl):
        full.update(zip([x for x in kl if x[1] < L], got))

    bias, got = _bias_tables(_bias_vector(rb), gather((0, 0)))
    arrived(got, (0, 0))
    saved = []
    h = x
    for l in range(L):
        kl = ((1, 0),) if l == 0 else ()
        (p, h_b, qkv), got = _in_proj(h, g1, full[0, l], qg2, kg2, l, gather(*kl))
        arrived(got, *kl)
        kl = ((2, 0), (3, 0)) if l == 0 else ((1, l), (3, l))
        (mix, lse, o32), got = _attn_fwd(qkv, bias, l, gather(*kl), relay_at=0.95 if l == 0 else 0.8)
        arrived(got, *kl)
        mix = _conv_pool_fwd(p, mix, cw, wbd, ps3, l)
        kl = ((0, l + 1), (2, l + 1))
        (xm, a, *out), got = _mlp_fwd(h, mix, full[1, l], g2, full[2, l], full[3, l], l, gather(*kl),
                                      target if l == L - 1 else None)
        arrived(got, *kl)
        saved.append((h, h_b, p, qkv, mix, lse, o32, xm, a))
        h = out[0]
    dx, sq = out

    grads = {}
    slots = {}

    def scatter(*kl):
        return [(("scatter", k), grads[k, l]) for k, l in kl if l < L]

    def left(got, *kl):
        slots.update(zip([x for x in kl if x[1] < L], got))

    per_layer = [None] * L
    for l in reversed(range(L)):
        x_in, h_b, p, qkv, mix, lse, o32, xm, a = saved[l]
        (dxm, dmix, f_b, da_b, h2_b, dxo_b, dxm_b, dg2), got = _mlp_bwd(
            dx, a, xm, g2, full[2, l], full[3, l], full[1, l], l, scatter((3, l + 1)))
        left(got, (3, l + 1))
        grads[1, l], _ = _wgrad(mix, dxm_b, f"w_out_l{l}")
        kl = ((1, 0),) if l == 0 else ()
        grads[2, l], got = _wgrad(da_b, h2_b, f"w_mlp1_l{l}", scatter(*kl))
        left(got, *kl)
        kl = ((2, 0),) if l == 0 else ()
        grads[3, l], got = _wgrad(f_b, dxo_b, f"w_mlp2_l{l}", scatter(*kl))
        left(got, *kl)
        kl = ((3, 0), (0, 1)) if l == 0 else ((2, l), (1, l))
        (dq, dk, dv, dbias, dqg, dkg), got = _attn_bwd(qkv, p, lse, o32, dmix, qg2, kg2, bias, l, scatter(*kl))
        left(got, *kl)
        dgb, dgc, dhin, du, dcw, dwbd, dps = _conv_pool_bwd(p, dmix, cw, wbd, ps3, l)
        (dx, dp_b, dg1), got = _in_proj_bwd(
            [dq, dk, dv, dgb, dgc, dhin, du], x_in, dxm, g1, full[0, l], l, scatter((0, l + 1)) if l else None)
        left(got, (0, l + 1))
        grads[0, l], _ = _wgrad(dp_b, h_b, f"w_in_l{l}")
        per_layer[l] = (dg1, dg2, dqg, dkg, _bias_tables_grad(dbias, l), dcw, dwbd, dps)
    (g_w1_t, g_w2), got = _sum_slots([slots[k, l] for k in (2, 3) for l in range(L)], scatter((0, 0)))
    left(got, (0, 0))
    sums = [_sum_slots([slots[k, l] for l in range(L)])[0][0] for k in (0, 1)] + [g_w1_t, g_w2]

    st = [jnp.stack([per_layer[l][k] for l in range(L)]) for k in range(8)]
    small = dict(
        g1=st[0].reshape(L, D), g2=st[1].reshape(L, D),
        qg=st[2].reshape(L, NH, HD).sum(1), kg=st[3].reshape(L, NH, HD).sum(1),
        rb=_bias_vector_grad(st[4].reshape(L, NH, NG)), cw=st[5], pw=_pool_blockdiag_grad(st[6]),
        ps=st[7].reshape(L, PWD))
    return sq, dx, sums, small


def kernel(x, norm1_g, w_in, q_norm_g, k_norm_g, rel_bias, conv_w, pool_w, pool_scale, w_out, norm2_g, w_mlp1, w_mlp2, loss_target, m_norm1_g, m_w_in, m_q_norm_g, m_k_norm_g, m_rel_bias, m_conv_w, m_pool_w, m_pool_scale, m_w_out, m_norm2_g, m_w_mlp1, m_w_mlp2, v_norm1_g, v_w_in, v_q_norm_g, v_k_norm_g, v_rel_bias, v_conv_w, v_pool_w, v_pool_scale, v_w_out, v_norm2_g, v_w_mlp1, v_w_mlp2):
    me = _index(_position())
    cshard = CW // NDEV

    shards = [jnp.swapaxes(w_in, 1, 2).astype(BF16), w_out.astype(BF16),
              jnp.swapaxes(w_mlp1, 1, 2).astype(BF16), w_mlp2.astype(BF16)]
    cw_all = _exchange_small(_pack([conv_w], 8), reduce=False)
    cw_full = jnp.concatenate(
        [cw_all[d].reshape(-1)[:L * 3 * cshard].reshape(L, 3, cshard) for d in range(NDEV)], axis=2)

    small_w = (norm1_g, q_norm_g, k_norm_g, rel_bias, cw_full, pool_w, pool_scale, norm2_g)
    sq, grad_x, (g_win_t, g_wout, g_w1_t, g_w2), small = _grads(x[0], loss_target[0], small_w, shards)
    g_w_in = jnp.swapaxes(g_win_t, 1, 2)
    g_w_mlp1 = jnp.swapaxes(g_w1_t, 1, 2)

    names = ("g1", "qg", "kg", "rb", "cw", "pw", "ps", "g2")
    gshapes = [(L, D), (L, HD), (L, HD), (L, NH, 2 * REL_CLIP + 1), (L, 3, CW), (L, 4, HD, HD), (L, PWD), (L, D)]
    garrs = [small[n] for n in names]
    rows = _rows_for(gshapes + [(1,)])
    total = _exchange_small(_pack(garrs + [sq[0, :1]], rows), reduce=True)
    g_g1, g_qg, g_kg, g_rb, g_cw_full, g_pw, g_ps, g_g2, sq_sum = _unpack(total, gshapes + [(1,)])
    loss = (0.5 / D) * sq_sum[0]
    g_cw = lax.dynamic_slice_in_dim(g_cw_full, me * cshard, cshard, axis=2)

    def big(w, g, m, v):
        shp = w.shape
        r = lambda a: a.reshape(-1, shp[-1])
        return [o.reshape(shp) for o in _adamw(r(w), r(g), r(m), r(v))]

    up_in = big(w_in, g_w_in, m_w_in, v_w_in)
    up_out = big(w_out, g_wout, m_w_out, v_w_out)
    up_1 = big(w_mlp1, g_w_mlp1, m_w_mlp1, v_w_mlp1)
    up_2 = big(w_mlp2, g_w2, m_w_mlp2, v_w_mlp2)

    sw = [norm1_g, q_norm_g, k_norm_g, rel_bias, conv_w, pool_w, pool_scale, norm2_g]
    sg = [g_g1, g_qg, g_kg, g_rb, g_cw, g_pw, g_ps, g_g2]
    sm = [m_norm1_g, m_q_norm_g, m_k_norm_g, m_rel_bias, m_conv_w, m_pool_w, m_pool_scale, m_norm2_g]
    sv = [v_norm1_g, v_q_norm_g, v_k_norm_g, v_rel_bias, v_conv_w, v_pool_w, v_pool_scale, v_norm2_g]
    s_delta, s_m, s_v = _adamw_small(sw, sg, sm, sv)

    def order(small_list, in_, out_, m1, m2):
        g1_, qg_, kg_, rb_, cw_, pw_, ps_, g2_ = small_list
        return [g1_, in_, qg_, kg_, rb_, cw_, pw_, ps_, out_, g2_, m1, m2]

    grads = order(sg, g_w_in, g_wout, g_w_mlp1, g_w2)
    deltas = order(s_delta, up_in[0], up_out[0], up_1[0], up_2[0])
    new_m = order(s_m, up_in[1], up_out[1], up_1[1], up_2[1])
    new_v = order(s_v, up_in[2], up_out[2], up_1[2], up_2[2])
    return (loss, grad_x[None], *grads, *deltas, *new_m, *new_v)
```

```python
import numpy as np
import jax
import jax.numpy as jnp
from jax import lax
from jax.experimental import pallas as pl
from jax.experimental.pallas import tpu as pltpu

F32 = jnp.float32
BF16 = jnp.bfloat16
SDS = jax.ShapeDtypeStruct
MESH_ID = pl.DeviceIdType.MESH

D = 1024
L = 4
CHUNK = 64
N_PREV = 8
HD = 64
NH = 8
AW = 512
CW = 256
PWD = 256
DIN = 2560
DFF = 4096
EPS = 1e-6
NEG = -1e30
REL_CLIP = 128
POOL_WINDOWS = (2, 4, 8, 16)
LR, B1, B2, AEPS, WD, STEP = 0.001, 0.9, 0.999, 1e-08, 0.01, 10

NDEV = 8
LANE = 128
BAND = N_PREV * CHUNK
TQ = 256
WIN = TQ + BAND
NVAR = BAND // TQ + 1
NTOE = -(-(WIN + TQ - 1) // LANE) * LANE
NG = (NVAR - 1) * TQ + NTOE
PAD = 16
RB_NORM = 64
RB_SOFT = 16
VMEM_LIMIT = 56 * 1024 * 1024
SHARD_ROWS = (DIN // NDEV, D // NDEV, DFF // NDEV, DFF // NDEV)

assert 2 * HD == LANE and NH * HD == AW and POOL_WINDOWS == (2, 4, 8, 16)
assert TQ % CHUNK == 0 and BAND % TQ == 0 and max(POOL_WINDOWS) <= PAD and all(r % 16 == 0 for r in SHARD_ROWS)


def _cp(**kw):
    return pltpu.CompilerParams(vmem_limit_bytes=VMEM_LIMIT, **kw)


def _nn(a, b):
    return jnp.dot(a, b, preferred_element_type=F32)


def _nt(a, b):
    return lax.dot_general(a, b, (((1,), (1,)), ((), ())), preferred_element_type=F32)


def _tn(a, b):
    return lax.dot_general(a, b, (((0,), (0,)), ((), ())), preferred_element_type=F32)


def _const(shape):
    n = len(shape)
    return pl.BlockSpec(shape, lambda *_: (0,) * n, pipeline_mode=pl.Buffered(1))


def _layer(shape, l):
    n = len(shape)
    return pl.BlockSpec((None,) + tuple(shape), lambda *_: (l,) + (0,) * n, pipeline_mode=pl.Buffered(1))


def _lo_mask():
    return lax.broadcasted_iota(jnp.int32, (1, LANE), 1) < HD


def _half_sum(t, lo):
    s_lo = jnp.sum(jnp.where(lo, t, 0.0), axis=-1, keepdims=True)
    s_hi = jnp.sum(jnp.where(lo, 0.0, t), axis=-1, keepdims=True)
    return jnp.where(lo, s_lo, s_hi)


def _head_norm(x, lo):
    r = lax.rsqrt(_half_sum(x * x, lo) * (1.0 / HD) + EPS)
    return x * r, r


def _head_norm_bwd(dy, xn, r, g, lo):
    dxn = dy * g
    mu = _half_sum(dxn * xn, lo) * (1.0 / HD)
    return r * (dxn - xn * mu), dy * xn


def _rms_bwd(dy, x, g):
    r = lax.rsqrt(jnp.mean(x * x, axis=-1, keepdims=True) + EPS)
    xn = x * r
    dxn = dy * g
    mu = jnp.mean(dxn * xn, axis=-1, keepdims=True)
    return r * (dxn - xn * mu), dy * xn


def _in_proj(x, g1, win_t, qg2, kg2, l, xchg=None):
    s = x.shape[0]
    t = min(512, s)
    nblk = AW // LANE

    def body(x_ref, g_ref, w_ref, qg_ref, kg_ref, p_ref, h_ref, qkv_ref):
        xv = x_ref[...]
        r = lax.rsqrt(jnp.mean(xv * xv, axis=-1, keepdims=True) + EPS)
        h = (xv * r * g_ref[...]).astype(BF16)
        h_ref[...] = h
        p_ref[...] = _nt(h, w_ref[...])
        lo = _lo_mask()
        gains = (qg_ref[...] * (HD ** -0.5), kg_ref[...])
        for r0 in range(0, t, RB_NORM):
            rows = pl.ds(r0, RB_NORM)
            for c in range(3 * nblk):
                cols = pl.ds(c * LANE, LANE)
                v = p_ref[rows, cols]
                if c < 2 * nblk:
                    v = _head_norm(v, lo)[0] * gains[c // nblk]
                qkv_ref[rows, cols] = v.astype(BF16)

    row = lambda c: pl.BlockSpec((t, c), lambda i: (i, 0))
    return _hosted_call(
        body, xchg, name="in_proj", grid=(s // t,),
        in_specs=[row(D), _layer((1, D), l), _const((DIN, D)), _layer((1, LANE), l), _layer((1, LANE), l)],
        out_specs=[row(DIN), row(D), row(3 * AW)],
        out_shape=[SDS((s, DIN), F32), SDS((s, D), BF16), SDS((s, 3 * AW), BF16)], args=[x, g1, win_t, qg2, kg2])


def _bias_spec():
    return pl.BlockSpec((None, 2, TQ, WIN), lambda j, i: (jnp.maximum(NVAR - 1 - i, 0), j, 0, 0))


def _bias_layer_spec(l):
    return pl.BlockSpec((None, None, 2, TQ, WIN), lambda j, i: (l, jnp.maximum(NVAR - 1 - i, 0), j, 0, 0))


def _attn_fwd(qkv, bias, l, xchg=None, relay_at=0.8):
    s = qkv.shape[0]
    nq = s // TQ

    def body(q_ref, k_ref, v_ref, b_ref, o_ref, lse_ref, o32_ref, s_ref, p_ref, m_ref, den_ref, o0_ref):
        i = pl.program_id(1)
        ks = pl.multiple_of(jnp.maximum(i * TQ - BAND, 0), TQ)
        lo = _lo_mask()
        q = q_ref[...]
        kwin = k_ref[pl.ds(ks, WIN), :]
        vwin = v_ref[pl.ds(ks, WIN), :]
        for half in range(2):
            m_ = lo if half == 0 else jnp.logical_not(lo)
            s_ref[half] = _nt(jnp.where(m_, q, jnp.zeros_like(q)), kwin)
            for r0 in range(0, TQ, RB_SOFT):
                rows = pl.ds(r0, RB_SOFT)
                mx = jnp.max(s_ref[half, rows, :] + b_ref[half, rows, :], axis=-1, keepdims=True)
                m_ref[rows, :] = jnp.broadcast_to(mx, (RB_SOFT, LANE))
            for r0 in range(0, TQ, RB_SOFT):
                rows = pl.ds(r0, RB_SOFT)
                mx = m_ref[rows, 0:1]
                e = jnp.exp(s_ref[half, rows, :] + b_ref[half, rows, :] - mx)
                p_ref[half, rows, :] = e.astype(BF16)
                den = jnp.sum(e, axis=-1, keepdims=True)
                den_ref[rows, :] = jnp.broadcast_to(den, (RB_SOFT, LANE))
                lse = mx + jnp.log(den)
                if half == 0:
                    lse_ref[rows, :] = jnp.broadcast_to(lse, (RB_SOFT, LANE))
                else:
                    lse_ref[rows, :] = jnp.where(lo, lse_ref[rows, :], lse)
            o = _nn(p_ref[half], vwin) * (1.0 / den_ref[...])
            if half == 0:
                o0_ref[...] = o
            else:
                o = jnp.where(lo, o0_ref[...], o)
                o32_ref[...] = o
                o_ref[...] = o.astype(BF16)

    tile = pl.BlockSpec((TQ, LANE), lambda j, i: (i, j))
    stat = pltpu.VMEM((TQ, LANE), F32)
    return _hosted_call(
        body, xchg, name="attn_fwd", grid=(NH // 2, nq),
        in_specs=[
            tile,
            pl.BlockSpec((s, LANE), lambda j, i: (0, AW // LANE + j)),
            pl.BlockSpec((s, LANE), lambda j, i: (0, 2 * AW // LANE + j)),
            _bias_layer_spec(l),
        ],
        out_specs=[tile, tile, tile],
        out_shape=[SDS((s, D), BF16), SDS((s, AW), F32), SDS((s, AW), F32)], args=[qkv, qkv, qkv, bias],
        scratch_shapes=[pltpu.VMEM((2, TQ, WIN), F32), pltpu.VMEM((2, TQ, WIN), BF16), stat, stat, stat],
        relay_at=relay_at)


_C0 = 3 * AW // LANE


def _cp_in_specs(s, l):
    blk = lambda f: pl.BlockSpec((s, LANE), f)
    return [
        blk(lambda g: (0, _C0 + jnp.minimum(g, 1))),
        blk(lambda g: (0, _C0 + 2 + jnp.minimum(g, 1))),
        blk(lambda g: (0, _C0 + 4 + jnp.minimum(g, 1))),
        blk(lambda g: (0, _C0 + 6 + jnp.maximum(g - 2, 0))),
        pl.BlockSpec((None, 3, LANE), lambda g: (l, 0, jnp.minimum(g, 1))),
        pl.BlockSpec((None, None, LANE, LANE), lambda g: (l, jnp.maximum(g - 2, 0), 0, 0)),
        pl.BlockSpec((None, 1, LANE), lambda g: (l, 0, jnp.maximum(g - 2, 0))),
    ]


def _pool_window_sums(u_ref, buf_a, buf_b, jj, s, rt):
    nrt = s // rt
    for t in range(nrt):
        buf_a[pl.ds(PAD + t * rt, rt), :] = u_ref[pl.ds(t * rt, rt), :]

    def stage(src, dst, sh):
        for t in range(nrt):
            r0 = PAD + t * rt
            dst[pl.ds(r0, rt), :] = src[pl.ds(r0, rt), :] + src[pl.ds(r0 - sh, rt), :]

    stage(buf_a, buf_b, 1)
    stage(buf_b, buf_a, 2)

    @pl.when(jj == 1)
    def _():
        stage(buf_a, buf_b, 4)
        stage(buf_b, buf_a, 8)


def _pool_counts(jj, lo, r0, rt):
    w0, w1, w2, w3 = [float(w) for w in POOL_WINDOWS]
    w = jnp.where(lo, jnp.where(jj == 0, w0, w2), jnp.where(jj == 0, w1, w3))
    pos1 = (lax.broadcasted_iota(jnp.int32, (rt, LANE), 0) + (r0 + 1)).astype(F32)
    return jnp.minimum(pos1, w)


def _conv_pool_fwd(p, mix, conv_w, wbd, pscale, l):
    s = p.shape[0]
    rt = min(256, s)
    nrt = s // rt

    def body(gb_ref, gc_ref, hin_ref, u_ref, cw_ref, wbd_ref, ps_ref, mix_in, o_ref, buf_a, buf_b):
        del mix_in
        g = pl.program_id(0)
        zpad = jnp.zeros((PAD, LANE), F32)
        buf_a[pl.ds(0, PAD), :] = zpad
        buf_b[pl.ds(0, PAD), :] = zpad

        @pl.when(g < 2)
        def _conv():
            for t in range(nrt):
                buf_a[pl.ds(PAD + t * rt, rt), :] = gc_ref[pl.ds(t * rt, rt), :] * hin_ref[pl.ds(t * rt, rt), :]
            w0, w1, w2 = cw_ref[0:1, :], cw_ref[1:2, :], cw_ref[2:3, :]
            for t in range(nrt):
                r0 = PAD + t * rt
                y = w0 * buf_a[pl.ds(r0 - 2, rt), :] + w1 * buf_a[pl.ds(r0 - 1, rt), :] + w2 * buf_a[pl.ds(r0, rt), :]
                o_ref[pl.ds(t * rt, rt), :] = (gb_ref[pl.ds(t * rt, rt), :] * y).astype(BF16)

        @pl.when(g >= 2)
        def _pool():
            jj = g - 2
            lo = _lo_mask()
            _pool_window_sums(u_ref, buf_a, buf_b, jj, s, rt)
            wb = wbd_ref[...]
            for t in range(nrt):
                r0 = PAD + t * rt
                wsum = jnp.where(lo, buf_b[pl.ds(r0, rt), :], buf_a[pl.ds(r0, rt), :])
                m = wsum / _pool_counts(jj, lo, t * rt, rt) - u_ref[pl.ds(t * rt, rt), :]
                o_ref[pl.ds(t * rt, rt), :] = (_nn(m.astype(BF16), wb) * ps_ref[...]).astype(BF16)

    return pl.pallas_call(
        body, name="conv_pool_fwd", grid=(4,),
        in_specs=_cp_in_specs(s, l) + [pl.BlockSpec(memory_space=pl.ANY)],
        out_specs=pl.BlockSpec((s, LANE), lambda g: (0, AW // LANE + g)),
        out_shape=SDS((s, D), BF16),
        scratch_shapes=[pltpu.VMEM((s + 2 * PAD, LANE), F32), pltpu.VMEM((s + 2 * PAD, LANE), F32)],
        input_output_aliases={7: 0},
        compiler_params=_cp(),
    )(p, p, p, p, conv_w, wbd, pscale, mix)


def _mlp_fwd(x, mix, wout, g2, w1_t, w2, l, xchg=None, target=None):
    s = x.shape[0]
    t = min(256, s)

    def body(*refs):
        x_ref, mix_ref, wo_ref, g_ref, w1_ref, w2_ref = refs[:6]
        xm_ref, a_ref, xo_ref = refs[-4:-1] if target is not None else refs[-3:]
        xm = x_ref[...] + _nn(mix_ref[...], wo_ref[...])
        xm_ref[...] = xm
        r = lax.rsqrt(jnp.mean(xm * xm, axis=-1, keepdims=True) + EPS)
        h2 = (xm * r * g_ref[...]).astype(BF16)
        a = _nt(h2, w1_ref[...])
        a_ref[...] = a.astype(BF16)
        f = jnp.square(jnp.maximum(a, 0.0)).astype(BF16)
        xo = xm + _nn(f, w2_ref[...])
        if target is None:
            xo_ref[...] = xo
        else:
            acc_ref = refs[-1]

            @pl.when(pl.program_id(0) == 0)
            def _():
                acc_ref[...] = jnp.zeros_like(acc_ref)
            e = xo - refs[6][...]
            xo_ref[...] = e * (1.0 / D)
            acc_ref[...] += jnp.sum(e * e)

    row = lambda c: pl.BlockSpec((t, c), lambda i: (i, 0))
    last = target is not None
    return _hosted_call(
        body, xchg, name="mlp_fwd_loss" if last else "mlp_fwd", grid=(s // t,),
        in_specs=[row(D), row(D), _const((D, D)), _layer((1, D), l), _const((DFF, D)), _const((DFF, D))] + [row(D)] * last,
        out_specs=[row(D), row(DFF), row(D)] + [pl.BlockSpec((8, LANE), lambda i: (0, 0))] * last,
        out_shape=[SDS((s, D), F32), SDS((s, DFF), BF16), SDS((s, D), F32)] + [SDS((8, LANE), F32)] * last,
        args=[x, mix, wout, g2, w1_t, w2] + [target] * last)


def _mlp_bwd(dxo, a, xm, g2, w1_t, w2, wout, l, xchg=None):
    s = dxo.shape[0]
    t = min(256, s)

    def body(dxo_ref, a_ref, xm_ref, g_ref, w1_ref, w2_ref, wo_ref,
             dxm_ref, dmix_ref, f_ref, da_ref, h2_ref, dxob_ref, dxmb_ref, dg_ref):
        @pl.when(pl.program_id(0) == 0)
        def _():
            dg_ref[...] = jnp.zeros_like(dg_ref)
        dxo = dxo_ref[...]
        dxob = dxo.astype(BF16)
        dxob_ref[...] = dxob
        ra = jnp.maximum(a_ref[...].astype(F32), 0.0)
        f_ref[...] = jnp.square(ra).astype(BF16)
        dab = (_nt(dxob, w2_ref[...]) * (2.0 * ra)).astype(BF16)
        da_ref[...] = dab
        dh2 = _nn(dab, w1_ref[...])
        xm = xm_ref[...]
        g = g_ref[...]
        r = lax.rsqrt(jnp.mean(xm * xm, axis=-1, keepdims=True) + EPS)
        h2_ref[...] = (xm * r * g).astype(BF16)
        dx_n, dgr = _rms_bwd(dh2, xm, g)
        dg_ref[...] += jnp.sum(dgr, axis=0, keepdims=True)
        dxm = dxo + dx_n
        dxm_ref[...] = dxm
        dxmb = dxm.astype(BF16)
        dxmb_ref[...] = dxmb
        dmix_ref[...] = _nt(dxmb, wo_ref[...])

    row = lambda c: pl.BlockSpec((t, c), lambda i: (i, 0))
    return _hosted_call(
        body, xchg, name="mlp_bwd", grid=(s // t,),
        in_specs=[row(D), row(DFF), row(D), _layer((1, D), l), _const((DFF, D)), _const((DFF, D)), _const((D, D))],
        out_specs=[row(D), row(D), row(DFF), row(DFF), row(D), row(D), row(D), pl.BlockSpec((1, D), lambda i: (0, 0))],
        out_shape=[SDS((s, D), F32), SDS((s, D), F32), SDS((s, DFF), BF16), SDS((s, DFF), BF16),
                   SDS((s, D), BF16), SDS((s, D), BF16), SDS((s, D), BF16), SDS((1, D), F32)],
        args=[dxo, a, xm, g2, w1_t, w2, wout])


def _attn_bwd(qkv, p, lse, o32, dmix, qg2, kg2, bias, l, xchg=None):
    s = p.shape[0]
    nq = s // TQ
    scale = HD ** -0.5

    def body(qs_ref, kb_ref, vb_ref, q_ref, k_ref, qg_ref, kg_ref, b_ref, lse_ref, o_ref, do_ref,
             dq_ref, dk_ref, dv_ref, db_ref, dqg_ref, dkg_ref,
             dk_acc, dv_acc, s_ref, dp_ref, ds_ref, pb_ref, dqn_ref, dl_ref):
        i = pl.program_id(1)
        kt = jnp.maximum(i - BAND // TQ, 0)
        ks = pl.multiple_of(kt * TQ, TQ)
        lo = _lo_mask()

        @pl.when(i == 0)
        def _():
            dk_acc[...] = jnp.zeros_like(dk_acc)
            dv_acc[...] = jnp.zeros_like(dv_acc)
            dqg_ref[...] = jnp.zeros_like(dqg_ref)
            dkg_ref[...] = jnp.zeros_like(dkg_ref)

        @pl.when(i < NVAR)
        def _():
            db_ref[...] = jnp.zeros_like(db_ref)

        qs = qs_ref[...]
        kwin = kb_ref[pl.ds(ks, WIN), :]
        vwin = vb_ref[pl.ds(ks, WIN), :]
        do = do_ref[...]
        dob = do.astype(BF16)
        dl_ref[...] = _half_sum(do * o_ref[...], lo)
        for half in range(2):
            m_ = lo if half == 0 else jnp.logical_not(lo)
            qa = jnp.where(m_, qs, jnp.zeros_like(qs))
            doa = jnp.where(m_, dob, jnp.zeros_like(dob))
            s_ref[half] = _nt(qa, kwin)
            dp_ref[half] = _nt(doa, vwin)
            for r0 in range(0, TQ, RB_SOFT):
                rows = pl.ds(r0, RB_SOFT)
                lse_h = lse_ref[rows, half * HD:half * HD + 1]
                pm = jnp.exp(s_ref[half, rows, :] + b_ref[half, rows, :] - lse_h)
                ds = pm * (dp_ref[half, rows, :] - dl_ref[rows, half * HD:half * HD + 1])
                db_ref[half, rows, :] += ds
                ds_ref[half, rows, :] = ds.astype(BF16)
                pb_ref[half, rows, :] = pm.astype(BF16)
            dsb = ds_ref[half]
            dq_h = _nn(dsb, kwin)
            if half == 0:
                dqn_ref[...] = dq_h
            else:
                dqn_ref[...] = jnp.where(lo, dqn_ref[...], dq_h)
            dk_t = _tn(qa, dsb)
            dv_t = _tn(doa, pb_ref[half])
            for t in range(WIN // TQ):
                dk_acc[kt + t] += dk_t[:, t * TQ:(t + 1) * TQ]
                dv_acc[kt + t] += dv_t[:, t * TQ:(t + 1) * TQ]
        qg, kg = qg_ref[...], kg_ref[...]
        xq, rq = _head_norm(q_ref[...], lo)
        dq, dqg_rows = _head_norm_bwd(dqn_ref[...] * scale, xq, rq, qg, lo)
        dq_ref[...] = dq.astype(BF16)
        dqg_ref[...] += jnp.sum(dqg_rows, axis=0, keepdims=True)

        @pl.when(i == nq - 1)
        def _():
            dkg = jnp.zeros((1, LANE), F32)
            for t in range(nq):
                rows = pl.ds(t * TQ, TQ)
                xk, rk = _head_norm(k_ref[rows, :], lo)
                dk, dkg_rows = _head_norm_bwd(dk_acc[t].T, xk, rk, kg, lo)
                dk_ref[rows, :] = dk.astype(BF16)
                dv_ref[rows, :] = dv_acc[t].T.astype(BF16)
                dkg = dkg + jnp.sum(dkg_rows, axis=0, keepdims=True)
            dkg_ref[...] = dkg

    tile = pl.BlockSpec((TQ, LANE), lambda j, i: (i, j))
    kcol = lambda c0: pl.BlockSpec((s, LANE), lambda j, i: (0, c0 + j))
    gain = pl.BlockSpec((None, 1, LANE), lambda j, i: (j, 0, 0))
    return _hosted_call(
        body, xchg, name="attn_bwd", grid=(NH // 2, nq),
        in_specs=[
            tile, kcol(AW // LANE), kcol(2 * AW // LANE), tile, kcol(AW // LANE),
            _layer((1, LANE), l), _layer((1, LANE), l),
            _bias_layer_spec(l), tile, tile, tile,
        ],
        out_specs=[tile, kcol(0), kcol(0), _bias_spec(), gain, gain],
        out_shape=[SDS((s, AW), BF16), SDS((s, AW), BF16), SDS((s, AW), BF16),
                   SDS((NVAR, NH, TQ, WIN), F32), SDS((NH // 2, 1, LANE), F32), SDS((NH // 2, 1, LANE), F32)],
        scratch_shapes=[pltpu.VMEM((nq, LANE, TQ), F32), pltpu.VMEM((nq, LANE, TQ), F32),
                        pltpu.VMEM((2, TQ, WIN), F32), pltpu.VMEM((2, TQ, WIN), F32),
                        pltpu.VMEM((2, TQ, WIN), BF16), pltpu.VMEM((2, TQ, WIN), BF16),
                        pltpu.VMEM((TQ, LANE), F32), pltpu.VMEM((TQ, LANE), F32)],
        args=[qkv, qkv, qkv, p, p, qg2, kg2, bias, lse, o32, dmix])


def _conv_pool_bwd(p, dmix, conv_w, wbd, pscale, l):
    s = p.shape[0]
    rt = min(256, s)
    nrt = s // rt

    def body(gb_ref, gc_ref, hin_ref, u_ref, cw_ref, wbd_ref, ps_ref, dy_ref,
             dgb_ref, dgc_ref, dhin_ref, du_ref, dcw_ref, dwbd_ref, dps_ref, buf_a, buf_b, buf_c, buf_d):
        g = pl.program_id(0)
        zpad = jnp.zeros((PAD, LANE), F32)
        for buf in (buf_a, buf_b, buf_c):
            buf[pl.ds(0, PAD), :] = zpad
            buf[pl.ds(PAD + s, PAD), :] = zpad

        @pl.when(g < 2)
        def _conv():
            for t in range(nrt):
                rows = pl.ds(t * rt, rt)
                buf_a[pl.ds(PAD + t * rt, rt), :] = gc_ref[rows, :] * hin_ref[rows, :]
                buf_b[pl.ds(PAD + t * rt, rt), :] = dy_ref[rows, :] * gb_ref[rows, :]
            w0, w1, w2 = cw_ref[0:1, :], cw_ref[1:2, :], cw_ref[2:3, :]
            d0 = jnp.zeros((1, LANE), F32)
            d1 = jnp.zeros((1, LANE), F32)
            d2 = jnp.zeros((1, LANE), F32)
            for t in range(nrt):
                rows = pl.ds(t * rt, rt)
                r0 = PAD + t * rt
                z2, z1, z0 = buf_a[pl.ds(r0 - 2, rt), :], buf_a[pl.ds(r0 - 1, rt), :], buf_a[pl.ds(r0, rt), :]
                y = w0 * z2 + w1 * z1 + w2 * z0
                dgb_ref[rows, :] = (dy_ref[rows, :] * y).astype(BF16)
                e0 = buf_b[pl.ds(r0, rt), :]
                d0 = d0 + jnp.sum(e0 * z2, axis=0, keepdims=True)
                d1 = d1 + jnp.sum(e0 * z1, axis=0, keepdims=True)
                d2 = d2 + jnp.sum(e0 * z0, axis=0, keepdims=True)
                dz = w2 * e0 + w1 * buf_b[pl.ds(r0 + 1, rt), :] + w0 * buf_b[pl.ds(r0 + 2, rt), :]
                dgc_ref[rows, :] = (dz * hin_ref[rows, :]).astype(BF16)
                dhin_ref[rows, :] = (dz * gc_ref[rows, :]).astype(BF16)
            dcw_ref[0:1, :] = d0
            dcw_ref[1:2, :] = d1
            dcw_ref[2:3, :] = d2

        @pl.when(g >= 2)
        def _pool():
            jj = g - 2
            lo = _lo_mask()
            _pool_window_sums(u_ref, buf_a, buf_b, jj, s, rt)
            wb = wbd_ref[...]
            ps = ps_ref[...]
            dps = jnp.zeros((1, LANE), F32)
            dwb = jnp.zeros((LANE, LANE), F32)
            for t in range(nrt):
                rows = pl.ds(t * rt, rt)
                r0 = PAD + t * rt
                cnt = _pool_counts(jj, lo, t * rt, rt)
                wsum = jnp.where(lo, buf_b[pl.ds(r0, rt), :], buf_a[pl.ds(r0, rt), :])
                mb = (wsum / cnt - u_ref[rows, :]).astype(BF16)
                dy = dy_ref[rows, :]
                dps = dps + jnp.sum(dy * _nn(mb, wb), axis=0, keepdims=True)
                dmp = (dy * ps).astype(BF16)
                dwb = dwb + _tn(mb, dmp)
                dm = _nt(dmp, wb)
                buf_d[rows, :] = dm
                buf_c[pl.ds(r0, rt), :] = dm / cnt
            dps_ref[...] = dps
            dwbd_ref[...] = dwb

            def stage(src, dst, sh):
                for t in range(nrt):
                    r0 = PAD + t * rt
                    dst[pl.ds(r0, rt), :] = src[pl.ds(r0, rt), :] + src[pl.ds(r0 + sh, rt), :]

            def finish(first, second):
                for t in range(nrt):
                    rows = pl.ds(t * rt, rt)
                    r0 = PAD + t * rt
                    fw = jnp.where(lo, first[pl.ds(r0, rt), :], second[pl.ds(r0, rt), :])
                    du_ref[rows, :] = (fw - buf_d[rows, :]).astype(BF16)

            stage(buf_c, buf_a, 1)
            stage(buf_a, buf_b, 2)

            @pl.when(jj == 0)
            def _():
                finish(buf_a, buf_b)

            @pl.when(jj == 1)
            def _():
                stage(buf_b, buf_c, 4)
                stage(buf_c, buf_a, 8)
                finish(buf_c, buf_a)

    cblk = pl.BlockSpec((s, LANE), lambda g: (0, jnp.minimum(g, 1)))
    pblk = pl.BlockSpec((s, LANE), lambda g: (0, jnp.maximum(g - 2, 0)))
    padded = pltpu.VMEM((s + 2 * PAD, LANE), F32)
    return pl.pallas_call(
        body, name="conv_pool_bwd", grid=(4,),
        in_specs=_cp_in_specs(s, l) + [pl.BlockSpec((s, LANE), lambda g: (0, AW // LANE + g))],
        out_specs=[cblk, cblk, cblk, pblk,
                   pl.BlockSpec((3, LANE), lambda g: (0, jnp.minimum(g, 1))),
                   pl.BlockSpec((None, LANE, LANE), lambda g: (jnp.maximum(g - 2, 0), 0, 0)),
                   pl.BlockSpec((1, LANE), lambda g: (0, jnp.maximum(g - 2, 0)))],
        out_shape=[SDS((s, CW), BF16), SDS((s, CW), BF16), SDS((s, CW), BF16), SDS((s, PWD), BF16),
                   SDS((3, CW), F32), SDS((2, LANE, LANE), F32), SDS((1, PWD), F32)],
        scratch_shapes=[padded, padded, padded, pltpu.VMEM((s, LANE), F32)],
        compiler_params=_cp(),
    )(p, p, p, p, conv_w, wbd, pscale, dmix)


def _in_proj_bwd(parts, x, dxm, g1, win_t, l, xchg=None):
    s = x.shape[0]
    t = min(256, s)
    widths = [a.shape[1] for a in parts]
    offs = [int(o) for o in np.cumsum([0] + widths[:-1])]
    n = len(parts)

    def body(*refs):
        part_refs = refs[:n]
        x_ref, dxm_ref, g_ref, w_ref, dx_ref, dp_ref, dg_ref = refs[n:]

        @pl.when(pl.program_id(0) == 0)
        def _():
            dg_ref[...] = jnp.zeros_like(dg_ref)
        for r, o, w in zip(part_refs, offs, widths):
            dp_ref[:, o:o + w] = r[...]
        dh = _nn(dp_ref[...], w_ref[...])
        dx_n, dgr = _rms_bwd(dh, x_ref[...], g_ref[...])
        dg_ref[...] += jnp.sum(dgr, axis=0, keepdims=True)
        dx_ref[...] = dxm_ref[...] + dx_n

    row = lambda c: pl.BlockSpec((t, c), lambda i: (i, 0))
    return _hosted_call(
        body, xchg, name="in_proj_bwd", grid=(s // t,),
        in_specs=[row(w) for w in widths] + [row(D), row(D), _layer((1, D), l), _const((DIN, D))],
        out_specs=[row(D), row(DIN), pl.BlockSpec((1, D), lambda i: (0, 0))],
        out_shape=[SDS((s, D), F32), SDS((s, DIN), BF16), SDS((1, D), F32)], args=[*parts, x, dxm, g1, win_t])


def _wgrad(a, b, tag, xchg=None):
    s, m = a.shape
    mb = 512

    def body(a_ref, b_ref, o_ref):
        o_ref[...] = _tn(a_ref[...], b_ref[...]).astype(BF16)

    (out,), got = _hosted_call(
        body, xchg, name=f"wgrad_{tag}", grid=(m // mb,),
        in_specs=[pl.BlockSpec((s, mb), lambda mi: (0, mi)), _const((s, D))],
        out_specs=[pl.BlockSpec((mb, D), lambda mi: (mi, 0))],
        out_shape=[SDS((m, D), BF16)], args=[a, b])
    return out, got


def _bias_tables(gvec, xchg=None):
    def body(g_ref, o_ref):
        qc = lax.broadcasted_iota(jnp.int32, (TQ, WIN), 0) // CHUNK
        kc = lax.broadcasted_iota(jnp.int32, (TQ, WIN), 1) // CHUNK
        for var in range(NVAR):
            vec = jnp.broadcast_to(g_ref[:, var * TQ:var * TQ + NTOE], (TQ, NTOE))
            toe = pltpu.roll(vec, NTOE - TQ + 1, 1, stride=1, stride_axis=0)[:, :WIN]
            rel = (BAND - var * TQ) // CHUNK + qc - kc
            o_ref[var] = jnp.where((rel >= 0) & (rel <= N_PREV), toe, NEG)

    (out,), got = _hosted_call(
        body, xchg, name="bias_tables", grid=(L, NH),
        in_specs=[pl.BlockSpec((None, None, 1, NG), lambda l, h: (l, h, 0, 0))],
        out_specs=[pl.BlockSpec((None, NVAR, None, TQ, WIN), lambda l, h: (l, 0, h, 0, 0))],
        out_shape=[SDS((L, NVAR, NH, TQ, WIN), F32)], args=[gvec])
    return out, got


def _bias_tables_grad(dbias, l):
    nb = NTOE // LANE
    wb = WIN // LANE

    def body(d_ref, o_ref):
        ii = lax.broadcasted_iota(jnp.int32, (LANE, LANE), 0)
        jj = lax.broadcasted_iota(jnp.int32, (LANE, LANE), 1)
        flip = jnp.where(ii + jj == LANE - 1, 1.0, 0.0).astype(BF16)
        o_ref[...] = jnp.zeros_like(o_ref)
        for var in range(NVAR):
            blocks = []
            for b in range(nb):
                src = nb - 1 - b
                if src >= wb:
                    blocks.append(jnp.zeros((TQ, LANE), F32))
                    continue
                xv = d_ref[var, :, src * LANE:(src + 1) * LANE]
                hi = xv.astype(BF16)
                lo = (xv - hi.astype(F32)).astype(BF16)
                blocks.append(_nn(hi, flip) + _nn(lo, flip))
            rev = jnp.concatenate(blocks, axis=1)
            skew = pltpu.roll(rev, NTOE - TQ + 1, 1, stride=1, stride_axis=0)
            off = NG - NTOE - var * TQ
            o_ref[:, off:off + NTOE] += jnp.sum(skew, axis=0, keepdims=True)

    return pl.pallas_call(
        body, name=f"bias_tables_grad_l{l}", grid=(NH,),
        in_specs=[pl.BlockSpec((NVAR, None, TQ, WIN), lambda h: (0, h, 0, 0))],
        out_specs=pl.BlockSpec((None, 1, NG), lambda h: (h, 0, 0)),
        out_shape=SDS((NH, 1, NG), F32),
        compiler_params=_cp(),
    )(dbias)


_SIBLING = (0, 0, 1)
_CHIPS = [(1, 0, 0), (0, 1, 0), (1, 1, 0)]
_MASKS = [_SIBLING] + _CHIPS + [(1, 0, 1), (0, 1, 1), (1, 1, 1)]


def _position():
    return lax.axis_index("x"), lax.axis_index("y"), lax.axis_index("c")


def _peer(pos, mask):
    return tuple(1 - a if f else a for a, f in zip(pos, mask))


def _index(pos):
    return 4 * pos[0] + 2 * pos[1] + pos[2]


def _exchange_phases(items, src, dst, sems):
    send_sems, recv_sems, local_sems = sems
    me = _position()
    sib = _peer(me, _SIBLING)

    def remote(s_ref, d_ref, pi, n, to):
        return pltpu.make_async_remote_copy(
            src_ref=s_ref, dst_ref=d_ref, send_sem=send_sems.at[pi, n], recv_sem=recv_sems.at[pi, n],
            device_id=to, device_id_type=MESH_ID)

    def parts(n):
        it = items[n]
        r = src[n].shape[1] if it[0] == "gather" else src[n].shape[0] // NDEV
        block = lambda ref, pos: ref.at[pl.ds(_index(pos) * r, r), :]
        if it[0] == "gather":
            own = src[n].at[it[2]]
            local = pltpu.make_async_copy(own, block(dst[n], me), local_sems.at[n])
            sends = [remote(own, block(dst[n], me), pi, n, _peer(me, m)) for pi, m in enumerate([_SIBLING] + _CHIPS)]
            hops = [(remote(block(dst[n], _peer(me, m)), block(dst[n], _peer(me, m)), 1 + j, n, _peer(me, m)),
                     remote(block(dst[n], _peer(me, m)), block(dst[n], _peer(me, m)), 4 + j, n, sib))
                    for j, m in enumerate(_CHIPS)]
            lands = [remote(own, block(dst[n], sib), 0, n, sib)]
            lands += [remote(own, block(dst[n], _peer(sib, m)), 4 + j, n, sib) for j, m in enumerate(_CHIPS)]
        else:
            local = pltpu.make_async_copy(block(src[n], me), dst[n].at[_index(me)], local_sems.at[n])
            sends = [remote(block(src[n], _peer(me, m)), dst[n].at[_index(me)], pi, n, _peer(me, m))
                     for pi, m in enumerate(_MASKS)]
            hops = []
            lands = [remote(block(src[n], me), dst[n].at[_index(_peer(me, m))], pi, n, _peer(me, m))
                     for pi, m in enumerate(_MASKS)]
        return local, sends, hops, lands

    def start():
        for n in range(len(items)):
            local, sends, _, _ = parts(n)
            local.start()
            for cp in sends:
                cp.start()

    def relay():
        for n in range(len(items)):
            for arrived, onward in parts(n)[2]:
                arrived.wait_recv()
                onward.start()

    def finish():
        for n in range(len(items)):
            local, sends, hops, lands = parts(n)
            for cp in lands:
                cp.wait_recv()
            for cp in sends + [onward for _, onward in hops]:
                cp.wait_send()
            local.wait()

    return start, relay, finish


def _hosted_call(body, xchg, *, name, grid, in_specs, out_specs, out_shape, args, scratch_shapes=(), relay_at=0.8):
    if not xchg:
        outs = pl.pallas_call(
            body, name=name, grid=grid, in_specs=list(in_specs), out_specs=list(out_specs),
            out_shape=list(out_shape), scratch_shapes=list(scratch_shapes), compiler_params=_cp())(*args)
        return outs, []
    items = [it for it, _ in xchg]
    n_in, n_out, n_scr, nit = len(args), len(out_shape), len(scratch_shapes), len(items)
    hbm = pl.BlockSpec(memory_space=pl.ANY)
    steps = int(np.prod(grid))
    relay_step = min(int(relay_at * steps), steps - 1)

    def dst_shape(it, a):
        if it[0] == "gather":
            return SDS((NDEV * a.shape[1], a.shape[2]), a.dtype)
        return SDS((NDEV, a.shape[0] // NDEV, a.shape[1]), a.dtype)

    def wrapped(*refs):
        ins = refs[:n_in]
        src = refs[n_in:n_in + nit]
        outs = refs[n_in + nit:n_in + nit + n_out]
        dst = refs[n_in + nit + n_out:n_in + 2 * nit + n_out]
        scratch = refs[n_in + 2 * nit + n_out:n_in + 2 * nit + n_out + n_scr]
        start, relay, finish = _exchange_phases(items, src, dst, refs[n_in + 2 * nit + n_out + n_scr:])
        step = 0
        for d, g in enumerate(grid):
            step = step * g + pl.program_id(d)
        pl.when(step == 0)(start)
        body(*ins, *outs, *scratch)
        pl.when(step == relay_step)(relay)
        pl.when(step == steps - 1)(finish)

    npeer = len(_MASKS)
    res = pl.pallas_call(
        wrapped, name=name, grid=grid,
        in_specs=list(in_specs) + [hbm] * nit,
        out_specs=list(out_specs) + [hbm] * nit,
        out_shape=list(out_shape) + [dst_shape(it, a) for it, a in xchg],
        scratch_shapes=list(scratch_shapes) + [
            pltpu.SemaphoreType.DMA((npeer, nit)), pltpu.SemaphoreType.DMA((npeer, nit)), pltpu.SemaphoreType.DMA((nit,))],
        compiler_params=_cp(),
    )(*args, *[a for _, a in xchg])
    return list(res[:n_out]), list(res[n_out:])


def _sum_slots(slots, xchg=None):
    _, r, _ = slots[0].shape
    n = len(slots) // L
    rt = 64

    def body(*refs):
        for k in range(n):
            for l in range(L):
                src = refs[k * L + l]
                acc = src[0].astype(F32)
                for d in range(1, NDEV):
                    acc = acc + src[d].astype(F32)
                refs[n * L + k][l] = acc

    return _hosted_call(
        body, xchg, name=f"sum_slots_r{r}" + ("_x" if xchg else ""), grid=(r // rt,),
        in_specs=[pl.BlockSpec((NDEV, rt, D), lambda i: (0, i, 0))] * (n * L),
        out_specs=[pl.BlockSpec((L, rt, D), lambda i: (0, i, 0))] * n,
        out_shape=[SDS((L, r, D), F32)] * n, args=list(slots))


def _sum_small(slots):
    rows = slots.shape[0] // NDEV

    def body(in_ref, o_ref):
        acc = in_ref[pl.ds(0, rows), :]
        for d in range(1, NDEV):
            acc = acc + in_ref[pl.ds(d * rows, rows), :]
        o_ref[...] = acc

    vm = pl.BlockSpec(memory_space=pltpu.VMEM)
    return pl.pallas_call(
        body, name="sum_small", in_specs=[vm], out_specs=vm, out_shape=SDS((rows, LANE), F32),
        compiler_params=_cp())(slots)


def _adamw_update(w_ref, g_ref, m_ref, v_ref, d_ref, nm_ref, nv_ref):
    gv = g_ref[...]
    mn = B1 * m_ref[...] + (1.0 - B1) * gv
    vn = B2 * v_ref[...] + (1.0 - B2) * jnp.square(gv)
    nm_ref[...] = mn
    nv_ref[...] = vn
    m_hat = mn / (1.0 - B1 ** STEP)
    v_hat = vn / (1.0 - B2 ** STEP)
    d_ref[...] = -LR * (m_hat / (jnp.sqrt(v_hat) + AEPS) + WD * w_ref[...])


def _adamw_small(ws, gs, ms, vs):
    n = len(ws)

    def body(*refs):
        for i in range(n):
            _adamw_update(*[refs[j * n + i] for j in range(7)])

    vm = pl.BlockSpec(memory_space=pltpu.VMEM)
    res = pl.pallas_call(
        body, name="adamw_small", in_specs=[vm] * (4 * n), out_specs=[vm] * (3 * n),
        out_shape=[SDS(w.shape, F32) for _ in range(3) for w in ws],
        compiler_params=_cp(),
    )(*ws, *gs, *ms, *vs)
    return res[:n], res[n:2 * n], res[2 * n:]


def _adamw(w, g, m, v, xchg=None):
    rows, cols = w.shape
    t = rows
    for cand in (512, 256, 128, 64, 32, 16, 8):
        if rows % cand == 0:
            t = cand
            break

    def body(*refs):
        _adamw_update(*refs)

    blk = pl.BlockSpec((t, cols), lambda i: (i, 0))
    return _hosted_call(
        body, xchg, name=f"adamw_{rows}x{cols}", grid=(rows // t,),
        in_specs=[blk] * 4, out_specs=[blk] * 3, out_shape=[SDS((rows, cols), F32)] * 3, args=[w, g, m, v])


_DIST0 = BAND + TQ - 1
_N_FAR = _DIST0 - REL_CLIP + 1
_N_NEAR = NG - _N_FAR - (2 * REL_CLIP - 1)


def _bias_vector(rel_bias):
    far = jnp.broadcast_to(rel_bias[..., -1:], (L, NH, _N_FAR))
    near = jnp.broadcast_to(rel_bias[..., :1], (L, NH, _N_NEAR))
    return jnp.concatenate([far, lax.rev(rel_bias[..., 1:-1], (2,)), near], axis=2)[:, :, None, :]


def _bias_vector_grad(dgr):
    first = jnp.sum(dgr[..., :_N_NEAR], axis=-1, keepdims=True)
    last = jnp.sum(dgr[..., NG - _N_FAR:], axis=-1, keepdims=True)
    return jnp.concatenate([first, dgr[..., _N_NEAR:NG - _N_FAR], last], axis=-1)


def _pool_blockdiag(pool_w):
    eye = jnp.eye(2, dtype=F32)
    pw = pool_w.reshape(L, 2, 2, HD, HD)
    return jnp.einsum("ljaik,ab->ljaibk", pw, eye).reshape(L, 2, LANE, LANE)


def _pool_blockdiag_grad(dwbd):
    d = dwbd.reshape(L, 2, 2, HD, 2, HD)
    return jnp.stack([d[:, :, 0, :, 0, :], d[:, :, 1, :, 1, :]], axis=2).reshape(L, 4, HD, HD)


def _pack(arrays, rows):
    flat = jnp.concatenate([a.reshape(-1).astype(F32) for a in arrays])
    return jnp.pad(flat, (0, rows * LANE - flat.shape[0])).reshape(rows, LANE)


def _unpack(packed, shapes):
    flat = packed.reshape(-1)
    out, o = [], 0
    for shp in shapes:
        n = int(np.prod(shp))
        out.append(flat[o:o + n].reshape(shp))
        o += n
    return out


def _rows_for(shapes):
    n = sum(int(np.prod(s)) for s in shapes)
    return -(-n // (8 * LANE)) * 8


def _grads(x, target, small_w, shards):
    g1, qg, kg, rb, cw_shard, pw, ps, g2 = small_w
    g1 = g1.reshape(L, 1, D)
    g2 = g2.reshape(L, 1, D)
    qg2 = jnp.tile(qg, (1, 2)).reshape(L, 1, LANE)
    kg2 = jnp.tile(kg, (1, 2)).reshape(L, 1, LANE)
    ps3 = ps.reshape(L, 1, PWD)
    wbd = _pool_blockdiag(pw).astype(BF16)

    def gather(*kl):
        return [(("gather", k, l), shards[k]) for k, l in kl if l < L]

    full = {}

    def arrived(got, *kl):
        full.update(zip([x for x in kl if x[1] < L], got))

    bias, got = _bias_tables(_bias_vector(rb), gather((0, 0)) + [(("gather", "conv_w", 0), cw_shard)])
    arrived(got[:1], (0, 0))
    cshard = CW // NDEV
    cw_all = got[1].reshape(NDEV, -1)
    cw = jnp.concatenate([cw_all[d, :L * 3 * cshard].reshape(L, 3, cshard) for d in range(NDEV)], axis=2)
    saved = []
    h = x
    for l in range(L):
        kl = ((1, 0),) if l == 0 else ()
        (p, h_b, qkv), got = _in_proj(h, g1, full[0, l], qg2, kg2, l, gather(*kl))
        arrived(got, *kl)
        kl = ((2, 0), (3, 0)) if l == 0 else ((1, l), (3, l))
        (mix, lse, o32), got = _attn_fwd(qkv, bias, l, gather(*kl), relay_at=0.95 if l == 0 else 0.8)
        arrived(got, *kl)
        mix = _conv_pool_fwd(p, mix, cw, wbd, ps3, l)
        kl = ((0, l + 1), (2, l + 1))
        (xm, a, *out), got = _mlp_fwd(h, mix, full[1, l], g2, full[2, l], full[3, l], l, gather(*kl),
                                      target if l == L - 1 else None)
        arrived(got, *kl)
        saved.append((h, h_b, p, qkv, mix, lse, o32, xm, a))
        h = out[0]
    dx, sq = out

    grads = {}
    slots = {}

    def scatter(*kl):
        return [(("scatter", k), grads[k, l]) for k, l in kl if l < L]

    def left(got, *kl):
        slots.update(zip([x for x in kl if x[1] < L], got))

    per_layer = [None] * L
    for l in reversed(range(L)):
        x_in, h_b, p, qkv, mix, lse, o32, xm, a = saved[l]
        (dxm, dmix, f_b, da_b, h2_b, dxo_b, dxm_b, dg2), got = _mlp_bwd(
            dx, a, xm, g2, full[2, l], full[3, l], full[1, l], l, scatter((3, l + 1)))
        left(got, (3, l + 1))
        grads[1, l], _ = _wgrad(mix, dxm_b, f"w_out_l{l}")
        kl = ((1, 0),) if l == 0 else ()
        grads[2, l], got = _wgrad(da_b, h2_b, f"w_mlp1_l{l}", scatter(*kl))
        left(got, *kl)
        kl = ((2, 0),) if l == 0 else ()
        grads[3, l], got = _wgrad(f_b, dxo_b, f"w_mlp2_l{l}", scatter(*kl))
        left(got, *kl)
        kl = ((3, 0), (0, 1)) if l == 0 else ((2, l), (1, l))
        (dq, dk, dv, dbias, dqg, dkg), got = _attn_bwd(qkv, p, lse, o32, dmix, qg2, kg2, bias, l, scatter(*kl))
        left(got, *kl)
        dgb, dgc, dhin, du, dcw, dwbd, dps = _conv_pool_bwd(p, dmix, cw, wbd, ps3, l)
        (dx, dp_b, dg1), got = _in_proj_bwd(
            [dq, dk, dv, dgb, dgc, dhin, du], x_in, dxm, g1, full[0, l], l, scatter((0, l + 1)) if l else None)
        left(got, (0, l + 1))
        grads[0, l], _ = _wgrad(dp_b, h_b, f"w_in_l{l}")
        per_layer[l] = (dg1, dg2, dqg, dkg, _bias_tables_grad(dbias, l), dcw, dwbd, dps)
    (g_w1_t, g_w2), got = _sum_slots([slots[k, l] for k in (2, 3) for l in range(L)], scatter((0, 0)))
    left(got, (0, 0))
    sums = [_sum_slots([slots[k, l] for l in range(L)])[0][0] for k in (0, 1)] + [g_w1_t, g_w2]

    st = [jnp.stack([per_layer[l][k] for l in range(L)]) for k in range(8)]
    small = dict(
        g1=st[0].reshape(L, D), g2=st[1].reshape(L, D),
        qg=st[2].reshape(L, NH, HD).sum(1), kg=st[3].reshape(L, NH, HD).sum(1),
        rb=_bias_vector_grad(st[4].reshape(L, NH, NG)), cw=st[5], pw=_pool_blockdiag_grad(st[6]),
        ps=st[7].reshape(L, PWD))
    return sq, dx, sums, small


def kernel(x, norm1_g, w_in, q_norm_g, k_norm_g, rel_bias, conv_w, pool_w, pool_scale, w_out, norm2_g, w_mlp1, w_mlp2, loss_target, m_norm1_g, m_w_in, m_q_norm_g, m_k_norm_g, m_rel_bias, m_conv_w, m_pool_w, m_pool_scale, m_w_out, m_norm2_g, m_w_mlp1, m_w_mlp2, v_norm1_g, v_w_in, v_q_norm_g, v_k_norm_g, v_rel_bias, v_conv_w, v_pool_w, v_pool_scale, v_w_out, v_norm2_g, v_w_mlp1, v_w_mlp2):
    me = _index(_position())
    cshard = CW // NDEV

    shards = [jnp.swapaxes(w_in, 1, 2).astype(BF16), w_out.astype(BF16),
              jnp.swapaxes(w_mlp1, 1, 2).astype(BF16), w_mlp2.astype(BF16)]
    small_w = (norm1_g, q_norm_g, k_norm_g, rel_bias, _pack([conv_w], 8)[None], pool_w, pool_scale, norm2_g)
    sq, grad_x, (g_win_t, g_wout, g_w1_t, g_w2), small = _grads(x[0], loss_target[0], small_w, shards)
    g_w_in = jnp.swapaxes(g_win_t, 1, 2)
    g_w_mlp1 = jnp.swapaxes(g_w1_t, 1, 2)

    names = ("g1", "qg", "kg", "rb", "cw", "pw", "ps", "g2")
    gshapes = [(L, D), (L, HD), (L, HD), (L, NH, 2 * REL_CLIP + 1), (L, 3, CW), (L, 4, HD, HD), (L, PWD), (L, D)]
    garrs = [small[n] for n in names]
    rows = _rows_for(gshapes + [(1,)])
    packed = _pack(garrs + [sq[0, :1]], rows)[None]

    def big(w, g, m, v, xchg=None):
        shp = w.shape
        r = lambda a: a.reshape(-1, shp[-1])
        outs, got = _adamw(r(w), r(g), r(m), r(v), xchg)
        return [o.reshape(shp) for o in outs], got

    up_1, got = big(w_mlp1, g_w_mlp1, m_w_mlp1, v_w_mlp1, [(("gather", "small", 0), packed)])
    total = _sum_small(got[0])
    g_g1, g_qg, g_kg, g_rb, g_cw_full, g_pw, g_ps, g_g2, sq_sum = _unpack(total, gshapes + [(1,)])
    loss = (0.5 / D) * sq_sum[0]
    g_cw = lax.dynamic_slice_in_dim(g_cw_full, me * cshard, cshard, axis=2)
    up_in = big(w_in, g_w_in, m_w_in, v_w_in)[0]
    up_out = big(w_out, g_wout, m_w_out, v_w_out)[0]
    up_2 = big(w_mlp2, g_w2, m_w_mlp2, v_w_mlp2)[0]

    sw = [norm1_g, q_norm_g, k_norm_g, rel_bias, conv_w, pool_w, pool_scale, norm2_g]
    sg = [g_g1, g_qg, g_kg, g_rb, g_cw, g_pw, g_ps, g_g2]
    sm = [m_norm1_g, m_q_norm_g, m_k_norm_g, m_rel_bias, m_conv_w, m_pool_w, m_pool_scale, m_norm2_g]
    sv = [v_norm1_g, v_q_norm_g, v_k_norm_g, v_rel_bias, v_conv_w, v_pool_w, v_pool_scale, v_norm2_g]
    s_delta, s_m, s_v = _adamw_small(sw, sg, sm, sv)

    def order(small_list, in_, out_, m1, m2):
        g1_, qg_, kg_, rb_, cw_, pw_, ps_, g2_ = small_list
        return [g1_, in_, qg_, kg_, rb_, cw_, pw_, ps_, out_, g2_, m1, m2]

    grads = order(sg, g_w_in, g_wout, g_w_mlp1, g_w2)
    deltas = order(s_delta, up_in[0], up_out[0], up_1[0], up_2[0])
    new_m = order(s_m, up_in[1], up_out[1], up_1[1], up_2[1])
    new_v = order(s_v, up_in[2], up_out[2], up_1[2], up_2[2])
    return (loss, grad_x[None], *grads, *deltas, *new_m, *new_v)
```

```python
import numpy as np
import jax
import jax.numpy as jnp
from jax import lax
from jax.experimental import pallas as pl
from jax.experimental.pallas import tpu as pltpu

F32 = jnp.float32
BF16 = jnp.bfloat16
SDS = jax.ShapeDtypeStruct
MESH_ID = pl.DeviceIdType.MESH

D = 1024
L = 4
CHUNK = 64
N_PREV = 8
HD = 64
NH = 8
AW = 512
CW = 256
PWD = 256
DIN = 2560
DFF = 4096
EPS = 1e-6
NEG = -1e30
REL_CLIP = 128
POOL_WINDOWS = (2, 4, 8, 16)
LR, B1, B2, AEPS, WD, STEP = 0.001, 0.9, 0.999, 1e-08, 0.01, 10

NDEV = 8
LANE = 128
BAND = N_PREV * CHUNK
TQ = 256
WIN = TQ + BAND
NVAR = BAND // TQ + 1
NTOE = -(-(WIN + TQ - 1) // LANE) * LANE
NG = (NVAR - 1) * TQ + NTOE
PAD = 16
RB_NORM = 64
RB_SOFT = 16
VMEM_LIMIT = 56 * 1024 * 1024
SHARD_ROWS = (DIN // NDEV, D // NDEV, DFF // NDEV, DFF // NDEV)

assert 2 * HD == LANE and NH * HD == AW and POOL_WINDOWS == (2, 4, 8, 16)
assert TQ % CHUNK == 0 and BAND % TQ == 0 and max(POOL_WINDOWS) <= PAD and all(r % 16 == 0 for r in SHARD_ROWS)


def _cp(**kw):
    return pltpu.CompilerParams(vmem_limit_bytes=VMEM_LIMIT, **kw)


def _nn(a, b):
    return jnp.dot(a, b, preferred_element_type=F32)


def _nt(a, b):
    return lax.dot_general(a, b, (((1,), (1,)), ((), ())), preferred_element_type=F32)


def _tn(a, b):
    return lax.dot_general(a, b, (((0,), (0,)), ((), ())), preferred_element_type=F32)


def _const(shape):
    n = len(shape)
    return pl.BlockSpec(shape, lambda *_: (0,) * n, pipeline_mode=pl.Buffered(1))


def _layer(shape, l):
    n = len(shape)
    return pl.BlockSpec((None,) + tuple(shape), lambda *_: (l,) + (0,) * n, pipeline_mode=pl.Buffered(1))


def _lo_mask():
    return lax.broadcasted_iota(jnp.int32, (1, LANE), 1) < HD


def _half_sum(t, lo):
    s_lo = jnp.sum(jnp.where(lo, t, 0.0), axis=-1, keepdims=True)
    s_hi = jnp.sum(jnp.where(lo, 0.0, t), axis=-1, keepdims=True)
    return jnp.where(lo, s_lo, s_hi)


def _half_sum_mxu(t, lo):
    del lo
    ii = lax.broadcasted_iota(jnp.int32, (LANE, LANE), 0) // HD
    jj = lax.broadcasted_iota(jnp.int32, (LANE, LANE), 1) // HD
    ones = jnp.where(ii == jj, 1.0, 0.0).astype(BF16)
    hi = t.astype(BF16)
    return _nn(hi, ones) + _nn((t - hi.astype(F32)).astype(BF16), ones)


def _head_norm(x, lo, half_sum=_half_sum):
    r = lax.rsqrt(half_sum(x * x, lo) * (1.0 / HD) + EPS)
    return x * r, r


def _head_norm_bwd(dy, xn, r, g, lo, half_sum=_half_sum):
    dxn = dy * g
    mu = half_sum(dxn * xn, lo) * (1.0 / HD)
    return r * (dxn - xn * mu), dy * xn


def _rms_bwd(dy, x, g):
    r = lax.rsqrt(jnp.mean(x * x, axis=-1, keepdims=True) + EPS)
    xn = x * r
    dxn = dy * g
    mu = jnp.mean(dxn * xn, axis=-1, keepdims=True)
    return r * (dxn - xn * mu), dy * xn


def _in_proj(x, g1, win_t, qg2, kg2, l, xchg=None):
    s = x.shape[0]
    t = min(512, s)
    nblk = AW // LANE

    def body(x_ref, g_ref, w_ref, qg_ref, kg_ref, p_ref, h_ref, qkv_ref):
        xv = x_ref[...]
        r = lax.rsqrt(jnp.mean(xv * xv, axis=-1, keepdims=True) + EPS)
        h = (xv * r * g_ref[...]).astype(BF16)
        h_ref[...] = h
        p_ref[...] = _nt(h, w_ref[...])
        lo = _lo_mask()
        gains = (qg_ref[...] * (HD ** -0.5), kg_ref[...])
        for r0 in range(0, t, RB_NORM):
            rows = pl.ds(r0, RB_NORM)
            for c in range(3 * nblk):
                cols = pl.ds(c * LANE, LANE)
                v = p_ref[rows, cols]
                if c < 2 * nblk:
                    v = _head_norm(v, lo)[0] * gains[c // nblk]
                qkv_ref[rows, cols] = v.astype(BF16)

    row = lambda c: pl.BlockSpec((t, c), lambda i: (i, 0))
    return _hosted_call(
        body, xchg, name="in_proj", grid=(s // t,),
        in_specs=[row(D), _layer((1, D), l), _const((DIN, D)), _layer((1, LANE), l), _layer((1, LANE), l)],
        out_specs=[row(DIN), row(D), row(3 * AW)],
        out_shape=[SDS((s, DIN), F32), SDS((s, D), BF16), SDS((s, 3 * AW), BF16)], args=[x, g1, win_t, qg2, kg2])


def _bias_spec():
    return pl.BlockSpec((None, 2, TQ, WIN), lambda j, i: (jnp.maximum(NVAR - 1 - i, 0), j, 0, 0))


def _bias_layer_spec(l):
    return pl.BlockSpec((None, None, 2, TQ, WIN), lambda j, i: (l, jnp.maximum(NVAR - 1 - i, 0), j, 0, 0))


def _attn_fwd(qkv, bias, l, xchg=None, relay_at=0.8):
    s = qkv.shape[0]
    nq = s // TQ

    def body(q_ref, k_ref, v_ref, b_ref, o_ref, lse_ref, o32_ref, s_ref, p_ref, m_ref, den_ref, o0_ref):
        i = pl.program_id(1)
        ks = pl.multiple_of(jnp.maximum(i * TQ - BAND, 0), TQ)
        lo = _lo_mask()
        q = q_ref[...]
        kwin = k_ref[pl.ds(ks, WIN), :]
        vwin = v_ref[pl.ds(ks, WIN), :]
        for half in range(2):
            m_ = lo if half == 0 else jnp.logical_not(lo)
            s_ref[half] = _nt(jnp.where(m_, q, jnp.zeros_like(q)), kwin)
            for r0 in range(0, TQ, RB_SOFT):
                rows = pl.ds(r0, RB_SOFT)
                mx = jnp.max(s_ref[half, rows, :] + b_ref[half, rows, :], axis=-1, keepdims=True)
                m_ref[rows, :] = jnp.broadcast_to(mx, (RB_SOFT, LANE))
            for r0 in range(0, TQ, RB_SOFT):
                rows = pl.ds(r0, RB_SOFT)
                mx = m_ref[rows, 0:1]
                e = jnp.exp(s_ref[half, rows, :] + b_ref[half, rows, :] - mx)
                p_ref[half, rows, :] = e.astype(BF16)
                den = jnp.sum(e, axis=-1, keepdims=True)
                den_ref[rows, :] = jnp.broadcast_to(den, (RB_SOFT, LANE))
                lse = mx + jnp.log(den)
                if half == 0:
                    lse_ref[rows, :] = jnp.broadcast_to(lse, (RB_SOFT, LANE))
                else:
                    lse_ref[rows, :] = jnp.where(lo, lse_ref[rows, :], lse)
            o = _nn(p_ref[half], vwin) * (1.0 / den_ref[...])
            if half == 0:
                o0_ref[...] = o
            else:
                o = jnp.where(lo, o0_ref[...], o)
                o32_ref[...] = o
                o_ref[...] = o.astype(BF16)

    tile = pl.BlockSpec((TQ, LANE), lambda j, i: (i, j))
    stat = pltpu.VMEM((TQ, LANE), F32)
    return _hosted_call(
        body, xchg, name="attn_fwd", grid=(NH // 2, nq),
        in_specs=[
            tile,
            pl.BlockSpec((s, LANE), lambda j, i: (0, AW // LANE + j)),
            pl.BlockSpec((s, LANE), lambda j, i: (0, 2 * AW // LANE + j)),
            _bias_layer_spec(l),
        ],
        out_specs=[tile, tile, tile],
        out_shape=[SDS((s, D), BF16), SDS((s, AW), F32), SDS((s, AW), F32)], args=[qkv, qkv, qkv, bias],
        scratch_shapes=[pltpu.VMEM((2, TQ, WIN), F32), pltpu.VMEM((2, TQ, WIN), BF16), stat, stat, stat],
        relay_at=relay_at)


_C0 = 3 * AW // LANE


def _cp_in_specs(s, l):
    blk = lambda f: pl.BlockSpec((s, LANE), f)
    return [
        blk(lambda g: (0, _C0 + jnp.minimum(g, 1))),
        blk(lambda g: (0, _C0 + 2 + jnp.minimum(g, 1))),
        blk(lambda g: (0, _C0 + 4 + jnp.minimum(g, 1))),
        blk(lambda g: (0, _C0 + 6 + jnp.maximum(g - 2, 0))),
        pl.BlockSpec((None, 3, LANE), lambda g: (l, 0, jnp.minimum(g, 1))),
        pl.BlockSpec((None, None, LANE, LANE), lambda g: (l, jnp.maximum(g - 2, 0), 0, 0)),
        pl.BlockSpec((None, 1, LANE), lambda g: (l, 0, jnp.maximum(g - 2, 0))),
    ]


def _pool_window_sums(u_ref, buf_a, buf_b, jj, s, rt):
    nrt = s // rt
    for t in range(nrt):
        buf_a[pl.ds(PAD + t * rt, rt), :] = u_ref[pl.ds(t * rt, rt), :]

    def stage(src, dst, sh):
        for t in range(nrt):
            r0 = PAD + t * rt
            dst[pl.ds(r0, rt), :] = src[pl.ds(r0, rt), :] + src[pl.ds(r0 - sh, rt), :]

    stage(buf_a, buf_b, 1)
    stage(buf_b, buf_a, 2)

    @pl.when(jj == 1)
    def _():
        stage(buf_a, buf_b, 4)
        stage(buf_b, buf_a, 8)


def _pool_counts(jj, lo, r0, rt):
    w0, w1, w2, w3 = [float(w) for w in POOL_WINDOWS]
    w = jnp.where(lo, jnp.where(jj == 0, w0, w2), jnp.where(jj == 0, w1, w3))
    pos1 = (lax.broadcasted_iota(jnp.int32, (rt, LANE), 0) + (r0 + 1)).astype(F32)
    return jnp.minimum(pos1, w)


def _conv_pool_fwd(p, mix, conv_w, wbd, pscale, l):
    s = p.shape[0]
    rt = min(256, s)
    nrt = s // rt

    def body(gb_ref, gc_ref, hin_ref, u_ref, cw_ref, wbd_ref, ps_ref, mix_in, o_ref, buf_a, buf_b):
        del mix_in
        g = pl.program_id(0)
        zpad = jnp.zeros((PAD, LANE), F32)
        buf_a[pl.ds(0, PAD), :] = zpad
        buf_b[pl.ds(0, PAD), :] = zpad

        @pl.when(g < 2)
        def _conv():
            for t in range(nrt):
                buf_a[pl.ds(PAD + t * rt, rt), :] = gc_ref[pl.ds(t * rt, rt), :] * hin_ref[pl.ds(t * rt, rt), :]
            w0, w1, w2 = cw_ref[0:1, :], cw_ref[1:2, :], cw_ref[2:3, :]
            for t in range(nrt):
                r0 = PAD + t * rt
                y = w0 * buf_a[pl.ds(r0 - 2, rt), :] + w1 * buf_a[pl.ds(r0 - 1, rt), :] + w2 * buf_a[pl.ds(r0, rt), :]
                o_ref[pl.ds(t * rt, rt), :] = (gb_ref[pl.ds(t * rt, rt), :] * y).astype(BF16)

        @pl.when(g >= 2)
        def _pool():
            jj = g - 2
            lo = _lo_mask()
            _pool_window_sums(u_ref, buf_a, buf_b, jj, s, rt)
            wb = wbd_ref[...]
            for t in range(nrt):
                r0 = PAD + t * rt
                wsum = jnp.where(lo, buf_b[pl.ds(r0, rt), :], buf_a[pl.ds(r0, rt), :])
                m = wsum / _pool_counts(jj, lo, t * rt, rt) - u_ref[pl.ds(t * rt, rt), :]
                o_ref[pl.ds(t * rt, rt), :] = (_nn(m.astype(BF16), wb) * ps_ref[...]).astype(BF16)

    return pl.pallas_call(
        body, name="conv_pool_fwd", grid=(4,),
        in_specs=_cp_in_specs(s, l) + [pl.BlockSpec(memory_space=pl.ANY)],
        out_specs=pl.BlockSpec((s, LANE), lambda g: (0, AW // LANE + g)),
        out_shape=SDS((s, D), BF16),
        scratch_shapes=[pltpu.VMEM((s + 2 * PAD, LANE), F32), pltpu.VMEM((s + 2 * PAD, LANE), F32)],
        input_output_aliases={7: 0},
        compiler_params=_cp(),
    )(p, p, p, p, conv_w, wbd, pscale, mix)


def _mlp_fwd(x, mix, wout, g2, w1_t, w2, l, xchg=None, target=None):
    s = x.shape[0]
    t = min(256, s)

    def body(*refs):
        x_ref, mix_ref, wo_ref, g_ref, w1_ref, w2_ref = refs[:6]
        xm_ref, a_ref, xo_ref = refs[-4:-1] if target is not None else refs[-3:]
        xm = x_ref[...] + _nn(mix_ref[...], wo_ref[...])
        xm_ref[...] = xm
        r = lax.rsqrt(jnp.mean(xm * xm, axis=-1, keepdims=True) + EPS)
        h2 = (xm * r * g_ref[...]).astype(BF16)
        a = _nt(h2, w1_ref[...])
        a_ref[...] = a.astype(BF16)
        f = jnp.square(jnp.maximum(a, 0.0)).astype(BF16)
        xo = xm + _nn(f, w2_ref[...])
        if target is None:
            xo_ref[...] = xo
        else:
            acc_ref = refs[-1]

            @pl.when(pl.program_id(0) == 0)
            def _():
                acc_ref[...] = jnp.zeros_like(acc_ref)
            e = xo - refs[6][...]
            xo_ref[...] = e * (1.0 / D)
            acc_ref[...] += jnp.sum(e * e)

    row = lambda c: pl.BlockSpec((t, c), lambda i: (i, 0))
    last = target is not None
    return _hosted_call(
        body, xchg, name="mlp_fwd_loss" if last else "mlp_fwd", grid=(s // t,),
        in_specs=[row(D), row(D), _const((D, D)), _layer((1, D), l), _const((DFF, D)), _const((DFF, D))] + [row(D)] * last,
        out_specs=[row(D), row(DFF), row(D)] + [pl.BlockSpec((8, LANE), lambda i: (0, 0))] * last,
        out_shape=[SDS((s, D), F32), SDS((s, DFF), BF16), SDS((s, D), F32)] + [SDS((8, LANE), F32)] * last,
        args=[x, mix, wout, g2, w1_t, w2] + [target] * last)


def _mlp_bwd(dxo, a, xm, g2, w1_t, w2, wout, l, xchg=None):
    s = dxo.shape[0]
    t = min(256, s)

    def body(dxo_ref, a_ref, xm_ref, g_ref, w1_ref, w2_ref, wo_ref,
             dxm_ref, dmix_ref, f_ref, da_ref, h2_ref, dxob_ref, dxmb_ref, dg_ref):
        @pl.when(pl.program_id(0) == 0)
        def _():
            dg_ref[...] = jnp.zeros_like(dg_ref)
        dxo = dxo_ref[...]
        dxob = dxo.astype(BF16)
        dxob_ref[...] = dxob
        ra = jnp.maximum(a_ref[...].astype(F32), 0.0)
        f_ref[...] = jnp.square(ra).astype(BF16)
        dab = (_nt(dxob, w2_ref[...]) * (2.0 * ra)).astype(BF16)
        da_ref[...] = dab
        dh2 = _nn(dab, w1_ref[...])
        xm = xm_ref[...]
        g = g_ref[...]
        r = lax.rsqrt(jnp.mean(xm * xm, axis=-1, keepdims=True) + EPS)
        h2_ref[...] = (xm * r * g).astype(BF16)
        dx_n, dgr = _rms_bwd(dh2, xm, g)
        dg_ref[...] += jnp.sum(dgr, axis=0, keepdims=True)
        dxm = dxo + dx_n
        dxm_ref[...] = dxm
        dxmb = dxm.astype(BF16)
        dxmb_ref[...] = dxmb
        dmix_ref[...] = _nt(dxmb, wo_ref[...])

    row = lambda c: pl.BlockSpec((t, c), lambda i: (i, 0))
    return _hosted_call(
        body, xchg, name="mlp_bwd", grid=(s // t,),
        in_specs=[row(D), row(DFF), row(D), _layer((1, D), l), _const((DFF, D)), _const((DFF, D)), _const((D, D))],
        out_specs=[row(D), row(D), row(DFF), row(DFF), row(D), row(D), row(D), pl.BlockSpec((1, D), lambda i: (0, 0))],
        out_shape=[SDS((s, D), F32), SDS((s, D), F32), SDS((s, DFF), BF16), SDS((s, DFF), BF16),
                   SDS((s, D), BF16), SDS((s, D), BF16), SDS((s, D), BF16), SDS((1, D), F32)],
        args=[dxo, a, xm, g2, w1_t, w2, wout])


def _attn_bwd(qkv, p, lse, o32, dmix, qg2, kg2, bias, l, xchg=None):
    s = p.shape[0]
    nq = s // TQ
    scale = HD ** -0.5

    def body(qs_ref, kb_ref, vb_ref, q_ref, k_ref, qg_ref, kg_ref, b_ref, lse_ref, o_ref, do_ref,
             dq_ref, dk_ref, dv_ref, db_ref, dqg_ref, dkg_ref,
             dk_acc, dv_acc, s_ref, dp_ref, ds_ref, pb_ref, dqn_ref, dl_ref):
        i = pl.program_id(1)
        kt = jnp.maximum(i - BAND // TQ, 0)
        ks = pl.multiple_of(kt * TQ, TQ)
        lo = _lo_mask()

        @pl.when(i == 0)
        def _():
            dk_acc[...] = jnp.zeros_like(dk_acc)
            dv_acc[...] = jnp.zeros_like(dv_acc)
            dqg_ref[...] = jnp.zeros_like(dqg_ref)
            dkg_ref[...] = jnp.zeros_like(dkg_ref)

        @pl.when(i < NVAR)
        def _():
            db_ref[...] = jnp.zeros_like(db_ref)

        qs = qs_ref[...]
        kwin = kb_ref[pl.ds(ks, WIN), :]
        vwin = vb_ref[pl.ds(ks, WIN), :]
        do = do_ref[...]
        dob = do.astype(BF16)
        dl_ref[...] = _half_sum(do * o_ref[...], lo)
        for half in range(2):
            m_ = lo if half == 0 else jnp.logical_not(lo)
            qa = jnp.where(m_, qs, jnp.zeros_like(qs))
            doa = jnp.where(m_, dob, jnp.zeros_like(dob))
            s_ref[half] = _nt(qa, kwin)
            dp_ref[half] = _nt(doa, vwin)
            for r0 in range(0, TQ, RB_SOFT):
                rows = pl.ds(r0, RB_SOFT)
                lse_h = lse_ref[rows, half * HD:half * HD + 1]
                pm = jnp.exp(s_ref[half, rows, :] + b_ref[half, rows, :] - lse_h)
                ds = pm * (dp_ref[half, rows, :] - dl_ref[rows, half * HD:half * HD + 1])
                db_ref[half, rows, :] += ds
                ds_ref[half, rows, :] = ds.astype(BF16)
                pb_ref[half, rows, :] = pm.astype(BF16)
            dsb = ds_ref[half]
            dq_h = _nn(dsb, kwin)
            if half == 0:
                dqn_ref[...] = dq_h
            else:
                dqn_ref[...] = jnp.where(lo, dqn_ref[...], dq_h)
            dk_t = _tn(qa, dsb)
            dv_t = _tn(doa, pb_ref[half])
            for t in range(WIN // TQ):
                dk_acc[kt + t] += dk_t[:, t * TQ:(t + 1) * TQ]
                dv_acc[kt + t] += dv_t[:, t * TQ:(t + 1) * TQ]
        qg, kg = qg_ref[...], kg_ref[...]
        xq, rq = _head_norm(q_ref[...], lo)
        dq, dqg_rows = _head_norm_bwd(dqn_ref[...] * scale, xq, rq, qg, lo)
        dq_ref[...] = dq.astype(BF16)
        dqg_ref[...] += jnp.sum(dqg_rows, axis=0, keepdims=True)

        @pl.when(i == nq - 1)
        def _():
            dkg = jnp.zeros((1, LANE), F32)
            for t in range(nq):
                rows = pl.ds(t * TQ, TQ)
                xk, rk = _head_norm(k_ref[rows, :], lo, _half_sum_mxu)
                dk, dkg_rows = _head_norm_bwd(dk_acc[t].T, xk, rk, kg, lo, _half_sum_mxu)
                dk_ref[rows, :] = dk.astype(BF16)
                dv_ref[rows, :] = dv_acc[t].T.astype(BF16)
                dkg = dkg + jnp.sum(dkg_rows, axis=0, keepdims=True)
            dkg_ref[...] = dkg

    tile = pl.BlockSpec((TQ, LANE), lambda j, i: (i, j))
    kcol = lambda c0: pl.BlockSpec((s, LANE), lambda j, i: (0, c0 + j))
    gain = pl.BlockSpec((None, 1, LANE), lambda j, i: (j, 0, 0))
    return _hosted_call(
        body, xchg, name="attn_bwd", grid=(NH // 2, nq),
        in_specs=[
            tile, kcol(AW // LANE), kcol(2 * AW // LANE), tile, kcol(AW // LANE),
            _layer((1, LANE), l), _layer((1, LANE), l),
            _bias_layer_spec(l), tile, tile, tile,
        ],
        out_specs=[tile, kcol(0), kcol(0), _bias_spec(), gain, gain],
        out_shape=[SDS((s, AW), BF16), SDS((s, AW), BF16), SDS((s, AW), BF16),
                   SDS((NVAR, NH, TQ, WIN), F32), SDS((NH // 2, 1, LANE), F32), SDS((NH // 2, 1, LANE), F32)],
        scratch_shapes=[pltpu.VMEM((nq, LANE, TQ), F32), pltpu.VMEM((nq, LANE, TQ), F32),
                        pltpu.VMEM((2, TQ, WIN), F32), pltpu.VMEM((2, TQ, WIN), F32),
                        pltpu.VMEM((2, TQ, WIN), BF16), pltpu.VMEM((2, TQ, WIN), BF16),
                        pltpu.VMEM((TQ, LANE), F32), pltpu.VMEM((TQ, LANE), F32)],
        args=[qkv, qkv, qkv, p, p, qg2, kg2, bias, lse, o32, dmix])


def _conv_pool_bwd(p, dmix, conv_w, wbd, pscale, l):
    s = p.shape[0]
    rt = min(256, s)
    nrt = s // rt

    def body(gb_ref, gc_ref, hin_ref, u_ref, cw_ref, wbd_ref, ps_ref, dy_ref,
             dgb_ref, dgc_ref, dhin_ref, du_ref, dcw_ref, dwbd_ref, dps_ref, buf_a, buf_b, buf_c, buf_d):
        g = pl.program_id(0)
        zpad = jnp.zeros((PAD, LANE), F32)
        for buf in (buf_a, buf_b, buf_c):
            buf[pl.ds(0, PAD), :] = zpad
            buf[pl.ds(PAD + s, PAD), :] = zpad

        @pl.when(g < 2)
        def _conv():
            for t in range(nrt):
                rows = pl.ds(t * rt, rt)
                buf_a[pl.ds(PAD + t * rt, rt), :] = gc_ref[rows, :] * hin_ref[rows, :]
                buf_b[pl.ds(PAD + t * rt, rt), :] = dy_ref[rows, :] * gb_ref[rows, :]
            w0, w1, w2 = cw_ref[0:1, :], cw_ref[1:2, :], cw_ref[2:3, :]
            d0 = jnp.zeros((1, LANE), F32)
            d1 = jnp.zeros((1, LANE), F32)
            d2 = jnp.zeros((1, LANE), F32)
            for t in range(nrt):
                rows = pl.ds(t * rt, rt)
                r0 = PAD + t * rt
                z2, z1, z0 = buf_a[pl.ds(r0 - 2, rt), :], buf_a[pl.ds(r0 - 1, rt), :], buf_a[pl.ds(r0, rt), :]
                y = w0 * z2 + w1 * z1 + w2 * z0
                dgb_ref[rows, :] = (dy_ref[rows, :] * y).astype(BF16)
                e0 = buf_b[pl.ds(r0, rt), :]
                d0 = d0 + jnp.sum(e0 * z2, axis=0, keepdims=True)
                d1 = d1 + jnp.sum(e0 * z1, axis=0, keepdims=True)
                d2 = d2 + jnp.sum(e0 * z0, axis=0, keepdims=True)
                dz = w2 * e0 + w1 * buf_b[pl.ds(r0 + 1, rt), :] + w0 * buf_b[pl.ds(r0 + 2, rt), :]
                dgc_ref[rows, :] = (dz * hin_ref[rows, :]).astype(BF16)
                dhin_ref[rows, :] = (dz * gc_ref[rows, :]).astype(BF16)
            dcw_ref[0:1, :] = d0
            dcw_ref[1:2, :] = d1
            dcw_ref[2:3, :] = d2

        @pl.when(g >= 2)
        def _pool():
            jj = g - 2
            lo = _lo_mask()
            _pool_window_sums(u_ref, buf_a, buf_b, jj, s, rt)
            wb = wbd_ref[...]
            ps = ps_ref[...]
            dps = jnp.zeros((1, LANE), F32)
            dwb = jnp.zeros((LANE, LANE), F32)
            for t in range(nrt):
                rows = pl.ds(t * rt, rt)
                r0 = PAD + t * rt
                cnt = _pool_counts(jj, lo, t * rt, rt)
                wsum = jnp.where(lo, buf_b[pl.ds(r0, rt), :], buf_a[pl.ds(r0, rt), :])
                mb = (wsum / cnt - u_ref[rows, :]).astype(BF16)
                dy = dy_ref[rows, :]
                dps = dps + jnp.sum(dy * _nn(mb, wb), axis=0, keepdims=True)
                dmp = (dy * ps).astype(BF16)
                dwb = dwb + _tn(mb, dmp)
                dm = _nt(dmp, wb)
                buf_d[rows, :] = dm
                buf_c[pl.ds(r0, rt), :] = dm / cnt
            dps_ref[...] = dps
            dwbd_ref[...] = dwb

            def stage(src, dst, sh):
                for t in range(nrt):
                    r0 = PAD + t * rt
                    dst[pl.ds(r0, rt), :] = src[pl.ds(r0, rt), :] + src[pl.ds(r0 + sh, rt), :]

            def finish(first, second):
                for t in range(nrt):
                    rows = pl.ds(t * rt, rt)
                    r0 = PAD + t * rt
                    fw = jnp.where(lo, first[pl.ds(r0, rt), :], second[pl.ds(r0, rt), :])
                    du_ref[rows, :] = (fw - buf_d[rows, :]).astype(BF16)

            stage(buf_c, buf_a, 1)
            stage(buf_a, buf_b, 2)

            @pl.when(jj == 0)
            def _():
                finish(buf_a, buf_b)

            @pl.when(jj == 1)
            def _():
                stage(buf_b, buf_c, 4)
                stage(buf_c, buf_a, 8)
                finish(buf_c, buf_a)

    cblk = pl.BlockSpec((s, LANE), lambda g: (0, jnp.minimum(g, 1)))
    pblk = pl.BlockSpec((s, LANE), lambda g: (0, jnp.maximum(g - 2, 0)))
    padded = pltpu.VMEM((s + 2 * PAD, LANE), F32)
    return pl.pallas_call(
        body, name="conv_pool_bwd", grid=(4,),
        in_specs=_cp_in_specs(s, l) + [pl.BlockSpec((s, LANE), lambda g: (0, AW // LANE + g))],
        out_specs=[cblk, cblk, cblk, pblk,
                   pl.BlockSpec((3, LANE), lambda g: (0, jnp.minimum(g, 1))),
                   pl.BlockSpec((None, LANE, LANE), lambda g: (jnp.maximum(g - 2, 0), 0, 0)),
                   pl.BlockSpec((1, LANE), lambda g: (0, jnp.maximum(g - 2, 0)))],
        out_shape=[SDS((s, CW), BF16), SDS((s, CW), BF16), SDS((s, CW), BF16), SDS((s, PWD), BF16),
                   SDS((3, CW), F32), SDS((2, LANE, LANE), F32), SDS((1, PWD), F32)],
        scratch_shapes=[padded, padded, padded, pltpu.VMEM((s, LANE), F32)],
        compiler_params=_cp(),
    )(p, p, p, p, conv_w, wbd, pscale, dmix)


def _in_proj_bwd(parts, x, dxm, g1, win_t, l, xchg=None):
    s = x.shape[0]
    t = min(256, s)
    widths = [a.shape[1] for a in parts]
    offs = [int(o) for o in np.cumsum([0] + widths[:-1])]
    n = len(parts)

    def body(*refs):
        part_refs = refs[:n]
        x_ref, dxm_ref, g_ref, w_ref, dx_ref, dp_ref, dg_ref = refs[n:]

        @pl.when(pl.program_id(0) == 0)
        def _():
            dg_ref[...] = jnp.zeros_like(dg_ref)
        for r, o, w in zip(part_refs, offs, widths):
            dp_ref[:, o:o + w] = r[...]
        dh = _nn(dp_ref[...], w_ref[...])
        dx_n, dgr = _rms_bwd(dh, x_ref[...], g_ref[...])
        dg_ref[...] += jnp.sum(dgr, axis=0, keepdims=True)
        dx_ref[...] = dxm_ref[...] + dx_n

    row = lambda c: pl.BlockSpec((t, c), lambda i: (i, 0))
    return _hosted_call(
        body, xchg, name="in_proj_bwd", grid=(s // t,),
        in_specs=[row(w) for w in widths] + [row(D), row(D), _layer((1, D), l), _const((DIN, D))],
        out_specs=[row(D), row(DIN), pl.BlockSpec((1, D), lambda i: (0, 0))],
        out_shape=[SDS((s, D), F32), SDS((s, DIN), BF16), SDS((1, D), F32)], args=[*parts, x, dxm, g1, win_t])


def _wgrad(a, b, tag, xchg=None):
    s, m = a.shape
    mb = 512

    def body(a_ref, b_ref, o_ref):
        o_ref[...] = _tn(a_ref[...], b_ref[...]).astype(BF16)

    (out,), got = _hosted_call(
        body, xchg, name=f"wgrad_{tag}", grid=(m // mb,),
        in_specs=[pl.BlockSpec((s, mb), lambda mi: (0, mi)), _const((s, D))],
        out_specs=[pl.BlockSpec((mb, D), lambda mi: (mi, 0))],
        out_shape=[SDS((m, D), BF16)], args=[a, b])
    return out, got


def _bias_tables(gvec, xchg=None):
    def body(g_ref, o_ref):
        qc = lax.broadcasted_iota(jnp.int32, (TQ, WIN), 0) // CHUNK
        kc = lax.broadcasted_iota(jnp.int32, (TQ, WIN), 1) // CHUNK
        for var in range(NVAR):
            vec = jnp.broadcast_to(g_ref[:, var * TQ:var * TQ + NTOE], (TQ, NTOE))
            toe = pltpu.roll(vec, NTOE - TQ + 1, 1, stride=1, stride_axis=0)[:, :WIN]
            rel = (BAND - var * TQ) // CHUNK + qc - kc
            o_ref[var] = jnp.where((rel >= 0) & (rel <= N_PREV), toe, NEG)

    (out,), got = _hosted_call(
        body, xchg, name="bias_tables", grid=(L, NH),
        in_specs=[pl.BlockSpec((None, None, 1, NG), lambda l, h: (l, h, 0, 0))],
        out_specs=[pl.BlockSpec((None, NVAR, None, TQ, WIN), lambda l, h: (l, 0, h, 0, 0))],
        out_shape=[SDS((L, NVAR, NH, TQ, WIN), F32)], args=[gvec])
    return out, got


def _bias_tables_grad(dbias, l):
    nb = NTOE // LANE
    wb = WIN // LANE

    def body(d_ref, o_ref):
        ii = lax.broadcasted_iota(jnp.int32, (LANE, LANE), 0)
        jj = lax.broadcasted_iota(jnp.int32, (LANE, LANE), 1)
        flip = jnp.where(ii + jj == LANE - 1, 1.0, 0.0).astype(BF16)
        o_ref[...] = jnp.zeros_like(o_ref)
        for var in range(NVAR):
            blocks = []
            for b in range(nb):
                src = nb - 1 - b
                if src >= wb:
                    blocks.append(jnp.zeros((TQ, LANE), F32))
                    continue
                xv = d_ref[var, :, src * LANE:(src + 1) * LANE]
                hi = xv.astype(BF16)
                lo = (xv - hi.astype(F32)).astype(BF16)
                blocks.append(_nn(hi, flip) + _nn(lo, flip))
            rev = jnp.concatenate(blocks, axis=1)
            skew = pltpu.roll(rev, NTOE - TQ + 1, 1, stride=1, stride_axis=0)
            off = NG - NTOE - var * TQ
            o_ref[:, off:off + NTOE] += jnp.sum(skew, axis=0, keepdims=True)

    return pl.pallas_call(
        body, name=f"bias_tables_grad_l{l}", grid=(NH,),
        in_specs=[pl.BlockSpec((NVAR, None, TQ, WIN), lambda h: (0, h, 0, 0))],
        out_specs=pl.BlockSpec((None, 1, NG), lambda h: (h, 0, 0)),
        out_shape=SDS((NH, 1, NG), F32),
        compiler_params=_cp(),
    )(dbias)


_SIBLING = (0, 0, 1)
_CHIPS = [(1, 0, 0), (0, 1, 0), (1, 1, 0)]
_MASKS = [_SIBLING] + _CHIPS + [(1, 0, 1), (0, 1, 1), (1, 1, 1)]


def _position():
    return lax.axis_index("x"), lax.axis_index("y"), lax.axis_index("c")


def _peer(pos, mask):
    return tuple(1 - a if f else a for a, f in zip(pos, mask))


def _index(pos):
    return 4 * pos[0] + 2 * pos[1] + pos[2]


def _exchange_phases(items, src, dst, sems):
    send_sems, recv_sems, local_sems = sems
    me = _position()
    sib = _peer(me, _SIBLING)

    def remote(s_ref, d_ref, pi, n, to):
        return pltpu.make_async_remote_copy(
            src_ref=s_ref, dst_ref=d_ref, send_sem=send_sems.at[pi, n], recv_sem=recv_sems.at[pi, n],
            device_id=to, device_id_type=MESH_ID)

    def parts(n):
        it = items[n]
        r = src[n].shape[1] if it[0] == "gather" else src[n].shape[0] // NDEV
        block = lambda ref, pos: ref.at[pl.ds(_index(pos) * r, r), :]
        if it[0] == "gather":
            own = src[n].at[it[2]]
            local = pltpu.make_async_copy(own, block(dst[n], me), local_sems.at[n])
            sends = [remote(own, block(dst[n], me), pi, n, _peer(me, m)) for pi, m in enumerate([_SIBLING] + _CHIPS)]
            hops = [(remote(block(dst[n], _peer(me, m)), block(dst[n], _peer(me, m)), 1 + j, n, _peer(me, m)),
                     remote(block(dst[n], _peer(me, m)), block(dst[n], _peer(me, m)), 4 + j, n, sib))
                    for j, m in enumerate(_CHIPS)]
            lands = [remote(own, block(dst[n], sib), 0, n, sib)]
            lands += [remote(own, block(dst[n], _peer(sib, m)), 4 + j, n, sib) for j, m in enumerate(_CHIPS)]
        else:
            local = pltpu.make_async_copy(block(src[n], me), dst[n].at[_index(me)], local_sems.at[n])
            sends = [remote(block(src[n], _peer(me, m)), dst[n].at[_index(me)], pi, n, _peer(me, m))
                     for pi, m in enumerate(_MASKS)]
            hops = []
            lands = [remote(block(src[n], me), dst[n].at[_index(_peer(me, m))], pi, n, _peer(me, m))
                     for pi, m in enumerate(_MASKS)]
        return local, sends, hops, lands

    def start():
        for n in range(len(items)):
            local, sends, _, _ = parts(n)
            local.start()
            for cp in sends:
                cp.start()

    def relay():
        for n in range(len(items)):
            for arrived, onward in parts(n)[2]:
                arrived.wait_recv()
                onward.start()

    def finish():
        for n in range(len(items)):
            local, sends, hops, lands = parts(n)
            for cp in lands:
                cp.wait_recv()
            for cp in sends + [onward for _, onward in hops]:
                cp.wait_send()
            local.wait()

    return start, relay, finish


def _hosted_call(body, xchg, *, name, grid, in_specs, out_specs, out_shape, args, scratch_shapes=(), relay_at=0.8):
    if not xchg:
        outs = pl.pallas_call(
            body, name=name, grid=grid, in_specs=list(in_specs), out_specs=list(out_specs),
            out_shape=list(out_shape), scratch_shapes=list(scratch_shapes), compiler_params=_cp())(*args)
        return outs, []
    items = [it for it, _ in xchg]
    n_in, n_out, n_scr, nit = len(args), len(out_shape), len(scratch_shapes), len(items)
    hbm = pl.BlockSpec(memory_space=pl.ANY)
    steps = int(np.prod(grid))
    relay_step = min(int(relay_at * steps), steps - 1)

    def dst_shape(it, a):
        if it[0] == "gather":
            return SDS((NDEV * a.shape[1], a.shape[2]), a.dtype)
        return SDS((NDEV, a.shape[0] // NDEV, a.shape[1]), a.dtype)

    def wrapped(*refs):
        ins = refs[:n_in]
        src = refs[n_in:n_in + nit]
        outs = refs[n_in + nit:n_in + nit + n_out]
        dst = refs[n_in + nit + n_out:n_in + 2 * nit + n_out]
        scratch = refs[n_in + 2 * nit + n_out:n_in + 2 * nit + n_out + n_scr]
        start, relay, finish = _exchange_phases(items, src, dst, refs[n_in + 2 * nit + n_out + n_scr:])
        step = 0
        for d, g in enumerate(grid):
            step = step * g + pl.program_id(d)
        pl.when(step == 0)(start)
        body(*ins, *outs, *scratch)
        pl.when(step == relay_step)(relay)
        pl.when(step == steps - 1)(finish)

    npeer = len(_MASKS)
    res = pl.pallas_call(
        wrapped, name=name, grid=grid,
        in_specs=list(in_specs) + [hbm] * nit,
        out_specs=list(out_specs) + [hbm] * nit,
        out_shape=list(out_shape) + [dst_shape(it, a) for it, a in xchg],
        scratch_shapes=list(scratch_shapes) + [
            pltpu.SemaphoreType.DMA((npeer, nit)), pltpu.SemaphoreType.DMA((npeer, nit)), pltpu.SemaphoreType.DMA((nit,))],
        compiler_params=_cp(),
    )(*args, *[a for _, a in xchg])
    return list(res[:n_out]), list(res[n_out:])


def _sum_slots(slots, xchg=None):
    _, r, _ = slots[0].shape
    n = len(slots) // L
    rt = 64

    def body(*refs):
        for k in range(n):
            for l in range(L):
                src = refs[k * L + l]
                acc = src[0].astype(F32)
                for d in range(1, NDEV):
                    acc = acc + src[d].astype(F32)
                refs[n * L + k][l] = acc

    return _hosted_call(
        body, xchg, name=f"sum_slots_r{r}" + ("_x" if xchg else ""), grid=(r // rt,),
        in_specs=[pl.BlockSpec((NDEV, rt, D), lambda i: (0, i, 0))] * (n * L),
        out_specs=[pl.BlockSpec((L, rt, D), lambda i: (0, i, 0))] * n,
        out_shape=[SDS((L, r, D), F32)] * n, args=list(slots))


def _sum_small(slots):
    rows = slots.shape[0] // NDEV

    def body(in_ref, o_ref):
        acc = in_ref[pl.ds(0, rows), :]
        for d in range(1, NDEV):
            acc = acc + in_ref[pl.ds(d * rows, rows), :]
        o_ref[...] = acc

    vm = pl.BlockSpec(memory_space=pltpu.VMEM)
    return pl.pallas_call(
        body, name="sum_small", in_specs=[vm], out_specs=vm, out_shape=SDS((rows, LANE), F32),
        compiler_params=_cp())(slots)


def _adamw_update(w_ref, g_ref, m_ref, v_ref, d_ref, nm_ref, nv_ref):
    gv = g_ref[...]
    mn = B1 * m_ref[...] + (1.0 - B1) * gv
    vn = B2 * v_ref[...] + (1.0 - B2) * jnp.square(gv)
    nm_ref[...] = mn
    nv_ref[...] = vn
    m_hat = mn / (1.0 - B1 ** STEP)
    v_hat = vn / (1.0 - B2 ** STEP)
    d_ref[...] = -LR * (m_hat / (jnp.sqrt(v_hat) + AEPS) + WD * w_ref[...])


def _adamw_small(ws, gs, ms, vs):
    n = len(ws)

    def body(*refs):
        for i in range(n):
            _adamw_update(*[refs[j * n + i] for j in range(7)])

    vm = pl.BlockSpec(memory_space=pltpu.VMEM)
    res = pl.pallas_call(
        body, name="adamw_small", in_specs=[vm] * (4 * n), out_specs=[vm] * (3 * n),
        out_shape=[SDS(w.shape, F32) for _ in range(3) for w in ws],
        compiler_params=_cp(),
    )(*ws, *gs, *ms, *vs)
    return res[:n], res[n:2 * n], res[2 * n:]


def _adamw(w, g, m, v, xchg=None):
    rows, cols = w.shape
    t = rows
    for cand in (512, 256, 128, 64, 32, 16, 8):
        if rows % cand == 0:
            t = cand
            break

    def body(*refs):
        _adamw_update(*refs)

    blk = pl.BlockSpec((t, cols), lambda i: (i, 0))
    return _hosted_call(
        body, xchg, name=f"adamw_{rows}x{cols}", grid=(rows // t,),
        in_specs=[blk] * 4, out_specs=[blk] * 3, out_shape=[SDS((rows, cols), F32)] * 3, args=[w, g, m, v])


_DIST0 = BAND + TQ - 1
_N_FAR = _DIST0 - REL_CLIP + 1
_N_NEAR = NG - _N_FAR - (2 * REL_CLIP - 1)


def _bias_vector(rel_bias):
    far = jnp.broadcast_to(rel_bias[..., -1:], (L, NH, _N_FAR))
    near = jnp.broadcast_to(rel_bias[..., :1], (L, NH, _N_NEAR))
    return jnp.concatenate([far, lax.rev(rel_bias[..., 1:-1], (2,)), near], axis=2)[:, :, None, :]


def _bias_vector_grad(dgr):
    first = jnp.sum(dgr[..., :_N_NEAR], axis=-1, keepdims=True)
    last = jnp.sum(dgr[..., NG - _N_FAR:], axis=-1, keepdims=True)
    return jnp.concatenate([first, dgr[..., _N_NEAR:NG - _N_FAR], last], axis=-1)


def _pool_blockdiag(pool_w):
    eye = jnp.eye(2, dtype=F32)
    pw = pool_w.reshape(L, 2, 2, HD, HD)
    return jnp.einsum("ljaik,ab->ljaibk", pw, eye).reshape(L, 2, LANE, LANE)


def _pool_blockdiag_grad(dwbd):
    d = dwbd.reshape(L, 2, 2, HD, 2, HD)
    return jnp.stack([d[:, :, 0, :, 0, :], d[:, :, 1, :, 1, :]], axis=2).reshape(L, 4, HD, HD)


def _pack(arrays, rows):
    flat = jnp.concatenate([a.reshape(-1).astype(F32) for a in arrays])
    return jnp.pad(flat, (0, rows * LANE - flat.shape[0])).reshape(rows, LANE)


def _unpack(packed, shapes):
    flat = packed.reshape(-1)
    out, o = [], 0
    for shp in shapes:
        n = int(np.prod(shp))
        out.append(flat[o:o + n].reshape(shp))
        o += n
    return out


def _rows_for(shapes):
    n = sum(int(np.prod(s)) for s in shapes)
    return -(-n // (8 * LANE)) * 8


def _grads(x, target, small_w, shards):
    g1, qg, kg, rb, cw_shard, pw, ps, g2 = small_w
    g1 = g1.reshape(L, 1, D)
    g2 = g2.reshape(L, 1, D)
    qg2 = jnp.tile(qg, (1, 2)).reshape(L, 1, LANE)
    kg2 = jnp.tile(kg, (1, 2)).reshape(L, 1, LANE)
    ps3 = ps.reshape(L, 1, PWD)
    wbd = _pool_blockdiag(pw).astype(BF16)

    def gather(*kl):
        return [(("gather", k, l), shards[k]) for k, l in kl if l < L]

    full = {}

    def arrived(got, *kl):
        full.update(zip([x for x in kl if x[1] < L], got))

    bias, got = _bias_tables(_bias_vector(rb), gather((0, 0)) + [(("gather", "conv_w", 0), cw_shard)])
    arrived(got[:1], (0, 0))
    cshard = CW // NDEV
    cw_all = got[1].reshape(NDEV, -1)
    cw = jnp.concatenate([cw_all[d, :L * 3 * cshard].reshape(L, 3, cshard) for d in range(NDEV)], axis=2)
    saved = []
    h = x
    for l in range(L):
        kl = ((1, 0),) if l == 0 else ()
        (p, h_b, qkv), got = _in_proj(h, g1, full[0, l], qg2, kg2, l, gather(*kl))
        arrived(got, *kl)
        kl = ((2, 0), (3, 0)) if l == 0 else ((1, l), (3, l))
        (mix, lse, o32), got = _attn_fwd(qkv, bias, l, gather(*kl), relay_at=0.95 if l == 0 else 0.8)
        arrived(got, *kl)
        mix = _conv_pool_fwd(p, mix, cw, wbd, ps3, l)
        kl = ((0, l + 1), (2, l + 1))
        (xm, a, *out), got = _mlp_fwd(h, mix, full[1, l], g2, full[2, l], full[3, l], l, gather(*kl),
                                      target if l == L - 1 else None)
        arrived(got, *kl)
        saved.append((h, h_b, p, qkv, mix, lse, o32, xm, a))
        h = out[0]
    dx, sq = out

    grads = {}
    slots = {}

    def scatter(*kl):
        return [(("scatter", k), grads[k, l]) for k, l in kl if l < L]

    def left(got, *kl):
        slots.update(zip([x for x in kl if x[1] < L], got))

    per_layer = [None] * L
    for l in reversed(range(L)):
        x_in, h_b, p, qkv, mix, lse, o32, xm, a = saved[l]
        (dxm, dmix, f_b, da_b, h2_b, dxo_b, dxm_b, dg2), got = _mlp_bwd(
            dx, a, xm, g2, full[2, l], full[3, l], full[1, l], l, scatter((3, l + 1)))
        left(got, (3, l + 1))
        grads[1, l], _ = _wgrad(mix, dxm_b, f"w_out_l{l}")
        kl = ((1, 0),) if l == 0 else ()
        grads[2, l], got = _wgrad(da_b, h2_b, f"w_mlp1_l{l}", scatter(*kl))
        left(got, *kl)
        kl = ((2, 0),) if l == 0 else ()
        grads[3, l], got = _wgrad(f_b, dxo_b, f"w_mlp2_l{l}", scatter(*kl))
        left(got, *kl)
        kl = ((3, 0), (0, 1)) if l == 0 else ((2, l), (1, l))
        (dq, dk, dv, dbias, dqg, dkg), got = _attn_bwd(qkv, p, lse, o32, dmix, qg2, kg2, bias, l, scatter(*kl))
        left(got, *kl)
        dgb, dgc, dhin, du, dcw, dwbd, dps = _conv_pool_bwd(p, dmix, cw, wbd, ps3, l)
        (dx, dp_b, dg1), got = _in_proj_bwd(
            [dq, dk, dv, dgb, dgc, dhin, du], x_in, dxm, g1, full[0, l], l, scatter((0, l + 1)) if l else None)
        left(got, (0, l + 1))
        grads[0, l], _ = _wgrad(dp_b, h_b, f"w_in_l{l}")
        per_layer[l] = (dg1, dg2, dqg, dkg, _bias_tables_grad(dbias, l), dcw, dwbd, dps)
    (g_w1_t, g_w2), got = _sum_slots([slots[k, l] for k in (2, 3) for l in range(L)], scatter((0, 0)))
    left(got, (0, 0))
    sums = [_sum_slots([slots[k, l] for l in range(L)])[0][0] for k in (0, 1)] + [g_w1_t, g_w2]

    st = [jnp.stack([per_layer[l][k] for l in range(L)]) for k in range(8)]
    small = dict(
        g1=st[0].reshape(L, D), g2=st[1].reshape(L, D),
        qg=st[2].reshape(L, NH, HD).sum(1), kg=st[3].reshape(L, NH, HD).sum(1),
        rb=_bias_vector_grad(st[4].reshape(L, NH, NG)), cw=st[5], pw=_pool_blockdiag_grad(st[6]),
        ps=st[7].reshape(L, PWD))
    return sq, dx, sums, small


def kernel(x, norm1_g, w_in, q_norm_g, k_norm_g, rel_bias, conv_w, pool_w, pool_scale, w_out, norm2_g, w_mlp1, w_mlp2, loss_target, m_norm1_g, m_w_in, m_q_norm_g, m_k_norm_g, m_rel_bias, m_conv_w, m_pool_w, m_pool_scale, m_w_out, m_norm2_g, m_w_mlp1, m_w_mlp2, v_norm1_g, v_w_in, v_q_norm_g, v_k_norm_g, v_rel_bias, v_conv_w, v_pool_w, v_pool_scale, v_w_out, v_norm2_g, v_w_mlp1, v_w_mlp2):
    me = _index(_position())
    cshard = CW // NDEV

    shards = [jnp.swapaxes(w_in, 1, 2).astype(BF16), w_out.astype(BF16),
              jnp.swapaxes(w_mlp1, 1, 2).astype(BF16), w_mlp2.astype(BF16)]
    small_w = (norm1_g, q_norm_g, k_norm_g, rel_bias, _pack([conv_w], 8)[None], pool_w, pool_scale, norm2_g)
    sq, grad_x, (g_win_t, g_wout, g_w1_t, g_w2), small = _grads(x[0], loss_target[0], small_w, shards)
    g_w_in = jnp.swapaxes(g_win_t, 1, 2)
    g_w_mlp1 = jnp.swapaxes(g_w1_t, 1, 2)

    names = ("g1", "qg", "kg", "rb", "cw", "pw", "ps", "g2")
    gshapes = [(L, D), (L, HD), (L, HD), (L, NH, 2 * REL_CLIP + 1), (L, 3, CW), (L, 4, HD, HD), (L, PWD), (L, D)]
    garrs = [small[n] for n in names]
    rows = _rows_for(gshapes + [(1,)])
    packed = _pack(garrs + [sq[0, :1]], rows)[None]

    def big(w, g, m, v, xchg=None):
        shp = w.shape
        r = lambda a: a.reshape(-1, shp[-1])
        outs, got = _adamw(r(w), r(g), r(m), r(v), xchg)
        return [o.reshape(shp) for o in outs], got

    up_1, got = big(w_mlp1, g_w_mlp1, m_w_mlp1, v_w_mlp1, [(("gather", "small", 0), packed)])
    total = _sum_small(got[0])
    g_g1, g_qg, g_kg, g_rb, g_cw_full, g_pw, g_ps, g_g2, sq_sum = _unpack(total, gshapes + [(1,)])
    loss = (0.5 / D) * sq_sum[0]
    g_cw = lax.dynamic_slice_in_dim(g_cw_full, me * cshard, cshard, axis=2)
    up_in = big(w_in, g_w_in, m_w_in, v_w_in)[0]
    up_out = big(w_out, g_wout, m_w_out, v_w_out)[0]
    up_2 = big(w_mlp2, g_w2, m_w_mlp2, v_w_mlp2)[0]

    sw = [norm1_g, q_norm_g, k_norm_g, rel_bias, conv_w, pool_w, pool_scale, norm2_g]
    sg = [g_g1, g_qg, g_kg, g_rb, g_cw, g_pw, g_ps, g_g2]
    sm = [m_norm1_g, m_q_norm_g, m_k_norm_g, m_rel_bias, m_conv_w, m_pool_w, m_pool_scale, m_norm2_g]
    sv = [v_norm1_g, v_q_norm_g, v_k_norm_g, v_rel_bias, v_conv_w, v_pool_w, v_pool_scale, v_norm2_g]
    s_delta, s_m, s_v = _adamw_small(sw, sg, sm, sv)

    def order(small_list, in_, out_, m1, m2):
        g1_, qg_, kg_, rb_, cw_, pw_, ps_, g2_ = small_list
        return [g1_, in_, qg_, kg_, rb_, cw_, pw_, ps_, out_, g2_, m1, m2]

    grads = order(sg, g_w_in, g_wout, g_w_mlp1, g_w2)
    deltas = order(s_delta, up_in[0], up_out[0], up_1[0], up_2[0])
    new_m = order(s_m, up_in[1], up_out[1], up_1[1], up_2[1])
    new_v = order(s_v, up_in[2], up_out[2], up_1[2], up_2[2])
    return (loss, grad_x[None], *grads, *deltas, *new_m, *new_v)
```

```python
import numpy as np
import jax
import jax.numpy as jnp
from jax import lax
from jax.experimental import pallas as pl
from jax.experimental.pallas import tpu as pltpu

F32 = jnp.float32
BF16 = jnp.bfloat16
SDS = jax.ShapeDtypeStruct
MESH_ID = pl.DeviceIdType.MESH

D = 1024
L = 4
CHUNK = 64
N_PREV = 8
HD = 64
NH = 8
AW = 512
CW = 256
PWD = 256
DIN = 2560
DFF = 4096
EPS = 1e-6
NEG = -1e30
REL_CLIP = 128
POOL_WINDOWS = (2, 4, 8, 16)
LR, B1, B2, AEPS, WD, STEP = 0.001, 0.9, 0.999, 1e-08, 0.01, 10

NDEV = 8
LANE = 128
BAND = N_PREV * CHUNK
TQ = 256
WIN = TQ + BAND
NVAR = BAND // TQ + 1
NTOE = -(-(WIN + TQ - 1) // LANE) * LANE
NG = (NVAR - 1) * TQ + NTOE
PAD = 16
RB_NORM = 64
RB_SOFT = 16
VMEM_LIMIT = 56 * 1024 * 1024
SHARD_ROWS = (DIN // NDEV, D // NDEV, DFF // NDEV, DFF // NDEV)

assert 2 * HD == LANE and NH * HD == AW and POOL_WINDOWS == (2, 4, 8, 16)
assert TQ % CHUNK == 0 and BAND % TQ == 0 and max(POOL_WINDOWS) <= PAD and all(r % 16 == 0 for r in SHARD_ROWS)


def _cp(**kw):
    return pltpu.CompilerParams(vmem_limit_bytes=VMEM_LIMIT, **kw)


def _nn(a, b):
    return jnp.dot(a, b, preferred_element_type=F32)


def _nt(a, b):
    return lax.dot_general(a, b, (((1,), (1,)), ((), ())), preferred_element_type=F32)


def _tn(a, b):
    return lax.dot_general(a, b, (((0,), (0,)), ((), ())), preferred_element_type=F32)


def _const(shape):
    n = len(shape)
    return pl.BlockSpec(shape, lambda *_: (0,) * n, pipeline_mode=pl.Buffered(1))


def _layer(shape, l):
    n = len(shape)
    return pl.BlockSpec((None,) + tuple(shape), lambda *_: (l,) + (0,) * n, pipeline_mode=pl.Buffered(1))


def _lo_mask():
    return lax.broadcasted_iota(jnp.int32, (1, LANE), 1) < HD


def _half_sum(t, lo):
    s_lo = jnp.sum(jnp.where(lo, t, 0.0), axis=-1, keepdims=True)
    s_hi = jnp.sum(jnp.where(lo, 0.0, t), axis=-1, keepdims=True)
    return jnp.where(lo, s_lo, s_hi)


def _half_sum_mxu(t, lo):
    del lo
    ii = lax.broadcasted_iota(jnp.int32, (LANE, LANE), 0) // HD
    jj = lax.broadcasted_iota(jnp.int32, (LANE, LANE), 1) // HD
    ones = jnp.where(ii == jj, 1.0, 0.0).astype(BF16)
    hi = t.astype(BF16)
    return _nn(hi, ones) + _nn((t - hi.astype(F32)).astype(BF16), ones)


def _head_norm(x, lo, half_sum=_half_sum):
    r = lax.rsqrt(half_sum(x * x, lo) * (1.0 / HD) + EPS)
    return x * r, r


def _head_norm_bwd(dy, xn, r, g, lo, half_sum=_half_sum):
    dxn = dy * g
    mu = half_sum(dxn * xn, lo) * (1.0 / HD)
    return r * (dxn - xn * mu), dy * xn


def _rms_bwd(dy, x, g):
    r = lax.rsqrt(jnp.mean(x * x, axis=-1, keepdims=True) + EPS)
    xn = x * r
    dxn = dy * g
    mu = jnp.mean(dxn * xn, axis=-1, keepdims=True)
    return r * (dxn - xn * mu), dy * xn


def _in_proj(x, g1, win_t, qg2, kg2, l, xchg=None):
    s = x.shape[0]
    t = min(512, s)
    nblk = AW // LANE

    def body(x_ref, g_ref, w_ref, qg_ref, kg_ref, p_ref, h_ref, qkv_ref):
        xv = x_ref[...]
        r = lax.rsqrt(jnp.mean(xv * xv, axis=-1, keepdims=True) + EPS)
        h = (xv * r * g_ref[...]).astype(BF16)
        h_ref[...] = h
        p_ref[...] = _nt(h, w_ref[...])
        lo = _lo_mask()
        gains = (qg_ref[...] * (HD ** -0.5), kg_ref[...])
        for r0 in range(0, t, RB_NORM):
            rows = pl.ds(r0, RB_NORM)
            for c in range(3 * nblk):
                cols = pl.ds(c * LANE, LANE)
                v = p_ref[rows, cols]
                if c < 2 * nblk:
                    v = _head_norm(v, lo)[0] * gains[c // nblk]
                qkv_ref[rows, cols] = v.astype(BF16)

    row = lambda c: pl.BlockSpec((t, c), lambda i: (i, 0))
    return _hosted_call(
        body, xchg, name="in_proj", grid=(s // t,),
        in_specs=[row(D), _layer((1, D), l), _const((DIN, D)), _layer((1, LANE), l), _layer((1, LANE), l)],
        out_specs=[row(DIN), row(D), row(3 * AW)],
        out_shape=[SDS((s, DIN), F32), SDS((s, D), BF16), SDS((s, 3 * AW), BF16)], args=[x, g1, win_t, qg2, kg2])


def _bias_spec():
    return pl.BlockSpec((None, 2, TQ, WIN), lambda j, i: (jnp.maximum(NVAR - 1 - i, 0), j, 0, 0))


def _bias_layer_spec(l):
    return pl.BlockSpec((None, None, 2, TQ, WIN), lambda j, i: (l, jnp.maximum(NVAR - 1 - i, 0), j, 0, 0))


def _attn_fwd(qkv, bias, l, xchg=None, relay_at=0.8):
    s = qkv.shape[0]
    nq = s // TQ

    def body(q_ref, k_ref, v_ref, b_ref, o_ref, lse_ref, o32_ref, s_ref, p_ref, m_ref, den_ref, o0_ref):
        i = pl.program_id(1)
        ks = pl.multiple_of(jnp.maximum(i * TQ - BAND, 0), TQ)
        lo = _lo_mask()
        q = q_ref[...]
        kwin = k_ref[pl.ds(ks, WIN), :]
        vwin = v_ref[pl.ds(ks, WIN), :]
        for half in range(2):
            m_ = lo if half == 0 else jnp.logical_not(lo)
            s_ref[half] = _nt(jnp.where(m_, q, jnp.zeros_like(q)), kwin)
            for r0 in range(0, TQ, RB_SOFT):
                rows = pl.ds(r0, RB_SOFT)
                mx = jnp.max(s_ref[half, rows, :] + b_ref[half, rows, :], axis=-1, keepdims=True)
                m_ref[rows, :] = jnp.broadcast_to(mx, (RB_SOFT, LANE))
            for r0 in range(0, TQ, RB_SOFT):
                rows = pl.ds(r0, RB_SOFT)
                mx = m_ref[rows, 0:1]
                e = jnp.exp(s_ref[half, rows, :] + b_ref[half, rows, :] - mx)
                p_ref[half, rows, :] = e.astype(BF16)
                den = jnp.sum(e, axis=-1, keepdims=True)
                den_ref[rows, :] = jnp.broadcast_to(den, (RB_SOFT, LANE))
                lse = mx + jnp.log(den)
                if half == 0:
                    lse_ref[rows, :] = jnp.broadcast_to(lse, (RB_SOFT, LANE))
                else:
                    lse_ref[rows, :] = jnp.where(lo, lse_ref[rows, :], lse)
            o = _nn(p_ref[half], vwin) * (1.0 / den_ref[...])
            if half == 0:
                o0_ref[...] = o
            else:
                o = jnp.where(lo, o0_ref[...], o)
                o32_ref[...] = o
                o_ref[...] = o.astype(BF16)

    tile = pl.BlockSpec((TQ, LANE), lambda j, i: (i, j))
    stat = pltpu.VMEM((TQ, LANE), F32)
    return _hosted_call(
        body, xchg, name="attn_fwd", grid=(NH // 2, nq),
        in_specs=[
            tile,
            pl.BlockSpec((s, LANE), lambda j, i: (0, AW // LANE + j)),
            pl.BlockSpec((s, LANE), lambda j, i: (0, 2 * AW // LANE + j)),
            _bias_layer_spec(l),
        ],
        out_specs=[tile, tile, tile],
        out_shape=[SDS((s, D), BF16), SDS((s, AW), F32), SDS((s, AW), F32)], args=[qkv, qkv, qkv, bias],
        scratch_shapes=[pltpu.VMEM((2, TQ, WIN), F32), pltpu.VMEM((2, TQ, WIN), BF16), stat, stat, stat],
        relay_at=relay_at)


_C0 = 3 * AW // LANE


def _cp_in_specs(s, l):
    blk = lambda f: pl.BlockSpec((s, LANE), f)
    return [
        blk(lambda g: (0, _C0 + jnp.minimum(g, 1))),
        blk(lambda g: (0, _C0 + 2 + jnp.minimum(g, 1))),
        blk(lambda g: (0, _C0 + 4 + jnp.minimum(g, 1))),
        blk(lambda g: (0, _C0 + 6 + jnp.maximum(g - 2, 0))),
        pl.BlockSpec((None, 3, LANE), lambda g: (l, 0, jnp.minimum(g, 1))),
        pl.BlockSpec((None, None, LANE, LANE), lambda g: (l, jnp.maximum(g - 2, 0), 0, 0)),
        pl.BlockSpec((None, 1, LANE), lambda g: (l, 0, jnp.maximum(g - 2, 0))),
    ]


def _pool_window_sums(u_ref, buf_a, buf_b, jj, s, rt):
    nrt = s // rt
    for t in range(nrt):
        buf_a[pl.ds(PAD + t * rt, rt), :] = u_ref[pl.ds(t * rt, rt), :]

    def stage(src, dst, sh):
        for t in range(nrt):
            r0 = PAD + t * rt
            dst[pl.ds(r0, rt), :] = src[pl.ds(r0, rt), :] + src[pl.ds(r0 - sh, rt), :]

    stage(buf_a, buf_b, 1)
    stage(buf_b, buf_a, 2)

    @pl.when(jj == 1)
    def _():
        stage(buf_a, buf_b, 4)
        stage(buf_b, buf_a, 8)


def _pool_counts(jj, lo, r0, rt):
    w0, w1, w2, w3 = [float(w) for w in POOL_WINDOWS]
    w = jnp.where(lo, jnp.where(jj == 0, w0, w2), jnp.where(jj == 0, w1, w3))
    pos1 = (lax.broadcasted_iota(jnp.int32, (rt, LANE), 0) + (r0 + 1)).astype(F32)
    return jnp.minimum(pos1, w)


def _conv_pool_fwd(p, mix, conv_w, wbd, pscale, l):
    s = p.shape[0]
    rt = min(256, s)
    nrt = s // rt

    def body(gb_ref, gc_ref, hin_ref, u_ref, cw_ref, wbd_ref, ps_ref, mix_in, o_ref, buf_a, buf_b):
        del mix_in
        g = pl.program_id(0)
        zpad = jnp.zeros((PAD, LANE), F32)
        buf_a[pl.ds(0, PAD), :] = zpad
        buf_b[pl.ds(0, PAD), :] = zpad

        @pl.when(g < 2)
        def _conv():
            for t in range(nrt):
                buf_a[pl.ds(PAD + t * rt, rt), :] = gc_ref[pl.ds(t * rt, rt), :] * hin_ref[pl.ds(t * rt, rt), :]
            w0, w1, w2 = cw_ref[0:1, :], cw_ref[1:2, :], cw_ref[2:3, :]
            for t in range(nrt):
                r0 = PAD + t * rt
                y = w0 * buf_a[pl.ds(r0 - 2, rt), :] + w1 * buf_a[pl.ds(r0 - 1, rt), :] + w2 * buf_a[pl.ds(r0, rt), :]
                o_ref[pl.ds(t * rt, rt), :] = (gb_ref[pl.ds(t * rt, rt), :] * y).astype(BF16)

        @pl.when(g >= 2)
        def _pool():
            jj = g - 2
            lo = _lo_mask()
            _pool_window_sums(u_ref, buf_a, buf_b, jj, s, rt)
            wb = wbd_ref[...]
            for t in range(nrt):
                r0 = PAD + t * rt
                wsum = jnp.where(lo, buf_b[pl.ds(r0, rt), :], buf_a[pl.ds(r0, rt), :])
                m = wsum / _pool_counts(jj, lo, t * rt, rt) - u_ref[pl.ds(t * rt, rt), :]
                o_ref[pl.ds(t * rt, rt), :] = (_nn(m.astype(BF16), wb) * ps_ref[...]).astype(BF16)

    return pl.pallas_call(
        body, name="conv_pool_fwd", grid=(4,),
        in_specs=_cp_in_specs(s, l) + [pl.BlockSpec(memory_space=pl.ANY)],
        out_specs=pl.BlockSpec((s, LANE), lambda g: (0, AW // LANE + g)),
        out_shape=SDS((s, D), BF16),
        scratch_shapes=[pltpu.VMEM((s + 2 * PAD, LANE), F32), pltpu.VMEM((s + 2 * PAD, LANE), F32)],
        input_output_aliases={7: 0},
        compiler_params=_cp(),
    )(p, p, p, p, conv_w, wbd, pscale, mix)


def _mlp_fwd(x, mix, wout, g2, w1_t, w2, l, xchg=None, target=None):
    s = x.shape[0]
    t = min(256, s)

    def body(*refs):
        x_ref, mix_ref, wo_ref, g_ref, w1_ref, w2_ref = refs[:6]
        xm_ref, a_ref, xo_ref = refs[-4:-1] if target is not None else refs[-3:]
        xm = x_ref[...] + _nn(mix_ref[...], wo_ref[...])
        xm_ref[...] = xm
        r = lax.rsqrt(jnp.mean(xm * xm, axis=-1, keepdims=True) + EPS)
        h2 = (xm * r * g_ref[...]).astype(BF16)
        a = _nt(h2, w1_ref[...])
        a_ref[...] = a.astype(BF16)
        f = jnp.square(jnp.maximum(a, 0.0)).astype(BF16)
        xo = xm + _nn(f, w2_ref[...])
        if target is None:
            xo_ref[...] = xo
        else:
            acc_ref = refs[-1]

            @pl.when(pl.program_id(0) == 0)
            def _():
                acc_ref[...] = jnp.zeros_like(acc_ref)
            e = xo - refs[6][...]
            xo_ref[...] = e * (1.0 / D)
            acc_ref[...] += jnp.sum(e * e)

    row = lambda c: pl.BlockSpec((t, c), lambda i: (i, 0))
    last = target is not None
    return _hosted_call(
        body, xchg, name="mlp_fwd_loss" if last else "mlp_fwd", grid=(s // t,),
        in_specs=[row(D), row(D), _const((D, D)), _layer((1, D), l), _const((DFF, D)), _const((DFF, D))] + [row(D)] * last,
        out_specs=[row(D), row(DFF), row(D)] + [pl.BlockSpec((8, LANE), lambda i: (0, 0))] * last,
        out_shape=[SDS((s, D), F32), SDS((s, DFF), BF16), SDS((s, D), F32)] + [SDS((8, LANE), F32)] * last,
        args=[x, mix, wout, g2, w1_t, w2] + [target] * last, relay_at=0.85)


def _mlp_bwd(dxo, a, xm, g2, w1_t, w2, wout, l, xchg=None):
    s = dxo.shape[0]
    t = min(256, s)

    def body(dxo_ref, a_ref, xm_ref, g_ref, w1_ref, w2_ref, wo_ref,
             dxm_ref, dmix_ref, f_ref, da_ref, h2_ref, dxob_ref, dxmb_ref, dg_ref):
        @pl.when(pl.program_id(0) == 0)
        def _():
            dg_ref[...] = jnp.zeros_like(dg_ref)
        dxo = dxo_ref[...]
        dxob = dxo.astype(BF16)
        dxob_ref[...] = dxob
        ra = jnp.maximum(a_ref[...].astype(F32), 0.0)
        f_ref[...] = jnp.square(ra).astype(BF16)
        dab = (_nt(dxob, w2_ref[...]) * (2.0 * ra)).astype(BF16)
        da_ref[...] = dab
        dh2 = _nn(dab, w1_ref[...])
        xm = xm_ref[...]
        g = g_ref[...]
        r = lax.rsqrt(jnp.mean(xm * xm, axis=-1, keepdims=True) + EPS)
        h2_ref[...] = (xm * r * g).astype(BF16)
        dx_n, dgr = _rms_bwd(dh2, xm, g)
        dg_ref[...] += jnp.sum(dgr, axis=0, keepdims=True)
        dxm = dxo + dx_n
        dxm_ref[...] = dxm
        dxmb = dxm.astype(BF16)
        dxmb_ref[...] = dxmb
        dmix_ref[...] = _nt(dxmb, wo_ref[...])

    row = lambda c: pl.BlockSpec((t, c), lambda i: (i, 0))
    return _hosted_call(
        body, xchg, name="mlp_bwd", grid=(s // t,),
        in_specs=[row(D), row(DFF), row(D), _layer((1, D), l), _const((DFF, D)), _const((DFF, D)), _const((D, D))],
        out_specs=[row(D), row(D), row(DFF), row(DFF), row(D), row(D), row(D), pl.BlockSpec((1, D), lambda i: (0, 0))],
        out_shape=[SDS((s, D), F32), SDS((s, D), F32), SDS((s, DFF), BF16), SDS((s, DFF), BF16),
                   SDS((s, D), BF16), SDS((s, D), BF16), SDS((s, D), BF16), SDS((1, D), F32)],
        args=[dxo, a, xm, g2, w1_t, w2, wout])


def _attn_bwd(qkv, p, lse, o32, dmix, qg2, kg2, bias, l, xchg=None):
    s = p.shape[0]
    nq = s // TQ
    scale = HD ** -0.5

    def body(qs_ref, kb_ref, vb_ref, q_ref, k_ref, qg_ref, kg_ref, b_ref, lse_ref, o_ref, do_ref,
             dq_ref, dk_ref, dv_ref, db_ref, dqg_ref, dkg_ref,
             dk_acc, dv_acc, s_ref, dp_ref, ds_ref, pb_ref, dqn_ref, dl_ref):
        i = pl.program_id(1)
        kt = jnp.maximum(i - BAND // TQ, 0)
        ks = pl.multiple_of(kt * TQ, TQ)
        lo = _lo_mask()

        @pl.when(i == 0)
        def _():
            dk_acc[...] = jnp.zeros_like(dk_acc)
            dv_acc[...] = jnp.zeros_like(dv_acc)
            dqg_ref[...] = jnp.zeros_like(dqg_ref)
            dkg_ref[...] = jnp.zeros_like(dkg_ref)

        @pl.when(i < NVAR)
        def _():
            db_ref[...] = jnp.zeros_like(db_ref)

        qs = qs_ref[...]
        kwin = kb_ref[pl.ds(ks, WIN), :]
        vwin = vb_ref[pl.ds(ks, WIN), :]
        do = do_ref[...]
        dob = do.astype(BF16)
        dl_ref[...] = _half_sum(do * o_ref[...], lo)
        for half in range(2):
            m_ = lo if half == 0 else jnp.logical_not(lo)
            qa = jnp.where(m_, qs, jnp.zeros_like(qs))
            doa = jnp.where(m_, dob, jnp.zeros_like(dob))
            s_ref[half] = _nt(qa, kwin)
            dp_ref[half] = _nt(doa, vwin)
            for r0 in range(0, TQ, RB_SOFT):
                rows = pl.ds(r0, RB_SOFT)
                lse_h = lse_ref[rows, half * HD:half * HD + 1]
                pm = jnp.exp(s_ref[half, rows, :] + b_ref[half, rows, :] - lse_h)
                ds = pm * (dp_ref[half, rows, :] - dl_ref[rows, half * HD:half * HD + 1])
                db_ref[half, rows, :] += ds
                ds_ref[half, rows, :] = ds.astype(BF16)
                pb_ref[half, rows, :] = pm.astype(BF16)
            dsb = ds_ref[half]
            dq_h = _nn(dsb, kwin)
            if half == 0:
                dqn_ref[...] = dq_h
            else:
                dqn_ref[...] = jnp.where(lo, dqn_ref[...], dq_h)
            dk_t = _tn(qa, dsb)
            dv_t = _tn(doa, pb_ref[half])
            for t in range(WIN // TQ):
                dk_acc[kt + t] += dk_t[:, t * TQ:(t + 1) * TQ]
                dv_acc[kt + t] += dv_t[:, t * TQ:(t + 1) * TQ]
        qg, kg = qg_ref[...], kg_ref[...]
        xq, rq = _head_norm(q_ref[...], lo)
        dq, dqg_rows = _head_norm_bwd(dqn_ref[...] * scale, xq, rq, qg, lo)
        dq_ref[...] = dq.astype(BF16)
        dqg_ref[...] += jnp.sum(dqg_rows, axis=0, keepdims=True)

        @pl.when(i == nq - 1)
        def _():
            dkg = jnp.zeros((1, LANE), F32)
            for t in range(nq):
                rows = pl.ds(t * TQ, TQ)
                xk, rk = _head_norm(k_ref[rows, :], lo, _half_sum_mxu)
                dk, dkg_rows = _head_norm_bwd(dk_acc[t].T, xk, rk, kg, lo, _half_sum_mxu)
                dk_ref[rows, :] = dk.astype(BF16)
                dv_ref[rows, :] = dv_acc[t].T.astype(BF16)
                dkg = dkg + jnp.sum(dkg_rows, axis=0, keepdims=True)
            dkg_ref[...] = dkg

    tile = pl.BlockSpec((TQ, LANE), lambda j, i: (i, j))
    kcol = lambda c0: pl.BlockSpec((s, LANE), lambda j, i: (0, c0 + j))
    gain = pl.BlockSpec((None, 1, LANE), lambda j, i: (j, 0, 0))
    return _hosted_call(
        body, xchg, name="attn_bwd", grid=(NH // 2, nq),
        in_specs=[
            tile, kcol(AW // LANE), kcol(2 * AW // LANE), tile, kcol(AW // LANE),
            _layer((1, LANE), l), _layer((1, LANE), l),
            _bias_layer_spec(l), tile, tile, tile,
        ],
        out_specs=[tile, kcol(0), kcol(0), _bias_spec(), gain, gain],
        out_shape=[SDS((s, AW), BF16), SDS((s, AW), BF16), SDS((s, AW), BF16),
                   SDS((NVAR, NH, TQ, WIN), F32), SDS((NH // 2, 1, LANE), F32), SDS((NH // 2, 1, LANE), F32)],
        scratch_shapes=[pltpu.VMEM((nq, LANE, TQ), F32), pltpu.VMEM((nq, LANE, TQ), F32),
                        pltpu.VMEM((2, TQ, WIN), F32), pltpu.VMEM((2, TQ, WIN), F32),
                        pltpu.VMEM((2, TQ, WIN), BF16), pltpu.VMEM((2, TQ, WIN), BF16),
                        pltpu.VMEM((TQ, LANE), F32), pltpu.VMEM((TQ, LANE), F32)],
        args=[qkv, qkv, qkv, p, p, qg2, kg2, bias, lse, o32, dmix])


def _conv_pool_bwd(p, dmix, conv_w, wbd, pscale, l):
    s = p.shape[0]
    rt = min(256, s)
    nrt = s // rt

    def body(gb_ref, gc_ref, hin_ref, u_ref, cw_ref, wbd_ref, ps_ref, dy_ref,
             dgb_ref, dgc_ref, dhin_ref, du_ref, dcw_ref, dwbd_ref, dps_ref, buf_a, buf_b, buf_c, buf_d):
        g = pl.program_id(0)
        zpad = jnp.zeros((PAD, LANE), F32)
        for buf in (buf_a, buf_b, buf_c):
            buf[pl.ds(0, PAD), :] = zpad
            buf[pl.ds(PAD + s, PAD), :] = zpad

        @pl.when(g < 2)
        def _conv():
            for t in range(nrt):
                rows = pl.ds(t * rt, rt)
                buf_a[pl.ds(PAD + t * rt, rt), :] = gc_ref[rows, :] * hin_ref[rows, :]
                buf_b[pl.ds(PAD + t * rt, rt), :] = dy_ref[rows, :] * gb_ref[rows, :]
            w0, w1, w2 = cw_ref[0:1, :], cw_ref[1:2, :], cw_ref[2:3, :]
            d0 = jnp.zeros((1, LANE), F32)
            d1 = jnp.zeros((1, LANE), F32)
            d2 = jnp.zeros((1, LANE), F32)
            for t in range(nrt):
                rows = pl.ds(t * rt, rt)
                r0 = PAD + t * rt
                z2, z1, z0 = buf_a[pl.ds(r0 - 2, rt), :], buf_a[pl.ds(r0 - 1, rt), :], buf_a[pl.ds(r0, rt), :]
                y = w0 * z2 + w1 * z1 + w2 * z0
                dgb_ref[rows, :] = (dy_ref[rows, :] * y).astype(BF16)
                e0 = buf_b[pl.ds(r0, rt), :]
                d0 = d0 + jnp.sum(e0 * z2, axis=0, keepdims=True)
                d1 = d1 + jnp.sum(e0 * z1, axis=0, keepdims=True)
                d2 = d2 + jnp.sum(e0 * z0, axis=0, keepdims=True)
                dz = w2 * e0 + w1 * buf_b[pl.ds(r0 + 1, rt), :] + w0 * buf_b[pl.ds(r0 + 2, rt), :]
                dgc_ref[rows, :] = (dz * hin_ref[rows, :]).astype(BF16)
                dhin_ref[rows, :] = (dz * gc_ref[rows, :]).astype(BF16)
            dcw_ref[0:1, :] = d0
            dcw_ref[1:2, :] = d1
            dcw_ref[2:3, :] = d2

        @pl.when(g >= 2)
        def _pool():
            jj = g - 2
            lo = _lo_mask()
            _pool_window_sums(u_ref, buf_a, buf_b, jj, s, rt)
            wb = wbd_ref[...]
            ps = ps_ref[...]
            dps = jnp.zeros((1, LANE), F32)
            dwb = jnp.zeros((LANE, LANE), F32)
            for t in range(nrt):
                rows = pl.ds(t * rt, rt)
                r0 = PAD + t * rt
                cnt = _pool_counts(jj, lo, t * rt, rt)
                wsum = jnp.where(lo, buf_b[pl.ds(r0, rt), :], buf_a[pl.ds(r0, rt), :])
                mb = (wsum / cnt - u_ref[rows, :]).astype(BF16)
                dy = dy_ref[rows, :]
                dps = dps + jnp.sum(dy * _nn(mb, wb), axis=0, keepdims=True)
                dmp = (dy * ps).astype(BF16)
                dwb = dwb + _tn(mb, dmp)
                dm = _nt(dmp, wb)
                buf_d[rows, :] = dm
                buf_c[pl.ds(r0, rt), :] = dm / cnt
            dps_ref[...] = dps
            dwbd_ref[...] = dwb

            def stage(src, dst, sh):
                for t in range(nrt):
                    r0 = PAD + t * rt
                    dst[pl.ds(r0, rt), :] = src[pl.ds(r0, rt), :] + src[pl.ds(r0 + sh, rt), :]

            def finish(first, second):
                for t in range(nrt):
                    rows = pl.ds(t * rt, rt)
                    r0 = PAD + t * rt
                    fw = jnp.where(lo, first[pl.ds(r0, rt), :], second[pl.ds(r0, rt), :])
                    du_ref[rows, :] = (fw - buf_d[rows, :]).astype(BF16)

            stage(buf_c, buf_a, 1)
            stage(buf_a, buf_b, 2)

            @pl.when(jj == 0)
            def _():
                finish(buf_a, buf_b)

            @pl.when(jj == 1)
            def _():
                stage(buf_b, buf_c, 4)
                stage(buf_c, buf_a, 8)
                finish(buf_c, buf_a)

    cblk = pl.BlockSpec((s, LANE), lambda g: (0, jnp.minimum(g, 1)))
    pblk = pl.BlockSpec((s, LANE), lambda g: (0, jnp.maximum(g - 2, 0)))
    padded = pltpu.VMEM((s + 2 * PAD, LANE), F32)
    return pl.pallas_call(
        body, name="conv_pool_bwd", grid=(4,),
        in_specs=_cp_in_specs(s, l) + [pl.BlockSpec((s, LANE), lambda g: (0, AW // LANE + g))],
        out_specs=[cblk, cblk, cblk, pblk,
                   pl.BlockSpec((3, LANE), lambda g: (0, jnp.minimum(g, 1))),
                   pl.BlockSpec((None, LANE, LANE), lambda g: (jnp.maximum(g - 2, 0), 0, 0)),
                   pl.BlockSpec((1, LANE), lambda g: (0, jnp.maximum(g - 2, 0)))],
        out_shape=[SDS((s, CW), BF16), SDS((s, CW), BF16), SDS((s, CW), BF16), SDS((s, PWD), BF16),
                   SDS((3, CW), F32), SDS((2, LANE, LANE), F32), SDS((1, PWD), F32)],
        scratch_shapes=[padded, padded, padded, pltpu.VMEM((s, LANE), F32)],
        compiler_params=_cp(),
    )(p, p, p, p, conv_w, wbd, pscale, dmix)


def _in_proj_bwd(parts, x, dxm, g1, win_t, l, xchg=None):
    s = x.shape[0]
    t = min(256, s)
    widths = [a.shape[1] for a in parts]
    offs = [int(o) for o in np.cumsum([0] + widths[:-1])]
    n = len(parts)

    def body(*refs):
        part_refs = refs[:n]
        x_ref, dxm_ref, g_ref, w_ref, dx_ref, dp_ref, dg_ref = refs[n:]

        @pl.when(pl.program_id(0) == 0)
        def _():
            dg_ref[...] = jnp.zeros_like(dg_ref)
        for r, o, w in zip(part_refs, offs, widths):
            dp_ref[:, o:o + w] = r[...]
        dh = _nn(dp_ref[...], w_ref[...])
        dx_n, dgr = _rms_bwd(dh, x_ref[...], g_ref[...])
        dg_ref[...] += jnp.sum(dgr, axis=0, keepdims=True)
        dx_ref[...] = dxm_ref[...] + dx_n

    row = lambda c: pl.BlockSpec((t, c), lambda i: (i, 0))
    return _hosted_call(
        body, xchg, name="in_proj_bwd", grid=(s // t,),
        in_specs=[row(w) for w in widths] + [row(D), row(D), _layer((1, D), l), _const((DIN, D))],
        out_specs=[row(D), row(DIN), pl.BlockSpec((1, D), lambda i: (0, 0))],
        out_shape=[SDS((s, D), F32), SDS((s, DIN), BF16), SDS((1, D), F32)], args=[*parts, x, dxm, g1, win_t])


def _wgrad(a, b, tag, xchg=None):
    s, m = a.shape
    mb = 512

    def body(a_ref, b_ref, o_ref):
        o_ref[...] = _tn(a_ref[...], b_ref[...]).astype(BF16)

    (out,), got = _hosted_call(
        body, xchg, name=f"wgrad_{tag}", grid=(m // mb,),
        in_specs=[pl.BlockSpec((s, mb), lambda mi: (0, mi)), _const((s, D))],
        out_specs=[pl.BlockSpec((mb, D), lambda mi: (mi, 0))],
        out_shape=[SDS((m, D), BF16)], args=[a, b])
    return out, got


def _bias_tables(gvec, xchg=None):
    def body(g_ref, o_ref):
        qc = lax.broadcasted_iota(jnp.int32, (TQ, WIN), 0) // CHUNK
        kc = lax.broadcasted_iota(jnp.int32, (TQ, WIN), 1) // CHUNK
        for var in range(NVAR):
            vec = jnp.broadcast_to(g_ref[:, var * TQ:var * TQ + NTOE], (TQ, NTOE))
            toe = pltpu.roll(vec, NTOE - TQ + 1, 1, stride=1, stride_axis=0)[:, :WIN]
            rel = (BAND - var * TQ) // CHUNK + qc - kc
            o_ref[var] = jnp.where((rel >= 0) & (rel <= N_PREV), toe, NEG)

    (out,), got = _hosted_call(
        body, xchg, name="bias_tables", grid=(L, NH),
        in_specs=[pl.BlockSpec((None, None, 1, NG), lambda l, h: (l, h, 0, 0))],
        out_specs=[pl.BlockSpec((None, NVAR, None, TQ, WIN), lambda l, h: (l, 0, h, 0, 0))],
        out_shape=[SDS((L, NVAR, NH, TQ, WIN), F32)], args=[gvec])
    return out, got


def _bias_tables_grad(dbias, l):
    nb = NTOE // LANE
    wb = WIN // LANE

    def body(d_ref, o_ref):
        ii = lax.broadcasted_iota(jnp.int32, (LANE, LANE), 0)
        jj = lax.broadcasted_iota(jnp.int32, (LANE, LANE), 1)
        flip = jnp.where(ii + jj == LANE - 1, 1.0, 0.0).astype(BF16)
        o_ref[...] = jnp.zeros_like(o_ref)
        for var in range(NVAR):
            blocks = []
            for b in range(nb):
                src = nb - 1 - b
                if src >= wb:
                    blocks.append(jnp.zeros((TQ, LANE), F32))
                    continue
                xv = d_ref[var, :, src * LANE:(src + 1) * LANE]
                hi = xv.astype(BF16)
                lo = (xv - hi.astype(F32)).astype(BF16)
                blocks.append(_nn(hi, flip) + _nn(lo, flip))
            rev = jnp.concatenate(blocks, axis=1)
            skew = pltpu.roll(rev, NTOE - TQ + 1, 1, stride=1, stride_axis=0)
            off = NG - NTOE - var * TQ
            o_ref[:, off:off + NTOE] += jnp.sum(skew, axis=0, keepdims=True)

    return pl.pallas_call(
        body, name=f"bias_tables_grad_l{l}", grid=(NH,),
        in_specs=[pl.BlockSpec((NVAR, None, TQ, WIN), lambda h: (0, h, 0, 0))],
        out_specs=pl.BlockSpec((None, 1, NG), lambda h: (h, 0, 0)),
        out_shape=SDS((NH, 1, NG), F32),
        compiler_params=_cp(),
    )(dbias)


_SIBLING = (0, 0, 1)
_CHIPS = [(1, 0, 0), (0, 1, 0), (1, 1, 0)]
_MASKS = [_SIBLING] + _CHIPS + [(1, 0, 1), (0, 1, 1), (1, 1, 1)]


def _position():
    return lax.axis_index("x"), lax.axis_index("y"), lax.axis_index("c")


def _peer(pos, mask):
    return tuple(1 - a if f else a for a, f in zip(pos, mask))


def _index(pos):
    return 4 * pos[0] + 2 * pos[1] + pos[2]


def _exchange_phases(items, src, dst, sems):
    send_sems, recv_sems, local_sems = sems
    me = _position()
    sib = _peer(me, _SIBLING)

    def remote(s_ref, d_ref, pi, n, to):
        return pltpu.make_async_remote_copy(
            src_ref=s_ref, dst_ref=d_ref, send_sem=send_sems.at[pi, n], recv_sem=recv_sems.at[pi, n],
            device_id=to, device_id_type=MESH_ID)

    def parts(n):
        it = items[n]
        r = src[n].shape[1] if it[0] == "gather" else src[n].shape[0] // NDEV
        block = lambda ref, pos: ref.at[pl.ds(_index(pos) * r, r), :]
        if it[0] == "gather":
            own = src[n].at[it[2]]
            local = pltpu.make_async_copy(own, block(dst[n], me), local_sems.at[n])
            sends = [remote(own, block(dst[n], me), pi, n, _peer(me, m)) for pi, m in enumerate([_SIBLING] + _CHIPS)]
            hops = [(remote(block(dst[n], _peer(me, m)), block(dst[n], _peer(me, m)), 1 + j, n, _peer(me, m)),
                     remote(block(dst[n], _peer(me, m)), block(dst[n], _peer(me, m)), 4 + j, n, sib))
                    for j, m in enumerate(_CHIPS)]
            lands = [remote(own, block(dst[n], sib), 0, n, sib)]
            lands += [remote(own, block(dst[n], _peer(sib, m)), 4 + j, n, sib) for j, m in enumerate(_CHIPS)]
        else:
            local = pltpu.make_async_copy(block(src[n], me), dst[n].at[_index(me)], local_sems.at[n])
            sends = [remote(block(src[n], _peer(me, m)), dst[n].at[_index(me)], pi, n, _peer(me, m))
                     for pi, m in enumerate(_MASKS)]
            hops = []
            lands = [remote(block(src[n], me), dst[n].at[_index(_peer(me, m))], pi, n, _peer(me, m))
                     for pi, m in enumerate(_MASKS)]
        return local, sends, hops, lands

    def start():
        for n in range(len(items)):
            local, sends, _, _ = parts(n)
            local.start()
            for cp in sends:
                cp.start()

    def relay():
        for n in range(len(items)):
            for arrived, onward in parts(n)[2]:
                arrived.wait_recv()
                onward.start()

    def finish():
        for n in range(len(items)):
            local, sends, hops, lands = parts(n)
            for cp in lands:
                cp.wait_recv()
            for cp in sends + [onward for _, onward in hops]:
                cp.wait_send()
            local.wait()

    return start, relay, finish


def _hosted_call(body, xchg, *, name, grid, in_specs, out_specs, out_shape, args, scratch_shapes=(), relay_at=0.8):
    if not xchg:
        outs = pl.pallas_call(
            body, name=name, grid=grid, in_specs=list(in_specs), out_specs=list(out_specs),
            out_shape=list(out_shape), scratch_shapes=list(scratch_shapes), compiler_params=_cp())(*args)
        return outs, []
    items = [it for it, _ in xchg]
    n_in, n_out, n_scr, nit = len(args), len(out_shape), len(scratch_shapes), len(items)
    hbm = pl.BlockSpec(memory_space=pl.ANY)
    steps = int(np.prod(grid))
    relay_step = min(int(relay_at * steps), steps - 1)

    def dst_shape(it, a):
        if it[0] == "gather":
            return SDS((NDEV * a.shape[1], a.shape[2]), a.dtype)
        return SDS((NDEV, a.shape[0] // NDEV, a.shape[1]), a.dtype)

    def wrapped(*refs):
        ins = refs[:n_in]
        src = refs[n_in:n_in + nit]
        outs = refs[n_in + nit:n_in + nit + n_out]
        dst = refs[n_in + nit + n_out:n_in + 2 * nit + n_out]
        scratch = refs[n_in + 2 * nit + n_out:n_in + 2 * nit + n_out + n_scr]
        start, relay, finish = _exchange_phases(items, src, dst, refs[n_in + 2 * nit + n_out + n_scr:])
        step = 0
        for d, g in enumerate(grid):
            step = step * g + pl.program_id(d)
        pl.when(step == 0)(start)
        body(*ins, *outs, *scratch)
        pl.when(step == relay_step)(relay)
        pl.when(step == steps - 1)(finish)

    npeer = len(_MASKS)
    res = pl.pallas_call(
        wrapped, name=name, grid=grid,
        in_specs=list(in_specs) + [hbm] * nit,
        out_specs=list(out_specs) + [hbm] * nit,
        out_shape=list(out_shape) + [dst_shape(it, a) for it, a in xchg],
        scratch_shapes=list(scratch_shapes) + [
            pltpu.SemaphoreType.DMA((npeer, nit)), pltpu.SemaphoreType.DMA((npeer, nit)), pltpu.SemaphoreType.DMA((nit,))],
        compiler_params=_cp(),
    )(*args, *[a for _, a in xchg])
    return list(res[:n_out]), list(res[n_out:])


def _sum_slots(slots, xchg=None):
    _, r, _ = slots[0].shape
    n = len(slots) // L
    rt = 64

    def body(*refs):
        for k in range(n):
            for l in range(L):
                src = refs[k * L + l]
                acc = src[0].astype(F32)
                for d in range(1, NDEV):
                    acc = acc + src[d].astype(F32)
                refs[n * L + k][l] = acc

    return _hosted_call(
        body, xchg, name=f"sum_slots_r{r}" + ("_x" if xchg else ""), grid=(r // rt,),
        in_specs=[pl.BlockSpec((NDEV, rt, D), lambda i: (0, i, 0))] * (n * L),
        out_specs=[pl.BlockSpec((L, rt, D), lambda i: (0, i, 0))] * n,
        out_shape=[SDS((L, r, D), F32)] * n, args=list(slots))


def _sum_small(slots):
    rows = slots.shape[0] // NDEV

    def body(in_ref, o_ref):
        acc = in_ref[pl.ds(0, rows), :]
        for d in range(1, NDEV):
            acc = acc + in_ref[pl.ds(d * rows, rows), :]
        o_ref[...] = acc

    vm = pl.BlockSpec(memory_space=pltpu.VMEM)
    return pl.pallas_call(
        body, name="sum_small", in_specs=[vm], out_specs=vm, out_shape=SDS((rows, LANE), F32),
        compiler_params=_cp())(slots)


def _adamw_update(w_ref, g_ref, m_ref, v_ref, d_ref, nm_ref, nv_ref):
    gv = g_ref[...]
    mn = B1 * m_ref[...] + (1.0 - B1) * gv
    vn = B2 * v_ref[...] + (1.0 - B2) * jnp.square(gv)
    nm_ref[...] = mn
    nv_ref[...] = vn
    m_hat = mn / (1.0 - B1 ** STEP)
    v_hat = vn / (1.0 - B2 ** STEP)
    d_ref[...] = -LR * (m_hat / (jnp.sqrt(v_hat) + AEPS) + WD * w_ref[...])


def _adamw_small(ws, gs, ms, vs):
    n = len(ws)

    def body(*refs):
        for i in range(n):
            _adamw_update(*[refs[j * n + i] for j in range(7)])

    vm = pl.BlockSpec(memory_space=pltpu.VMEM)
    res = pl.pallas_call(
        body, name="adamw_small", in_specs=[vm] * (4 * n), out_specs=[vm] * (3 * n),
        out_shape=[SDS(w.shape, F32) for _ in range(3) for w in ws],
        compiler_params=_cp(),
    )(*ws, *gs, *ms, *vs)
    return res[:n], res[n:2 * n], res[2 * n:]


def _adamw(w, g, m, v, xchg=None):
    rows, cols = w.shape
    t = rows
    for cand in (512, 256, 128, 64, 32, 16, 8):
        if rows % cand == 0:
            t = cand
            break

    def body(*refs):
        _adamw_update(*refs)

    blk = pl.BlockSpec((t, cols), lambda i: (i, 0))
    return _hosted_call(
        body, xchg, name=f"adamw_{rows}x{cols}", grid=(rows // t,),
        in_specs=[blk] * 4, out_specs=[blk] * 3, out_shape=[SDS((rows, cols), F32)] * 3, args=[w, g, m, v])


_DIST0 = BAND + TQ - 1
_N_FAR = _DIST0 - REL_CLIP + 1
_N_NEAR = NG - _N_FAR - (2 * REL_CLIP - 1)


def _bias_vector(rel_bias):
    far = jnp.broadcast_to(rel_bias[..., -1:], (L, NH, _N_FAR))
    near = jnp.broadcast_to(rel_bias[..., :1], (L, NH, _N_NEAR))
    return jnp.concatenate([far, lax.rev(rel_bias[..., 1:-1], (2,)), near], axis=2)[:, :, None, :]


def _bias_vector_grad(dgr):
    first = jnp.sum(dgr[..., :_N_NEAR], axis=-1, keepdims=True)
    last = jnp.sum(dgr[..., NG - _N_FAR:], axis=-1, keepdims=True)
    return jnp.concatenate([first, dgr[..., _N_NEAR:NG - _N_FAR], last], axis=-1)


def _pool_blockdiag(pool_w):
    eye = jnp.eye(2, dtype=F32)
    pw = pool_w.reshape(L, 2, 2, HD, HD)
    return jnp.einsum("ljaik,ab->ljaibk", pw, eye).reshape(L, 2, LANE, LANE)


def _pool_blockdiag_grad(dwbd):
    d = dwbd.reshape(L, 2, 2, HD, 2, HD)
    return jnp.stack([d[:, :, 0, :, 0, :], d[:, :, 1, :, 1, :]], axis=2).reshape(L, 4, HD, HD)


def _pack(arrays, rows):
    flat = jnp.concatenate([a.reshape(-1).astype(F32) for a in arrays])
    return jnp.pad(flat, (0, rows * LANE - flat.shape[0])).reshape(rows, LANE)


def _unpack(packed, shapes):
    flat = packed.reshape(-1)
    out, o = [], 0
    for shp in shapes:
        n = int(np.prod(shp))
        out.append(flat[o:o + n].reshape(shp))
        o += n
    return out


def _rows_for(shapes):
    n = sum(int(np.prod(s)) for s in shapes)
    return -(-n // (8 * LANE)) * 8


def _grads(x, target, small_w, shards):
    g1, qg, kg, rb, cw_shard, pw, ps, g2 = small_w
    g1 = g1.reshape(L, 1, D)
    g2 = g2.reshape(L, 1, D)
    qg2 = jnp.tile(qg, (1, 2)).reshape(L, 1, LANE)
    kg2 = jnp.tile(kg, (1, 2)).reshape(L, 1, LANE)
    ps3 = ps.reshape(L, 1, PWD)
    wbd = _pool_blockdiag(pw).astype(BF16)

    def gather(*kl):
        return [(("gather", k, l), shards[k]) for k, l in kl if l < L]

    full = {}

    def arrived(got, *kl):
        full.update(zip([x for x in kl if x[1] < L], got))

    bias, got = _bias_tables(_bias_vector(rb), gather((0, 0)) + [(("gather", "conv_w", 0), cw_shard)])
    arrived(got[:1], (0, 0))
    cshard = CW // NDEV
    cw_all = got[1].reshape(NDEV, -1)
    cw = jnp.concatenate([cw_all[d, :L * 3 * cshard].reshape(L, 3, cshard) for d in range(NDEV)], axis=2)
    saved = []
    h = x
    for l in range(L):
        kl = ((1, 0),) if l == 0 else ()
        (p, h_b, qkv), got = _in_proj(h, g1, full[0, l], qg2, kg2, l, gather(*kl))
        arrived(got, *kl)
        kl = ((2, 0), (3, 0)) if l == 0 else ((1, l), (3, l))
        (mix, lse, o32), got = _attn_fwd(qkv, bias, l, gather(*kl), relay_at=0.95 if l == 0 else 0.8)
        arrived(got, *kl)
        mix = _conv_pool_fwd(p, mix, cw, wbd, ps3, l)
        kl = ((0, l + 1), (2, l + 1))
        (xm, a, *out), got = _mlp_fwd(h, mix, full[1, l], g2, full[2, l], full[3, l], l, gather(*kl),
                                      target if l == L - 1 else None)
        arrived(got, *kl)
        saved.append((h, h_b, p, qkv, mix, lse, o32, xm, a))
        h = out[0]
    dx, sq = out

    grads = {}
    slots = {}

    def scatter(*kl):
        return [(("scatter", k), grads[k, l]) for k, l in kl if l < L]

    def left(got, *kl):
        slots.update(zip([x for x in kl if x[1] < L], got))

    per_layer = [None] * L
    for l in reversed(range(L)):
        x_in, h_b, p, qkv, mix, lse, o32, xm, a = saved[l]
        (dxm, dmix, f_b, da_b, h2_b, dxo_b, dxm_b, dg2), got = _mlp_bwd(
            dx, a, xm, g2, full[2, l], full[3, l], full[1, l], l, scatter((3, l + 1)))
        left(got, (3, l + 1))
        grads[1, l], _ = _wgrad(mix, dxm_b, f"w_out_l{l}")
        kl = ((1, 0),) if l == 0 else ()
        grads[2, l], got = _wgrad(da_b, h2_b, f"w_mlp1_l{l}", scatter(*kl))
        left(got, *kl)
        kl = ((2, 0),) if l == 0 else ()
        grads[3, l], got = _wgrad(f_b, dxo_b, f"w_mlp2_l{l}", scatter(*kl))
        left(got, *kl)
        kl = ((3, 0), (0, 1)) if l == 0 else ((2, l), (1, l))
        (dq, dk, dv, dbias, dqg, dkg), got = _attn_bwd(qkv, p, lse, o32, dmix, qg2, kg2, bias, l, scatter(*kl))
        left(got, *kl)
        dgb, dgc, dhin, du, dcw, dwbd, dps = _conv_pool_bwd(p, dmix, cw, wbd, ps3, l)
        (dx, dp_b, dg1), got = _in_proj_bwd(
            [dq, dk, dv, dgb, dgc, dhin, du], x_in, dxm, g1, full[0, l], l, scatter((0, l + 1)) if l else None)
        left(got, (0, l + 1))
        grads[0, l], _ = _wgrad(dp_b, h_b, f"w_in_l{l}")
        per_layer[l] = (dg1, dg2, dqg, dkg, _bias_tables_grad(dbias, l), dcw, dwbd, dps)
    (g_w1_t, g_w2), got = _sum_slots([slots[k, l] for k in (2, 3) for l in range(L)], scatter((0, 0)))
    left(got, (0, 0))
    sums = [_sum_slots([slots[k, l] for l in range(L)])[0][0] for k in (0, 1)] + [g_w1_t, g_w2]

    st = [jnp.stack([per_layer[l][k] for l in range(L)]) for k in range(8)]
    small = dict(
        g1=st[0].reshape(L, D), g2=st[1].reshape(L, D),
        qg=st[2].reshape(L, NH, HD).sum(1), kg=st[3].reshape(L, NH, HD).sum(1),
        rb=_bias_vector_grad(st[4].reshape(L, NH, NG)), cw=st[5], pw=_pool_blockdiag_grad(st[6]),
        ps=st[7].reshape(L, PWD))
    return sq, dx, sums, small


def kernel(x, norm1_g, w_in, q_norm_g, k_norm_g, rel_bias, conv_w, pool_w, pool_scale, w_out, norm2_g, w_mlp1, w_mlp2, loss_target, m_norm1_g, m_w_in, m_q_norm_g, m_k_norm_g, m_rel_bias, m_conv_w, m_pool_w, m_pool_scale, m_w_out, m_norm2_g, m_w_mlp1, m_w_mlp2, v_norm1_g, v_w_in, v_q_norm_g, v_k_norm_g, v_rel_bias, v_conv_w, v_pool_w, v_pool_scale, v_w_out, v_norm2_g, v_w_mlp1, v_w_mlp2):
    me = _index(_position())
    cshard = CW // NDEV

    shards = [jnp.swapaxes(w_in, 1, 2).astype(BF16), w_out.astype(BF16),
              jnp.swapaxes(w_mlp1, 1, 2).astype(BF16), w_mlp2.astype(BF16)]
    small_w = (norm1_g, q_norm_g, k_norm_g, rel_bias, _pack([conv_w], 8)[None], pool_w, pool_scale, norm2_g)
    sq, grad_x, (g_win_t, g_wout, g_w1_t, g_w2), small = _grads(x[0], loss_target[0], small_w, shards)
    g_w_in = jnp.swapaxes(g_win_t, 1, 2)
    g_w_mlp1 = jnp.swapaxes(g_w1_t, 1, 2)

    names = ("g1", "qg", "kg", "rb", "cw", "pw", "ps", "g2")
    gshapes = [(L, D), (L, HD), (L, HD), (L, NH, 2 * REL_CLIP + 1), (L, 3, CW), (L, 4, HD, HD), (L, PWD), (L, D)]
    garrs = [small[n] for n in names]
    rows = _rows_for(gshapes + [(1,)])
    packed = _pack(garrs + [sq[0, :1]], rows)[None]

    def big(w, g, m, v, xchg=None):
        shp = w.shape
        r = lambda a: a.reshape(-1, shp[-1])
        outs, got = _adamw(r(w), r(g), r(m), r(v), xchg)
        return [o.reshape(shp) for o in outs], got

    up_1, got = big(w_mlp1, g_w_mlp1, m_w_mlp1, v_w_mlp1, [(("gather", "small", 0), packed)])
    total = _sum_small(got[0])
    g_g1, g_qg, g_kg, g_rb, g_cw_full, g_pw, g_ps, g_g2, sq_sum = _unpack(total, gshapes + [(1,)])
    loss = (0.5 / D) * sq_sum[0]
    g_cw = lax.dynamic_slice_in_dim(g_cw_full, me * cshard, cshard, axis=2)
    up_in = big(w_in, g_w_in, m_w_in, v_w_in)[0]
    up_out = big(w_out, g_wout, m_w_out, v_w_out)[0]
    up_2 = big(w_mlp2, g_w2, m_w_mlp2, v_w_mlp2)[0]

    sw = [norm1_g, q_norm_g, k_norm_g, rel_bias, conv_w, pool_w, pool_scale, norm2_g]
    sg = [g_g1, g_qg, g_kg, g_rb, g_cw, g_pw, g_ps, g_g2]
    sm = [m_norm1_g, m_q_norm_g, m_k_norm_g, m_rel_bias, m_conv_w, m_pool_w, m_pool_scale, m_norm2_g]
    sv = [v_norm1_g, v_q_norm_g, v_k_norm_g, v_rel_bias, v_conv_w, v_pool_w, v_pool_scale, v_norm2_g]
    s_delta, s_m, s_v = _adamw_small(sw, sg, sm, sv)

    def order(small_list, in_, out_, m1, m2):
        g1_, qg_, kg_, rb_, cw_, pw_, ps_, g2_ = small_list
        return [g1_, in_, qg_, kg_, rb_, cw_, pw_, ps_, out_, g2_, m1, m2]

    grads = order(sg, g_w_in, g_wout, g_w_mlp1, g_w2)
    deltas = order(s_delta, up_in[0], up_out[0], up_1[0], up_2[0])
    new_m = order(s_m, up_in[1], up_out[1], up_1[1], up_2[1])
    new_v = order(s_v, up_in[2], up_out[2], up_1[2], up_2[2])
    return (loss, grad_x[None], *grads, *deltas, *new_m, *new_v)
```

```python
import numpy as np
import jax
import jax.numpy as jnp
from jax import lax
from jax.experimental import pallas as pl
from jax.experimental.pallas import tpu as pltpu

F32 = jnp.float32
BF16 = jnp.bfloat16
SDS = jax.ShapeDtypeStruct
MESH_ID = pl.DeviceIdType.MESH

D = 1024
L = 4
CHUNK = 64
N_PREV = 8
HD = 64
NH = 8
AW = 512
CW = 256
PWD = 256
DIN = 2560
DFF = 4096
EPS = 1e-6
NEG = -1e30
REL_CLIP = 128
POOL_WINDOWS = (2, 4, 8, 16)
LR, B1, B2, AEPS, WD, STEP = 0.001, 0.9, 0.999, 1e-08, 0.01, 10

NDEV = 8
LANE = 128
BAND = N_PREV * CHUNK
TQ = 256
WIN = TQ + BAND
NVAR = BAND // TQ + 1
NTOE = -(-(WIN + TQ - 1) // LANE) * LANE
NG = (NVAR - 1) * TQ + NTOE
PAD = 16
RB_NORM = 64
RB_SOFT = 16
VMEM_LIMIT = 56 * 1024 * 1024
SHARD_ROWS = (DIN // NDEV, D // NDEV, DFF // NDEV, DFF // NDEV)

assert 2 * HD == LANE and NH * HD == AW and POOL_WINDOWS == (2, 4, 8, 16)
assert TQ % CHUNK == 0 and BAND % TQ == 0 and max(POOL_WINDOWS) <= PAD and all(r % 16 == 0 for r in SHARD_ROWS)


def _cp(**kw):
    return pltpu.CompilerParams(vmem_limit_bytes=VMEM_LIMIT, **kw)


def _nn(a, b):
    return jnp.dot(a, b, preferred_element_type=F32)


def _nt(a, b):
    return lax.dot_general(a, b, (((1,), (1,)), ((), ())), preferred_element_type=F32)


def _tn(a, b):
    return lax.dot_general(a, b, (((0,), (0,)), ((), ())), preferred_element_type=F32)


def _const(shape):
    n = len(shape)
    return pl.BlockSpec(shape, lambda *_: (0,) * n, pipeline_mode=pl.Buffered(1))


def _layer(shape, l):
    n = len(shape)
    return pl.BlockSpec((None,) + tuple(shape), lambda *_: (l,) + (0,) * n, pipeline_mode=pl.Buffered(1))


def _lo_mask():
    return lax.broadcasted_iota(jnp.int32, (1, LANE), 1) < HD


def _half_sum(t, lo):
    s_lo = jnp.sum(jnp.where(lo, t, 0.0), axis=-1, keepdims=True)
    s_hi = jnp.sum(jnp.where(lo, 0.0, t), axis=-1, keepdims=True)
    return jnp.where(lo, s_lo, s_hi)


def _half_sum_mxu(t, lo):
    del lo
    ii = lax.broadcasted_iota(jnp.int32, (LANE, LANE), 0) // HD
    jj = lax.broadcasted_iota(jnp.int32, (LANE, LANE), 1) // HD
    ones = jnp.where(ii == jj, 1.0, 0.0).astype(BF16)
    hi = t.astype(BF16)
    return _nn(hi, ones) + _nn((t - hi.astype(F32)).astype(BF16), ones)


def _head_norm(x, lo, half_sum=_half_sum):
    r = lax.rsqrt(half_sum(x * x, lo) * (1.0 / HD) + EPS)
    return x * r, r


def _head_norm_bwd(dy, xn, r, g, lo, half_sum=_half_sum):
    dxn = dy * g
    mu = half_sum(dxn * xn, lo) * (1.0 / HD)
    return r * (dxn - xn * mu), dy * xn


def _rms_bwd(dy, x, g):
    r = lax.rsqrt(jnp.mean(x * x, axis=-1, keepdims=True) + EPS)
    xn = x * r
    dxn = dy * g
    mu = jnp.mean(dxn * xn, axis=-1, keepdims=True)
    return r * (dxn - xn * mu), dy * xn


def _in_proj(x, g1, win_t, qg2, kg2, l, xchg=None):
    s = x.shape[0]
    t = min(512, s)
    nblk = AW // LANE

    def body(x_ref, g_ref, w_ref, qg_ref, kg_ref, p_ref, h_ref, qkv_ref):
        xv = x_ref[...]
        r = lax.rsqrt(jnp.mean(xv * xv, axis=-1, keepdims=True) + EPS)
        h = (xv * r * g_ref[...]).astype(BF16)
        h_ref[...] = h
        p_ref[...] = _nt(h, w_ref[...])
        lo = _lo_mask()
        gains = (qg_ref[...] * (HD ** -0.5), kg_ref[...])
        for r0 in range(0, t, RB_NORM):
            rows = pl.ds(r0, RB_NORM)
            for c in range(3 * nblk):
                cols = pl.ds(c * LANE, LANE)
                v = p_ref[rows, cols]
                if c < 2 * nblk:
                    v = _head_norm(v, lo)[0] * gains[c // nblk]
                qkv_ref[rows, cols] = v.astype(BF16)

    row = lambda c: pl.BlockSpec((t, c), lambda i: (i, 0))
    return _hosted_call(
        body, xchg, name="in_proj", grid=(s // t,),
        in_specs=[row(D), _layer((1, D), l), _const((DIN, D)), _layer((1, LANE), l), _layer((1, LANE), l)],
        out_specs=[row(DIN), row(D), row(3 * AW)],
        out_shape=[SDS((s, DIN), F32), SDS((s, D), BF16), SDS((s, 3 * AW), BF16)], args=[x, g1, win_t, qg2, kg2])


def _bias_spec():
    return pl.BlockSpec((None, 2, TQ, WIN), lambda j, i: (jnp.maximum(NVAR - 1 - i, 0), j, 0, 0))


def _bias_layer_spec(l):
    return pl.BlockSpec((None, None, 2, TQ, WIN), lambda j, i: (l, jnp.maximum(NVAR - 1 - i, 0), j, 0, 0))


def _attn_fwd(qkv, bias, l, xchg=None, relay_at=0.8):
    s = qkv.shape[0]
    nq = s // TQ

    def body(q_ref, k_ref, v_ref, b_ref, o_ref, lse_ref, o32_ref, s_ref, p_ref, m_ref, den_ref, o0_ref):
        i = pl.program_id(1)
        ks = pl.multiple_of(jnp.maximum(i * TQ - BAND, 0), TQ)
        lo = _lo_mask()
        q = q_ref[...]
        kwin = k_ref[pl.ds(ks, WIN), :]
        vwin = v_ref[pl.ds(ks, WIN), :]
        for half in range(2):
            m_ = lo if half == 0 else jnp.logical_not(lo)
            s_ref[half] = _nt(jnp.where(m_, q, jnp.zeros_like(q)), kwin)
            for r0 in range(0, TQ, RB_SOFT):
                rows = pl.ds(r0, RB_SOFT)
                mx = jnp.max(s_ref[half, rows, :] + b_ref[half, rows, :], axis=-1, keepdims=True)
                m_ref[rows, :] = jnp.broadcast_to(mx, (RB_SOFT, LANE))
            for r0 in range(0, TQ, RB_SOFT):
                rows = pl.ds(r0, RB_SOFT)
                mx = m_ref[rows, 0:1]
                e = jnp.exp(s_ref[half, rows, :] + b_ref[half, rows, :] - mx)
                p_ref[half, rows, :] = e.astype(BF16)
                den = jnp.sum(e, axis=-1, keepdims=True)
                den_ref[rows, :] = jnp.broadcast_to(den, (RB_SOFT, LANE))
                lse = mx + jnp.log(den)
                if half == 0:
                    lse_ref[rows, :] = jnp.broadcast_to(lse, (RB_SOFT, LANE))
                else:
                    lse_ref[rows, :] = jnp.where(lo, lse_ref[rows, :], lse)
            o = _nn(p_ref[half], vwin) * (1.0 / den_ref[...])
            if half == 0:
                o0_ref[...] = o
            else:
                o = jnp.where(lo, o0_ref[...], o)
                o32_ref[...] = o
                o_ref[...] = o.astype(BF16)

    tile = pl.BlockSpec((TQ, LANE), lambda j, i: (i, j))
    stat = pltpu.VMEM((TQ, LANE), F32)
    return _hosted_call(
        body, xchg, name="attn_fwd", grid=(NH // 2, nq),
        in_specs=[
            tile,
            pl.BlockSpec((s, LANE), lambda j, i: (0, AW // LANE + j)),
            pl.BlockSpec((s, LANE), lambda j, i: (0, 2 * AW // LANE + j)),
            _bias_layer_spec(l),
        ],
        out_specs=[tile, tile, tile],
        out_shape=[SDS((s, D), BF16), SDS((s, AW), F32), SDS((s, AW), F32)], args=[qkv, qkv, qkv, bias],
        scratch_shapes=[pltpu.VMEM((2, TQ, WIN), F32), pltpu.VMEM((2, TQ, WIN), BF16), stat, stat, stat],
        relay_at=relay_at)


_C0 = 3 * AW // LANE


def _cp_in_specs(s, l):
    blk = lambda f: pl.BlockSpec((s, LANE), f)
    return [
        blk(lambda g: (0, _C0 + jnp.minimum(g, 1))),
        blk(lambda g: (0, _C0 + 2 + jnp.minimum(g, 1))),
        blk(lambda g: (0, _C0 + 4 + jnp.minimum(g, 1))),
        blk(lambda g: (0, _C0 + 6 + jnp.maximum(g - 2, 0))),
        pl.BlockSpec((None, 3, LANE), lambda g: (l, 0, jnp.minimum(g, 1))),
        pl.BlockSpec((None, None, LANE, LANE), lambda g: (l, jnp.maximum(g - 2, 0), 0, 0)),
        pl.BlockSpec((None, 1, LANE), lambda g: (l, 0, jnp.maximum(g - 2, 0))),
    ]


def _pool_window_sums(u_ref, buf_a, buf_b, jj, s, rt):
    nrt = s // rt
    for t in range(nrt):
        buf_a[pl.ds(PAD + t * rt, rt), :] = u_ref[pl.ds(t * rt, rt), :]

    def stage(src, dst, sh):
        for t in range(nrt):
            r0 = PAD + t * rt
            dst[pl.ds(r0, rt), :] = src[pl.ds(r0, rt), :] + src[pl.ds(r0 - sh, rt), :]

    stage(buf_a, buf_b, 1)
    stage(buf_b, buf_a, 2)

    @pl.when(jj == 1)
    def _():
        stage(buf_a, buf_b, 4)
        stage(buf_b, buf_a, 8)


def _pool_counts(jj, lo, r0, rt):
    w0, w1, w2, w3 = [float(w) for w in POOL_WINDOWS]
    w = jnp.where(lo, jnp.where(jj == 0, w0, w2), jnp.where(jj == 0, w1, w3))
    pos1 = (lax.broadcasted_iota(jnp.int32, (rt, LANE), 0) + (r0 + 1)).astype(F32)
    return jnp.minimum(pos1, w)


def _conv_pool_fwd(p, mix, conv_w, wbd, pscale, l):
    s = p.shape[0]
    rt = min(256, s)
    nrt = s // rt

    def body(gb_ref, gc_ref, hin_ref, u_ref, cw_ref, wbd_ref, ps_ref, mix_in, o_ref, buf_a, buf_b):
        del mix_in
        g = pl.program_id(0)
        zpad = jnp.zeros((PAD, LANE), F32)
        buf_a[pl.ds(0, PAD), :] = zpad
        buf_b[pl.ds(0, PAD), :] = zpad

        @pl.when(g < 2)
        def _conv():
            for t in range(nrt):
                buf_a[pl.ds(PAD + t * rt, rt), :] = gc_ref[pl.ds(t * rt, rt), :] * hin_ref[pl.ds(t * rt, rt), :]
            w0, w1, w2 = cw_ref[0:1, :], cw_ref[1:2, :], cw_ref[2:3, :]
            for t in range(nrt):
                r0 = PAD + t * rt
                y = w0 * buf_a[pl.ds(r0 - 2, rt), :] + w1 * buf_a[pl.ds(r0 - 1, rt), :] + w2 * buf_a[pl.ds(r0, rt), :]
                o_ref[pl.ds(t * rt, rt), :] = (gb_ref[pl.ds(t * rt, rt), :] * y).astype(BF16)

        @pl.when(g >= 2)
        def _pool():
            jj = g - 2
            lo = _lo_mask()
            _pool_window_sums(u_ref, buf_a, buf_b, jj, s, rt)
            wb = wbd_ref[...]
            for t in range(nrt):
                r0 = PAD + t * rt
                wsum = jnp.where(lo, buf_b[pl.ds(r0, rt), :], buf_a[pl.ds(r0, rt), :])
                m = wsum / _pool_counts(jj, lo, t * rt, rt) - u_ref[pl.ds(t * rt, rt), :]
                o_ref[pl.ds(t * rt, rt), :] = (_nn(m.astype(BF16), wb) * ps_ref[...]).astype(BF16)

    return pl.pallas_call(
        body, name="conv_pool_fwd", grid=(4,),
        in_specs=_cp_in_specs(s, l) + [pl.BlockSpec(memory_space=pl.ANY)],
        out_specs=pl.BlockSpec((s, LANE), lambda g: (0, AW // LANE + g)),
        out_shape=SDS((s, D), BF16),
        scratch_shapes=[pltpu.VMEM((s + 2 * PAD, LANE), F32), pltpu.VMEM((s + 2 * PAD, LANE), F32)],
        input_output_aliases={7: 0},
        compiler_params=_cp(),
    )(p, p, p, p, conv_w, wbd, pscale, mix)


def _mlp_fwd(x, mix, wout, g2, w1_t, w2, l, xchg=None, target=None):
    s = x.shape[0]
    t = min(256, s)

    def body(*refs):
        x_ref, mix_ref, wo_ref, g_ref, w1_ref, w2_ref = refs[:6]
        xm_ref, a_ref, xo_ref = refs[-4:-1] if target is not None else refs[-3:]
        xm = x_ref[...] + _nn(mix_ref[...], wo_ref[...])
        xm_ref[...] = xm
        r = lax.rsqrt(jnp.mean(xm * xm, axis=-1, keepdims=True) + EPS)
        h2 = (xm * r * g_ref[...]).astype(BF16)
        a = _nt(h2, w1_ref[...])
        a_ref[...] = a.astype(BF16)
        f = jnp.square(jnp.maximum(a, 0.0)).astype(BF16)
        xo = xm + _nn(f, w2_ref[...])
        if target is None:
            xo_ref[...] = xo
        else:
            acc_ref = refs[-1]

            @pl.when(pl.program_id(0) == 0)
            def _():
                acc_ref[...] = jnp.zeros_like(acc_ref)
            e = xo - refs[6][...]
            xo_ref[...] = e * (1.0 / D)
            acc_ref[...] += jnp.sum(e * e)

    row = lambda c: pl.BlockSpec((t, c), lambda i: (i, 0))
    last = target is not None
    return _hosted_call(
        body, xchg, name="mlp_fwd_loss" if last else "mlp_fwd", grid=(s // t,),
        in_specs=[row(D), row(D), _const((D, D)), _layer((1, D), l), _const((DFF, D)), _const((DFF, D))] + [row(D)] * last,
        out_specs=[row(D), row(DFF), row(D)] + [pl.BlockSpec((8, LANE), lambda i: (0, 0))] * last,
        out_shape=[SDS((s, D), F32), SDS((s, DFF), BF16), SDS((s, D), F32)] + [SDS((8, LANE), F32)] * last,
        args=[x, mix, wout, g2, w1_t, w2] + [target] * last, relay_at=0.85)


def _mlp_bwd(dxo, a, xm, g2, w1_t, w2, wout, l, xchg=None):
    s = dxo.shape[0]
    t = min(256, s)

    def body(dxo_ref, a_ref, xm_ref, g_ref, w1_ref, w2_ref, wo_ref,
             dxm_ref, dmix_ref, f_ref, da_ref, h2_ref, dxob_ref, dxmb_ref, dg_ref):
        @pl.when(pl.program_id(0) == 0)
        def _():
            dg_ref[...] = jnp.zeros_like(dg_ref)
        dxo = dxo_ref[...]
        dxob = dxo.astype(BF16)
        dxob_ref[...] = dxob
        ra = jnp.maximum(a_ref[...].astype(F32), 0.0)
        f_ref[...] = jnp.square(ra).astype(BF16)
        dab = (_nt(dxob, w2_ref[...]) * (2.0 * ra)).astype(BF16)
        da_ref[...] = dab
        dh2 = _nn(dab, w1_ref[...])
        xm = xm_ref[...]
        g = g_ref[...]
        r = lax.rsqrt(jnp.mean(xm * xm, axis=-1, keepdims=True) + EPS)
        h2_ref[...] = (xm * r * g).astype(BF16)
        dx_n, dgr = _rms_bwd(dh2, xm, g)
        dg_ref[...] += jnp.sum(dgr, axis=0, keepdims=True)
        dxm = dxo + dx_n
        dxm_ref[...] = dxm
        dxmb = dxm.astype(BF16)
        dxmb_ref[...] = dxmb
        dmix_ref[...] = _nt(dxmb, wo_ref[...])

    row = lambda c: pl.BlockSpec((t, c), lambda i: (i, 0))
    return _hosted_call(
        body, xchg, name="mlp_bwd", grid=(s // t,),
        in_specs=[row(D), row(DFF), row(D), _layer((1, D), l), _const((DFF, D)), _const((DFF, D)), _const((D, D))],
        out_specs=[row(D), row(D), row(DFF), row(DFF), row(D), row(D), row(D), pl.BlockSpec((1, D), lambda i: (0, 0))],
        out_shape=[SDS((s, D), F32), SDS((s, D), F32), SDS((s, DFF), BF16), SDS((s, DFF), BF16),
                   SDS((s, D), BF16), SDS((s, D), BF16), SDS((s, D), BF16), SDS((1, D), F32)],
        args=[dxo, a, xm, g2, w1_t, w2, wout])


def _attn_bwd(qkv, p, lse, o32, dmix, qg2, kg2, bias, l, xchg=None):
    s = p.shape[0]
    nq = s // TQ
    scale = HD ** -0.5

    def body(qs_ref, kb_ref, vb_ref, q_ref, k_ref, qg_ref, kg_ref, b_ref, lse_ref, o_ref, do_ref,
             dq_ref, dk_ref, dv_ref, db_ref, dqg_ref, dkg_ref,
             dk_acc, dv_acc, s_ref, dp_ref, ds_ref, pb_ref, dqn_ref, dl_ref):
        i = pl.program_id(1)
        kt = jnp.maximum(i - BAND // TQ, 0)
        ks = pl.multiple_of(kt * TQ, TQ)
        lo = _lo_mask()

        @pl.when(i == 0)
        def _():
            dk_acc[...] = jnp.zeros_like(dk_acc)
            dv_acc[...] = jnp.zeros_like(dv_acc)
            dqg_ref[...] = jnp.zeros_like(dqg_ref)
            dkg_ref[...] = jnp.zeros_like(dkg_ref)

        @pl.when(i < NVAR)
        def _():
            db_ref[...] = jnp.zeros_like(db_ref)

        qs = qs_ref[...]
        kwin = kb_ref[pl.ds(ks, WIN), :]
        vwin = vb_ref[pl.ds(ks, WIN), :]
        do = do_ref[...]
        dob = do.astype(BF16)
        dl_ref[...] = _half_sum(do * o_ref[...], lo)
        for half in range(2):
            m_ = lo if half == 0 else jnp.logical_not(lo)
            qa = jnp.where(m_, qs, jnp.zeros_like(qs))
            doa = jnp.where(m_, dob, jnp.zeros_like(dob))
            s_ref[half] = _nt(qa, kwin)
            dp_ref[half] = _nt(doa, vwin)
            for r0 in range(0, TQ, RB_SOFT):
                rows = pl.ds(r0, RB_SOFT)
                lse_h = lse_ref[rows, half * HD:half * HD + 1]
                pm = jnp.exp(s_ref[half, rows, :] + b_ref[half, rows, :] - lse_h)
                ds = pm * (dp_ref[half, rows, :] - dl_ref[rows, half * HD:half * HD + 1])
                db_ref[half, rows, :] += ds
                ds_ref[half, rows, :] = ds.astype(BF16)
                pb_ref[half, rows, :] = pm.astype(BF16)
            dsb = ds_ref[half]
            dq_h = _nn(dsb, kwin)
            if half == 0:
                dqn_ref[...] = dq_h
            else:
                dqn_ref[...] = jnp.where(lo, dqn_ref[...], dq_h)
            dk_t = _tn(qa, dsb)
            dv_t = _tn(doa, pb_ref[half])
            for t in range(WIN // TQ):
                dk_acc[kt + t] += dk_t[:, t * TQ:(t + 1) * TQ]
                dv_acc[kt + t] += dv_t[:, t * TQ:(t + 1) * TQ]
        qg, kg = qg_ref[...], kg_ref[...]
        xq, rq = _head_norm(q_ref[...], lo)
        dq, dqg_rows = _head_norm_bwd(dqn_ref[...] * scale, xq, rq, qg, lo)
        dq_ref[...] = dq.astype(BF16)
        dqg_ref[...] += jnp.sum(dqg_rows, axis=0, keepdims=True)

        @pl.when(i == nq - 1)
        def _():
            dkg = jnp.zeros((1, LANE), F32)
            for t in range(nq):
                rows = pl.ds(t * TQ, TQ)
                xk, rk = _head_norm(k_ref[rows, :], lo, _half_sum_mxu)
                dk, dkg_rows = _head_norm_bwd(dk_acc[t].T, xk, rk, kg, lo, _half_sum_mxu)
                dk_ref[rows, :] = dk.astype(BF16)
                dv_ref[rows, :] = dv_acc[t].T.astype(BF16)
                dkg = dkg + jnp.sum(dkg_rows, axis=0, keepdims=True)
            dkg_ref[...] = dkg

    tile = pl.BlockSpec((TQ, LANE), lambda j, i: (i, j))
    kcol = lambda c0: pl.BlockSpec((s, LANE), lambda j, i: (0, c0 + j))
    gain = pl.BlockSpec((None, 1, LANE), lambda j, i: (j, 0, 0))
    return _hosted_call(
        body, xchg, name="attn_bwd", grid=(NH // 2, nq),
        in_specs=[
            tile, kcol(AW // LANE), kcol(2 * AW // LANE), tile, kcol(AW // LANE),
            _layer((1, LANE), l), _layer((1, LANE), l),
            _bias_layer_spec(l), tile, tile, tile,
        ],
        out_specs=[tile, kcol(0), kcol(0), _bias_spec(), gain, gain],
        out_shape=[SDS((s, AW), BF16), SDS((s, AW), BF16), SDS((s, AW), BF16),
                   SDS((NVAR, NH, TQ, WIN), F32), SDS((NH // 2, 1, LANE), F32), SDS((NH // 2, 1, LANE), F32)],
        scratch_shapes=[pltpu.VMEM((nq, LANE, TQ), F32), pltpu.VMEM((nq, LANE, TQ), F32),
                        pltpu.VMEM((2, TQ, WIN), F32), pltpu.VMEM((2, TQ, WIN), F32),
                        pltpu.VMEM((2, TQ, WIN), BF16), pltpu.VMEM((2, TQ, WIN), BF16),
                        pltpu.VMEM((TQ, LANE), F32), pltpu.VMEM((TQ, LANE), F32)],
        args=[qkv, qkv, qkv, p, p, qg2, kg2, bias, lse, o32, dmix])


def _conv_pool_bwd(p, dmix, conv_w, wbd, pscale, l):
    s = p.shape[0]
    rt = min(256, s)
    nrt = s // rt

    def body(gb_ref, gc_ref, hin_ref, u_ref, cw_ref, wbd_ref, ps_ref, dy_ref,
             dgb_ref, dgc_ref, dhin_ref, du_ref, dcw_ref, dwbd_ref, dps_ref, buf_a, buf_b, buf_c, buf_d):
        g = pl.program_id(0)
        zpad = jnp.zeros((PAD, LANE), F32)
        for buf in (buf_a, buf_b, buf_c):
            buf[pl.ds(0, PAD), :] = zpad
            buf[pl.ds(PAD + s, PAD), :] = zpad

        @pl.when(g < 2)
        def _conv():
            for t in range(nrt):
                rows = pl.ds(t * rt, rt)
                buf_a[pl.ds(PAD + t * rt, rt), :] = gc_ref[rows, :] * hin_ref[rows, :]
                buf_b[pl.ds(PAD + t * rt, rt), :] = dy_ref[rows, :] * gb_ref[rows, :]
            w0, w1, w2 = cw_ref[0:1, :], cw_ref[1:2, :], cw_ref[2:3, :]
            d0 = jnp.zeros((1, LANE), F32)
            d1 = jnp.zeros((1, LANE), F32)
            d2 = jnp.zeros((1, LANE), F32)
            for t in range(nrt):
                rows = pl.ds(t * rt, rt)
                r0 = PAD + t * rt
                z2, z1, z0 = buf_a[pl.ds(r0 - 2, rt), :], buf_a[pl.ds(r0 - 1, rt), :], buf_a[pl.ds(r0, rt), :]
                y = w0 * z2 + w1 * z1 + w2 * z0
                dgb_ref[rows, :] = (dy_ref[rows, :] * y).astype(BF16)
                e0 = buf_b[pl.ds(r0, rt), :]
                d0 = d0 + jnp.sum(e0 * z2, axis=0, keepdims=True)
                d1 = d1 + jnp.sum(e0 * z1, axis=0, keepdims=True)
                d2 = d2 + jnp.sum(e0 * z0, axis=0, keepdims=True)
                dz = w2 * e0 + w1 * buf_b[pl.ds(r0 + 1, rt), :] + w0 * buf_b[pl.ds(r0 + 2, rt), :]
                dgc_ref[rows, :] = (dz * hin_ref[rows, :]).astype(BF16)
                dhin_ref[rows, :] = (dz * gc_ref[rows, :]).astype(BF16)
            dcw_ref[0:1, :] = d0
            dcw_ref[1:2, :] = d1
            dcw_ref[2:3, :] = d2

        @pl.when(g >= 2)
        def _pool():
            jj = g - 2
            lo = _lo_mask()
            _pool_window_sums(u_ref, buf_a, buf_b, jj, s, rt)
            wb = wbd_ref[...]
            ps = ps_ref[...]
            dps = jnp.zeros((1, LANE), F32)
            dwb = jnp.zeros((LANE, LANE), F32)
            for t in range(nrt):
                rows = pl.ds(t * rt, rt)
                r0 = PAD + t * rt
                cnt = _pool_counts(jj, lo, t * rt, rt)
                wsum = jnp.where(lo, buf_b[pl.ds(r0, rt), :], buf_a[pl.ds(r0, rt), :])
                mb = (wsum / cnt - u_ref[rows, :]).astype(BF16)
                dy = dy_ref[rows, :]
                dps = dps + jnp.sum(dy * _nn(mb, wb), axis=0, keepdims=True)
                dmp = (dy * ps).astype(BF16)
                dwb = dwb + _tn(mb, dmp)
                dm = _nt(dmp, wb)
                buf_d[rows, :] = dm
                buf_c[pl.ds(r0, rt), :] = dm / cnt
            dps_ref[...] = dps
            dwbd_ref[...] = dwb

            def stage(src, dst, sh):
                for t in range(nrt):
                    r0 = PAD + t * rt
                    dst[pl.ds(r0, rt), :] = src[pl.ds(r0, rt), :] + src[pl.ds(r0 + sh, rt), :]

            def finish(first, second):
                for t in range(nrt):
                    rows = pl.ds(t * rt, rt)
                    r0 = PAD + t * rt
                    fw = jnp.where(lo, first[pl.ds(r0, rt), :], second[pl.ds(r0, rt), :])
                    du_ref[rows, :] = (fw - buf_d[rows, :]).astype(BF16)

            stage(buf_c, buf_a, 1)
            stage(buf_a, buf_b, 2)

            @pl.when(jj == 0)
            def _():
                finish(buf_a, buf_b)

            @pl.when(jj == 1)
            def _():
                stage(buf_b, buf_c, 4)
                stage(buf_c, buf_a, 8)
                finish(buf_c, buf_a)

    cblk = pl.BlockSpec((s, LANE), lambda g: (0, jnp.minimum(g, 1)))
    pblk = pl.BlockSpec((s, LANE), lambda g: (0, jnp.maximum(g - 2, 0)))
    padded = pltpu.VMEM((s + 2 * PAD, LANE), F32)
    return pl.pallas_call(
        body, name="conv_pool_bwd", grid=(4,),
        in_specs=_cp_in_specs(s, l) + [pl.BlockSpec((s, LANE), lambda g: (0, AW // LANE + g))],
        out_specs=[cblk, cblk, cblk, pblk,
                   pl.BlockSpec((3, LANE), lambda g: (0, jnp.minimum(g, 1))),
                   pl.BlockSpec((None, LANE, LANE), lambda g: (jnp.maximum(g - 2, 0), 0, 0)),
                   pl.BlockSpec((1, LANE), lambda g: (0, jnp.maximum(g - 2, 0)))],
        out_shape=[SDS((s, CW), BF16), SDS((s, CW), BF16), SDS((s, CW), BF16), SDS((s, PWD), BF16),
                   SDS((3, CW), F32), SDS((2, LANE, LANE), F32), SDS((1, PWD), F32)],
        scratch_shapes=[padded, padded, padded, pltpu.VMEM((s, LANE), F32)],
        compiler_params=_cp(),
    )(p, p, p, p, conv_w, wbd, pscale, dmix)


def _in_proj_bwd(parts, x, dxm, g1, win_t, l, xchg=None):
    s = x.shape[0]
    t = min(256, s)
    widths = [a.shape[1] for a in parts]
    offs = [int(o) for o in np.cumsum([0] + widths[:-1])]
    n = len(parts)

    def body(*refs):
        part_refs = refs[:n]
        x_ref, dxm_ref, g_ref, w_ref, dx_ref, dp_ref, dg_ref = refs[n:]

        @pl.when(pl.program_id(0) == 0)
        def _():
            dg_ref[...] = jnp.zeros_like(dg_ref)
        for r, o, w in zip(part_refs, offs, widths):
            dp_ref[:, o:o + w] = r[...]
        dh = _nn(dp_ref[...], w_ref[...])
        dx_n, dgr = _rms_bwd(dh, x_ref[...], g_ref[...])
        dg_ref[...] += jnp.sum(dgr, axis=0, keepdims=True)
        dx_ref[...] = dxm_ref[...] + dx_n

    row = lambda c: pl.BlockSpec((t, c), lambda i: (i, 0))
    return _hosted_call(
        body, xchg, name="in_proj_bwd", grid=(s // t,),
        in_specs=[row(w) for w in widths] + [row(D), row(D), _layer((1, D), l), _const((DIN, D))],
        out_specs=[row(D), row(DIN), pl.BlockSpec((1, D), lambda i: (0, 0))],
        out_shape=[SDS((s, D), F32), SDS((s, DIN), BF16), SDS((1, D), F32)], args=[*parts, x, dxm, g1, win_t])


def _wgrad(a, b, tag, xchg=None):
    s, m = a.shape
    mb = 512

    def body(a_ref, b_ref, o_ref):
        o_ref[...] = _tn(a_ref[...], b_ref[...]).astype(BF16)

    (out,), got = _hosted_call(
        body, xchg, name=f"wgrad_{tag}", grid=(m // mb,),
        in_specs=[pl.BlockSpec((s, mb), lambda mi: (0, mi)), _const((s, D))],
        out_specs=[pl.BlockSpec((mb, D), lambda mi: (mi, 0))],
        out_shape=[SDS((m, D), BF16)], args=[a, b])
    return out, got


def _bias_tables(gvec, xchg=None):
    def body(g_ref, o_ref):
        qc = lax.broadcasted_iota(jnp.int32, (TQ, WIN), 0) // CHUNK
        kc = lax.broadcasted_iota(jnp.int32, (TQ, WIN), 1) // CHUNK
        for var in range(NVAR):
            vec = jnp.broadcast_to(g_ref[:, var * TQ:var * TQ + NTOE], (TQ, NTOE))
            toe = pltpu.roll(vec, NTOE - TQ + 1, 1, stride=1, stride_axis=0)[:, :WIN]
            rel = (BAND - var * TQ) // CHUNK + qc - kc
            o_ref[var] = jnp.where((rel >= 0) & (rel <= N_PREV), toe, NEG)

    (out,), got = _hosted_call(
        body, xchg, name="bias_tables", grid=(L, NH),
        in_specs=[pl.BlockSpec((None, None, 1, NG), lambda l, h: (l, h, 0, 0))],
        out_specs=[pl.BlockSpec((None, NVAR, None, TQ, WIN), lambda l, h: (l, 0, h, 0, 0))],
        out_shape=[SDS((L, NVAR, NH, TQ, WIN), F32)], args=[gvec], relay_at=0.95)
    return out, got


def _bias_tables_grad(dbias, l):
    nb = NTOE // LANE
    wb = WIN // LANE

    def body(d_ref, o_ref):
        ii = lax.broadcasted_iota(jnp.int32, (LANE, LANE), 0)
        jj = lax.broadcasted_iota(jnp.int32, (LANE, LANE), 1)
        flip = jnp.where(ii + jj == LANE - 1, 1.0, 0.0).astype(BF16)
        o_ref[...] = jnp.zeros_like(o_ref)
        for var in range(NVAR):
            blocks = []
            for b in range(nb):
                src = nb - 1 - b
                if src >= wb:
                    blocks.append(jnp.zeros((TQ, LANE), F32))
                    continue
                xv = d_ref[var, :, src * LANE:(src + 1) * LANE]
                hi = xv.astype(BF16)
                lo = (xv - hi.astype(F32)).astype(BF16)
                blocks.append(_nn(hi, flip) + _nn(lo, flip))
            rev = jnp.concatenate(blocks, axis=1)
            skew = pltpu.roll(rev, NTOE - TQ + 1, 1, stride=1, stride_axis=0)
            off = NG - NTOE - var * TQ
            o_ref[:, off:off + NTOE] += jnp.sum(skew, axis=0, keepdims=True)

    return pl.pallas_call(
        body, name=f"bias_tables_grad_l{l}", grid=(NH,),
        in_specs=[pl.BlockSpec((NVAR, None, TQ, WIN), lambda h: (0, h, 0, 0))],
        out_specs=pl.BlockSpec((None, 1, NG), lambda h: (h, 0, 0)),
        out_shape=SDS((NH, 1, NG), F32),
        compiler_params=_cp(),
    )(dbias)


_SIBLING = (0, 0, 1)
_CHIPS = [(1, 0, 0), (0, 1, 0), (1, 1, 0)]
_MASKS = [_SIBLING] + _CHIPS + [(1, 0, 1), (0, 1, 1), (1, 1, 1)]


def _position():
    return lax.axis_index("x"), lax.axis_index("y"), lax.axis_index("c")


def _peer(pos, mask):
    return tuple(1 - a if f else a for a, f in zip(pos, mask))


def _index(pos):
    return 4 * pos[0] + 2 * pos[1] + pos[2]


def _exchange_phases(items, src, dst, sems):
    send_sems, recv_sems, local_sems = sems
    me = _position()
    sib = _peer(me, _SIBLING)

    def remote(s_ref, d_ref, pi, n, to):
        return pltpu.make_async_remote_copy(
            src_ref=s_ref, dst_ref=d_ref, send_sem=send_sems.at[pi, n], recv_sem=recv_sems.at[pi, n],
            device_id=to, device_id_type=MESH_ID)

    def parts(n):
        it = items[n]
        r = src[n].shape[1] if it[0] == "gather" else src[n].shape[0] // NDEV
        block = lambda ref, pos: ref.at[pl.ds(_index(pos) * r, r), :]
        if it[0] == "gather":
            own = src[n].at[it[2]]
            local = pltpu.make_async_copy(own, block(dst[n], me), local_sems.at[n])
            sends = [remote(own, block(dst[n], me), pi, n, _peer(me, m)) for pi, m in enumerate([_SIBLING] + _CHIPS)]
            hops = [(remote(block(dst[n], _peer(me, m)), block(dst[n], _peer(me, m)), 1 + j, n, _peer(me, m)),
                     remote(block(dst[n], _peer(me, m)), block(dst[n], _peer(me, m)), 4 + j, n, sib))
                    for j, m in enumerate(_CHIPS)]
            lands = [remote(own, block(dst[n], sib), 0, n, sib)]
            lands += [remote(own, block(dst[n], _peer(sib, m)), 4 + j, n, sib) for j, m in enumerate(_CHIPS)]
        else:
            local = pltpu.make_async_copy(block(src[n], me), dst[n].at[_index(me)], local_sems.at[n])
            sends = [remote(block(src[n], _peer(me, m)), dst[n].at[_index(me)], pi, n, _peer(me, m))
                     for pi, m in enumerate(_MASKS)]
            hops = []
            lands = [remote(block(src[n], me), dst[n].at[_index(_peer(me, m))], pi, n, _peer(me, m))
                     for pi, m in enumerate(_MASKS)]
        return local, sends, hops, lands

    def start():
        for n in range(len(items)):
            local, sends, _, _ = parts(n)
            local.start()
            for cp in sends:
                cp.start()

    def relay():
        for n in range(len(items)):
            for arrived, onward in parts(n)[2]:
                arrived.wait_recv()
                onward.start()

    def finish():
        for n in range(len(items)):
            local, sends, hops, lands = parts(n)
            for cp in lands:
                cp.wait_recv()
            for cp in sends + [onward for _, onward in hops]:
                cp.wait_send()
            local.wait()

    return start, relay, finish


def _hosted_call(body, xchg, *, name, grid, in_specs, out_specs, out_shape, args, scratch_shapes=(), relay_at=0.8):
    if not xchg:
        outs = pl.pallas_call(
            body, name=name, grid=grid, in_specs=list(in_specs), out_specs=list(out_specs),
            out_shape=list(out_shape), scratch_shapes=list(scratch_shapes), compiler_params=_cp())(*args)
        return outs, []
    items = [it for it, _ in xchg]
    n_in, n_out, n_scr, nit = len(args), len(out_shape), len(scratch_shapes), len(items)
    hbm = pl.BlockSpec(memory_space=pl.ANY)
    steps = int(np.prod(grid))
    relay_step = min(int(relay_at * steps), steps - 1)

    def dst_shape(it, a):
        if it[0] == "gather":
            return SDS((NDEV * a.shape[1], a.shape[2]), a.dtype)
        return SDS((NDEV, a.shape[0] // NDEV, a.shape[1]), a.dtype)

    def wrapped(*refs):
        ins = refs[:n_in]
        src = refs[n_in:n_in + nit]
        outs = refs[n_in + nit:n_in + nit + n_out]
        dst = refs[n_in + nit + n_out:n_in + 2 * nit + n_out]
        scratch = refs[n_in + 2 * nit + n_out:n_in + 2 * nit + n_out + n_scr]
        start, relay, finish = _exchange_phases(items, src, dst, refs[n_in + 2 * nit + n_out + n_scr:])
        step = 0
        for d, g in enumerate(grid):
            step = step * g + pl.program_id(d)
        pl.when(step == 0)(start)
        body(*ins, *outs, *scratch)
        pl.when(step == relay_step)(relay)
        pl.when(step == steps - 1)(finish)

    npeer = len(_MASKS)
    res = pl.pallas_call(
        wrapped, name=name, grid=grid,
        in_specs=list(in_specs) + [hbm] * nit,
        out_specs=list(out_specs) + [hbm] * nit,
        out_shape=list(out_shape) + [dst_shape(it, a) for it, a in xchg],
        scratch_shapes=list(scratch_shapes) + [
            pltpu.SemaphoreType.DMA((npeer, nit)), pltpu.SemaphoreType.DMA((npeer, nit)), pltpu.SemaphoreType.DMA((nit,))],
        compiler_params=_cp(),
    )(*args, *[a for _, a in xchg])
    return list(res[:n_out]), list(res[n_out:])


def _sum_slots(slots, xchg=None):
    _, r, _ = slots[0].shape
    n = len(slots) // L
    rt = 64

    def body(*refs):
        for k in range(n):
            for l in range(L):
                src = refs[k * L + l]
                acc = src[0].astype(F32)
                for d in range(1, NDEV):
                    acc = acc + src[d].astype(F32)
                refs[n * L + k][l] = acc

    return _hosted_call(
        body, xchg, name=f"sum_slots_r{r}" + ("_x" if xchg else ""), grid=(r // rt,),
        in_specs=[pl.BlockSpec((NDEV, rt, D), lambda i: (0, i, 0))] * (n * L),
        out_specs=[pl.BlockSpec((L, rt, D), lambda i: (0, i, 0))] * n,
        out_shape=[SDS((L, r, D), F32)] * n, args=list(slots))


def _sum_small(slots):
    rows = slots.shape[0] // NDEV

    def body(in_ref, o_ref):
        acc = in_ref[pl.ds(0, rows), :]
        for d in range(1, NDEV):
            acc = acc + in_ref[pl.ds(d * rows, rows), :]
        o_ref[...] = acc

    vm = pl.BlockSpec(memory_space=pltpu.VMEM)
    return pl.pallas_call(
        body, name="sum_small", in_specs=[vm], out_specs=vm, out_shape=SDS((rows, LANE), F32),
        compiler_params=_cp())(slots)


def _adamw_update(w_ref, g_ref, m_ref, v_ref, d_ref, nm_ref, nv_ref):
    gv = g_ref[...]
    mn = B1 * m_ref[...] + (1.0 - B1) * gv
    vn = B2 * v_ref[...] + (1.0 - B2) * jnp.square(gv)
    nm_ref[...] = mn
    nv_ref[...] = vn
    m_hat = mn / (1.0 - B1 ** STEP)
    v_hat = vn / (1.0 - B2 ** STEP)
    d_ref[...] = -LR * (m_hat / (jnp.sqrt(v_hat) + AEPS) + WD * w_ref[...])


def _adamw_small(ws, gs, ms, vs):
    n = len(ws)

    def body(*refs):
        for i in range(n):
            _adamw_update(*[refs[j * n + i] for j in range(7)])

    vm = pl.BlockSpec(memory_space=pltpu.VMEM)
    res = pl.pallas_call(
        body, name="adamw_small", in_specs=[vm] * (4 * n), out_specs=[vm] * (3 * n),
        out_shape=[SDS(w.shape, F32) for _ in range(3) for w in ws],
        compiler_params=_cp(),
    )(*ws, *gs, *ms, *vs)
    return res[:n], res[n:2 * n], res[2 * n:]


def _adamw(w, g, m, v, xchg=None):
    rows, cols = w.shape
    t = rows
    for cand in (512, 256, 128, 64, 32, 16, 8):
        if rows % cand == 0:
            t = cand
            break

    def body(*refs):
        _adamw_update(*refs)

    blk = pl.BlockSpec((t, cols), lambda i: (i, 0))
    return _hosted_call(
        body, xchg, name=f"adamw_{rows}x{cols}", grid=(rows // t,),
        in_specs=[blk] * 4, out_specs=[blk] * 3, out_shape=[SDS((rows, cols), F32)] * 3, args=[w, g, m, v],
        relay_at=0.9)


_DIST0 = BAND + TQ - 1
_N_FAR = _DIST0 - REL_CLIP + 1
_N_NEAR = NG - _N_FAR - (2 * REL_CLIP - 1)


def _bias_vector(rel_bias):
    far = jnp.broadcast_to(rel_bias[..., -1:], (L, NH, _N_FAR))
    near = jnp.broadcast_to(rel_bias[..., :1], (L, NH, _N_NEAR))
    return jnp.concatenate([far, lax.rev(rel_bias[..., 1:-1], (2,)), near], axis=2)[:, :, None, :]


def _bias_vector_grad(dgr):
    first = jnp.sum(dgr[..., :_N_NEAR], axis=-1, keepdims=True)
    last = jnp.sum(dgr[..., NG - _N_FAR:], axis=-1, keepdims=True)
    return jnp.concatenate([first, dgr[..., _N_NEAR:NG - _N_FAR], last], axis=-1)


def _pool_blockdiag(pool_w):
    eye = jnp.eye(2, dtype=F32)
    pw = pool_w.reshape(L, 2, 2, HD, HD)
    return jnp.einsum("ljaik,ab->ljaibk", pw, eye).reshape(L, 2, LANE, LANE)


def _pool_blockdiag_grad(dwbd):
    d = dwbd.reshape(L, 2, 2, HD, 2, HD)
    return jnp.stack([d[:, :, 0, :, 0, :], d[:, :, 1, :, 1, :]], axis=2).reshape(L, 4, HD, HD)


def _pack(arrays, rows):
    flat = jnp.concatenate([a.reshape(-1).astype(F32) for a in arrays])
    return jnp.pad(flat, (0, rows * LANE - flat.shape[0])).reshape(rows, LANE)


def _unpack(packed, shapes):
    flat = packed.reshape(-1)
    out, o = [], 0
    for shp in shapes:
        n = int(np.prod(shp))
        out.append(flat[o:o + n].reshape(shp))
        o += n
    return out


def _rows_for(shapes):
    n = sum(int(np.prod(s)) for s in shapes)
    return -(-n // (8 * LANE)) * 8


def _grads(x, target, small_w, shards):
    g1, qg, kg, rb, cw_shard, pw, ps, g2 = small_w
    g1 = g1.reshape(L, 1, D)
    g2 = g2.reshape(L, 1, D)
    qg2 = jnp.tile(qg, (1, 2)).reshape(L, 1, LANE)
    kg2 = jnp.tile(kg, (1, 2)).reshape(L, 1, LANE)
    ps3 = ps.reshape(L, 1, PWD)
    wbd = _pool_blockdiag(pw).astype(BF16)

    def gather(*kl):
        return [(("gather", k, l), shards[k]) for k, l in kl if l < L]

    full = {}

    def arrived(got, *kl):
        full.update(zip([x for x in kl if x[1] < L], got))

    bias, got = _bias_tables(_bias_vector(rb), gather((0, 0)) + [(("gather", "conv_w", 0), cw_shard)])
    arrived(got[:1], (0, 0))
    cshard = CW // NDEV
    cw_all = got[1].reshape(NDEV, -1)
    cw = jnp.concatenate([cw_all[d, :L * 3 * cshard].reshape(L, 3, cshard) for d in range(NDEV)], axis=2)
    saved = []
    h = x
    for l in range(L):
        kl = ((1, 0),) if l == 0 else ()
        (p, h_b, qkv), got = _in_proj(h, g1, full[0, l], qg2, kg2, l, gather(*kl))
        arrived(got, *kl)
        kl = ((2, 0), (3, 0)) if l == 0 else ((1, l), (3, l))
        (mix, lse, o32), got = _attn_fwd(qkv, bias, l, gather(*kl), relay_at=0.95 if l == 0 else 0.8)
        arrived(got, *kl)
        mix = _conv_pool_fwd(p, mix, cw, wbd, ps3, l)
        kl = ((0, l + 1), (2, l + 1))
        (xm, a, *out), got = _mlp_fwd(h, mix, full[1, l], g2, full[2, l], full[3, l], l, gather(*kl),
                                      target if l == L - 1 else None)
        arrived(got, *kl)
        saved.append((h, h_b, p, qkv, mix, lse, o32, xm, a))
        h = out[0]
    dx, sq = out

    grads = {}
    slots = {}

    def scatter(*kl):
        return [(("scatter", k), grads[k, l]) for k, l in kl if l < L]

    def left(got, *kl):
        slots.update(zip([x for x in kl if x[1] < L], got))

    per_layer = [None] * L
    for l in reversed(range(L)):
        x_in, h_b, p, qkv, mix, lse, o32, xm, a = saved[l]
        (dxm, dmix, f_b, da_b, h2_b, dxo_b, dxm_b, dg2), got = _mlp_bwd(
            dx, a, xm, g2, full[2, l], full[3, l], full[1, l], l, scatter((3, l + 1)))
        left(got, (3, l + 1))
        grads[1, l], _ = _wgrad(mix, dxm_b, f"w_out_l{l}")
        kl = ((1, 0),) if l == 0 else ()
        grads[2, l], got = _wgrad(da_b, h2_b, f"w_mlp1_l{l}", scatter(*kl))
        left(got, *kl)
        kl = ((2, 0),) if l == 0 else ()
        grads[3, l], got = _wgrad(f_b, dxo_b, f"w_mlp2_l{l}", scatter(*kl))
        left(got, *kl)
        kl = ((3, 0), (0, 1)) if l == 0 else ((2, l), (1, l))
        (dq, dk, dv, dbias, dqg, dkg), got = _attn_bwd(qkv, p, lse, o32, dmix, qg2, kg2, bias, l, scatter(*kl))
        left(got, *kl)
        dgb, dgc, dhin, du, dcw, dwbd, dps = _conv_pool_bwd(p, dmix, cw, wbd, ps3, l)
        (dx, dp_b, dg1), got = _in_proj_bwd(
            [dq, dk, dv, dgb, dgc, dhin, du], x_in, dxm, g1, full[0, l], l, scatter((0, l + 1)) if l else None)
        left(got, (0, l + 1))
        grads[0, l], _ = _wgrad(dp_b, h_b, f"w_in_l{l}")
        per_layer[l] = (dg1, dg2, dqg, dkg, _bias_tables_grad(dbias, l), dcw, dwbd, dps)
    (g_w1_t, g_w2), got = _sum_slots([slots[k, l] for k in (2, 3) for l in range(L)], scatter((0, 0)))
    left(got, (0, 0))
    sums = [_sum_slots([slots[k, l] for l in range(L)])[0][0] for k in (0, 1)] + [g_w1_t, g_w2]

    st = [jnp.stack([per_layer[l][k] for l in range(L)]) for k in range(8)]
    small = dict(
        g1=st[0].reshape(L, D), g2=st[1].reshape(L, D),
        qg=st[2].reshape(L, NH, HD).sum(1), kg=st[3].reshape(L, NH, HD).sum(1),
        rb=_bias_vector_grad(st[4].reshape(L, NH, NG)), cw=st[5], pw=_pool_blockdiag_grad(st[6]),
        ps=st[7].reshape(L, PWD))
    return sq, dx, sums, small


def kernel(x, norm1_g, w_in, q_norm_g, k_norm_g, rel_bias, conv_w, pool_w, pool_scale, w_out, norm2_g, w_mlp1, w_mlp2, loss_target, m_norm1_g, m_w_in, m_q_norm_g, m_k_norm_g, m_rel_bias, m_conv_w, m_pool_w, m_pool_scale, m_w_out, m_norm2_g, m_w_mlp1, m_w_mlp2, v_norm1_g, v_w_in, v_q_norm_g, v_k_norm_g, v_rel_bias, v_conv_w, v_pool_w, v_pool_scale, v_w_out, v_norm2_g, v_w_mlp1, v_w_mlp2):
    me = _index(_position())
    cshard = CW // NDEV

    shards = [jnp.swapaxes(w_in, 1, 2).astype(BF16), w_out.astype(BF16),
              jnp.swapaxes(w_mlp1, 1, 2).astype(BF16), w_mlp2.astype(BF16)]
    small_w = (norm1_g, q_norm_g, k_norm_g, rel_bias, _pack([conv_w], 8)[None], pool_w, pool_scale, norm2_g)
    sq, grad_x, (g_win_t, g_wout, g_w1_t, g_w2), small = _grads(x[0], loss_target[0], small_w, shards)
    g_w_in = jnp.swapaxes(g_win_t, 1, 2)
    g_w_mlp1 = jnp.swapaxes(g_w1_t, 1, 2)

    names = ("g1", "qg", "kg", "rb", "cw", "pw", "ps", "g2")
    gshapes = [(L, D), (L, HD), (L, HD), (L, NH, 2 * REL_CLIP + 1), (L, 3, CW), (L, 4, HD, HD), (L, PWD), (L, D)]
    garrs = [small[n] for n in names]
    rows = _rows_for(gshapes + [(1,)])
    packed = _pack(garrs + [sq[0, :1]], rows)[None]

    def big(w, g, m, v, xchg=None):
        shp = w.shape
        r = lambda a: a.reshape(-1, shp[-1])
        outs, got = _adamw(r(w), r(g), r(m), r(v), xchg)
        return [o.reshape(shp) for o in outs], got

    up_1, got = big(w_mlp1, g_w_mlp1, m_w_mlp1, v_w_mlp1, [(("gather", "small", 0), packed)])
    total = _sum_small(got[0])
    g_g1, g_qg, g_kg, g_rb, g_cw_full, g_pw, g_ps, g_g2, sq_sum = _unpack(total, gshapes + [(1,)])
    loss = (0.5 / D) * sq_sum[0]
    g_cw = lax.dynamic_slice_in_dim(g_cw_full, me * cshard, cshard, axis=2)
    up_in = big(w_in, g_w_in, m_w_in, v_w_in)[0]
    up_out = big(w_out, g_wout, m_w_out, v_w_out)[0]
    up_2 = big(w_mlp2, g_w2, m_w_mlp2, v_w_mlp2)[0]

    sw = [norm1_g, q_norm_g, k_norm_g, rel_bias, conv_w, pool_w, pool_scale, norm2_g]
    sg = [g_g1, g_qg, g_kg, g_rb, g_cw, g_pw, g_ps, g_g2]
    sm = [m_norm1_g, m_q_norm_g, m_k_norm_g, m_rel_bias, m_conv_w, m_pool_w, m_pool_scale, m_norm2_g]
    sv = [v_norm1_g, v_q_norm_g, v_k_norm_g, v_rel_bias, v_conv_w, v_pool_w, v_pool_scale, v_norm2_g]
    s_delta, s_m, s_v = _adamw_small(sw, sg, sm, sv)

    def order(small_list, in_, out_, m1, m2):
        g1_, qg_, kg_, rb_, cw_, pw_, ps_, g2_ = small_list
        return [g1_, in_, qg_, kg_, rb_, cw_, pw_, ps_, out_, g2_, m1, m2]

    grads = order(sg, g_w_in, g_wout, g_w_mlp1, g_w2)
    deltas = order(s_delta, up_in[0], up_out[0], up_1[0], up_2[0])
    new_m = order(s_m, up_in[1], up_out[1], up_1[1], up_2[1])
    new_v = order(s_v, up_in[2], up_out[2], up_1[2], up_2[2])
    return (loss, grad_x[None], *grads, *deltas, *new_m, *new_v)
```

```python
import numpy as np
import jax
import jax.numpy as jnp
from jax import lax
from jax.experimental import pallas as pl
from jax.experimental.pallas import tpu as pltpu

F32 = jnp.float32
BF16 = jnp.bfloat16
SDS = jax.ShapeDtypeStruct
MESH_ID = pl.DeviceIdType.MESH

D = 1024
L = 4
CHUNK = 64
N_PREV = 8
HD = 64
NH = 8
AW = 512
CW = 256
PWD = 256
DIN = 2560
DFF = 4096
EPS = 1e-6
NEG = -1e30
REL_CLIP = 128
POOL_WINDOWS = (2, 4, 8, 16)
LR, B1, B2, AEPS, WD, STEP = 0.001, 0.9, 0.999, 1e-08, 0.01, 10

NDEV = 8
LANE = 128
BAND = N_PREV * CHUNK
TQ = 256
WIN = TQ + BAND
NVAR = BAND // TQ + 1
NTOE = -(-(WIN + TQ - 1) // LANE) * LANE
NG = (NVAR - 1) * TQ + NTOE
PAD = 16
RB_NORM = 64
RB_SOFT = 16
VMEM_LIMIT = 56 * 1024 * 1024
SHARD_ROWS = (DIN // NDEV, D // NDEV, DFF // NDEV, DFF // NDEV)

assert 2 * HD == LANE and NH * HD == AW and POOL_WINDOWS == (2, 4, 8, 16)
assert TQ % CHUNK == 0 and BAND % TQ == 0 and max(POOL_WINDOWS) <= PAD and all(r % 16 == 0 for r in SHARD_ROWS)


def _cp(**kw):
    return pltpu.CompilerParams(vmem_limit_bytes=VMEM_LIMIT, **kw)


def _nn(a, b):
    return jnp.dot(a, b, preferred_element_type=F32)


def _nt(a, b):
    return lax.dot_general(a, b, (((1,), (1,)), ((), ())), preferred_element_type=F32)


def _tn(a, b):
    return lax.dot_general(a, b, (((0,), (0,)), ((), ())), preferred_element_type=F32)


def _const(shape):
    n = len(shape)
    return pl.BlockSpec(shape, lambda *_: (0,) * n, pipeline_mode=pl.Buffered(1))


def _layer(shape, l):
    n = len(shape)
    return pl.BlockSpec((None,) + tuple(shape), lambda *_: (l,) + (0,) * n, pipeline_mode=pl.Buffered(1))


def _lo_mask():
    return lax.broadcasted_iota(jnp.int32, (1, LANE), 1) < HD


def _half_sum(t, lo):
    s_lo = jnp.sum(jnp.where(lo, t, 0.0), axis=-1, keepdims=True)
    s_hi = jnp.sum(jnp.where(lo, 0.0, t), axis=-1, keepdims=True)
    return jnp.where(lo, s_lo, s_hi)


def _half_sum_mxu(t, lo):
    del lo
    ii = lax.broadcasted_iota(jnp.int32, (LANE, LANE), 0) // HD
    jj = lax.broadcasted_iota(jnp.int32, (LANE, LANE), 1) // HD
    ones = jnp.where(ii == jj, 1.0, 0.0).astype(BF16)
    hi = t.astype(BF16)
    return _nn(hi, ones) + _nn((t - hi.astype(F32)).astype(BF16), ones)


def _head_norm(x, lo, half_sum=_half_sum):
    r = lax.rsqrt(half_sum(x * x, lo) * (1.0 / HD) + EPS)
    return x * r, r


def _head_norm_bwd(dy, xn, r, g, lo, half_sum=_half_sum):
    dxn = dy * g
    mu = half_sum(dxn * xn, lo) * (1.0 / HD)
    return r * (dxn - xn * mu), dy * xn


def _rms_bwd(dy, x, g):
    r = lax.rsqrt(jnp.mean(x * x, axis=-1, keepdims=True) + EPS)
    xn = x * r
    dxn = dy * g
    mu = jnp.mean(dxn * xn, axis=-1, keepdims=True)
    return r * (dxn - xn * mu), dy * xn


def _in_proj(x, g1, win_t, qg2, kg2, l, xchg=None):
    s = x.shape[0]
    t = min(512, s)
    nblk = AW // LANE

    def body(x_ref, g_ref, w_ref, qg_ref, kg_ref, p_ref, h_ref, qkv_ref):
        xv = x_ref[...]
        r = lax.rsqrt(jnp.mean(xv * xv, axis=-1, keepdims=True) + EPS)
        h = (xv * r * g_ref[...]).astype(BF16)
        h_ref[...] = h
        p_ref[...] = _nt(h, w_ref[...])
        lo = _lo_mask()
        gains = (qg_ref[...] * (HD ** -0.5), kg_ref[...])
        for r0 in range(0, t, RB_NORM):
            rows = pl.ds(r0, RB_NORM)
            for c in range(3 * nblk):
                cols = pl.ds(c * LANE, LANE)
                v = p_ref[rows, cols]
                if c < 2 * nblk:
                    v = _head_norm(v, lo)[0] * gains[c // nblk]
                qkv_ref[rows, cols] = v.astype(BF16)

    row = lambda c: pl.BlockSpec((t, c), lambda i: (i, 0))
    return _hosted_call(
        body, xchg, name="in_proj", grid=(s // t,),
        in_specs=[row(D), _layer((1, D), l), _const((DIN, D)), _layer((1, LANE), l), _layer((1, LANE), l)],
        out_specs=[row(DIN), row(D), row(3 * AW)],
        out_shape=[SDS((s, DIN), F32), SDS((s, D), BF16), SDS((s, 3 * AW), BF16)], args=[x, g1, win_t, qg2, kg2])


def _bias_spec():
    return pl.BlockSpec((None, 2, TQ, WIN), lambda j, i: (jnp.maximum(NVAR - 1 - i, 0), j, 0, 0))


def _bias_layer_spec(l):
    return pl.BlockSpec((None, None, 2, TQ, WIN), lambda j, i: (l, jnp.maximum(NVAR - 1 - i, 0), j, 0, 0))


def _attn_fwd(qkv, bias, l, xchg=None, relay_at=0.8):
    s = qkv.shape[0]
    nq = s // TQ

    def body(q_ref, k_ref, v_ref, b_ref, o_ref, lse_ref, o32_ref, s_ref, p_ref, m_ref, den_ref, o0_ref):
        i = pl.program_id(1)
        ks = pl.multiple_of(jnp.maximum(i * TQ - BAND, 0), TQ)
        lo = _lo_mask()
        q = q_ref[...]
        kwin = k_ref[pl.ds(ks, WIN), :]
        vwin = v_ref[pl.ds(ks, WIN), :]
        for half in range(2):
            m_ = lo if half == 0 else jnp.logical_not(lo)
            s_ref[half] = _nt(jnp.where(m_, q, jnp.zeros_like(q)), kwin)
            for r0 in range(0, TQ, RB_SOFT):
                rows = pl.ds(r0, RB_SOFT)
                mx = jnp.max(s_ref[half, rows, :] + b_ref[half, rows, :], axis=-1, keepdims=True)
                m_ref[rows, :] = jnp.broadcast_to(mx, (RB_SOFT, LANE))
            for r0 in range(0, TQ, RB_SOFT):
                rows = pl.ds(r0, RB_SOFT)
                mx = m_ref[rows, 0:1]
                e = jnp.exp(s_ref[half, rows, :] + b_ref[half, rows, :] - mx)
                p_ref[half, rows, :] = e.astype(BF16)
                den = jnp.sum(e, axis=-1, keepdims=True)
                den_ref[rows, :] = jnp.broadcast_to(den, (RB_SOFT, LANE))
                lse = mx + jnp.log(den)
                if half == 0:
                    lse_ref[rows, :] = jnp.broadcast_to(lse, (RB_SOFT, LANE))
                else:
                    lse_ref[rows, :] = jnp.where(lo, lse_ref[rows, :], lse)
            o = _nn(p_ref[half], vwin) * (1.0 / den_ref[...])
            if half == 0:
                o0_ref[...] = o
            else:
                o = jnp.where(lo, o0_ref[...], o)
                o32_ref[...] = o
                o_ref[...] = o.astype(BF16)

    tile = pl.BlockSpec((TQ, LANE), lambda j, i: (i, j))
    stat = pltpu.VMEM((TQ, LANE), F32)
    return _hosted_call(
        body, xchg, name="attn_fwd", grid=(NH // 2, nq),
        in_specs=[
            tile,
            pl.BlockSpec((s, LANE), lambda j, i: (0, AW // LANE + j)),
            pl.BlockSpec((s, LANE), lambda j, i: (0, 2 * AW // LANE + j)),
            _bias_layer_spec(l),
        ],
        out_specs=[tile, tile, tile],
        out_shape=[SDS((s, D), BF16), SDS((s, AW), F32), SDS((s, AW), F32)], args=[qkv, qkv, qkv, bias],
        scratch_shapes=[pltpu.VMEM((2, TQ, WIN), F32), pltpu.VMEM((2, TQ, WIN), BF16), stat, stat, stat],
        relay_at=relay_at)


_C0 = 3 * AW // LANE


def _cp_in_specs(s, l):
    blk = lambda f: pl.BlockSpec((s, LANE), f)
    return [
        blk(lambda g: (0, _C0 + jnp.minimum(g, 1))),
        blk(lambda g: (0, _C0 + 2 + jnp.minimum(g, 1))),
        blk(lambda g: (0, _C0 + 4 + jnp.minimum(g, 1))),
        blk(lambda g: (0, _C0 + 6 + jnp.maximum(g - 2, 0))),
        pl.BlockSpec((None, 3, LANE), lambda g: (l, 0, jnp.minimum(g, 1))),
        pl.BlockSpec((None, None, LANE, LANE), lambda g: (l, jnp.maximum(g - 2, 0), 0, 0)),
        pl.BlockSpec((None, 1, LANE), lambda g: (l, 0, jnp.maximum(g - 2, 0))),
    ]


def _pool_window_sums(u_ref, buf_a, buf_b, jj, s, rt):
    nrt = s // rt
    for t in range(nrt):
        buf_a[pl.ds(PAD + t * rt, rt), :] = u_ref[pl.ds(t * rt, rt), :]

    def stage(src, dst, sh):
        for t in range(nrt):
            r0 = PAD + t * rt
            dst[pl.ds(r0, rt), :] = src[pl.ds(r0, rt), :] + src[pl.ds(r0 - sh, rt), :]

    stage(buf_a, buf_b, 1)
    stage(buf_b, buf_a, 2)

    @pl.when(jj == 1)
    def _():
        stage(buf_a, buf_b, 4)
        stage(buf_b, buf_a, 8)


def _pool_counts(jj, lo, r0, rt):
    w0, w1, w2, w3 = [float(w) for w in POOL_WINDOWS]
    w = jnp.where(lo, jnp.where(jj == 0, w0, w2), jnp.where(jj == 0, w1, w3))
    pos1 = (lax.broadcasted_iota(jnp.int32, (rt, LANE), 0) + (r0 + 1)).astype(F32)
    return jnp.minimum(pos1, w)


def _conv_pool_fwd(p, mix, conv_w, wbd, pscale, l):
    s = p.shape[0]
    rt = min(256, s)
    nrt = s // rt

    def body(gb_ref, gc_ref, hin_ref, u_ref, cw_ref, wbd_ref, ps_ref, mix_in, o_ref, buf_a, buf_b):
        del mix_in
        g = pl.program_id(0)
        zpad = jnp.zeros((PAD, LANE), F32)
        buf_a[pl.ds(0, PAD), :] = zpad
        buf_b[pl.ds(0, PAD), :] = zpad

        @pl.when(g < 2)
        def _conv():
            for t in range(nrt):
                buf_a[pl.ds(PAD + t * rt, rt), :] = gc_ref[pl.ds(t * rt, rt), :] * hin_ref[pl.ds(t * rt, rt), :]
            w0, w1, w2 = cw_ref[0:1, :], cw_ref[1:2, :], cw_ref[2:3, :]
            for t in range(nrt):
                r0 = PAD + t * rt
                y = w0 * buf_a[pl.ds(r0 - 2, rt), :] + w1 * buf_a[pl.ds(r0 - 1, rt), :] + w2 * buf_a[pl.ds(r0, rt), :]
                o_ref[pl.ds(t * rt, rt), :] = (gb_ref[pl.ds(t * rt, rt), :] * y).astype(BF16)

        @pl.when(g >= 2)
        def _pool():
            jj = g - 2
            lo = _lo_mask()
            _pool_window_sums(u_ref, buf_a, buf_b, jj, s, rt)
            wb = wbd_ref[...]
            for t in range(nrt):
                r0 = PAD + t * rt
                wsum = jnp.where(lo, buf_b[pl.ds(r0, rt), :], buf_a[pl.ds(r0, rt), :])
                m = wsum / _pool_counts(jj, lo, t * rt, rt) - u_ref[pl.ds(t * rt, rt), :]
                o_ref[pl.ds(t * rt, rt), :] = (_nn(m.astype(BF16), wb) * ps_ref[...]).astype(BF16)

    return pl.pallas_call(
        body, name="conv_pool_fwd", grid=(4,),
        in_specs=_cp_in_specs(s, l) + [pl.BlockSpec(memory_space=pl.ANY)],
        out_specs=pl.BlockSpec((s, LANE), lambda g: (0, AW // LANE + g)),
        out_shape=SDS((s, D), BF16),
        scratch_shapes=[pltpu.VMEM((s + 2 * PAD, LANE), F32), pltpu.VMEM((s + 2 * PAD, LANE), F32)],
        input_output_aliases={7: 0},
        compiler_params=_cp(),
    )(p, p, p, p, conv_w, wbd, pscale, mix)


def _mlp_fwd(x, mix, wout, g2, w1_t, w2, l, xchg=None, target=None):
    s = x.shape[0]
    t = min(256, s)

    def body(*refs):
        x_ref, mix_ref, wo_ref, g_ref, w1_ref, w2_ref = refs[:6]
        xm_ref, a_ref, xo_ref = refs[-4:-1] if target is not None else refs[-3:]
        xm = x_ref[...] + _nn(mix_ref[...], wo_ref[...])
        xm_ref[...] = xm
        r = lax.rsqrt(jnp.mean(xm * xm, axis=-1, keepdims=True) + EPS)
        h2 = (xm * r * g_ref[...]).astype(BF16)
        a = _nt(h2, w1_ref[...])
        a_ref[...] = a.astype(BF16)
        f = jnp.square(jnp.maximum(a, 0.0)).astype(BF16)
        xo = xm + _nn(f, w2_ref[...])
        if target is None:
            xo_ref[...] = xo
        else:
            acc_ref = refs[-1]

            @pl.when(pl.program_id(0) == 0)
            def _():
                acc_ref[...] = jnp.zeros_like(acc_ref)
            e = xo - refs[6][...]
            xo_ref[...] = e * (1.0 / D)
            acc_ref[...] += jnp.sum(e * e)

    row = lambda c: pl.BlockSpec((t, c), lambda i: (i, 0))
    last = target is not None
    return _hosted_call(
        body, xchg, name="mlp_fwd_loss" if last else "mlp_fwd", grid=(s // t,),
        in_specs=[row(D), row(D), _const((D, D)), _layer((1, D), l), _const((DFF, D)), _const((DFF, D))] + [row(D)] * last,
        out_specs=[row(D), row(DFF), row(D)] + [pl.BlockSpec((8, LANE), lambda i: (0, 0))] * last,
        out_shape=[SDS((s, D), F32), SDS((s, DFF), BF16), SDS((s, D), F32)] + [SDS((8, LANE), F32)] * last,
        args=[x, mix, wout, g2, w1_t, w2] + [target] * last, relay_at=0.85)


def _mlp_bwd(dxo, a, xm, g2, w1_t, w2, wout, l, xchg=None):
    s = dxo.shape[0]
    t = min(256, s)

    def body(dxo_ref, a_ref, xm_ref, g_ref, w1_ref, w2_ref, wo_ref,
             dxm_ref, dmix_ref, f_ref, da_ref, h2_ref, dxob_ref, dxmb_ref, dg_ref):
        @pl.when(pl.program_id(0) == 0)
        def _():
            dg_ref[...] = jnp.zeros_like(dg_ref)
        dxo = dxo_ref[...]
        dxob = dxo.astype(BF16)
        dxob_ref[...] = dxob
        ra = jnp.maximum(a_ref[...].astype(F32), 0.0)
        f_ref[...] = jnp.square(ra).astype(BF16)
        dab = (_nt(dxob, w2_ref[...]) * (2.0 * ra)).astype(BF16)
        da_ref[...] = dab
        dh2 = _nn(dab, w1_ref[...])
        xm = xm_ref[...]
        g = g_ref[...]
        r = lax.rsqrt(jnp.mean(xm * xm, axis=-1, keepdims=True) + EPS)
        h2_ref[...] = (xm * r * g).astype(BF16)
        dx_n, dgr = _rms_bwd(dh2, xm, g)
        dg_ref[...] += jnp.sum(dgr, axis=0, keepdims=True)
        dxm = dxo + dx_n
        dxm_ref[...] = dxm
        dxmb = dxm.astype(BF16)
        dxmb_ref[...] = dxmb
        dmix_ref[...] = _nt(dxmb, wo_ref[...])

    row = lambda c: pl.BlockSpec((t, c), lambda i: (i, 0))
    return _hosted_call(
        body, xchg, name="mlp_bwd", grid=(s // t,),
        in_specs=[row(D), row(DFF), row(D), _layer((1, D), l), _const((DFF, D)), _const((DFF, D)), _const((D, D))],
        out_specs=[row(D), row(D), row(DFF), row(DFF), row(D), row(D), row(D), pl.BlockSpec((1, D), lambda i: (0, 0))],
        out_shape=[SDS((s, D), F32), SDS((s, D), F32), SDS((s, DFF), BF16), SDS((s, DFF), BF16),
                   SDS((s, D), BF16), SDS((s, D), BF16), SDS((s, D), BF16), SDS((1, D), F32)],
        args=[dxo, a, xm, g2, w1_t, w2, wout])


def _attn_bwd(qkv, p, lse, o32, dmix, qg2, kg2, bias, l, xchg=None):
    s = p.shape[0]
    nq = s // TQ
    scale = HD ** -0.5

    def body(qs_ref, kb_ref, vb_ref, q_ref, k_ref, qg_ref, kg_ref, b_ref, lse_ref, o_ref, do_ref,
             dq_ref, dk_ref, dv_ref, db_ref, dqg_ref, dkg_ref,
             dk_acc, dv_acc, s_ref, dp_ref, ds_ref, pb_ref, dqn_ref, dl_ref):
        i = pl.program_id(1)
        kt = jnp.maximum(i - BAND // TQ, 0)
        ks = pl.multiple_of(kt * TQ, TQ)
        lo = _lo_mask()

        @pl.when(i == 0)
        def _():
            dk_acc[...] = jnp.zeros_like(dk_acc)
            dv_acc[...] = jnp.zeros_like(dv_acc)
            dqg_ref[...] = jnp.zeros_like(dqg_ref)
            dkg_ref[...] = jnp.zeros_like(dkg_ref)

        @pl.when(i < NVAR)
        def _():
            db_ref[...] = jnp.zeros_like(db_ref)

        qs = qs_ref[...]
        kwin = kb_ref[pl.ds(ks, WIN), :]
        vwin = vb_ref[pl.ds(ks, WIN), :]
        do = do_ref[...]
        dob = do.astype(BF16)
        dl_ref[...] = _half_sum(do * o_ref[...], lo)
        for half in range(2):
            m_ = lo if half == 0 else jnp.logical_not(lo)
            qa = jnp.where(m_, qs, jnp.zeros_like(qs))
            doa = jnp.where(m_, dob, jnp.zeros_like(dob))
            s_ref[half] = _nt(qa, kwin)
            dp_ref[half] = _nt(doa, vwin)
            for r0 in range(0, TQ, RB_SOFT):
                rows = pl.ds(r0, RB_SOFT)
                lse_h = lse_ref[rows, half * HD:half * HD + 1]
                pm = jnp.exp(s_ref[half, rows, :] + b_ref[half, rows, :] - lse_h)
                ds = pm * (dp_ref[half, rows, :] - dl_ref[rows, half * HD:half * HD + 1])
                db_ref[half, rows, :] += ds
                ds_ref[half, rows, :] = ds.astype(BF16)
                pb_ref[half, rows, :] = pm.astype(BF16)
            dsb = ds_ref[half]
            dq_h = _nn(dsb, kwin)
            if half == 0:
                dqn_ref[...] = dq_h
            else:
                dqn_ref[...] = jnp.where(lo, dqn_ref[...], dq_h)
            dk_t = _tn(qa, dsb)
            dv_t = _tn(doa, pb_ref[half])
            for t in range(WIN // TQ):
                dk_acc[kt + t] += dk_t[:, t * TQ:(t + 1) * TQ]
                dv_acc[kt + t] += dv_t[:, t * TQ:(t + 1) * TQ]
        qg, kg = qg_ref[...], kg_ref[...]
        xq, rq = _head_norm(q_ref[...], lo)
        dq, dqg_rows = _head_norm_bwd(dqn_ref[...] * scale, xq, rq, qg, lo)
        dq_ref[...] = dq.astype(BF16)
        dqg_ref[...] += jnp.sum(dqg_rows, axis=0, keepdims=True)

        @pl.when(i == nq - 1)
        def _():
            dkg = jnp.zeros((1, LANE), F32)
            for t in range(nq):
                rows = pl.ds(t * TQ, TQ)
                xk, rk = _head_norm(k_ref[rows, :], lo, _half_sum_mxu)
                dk, dkg_rows = _head_norm_bwd(dk_acc[t].T, xk, rk, kg, lo, _half_sum_mxu)
                dk_ref[rows, :] = dk.astype(BF16)
                dv_ref[rows, :] = dv_acc[t].T.astype(BF16)
                dkg = dkg + jnp.sum(dkg_rows, axis=0, keepdims=True)
            dkg_ref[...] = dkg

    tile = pl.BlockSpec((TQ, LANE), lambda j, i: (i, j))
    kcol = lambda c0: pl.BlockSpec((s, LANE), lambda j, i: (0, c0 + j))
    gain = pl.BlockSpec((None, 1, LANE), lambda j, i: (j, 0, 0))
    return _hosted_call(
        body, xchg, name="attn_bwd", grid=(NH // 2, nq),
        in_specs=[
            tile, kcol(AW // LANE), kcol(2 * AW // LANE), tile, kcol(AW // LANE),
            _layer((1, LANE), l), _layer((1, LANE), l),
            _bias_layer_spec(l), tile, tile, tile,
        ],
        out_specs=[tile, kcol(0), kcol(0), _bias_spec(), gain, gain],
        out_shape=[SDS((s, AW), BF16), SDS((s, AW), BF16), SDS((s, AW), BF16),
                   SDS((NVAR, NH, TQ, WIN), F32), SDS((NH // 2, 1, LANE), F32), SDS((NH // 2, 1, LANE), F32)],
        scratch_shapes=[pltpu.VMEM((nq, LANE, TQ), F32), pltpu.VMEM((nq, LANE, TQ), F32),
                        pltpu.VMEM((2, TQ, WIN), F32), pltpu.VMEM((2, TQ, WIN), F32),
                        pltpu.VMEM((2, TQ, WIN), BF16), pltpu.VMEM((2, TQ, WIN), BF16),
                        pltpu.VMEM((TQ, LANE), F32), pltpu.VMEM((TQ, LANE), F32)],
        args=[qkv, qkv, qkv, p, p, qg2, kg2, bias, lse, o32, dmix])


def _conv_pool_bwd(p, dmix, conv_w, wbd, pscale, l):
    s = p.shape[0]
    rt = min(256, s)
    nrt = s // rt

    def body(gb_ref, gc_ref, hin_ref, u_ref, cw_ref, wbd_ref, ps_ref, dy_ref,
             dgb_ref, dgc_ref, dhin_ref, du_ref, dcw_ref, dwbd_ref, dps_ref, buf_a, buf_b, buf_c, buf_d):
        g = pl.program_id(0)
        zpad = jnp.zeros((PAD, LANE), F32)
        for buf in (buf_a, buf_b, buf_c):
            buf[pl.ds(0, PAD), :] = zpad
            buf[pl.ds(PAD + s, PAD), :] = zpad

        @pl.when(g < 2)
        def _conv():
            for t in range(nrt):
                rows = pl.ds(t * rt, rt)
                buf_a[pl.ds(PAD + t * rt, rt), :] = gc_ref[rows, :] * hin_ref[rows, :]
                buf_b[pl.ds(PAD + t * rt, rt), :] = dy_ref[rows, :] * gb_ref[rows, :]
            w0, w1, w2 = cw_ref[0:1, :], cw_ref[1:2, :], cw_ref[2:3, :]
            d0 = jnp.zeros((1, LANE), F32)
            d1 = jnp.zeros((1, LANE), F32)
            d2 = jnp.zeros((1, LANE), F32)
            for t in range(nrt):
                rows = pl.ds(t * rt, rt)
                r0 = PAD + t * rt
                z2, z1, z0 = buf_a[pl.ds(r0 - 2, rt), :], buf_a[pl.ds(r0 - 1, rt), :], buf_a[pl.ds(r0, rt), :]
                y = w0 * z2 + w1 * z1 + w2 * z0
                dgb_ref[rows, :] = (dy_ref[rows, :] * y).astype(BF16)
                e0 = buf_b[pl.ds(r0, rt), :]
                d0 = d0 + jnp.sum(e0 * z2, axis=0, keepdims=True)
                d1 = d1 + jnp.sum(e0 * z1, axis=0, keepdims=True)
                d2 = d2 + jnp.sum(e0 * z0, axis=0, keepdims=True)
                dz = w2 * e0 + w1 * buf_b[pl.ds(r0 + 1, rt), :] + w0 * buf_b[pl.ds(r0 + 2, rt), :]
                dgc_ref[rows, :] = (dz * hin_ref[rows, :]).astype(BF16)
                dhin_ref[rows, :] = (dz * gc_ref[rows, :]).astype(BF16)
            dcw_ref[0:1, :] = d0
            dcw_ref[1:2, :] = d1
            dcw_ref[2:3, :] = d2

        @pl.when(g >= 2)
        def _pool():
            jj = g - 2
            lo = _lo_mask()
            _pool_window_sums(u_ref, buf_a, buf_b, jj, s, rt)
            wb = wbd_ref[...]
            ps = ps_ref[...]
            dps = jnp.zeros((1, LANE), F32)
            dwb = jnp.zeros((LANE, LANE), F32)
            for t in range(nrt):
                rows = pl.ds(t * rt, rt)
                r0 = PAD + t * rt
                cnt = _pool_counts(jj, lo, t * rt, rt)
                wsum = jnp.where(lo, buf_b[pl.ds(r0, rt), :], buf_a[pl.ds(r0, rt), :])
                mb = (wsum / cnt - u_ref[rows, :]).astype(BF16)
                dy = dy_ref[rows, :]
                dps = dps + jnp.sum(dy * _nn(mb, wb), axis=0, keepdims=True)
                dmp = (dy * ps).astype(BF16)
                dwb = dwb + _tn(mb, dmp)
                dm = _nt(dmp, wb)
                buf_d[rows, :] = dm
                buf_c[pl.ds(r0, rt), :] = dm / cnt
            dps_ref[...] = dps
            dwbd_ref[...] = dwb

            def stage(src, dst, sh):
                for t in range(nrt):
                    r0 = PAD + t * rt
                    dst[pl.ds(r0, rt), :] = src[pl.ds(r0, rt), :] + src[pl.ds(r0 + sh, rt), :]

            def finish(first, second):
                for t in range(nrt):
                    rows = pl.ds(t * rt, rt)
                    r0 = PAD + t * rt
                    fw = jnp.where(lo, first[pl.ds(r0, rt), :], second[pl.ds(r0, rt), :])
                    du_ref[rows, :] = (fw - buf_d[rows, :]).astype(BF16)

            stage(buf_c, buf_a, 1)
            stage(buf_a, buf_b, 2)

            @pl.when(jj == 0)
            def _():
                finish(buf_a, buf_b)

            @pl.when(jj == 1)
            def _():
                stage(buf_b, buf_c, 4)
                stage(buf_c, buf_a, 8)
                finish(buf_c, buf_a)

    cblk = pl.BlockSpec((s, LANE), lambda g: (0, jnp.minimum(g, 1)))
    pblk = pl.BlockSpec((s, LANE), lambda g: (0, jnp.maximum(g - 2, 0)))
    padded = pltpu.VMEM((s + 2 * PAD, LANE), F32)
    return pl.pallas_call(
        body, name="conv_pool_bwd", grid=(4,),
        in_specs=_cp_in_specs(s, l) + [pl.BlockSpec((s, LANE), lambda g: (0, AW // LANE + g))],
        out_specs=[cblk, cblk, cblk, pblk,
                   pl.BlockSpec((3, LANE), lambda g: (0, jnp.minimum(g, 1))),
                   pl.BlockSpec((None, LANE, LANE), lambda g: (jnp.maximum(g - 2, 0), 0, 0)),
                   pl.BlockSpec((1, LANE), lambda g: (0, jnp.maximum(g - 2, 0)))],
        out_shape=[SDS((s, CW), BF16), SDS((s, CW), BF16), SDS((s, CW), BF16), SDS((s, PWD), BF16),
                   SDS((3, CW), F32), SDS((2, LANE, LANE), F32), SDS((1, PWD), F32)],
        scratch_shapes=[padded, padded, padded, pltpu.VMEM((s, LANE), F32)],
        compiler_params=_cp(),
    )(p, p, p, p, conv_w, wbd, pscale, dmix)


def _in_proj_bwd(parts, x, dxm, g1, win_t, l, xchg=None):
    s = x.shape[0]
    t = min(256, s)
    widths = [a.shape[1] for a in parts]
    offs = [int(o) for o in np.cumsum([0] + widths[:-1])]
    n = len(parts)

    def body(*refs):
        part_refs = refs[:n]
        x_ref, dxm_ref, g_ref, w_ref, dx_ref, dp_ref, dg_ref = refs[n:]

        @pl.when(pl.program_id(0) == 0)
        def _():
            dg_ref[...] = jnp.zeros_like(dg_ref)
        for r, o, w in zip(part_refs, offs, widths):
            dp_ref[:, o:o + w] = r[...]
        dh = _nn(dp_ref[...], w_ref[...])
        dx_n, dgr = _rms_bwd(dh, x_ref[...], g_ref[...])
        dg_ref[...] += jnp.sum(dgr, axis=0, keepdims=True)
        dx_ref[...] = dxm_ref[...] + dx_n

    row = lambda c: pl.BlockSpec((t, c), lambda i: (i, 0))
    return _hosted_call(
        body, xchg, name="in_proj_bwd", grid=(s // t,),
        in_specs=[row(w) for w in widths] + [row(D), row(D), _layer((1, D), l), _const((DIN, D))],
        out_specs=[row(D), row(DIN), pl.BlockSpec((1, D), lambda i: (0, 0))],
        out_shape=[SDS((s, D), F32), SDS((s, DIN), BF16), SDS((1, D), F32)], args=[*parts, x, dxm, g1, win_t])


def _wgrad(a, b, tag, xchg=None):
    s, m = a.shape
    mb = 512

    def body(a_ref, b_ref, o_ref):
        o_ref[...] = _tn(a_ref[...], b_ref[...]).astype(BF16)

    (out,), got = _hosted_call(
        body, xchg, name=f"wgrad_{tag}", grid=(m // mb,),
        in_specs=[pl.BlockSpec((s, mb), lambda mi: (0, mi)), _const((s, D))],
        out_specs=[pl.BlockSpec((mb, D), lambda mi: (mi, 0))],
        out_shape=[SDS((m, D), BF16)], args=[a, b])
    return out, got


def _bias_tables(gvec, xchg=None):
    def body(g_ref, o_ref):
        qc = lax.broadcasted_iota(jnp.int32, (TQ, WIN), 0) // CHUNK
        kc = lax.broadcasted_iota(jnp.int32, (TQ, WIN), 1) // CHUNK
        for var in range(NVAR):
            vec = jnp.broadcast_to(g_ref[:, var * TQ:var * TQ + NTOE], (TQ, NTOE))
            toe = pltpu.roll(vec, NTOE - TQ + 1, 1, stride=1, stride_axis=0)[:, :WIN]
            rel = (BAND - var * TQ) // CHUNK + qc - kc
            o_ref[var] = jnp.where((rel >= 0) & (rel <= N_PREV), toe, NEG)

    (out,), got = _hosted_call(
        body, xchg, name="bias_tables", grid=(L, NH),
        in_specs=[pl.BlockSpec((None, None, 1, NG), lambda l, h: (l, h, 0, 0))],
        out_specs=[pl.BlockSpec((None, NVAR, None, TQ, WIN), lambda l, h: (l, 0, h, 0, 0))],
        out_shape=[SDS((L, NVAR, NH, TQ, WIN), F32)], args=[gvec], relay_at=0.95)
    return out, got


def _bias_tables_grad(dbias, l):
    nb = NTOE // LANE
    wb = WIN // LANE

    def body(d_ref, o_ref):
        ii = lax.broadcasted_iota(jnp.int32, (LANE, LANE), 0)
        jj = lax.broadcasted_iota(jnp.int32, (LANE, LANE), 1)
        flip = jnp.where(ii + jj == LANE - 1, 1.0, 0.0).astype(BF16)
        o_ref[...] = jnp.zeros_like(o_ref)
        for var in range(NVAR):
            blocks = []
            for b in range(nb):
                src = nb - 1 - b
                if src >= wb:
                    blocks.append(jnp.zeros((TQ, LANE), F32))
                    continue
                xv = d_ref[var, :, src * LANE:(src + 1) * LANE]
                hi = xv.astype(BF16)
                lo = (xv - hi.astype(F32)).astype(BF16)
                blocks.append(_nn(hi, flip) + _nn(lo, flip))
            rev = jnp.concatenate(blocks, axis=1)
            skew = pltpu.roll(rev, NTOE - TQ + 1, 1, stride=1, stride_axis=0)
            off = NG - NTOE - var * TQ
            o_ref[:, off:off + NTOE] += jnp.sum(skew, axis=0, keepdims=True)

    return pl.pallas_call(
        body, name=f"bias_tables_grad_l{l}", grid=(NH,),
        in_specs=[pl.BlockSpec((NVAR, None, TQ, WIN), lambda h: (0, h, 0, 0))],
        out_specs=pl.BlockSpec((None, 1, NG), lambda h: (h, 0, 0)),
        out_shape=SDS((NH, 1, NG), F32),
        compiler_params=_cp(),
    )(dbias)


_SIBLING = (0, 0, 1)
_CHIPS = [(1, 0, 0), (0, 1, 0), (1, 1, 0)]
_MASKS = [_SIBLING] + _CHIPS + [(1, 0, 1), (0, 1, 1), (1, 1, 1)]


def _position():
    return lax.axis_index("x"), lax.axis_index("y"), lax.axis_index("c")


def _peer(pos, mask):
    return tuple(1 - a if f else a for a, f in zip(pos, mask))


def _index(pos):
    return 4 * pos[0] + 2 * pos[1] + pos[2]


def _exchange_phases(items, src, dst, sems):
    send_sems, recv_sems, local_sems = sems
    me = _position()
    sib = _peer(me, _SIBLING)

    def remote(s_ref, d_ref, pi, n, to):
        return pltpu.make_async_remote_copy(
            src_ref=s_ref, dst_ref=d_ref, send_sem=send_sems.at[pi, n], recv_sem=recv_sems.at[pi, n],
            device_id=to, device_id_type=MESH_ID)

    def parts(n):
        it = items[n]
        r = src[n].shape[1] if it[0] == "gather" else src[n].shape[0] // NDEV
        block = lambda ref, pos: ref.at[pl.ds(_index(pos) * r, r), :]
        if it[0] == "gather":
            own = src[n].at[it[2]]
            local = pltpu.make_async_copy(own, block(dst[n], me), local_sems.at[n])
            sends = [remote(own, block(dst[n], me), pi, n, _peer(me, m)) for pi, m in enumerate([_SIBLING] + _CHIPS)]
            hops = [(remote(block(dst[n], _peer(me, m)), block(dst[n], _peer(me, m)), 1 + j, n, _peer(me, m)),
                     remote(block(dst[n], _peer(me, m)), block(dst[n], _peer(me, m)), 4 + j, n, sib))
                    for j, m in enumerate(_CHIPS)]
            lands = [remote(own, block(dst[n], sib), 0, n, sib)]
            lands += [remote(own, block(dst[n], _peer(sib, m)), 4 + j, n, sib) for j, m in enumerate(_CHIPS)]
        else:
            local = pltpu.make_async_copy(block(src[n], me), dst[n].at[_index(me)], local_sems.at[n])
            sends = [remote(block(src[n], _peer(me, m)), dst[n].at[_index(me)], pi, n, _peer(me, m))
                     for pi, m in enumerate(_MASKS)]
            hops = []
            lands = [remote(block(src[n], me), dst[n].at[_index(_peer(me, m))], pi, n, _peer(me, m))
                     for pi, m in enumerate(_MASKS)]
        return local, sends, hops, lands

    def start():
        for n in range(len(items)):
            local, sends, _, _ = parts(n)
            local.start()
            for cp in sends:
                cp.start()

    def relay():
        for n in range(len(items)):
            for arrived, onward in parts(n)[2]:
                arrived.wait_recv()
                onward.start()

    def finish():
        for n in range(len(items)):
            local, sends, hops, lands = parts(n)
            for cp in lands:
                cp.wait_recv()
            for cp in sends + [onward for _, onward in hops]:
                cp.wait_send()
            local.wait()

    return start, relay, finish


def _hosted_call(body, xchg, *, name, grid, in_specs, out_specs, out_shape, args, scratch_shapes=(), relay_at=0.8):
    if not xchg:
        outs = pl.pallas_call(
            body, name=name, grid=grid, in_specs=list(in_specs), out_specs=list(out_specs),
            out_shape=list(out_shape), scratch_shapes=list(scratch_shapes), compiler_params=_cp())(*args)
        return outs, []
    items = [it for it, _ in xchg]
    n_in, n_out, n_scr, nit = len(args), len(out_shape), len(scratch_shapes), len(items)
    hbm = pl.BlockSpec(memory_space=pl.ANY)
    steps = int(np.prod(grid))
    relay_step = min(int(relay_at * steps), steps - 1)

    def dst_shape(it, a):
        if it[0] == "gather":
            return SDS((NDEV * a.shape[1], a.shape[2]), a.dtype)
        return SDS((NDEV, a.shape[0] // NDEV, a.shape[1]), a.dtype)

    def wrapped(*refs):
        ins = refs[:n_in]
        src = refs[n_in:n_in + nit]
        outs = refs[n_in + nit:n_in + nit + n_out]
        dst = refs[n_in + nit + n_out:n_in + 2 * nit + n_out]
        scratch = refs[n_in + 2 * nit + n_out:n_in + 2 * nit + n_out + n_scr]
        start, relay, finish = _exchange_phases(items, src, dst, refs[n_in + 2 * nit + n_out + n_scr:])
        step = 0
        for d, g in enumerate(grid):
            step = step * g + pl.program_id(d)
        pl.when(step == 0)(start)
        body(*ins, *outs, *scratch)
        pl.when(step == relay_step)(relay)
        pl.when(step == steps - 1)(finish)

    npeer = len(_MASKS)
    res = pl.pallas_call(
        wrapped, name=name, grid=grid,
        in_specs=list(in_specs) + [hbm] * nit,
        out_specs=list(out_specs) + [hbm] * nit,
        out_shape=list(out_shape) + [dst_shape(it, a) for it, a in xchg],
        scratch_shapes=list(scratch_shapes) + [
            pltpu.SemaphoreType.DMA((npeer, nit)), pltpu.SemaphoreType.DMA((npeer, nit)), pltpu.SemaphoreType.DMA((nit,))],
        compiler_params=_cp(),
    )(*args, *[a for _, a in xchg])
    return list(res[:n_out]), list(res[n_out:])


def _sum_slots(slots, xchg=None):
    _, r, _ = slots[0].shape
    n = len(slots) // L
    rt = 64

    def body(*refs):
        for k in range(n):
            for l in range(L):
                src = refs[k * L + l]
                acc = src[0].astype(F32)
                for d in range(1, NDEV):
                    acc = acc + src[d].astype(F32)
                refs[n * L + k][l] = acc

    return _hosted_call(
        body, xchg, name=f"sum_slots_r{r}" + ("_x" if xchg else ""), grid=(r // rt,),
        in_specs=[pl.BlockSpec((NDEV, rt, D), lambda i: (0, i, 0))] * (n * L),
        out_specs=[pl.BlockSpec((L, rt, D), lambda i: (0, i, 0))] * n,
        out_shape=[SDS((L, r, D), F32)] * n, args=list(slots))


def _sum_small(slots):
    rows = slots.shape[0] // NDEV

    def body(in_ref, o_ref):
        acc = in_ref[pl.ds(0, rows), :]
        for d in range(1, NDEV):
            acc = acc + in_ref[pl.ds(d * rows, rows), :]
        o_ref[...] = acc

    vm = pl.BlockSpec(memory_space=pltpu.VMEM)
    return pl.pallas_call(
        body, name="sum_small", in_specs=[vm], out_specs=vm, out_shape=SDS((rows, LANE), F32),
        compiler_params=_cp())(slots)


def _adamw_update(w_ref, g_ref, m_ref, v_ref, d_ref, nm_ref, nv_ref):
    gv = g_ref[...]
    mn = B1 * m_ref[...] + (1.0 - B1) * gv
    vn = B2 * v_ref[...] + (1.0 - B2) * jnp.square(gv)
    nm_ref[...] = mn
    nv_ref[...] = vn
    m_hat = mn / (1.0 - B1 ** STEP)
    v_hat = vn / (1.0 - B2 ** STEP)
    d_ref[...] = -LR * (m_hat / (jnp.sqrt(v_hat) + AEPS) + WD * w_ref[...])


def _adamw_small(ws, gs, ms, vs):
    n = len(ws)

    def body(*refs):
        for i in range(n):
            _adamw_update(*[refs[j * n + i] for j in range(7)])

    vm = pl.BlockSpec(memory_space=pltpu.VMEM)
    res = pl.pallas_call(
        body, name="adamw_small", in_specs=[vm] * (4 * n), out_specs=[vm] * (3 * n),
        out_shape=[SDS(w.shape, F32) for _ in range(3) for w in ws],
        compiler_params=_cp(),
    )(*ws, *gs, *ms, *vs)
    return res[:n], res[n:2 * n], res[2 * n:]


def _adamw(w, g, m, v, xchg=None):
    rows, cols = w.shape
    t = rows
    for cand in (512, 256, 128, 64, 32, 16, 8):
        if rows % cand == 0:
            t = cand
            break

    def body(*refs):
        _adamw_update(*refs)

    blk = pl.BlockSpec((t, cols), lambda i: (i, 0))
    return _hosted_call(
        body, xchg, name=f"adamw_{rows}x{cols}", grid=(rows // t,),
        in_specs=[blk] * 4, out_specs=[blk] * 3, out_shape=[SDS((rows, cols), F32)] * 3, args=[w, g, m, v])


_DIST0 = BAND + TQ - 1
_N_FAR = _DIST0 - REL_CLIP + 1
_N_NEAR = NG - _N_FAR - (2 * REL_CLIP - 1)


def _bias_vector(rel_bias):
    far = jnp.broadcast_to(rel_bias[..., -1:], (L, NH, _N_FAR))
    near = jnp.broadcast_to(rel_bias[..., :1], (L, NH, _N_NEAR))
    return jnp.concatenate([far, lax.rev(rel_bias[..., 1:-1], (2,)), near], axis=2)[:, :, None, :]


def _bias_vector_grad(dgr):
    first = jnp.sum(dgr[..., :_N_NEAR], axis=-1, keepdims=True)
    last = jnp.sum(dgr[..., NG - _N_FAR:], axis=-1, keepdims=True)
    return jnp.concatenate([first, dgr[..., _N_NEAR:NG - _N_FAR], last], axis=-1)


def _pool_blockdiag(pool_w):
    eye = jnp.eye(2, dtype=F32)
    pw = pool_w.reshape(L, 2, 2, HD, HD)
    return jnp.einsum("ljaik,ab->ljaibk", pw, eye).reshape(L, 2, LANE, LANE)


def _pool_blockdiag_grad(dwbd):
    d = dwbd.reshape(L, 2, 2, HD, 2, HD)
    return jnp.stack([d[:, :, 0, :, 0, :], d[:, :, 1, :, 1, :]], axis=2).reshape(L, 4, HD, HD)


def _pack(arrays, rows):
    flat = jnp.concatenate([a.reshape(-1).astype(F32) for a in arrays])
    return jnp.pad(flat, (0, rows * LANE - flat.shape[0])).reshape(rows, LANE)


def _unpack(packed, shapes):
    flat = packed.reshape(-1)
    out, o = [], 0
    for shp in shapes:
        n = int(np.prod(shp))
        out.append(flat[o:o + n].reshape(shp))
        o += n
    return out


def _rows_for(shapes):
    n = sum(int(np.prod(s)) for s in shapes)
    return -(-n // (8 * LANE)) * 8


def _grads(x, target, small_w, shards):
    g1, qg, kg, rb, cw_shard, pw, ps, g2 = small_w
    g1 = g1.reshape(L, 1, D)
    g2 = g2.reshape(L, 1, D)
    qg2 = jnp.tile(qg, (1, 2)).reshape(L, 1, LANE)
    kg2 = jnp.tile(kg, (1, 2)).reshape(L, 1, LANE)
    ps3 = ps.reshape(L, 1, PWD)
    wbd = _pool_blockdiag(pw).astype(BF16)

    def gather(*kl):
        return [(("gather", k, l), shards[k]) for k, l in kl if l < L]

    full = {}

    def arrived(got, *kl):
        full.update(zip([x for x in kl if x[1] < L], got))

    bias, got = _bias_tables(_bias_vector(rb), gather((0, 0), (1, 0)) + [(("gather", "conv_w", 0), cw_shard)])
    arrived(got[:2], (0, 0), (1, 0))
    cshard = CW // NDEV
    cw_all = got[2].reshape(NDEV, -1)
    cw = jnp.concatenate([cw_all[d, :L * 3 * cshard].reshape(L, 3, cshard) for d in range(NDEV)], axis=2)
    saved = []
    h = x
    for l in range(L):
        (p, h_b, qkv), _ = _in_proj(h, g1, full[0, l], qg2, kg2, l)
        kl = ((2, 0), (3, 0)) if l == 0 else ((1, l), (3, l))
        (mix, lse, o32), got = _attn_fwd(qkv, bias, l, gather(*kl), relay_at=0.95 if l == 0 else 0.8)
        arrived(got, *kl)
        mix = _conv_pool_fwd(p, mix, cw, wbd, ps3, l)
        kl = ((0, l + 1), (2, l + 1))
        (xm, a, *out), got = _mlp_fwd(h, mix, full[1, l], g2, full[2, l], full[3, l], l, gather(*kl),
                                      target if l == L - 1 else None)
        arrived(got, *kl)
        saved.append((h, h_b, p, qkv, mix, lse, o32, xm, a))
        h = out[0]
    dx, sq = out

    grads = {}
    slots = {}

    def scatter(*kl):
        return [(("scatter", k), grads[k, l]) for k, l in kl if l < L]

    def left(got, *kl):
        slots.update(zip([x for x in kl if x[1] < L], got))

    per_layer = [None] * L
    for l in reversed(range(L)):
        x_in, h_b, p, qkv, mix, lse, o32, xm, a = saved[l]
        (dxm, dmix, f_b, da_b, h2_b, dxo_b, dxm_b, dg2), got = _mlp_bwd(
            dx, a, xm, g2, full[2, l], full[3, l], full[1, l], l, scatter((3, l + 1)))
        left(got, (3, l + 1))
        grads[1, l], _ = _wgrad(mix, dxm_b, f"w_out_l{l}")
        kl = ((1, 0),) if l == 0 else ()
        grads[2, l], got = _wgrad(da_b, h2_b, f"w_mlp1_l{l}", scatter(*kl))
        left(got, *kl)
        kl = ((2, 0),) if l == 0 else ()
        grads[3, l], got = _wgrad(f_b, dxo_b, f"w_mlp2_l{l}", scatter(*kl))
        left(got, *kl)
        kl = ((3, 0), (0, 1)) if l == 0 else ((2, l), (1, l))
        (dq, dk, dv, dbias, dqg, dkg), got = _attn_bwd(qkv, p, lse, o32, dmix, qg2, kg2, bias, l, scatter(*kl))
        left(got, *kl)
        dgb, dgc, dhin, du, dcw, dwbd, dps = _conv_pool_bwd(p, dmix, cw, wbd, ps3, l)
        (dx, dp_b, dg1), got = _in_proj_bwd(
            [dq, dk, dv, dgb, dgc, dhin, du], x_in, dxm, g1, full[0, l], l, scatter((0, l + 1)) if l else None)
        left(got, (0, l + 1))
        grads[0, l], _ = _wgrad(dp_b, h_b, f"w_in_l{l}")
        per_layer[l] = (dg1, dg2, dqg, dkg, _bias_tables_grad(dbias, l), dcw, dwbd, dps)
    (g_w1_t, g_w2), got = _sum_slots([slots[k, l] for k in (2, 3) for l in range(L)], scatter((0, 0)))
    left(got, (0, 0))
    sums = [_sum_slots([slots[k, l] for l in range(L)])[0][0] for k in (0, 1)] + [g_w1_t, g_w2]

    st = [jnp.stack([per_layer[l][k] for l in range(L)]) for k in range(8)]
    small = dict(
        g1=st[0].reshape(L, D), g2=st[1].reshape(L, D),
        qg=st[2].reshape(L, NH, HD).sum(1), kg=st[3].reshape(L, NH, HD).sum(1),
        rb=_bias_vector_grad(st[4].reshape(L, NH, NG)), cw=st[5], pw=_pool_blockdiag_grad(st[6]),
        ps=st[7].reshape(L, PWD))
    return sq, dx, sums, small


def kernel(x, norm1_g, w_in, q_norm_g, k_norm_g, rel_bias, conv_w, pool_w, pool_scale, w_out, norm2_g, w_mlp1, w_mlp2, loss_target, m_norm1_g, m_w_in, m_q_norm_g, m_k_norm_g, m_rel_bias, m_conv_w, m_pool_w, m_pool_scale, m_w_out, m_norm2_g, m_w_mlp1, m_w_mlp2, v_norm1_g, v_w_in, v_q_norm_g, v_k_norm_g, v_rel_bias, v_conv_w, v_pool_w, v_pool_scale, v_w_out, v_norm2_g, v_w_mlp1, v_w_mlp2):
    me = _index(_position())
    cshard = CW // NDEV

    shards = [jnp.swapaxes(w_in, 1, 2).astype(BF16), w_out.astype(BF16),
              jnp.swapaxes(w_mlp1, 1, 2).astype(BF16), w_mlp2.astype(BF16)]
    small_w = (norm1_g, q_norm_g, k_norm_g, rel_bias, _pack([conv_w], 8)[None], pool_w, pool_scale, norm2_g)
    sq, grad_x, (g_win_t, g_wout, g_w1_t, g_w2), small = _grads(x[0], loss_target[0], small_w, shards)
    g_w_in = jnp.swapaxes(g_win_t, 1, 2)
    g_w_mlp1 = jnp.swapaxes(g_w1_t, 1, 2)

    names = ("g1", "qg", "kg", "rb", "cw", "pw", "ps", "g2")
    gshapes = [(L, D), (L, HD), (L, HD), (L, NH, 2 * REL_CLIP + 1), (L, 3, CW), (L, 4, HD, HD), (L, PWD), (L, D)]
    garrs = [small[n] for n in names]
    rows = _rows_for(gshapes + [(1,)])
    packed = _pack(garrs + [sq[0, :1]], rows)[None]

    def big(w, g, m, v, xchg=None):
        shp = w.shape
        r = lambda a: a.reshape(-1, shp[-1])
        outs, got = _adamw(r(w), r(g), r(m), r(v), xchg)
        return [o.reshape(shp) for o in outs], got

    up_1, got = big(w_mlp1, g_w_mlp1, m_w_mlp1, v_w_mlp1, [(("gather", "small", 0), packed)])
    total = _sum_small(got[0])
    g_g1, g_qg, g_kg, g_rb, g_cw_full, g_pw, g_ps, g_g2, sq_sum = _unpack(total, gshapes + [(1,)])
    loss = (0.5 / D) * sq_sum[0]
    g_cw = lax.dynamic_slice_in_dim(g_cw_full, me * cshard, cshard, axis=2)
    up_in = big(w_in, g_w_in, m_w_in, v_w_in)[0]
    up_out = big(w_out, g_wout, m_w_out, v_w_out)[0]
    up_2 = big(w_mlp2, g_w2, m_w_mlp2, v_w_mlp2)[0]

    sw = [norm1_g, q_norm_g, k_norm_g, rel_bias, conv_w, pool_w, pool_scale, norm2_g]
    sg = [g_g1, g_qg, g_kg, g_rb, g_cw, g_pw, g_ps, g_g2]
    sm = [m_norm1_g, m_q_norm_g, m_k_norm_g, m_rel_bias, m_conv_w, m_pool_w, m_pool_scale, m_norm2_g]
    sv = [v_norm1_g, v_q_norm_g, v_k_norm_g, v_rel_bias, v_conv_w, v_pool_w, v_pool_scale, v_norm2_g]
    s_delta, s_m, s_v = _adamw_small(sw, sg, sm, sv)

    def order(small_list, in_, out_, m1, m2):
        g1_, qg_, kg_, rb_, cw_, pw_, ps_, g2_ = small_list
        return [g1_, in_, qg_, kg_, rb_, cw_, pw_, ps_, out_, g2_, m1, m2]

    grads = order(sg, g_w_in, g_wout, g_w_mlp1, g_w2)
    deltas = order(s_delta, up_in[0], up_out[0], up_1[0], up_2[0])
    new_m = order(s_m, up_in[1], up_out[1], up_1[1], up_2[1])
    new_v = order(s_v, up_in[2], up_out[2], up_1[2], up_2[2])
    return (loss, grad_x[None], *grads, *deltas, *new_m, *new_v)
```

```python
import numpy as np
import jax
import jax.numpy as jnp
from jax import lax
from jax.experimental import pallas as pl
from jax.experimental.pallas import tpu as pltpu

F32 = jnp.float32
BF16 = jnp.bfloat16
SDS = jax.ShapeDtypeStruct
MESH_ID = pl.DeviceIdType.MESH

D = 1024
L = 4
CHUNK = 64
N_PREV = 8
HD = 64
NH = 8
AW = 512
CW = 256
PWD = 256
DIN = 2560
DFF = 4096
EPS = 1e-6
NEG = -1e30
REL_CLIP = 128
POOL_WINDOWS = (2, 4, 8, 16)
LR, B1, B2, AEPS, WD, STEP = 0.001, 0.9, 0.999, 1e-08, 0.01, 10

NDEV = 8
LANE = 128
BAND = N_PREV * CHUNK
TQ = 256
WIN = TQ + BAND
NVAR = BAND // TQ + 1
NTOE = -(-(WIN + TQ - 1) // LANE) * LANE
NG = (NVAR - 1) * TQ + NTOE
PAD = 16
RB_NORM = 64
RB_SOFT = 16
VMEM_LIMIT = 56 * 1024 * 1024
SHARD_ROWS = (DIN // NDEV, D // NDEV, DFF // NDEV, DFF // NDEV)

assert 2 * HD == LANE and NH * HD == AW and POOL_WINDOWS == (2, 4, 8, 16)
assert TQ % CHUNK == 0 and BAND % TQ == 0 and max(POOL_WINDOWS) <= PAD and all(r % 16 == 0 for r in SHARD_ROWS)


def _cp(**kw):
    return pltpu.CompilerParams(vmem_limit_bytes=VMEM_LIMIT, **kw)


def _nn(a, b):
    return jnp.dot(a, b, preferred_element_type=F32)


def _nt(a, b):
    return lax.dot_general(a, b, (((1,), (1,)), ((), ())), preferred_element_type=F32)


def _tn(a, b):
    return lax.dot_general(a, b, (((0,), (0,)), ((), ())), preferred_element_type=F32)


def _const(shape):
    n = len(shape)
    return pl.BlockSpec(shape, lambda *_: (0,) * n, pipeline_mode=pl.Buffered(1))


def _layer(shape, l):
    n = len(shape)
    return pl.BlockSpec((None,) + tuple(shape), lambda *_: (l,) + (0,) * n, pipeline_mode=pl.Buffered(1))


def _lo_mask():
    return lax.broadcasted_iota(jnp.int32, (1, LANE), 1) < HD


def _half_sum(t, lo):
    s_lo = jnp.sum(jnp.where(lo, t, 0.0), axis=-1, keepdims=True)
    s_hi = jnp.sum(jnp.where(lo, 0.0, t), axis=-1, keepdims=True)
    return jnp.where(lo, s_lo, s_hi)


def _half_sum_mxu(t, lo):
    del lo
    ii = lax.broadcasted_iota(jnp.int32, (LANE, LANE), 0) // HD
    jj = lax.broadcasted_iota(jnp.int32, (LANE, LANE), 1) // HD
    ones = jnp.where(ii == jj, 1.0, 0.0).astype(BF16)
    hi = t.astype(BF16)
    return _nn(hi, ones) + _nn((t - hi.astype(F32)).astype(BF16), ones)


def _head_norm(x, lo, half_sum=_half_sum):
    r = lax.rsqrt(half_sum(x * x, lo) * (1.0 / HD) + EPS)
    return x * r, r


def _head_norm_bwd(dy, xn, r, g, lo, half_sum=_half_sum):
    dxn = dy * g
    mu = half_sum(dxn * xn, lo) * (1.0 / HD)
    return r * (dxn - xn * mu), dy * xn


def _rms_bwd(dy, x, g):
    r = lax.rsqrt(jnp.mean(x * x, axis=-1, keepdims=True) + EPS)
    xn = x * r
    dxn = dy * g
    mu = jnp.mean(dxn * xn, axis=-1, keepdims=True)
    return r * (dxn - xn * mu), dy * xn


def _in_proj(x, g1, win_t, qg2, kg2, l, xchg=None):
    s = x.shape[0]
    t = min(512, s)
    nblk = AW // LANE

    def body(x_ref, g_ref, w_ref, qg_ref, kg_ref, p_ref, h_ref, qkv_ref):
        xv = x_ref[...]
        r = lax.rsqrt(jnp.mean(xv * xv, axis=-1, keepdims=True) + EPS)
        h = (xv * r * g_ref[...]).astype(BF16)
        h_ref[...] = h
        p_ref[...] = _nt(h, w_ref[...])
        lo = _lo_mask()
        gains = (qg_ref[...] * (HD ** -0.5), kg_ref[...])
        for r0 in range(0, t, RB_NORM):
            rows = pl.ds(r0, RB_NORM)
            for c in range(3 * nblk):
                cols = pl.ds(c * LANE, LANE)
                v = p_ref[rows, cols]
                if c < 2 * nblk:
                    v = _head_norm(v, lo)[0] * gains[c // nblk]
                qkv_ref[rows, cols] = v.astype(BF16)

    row = lambda c: pl.BlockSpec((t, c), lambda i: (i, 0))
    return _hosted_call(
        body, xchg, name="in_proj", grid=(s // t,),
        in_specs=[row(D), _layer((1, D), l), _const((DIN, D)), _layer((1, LANE), l), _layer((1, LANE), l)],
        out_specs=[row(DIN), row(D), row(3 * AW)],
        out_shape=[SDS((s, DIN), F32), SDS((s, D), BF16), SDS((s, 3 * AW), BF16)], args=[x, g1, win_t, qg2, kg2])


def _bias_spec():
    return pl.BlockSpec((None, 2, TQ, WIN), lambda j, i: (jnp.maximum(NVAR - 1 - i, 0), j, 0, 0))


def _bias_layer_spec(l):
    return pl.BlockSpec((None, None, 2, TQ, WIN), lambda j, i: (l, jnp.maximum(NVAR - 1 - i, 0), j, 0, 0))


def _attn_fwd(qkv, bias, l, xchg=None, relay_at=0.8):
    s = qkv.shape[0]
    nq = s // TQ

    def body(q_ref, k_ref, v_ref, b_ref, o_ref, lse_ref, o32_ref, s_ref, p_ref, m_ref, den_ref, o0_ref):
        i = pl.program_id(1)
        ks = pl.multiple_of(jnp.maximum(i * TQ - BAND, 0), TQ)
        lo = _lo_mask()
        q = q_ref[...]
        kwin = k_ref[pl.ds(ks, WIN), :]
        vwin = v_ref[pl.ds(ks, WIN), :]
        for half in range(2):
            m_ = lo if half == 0 else jnp.logical_not(lo)
            s_ref[half] = _nt(jnp.where(m_, q, jnp.zeros_like(q)), kwin)
            for r0 in range(0, TQ, RB_SOFT):
                rows = pl.ds(r0, RB_SOFT)
                mx = jnp.max(s_ref[half, rows, :] + b_ref[half, rows, :], axis=-1, keepdims=True)
                m_ref[rows, :] = jnp.broadcast_to(mx, (RB_SOFT, LANE))
            for r0 in range(0, TQ, RB_SOFT):
                rows = pl.ds(r0, RB_SOFT)
                mx = m_ref[rows, 0:1]
                e = jnp.exp(s_ref[half, rows, :] + b_ref[half, rows, :] - mx)
                p_ref[half, rows, :] = e.astype(BF16)
                den = jnp.sum(e, axis=-1, keepdims=True)
                den_ref[rows, :] = jnp.broadcast_to(den, (RB_SOFT, LANE))
                lse = mx + jnp.log(den)
                if half == 0:
                    lse_ref[rows, :] = jnp.broadcast_to(lse, (RB_SOFT, LANE))
                else:
                    lse_ref[rows, :] = jnp.where(lo, lse_ref[rows, :], lse)
            o = _nn(p_ref[half], vwin) * (1.0 / den_ref[...])
            if half == 0:
                o0_ref[...] = o
            else:
                o = jnp.where(lo, o0_ref[...], o)
                o32_ref[...] = o
                o_ref[...] = o.astype(BF16)

    tile = pl.BlockSpec((TQ, LANE), lambda j, i: (i, j))
    stat = pltpu.VMEM((TQ, LANE), F32)
    return _hosted_call(
        body, xchg, name="attn_fwd", grid=(NH // 2, nq),
        in_specs=[
            tile,
            pl.BlockSpec((s, LANE), lambda j, i: (0, AW // LANE + j)),
            pl.BlockSpec((s, LANE), lambda j, i: (0, 2 * AW // LANE + j)),
            _bias_layer_spec(l),
        ],
        out_specs=[tile, tile, tile],
        out_shape=[SDS((s, D), BF16), SDS((s, AW), F32), SDS((s, AW), F32)], args=[qkv, qkv, qkv, bias],
        scratch_shapes=[pltpu.VMEM((2, TQ, WIN), F32), pltpu.VMEM((2, TQ, WIN), BF16), stat, stat, stat],
        relay_at=relay_at)


_C0 = 3 * AW // LANE


def _cp_in_specs(s, l):
    blk = lambda f: pl.BlockSpec((s, LANE), f)
    return [
        blk(lambda g: (0, _C0 + jnp.minimum(g, 1))),
        blk(lambda g: (0, _C0 + 2 + jnp.minimum(g, 1))),
        blk(lambda g: (0, _C0 + 4 + jnp.minimum(g, 1))),
        blk(lambda g: (0, _C0 + 6 + jnp.maximum(g - 2, 0))),
        pl.BlockSpec((None, 3, LANE), lambda g: (l, 0, jnp.minimum(g, 1))),
        pl.BlockSpec((None, None, LANE, LANE), lambda g: (l, jnp.maximum(g - 2, 0), 0, 0)),
        pl.BlockSpec((None, 1, LANE), lambda g: (l, 0, jnp.maximum(g - 2, 0))),
    ]


def _pool_window_sums(u_ref, buf_a, buf_b, jj, s, rt):
    nrt = s // rt
    for t in range(nrt):
        buf_a[pl.ds(PAD + t * rt, rt), :] = u_ref[pl.ds(t * rt, rt), :]

    def stage(src, dst, sh):
        for t in range(nrt):
            r0 = PAD + t * rt
            dst[pl.ds(r0, rt), :] = src[pl.ds(r0, rt), :] + src[pl.ds(r0 - sh, rt), :]

    stage(buf_a, buf_b, 1)
    stage(buf_b, buf_a, 2)

    @pl.when(jj == 1)
    def _():
        stage(buf_a, buf_b, 4)
        stage(buf_b, buf_a, 8)


def _pool_counts(jj, lo, r0, rt):
    w0, w1, w2, w3 = [float(w) for w in POOL_WINDOWS]
    w = jnp.where(lo, jnp.where(jj == 0, w0, w2), jnp.where(jj == 0, w1, w3))
    pos1 = (lax.broadcasted_iota(jnp.int32, (rt, LANE), 0) + (r0 + 1)).astype(F32)
    return jnp.minimum(pos1, w)


def _conv_pool_fwd(p, mix, conv_w, wbd, pscale, l):
    s = p.shape[0]
    rt = min(256, s)
    nrt = s // rt

    def body(gb_ref, gc_ref, hin_ref, u_ref, cw_ref, wbd_ref, ps_ref, mix_in, o_ref, buf_a, buf_b):
        del mix_in
        g = pl.program_id(0)
        zpad = jnp.zeros((PAD, LANE), F32)
        buf_a[pl.ds(0, PAD), :] = zpad
        buf_b[pl.ds(0, PAD), :] = zpad

        @pl.when(g < 2)
        def _conv():
            for t in range(nrt):
                buf_a[pl.ds(PAD + t * rt, rt), :] = gc_ref[pl.ds(t * rt, rt), :] * hin_ref[pl.ds(t * rt, rt), :]
            w0, w1, w2 = cw_ref[0:1, :], cw_ref[1:2, :], cw_ref[2:3, :]
            for t in range(nrt):
                r0 = PAD + t * rt
                y = w0 * buf_a[pl.ds(r0 - 2, rt), :] + w1 * buf_a[pl.ds(r0 - 1, rt), :] + w2 * buf_a[pl.ds(r0, rt), :]
                o_ref[pl.ds(t * rt, rt), :] = (gb_ref[pl.ds(t * rt, rt), :] * y).astype(BF16)

        @pl.when(g >= 2)
        def _pool():
            jj = g - 2
            lo = _lo_mask()
            _pool_window_sums(u_ref, buf_a, buf_b, jj, s, rt)
            wb = wbd_ref[...]
            for t in range(nrt):
                r0 = PAD + t * rt
                wsum = jnp.where(lo, buf_b[pl.ds(r0, rt), :], buf_a[pl.ds(r0, rt), :])
                m = wsum / _pool_counts(jj, lo, t * rt, rt) - u_ref[pl.ds(t * rt, rt), :]
                o_ref[pl.ds(t * rt, rt), :] = (_nn(m.astype(BF16), wb) * ps_ref[...]).astype(BF16)

    return pl.pallas_call(
        body, name="conv_pool_fwd", grid=(4,),
        in_specs=_cp_in_specs(s, l) + [pl.BlockSpec(memory_space=pl.ANY)],
        out_specs=pl.BlockSpec((s, LANE), lambda g: (0, AW // LANE + g)),
        out_shape=SDS((s, D), BF16),
        scratch_shapes=[pltpu.VMEM((s + 2 * PAD, LANE), F32), pltpu.VMEM((s + 2 * PAD, LANE), F32)],
        input_output_aliases={7: 0},
        compiler_params=_cp(),
    )(p, p, p, p, conv_w, wbd, pscale, mix)


def _mlp_fwd(x, mix, wout, g2, w1_t, w2, l, xchg=None, target=None):
    s = x.shape[0]
    t = min(256, s)

    def body(*refs):
        x_ref, mix_ref, wo_ref, g_ref, w1_ref, w2_ref = refs[:6]
        xm_ref, a_ref, xo_ref = refs[-4:-1] if target is not None else refs[-3:]
        xm = x_ref[...] + _nn(mix_ref[...], wo_ref[...])
        xm_ref[...] = xm
        r = lax.rsqrt(jnp.mean(xm * xm, axis=-1, keepdims=True) + EPS)
        h2 = (xm * r * g_ref[...]).astype(BF16)
        a = _nt(h2, w1_ref[...])
        a_ref[...] = a.astype(BF16)
        f = jnp.square(jnp.maximum(a, 0.0)).astype(BF16)
        xo = xm + _nn(f, w2_ref[...])
        if target is None:
            xo_ref[...] = xo
        else:
            acc_ref = refs[-1]

            @pl.when(pl.program_id(0) == 0)
            def _():
                acc_ref[...] = jnp.zeros_like(acc_ref)
            e = xo - refs[6][...]
            xo_ref[...] = e * (1.0 / D)
            acc_ref[...] += jnp.sum(e * e)

    row = lambda c: pl.BlockSpec((t, c), lambda i: (i, 0))
    last = target is not None
    return _hosted_call(
        body, xchg, name="mlp_fwd_loss" if last else "mlp_fwd", grid=(s // t,),
        in_specs=[row(D), row(D), _const((D, D)), _layer((1, D), l), _const((DFF, D)), _const((DFF, D))] + [row(D)] * last,
        out_specs=[row(D), row(DFF), row(D)] + [pl.BlockSpec((8, LANE), lambda i: (0, 0))] * last,
        out_shape=[SDS((s, D), F32), SDS((s, DFF), BF16), SDS((s, D), F32)] + [SDS((8, LANE), F32)] * last,
        args=[x, mix, wout, g2, w1_t, w2] + [target] * last, relay_at=0.9)


def _mlp_bwd(dxo, a, xm, g2, w1_t, w2, wout, l, xchg=None):
    s = dxo.shape[0]
    t = min(256, s)

    def body(dxo_ref, a_ref, xm_ref, g_ref, w1_ref, w2_ref, wo_ref,
             dxm_ref, dmix_ref, f_ref, da_ref, h2_ref, dxob_ref, dxmb_ref, dg_ref):
        @pl.when(pl.program_id(0) == 0)
        def _():
            dg_ref[...] = jnp.zeros_like(dg_ref)
        dxo = dxo_ref[...]
        dxob = dxo.astype(BF16)
        dxob_ref[...] = dxob
        ra = jnp.maximum(a_ref[...].astype(F32), 0.0)
        f_ref[...] = jnp.square(ra).astype(BF16)
        dab = (_nt(dxob, w2_ref[...]) * (2.0 * ra)).astype(BF16)
        da_ref[...] = dab
        dh2 = _nn(dab, w1_ref[...])
        xm = xm_ref[...]
        g = g_ref[...]
        r = lax.rsqrt(jnp.mean(xm * xm, axis=-1, keepdims=True) + EPS)
        h2_ref[...] = (xm * r * g).astype(BF16)
        dx_n, dgr = _rms_bwd(dh2, xm, g)
        dg_ref[...] += jnp.sum(dgr, axis=0, keepdims=True)
        dxm = dxo + dx_n
        dxm_ref[...] = dxm
        dxmb = dxm.astype(BF16)
        dxmb_ref[...] = dxmb
        dmix_ref[...] = _nt(dxmb, wo_ref[...])

    row = lambda c: pl.BlockSpec((t, c), lambda i: (i, 0))
    return _hosted_call(
        body, xchg, name="mlp_bwd", grid=(s // t,),
        in_specs=[row(D), row(DFF), row(D), _layer((1, D), l), _const((DFF, D)), _const((DFF, D)), _const((D, D))],
        out_specs=[row(D), row(D), row(DFF), row(DFF), row(D), row(D), row(D), pl.BlockSpec((1, D), lambda i: (0, 0))],
        out_shape=[SDS((s, D), F32), SDS((s, D), F32), SDS((s, DFF), BF16), SDS((s, DFF), BF16),
                   SDS((s, D), BF16), SDS((s, D), BF16), SDS((s, D), BF16), SDS((1, D), F32)],
        args=[dxo, a, xm, g2, w1_t, w2, wout])


def _attn_bwd(qkv, p, lse, o32, dmix, qg2, kg2, bias, l, xchg=None):
    s = p.shape[0]
    nq = s // TQ
    scale = HD ** -0.5

    def body(qs_ref, kb_ref, vb_ref, q_ref, k_ref, qg_ref, kg_ref, b_ref, lse_ref, o_ref, do_ref,
             dq_ref, dk_ref, dv_ref, db_ref, dqg_ref, dkg_ref,
             dk_acc, dv_acc, s_ref, dp_ref, ds_ref, pb_ref, dqn_ref, dl_ref):
        i = pl.program_id(1)
        kt = jnp.maximum(i - BAND // TQ, 0)
        ks = pl.multiple_of(kt * TQ, TQ)
        lo = _lo_mask()

        @pl.when(i == 0)
        def _():
            dk_acc[...] = jnp.zeros_like(dk_acc)
            dv_acc[...] = jnp.zeros_like(dv_acc)
            dqg_ref[...] = jnp.zeros_like(dqg_ref)
            dkg_ref[...] = jnp.zeros_like(dkg_ref)

        @pl.when(i < NVAR)
        def _():
            db_ref[...] = jnp.zeros_like(db_ref)

        qs = qs_ref[...]
        kwin = kb_ref[pl.ds(ks, WIN), :]
        vwin = vb_ref[pl.ds(ks, WIN), :]
        do = do_ref[...]
        dob = do.astype(BF16)
        dl_ref[...] = _half_sum(do * o_ref[...], lo)
        for half in range(2):
            m_ = lo if half == 0 else jnp.logical_not(lo)
            qa = jnp.where(m_, qs, jnp.zeros_like(qs))
            doa = jnp.where(m_, dob, jnp.zeros_like(dob))
            s_ref[half] = _nt(qa, kwin)
            dp_ref[half] = _nt(doa, vwin)
            for r0 in range(0, TQ, RB_SOFT):
                rows = pl.ds(r0, RB_SOFT)
                lse_h = lse_ref[rows, half * HD:half * HD + 1]
                pm = jnp.exp(s_ref[half, rows, :] + b_ref[half, rows, :] - lse_h)
                ds = pm * (dp_ref[half, rows, :] - dl_ref[rows, half * HD:half * HD + 1])
                db_ref[half, rows, :] += ds
                ds_ref[half, rows, :] = ds.astype(BF16)
                pb_ref[half, rows, :] = pm.astype(BF16)
            dsb = ds_ref[half]
            dq_h = _nn(dsb, kwin)
            if half == 0:
                dqn_ref[...] = dq_h
            else:
                dqn_ref[...] = jnp.where(lo, dqn_ref[...], dq_h)
            dk_t = _tn(qa, dsb)
            dv_t = _tn(doa, pb_ref[half])
            for t in range(WIN // TQ):
                dk_acc[kt + t] += dk_t[:, t * TQ:(t + 1) * TQ]
                dv_acc[kt + t] += dv_t[:, t * TQ:(t + 1) * TQ]
        qg, kg = qg_ref[...], kg_ref[...]
        xq, rq = _head_norm(q_ref[...], lo)
        dq, dqg_rows = _head_norm_bwd(dqn_ref[...] * scale, xq, rq, qg, lo)
        dq_ref[...] = dq.astype(BF16)
        dqg_ref[...] += jnp.sum(dqg_rows, axis=0, keepdims=True)

        @pl.when(i == nq - 1)
        def _():
            dkg = jnp.zeros((1, LANE), F32)
            for t in range(nq):
                rows = pl.ds(t * TQ, TQ)
                xk, rk = _head_norm(k_ref[rows, :], lo, _half_sum_mxu)
                dk, dkg_rows = _head_norm_bwd(dk_acc[t].T, xk, rk, kg, lo, _half_sum_mxu)
                dk_ref[rows, :] = dk.astype(BF16)
                dv_ref[rows, :] = dv_acc[t].T.astype(BF16)
                dkg = dkg + jnp.sum(dkg_rows, axis=0, keepdims=True)
            dkg_ref[...] = dkg

    tile = pl.BlockSpec((TQ, LANE), lambda j, i: (i, j))
    kcol = lambda c0: pl.BlockSpec((s, LANE), lambda j, i: (0, c0 + j))
    gain = pl.BlockSpec((None, 1, LANE), lambda j, i: (j, 0, 0))
    return _hosted_call(
        body, xchg, name="attn_bwd", grid=(NH // 2, nq),
        in_specs=[
            tile, kcol(AW // LANE), kcol(2 * AW // LANE), tile, kcol(AW // LANE),
            _layer((1, LANE), l), _layer((1, LANE), l),
            _bias_layer_spec(l), tile, tile, tile,
        ],
        out_specs=[tile, kcol(0), kcol(0), _bias_spec(), gain, gain],
        out_shape=[SDS((s, AW), BF16), SDS((s, AW), BF16), SDS((s, AW), BF16),
                   SDS((NVAR, NH, TQ, WIN), F32), SDS((NH // 2, 1, LANE), F32), SDS((NH // 2, 1, LANE), F32)],
        scratch_shapes=[pltpu.VMEM((nq, LANE, TQ), F32), pltpu.VMEM((nq, LANE, TQ), F32),
                        pltpu.VMEM((2, TQ, WIN), F32), pltpu.VMEM((2, TQ, WIN), F32),
                        pltpu.VMEM((2, TQ, WIN), BF16), pltpu.VMEM((2, TQ, WIN), BF16),
                        pltpu.VMEM((TQ, LANE), F32), pltpu.VMEM((TQ, LANE), F32)],
        args=[qkv, qkv, qkv, p, p, qg2, kg2, bias, lse, o32, dmix])


def _conv_pool_bwd(p, dmix, conv_w, wbd, pscale, l):
    s = p.shape[0]
    rt = min(256, s)
    nrt = s // rt

    def body(gb_ref, gc_ref, hin_ref, u_ref, cw_ref, wbd_ref, ps_ref, dy_ref,
             dgb_ref, dgc_ref, dhin_ref, du_ref, dcw_ref, dwbd_ref, dps_ref, buf_a, buf_b, buf_c, buf_d):
        g = pl.program_id(0)
        zpad = jnp.zeros((PAD, LANE), F32)
        for buf in (buf_a, buf_b, buf_c):
            buf[pl.ds(0, PAD), :] = zpad
            buf[pl.ds(PAD + s, PAD), :] = zpad

        @pl.when(g < 2)
        def _conv():
            for t in range(nrt):
                rows = pl.ds(t * rt, rt)
                buf_a[pl.ds(PAD + t * rt, rt), :] = gc_ref[rows, :] * hin_ref[rows, :]
                buf_b[pl.ds(PAD + t * rt, rt), :] = dy_ref[rows, :] * gb_ref[rows, :]
            w0, w1, w2 = cw_ref[0:1, :], cw_ref[1:2, :], cw_ref[2:3, :]
            d0 = jnp.zeros((1, LANE), F32)
            d1 = jnp.zeros((1, LANE), F32)
            d2 = jnp.zeros((1, LANE), F32)
            for t in range(nrt):
                rows = pl.ds(t * rt, rt)
                r0 = PAD + t * rt
                z2, z1, z0 = buf_a[pl.ds(r0 - 2, rt), :], buf_a[pl.ds(r0 - 1, rt), :], buf_a[pl.ds(r0, rt), :]
                y = w0 * z2 + w1 * z1 + w2 * z0
                dgb_ref[rows, :] = (dy_ref[rows, :] * y).astype(BF16)
                e0 = buf_b[pl.ds(r0, rt), :]
                d0 = d0 + jnp.sum(e0 * z2, axis=0, keepdims=True)
                d1 = d1 + jnp.sum(e0 * z1, axis=0, keepdims=True)
                d2 = d2 + jnp.sum(e0 * z0, axis=0, keepdims=True)
                dz = w2 * e0 + w1 * buf_b[pl.ds(r0 + 1, rt), :] + w0 * buf_b[pl.ds(r0 + 2, rt), :]
                dgc_ref[rows, :] = (dz * hin_ref[rows, :]).astype(BF16)
                dhin_ref[rows, :] = (dz * gc_ref[rows, :]).astype(BF16)
            dcw_ref[0:1, :] = d0
            dcw_ref[1:2, :] = d1
            dcw_ref[2:3, :] = d2

        @pl.when(g >= 2)
        def _pool():
            jj = g - 2
            lo = _lo_mask()
            _pool_window_sums(u_ref, buf_a, buf_b, jj, s, rt)
            wb = wbd_ref[...]
            ps = ps_ref[...]
            dps = jnp.zeros((1, LANE), F32)
            dwb = jnp.zeros((LANE, LANE), F32)
            for t in range(nrt):
                rows = pl.ds(t * rt, rt)
                r0 = PAD + t * rt
                cnt = _pool_counts(jj, lo, t * rt, rt)
                wsum = jnp.where(lo, buf_b[pl.ds(r0, rt), :], buf_a[pl.ds(r0, rt), :])
                mb = (wsum / cnt - u_ref[rows, :]).astype(BF16)
                dy = dy_ref[rows, :]
                dps = dps + jnp.sum(dy * _nn(mb, wb), axis=0, keepdims=True)
                dmp = (dy * ps).astype(BF16)
                dwb = dwb + _tn(mb, dmp)
                dm = _nt(dmp, wb)
                buf_d[rows, :] = dm
                buf_c[pl.ds(r0, rt), :] = dm / cnt
            dps_ref[...] = dps
            dwbd_ref[...] = dwb

            def stage(src, dst, sh):
                for t in range(nrt):
                    r0 = PAD + t * rt
                    dst[pl.ds(r0, rt), :] = src[pl.ds(r0, rt), :] + src[pl.ds(r0 + sh, rt), :]

            def finish(first, second):
                for t in range(nrt):
                    rows = pl.ds(t * rt, rt)
                    r0 = PAD + t * rt
                    fw = jnp.where(lo, first[pl.ds(r0, rt), :], second[pl.ds(r0, rt), :])
                    du_ref[rows, :] = (fw - buf_d[rows, :]).astype(BF16)

            stage(buf_c, buf_a, 1)
            stage(buf_a, buf_b, 2)

            @pl.when(jj == 0)
            def _():
                finish(buf_a, buf_b)

            @pl.when(jj == 1)
            def _():
                stage(buf_b, buf_c, 4)
                stage(buf_c, buf_a, 8)
                finish(buf_c, buf_a)

    cblk = pl.BlockSpec((s, LANE), lambda g: (0, jnp.minimum(g, 1)))
    pblk = pl.BlockSpec((s, LANE), lambda g: (0, jnp.maximum(g - 2, 0)))
    padded = pltpu.VMEM((s + 2 * PAD, LANE), F32)
    return pl.pallas_call(
        body, name="conv_pool_bwd", grid=(4,),
        in_specs=_cp_in_specs(s, l) + [pl.BlockSpec((s, LANE), lambda g: (0, AW // LANE + g))],
        out_specs=[cblk, cblk, cblk, pblk,
                   pl.BlockSpec((3, LANE), lambda g: (0, jnp.minimum(g, 1))),
                   pl.BlockSpec((None, LANE, LANE), lambda g: (jnp.maximum(g - 2, 0), 0, 0)),
                   pl.BlockSpec((1, LANE), lambda g: (0, jnp.maximum(g - 2, 0)))],
        out_shape=[SDS((s, CW), BF16), SDS((s, CW), BF16), SDS((s, CW), BF16), SDS((s, PWD), BF16),
                   SDS((3, CW), F32), SDS((2, LANE, LANE), F32), SDS((1, PWD), F32)],
        scratch_shapes=[padded, padded, padded, pltpu.VMEM((s, LANE), F32)],
        compiler_params=_cp(),
    )(p, p, p, p, conv_w, wbd, pscale, dmix)


def _in_proj_bwd(parts, x, dxm, g1, win_t, l, xchg=None):
    s = x.shape[0]
    t = min(256, s)
    widths = [a.shape[1] for a in parts]
    offs = [int(o) for o in np.cumsum([0] + widths[:-1])]
    n = len(parts)

    def body(*refs):
        part_refs = refs[:n]
        x_ref, dxm_ref, g_ref, w_ref, dx_ref, dp_ref, dg_ref = refs[n:]

        @pl.when(pl.program_id(0) == 0)
        def _():
            dg_ref[...] = jnp.zeros_like(dg_ref)
        for r, o, w in zip(part_refs, offs, widths):
            dp_ref[:, o:o + w] = r[...]
        dh = _nn(dp_ref[...], w_ref[...])
        dx_n, dgr = _rms_bwd(dh, x_ref[...], g_ref[...])
        dg_ref[...] += jnp.sum(dgr, axis=0, keepdims=True)
        dx_ref[...] = dxm_ref[...] + dx_n

    row = lambda c: pl.BlockSpec((t, c), lambda i: (i, 0))
    return _hosted_call(
        body, xchg, name="in_proj_bwd", grid=(s // t,),
        in_specs=[row(w) for w in widths] + [row(D), row(D), _layer((1, D), l), _const((DIN, D))],
        out_specs=[row(D), row(DIN), pl.BlockSpec((1, D), lambda i: (0, 0))],
        out_shape=[SDS((s, D), F32), SDS((s, DIN), BF16), SDS((1, D), F32)], args=[*parts, x, dxm, g1, win_t])


def _wgrad(a, b, tag, xchg=None):
    s, m = a.shape
    mb = 512

    def body(a_ref, b_ref, o_ref):
        o_ref[...] = _tn(a_ref[...], b_ref[...]).astype(BF16)

    (out,), got = _hosted_call(
        body, xchg, name=f"wgrad_{tag}", grid=(m // mb,),
        in_specs=[pl.BlockSpec((s, mb), lambda mi: (0, mi)), _const((s, D))],
        out_specs=[pl.BlockSpec((mb, D), lambda mi: (mi, 0))],
        out_shape=[SDS((m, D), BF16)], args=[a, b])
    return out, got


def _bias_tables(gvec, xchg=None):
    def body(g_ref, o_ref):
        qc = lax.broadcasted_iota(jnp.int32, (TQ, WIN), 0) // CHUNK
        kc = lax.broadcasted_iota(jnp.int32, (TQ, WIN), 1) // CHUNK
        for var in range(NVAR):
            vec = jnp.broadcast_to(g_ref[:, var * TQ:var * TQ + NTOE], (TQ, NTOE))
            toe = pltpu.roll(vec, NTOE - TQ + 1, 1, stride=1, stride_axis=0)[:, :WIN]
            rel = (BAND - var * TQ) // CHUNK + qc - kc
            o_ref[var] = jnp.where((rel >= 0) & (rel <= N_PREV), toe, NEG)

    (out,), got = _hosted_call(
        body, xchg, name="bias_tables", grid=(L, NH),
        in_specs=[pl.BlockSpec((None, None, 1, NG), lambda l, h: (l, h, 0, 0))],
        out_specs=[pl.BlockSpec((None, NVAR, None, TQ, WIN), lambda l, h: (l, 0, h, 0, 0))],
        out_shape=[SDS((L, NVAR, NH, TQ, WIN), F32)], args=[gvec], relay_at=0.95)
    return out, got


def _bias_tables_grad(dbias, l):
    nb = NTOE // LANE
    wb = WIN // LANE

    def body(d_ref, o_ref):
        ii = lax.broadcasted_iota(jnp.int32, (LANE, LANE), 0)
        jj = lax.broadcasted_iota(jnp.int32, (LANE, LANE), 1)
        flip = jnp.where(ii + jj == LANE - 1, 1.0, 0.0).astype(BF16)
        o_ref[...] = jnp.zeros_like(o_ref)
        for var in range(NVAR):
            blocks = []
            for b in range(nb):
                src = nb - 1 - b
                if src >= wb:
                    blocks.append(jnp.zeros((TQ, LANE), F32))
                    continue
                xv = d_ref[var, :, src * LANE:(src + 1) * LANE]
                hi = xv.astype(BF16)
                lo = (xv - hi.astype(F32)).astype(BF16)
                blocks.append(_nn(hi, flip) + _nn(lo, flip))
            rev = jnp.concatenate(blocks, axis=1)
            skew = pltpu.roll(rev, NTOE - TQ + 1, 1, stride=1, stride_axis=0)
            off = NG - NTOE - var * TQ
            o_ref[:, off:off + NTOE] += jnp.sum(skew, axis=0, keepdims=True)

    return pl.pallas_call(
        body, name=f"bias_tables_grad_l{l}", grid=(NH,),
        in_specs=[pl.BlockSpec((NVAR, None, TQ, WIN), lambda h: (0, h, 0, 0))],
        out_specs=pl.BlockSpec((None, 1, NG), lambda h: (h, 0, 0)),
        out_shape=SDS((NH, 1, NG), F32),
        compiler_params=_cp(),
    )(dbias)


_SIBLING = (0, 0, 1)
_CHIPS = [(1, 0, 0), (0, 1, 0), (1, 1, 0)]
_MASKS = [_SIBLING] + _CHIPS + [(1, 0, 1), (0, 1, 1), (1, 1, 1)]


def _position():
    return lax.axis_index("x"), lax.axis_index("y"), lax.axis_index("c")


def _peer(pos, mask):
    return tuple(1 - a if f else a for a, f in zip(pos, mask))


def _index(pos):
    return 4 * pos[0] + 2 * pos[1] + pos[2]


def _exchange_phases(items, src, dst, sems):
    send_sems, recv_sems, local_sems = sems
    me = _position()
    sib = _peer(me, _SIBLING)

    def remote(s_ref, d_ref, pi, n, to):
        return pltpu.make_async_remote_copy(
            src_ref=s_ref, dst_ref=d_ref, send_sem=send_sems.at[pi, n], recv_sem=recv_sems.at[pi, n],
            device_id=to, device_id_type=MESH_ID)

    def parts(n):
        it = items[n]
        r = src[n].shape[1] if it[0] == "gather" else src[n].shape[0] // NDEV
        block = lambda ref, pos: ref.at[pl.ds(_index(pos) * r, r), :]
        if it[0] == "gather":
            own = src[n].at[it[2]]
            local = pltpu.make_async_copy(own, block(dst[n], me), local_sems.at[n])
            sends = [remote(own, block(dst[n], me), pi, n, _peer(me, m)) for pi, m in enumerate([_SIBLING] + _CHIPS)]
            hops = [(remote(block(dst[n], _peer(me, m)), block(dst[n], _peer(me, m)), 1 + j, n, _peer(me, m)),
                     remote(block(dst[n], _peer(me, m)), block(dst[n], _peer(me, m)), 4 + j, n, sib))
                    for j, m in enumerate(_CHIPS)]
            lands = [remote(own, block(dst[n], sib), 0, n, sib)]
            lands += [remote(own, block(dst[n], _peer(sib, m)), 4 + j, n, sib) for j, m in enumerate(_CHIPS)]
        else:
            local = pltpu.make_async_copy(block(src[n], me), dst[n].at[_index(me)], local_sems.at[n])
            sends = [remote(block(src[n], _peer(me, m)), dst[n].at[_index(me)], pi, n, _peer(me, m))
                     for pi, m in enumerate(_MASKS)]
            hops = []
            lands = [remote(block(src[n], me), dst[n].at[_index(_peer(me, m))], pi, n, _peer(me, m))
                     for pi, m in enumerate(_MASKS)]
        return local, sends, hops, lands

    def start():
        for n in range(len(items)):
            local, sends, _, _ = parts(n)
            local.start()
            for cp in sends:
                cp.start()

    def relay():
        for n in range(len(items)):
            for arrived, onward in parts(n)[2]:
                arrived.wait_recv()
                onward.start()

    def finish():
        for n in range(len(items)):
            local, sends, hops, lands = parts(n)
            for cp in lands:
                cp.wait_recv()
            for cp in sends + [onward for _, onward in hops]:
                cp.wait_send()
            local.wait()

    return start, relay, finish


def _hosted_call(body, xchg, *, name, grid, in_specs, out_specs, out_shape, args, scratch_shapes=(), relay_at=0.8):
    if not xchg:
        outs = pl.pallas_call(
            body, name=name, grid=grid, in_specs=list(in_specs), out_specs=list(out_specs),
            out_shape=list(out_shape), scratch_shapes=list(scratch_shapes), compiler_params=_cp())(*args)
        return outs, []
    items = [it for it, _ in xchg]
    n_in, n_out, n_scr, nit = len(args), len(out_shape), len(scratch_shapes), len(items)
    hbm = pl.BlockSpec(memory_space=pl.ANY)
    steps = int(np.prod(grid))
    relay_step = min(int(relay_at * steps), steps - 1)

    def dst_shape(it, a):
        if it[0] == "gather":
            return SDS((NDEV * a.shape[1], a.shape[2]), a.dtype)
        return SDS((NDEV, a.shape[0] // NDEV, a.shape[1]), a.dtype)

    def wrapped(*refs):
        ins = refs[:n_in]
        src = refs[n_in:n_in + nit]
        outs = refs[n_in + nit:n_in + nit + n_out]
        dst = refs[n_in + nit + n_out:n_in + 2 * nit + n_out]
        scratch = refs[n_in + 2 * nit + n_out:n_in + 2 * nit + n_out + n_scr]
        start, relay, finish = _exchange_phases(items, src, dst, refs[n_in + 2 * nit + n_out + n_scr:])
        step = 0
        for d, g in enumerate(grid):
            step = step * g + pl.program_id(d)
        pl.when(step == 0)(start)
        body(*ins, *outs, *scratch)
        pl.when(step == relay_step)(relay)
        pl.when(step == steps - 1)(finish)

    npeer = len(_MASKS)
    res = pl.pallas_call(
        wrapped, name=name, grid=grid,
        in_specs=list(in_specs) + [hbm] * nit,
        out_specs=list(out_specs) + [hbm] * nit,
        out_shape=list(out_shape) + [dst_shape(it, a) for it, a in xchg],
        scratch_shapes=list(scratch_shapes) + [
            pltpu.SemaphoreType.DMA((npeer, nit)), pltpu.SemaphoreType.DMA((npeer, nit)), pltpu.SemaphoreType.DMA((nit,))],
        compiler_params=_cp(),
    )(*args, *[a for _, a in xchg])
    return list(res[:n_out]), list(res[n_out:])


def _sum_slots(slots, xchg=None):
    _, r, _ = slots[0].shape
    n = len(slots) // L
    rt = 64

    def body(*refs):
        for k in range(n):
            for l in range(L):
                src = refs[k * L + l]
                acc = src[0].astype(F32)
                for d in range(1, NDEV):
                    acc = acc + src[d].astype(F32)
                refs[n * L + k][l] = acc

    return _hosted_call(
        body, xchg, name=f"sum_slots_r{r}" + ("_x" if xchg else ""), grid=(r // rt,),
        in_specs=[pl.BlockSpec((NDEV, rt, D), lambda i: (0, i, 0))] * (n * L),
        out_specs=[pl.BlockSpec((L, rt, D), lambda i: (0, i, 0))] * n,
        out_shape=[SDS((L, r, D), F32)] * n, args=list(slots))


def _sum_small(slots):
    rows = slots.shape[0] // NDEV

    def body(in_ref, o_ref):
        acc = in_ref[pl.ds(0, rows), :]
        for d in range(1, NDEV):
            acc = acc + in_ref[pl.ds(d * rows, rows), :]
        o_ref[...] = acc

    vm = pl.BlockSpec(memory_space=pltpu.VMEM)
    return pl.pallas_call(
        body, name="sum_small", in_specs=[vm], out_specs=vm, out_shape=SDS((rows, LANE), F32),
        compiler_params=_cp())(slots)


def _adamw_update(w_ref, g_ref, m_ref, v_ref, d_ref, nm_ref, nv_ref):
    gv = g_ref[...]
    mn = B1 * m_ref[...] + (1.0 - B1) * gv
    vn = B2 * v_ref[...] + (1.0 - B2) * jnp.square(gv)
    nm_ref[...] = mn
    nv_ref[...] = vn
    m_hat = mn / (1.0 - B1 ** STEP)
    v_hat = vn / (1.0 - B2 ** STEP)
    d_ref[...] = -LR * (m_hat / (jnp.sqrt(v_hat) + AEPS) + WD * w_ref[...])


def _adamw_small(ws, gs, ms, vs):
    n = len(ws)

    def body(*refs):
        for i in range(n):
            _adamw_update(*[refs[j * n + i] for j in range(7)])

    vm = pl.BlockSpec(memory_space=pltpu.VMEM)
    res = pl.pallas_call(
        body, name="adamw_small", in_specs=[vm] * (4 * n), out_specs=[vm] * (3 * n),
        out_shape=[SDS(w.shape, F32) for _ in range(3) for w in ws],
        compiler_params=_cp(),
    )(*ws, *gs, *ms, *vs)
    return res[:n], res[n:2 * n], res[2 * n:]


def _adamw(w, g, m, v, xchg=None):
    rows, cols = w.shape
    t = rows
    for cand in (512, 256, 128, 64, 32, 16, 8):
        if rows % cand == 0:
            t = cand
            break

    def body(*refs):
        _adamw_update(*refs)

    blk = pl.BlockSpec((t, cols), lambda i: (i, 0))
    return _hosted_call(
        body, xchg, name=f"adamw_{rows}x{cols}", grid=(rows // t,),
        in_specs=[blk] * 4, out_specs=[blk] * 3, out_shape=[SDS((rows, cols), F32)] * 3, args=[w, g, m, v])


_DIST0 = BAND + TQ - 1
_N_FAR = _DIST0 - REL_CLIP + 1
_N_NEAR = NG - _N_FAR - (2 * REL_CLIP - 1)


def _bias_vector(rel_bias):
    far = jnp.broadcast_to(rel_bias[..., -1:], (L, NH, _N_FAR))
    near = jnp.broadcast_to(rel_bias[..., :1], (L, NH, _N_NEAR))
    return jnp.concatenate([far, lax.rev(rel_bias[..., 1:-1], (2,)), near], axis=2)[:, :, None, :]


def _bias_vector_grad(dgr):
    first = jnp.sum(dgr[..., :_N_NEAR], axis=-1, keepdims=True)
    last = jnp.sum(dgr[..., NG - _N_FAR:], axis=-1, keepdims=True)
    return jnp.concatenate([first, dgr[..., _N_NEAR:NG - _N_FAR], last], axis=-1)


def _pool_blockdiag(pool_w):
    eye = jnp.eye(2, dtype=F32)
    pw = pool_w.reshape(L, 2, 2, HD, HD)
    return jnp.einsum("ljaik,ab->ljaibk", pw, eye).reshape(L, 2, LANE, LANE)


def _pool_blockdiag_grad(dwbd):
    d = dwbd.reshape(L, 2, 2, HD, 2, HD)
    return jnp.stack([d[:, :, 0, :, 0, :], d[:, :, 1, :, 1, :]], axis=2).reshape(L, 4, HD, HD)


def _pack(arrays, rows):
    flat = jnp.concatenate([a.reshape(-1).astype(F32) for a in arrays])
    return jnp.pad(flat, (0, rows * LANE - flat.shape[0])).reshape(rows, LANE)


def _unpack(packed, shapes):
    flat = packed.reshape(-1)
    out, o = [], 0
    for shp in shapes:
        n = int(np.prod(shp))
        out.append(flat[o:o + n].reshape(shp))
        o += n
    return out


def _rows_for(shapes):
    n = sum(int(np.prod(s)) for s in shapes)
    return -(-n // (8 * LANE)) * 8


def _grads(x, target, small_w, shards):
    g1, qg, kg, rb, cw_shard, pw, ps, g2 = small_w
    g1 = g1.reshape(L, 1, D)
    g2 = g2.reshape(L, 1, D)
    qg2 = jnp.tile(qg, (1, 2)).reshape(L, 1, LANE)
    kg2 = jnp.tile(kg, (1, 2)).reshape(L, 1, LANE)
    ps3 = ps.reshape(L, 1, PWD)
    wbd = _pool_blockdiag(pw).astype(BF16)

    def gather(*kl):
        return [(("gather", k, l), shards[k]) for k, l in kl if l < L]

    full = {}

    def arrived(got, *kl):
        full.update(zip([x for x in kl if x[1] < L], got))

    bias, got = _bias_tables(_bias_vector(rb), gather((0, 0), (1, 0)) + [(("gather", "conv_w", 0), cw_shard)])
    arrived(got[:2], (0, 0), (1, 0))
    cshard = CW // NDEV
    cw_all = got[2].reshape(NDEV, -1)
    cw = jnp.concatenate([cw_all[d, :L * 3 * cshard].reshape(L, 3, cshard) for d in range(NDEV)], axis=2)
    saved = []
    h = x
    for l in range(L):
        (p, h_b, qkv), _ = _in_proj(h, g1, full[0, l], qg2, kg2, l)
        kl = ((2, 0), (3, 0)) if l == 0 else ((1, l), (3, l))
        (mix, lse, o32), got = _attn_fwd(qkv, bias, l, gather(*kl), relay_at=0.95 if l == 0 else 0.8)
        arrived(got, *kl)
        mix = _conv_pool_fwd(p, mix, cw, wbd, ps3, l)
        kl = ((0, l + 1), (2, l + 1))
        (xm, a, *out), got = _mlp_fwd(h, mix, full[1, l], g2, full[2, l], full[3, l], l, gather(*kl),
                                      target if l == L - 1 else None)
        arrived(got, *kl)
        saved.append((h, h_b, p, qkv, mix, lse, o32, xm, a))
        h = out[0]
    dx, sq = out

    grads = {}
    slots = {}

    def scatter(*kl):
        return [(("scatter", k), grads[k, l]) for k, l in kl if l < L]

    def left(got, *kl):
        slots.update(zip([x for x in kl if x[1] < L], got))

    per_layer = [None] * L
    for l in reversed(range(L)):
        x_in, h_b, p, qkv, mix, lse, o32, xm, a = saved[l]
        (dxm, dmix, f_b, da_b, h2_b, dxo_b, dxm_b, dg2), got = _mlp_bwd(
            dx, a, xm, g2, full[2, l], full[3, l], full[1, l], l, scatter((3, l + 1)))
        left(got, (3, l + 1))
        grads[1, l], _ = _wgrad(mix, dxm_b, f"w_out_l{l}")
        kl = ((1, 0),) if l == 0 else ()
        grads[2, l], got = _wgrad(da_b, h2_b, f"w_mlp1_l{l}", scatter(*kl))
        left(got, *kl)
        kl = ((2, 0),) if l == 0 else ()
        grads[3, l], got = _wgrad(f_b, dxo_b, f"w_mlp2_l{l}", scatter(*kl))
        left(got, *kl)
        kl = ((3, 0), (0, 1)) if l == 0 else ((2, l), (1, l))
        (dq, dk, dv, dbias, dqg, dkg), got = _attn_bwd(qkv, p, lse, o32, dmix, qg2, kg2, bias, l, scatter(*kl))
        left(got, *kl)
        dgb, dgc, dhin, du, dcw, dwbd, dps = _conv_pool_bwd(p, dmix, cw, wbd, ps3, l)
        (dx, dp_b, dg1), got = _in_proj_bwd(
            [dq, dk, dv, dgb, dgc, dhin, du], x_in, dxm, g1, full[0, l], l, scatter((0, l + 1)) if l else None)
        left(got, (0, l + 1))
        grads[0, l], _ = _wgrad(dp_b, h_b, f"w_in_l{l}")
        per_layer[l] = (dg1, dg2, dqg, dkg, _bias_tables_grad(dbias, l), dcw, dwbd, dps)
    (g_w1_t, g_w2), got = _sum_slots([slots[k, l] for k in (2, 3) for l in range(L)], scatter((0, 0)))
    left(got, (0, 0))
    sums = [_sum_slots([slots[k, l] for l in range(L)])[0][0] for k in (0, 1)] + [g_w1_t, g_w2]

    st = [jnp.stack([per_layer[l][k] for l in range(L)]) for k in range(8)]
    small = dict(
        g1=st[0].reshape(L, D), g2=st[1].reshape(L, D),
        qg=st[2].reshape(L, NH, HD).sum(1), kg=st[3].reshape(L, NH, HD).sum(1),
        rb=_bias_vector_grad(st[4].reshape(L, NH, NG)), cw=st[5], pw=_pool_blockdiag_grad(st[6]),
        ps=st[7].reshape(L, PWD))
    return sq, dx, sums, small


def kernel(x, norm1_g, w_in, q_norm_g, k_norm_g, rel_bias, conv_w, pool_w, pool_scale, w_out, norm2_g, w_mlp1, w_mlp2, loss_target, m_norm1_g, m_w_in, m_q_norm_g, m_k_norm_g, m_rel_bias, m_conv_w, m_pool_w, m_pool_scale, m_w_out, m_norm2_g, m_w_mlp1, m_w_mlp2, v_norm1_g, v_w_in, v_q_norm_g, v_k_norm_g, v_rel_bias, v_conv_w, v_pool_w, v_pool_scale, v_w_out, v_norm2_g, v_w_mlp1, v_w_mlp2):
    me = _index(_position())
    cshard = CW // NDEV

    shards = [jnp.swapaxes(w_in, 1, 2).astype(BF16), w_out.astype(BF16),
              jnp.swapaxes(w_mlp1, 1, 2).astype(BF16), w_mlp2.astype(BF16)]
    small_w = (norm1_g, q_norm_g, k_norm_g, rel_bias, _pack([conv_w], 8)[None], pool_w, pool_scale, norm2_g)
    sq, grad_x, (g_win_t, g_wout, g_w1_t, g_w2), small = _grads(x[0], loss_target[0], small_w, shards)
    g_w_in = jnp.swapaxes(g_win_t, 1, 2)
    g_w_mlp1 = jnp.swapaxes(g_w1_t, 1, 2)

    names = ("g1", "qg", "kg", "rb", "cw", "pw", "ps", "g2")
    gshapes = [(L, D), (L, HD), (L, HD), (L, NH, 2 * REL_CLIP + 1), (L, 3, CW), (L, 4, HD, HD), (L, PWD), (L, D)]
    garrs = [small[n] for n in names]
    rows = _rows_for(gshapes + [(1,)])
    packed = _pack(garrs + [sq[0, :1]], rows)[None]

    def big(w, g, m, v, xchg=None):
        shp = w.shape
        r = lambda a: a.reshape(-1, shp[-1])
        outs, got = _adamw(r(w), r(g), r(m), r(v), xchg)
        return [o.reshape(shp) for o in outs], got

    up_1, got = big(w_mlp1, g_w_mlp1, m_w_mlp1, v_w_mlp1, [(("gather", "small", 0), packed)])
    total = _sum_small(got[0])
    g_g1, g_qg, g_kg, g_rb, g_cw_full, g_pw, g_ps, g_g2, sq_sum = _unpack(total, gshapes + [(1,)])
    loss = (0.5 / D) * sq_sum[0]
    g_cw = lax.dynamic_slice_in_dim(g_cw_full, me * cshard, cshard, axis=2)
    up_in = big(w_in, g_w_in, m_w_in, v_w_in)[0]
    up_out = big(w_out, g_wout, m_w_out, v_w_out)[0]
    up_2 = big(w_mlp2, g_w2, m_w_mlp2, v_w_mlp2)[0]

    sw = [norm1_g, q_norm_g, k_norm_g, rel_bias, conv_w, pool_w, pool_scale, norm2_g]
    sg = [g_g1, g_qg, g_kg, g_rb, g_cw, g_pw, g_ps, g_g2]
    sm = [m_norm1_g, m_q_norm_g, m_k_norm_g, m_rel_bias, m_conv_w, m_pool_w, m_pool_scale, m_norm2_g]
    sv = [v_norm1_g, v_q_norm_g, v_k_norm_g, v_rel_bias, v_conv_w, v_pool_w, v_pool_scale, v_norm2_g]
    s_delta, s_m, s_v = _adamw_small(sw, sg, sm, sv)

    def order(small_list, in_, out_, m1, m2):
        g1_, qg_, kg_, rb_, cw_, pw_, ps_, g2_ = small_list
        return [g1_, in_, qg_, kg_, rb_, cw_, pw_, ps_, out_, g2_, m1, m2]

    grads = order(sg, g_w_in, g_wout, g_w_mlp1, g_w2)
    deltas = order(s_delta, up_in[0], up_out[0], up_1[0], up_2[0])
    new_m = order(s_m, up_in[1], up_out[1], up_1[1], up_2[1])
    new_v = order(s_v, up_in[2], up_out[2], up_1[2], up_2[2])
    return (loss, grad_x[None], *grads, *deltas, *new_m, *new_v)
```
